```python
import math
import jax, jax.numpy as jnp
from jax import lax
import numpy as np

D_MODEL = 1024
BATCH = 8
SEQ = 4096
DEPTH = 2

HEAD_DIM_A = 64
HEADS_PER_GROUP_A = 4
DIL_GROUPS = ((128, 1), (512, 4), (2048, 16))
N_HEADS_A = HEADS_PER_GROUP_A * len(DIL_GROUPS)
DA = N_HEADS_A * HEAD_DIM_A
DA_OUT = HEADS_PER_GROUP_A * HEAD_DIM_A
POOL_WINDOWS = (2, 4, 8, 16)
POOL_GROUP_DIM = 128
DB = POOL_GROUP_DIM * len(POOL_WINDOWS)
N_HEADS_C = 8
QK_NOPE = 64
QK_ROPE = 32
V_DIM = 64
Q_LORA = 384
KV_LORA = 256
DC = N_HEADS_C * V_DIM
ROPE_THETA = 10000.0
N_EXPERTS = 64
TOP_K = 8
N_GROUPS = 8
TOPK_GROUPS = 4
EXPERT_FF = 256
SHARED_FF = 256
ROUTED_SCALE = 2.5
MOE_BLOCK = 128
Q_BLOCK = 128
EPS = 1e-6
NEG = -1e30
N_IN = 3 * DA + DB + Q_LORA + KV_LORA + QK_ROPE + 3 * D_MODEL

kernel_name = 'hybrid_dilated_pool_mla_moe_adaln'


def rmsnorm(x, g):
    xf = x.astype(jnp.float32)
    y = xf * lax.rsqrt(jnp.mean(xf * xf, axis=-1, keepdims=True) + EPS)
    return (y * g.astype(jnp.float32)).astype(x.dtype)


def adaln(c, w, b):
    m = jax.nn.silu(c) @ w + b
    shift, scale, gate = jnp.split(m, 3, axis=-1)
    return shift[:, None, :], scale[:, None, :], gate[:, None, :]


def rope(x, positions):
    r = x.shape[-1]
    freqs = ROPE_THETA ** (-jnp.arange(0, r, 2, dtype=jnp.float32) / r)
    ang = positions.astype(jnp.float32)[..., None] * freqs
    if x.ndim == 4:
        ang = ang[:, :, None, :]
    cos, sin = jnp.cos(ang), jnp.sin(ang)
    xf = x.astype(jnp.float32)
    x1, x2 = xf[..., : r // 2], xf[..., r // 2:]
    return jnp.concatenate([x1 * cos - x2 * sin, x2 * cos + x1 * sin], axis=-1).astype(x.dtype)


def dilated_attention(q, k, v, dilation, n_back):
    B, S, H, Dh = q.shape
    L = S // dilation
    qb_len = math.gcd(L, Q_BLOCK)
    nb = L // qb_len
    kw_len = qb_len + n_back

    def by_residue(t):
        return t.reshape(B, L, dilation, H, Dh).transpose(0, 2, 1, 3, 4)

    qr, kr, vr = by_residue(q), by_residue(k), by_residue(v)
    pad = ((0, 0), (0, 0), (n_back, 0), (0, 0), (0, 0))
    win = jnp.arange(nb)[:, None] * qb_len + jnp.arange(kw_len)[None, :]
    kw = jnp.pad(kr, pad)[:, :, win]
    vw = jnp.pad(vr, pad)[:, :, win]
    qb = qr.reshape(B, dilation, nb, qb_len, H, Dh)
    s = jnp.einsum('brnqhe,brnkhe->brnhqk', qb, kw).astype(jnp.float32) * (Dh ** -0.5)
    dist = jnp.arange(qb_len)[:, None] + n_back - jnp.arange(kw_len)[None, :]
    kpos = (jnp.arange(nb) * qb_len)[:, None, None] + jnp.arange(kw_len)[None, None, :] - n_back
    valid = ((dist >= 0) & (dist <= n_back))[None] & (kpos >= 0)
    s = jnp.where(valid[None, None, :, None], s, NEG)
    m = jnp.max(s, axis=-1, keepdims=True)
    p = jnp.exp(s - m)
    den = jnp.sum(p, axis=-1, keepdims=True)
    o = jnp.einsum('brnhqk,brnkhe->brnqhe', (p / den).astype(v.dtype), vw)
    lse = (m + jnp.log(den))[..., 0]
    o = o.reshape(B, dilation, L, H, Dh).transpose(0, 2, 1, 3, 4).reshape(B, S, H, Dh)
    lse = lse.transpose(0, 1, 2, 4, 3).reshape(B, dilation, L, H).transpose(0, 2, 1, 3).reshape(B, S, H)
    return o, lse


def pool_mixer(u, pool_w, pool_scale):
    B, S, _ = u.shape
    ug = u.reshape(B, S, len(POOL_WINDOWS), POOL_GROUP_DIM).astype(jnp.float32)
    cs = jnp.cumsum(ug, axis=1)
    t = jnp.arange(S)
    outs = []
    for gi, w in enumerate(POOL_WINDOWS):
        csg = cs[:, :, gi]
        lag = jnp.pad(csg, ((0, 0), (w, 0), (0, 0)))[:, :S]
        cnt = jnp.minimum(t + 1, w).astype(jnp.float32)[None, :, None]
        outs.append((csg - lag) / cnt - ug[:, :, gi])
    pooled = jnp.stack(outs, axis=2).astype(u.dtype)
    mixed = jnp.einsum('bsgc,gcd->bsgd', pooled, pool_w).reshape(B, S, DB)
    return mixed * pool_scale


def mla(cq, ckv, kr, positions, cq_g, ckv_g, w_uq, w_ukv):
    B, S, _ = cq.shape
    q = (rmsnorm(cq, cq_g) @ w_uq).reshape(B, S, N_HEADS_C, QK_NOPE + QK_ROPE)
    qn, qr = q[..., :QK_NOPE], rope(q[..., QK_NOPE:], positions)
    kv = (rmsnorm(ckv, ckv_g) @ w_ukv).reshape(B, S, N_HEADS_C, QK_NOPE + V_DIM)
    kn, v = kv[..., :QK_NOPE], kv[..., QK_NOPE:]
    krr = rope(kr, positions)
    nb = S // Q_BLOCK
    scale = (QK_NOPE + QK_ROPE) ** -0.5
    kpos = jnp.arange(S)

    def blocks(t):
        return t.reshape(B, nb, Q_BLOCK, *t.shape[2:]).swapaxes(0, 1)

    def attend(args):
        i, qn_b, qr_b = args
        s = (jnp.einsum('bqhe,bkhe->bhqk', qn_b, kn) + jnp.einsum('bqhr,bkr->bhqk', qr_b, krr)).astype(jnp.float32) * scale
        qpos = i * Q_BLOCK + jnp.arange(Q_BLOCK)
        s = jnp.where(kpos[None, :] <= qpos[:, None], s, NEG)
        p = jax.nn.softmax(s, axis=-1).astype(v.dtype)
        return jnp.einsum('bhqk,bkhe->bqhe', p, v)

    o = lax.map(attend, (jnp.arange(nb), blocks(qn), blocks(qr)))
    return o.swapaxes(0, 1).reshape(B, S, DC)


def token_mixer(h, positions, w_in, pool_w, pool_scale, cq_g, ckv_g, w_uq, w_ukv, w_oa, w_ob, w_oc, w_out):
    B, S, _ = h.shape
    sizes = (DA, DA, DA, DB, Q_LORA, KV_LORA, QK_ROPE, D_MODEL, D_MODEL, D_MODEL)
    points = [int(p) for p in np.cumsum(sizes)[:-1]]
    qa, ka, va, ub, cq, ckv, kr, ga, gb, gc = jnp.split(h @ w_in, points, axis=-1)
    qa, ka, va = (t.reshape(B, S, N_HEADS_A, HEAD_DIM_A) for t in (qa, ka, va))
    outs, lses = [], []
    for g, (window, dil) in enumerate(DIL_GROUPS):
        sl = slice(g * HEADS_PER_GROUP_A, (g + 1) * HEADS_PER_GROUP_A)
        o, l = dilated_attention(qa[:, :, sl], ka[:, :, sl], va[:, :, sl], dil, window // dil)
        outs.append(o)
        lses.append(l)
    wts = jax.nn.softmax(jnp.stack(lses), axis=0)
    ya = jnp.einsum('gbsh,gbshe->bshe', wts, jnp.stack(outs).astype(jnp.float32)).reshape(B, S, DA_OUT).astype(h.dtype)
    yb = pool_mixer(ub, pool_w, pool_scale)
    yc = mla(cq, ckv, kr, positions, cq_g, ckv_g, w_uq, w_ukv)
    mix = jax.nn.sigmoid(ga) * (ya @ w_oa) + jax.nn.sigmoid(gb) * (yb @ w_ob) + jax.nn.sigmoid(gc) * (yc @ w_oc)
    return mix @ w_out


def routed_experts(h2, router_w, router_bias, w1, w3, w2):
    N, D = h2.shape
    scores = jax.nn.sigmoid((h2 @ router_w).astype(jnp.float32))
    sel = scores + router_bias.astype(jnp.float32)
    grp = sel.reshape(N, N_GROUPS, N_EXPERTS // N_GROUPS)
    grp_score = lax.top_k(grp, 2)[0].sum(-1)
    top_g = lax.top_k(grp_score, TOPK_GROUPS)[1]
    gmask = jax.nn.one_hot(top_g, N_GROUPS, dtype=jnp.float32).sum(1) > 0
    emask = jnp.repeat(gmask, N_EXPERTS // N_GROUPS, axis=1)
    _, idx = lax.top_k(jnp.where(emask, sel, NEG), TOP_K)
    wk = jnp.take_along_axis(scores, idx, axis=1)
    wk = wk / jnp.sum(wk, axis=-1, keepdims=True) * ROUTED_SCALE
    M = N * TOP_K
    flat_e = idx.reshape(-1).astype(jnp.int32)
    flat_tok = jnp.arange(M, dtype=jnp.int32) // TOP_K
    flat_w = wk.reshape(-1)
    order = jnp.argsort(flat_e)
    se = flat_e[order]
    counts = jnp.bincount(flat_e, length=N_EXPERTS)
    padded = (counts + MOE_BLOCK - 1) // MOE_BLOCK * MOE_BLOCK
    start = jnp.cumsum(counts) - counts
    pend = jnp.cumsum(padded)
    pstart = pend - padded
    dest = jnp.arange(M, dtype=jnp.int32) - start[se] + pstart[se]
    P = (M + MOE_BLOCK - 1) // MOE_BLOCK * MOE_BLOCK + N_EXPERTS * MOE_BLOCK
    nblk = P // MOE_BLOCK
    slot_tok = jnp.full((P,), N, jnp.int32).at[dest].set(flat_tok[order])
    slot_w = jnp.zeros((P,), h2.dtype).at[dest].set(flat_w[order].astype(h2.dtype))
    blk_e = jnp.minimum(jnp.searchsorted(pend, jnp.arange(nblk) * MOE_BLOCK, side='right'), N_EXPERTS - 1)
    x_pad = jnp.concatenate([h2, jnp.zeros((1, D), h2.dtype)], axis=0)

    def run(args):
        e, tok, w = args
        xb = x_pad[tok]
        y = (jax.nn.silu(xb @ w1[e]) * (xb @ w3[e])) @ w2[e]
        return y * w[:, None]

    ys = lax.map(run, (blk_e, slot_tok.reshape(nblk, MOE_BLOCK), slot_w.reshape(nblk, MOE_BLOCK)))
    out = jnp.zeros((N + 1, D), h2.dtype).at[slot_tok].add(ys.reshape(P, D))
    return out[:N]


def channel_mixer(h, router_w, router_bias, w1, w3, w2, sh_w1, sh_w3, sh_w2):
    B, S, D = h.shape
    h2 = h.reshape(B * S, D)
    shared = (jax.nn.silu(h2 @ sh_w1) * (h2 @ sh_w3)) @ sh_w2
    return (shared + routed_experts(h2, router_w, router_bias, w1, w3, w2)).reshape(B, S, D)


def setup_inputs(seed: int = 0) -> dict:
    key = jax.random.key(seed)
    ks = jax.random.split(key, 32)
    f32 = jnp.float32
    L = DEPTH

    def nrm(k, shape, fan_in, scale=1.0):
        return jax.random.normal(k, shape, f32) * (scale * fan_in ** -0.5)

    def gain(k, shape):
        return 1.0 + 0.05 * jax.random.normal(k, shape, f32)

    x = jax.random.normal(ks[0], (BATCH, SEQ, D_MODEL), f32)
    c = jax.random.normal(ks[1], (BATCH, D_MODEL), f32)
    positions = (jnp.arange(SEQ, dtype=jnp.int32)[None, :] + jax.random.randint(ks[2], (BATCH, 1), 0, 4096)).astype(jnp.int32)
    return {
        'x': x,
        'c': c,
        'positions': positions,
        'ada_mix_w': nrm(ks[3], (L, D_MODEL, 3 * D_MODEL), D_MODEL, 0.5),
        'ada_mix_b': 0.02 * jax.random.normal(ks[4], (L, 3 * D_MODEL), f32),
        'norm_mix_g': gain(ks[5], (L, D_MODEL)),
        'w_in': nrm(ks[6], (L, D_MODEL, N_IN), D_MODEL),
        'pool_w': nrm(ks[7], (L, len(POOL_WINDOWS), POOL_GROUP_DIM, POOL_GROUP_DIM), POOL_GROUP_DIM),
        'pool_scale': gain(ks[8], (L, DB)),
        'cq_norm_g': gain(ks[9], (L, Q_LORA)),
        'ckv_norm_g': gain(ks[10], (L, KV_LORA)),
        'w_uq': nrm(ks[11], (L, Q_LORA, N_HEADS_C * (QK_NOPE + QK_ROPE)), Q_LORA),
        'w_ukv': nrm(ks[12], (L, KV_LORA, N_HEADS_C * (QK_NOPE + V_DIM)), KV_LORA),
        'w_oa': nrm(ks[13], (L, DA_OUT, D_MODEL), DA_OUT),
        'w_ob': nrm(ks[14], (L, DB, D_MODEL), DB),
        'w_oc': nrm(ks[15], (L, DC, D_MODEL), DC),
        'w_out': nrm(ks[16], (L, D_MODEL, D_MODEL), D_MODEL),
        'ada_ffn_w': nrm(ks[17], (L, D_MODEL, 3 * D_MODEL), D_MODEL, 0.5),
        'ada_ffn_b': 0.02 * jax.random.normal(ks[18], (L, 3 * D_MODEL), f32),
        'norm_ffn_g': gain(ks[19], (L, D_MODEL)),
        'router_w': nrm(ks[20], (L, D_MODEL, N_EXPERTS), D_MODEL),
        'router_bias': 0.01 * jax.random.normal(ks[21], (L, N_EXPERTS), f32),
        'exp_w1': nrm(ks[22], (L, N_EXPERTS, D_MODEL, EXPERT_FF), D_MODEL),
        'exp_w3': nrm(ks[23], (L, N_EXPERTS, D_MODEL, EXPERT_FF), D_MODEL),
        'exp_w2': nrm(ks[24], (L, N_EXPERTS, EXPERT_FF, D_MODEL), EXPERT_FF),
        'sh_w1': nrm(ks[25], (L, D_MODEL, SHARED_FF), D_MODEL),
        'sh_w3': nrm(ks[26], (L, D_MODEL, SHARED_FF), D_MODEL),
        'sh_w2': nrm(ks[27], (L, SHARED_FF, D_MODEL), SHARED_FF),
        'final_g': gain(ks[28], (D_MODEL,)),
    }


def reference(x, c, positions, ada_mix_w, ada_mix_b, norm_mix_g, w_in, pool_w, pool_scale, cq_norm_g, ckv_norm_g, w_uq, w_ukv, w_oa, w_ob, w_oc, w_out, ada_ffn_w, ada_ffn_b, norm_ffn_g, router_w, router_bias, exp_w1, exp_w3, exp_w2, sh_w1, sh_w3, sh_w2, final_g):
    for l in range(DEPTH):
        shift, scale, gate = adaln(c, ada_mix_w[l], ada_mix_b[l])
        h = rmsnorm(x, norm_mix_g[l]) * (1.0 + scale) + shift
        x = x + gate * token_mixer(h, positions, w_in[l], pool_w[l], pool_scale[l], cq_norm_g[l], ckv_norm_g[l], w_uq[l], w_ukv[l], w_oa[l], w_ob[l], w_oc[l], w_out[l])
        shift, scale, gate = adaln(c, ada_ffn_w[l], ada_ffn_b[l])
        h = rmsnorm(x, norm_ffn_g[l]) * (1.0 + scale) + shift
        x = x + gate * channel_mixer(h, router_w[l], router_bias[l], exp_w1[l], exp_w3[l], exp_w2[l], sh_w1[l], sh_w3[l], sh_w2[l])
    return rmsnorm(x, final_g)
```

```python
import functools
import math

import jax
import jax.numpy as jnp
import numpy as np
from jax import lax
from jax.experimental import pallas as pl
from jax.experimental.pallas import tpu as pltpu

F32 = jnp.float32
BF16 = jnp.bfloat16
HIGHEST = lax.Precision.HIGHEST

D_MODEL = 1024
HEAD_DIM_A = 64
HEADS_PER_GROUP_A = 4
DIL_GROUPS = ((128, 1), (512, 4), (2048, 16))
GROUP_W = HEADS_PER_GROUP_A * HEAD_DIM_A
DA = GROUP_W * len(DIL_GROUPS)
POOL_WINDOWS = (2, 4, 8, 16)
POOL_GROUP_DIM = 128
DB = POOL_GROUP_DIM * len(POOL_WINDOWS)
POOL_HALO = 16
N_HEADS_C = 8
QK_NOPE = 64
QK_ROPE = 32
V_DIM = 64
Q_LORA = 384
KV_LORA = 256
DC = N_HEADS_C * V_DIM
HEAD_PAD_C = 128
ROPE_THETA = 10000.0
N_EXPERTS = 64
TOP_K = 8
N_GROUPS = 8
TOPK_GROUPS = 4
GROUP_SIZE = N_EXPERTS // N_GROUPS
EXPERT_FF = 256
ROUTED_SCALE = 2.5
EPS = 1e-6
NEG = -1e30
Q_BLOCK = 128

COL_GATES = 0
COL_QKV = 3 * D_MODEL
COL_LAT = COL_QKV + 3 * DA
LAT_W = 768
COL_UB = COL_LAT + LAT_W
NP_COLS = COL_UB + DB

VMEM_LIMIT = 56 * 1024 * 1024


def _cp(sem, vmem=None):
    return pltpu.CompilerParams(dimension_semantics=sem, vmem_limit_bytes=vmem)


def _silu(v):
    return v * jax.nn.sigmoid(v)


def _nt_dot(a, b):
    return lax.dot_general(a, b, (((1,), (1,)), ((), ())), preferred_element_type=F32)


def _adaln_kernel(c_ref, w_ref, b_ref, o_ref):
    s = _silu(c_ref[...])
    o_ref[0] = jnp.dot(s, w_ref[0], preferred_element_type=F32, precision=HIGHEST) + b_ref[0]


def adaln_rows(c, w, b):
    L, D, D3 = w.shape
    B = c.shape[0]
    tn = 1024
    return pl.pallas_call(
        _adaln_kernel,
        grid=(L, D3 // tn),
        in_specs=[
            pl.BlockSpec((B, D), lambda l, j: (0, 0)),
            pl.BlockSpec((1, D, tn), lambda l, j: (l, 0, j)),
            pl.BlockSpec((1, 1, tn), lambda l, j: (l, 0, j)),
        ],
        out_specs=pl.BlockSpec((1, B, tn), lambda l, j: (l, 0, j)),
        out_shape=jax.ShapeDtypeStruct((L, B, D3), F32),
        compiler_params=_cp(("parallel", "parallel")),
        name="adaln_rows",
    )(c, w, b.reshape(L, 1, D3))


def _inproj_kernel(x_ref, g_ref, mod_ref, w_ref, o_ref, *, chunk):
    D = x_ref.shape[1]
    x = x_ref[...]
    y = x * lax.rsqrt(jnp.mean(x * x, axis=-1, keepdims=True) + EPS) * g_ref[...]
    mod = mod_ref[0]
    h = (y * (1.0 + mod[:, D:2 * D]) + mod[:, :D]).astype(BF16)
    for j in range(o_ref.shape[1] // chunk):
        sl = slice(j * chunk, (j + 1) * chunk)
        o_ref[:, sl] = jnp.dot(h, w_ref[:, sl], preferred_element_type=F32).astype(o_ref.dtype)


def in_projection(x2, g, mod, w, seq):
    N, D = x2.shape
    NP = w.shape[1]
    tm = 512
    tpb = seq // tm
    return pl.pallas_call(
        functools.partial(_inproj_kernel, chunk=512),
        grid=(N // tm,),
        in_specs=[
            pl.BlockSpec((tm, D), lambda i: (i, 0)),
            pl.BlockSpec((1, D), lambda i: (0, 0)),
            pl.BlockSpec((1, 1, 3 * D), lambda i: (i // tpb, 0, 0)),
            pl.BlockSpec((D, NP), lambda i: (0, 0), pipeline_mode=pl.Buffered(1)),
        ],
        out_specs=pl.BlockSpec((tm, NP), lambda i: (i, 0)),
        out_shape=jax.ShapeDtypeStruct((N, NP), BF16),
        compiler_params=_cp(("parallel",), VMEM_LIMIT),
        name="in_projection",
    )(x2, g.reshape(1, D), mod, w)


def _dilated_kernel(q_ref, kc_ref, kp_ref, vc_ref, vp_ref, o_ref, lse_ref):
    i = pl.program_id(2)
    q, kc, kp, vc, vp = q_ref[0], kc_ref[0], kp_ref[0], vc_ref[0], vp_ref[0]
    T = q.shape[0]
    row = lax.broadcasted_iota(jnp.int32, (T, T), 0)
    col = lax.broadcasted_iota(jnp.int32, (T, T), 1)
    valid_c = col <= row
    valid_p = (col >= row) & (i > 0)
    outs, lses = [], []
    for h in range(HEADS_PER_GROUP_A):
        sl = slice(h * HEAD_DIM_A, (h + 1) * HEAD_DIM_A)
        qh = q[:, sl]
        sc = jnp.where(valid_c, _nt_dot(qh, kc[:, sl]) * (HEAD_DIM_A ** -0.5), NEG)
        sp = jnp.where(valid_p, _nt_dot(qh, kp[:, sl]) * (HEAD_DIM_A ** -0.5), NEG)
        m = jnp.maximum(jnp.max(sc, axis=-1, keepdims=True), jnp.max(sp, axis=-1, keepdims=True))
        pc = jnp.exp(sc - m)
        pp = jnp.exp(sp - m)
        den = jnp.sum(pc, axis=-1, keepdims=True) + jnp.sum(pp, axis=-1, keepdims=True)
        o = jnp.dot(pc.astype(BF16), vc[:, sl], preferred_element_type=F32)
        o = o + jnp.dot(pp.astype(BF16), vp[:, sl], preferred_element_type=F32)
        outs.append(o / den)
        lses.append(jnp.broadcast_to(m + jnp.log(den), (T, HEAD_DIM_A)))
    o_ref[0] = jnp.concatenate(outs, axis=1).astype(o_ref.dtype)
    lse_ref[0] = jnp.concatenate(lses, axis=1)


def dilated_attention(proj, batch, seq, group, dilation):
    NP = proj.shape[1]
    L = seq // dilation
    nb = L // Q_BLOCK
    cb = NP // GROUP_W
    pv = proj.reshape(batch, L, dilation * NP)
    qc = COL_QKV // GROUP_W + group
    kcol = qc + DA // GROUP_W
    vcol = kcol + DA // GROUP_W
    blk = (1, Q_BLOCK, GROUP_W)

    def cur(c):
        return pl.BlockSpec(blk, lambda b, r, i: (b, i, r * cb + c))

    def prev(c):
        return pl.BlockSpec(blk, lambda b, r, i: (b, jnp.maximum(i - 1, 0), r * cb + c))

    out_spec = pl.BlockSpec(blk, lambda b, r, i: (b, i, r))
    o, lse = pl.pallas_call(
        _dilated_kernel,
        grid=(batch, dilation, nb),
        in_specs=[cur(qc), cur(kcol), prev(kcol), cur(vcol), prev(vcol)],
        out_specs=[out_spec, out_spec],
        out_shape=[
            jax.ShapeDtypeStruct((batch, L, dilation * GROUP_W), BF16),
            jax.ShapeDtypeStruct((batch, L, dilation * GROUP_W), F32),
        ],
        compiler_params=_cp(("parallel", "parallel", "parallel")),
        name=f"dilated_attention_d{dilation}",
    )(pv, pv, pv, pv, pv)
    return o.reshape(batch * seq, GROUP_W), lse.reshape(batch * seq, GROUP_W)


def _mla_prep_kernel(lat_ref, pos_ref, gq_ref, gkv_ref, wq_ref, wk_ref, freq_ref, exp_ref, one_ref,
                     q_ref, k_ref, v_ref):
    HP = N_HEADS_C * HEAD_PAD_C
    lat = lat_ref[...].astype(F32)
    cq = lat[:, :Q_LORA]
    ckr = lat[:, Q_LORA:]
    zq = (cq * lax.rsqrt(jnp.mean(cq * cq, axis=-1, keepdims=True) + EPS) * gq_ref[...]).astype(BF16)
    lane = lax.broadcasted_iota(jnp.int32, ckr.shape, 1)
    is_kv = lane < KV_LORA
    ms = jnp.sum(jnp.where(is_kv, ckr * ckr, 0.0), axis=-1, keepdims=True) * (1.0 / KV_LORA)
    zkv = (ckr * jnp.where(is_kv, lax.rsqrt(ms + EPS) * gkv_ref[...], 1.0)).astype(BF16)
    qq = jnp.dot(zq, wq_ref[...], preferred_element_type=F32)
    kk = jnp.dot(zkv, wk_ref[...], preferred_element_type=F32)
    ang = pos_ref[...].astype(F32) * freq_ref[...]
    cos = jnp.dot(jnp.cos(ang), exp_ref[...], preferred_element_type=F32, precision=HIGHEST) + one_ref[...]
    sin = jnp.dot(jnp.sin(ang), exp_ref[...], preferred_element_type=F32, precision=HIGHEST)
    q_ref[...] = (qq[:, :HP] * cos + qq[:, HP:] * sin).astype(q_ref.dtype)
    k_ref[...] = (kk[:, :HP] * cos + kk[:, HP:2 * HP] * sin).astype(k_ref.dtype)
    v_ref[...] = kk[:, 2 * HP:].astype(v_ref.dtype)


def _mla_weights(cq_g, ckv_g, w_uq, w_ukv):
    H, HPAD, half = N_HEADS_C, HEAD_PAD_C, QK_ROPE // 2
    scale = (QK_NOPE + QK_ROPE) ** -0.5
    wq = w_uq.reshape(Q_LORA, H, QK_NOPE + QK_ROPE) * scale
    q_lin = jnp.pad(wq, ((0, 0), (0, 0), (0, HPAD - QK_NOPE - QK_ROPE)))
    r1, r2 = wq[..., QK_NOPE:QK_NOPE + half], wq[..., QK_NOPE + half:]
    q_sw = jnp.concatenate([jnp.zeros((Q_LORA, H, QK_NOPE), F32), -r2, r1,
                            jnp.zeros((Q_LORA, H, HPAD - QK_NOPE - QK_ROPE), F32)], axis=-1)
    wq_big = jnp.concatenate([q_lin.reshape(Q_LORA, H * HPAD), q_sw.reshape(Q_LORA, H * HPAD)], axis=1)

    rows = LAT_W - Q_LORA
    wkv = w_ukv.reshape(KV_LORA, H, QK_NOPE + V_DIM)
    eye = jnp.eye(QK_ROPE, dtype=F32)
    k_lin = jnp.zeros((rows, H, HPAD), F32)
    k_lin = k_lin.at[:KV_LORA, :, :QK_NOPE].set(wkv[..., :QK_NOPE])
    k_lin = k_lin.at[KV_LORA:KV_LORA + QK_ROPE, :, QK_NOPE:QK_NOPE + QK_ROPE].set(
        jnp.broadcast_to(eye[:, None, :], (QK_ROPE, H, QK_ROPE)))
    swap = jnp.zeros((QK_ROPE, QK_ROPE), F32).at[half:, :half].set(-jnp.eye(half)).at[:half, half:].set(jnp.eye(half))
    k_sw = jnp.zeros((rows, H, HPAD), F32)
    k_sw = k_sw.at[KV_LORA:KV_LORA + QK_ROPE, :, QK_NOPE:QK_NOPE + QK_ROPE].set(
        jnp.broadcast_to(swap[:, None, :], (QK_ROPE, H, QK_ROPE)))
    v_w = jnp.zeros((rows, H, V_DIM), F32).at[:KV_LORA].set(wkv[..., QK_NOPE:])
    wk_big = jnp.concatenate([k_lin.reshape(rows, H * HPAD), k_sw.reshape(rows, H * HPAD),
                              v_w.reshape(rows, H * V_DIM)], axis=1)

    gkv = jnp.concatenate([ckv_g, jnp.ones((rows - KV_LORA,), F32)]).reshape(1, rows)
    return cq_g.reshape(1, Q_LORA), gkv, wq_big.astype(BF16), wk_big.astype(BF16)


def _rope_tables():
    half = QK_ROPE // 2
    freqs = (ROPE_THETA ** (-jnp.arange(0, QK_ROPE, 2, dtype=F32) / QK_ROPE)).reshape(1, half)
    expand = np.zeros((half, N_HEADS_C, HEAD_PAD_C), np.float32)
    ones = np.zeros((1, N_HEADS_C, HEAD_PAD_C), np.float32)
    for j in range(half):
        expand[j, :, QK_NOPE + j] = 1.0
        expand[j, :, QK_NOPE + half + j] = 1.0
    ones[0, :, :QK_NOPE] = 1.0
    return freqs, jnp.asarray(expand.reshape(half, -1)), jnp.asarray(ones.reshape(1, -1))


def mla_prep(proj, pos_col, gq, gkv, wq_big, wk_big):
    N = proj.shape[0]
    HP = N_HEADS_C * HEAD_PAD_C
    tm = 512
    freqs, expand, ones = _rope_tables()
    const = lambda shape: pl.BlockSpec(shape, lambda i: (0, 0))
    return pl.pallas_call(
        _mla_prep_kernel,
        grid=(N // tm,),
        in_specs=[
            pl.BlockSpec((tm, LAT_W), lambda i: (i, COL_LAT // LAT_W)),
            pl.BlockSpec((tm, 1), lambda i: (i, 0)),
            const(gq.shape), const(gkv.shape), const(wq_big.shape), const(wk_big.shape),
            const(freqs.shape), const(expand.shape), const(ones.shape),
        ],
        out_specs=[
            pl.BlockSpec((tm, HP), lambda i: (i, 0)),
            pl.BlockSpec((tm, HP), lambda i: (i, 0)),
            pl.BlockSpec((tm, DC), lambda i: (i, 0)),
        ],
        out_shape=[
            jax.ShapeDtypeStruct((N, HP), BF16),
            jax.ShapeDtypeStruct((N, HP), BF16),
            jax.ShapeDtypeStruct((N, DC), BF16),
        ],
        compiler_params=_cp(("parallel",), VMEM_LIMIT),
        name="mla_prep",
    )(proj, pos_col, gq, gkv, wq_big, wk_big, freqs, expand, ones)


HEADS_PER_STEP_C = 2


def _mla_flash_kernel(qi_ref, ki_ref, q_ref, k_ref, v_ref, o_ref, m_sc, l_sc, acc_sc):
    t = pl.program_id(2)
    qi, ki = qi_ref[t], ki_ref[t]

    @pl.when(ki == 0)
    def _():
        m_sc[...] = jnp.full(m_sc.shape, NEG, F32)
        l_sc[...] = jnp.zeros(l_sc.shape, F32)
        acc_sc[...] = jnp.zeros(acc_sc.shape, F32)

    def step(masked):
        T = q_ref.shape[1]
        if masked:
            row = lax.broadcasted_iota(jnp.int32, (T, T), 0)
            col = lax.broadcasted_iota(jnp.int32, (T, T), 1)
            keep = col <= row
        for h in range(HEADS_PER_STEP_C):
            q = q_ref[0, :, h * HEAD_PAD_C:(h + 1) * HEAD_PAD_C]
            k = k_ref[0, :, h * HEAD_PAD_C:(h + 1) * HEAD_PAD_C]
            v = v_ref[0, :, h * V_DIM:(h + 1) * V_DIM]
            s = _nt_dot(q, k)
            if masked:
                s = jnp.where(keep, s, NEG)
            m_prev = m_sc[h]
            m_new = jnp.maximum(m_prev, jnp.max(s, axis=-1, keepdims=True))
            alpha = jnp.exp(m_prev - m_new)
            p = jnp.exp(s - m_new)
            l_sc[h] = alpha * l_sc[h] + jnp.sum(p, axis=-1, keepdims=True)
            acc_sc[h] = alpha * acc_sc[h] + jnp.dot(p.astype(BF16), v, preferred_element_type=F32)
            m_sc[h] = m_new

    @pl.when(ki < qi)
    def _():
        step(False)

    @pl.when(ki == qi)
    def _():
        step(True)
        o_ref[0] = jnp.concatenate(
            [acc_sc[h] / l_sc[h] for h in range(HEADS_PER_STEP_C)], axis=1).astype(o_ref.dtype)


def mla_attention(q_all, k_all, v_all, batch, seq):
    T = 512
    nq = seq // T
    pairs = [(a, b) for a in range(nq) for b in range(a + 1)]
    qi_tab = jnp.asarray([p[0] for p in pairs], jnp.int32)
    ki_tab = jnp.asarray([p[1] for p in pairs], jnp.int32)
    hp = N_HEADS_C // HEADS_PER_STEP_C
    qw = HEADS_PER_STEP_C * HEAD_PAD_C
    vw = HEADS_PER_STEP_C * V_DIM
    q3 = q_all.reshape(batch, seq, -1)
    k3 = k_all.reshape(batch, seq, -1)
    v3 = v_all.reshape(batch, seq, -1)
    grid_spec = pltpu.PrefetchScalarGridSpec(
        num_scalar_prefetch=2,
        grid=(batch, hp, len(pairs)),
        in_specs=[
            pl.BlockSpec((1, T, qw), lambda b, h, t, qi, ki: (b, qi[t], h)),
            pl.BlockSpec((1, T, qw), lambda b, h, t, qi, ki: (b, ki[t], h)),
            pl.BlockSpec((1, T, vw), lambda b, h, t, qi, ki: (b, ki[t], h)),
        ],
        out_specs=pl.BlockSpec((1, T, vw), lambda b, h, t, qi, ki: (b, qi[t], h)),
        scratch_shapes=[
            pltpu.VMEM((HEADS_PER_STEP_C, T, 1), F32),
            pltpu.VMEM((HEADS_PER_STEP_C, T, 1), F32),
            pltpu.VMEM((HEADS_PER_STEP_C, T, V_DIM), F32),
        ],
    )
    o = pl.pallas_call(
        _mla_flash_kernel,
        grid_spec=grid_spec,
        out_shape=jax.ShapeDtypeStruct((batch, seq, DC), BF16),
        compiler_params=_cp(("parallel", "parallel", "arbitrary")),
        name="mla_attention",
    )(qi_tab, ki_tab, q3, k3, v3)
    return o.reshape(batch * seq, DC)


def _mixout_kernel(x_ref, gates_ref, ub_ref, ubh_ref, o1_ref, o2_ref, o3_ref, l1_ref, l2_ref, l3_ref, yc_ref,
                   mod1_ref, mod2_ref, g2_ref, poolw_ref, pscale_ref, woa_ref, wob_ref, woc_ref, wout_ref,
                   rwt_ref, sw1_ref, sw3_ref, sw2_ref,
                   xmid_ref, h2_ref, logit_ref, *, tiles_per_batch):
    D = x_ref.shape[1]
    tm = x_ref.shape[0]
    tile = pl.program_id(0) % tiles_per_batch

    l1, l2, l3 = l1_ref[...], l2_ref[...], l3_ref[...]
    mx = jnp.maximum(jnp.maximum(l1, l2), l3)
    e1, e2, e3 = jnp.exp(l1 - mx), jnp.exp(l2 - mx), jnp.exp(l3 - mx)
    ya = (e1 * o1_ref[...].astype(F32) + e2 * o2_ref[...].astype(F32) + e3 * o3_ref[...].astype(F32)) / (e1 + e2 + e3)
    a_out = jnp.dot(ya.astype(BF16), woa_ref[...], preferred_element_type=F32)

    u = ub_ref[...].astype(F32)
    halo = jnp.where(tile > 0, ubh_ref[...].astype(F32), 0.0)
    ext = jnp.concatenate([halo, u], axis=0)
    t_seq = tile * tm + lax.broadcasted_iota(jnp.int32, (tm, 1), 0)
    pooled = []
    for gi, w in enumerate(POOL_WINDOWS):
        sl = slice(gi * POOL_GROUP_DIM, (gi + 1) * POOL_GROUP_DIM)
        acc = ext[:, sl]
        k = 1
        while k < w:
            acc = acc + pltpu.roll(acc, k, axis=0)
            k *= 2
        cnt = jnp.minimum(t_seq + 1, w).astype(F32)
        pg = acc[POOL_HALO:] / cnt - u[:, sl]
        pooled.append(jnp.dot(pg.astype(BF16), poolw_ref[gi], preferred_element_type=F32))
    yb = jnp.concatenate(pooled, axis=1) * pscale_ref[...]
    b_out = jnp.dot(yb.astype(BF16), wob_ref[...], preferred_element_type=F32)
    c_out = jnp.dot(yc_ref[...], woc_ref[...], preferred_element_type=F32)

    g = gates_ref[...].astype(F32)
    mix = (jax.nn.sigmoid(g[:, :D]) * a_out + jax.nn.sigmoid(g[:, D:2 * D]) * b_out
           + jax.nn.sigmoid(g[:, 2 * D:]) * c_out)
    tok = jnp.dot(mix.astype(BF16), wout_ref[...], preferred_element_type=F32)
    xn = x_ref[...] + mod1_ref[0][:, 2 * D:] * tok

    mod2 = mod2_ref[0]
    y = xn * lax.rsqrt(jnp.mean(xn * xn, axis=-1, keepdims=True) + EPS) * g2_ref[...]
    h2 = y * (1.0 + mod2[:, D:2 * D]) + mod2[:, :D]
    h2_ref[...] = h2
    h2b = h2.astype(BF16)
    logit_ref[...] = _nt_dot(rwt_ref[...], h2b)
    hid = _silu(jnp.dot(h2b, sw1_ref[...], preferred_element_type=F32)) * jnp.dot(
        h2b, sw3_ref[...], preferred_element_type=F32)
    shared = jnp.dot(hid.astype(BF16), sw2_ref[...], preferred_element_type=F32)
    xmid_ref[...] = xn + mod2[:, 2 * D:] * shared


def mix_out(x2, proj, dil, yc, mod1, mod2, g2, pool_w, pool_scale, w_oa, w_ob, w_oc, w_out, rwt, sw1, sw3, sw2, seq):
    N, D = x2.shape
    tm = 256
    tpb = seq // tm
    (o1, l1), (o2, l2), (o3, l3) = dil
    row = lambda w, c=0: pl.BlockSpec((tm, w), lambda i: (i, c))
    const2 = lambda a: pl.BlockSpec(a.shape, lambda i: (0,) * a.ndim)
    modspec = pl.BlockSpec((1, 1, 3 * D), lambda i: (i // tpb, 0, 0))
    halo_spec = pl.BlockSpec(
        (POOL_HALO, DB), lambda i: (jnp.maximum(i * (tm // POOL_HALO) - 1, 0), COL_UB // DB))
    weights = [g2.reshape(1, D), pool_w, pool_scale.reshape(1, DB), w_oa, w_ob, w_oc, w_out, rwt, sw1, sw3, sw2]
    return pl.pallas_call(
        functools.partial(_mixout_kernel, tiles_per_batch=tpb),
        grid=(N // tm,),
        in_specs=[
            row(D), row(3 * D, COL_GATES // (3 * D)), row(DB, COL_UB // DB), halo_spec,
            row(GROUP_W), row(GROUP_W), row(GROUP_W), row(GROUP_W), row(GROUP_W), row(GROUP_W), row(DC),
            modspec, modspec,
        ] + [const2(a) for a in weights],
        out_specs=[row(D), row(D), pl.BlockSpec((N_EXPERTS, tm), lambda i: (0, i))],
        out_shape=[
            jax.ShapeDtypeStruct((N, D), F32),
            jax.ShapeDtypeStruct((N, D), F32),
            jax.ShapeDtypeStruct((N_EXPERTS, N), F32),
        ],
        compiler_params=_cp(("parallel",), VMEM_LIMIT),
        name="mix_out",
    )(x2, proj, proj, proj, o1, o2, o3, l1, l2, l3, yc, mod1, mod2, *weights)


def _route_kernel(lg_ref, bias_ref, w_ref, sel_ref):
    G, GS = N_GROUPS, GROUP_SIZE
    scores = jax.nn.sigmoid(lg_ref[...])
    sel = scores + bias_ref[...]
    tn = sel.shape[1]
    eio = lax.broadcasted_iota(jnp.int32, (GS, tn), 0)
    ninf = -jnp.inf

    gs = []
    for g in range(G):
        v = sel[g * GS:(g + 1) * GS]
        m1 = jnp.max(v, axis=0, keepdims=True)
        i1 = jnp.min(jnp.where(v == m1, eio, GS), axis=0, keepdims=True)
        m2 = jnp.max(jnp.where(eio == i1, ninf, v), axis=0, keepdims=True)
        gs.append(m1 + m2)
    gsm = jnp.concatenate(gs, axis=0)
    gio = lax.broadcasted_iota(jnp.int32, (G, tn), 0)
    rank = jnp.zeros((G, tn), jnp.int32)
    for g2 in range(G):
        beats = (gs[g2] > gsm) | ((gs[g2] == gsm) & (g2 < gio))
        rank = rank + beats.astype(jnp.int32)
    gsel = rank < TOPK_GROUPS

    vs = [jnp.where(gsel[g:g + 1], sel[g * GS:(g + 1) * GS], NEG) for g in range(G)]
    eid = [eio + g * GS for g in range(G)]
    chosen = [jnp.zeros((GS, tn), jnp.bool_) for _ in range(G)]
    for _ in range(TOP_K):
        m = functools.reduce(jnp.maximum, [jnp.max(v, axis=0, keepdims=True) for v in vs])
        idx = functools.reduce(jnp.minimum, [
            jnp.min(jnp.where(v == m, e, N_EXPERTS), axis=0, keepdims=True) for v, e in zip(vs, eid)])
        for g in range(G):
            hit = eid[g] == idx
            chosen[g] = chosen[g] | hit
            vs[g] = jnp.where(hit, ninf, vs[g])
    mask = jnp.concatenate(chosen, axis=0).astype(F32)
    wk = scores * mask
    w_ref[...] = wk / jnp.sum(wk, axis=0, keepdims=True) * ROUTED_SCALE
    sel_ref[...] = mask


def route(logits_t, bias):
    E, N = logits_t.shape
    tn = 1024
    spec = pl.BlockSpec((E, tn), lambda i: (0, i))
    return pl.pallas_call(
        _route_kernel,
        grid=(N // tn,),
        in_specs=[spec, pl.BlockSpec((E, 1), lambda i: (0, 0))],
        out_specs=[spec, spec],
        out_shape=[jax.ShapeDtypeStruct((E, N), F32), jax.ShapeDtypeStruct((E, N), F32)],
        compiler_params=_cp(("parallel",)),
        name="route",
    )(logits_t, bias.reshape(E, 1))


SLOT_BLOCK = 256


def dispatch_tables(w_t, sel_t):
    E, N = sel_t.shape
    M = N * TOP_K
    blk = SLOT_BLOCK
    P = M + E * blk
    nblk = P // blk
    counts = jnp.sum(sel_t, axis=1).astype(jnp.int32)
    pos = jnp.nonzero(sel_t.reshape(-1) > 0, size=M, fill_value=0)[0].astype(jnp.int32)
    t_sorted = pos % N
    k_of = (jnp.cumsum(sel_t, axis=0) - sel_t).astype(jnp.int32).reshape(-1)[pos]
    w_sorted = w_t.reshape(-1)[pos]
    padded = (counts + blk - 1) // blk * blk
    pend = jnp.cumsum(padded)
    pstart = pend - padded
    start = jnp.cumsum(counts) - counts
    blk_e = jnp.minimum(jnp.searchsorted(pend, jnp.arange(nblk, dtype=jnp.int32) * blk, side='right'),
                        E - 1).astype(jnp.int32)
    p = jnp.arange(P, dtype=jnp.int32)
    e_p = blk_e[p // blk]
    j = p - pstart[e_p]
    valid = (j < counts[e_p]) & (p < pend[-1])
    src = jnp.clip(start[e_p] + j, 0, M - 1)
    slot_tok = jnp.where(valid, t_sorted[src], 0).astype(jnp.int32)
    slot_dst = jnp.where(valid, t_sorted[src] * TOP_K + k_of[src], -1).astype(jnp.int32)
    slot_w = jnp.where(valid, w_sorted[src], 0.0).reshape(nblk, 1, blk)
    nused = (pend[-1] // blk).astype(jnp.int32).reshape(1)
    return blk_e, slot_tok, slot_dst, slot_w, nused


def _expert_kernel(blk_e_ref, nused_ref,
                   tok0_ref, tokn_ref, dst_ref, h2_hbm, sw_ref, w1_ref, w3_ref, w2_ref, out8_hbm,
                   xbuf, ybuf, gsem, ssem):
    blk = xbuf.shape[1]
    dump_base = out8_hbm.shape[0] - 2 * blk
    b = pl.program_id(0)
    nb = pl.num_programs(0)
    nused = nused_ref[0]
    slot = b % 2

    def gather_rows(idx_ref, sl):
        def body(j, carry):
            tok = idx_ref[0, 0, j]
            pltpu.make_async_copy(h2_hbm.at[pl.ds(tok, 1)], xbuf.at[sl, pl.ds(j, 1)], gsem.at[sl]).start()
            return carry

        lax.fori_loop(0, blk, body, 0, unroll=8)

    def scatter_rows(sl):
        def body(j, carry):
            dst = dst_ref[0, 0, j]
            dst = jnp.where(dst < 0, dump_base + sl * blk + j, dst)
            pltpu.make_async_copy(ybuf.at[sl, pl.ds(j, 1)], out8_hbm.at[pl.ds(dst, 1)], ssem.at[sl]).start()
            return carry

        lax.fori_loop(0, blk, body, 0, unroll=8)

    def wait_gather(sl):
        pltpu.make_async_copy(h2_hbm.at[pl.ds(0, blk)], xbuf.at[sl], gsem.at[sl]).wait()

    def wait_scatter(sl):
        pltpu.make_async_copy(ybuf.at[sl], out8_hbm.at[pl.ds(0, blk)], ssem.at[sl]).wait()

    @pl.when(b == 0)
    def _():
        ybuf[0] = jnp.zeros(ybuf.shape[1:], ybuf.dtype)
        for half in range(2):
            fill = pltpu.make_async_copy(ybuf.at[0], out8_hbm.at[pl.ds(dump_base + half * blk, blk)], ssem.at[0])
            fill.start()
            fill.wait()

    @pl.when((b == 0) & (nused > 0))
    def _():
        gather_rows(tok0_ref, 0)

    @pl.when(b < nused)
    def _():
        @pl.when(b + 1 < nused)
        def _():
            gather_rows(tokn_ref, 1 - slot)

        wait_gather(slot)
        x = xbuf[slot].astype(BF16)
        hid = _silu(jnp.dot(x, w1_ref[0].astype(BF16), preferred_element_type=F32)) * jnp.dot(
            x, w3_ref[0].astype(BF16), preferred_element_type=F32)
        y = jnp.dot(hid.astype(BF16), w2_ref[0].astype(BF16), preferred_element_type=F32)
        wrow = sw_ref[0]
        eye = lax.broadcasted_iota(jnp.int32, (blk, blk), 0) == lax.broadcasted_iota(jnp.int32, (blk, blk), 1)
        wcol = jnp.sum(jnp.where(eye, wrow, 0.0), axis=1, keepdims=True)

        @pl.when(b >= 2)
        def _():
            wait_scatter(slot)

        ybuf[slot] = y * wcol
        scatter_rows(slot)

    @pl.when(b == nb - 1)
    def _():
        @pl.when(nused >= 1)
        def _():
            wait_scatter((nused - 1) % 2)

        @pl.when(nused >= 2)
        def _():
            wait_scatter(nused % 2)


def routed_experts(h2, tables, w1, w3, w2):
    blk_e, slot_tok, slot_dst, slot_w, nused = tables
    N, D = h2.shape
    blk = SLOT_BLOCK
    nblk = blk_e.shape[0]
    FF = w1.shape[2]
    tok3 = slot_tok.reshape(nblk, 1, blk)
    dst3 = slot_dst.reshape(nblk, 1, blk)
    smem_blk = lambda imap: pl.BlockSpec((1, 1, blk), imap, memory_space=pltpu.SMEM)
    grid_spec = pltpu.PrefetchScalarGridSpec(
        num_scalar_prefetch=2,
        grid=(nblk,),
        in_specs=[
            smem_blk(lambda b, be, nu: (0, 0, 0)),
            smem_blk(lambda b, be, nu: (jnp.minimum(b + 1, nblk - 1), 0, 0)),
            smem_blk(lambda b, be, nu: (b, 0, 0)),
            pl.BlockSpec(memory_space=pl.ANY),
            pl.BlockSpec((1, 1, blk), lambda b, be, nu: (b, 0, 0)),
            pl.BlockSpec((1, D, FF), lambda b, be, nu: (be[b], 0, 0)),
            pl.BlockSpec((1, D, FF), lambda b, be, nu: (be[b], 0, 0)),
            pl.BlockSpec((1, FF, D), lambda b, be, nu: (be[b], 0, 0)),
        ],
        out_specs=pl.BlockSpec(memory_space=pl.ANY),
        scratch_shapes=[
            pltpu.VMEM((2, blk, D), F32),
            pltpu.VMEM((2, blk, D), F32),
            pltpu.SemaphoreType.DMA((2,)),
            pltpu.SemaphoreType.DMA((2,)),
        ],
    )
    return pl.pallas_call(
        _expert_kernel,
        grid_spec=grid_spec,
        out_shape=jax.ShapeDtypeStruct((N * TOP_K + 2 * blk, D), F32),
        compiler_params=_cp(("arbitrary",), VMEM_LIMIT),
        name="routed_experts",
    )(blk_e, nused, tok3, tok3, dst3, h2, slot_w, w1, w3, w2)


def _combine_kernel(xmid_ref, o8_ref, mod2_ref, fg_ref, out_ref, *, final):
    D = xmid_ref.shape[1]
    acc = o8_ref[:, 0:D]
    for k in range(1, TOP_K):
        acc = acc + o8_ref[:, k * D:(k + 1) * D]
    x = xmid_ref[...] + mod2_ref[0][:, 2 * D:] * acc
    if final:
        x = x * lax.rsqrt(jnp.mean(x * x, axis=-1, keepdims=True) + EPS) * fg_ref[...]
    out_ref[...] = x


def combine(xmid, out8, mod2, final_g, seq, final):
    N, D = xmid.shape
    tm = 256
    tpb = seq // tm
    o8 = out8.reshape(-1, TOP_K * D)
    return pl.pallas_call(
        functools.partial(_combine_kernel, final=final),
        grid=(N // tm,),
        in_specs=[
            pl.BlockSpec((tm, D), lambda i: (i, 0)),
            pl.BlockSpec((tm, TOP_K * D), lambda i: (i, 0)),
            pl.BlockSpec((1, 1, 3 * D), lambda i: (i // tpb, 0, 0)),
            pl.BlockSpec((1, D), lambda i: (0, 0)),
        ],
        out_specs=pl.BlockSpec((tm, D), lambda i: (i, 0)),
        out_shape=jax.ShapeDtypeStruct((N, D), F32),
        compiler_params=_cp(("parallel",), VMEM_LIMIT),
        name="combine",
    )(xmid, o8, mod2, final_g.reshape(1, D))


def _permute_w_in(w):
    qkv, ub = w[:, :3 * DA], w[:, 3 * DA:3 * DA + DB]
    lat_lo = 3 * DA + DB
    lat_hi = lat_lo + Q_LORA + KV_LORA + QK_ROPE
    lat, gates = w[:, lat_lo:lat_hi], w[:, lat_hi:]
    pad = jnp.zeros((w.shape[0], LAT_W - (lat_hi - lat_lo)), w.dtype)
    return jnp.concatenate([gates, qkv, lat, pad, ub], axis=1).astype(BF16)


def kernel(x, c, positions, ada_mix_w, ada_mix_b, norm_mix_g, w_in, pool_w, pool_scale, cq_norm_g, ckv_norm_g, w_uq, w_ukv, w_oa, w_ob, w_oc, w_out, ada_ffn_w, ada_ffn_b, norm_ffn_g, router_w, router_bias, exp_w1, exp_w3, exp_w2, sh_w1, sh_w3, sh_w2, final_g):
    B, S, D = x.shape
    depth = w_in.shape[0]
    N = B * S
    mod_mix = adaln_rows(c, ada_mix_w, ada_mix_b)
    mod_ffn = adaln_rows(c, ada_ffn_w, ada_ffn_b)
    pos_col = positions.reshape(N, 1)
    x2 = x.reshape(N, D)
    for l in range(depth):
        mod1 = mod_mix[l].reshape(B, 1, 3 * D)
        mod2 = mod_ffn[l].reshape(B, 1, 3 * D)
        proj = in_projection(x2, norm_mix_g[l], mod1, _permute_w_in(w_in[l]), S)
        dil = [dilated_attention(proj, B, S, g, d) for g, (_, d) in enumerate(DIL_GROUPS)]
        gq, gkv, wq_big, wk_big = _mla_weights(cq_norm_g[l], ckv_norm_g[l], w_uq[l], w_ukv[l])
        q_all, k_all, v_all = mla_prep(proj, pos_col, gq, gkv, wq_big, wk_big)
        yc = mla_attention(q_all, k_all, v_all, B, S)
        xmid, h2, logits_t = mix_out(
            x2, proj, dil, yc, mod1, mod2, norm_ffn_g[l], pool_w[l].astype(BF16), pool_scale[l],
            w_oa[l].astype(BF16), w_ob[l].astype(BF16), w_oc[l].astype(BF16), w_out[l].astype(BF16),
            router_w[l].T.astype(BF16), sh_w1[l].astype(BF16), sh_w3[l].astype(BF16), sh_w2[l].astype(BF16), S)
        w_t, sel_t = route(logits_t, router_bias[l])
        tables = dispatch_tables(w_t, sel_t)
        out8 = routed_experts(h2, tables, exp_w1[l], exp_w3[l], exp_w2[l])
        x2 = combine(xmid, out8, mod2, final_g, S, final=(l == depth - 1))
    return x2.reshape(B, S, D)
```

```python
import functools
import math

import jax
import jax.numpy as jnp
import numpy as np
from jax import lax
from jax.experimental import pallas as pl
from jax.experimental.pallas import tpu as pltpu

F32 = jnp.float32
BF16 = jnp.bfloat16
HIGHEST = lax.Precision.HIGHEST

D_MODEL = 1024
HEAD_DIM_A = 64
HEADS_PER_GROUP_A = 4
DIL_GROUPS = ((128, 1), (512, 4), (2048, 16))
GROUP_W = HEADS_PER_GROUP_A * HEAD_DIM_A
DA = GROUP_W * len(DIL_GROUPS)
POOL_WINDOWS = (2, 4, 8, 16)
POOL_GROUP_DIM = 128
DB = POOL_GROUP_DIM * len(POOL_WINDOWS)
POOL_HALO = 16
N_HEADS_C = 8
QK_NOPE = 64
QK_ROPE = 32
V_DIM = 64
Q_LORA = 384
KV_LORA = 256
DC = N_HEADS_C * V_DIM
HEAD_PAD_C = 128
ROPE_THETA = 10000.0
N_EXPERTS = 64
TOP_K = 8
N_GROUPS = 8
TOPK_GROUPS = 4
GROUP_SIZE = N_EXPERTS // N_GROUPS
EXPERT_FF = 256
ROUTED_SCALE = 2.5
EPS = 1e-6
NEG = -1e30
Q_BLOCK = 128

LAT_W = 768
GU_W = 3 * D_MODEL + DB
IN_OUT_WIDTHS = (GU_W, LAT_W) + (2 * GROUP_W, GROUP_W) * len(DIL_GROUPS)

VMEM_LIMIT = 56 * 1024 * 1024


def _cp(sem, vmem=None):
    return pltpu.CompilerParams(dimension_semantics=sem, vmem_limit_bytes=vmem)


def _silu(v):
    return v * jax.nn.sigmoid(v)


def _nt_dot(a, b):
    return lax.dot_general(a, b, (((1,), (1,)), ((), ())), preferred_element_type=F32)


def _adaln_kernel(c_ref, w_ref, b_ref, o_ref):
    s = _silu(c_ref[...])
    o_ref[0] = jnp.dot(s, w_ref[0], preferred_element_type=F32, precision=HIGHEST) + b_ref[0]


def adaln_rows(c, w, b):
    L, D, D3 = w.shape
    B = c.shape[0]
    tn = 1024
    return pl.pallas_call(
        _adaln_kernel,
        grid=(L, D3 // tn),
        in_specs=[
            pl.BlockSpec((B, D), lambda l, j: (0, 0)),
            pl.BlockSpec((1, D, tn), lambda l, j: (l, 0, j)),
            pl.BlockSpec((1, 1, tn), lambda l, j: (l, 0, j)),
        ],
        out_specs=pl.BlockSpec((1, B, tn), lambda l, j: (l, 0, j)),
        out_shape=jax.ShapeDtypeStruct((L, B, D3), F32),
        compiler_params=_cp(("parallel", "parallel")),
        name="adaln_rows",
    )(c, w, b.reshape(L, 1, D3))


def _inproj_kernel(x_ref, g_ref, mod_ref, w_ref, *o_refs, chunk):
    D = x_ref.shape[1]
    x = x_ref[...]
    y = x * lax.rsqrt(jnp.mean(x * x, axis=-1, keepdims=True) + EPS) * g_ref[...]
    mod = mod_ref[0]
    h = (y * (1.0 + mod[:, D:2 * D]) + mod[:, :D]).astype(BF16)
    col = 0
    for o_ref in o_refs:
        width = o_ref.shape[1]
        for c0 in range(0, width, chunk):
            cw = min(chunk, width - c0)
            o_ref[:, c0:c0 + cw] = jnp.dot(
                h, w_ref[:, col + c0:col + c0 + cw], preferred_element_type=F32).astype(o_ref.dtype)
        col += width


def in_projection(x2, g, mod, w, seq):
    N, D = x2.shape
    tm = 512
    tpb = seq // tm
    return pl.pallas_call(
        functools.partial(_inproj_kernel, chunk=512),
        grid=(N // tm,),
        in_specs=[
            pl.BlockSpec((tm, D), lambda i: (i, 0)),
            pl.BlockSpec((1, D), lambda i: (0, 0)),
            pl.BlockSpec((1, 1, 3 * D), lambda i: (i // tpb, 0, 0)),
            pl.BlockSpec(w.shape, lambda i: (0, 0), pipeline_mode=pl.Buffered(1)),
        ],
        out_specs=[pl.BlockSpec((tm, wd), lambda i: (i, 0)) for wd in IN_OUT_WIDTHS],
        out_shape=[jax.ShapeDtypeStruct((N, wd), BF16) for wd in IN_OUT_WIDTHS],
        compiler_params=_cp(("parallel",), VMEM_LIMIT),
        name="in_projection",
    )(x2, g.reshape(1, D), mod, w)


def _dilated_kernel(q_ref, kc_ref, kp_ref, vtc_ref, vtp_ref, ot_ref, lse_ref):
    i = pl.program_id(1)
    q, kc, kp, vtc, vtp = q_ref[0], kc_ref[0], kp_ref[0], vtc_ref[0], vtp_ref[0]
    T = q.shape[0]
    key = lax.broadcasted_iota(jnp.int32, (T, T), 0)
    qry = lax.broadcasted_iota(jnp.int32, (T, T), 1)
    valid_c = key <= qry
    valid_p = (key >= qry) & (i > 0)
    outs, lses = [], []
    for h in range(HEADS_PER_GROUP_A):
        sl = slice(h * HEAD_DIM_A, (h + 1) * HEAD_DIM_A)
        qh = q[:, sl]
        sc = jnp.where(valid_c, _nt_dot(kc[:, sl], qh), NEG)
        sp = jnp.where(valid_p, _nt_dot(kp[:, sl], qh), NEG)
        m = jnp.maximum(jnp.max(sc, axis=0, keepdims=True), jnp.max(sp, axis=0, keepdims=True))
        pc = jnp.exp(sc - m)
        pp = jnp.exp(sp - m)
        den = jnp.sum(pc, axis=0, keepdims=True) + jnp.sum(pp, axis=0, keepdims=True)
        o = jnp.dot(vtc[sl, :], pc.astype(BF16), preferred_element_type=F32)
        o = o + jnp.dot(vtp[sl, :], pp.astype(BF16), preferred_element_type=F32)
        outs.append(o / den)
        lses.append(m + jnp.log(den))
    ot_ref[0] = jnp.concatenate(outs, axis=0).astype(ot_ref.dtype)
    lse_ref[0] = jnp.concatenate(lses, axis=0)


def dilated_attention(qk, v, batch, seq, dilation):
    L = seq // dilation
    nb = L // Q_BLOCK
    H = HEADS_PER_GROUP_A
    qk_r = qk.reshape(batch, L, dilation, 2 * GROUP_W).transpose(0, 2, 1, 3).reshape(batch * dilation, L, 2 * GROUP_W)
    vt_r = v.reshape(batch, L, dilation, GROUP_W).transpose(0, 2, 3, 1).reshape(batch * dilation, GROUP_W, L)
    row_blk = (1, Q_BLOCK, GROUP_W)
    col_blk = (1, GROUP_W, Q_BLOCK)
    ot, lse = pl.pallas_call(
        _dilated_kernel,
        grid=(batch * dilation, nb),
        in_specs=[
            pl.BlockSpec(row_blk, lambda s, i: (s, i, 0)),
            pl.BlockSpec(row_blk, lambda s, i: (s, i, 1)),
            pl.BlockSpec(row_blk, lambda s, i: (s, jnp.maximum(i - 1, 0), 1)),
            pl.BlockSpec(col_blk, lambda s, i: (s, 0, i)),
            pl.BlockSpec(col_blk, lambda s, i: (s, 0, jnp.maximum(i - 1, 0))),
        ],
        out_specs=[
            pl.BlockSpec(col_blk, lambda s, i: (s, 0, i)),
            pl.BlockSpec((1, H, Q_BLOCK), lambda s, i: (s, 0, i)),
        ],
        out_shape=[
            jax.ShapeDtypeStruct((batch * dilation, GROUP_W, L), BF16),
            jax.ShapeDtypeStruct((batch * dilation, H, L), F32),
        ],
        compiler_params=_cp(("parallel", "parallel")),
        name=f"dilated_attention_d{dilation}",
    )(qk_r, qk_r, qk_r, vt_r, vt_r)
    o = ot.reshape(batch, dilation, GROUP_W, L).transpose(0, 3, 1, 2).reshape(batch * seq, GROUP_W)
    lse = lse.reshape(batch, dilation, H, L).transpose(0, 3, 1, 2).reshape(batch * seq, H)
    return o, jnp.repeat(lse, HEAD_DIM_A, axis=1)


def _mla_prep_kernel(lat_ref, pos_ref, gq_ref, gkv_ref, wq_ref, wk_ref, wvt_ref, freq_ref, exp_ref, one_ref,
                     q_ref, k_ref, vt_ref):
    HP = N_HEADS_C * HEAD_PAD_C
    lat = lat_ref[...].astype(F32)
    cq = lat[:, :Q_LORA]
    ckr = lat[:, Q_LORA:]
    zq = (cq * lax.rsqrt(jnp.mean(cq * cq, axis=-1, keepdims=True) + EPS) * gq_ref[...]).astype(BF16)
    lane = lax.broadcasted_iota(jnp.int32, ckr.shape, 1)
    is_kv = lane < KV_LORA
    ms = jnp.sum(jnp.where(is_kv, ckr * ckr, 0.0), axis=-1, keepdims=True) * (1.0 / KV_LORA)
    zkv = (ckr * jnp.where(is_kv, lax.rsqrt(ms + EPS) * gkv_ref[...], 1.0)).astype(BF16)
    qq = jnp.dot(zq, wq_ref[...], preferred_element_type=F32)
    kk = jnp.dot(zkv, wk_ref[...], preferred_element_type=F32)
    ang = pos_ref[...].astype(F32) * freq_ref[...]
    cos = jnp.dot(jnp.cos(ang), exp_ref[...], preferred_element_type=F32, precision=HIGHEST) + one_ref[...]
    sin = jnp.dot(jnp.sin(ang), exp_ref[...], preferred_element_type=F32, precision=HIGHEST)
    q_ref[...] = (qq[:, :HP] * cos + qq[:, HP:] * sin).astype(q_ref.dtype)
    k_ref[...] = (kk[:, :HP] * cos + kk[:, HP:] * sin).astype(k_ref.dtype)
    vt_ref[0] = _nt_dot(wvt_ref[...], zkv).astype(vt_ref.dtype)


def _mla_weights(cq_g, ckv_g, w_uq, w_ukv):
    H, HPAD, half = N_HEADS_C, HEAD_PAD_C, QK_ROPE // 2
    scale = (QK_NOPE + QK_ROPE) ** -0.5 * math.log2(math.e)
    wq = w_uq.reshape(Q_LORA, H, QK_NOPE + QK_ROPE) * scale
    q_lin = jnp.pad(wq, ((0, 0), (0, 0), (0, HPAD - QK_NOPE - QK_ROPE)))
    r1, r2 = wq[..., QK_NOPE:QK_NOPE + half], wq[..., QK_NOPE + half:]
    q_sw = jnp.concatenate([jnp.zeros((Q_LORA, H, QK_NOPE), F32), -r2, r1,
                            jnp.zeros((Q_LORA, H, HPAD - QK_NOPE - QK_ROPE), F32)], axis=-1)
    wq_big = jnp.concatenate([q_lin.reshape(Q_LORA, H * HPAD), q_sw.reshape(Q_LORA, H * HPAD)], axis=1)

    rows = LAT_W - Q_LORA
    wkv = w_ukv.reshape(KV_LORA, H, QK_NOPE + V_DIM)
    eye = jnp.eye(QK_ROPE, dtype=F32)
    k_lin = jnp.zeros((rows, H, HPAD), F32)
    k_lin = k_lin.at[:KV_LORA, :, :QK_NOPE].set(wkv[..., :QK_NOPE])
    k_lin = k_lin.at[KV_LORA:KV_LORA + QK_ROPE, :, QK_NOPE:QK_NOPE + QK_ROPE].set(
        jnp.broadcast_to(eye[:, None, :], (QK_ROPE, H, QK_ROPE)))
    swap = jnp.zeros((QK_ROPE, QK_ROPE), F32).at[half:, :half].set(-jnp.eye(half)).at[:half, half:].set(jnp.eye(half))
    k_sw = jnp.zeros((rows, H, HPAD), F32)
    k_sw = k_sw.at[KV_LORA:KV_LORA + QK_ROPE, :, QK_NOPE:QK_NOPE + QK_ROPE].set(
        jnp.broadcast_to(swap[:, None, :], (QK_ROPE, H, QK_ROPE)))
    v_w = jnp.zeros((rows, H, V_DIM), F32).at[:KV_LORA].set(wkv[..., QK_NOPE:])
    wk_big = jnp.concatenate([k_lin.reshape(rows, H * HPAD), k_sw.reshape(rows, H * HPAD)], axis=1)
    wv_t = v_w.reshape(rows, H * V_DIM).T

    gkv = jnp.concatenate([ckv_g, jnp.ones((rows - KV_LORA,), F32)]).reshape(1, rows)
    return cq_g.reshape(1, Q_LORA), gkv, wq_big.astype(BF16), wk_big.astype(BF16), wv_t.astype(BF16)


def _rope_tables():
    half = QK_ROPE // 2
    freqs = (ROPE_THETA ** (-jnp.arange(0, QK_ROPE, 2, dtype=F32) / QK_ROPE)).reshape(1, half)
    expand = np.zeros((half, N_HEADS_C, HEAD_PAD_C), np.float32)
    ones = np.zeros((1, N_HEADS_C, HEAD_PAD_C), np.float32)
    for j in range(half):
        expand[j, :, QK_NOPE + j] = 1.0
        expand[j, :, QK_NOPE + half + j] = 1.0
    ones[0, :, :QK_NOPE] = 1.0
    return freqs, jnp.asarray(expand.reshape(half, -1)), jnp.asarray(ones.reshape(1, -1))


def mla_prep(lat, pos_col, gq, gkv, wq_big, wk_big, wv_t, batch, seq):
    N = lat.shape[0]
    HP = N_HEADS_C * HEAD_PAD_C
    tm = 512
    tpb = seq // tm
    freqs, expand, ones = _rope_tables()
    const = lambda shape: pl.BlockSpec(shape, lambda i: (0, 0))
    return pl.pallas_call(
        _mla_prep_kernel,
        grid=(N // tm,),
        in_specs=[
            pl.BlockSpec((tm, LAT_W), lambda i: (i, 0)),
            pl.BlockSpec((tm, 1), lambda i: (i, 0)),
            const(gq.shape), const(gkv.shape), const(wq_big.shape), const(wk_big.shape), const(wv_t.shape),
            const(freqs.shape), const(expand.shape), const(ones.shape),
        ],
        out_specs=[
            pl.BlockSpec((tm, HP), lambda i: (i, 0)),
            pl.BlockSpec((tm, HP), lambda i: (i, 0)),
            pl.BlockSpec((1, DC, tm), lambda i: (i // tpb, 0, i % tpb)),
        ],
        out_shape=[
            jax.ShapeDtypeStruct((N, HP), BF16),
            jax.ShapeDtypeStruct((N, HP), BF16),
            jax.ShapeDtypeStruct((batch, DC, seq), BF16),
        ],
        compiler_params=_cp(("parallel",), VMEM_LIMIT),
        name="mla_prep",
    )(lat, pos_col, gq, gkv, wq_big, wk_big, wv_t, freqs, expand, ones)


HEADS_PER_STEP_C = 2


def _mla_flash_kernel(qi_ref, ki_ref, q_ref, k_ref, vt_ref, o_ref, m_sc, l_sc, acc_sc):
    t = pl.program_id(2)
    qi, ki = qi_ref[t], ki_ref[t]

    @pl.when(ki == 0)
    def _():
        m_sc[...] = jnp.full(m_sc.shape, NEG, F32)
        l_sc[...] = jnp.zeros(l_sc.shape, F32)
        acc_sc[...] = jnp.zeros(acc_sc.shape, F32)

    def step(masked):
        T = q_ref.shape[1]
        if masked:
            key = lax.broadcasted_iota(jnp.int32, (T, T), 0)
            qry = lax.broadcasted_iota(jnp.int32, (T, T), 1)
            keep = key <= qry
        for h in range(HEADS_PER_STEP_C):
            q = q_ref[0, :, h * HEAD_PAD_C:(h + 1) * HEAD_PAD_C]
            k = k_ref[0, :, h * HEAD_PAD_C:(h + 1) * HEAD_PAD_C]
            vt = vt_ref[0, h * V_DIM:(h + 1) * V_DIM, :]
            st = _nt_dot(k, q)
            if masked:
                st = jnp.where(keep, st, NEG)
            m_prev = m_sc[h]
            m_new = jnp.maximum(m_prev, jnp.max(st, axis=0, keepdims=True))
            alpha = jnp.exp2(m_prev - m_new)
            p = jnp.exp2(st - m_new)
            l_sc[h] = alpha * l_sc[h] + jnp.sum(p, axis=0, keepdims=True)
            acc_sc[h] = alpha * acc_sc[h] + jnp.dot(vt, p.astype(BF16), preferred_element_type=F32)
            m_sc[h] = m_new

    @pl.when(ki < qi)
    def _():
        step(False)

    @pl.when(ki == qi)
    def _():
        step(True)
        ot = jnp.concatenate([acc_sc[h] / l_sc[h] for h in range(HEADS_PER_STEP_C)], axis=0)
        o_ref[0] = ot.T.astype(o_ref.dtype)


def mla_attention(q_all, k_all, vt_all, batch, seq):
    T = 512
    nq = seq // T
    pairs = [(a, b) for a in range(nq) for b in range(a + 1)]
    qi_tab = jnp.asarray([p[0] for p in pairs], jnp.int32)
    ki_tab = jnp.asarray([p[1] for p in pairs], jnp.int32)
    hp = N_HEADS_C // HEADS_PER_STEP_C
    qw = HEADS_PER_STEP_C * HEAD_PAD_C
    vw = HEADS_PER_STEP_C * V_DIM
    q3 = q_all.reshape(batch, seq, -1)
    k3 = k_all.reshape(batch, seq, -1)
    grid_spec = pltpu.PrefetchScalarGridSpec(
        num_scalar_prefetch=2,
        grid=(batch, hp, len(pairs)),
        in_specs=[
            pl.BlockSpec((1, T, qw), lambda b, h, t, qi, ki: (b, qi[t], h)),
            pl.BlockSpec((1, T, qw), lambda b, h, t, qi, ki: (b, ki[t], h)),
            pl.BlockSpec((1, vw, T), lambda b, h, t, qi, ki: (b, h, ki[t])),
        ],
        out_specs=pl.BlockSpec((1, T, vw), lambda b, h, t, qi, ki: (b, qi[t], h)),
        scratch_shapes=[
            pltpu.VMEM((HEADS_PER_STEP_C, 1, T), F32),
            pltpu.VMEM((HEADS_PER_STEP_C, 1, T), F32),
            pltpu.VMEM((HEADS_PER_STEP_C, V_DIM, T), F32),
        ],
    )
    o = pl.pallas_call(
        _mla_flash_kernel,
        grid_spec=grid_spec,
        out_shape=jax.ShapeDtypeStruct((batch, seq, DC), BF16),
        compiler_params=_cp(("parallel", "parallel", "arbitrary")),
        name="mla_attention",
    )(qi_tab, ki_tab, q3, k3, vt_all)
    return o.reshape(batch * seq, DC)


def _mixout_kernel(x_ref, gates_ref, ub_ref, ubh_ref, o1_ref, o2_ref, o3_ref, l1_ref, l2_ref, l3_ref, yc_ref,
                   mod1_ref, mod2_ref, g2_ref, poolw_ref, pscale_ref, woa_ref, wob_ref, woc_ref, wout_ref,
                   rwt_ref, sw1_ref, sw3_ref, sw2_ref,
                   xmid_ref, h2_ref, logit_ref, *, tiles_per_batch):
    D = x_ref.shape[1]
    tm = x_ref.shape[0]
    tile = pl.program_id(0) % tiles_per_batch

    l1, l2, l3 = l1_ref[...], l2_ref[...], l3_ref[...]
    mx = jnp.maximum(jnp.maximum(l1, l2), l3)
    e1, e2, e3 = jnp.exp(l1 - mx), jnp.exp(l2 - mx), jnp.exp(l3 - mx)
    ya = (e1 * o1_ref[...].astype(F32) + e2 * o2_ref[...].astype(F32) + e3 * o3_ref[...].astype(F32)) / (e1 + e2 + e3)
    a_out = jnp.dot(ya.astype(BF16), woa_ref[...], preferred_element_type=F32)

    u = ub_ref[...].astype(F32)
    halo = jnp.where(tile > 0, ubh_ref[...].astype(F32), 0.0)
    ext = jnp.concatenate([halo, u], axis=0)
    t_seq = tile * tm + lax.broadcasted_iota(jnp.int32, (tm, 1), 0)
    pooled = []
    for gi, w in enumerate(POOL_WINDOWS):
        sl = slice(gi * POOL_GROUP_DIM, (gi + 1) * POOL_GROUP_DIM)
        acc = ext[:, sl]
        k = 1
        while k < w:
            acc = acc + pltpu.roll(acc, k, axis=0)
            k *= 2
        cnt = jnp.minimum(t_seq + 1, w).astype(F32)
        pg = acc[POOL_HALO:] / cnt - u[:, sl]
        pooled.append(jnp.dot(pg.astype(BF16), poolw_ref[gi], preferred_element_type=F32))
    yb = jnp.concatenate(pooled, axis=1) * pscale_ref[...]
    b_out = jnp.dot(yb.astype(BF16), wob_ref[...], preferred_element_type=F32)
    c_out = jnp.dot(yc_ref[...], woc_ref[...], preferred_element_type=F32)

    g = gates_ref[...].astype(F32)
    mix = (jax.nn.sigmoid(g[:, :D]) * a_out + jax.nn.sigmoid(g[:, D:2 * D]) * b_out
           + jax.nn.sigmoid(g[:, 2 * D:]) * c_out)
    tok = jnp.dot(mix.astype(BF16), wout_ref[...], preferred_element_type=F32)
    xn = x_ref[...] + mod1_ref[0][:, 2 * D:] * tok

    mod2 = mod2_ref[0]
    y = xn * lax.rsqrt(jnp.mean(xn * xn, axis=-1, keepdims=True) + EPS) * g2_ref[...]
    h2 = y * (1.0 + mod2[:, D:2 * D]) + mod2[:, :D]
    h2_ref[...] = h2
    h2b = h2.astype(BF16)
    logit_ref[...] = _nt_dot(rwt_ref[...], h2b)
    hid = _silu(jnp.dot(h2b, sw1_ref[...], preferred_element_type=F32)) * jnp.dot(
        h2b, sw3_ref[...], preferred_element_type=F32)
    shared = jnp.dot(hid.astype(BF16), sw2_ref[...], preferred_element_type=F32)
    xmid_ref[...] = xn + mod2[:, 2 * D:] * shared


def mix_out(x2, gu, dil, yc, mod1, mod2, g2, pool_w, pool_scale, w_oa, w_ob, w_oc, w_out, rwt, sw1, sw3, sw2, seq):
    N, D = x2.shape
    tm = 256
    tpb = seq // tm
    (o1, l1), (o2, l2), (o3, l3) = dil
    row = lambda w, c=0: pl.BlockSpec((tm, w), lambda i: (i, c))
    const2 = lambda a: pl.BlockSpec(a.shape, lambda i: (0,) * a.ndim)
    modspec = pl.BlockSpec((1, 1, 3 * D), lambda i: (i // tpb, 0, 0))
    ub_col = 3 * D // DB
    halo_spec = pl.BlockSpec(
        (POOL_HALO, DB), lambda i: (jnp.maximum(i * (tm // POOL_HALO) - 1, 0), ub_col))
    weights = [g2.reshape(1, D), pool_w, pool_scale.reshape(1, DB), w_oa, w_ob, w_oc, w_out, rwt, sw1, sw3, sw2]
    return pl.pallas_call(
        functools.partial(_mixout_kernel, tiles_per_batch=tpb),
        grid=(N // tm,),
        in_specs=[
            row(D), row(3 * D), row(DB, ub_col), halo_spec,
            row(GROUP_W), row(GROUP_W), row(GROUP_W), row(GROUP_W), row(GROUP_W), row(GROUP_W), row(DC),
            modspec, modspec,
        ] + [const2(a) for a in weights],
        out_specs=[row(D), row(D), pl.BlockSpec((N_EXPERTS, tm), lambda i: (0, i))],
        out_shape=[
            jax.ShapeDtypeStruct((N, D), F32),
            jax.ShapeDtypeStruct((N, D), F32),
            jax.ShapeDtypeStruct((N_EXPERTS, N), F32),
        ],
        compiler_params=_cp(("parallel",), VMEM_LIMIT),
        name="mix_out",
    )(x2, gu, gu, gu, o1, o2, o3, l1, l2, l3, yc, mod1, mod2, *weights)


def _route_kernel(lg_ref, bias_ref, w_ref, sel_ref):
    G, GS = N_GROUPS, GROUP_SIZE
    scores = jax.nn.sigmoid(lg_ref[...])
    sel = scores + bias_ref[...]
    tn = sel.shape[1]
    eio = lax.broadcasted_iota(jnp.int32, (GS, tn), 0)
    ninf = -jnp.inf

    gs = []
    for g in range(G):
        v = sel[g * GS:(g + 1) * GS]
        m1 = jnp.max(v, axis=0, keepdims=True)
        i1 = jnp.min(jnp.where(v == m1, eio, GS), axis=0, keepdims=True)
        m2 = jnp.max(jnp.where(eio == i1, ninf, v), axis=0, keepdims=True)
        gs.append(m1 + m2)
    gsm = jnp.concatenate(gs, axis=0)
    gio = lax.broadcasted_iota(jnp.int32, (G, tn), 0)
    rank = jnp.zeros((G, tn), jnp.int32)
    for g2 in range(G):
        beats = (gs[g2] > gsm) | ((gs[g2] == gsm) & (g2 < gio))
        rank = rank + beats.astype(jnp.int32)
    gsel = rank < TOPK_GROUPS

    vs = [jnp.where(gsel[g:g + 1], sel[g * GS:(g + 1) * GS], NEG) for g in range(G)]
    eid = [eio + g * GS for g in range(G)]
    chosen = [jnp.zeros((GS, tn), jnp.bool_) for _ in range(G)]
    for _ in range(TOP_K):
        m = functools.reduce(jnp.maximum, [jnp.max(v, axis=0, keepdims=True) for v in vs])
        idx = functools.reduce(jnp.minimum, [
            jnp.min(jnp.where(v == m, e, N_EXPERTS), axis=0, keepdims=True) for v, e in zip(vs, eid)])
        for g in range(G):
            hit = eid[g] == idx
            chosen[g] = chosen[g] | hit
            vs[g] = jnp.where(hit, ninf, vs[g])
    mask = jnp.concatenate(chosen, axis=0).astype(F32)
    wk = scores * mask
    w_ref[...] = wk / jnp.sum(wk, axis=0, keepdims=True) * ROUTED_SCALE
    sel_ref[...] = mask


def route(logits_t, bias):
    E, N = logits_t.shape
    tn = 1024
    spec = pl.BlockSpec((E, tn), lambda i: (0, i))
    return pl.pallas_call(
        _route_kernel,
        grid=(N // tn,),
        in_specs=[spec, pl.BlockSpec((E, 1), lambda i: (0, 0))],
        out_specs=[spec, spec],
        out_shape=[jax.ShapeDtypeStruct((E, N), F32), jax.ShapeDtypeStruct((E, N), F32)],
        compiler_params=_cp(("parallel",)),
        name="route",
    )(logits_t, bias.reshape(E, 1))


SLOT_BLOCK = 256


def dispatch_tables(w_t, sel_t):
    E, N = sel_t.shape
    M = N * TOP_K
    blk = SLOT_BLOCK
    P = M + E * blk
    nblk = P // blk
    counts = jnp.sum(sel_t, axis=1).astype(jnp.int32)
    pos = jnp.nonzero(sel_t.reshape(-1) > 0, size=M, fill_value=0)[0].astype(jnp.int32)
    t_sorted = pos % N
    k_of = (jnp.cumsum(sel_t, axis=0) - sel_t).astype(jnp.int32).reshape(-1)[pos]
    w_sorted = w_t.reshape(-1)[pos]
    padded = (counts + blk - 1) // blk * blk
    pend = jnp.cumsum(padded)
    pstart = pend - padded
    start = jnp.cumsum(counts) - counts
    blk_e = jnp.minimum(jnp.searchsorted(pend, jnp.arange(nblk, dtype=jnp.int32) * blk, side='right'),
                        E - 1).astype(jnp.int32)
    p = jnp.arange(P, dtype=jnp.int32)
    e_p = blk_e[p // blk]
    j = p - pstart[e_p]
    valid = (j < counts[e_p]) & (p < pend[-1])
    src = jnp.clip(start[e_p] + j, 0, M - 1)
    slot_tok = jnp.where(valid, t_sorted[src], 0).astype(jnp.int32)
    slot_dst = jnp.where(valid, t_sorted[src] * TOP_K + k_of[src], -1).astype(jnp.int32)
    slot_w = jnp.where(valid, w_sorted[src], 0.0).reshape(nblk, 1, blk)
    nused = (pend[-1] // blk).astype(jnp.int32).reshape(1)
    return blk_e, slot_tok, slot_dst, slot_w, nused


def _expert_kernel(blk_e_ref, nused_ref,
                   tok0_ref, tokn_ref, dst_ref, h2_hbm, sw_ref, w1_ref, w3_ref, w2_ref, out8_hbm,
                   xbuf, ybuf, gsem, ssem):
    blk = xbuf.shape[1]
    dump_base = out8_hbm.shape[0] - 2 * blk
    b = pl.program_id(0)
    nb = pl.num_programs(0)
    nused = nused_ref[0]
    slot = b % 2

    def gather_rows(idx_ref, sl):
        def body(j, carry):
            tok = idx_ref[0, 0, j]
            pltpu.make_async_copy(h2_hbm.at[pl.ds(tok, 1)], xbuf.at[sl, pl.ds(j, 1)], gsem.at[sl]).start()
            return carry

        lax.fori_loop(0, blk, body, 0, unroll=8)

    def scatter_rows(sl):
        def body(j, carry):
            dst = dst_ref[0, 0, j]
            dst = jnp.where(dst < 0, dump_base + sl * blk + j, dst)
            pltpu.make_async_copy(ybuf.at[sl, pl.ds(j, 1)], out8_hbm.at[pl.ds(dst, 1)], ssem.at[sl]).start()
            return carry

        lax.fori_loop(0, blk, body, 0, unroll=8)

    def wait_gather(sl):
        pltpu.make_async_copy(h2_hbm.at[pl.ds(0, blk)], xbuf.at[sl], gsem.at[sl]).wait()

    def wait_scatter(sl):
        pltpu.make_async_copy(ybuf.at[sl], out8_hbm.at[pl.ds(0, blk)], ssem.at[sl]).wait()

    @pl.when(b == 0)
    def _():
        ybuf[0] = jnp.zeros(ybuf.shape[1:], ybuf.dtype)
        for half in range(2):
            fill = pltpu.make_async_copy(ybuf.at[0], out8_hbm.at[pl.ds(dump_base + half * blk, blk)], ssem.at[0])
            fill.start()
            fill.wait()

    @pl.when((b == 0) & (nused > 0))
    def _():
        gather_rows(tok0_ref, 0)

    @pl.when(b < nused)
    def _():
        @pl.when(b + 1 < nused)
        def _():
            gather_rows(tokn_ref, 1 - slot)

        wait_gather(slot)
        x = xbuf[slot].astype(BF16)
        hid = _silu(jnp.dot(x, w1_ref[0].astype(BF16), preferred_element_type=F32)) * jnp.dot(
            x, w3_ref[0].astype(BF16), preferred_element_type=F32)
        y = jnp.dot(hid.astype(BF16), w2_ref[0].astype(BF16), preferred_element_type=F32)
        wrow = sw_ref[0]
        eye = lax.broadcasted_iota(jnp.int32, (blk, blk), 0) == lax.broadcasted_iota(jnp.int32, (blk, blk), 1)
        wcol = jnp.sum(jnp.where(eye, wrow, 0.0), axis=1, keepdims=True)

        @pl.when(b >= 2)
        def _():
            wait_scatter(slot)

        ybuf[slot] = y * wcol
        scatter_rows(slot)

    @pl.when(b == nb - 1)
    def _():
        @pl.when(nused >= 1)
        def _():
            wait_scatter((nused - 1) % 2)

        @pl.when(nused >= 2)
        def _():
            wait_scatter(nused % 2)


def routed_experts(h2, tables, w1, w3, w2):
    blk_e, slot_tok, slot_dst, slot_w, nused = tables
    N, D = h2.shape
    blk = SLOT_BLOCK
    nblk = blk_e.shape[0]
    FF = w1.shape[2]
    tok3 = slot_tok.reshape(nblk, 1, blk)
    dst3 = slot_dst.reshape(nblk, 1, blk)
    smem_blk = lambda imap: pl.BlockSpec((1, 1, blk), imap, memory_space=pltpu.SMEM)
    grid_spec = pltpu.PrefetchScalarGridSpec(
        num_scalar_prefetch=2,
        grid=(nblk,),
        in_specs=[
            smem_blk(lambda b, be, nu: (0, 0, 0)),
            smem_blk(lambda b, be, nu: (jnp.minimum(b + 1, nblk - 1), 0, 0)),
            smem_blk(lambda b, be, nu: (b, 0, 0)),
            pl.BlockSpec(memory_space=pl.ANY),
            pl.BlockSpec((1, 1, blk), lambda b, be, nu: (b, 0, 0)),
            pl.BlockSpec((1, D, FF), lambda b, be, nu: (be[b], 0, 0)),
            pl.BlockSpec((1, D, FF), lambda b, be, nu: (be[b], 0, 0)),
            pl.BlockSpec((1, FF, D), lambda b, be, nu: (be[b], 0, 0)),
        ],
        out_specs=pl.BlockSpec(memory_space=pl.ANY),
        scratch_shapes=[
            pltpu.VMEM((2, blk, D), F32),
            pltpu.VMEM((2, blk, D), F32),
            pltpu.SemaphoreType.DMA((2,)),
            pltpu.SemaphoreType.DMA((2,)),
        ],
    )
    return pl.pallas_call(
        _expert_kernel,
        grid_spec=grid_spec,
        out_shape=jax.ShapeDtypeStruct((N * TOP_K + 2 * blk, D), F32),
        compiler_params=_cp(("arbitrary",), VMEM_LIMIT),
        name="routed_experts",
    )(blk_e, nused, tok3, tok3, dst3, h2, slot_w, w1, w3, w2)


def _combine_kernel(xmid_ref, o8_ref, mod2_ref, fg_ref, out_ref, *, final):
    D = xmid_ref.shape[1]
    acc = o8_ref[:, 0:D]
    for k in range(1, TOP_K):
        acc = acc + o8_ref[:, k * D:(k + 1) * D]
    x = xmid_ref[...] + mod2_ref[0][:, 2 * D:] * acc
    if final:
        x = x * lax.rsqrt(jnp.mean(x * x, axis=-1, keepdims=True) + EPS) * fg_ref[...]
    out_ref[...] = x


def combine(xmid, out8, mod2, final_g, seq, final):
    N, D = xmid.shape
    tm = 256
    tpb = seq // tm
    o8 = out8.reshape(-1, TOP_K * D)
    return pl.pallas_call(
        functools.partial(_combine_kernel, final=final),
        grid=(N // tm,),
        in_specs=[
            pl.BlockSpec((tm, D), lambda i: (i, 0)),
            pl.BlockSpec((tm, TOP_K * D), lambda i: (i, 0)),
            pl.BlockSpec((1, 1, 3 * D), lambda i: (i // tpb, 0, 0)),
            pl.BlockSpec((1, D), lambda i: (0, 0)),
        ],
        out_specs=pl.BlockSpec((tm, D), lambda i: (i, 0)),
        out_shape=jax.ShapeDtypeStruct((N, D), F32),
        compiler_params=_cp(("parallel",), VMEM_LIMIT),
        name="combine",
    )(xmid, o8, mod2, final_g.reshape(1, D))


def _permute_w_in(w):
    ub = w[:, 3 * DA:3 * DA + DB]
    lat_lo = 3 * DA + DB
    lat_hi = lat_lo + Q_LORA + KV_LORA + QK_ROPE
    lat, gates = w[:, lat_lo:lat_hi], w[:, lat_hi:]
    pad = jnp.zeros((w.shape[0], LAT_W - (lat_hi - lat_lo)), w.dtype)
    parts = [gates, ub, lat, pad]
    for g in range(len(DIL_GROUPS)):
        sl = slice(g * GROUP_W, (g + 1) * GROUP_W)
        parts += [w[:, :DA][:, sl] * (HEAD_DIM_A ** -0.5), w[:, DA:2 * DA][:, sl], w[:, 2 * DA:3 * DA][:, sl]]
    return jnp.concatenate(parts, axis=1).astype(BF16)


def kernel(x, c, positions, ada_mix_w, ada_mix_b, norm_mix_g, w_in, pool_w, pool_scale, cq_norm_g, ckv_norm_g, w_uq, w_ukv, w_oa, w_ob, w_oc, w_out, ada_ffn_w, ada_ffn_b, norm_ffn_g, router_w, router_bias, exp_w1, exp_w3, exp_w2, sh_w1, sh_w3, sh_w2, final_g):
    B, S, D = x.shape
    depth = w_in.shape[0]
    N = B * S
    mod_mix = adaln_rows(c, ada_mix_w, ada_mix_b)
    mod_ffn = adaln_rows(c, ada_ffn_w, ada_ffn_b)
    pos_col = positions.reshape(N, 1)
    x2 = x.reshape(N, D)
    for l in range(depth):
        mod1 = mod_mix[l].reshape(B, 1, 3 * D)
        mod2 = mod_ffn[l].reshape(B, 1, 3 * D)
        gu, lat, *qkv = in_projection(x2, norm_mix_g[l], mod1, _permute_w_in(w_in[l]), S)
        dil = [dilated_attention(qkv[2 * g], qkv[2 * g + 1], B, S, d) for g, (_, d) in enumerate(DIL_GROUPS)]
        mla_w = _mla_weights(cq_norm_g[l], ckv_norm_g[l], w_uq[l], w_ukv[l])
        q_all, k_all, vt_all = mla_prep(lat, pos_col, *mla_w, B, S)
        yc = mla_attention(q_all, k_all, vt_all, B, S)
        xmid, h2, logits_t = mix_out(
            x2, gu, dil, yc, mod1, mod2, norm_ffn_g[l], pool_w[l].astype(BF16), pool_scale[l],
            w_oa[l].astype(BF16), w_ob[l].astype(BF16), w_oc[l].astype(BF16), w_out[l].astype(BF16),
            router_w[l].T.astype(BF16), sh_w1[l].astype(BF16), sh_w3[l].astype(BF16), sh_w2[l].astype(BF16), S)
        w_t, sel_t = route(logits_t, router_bias[l])
        tables = dispatch_tables(w_t, sel_t)
        out8 = routed_experts(h2, tables, exp_w1[l], exp_w3[l], exp_w2[l])
        x2 = combine(xmid, out8, mod2, final_g, S, final=(l == depth - 1))
    return x2.reshape(B, S, D)
```

```python
import functools
import math

import jax
import jax.numpy as jnp
import numpy as np
from jax import lax
from jax.experimental import pallas as pl
from jax.experimental.pallas import tpu as pltpu
from jax.experimental.pallas import tpu_sc as plsc

F32 = jnp.float32
BF16 = jnp.bfloat16
HIGHEST = lax.Precision.HIGHEST

D_MODEL = 1024
HEAD_DIM_A = 64
HEADS_PER_GROUP_A = 4
DIL_GROUPS = ((128, 1), (512, 4), (2048, 16))
GROUP_W = HEADS_PER_GROUP_A * HEAD_DIM_A
DA = GROUP_W * len(DIL_GROUPS)
POOL_WINDOWS = (2, 4, 8, 16)
POOL_GROUP_DIM = 128
DB = POOL_GROUP_DIM * len(POOL_WINDOWS)
POOL_HALO = 16
N_HEADS_C = 8
QK_NOPE = 64
QK_ROPE = 32
V_DIM = 64
Q_LORA = 384
KV_LORA = 256
DC = N_HEADS_C * V_DIM
HEAD_PAD_C = 128
ROPE_THETA = 10000.0
N_EXPERTS = 64
TOP_K = 8
N_GROUPS = 8
TOPK_GROUPS = 4
GROUP_SIZE = N_EXPERTS // N_GROUPS
EXPERT_FF = 256
ROUTED_SCALE = 2.5
EPS = 1e-6
NEG = -1e30
Q_BLOCK = 128

LAT_W = 768
GU_W = 3 * D_MODEL + DB
IN_OUT_WIDTHS = (GU_W, LAT_W) + (2 * GROUP_W, GROUP_W) * len(DIL_GROUPS)

VMEM_LIMIT = 56 * 1024 * 1024


def _cp(sem, vmem=None):
    return pltpu.CompilerParams(dimension_semantics=sem, vmem_limit_bytes=vmem)


def _silu(v):
    return v * jax.nn.sigmoid(v)


def _nt_dot(a, b):
    return lax.dot_general(a, b, (((1,), (1,)), ((), ())), preferred_element_type=F32)


PACK_W = D_MODEL // 4
_HI_MASK = -65536


def _bf16_bits(v):
    return lax.bitcast_convert_type(v.astype(BF16).astype(F32), jnp.int32)


def _pack_row_halves(v):
    halves = []
    for h in range(2):
        lo = _bf16_bits(v[:, (2 * h) * PACK_W:(2 * h + 1) * PACK_W])
        hi = _bf16_bits(v[:, (2 * h + 1) * PACK_W:(2 * h + 2) * PACK_W])
        halves.append(lax.shift_right_logical(lo, 16) | (hi & _HI_MASK))
    return halves


def _unpack_row_halves(wa, wb):
    parts = []
    for w in (wa, wb):
        parts.append(lax.bitcast_convert_type(lax.shift_left(w, 16), F32))
        parts.append(lax.bitcast_convert_type(w & _HI_MASK, F32))
    return jnp.concatenate(parts, axis=1)


def _adaln_kernel(c_ref, w_ref, b_ref, o_ref):
    s = _silu(c_ref[...])
    o_ref[0] = jnp.dot(s, w_ref[0], preferred_element_type=F32, precision=HIGHEST) + b_ref[0]


def adaln_rows(c, w, b):
    L, D, D3 = w.shape
    B = c.shape[0]
    tn = 1024
    return pl.pallas_call(
        _adaln_kernel,
        grid=(L, D3 // tn),
        in_specs=[
            pl.BlockSpec((B, D), lambda l, j: (0, 0)),
            pl.BlockSpec((1, D, tn), lambda l, j: (l, 0, j)),
            pl.BlockSpec((1, 1, tn), lambda l, j: (l, 0, j)),
        ],
        out_specs=pl.BlockSpec((1, B, tn), lambda l, j: (l, 0, j)),
        out_shape=jax.ShapeDtypeStruct((L, B, D3), F32),
        compiler_params=_cp(("parallel", "parallel")),
        name="adaln_rows",
    )(c, w, b.reshape(L, 1, D3))


def _inproj_kernel(x_ref, g_ref, mod_ref, w_ref, *o_refs, chunk):
    D = x_ref.shape[1]
    x = x_ref[...]
    y = x * lax.rsqrt(jnp.mean(x * x, axis=-1, keepdims=True) + EPS) * g_ref[...]
    mod = mod_ref[0]
    h = (y * (1.0 + mod[:, D:2 * D]) + mod[:, :D]).astype(BF16)
    col = 0
    for o_ref in o_refs:
        width = o_ref.shape[1]
        for c0 in range(0, width, chunk):
            cw = min(chunk, width - c0)
            o_ref[:, c0:c0 + cw] = jnp.dot(
                h, w_ref[:, col + c0:col + c0 + cw], preferred_element_type=F32).astype(o_ref.dtype)
        col += width


def in_projection(x2, g, mod, w, seq):
    N, D = x2.shape
    tm = 512
    tpb = seq // tm
    return pl.pallas_call(
        functools.partial(_inproj_kernel, chunk=512),
        grid=(N // tm,),
        in_specs=[
            pl.BlockSpec((tm, D), lambda i: (i, 0)),
            pl.BlockSpec((1, D), lambda i: (0, 0)),
            pl.BlockSpec((1, 1, 3 * D), lambda i: (i // tpb, 0, 0)),
            pl.BlockSpec(w.shape, lambda i: (0, 0), pipeline_mode=pl.Buffered(1)),
        ],
        out_specs=[pl.BlockSpec((tm, wd), lambda i: (i, 0)) for wd in IN_OUT_WIDTHS],
        out_shape=[jax.ShapeDtypeStruct((N, wd), BF16) for wd in IN_OUT_WIDTHS],
        compiler_params=_cp(("parallel",), VMEM_LIMIT),
        name="in_projection",
    )(x2, g.reshape(1, D), mod, w)


def _dilated_kernel(q_ref, kc_ref, kp_ref, vtc_ref, vtp_ref, ot_ref, lse_ref):
    i = pl.program_id(1)
    q, kc, kp, vtc, vtp = q_ref[0], kc_ref[0], kp_ref[0], vtc_ref[0], vtp_ref[0]
    T = q.shape[0]
    key = lax.broadcasted_iota(jnp.int32, (T, T), 0)
    qry = lax.broadcasted_iota(jnp.int32, (T, T), 1)
    valid_c = key <= qry
    valid_p = (key >= qry) & (i > 0)
    outs, lses = [], []
    for h in range(HEADS_PER_GROUP_A):
        sl = slice(h * HEAD_DIM_A, (h + 1) * HEAD_DIM_A)
        qh = q[:, sl]
        sc = jnp.where(valid_c, _nt_dot(kc[:, sl], qh), NEG)
        sp = jnp.where(valid_p, _nt_dot(kp[:, sl], qh), NEG)
        m = jnp.maximum(jnp.max(sc, axis=0, keepdims=True), jnp.max(sp, axis=0, keepdims=True))
        pc = jnp.exp(sc - m)
        pp = jnp.exp(sp - m)
        den = jnp.sum(pc, axis=0, keepdims=True) + jnp.sum(pp, axis=0, keepdims=True)
        o = jnp.dot(vtc[sl, :], pc.astype(BF16), preferred_element_type=F32)
        o = o + jnp.dot(vtp[sl, :], pp.astype(BF16), preferred_element_type=F32)
        outs.append(o / den)
        lses.append(m + jnp.log(den))
    ot_ref[0] = jnp.concatenate(outs, axis=0).astype(ot_ref.dtype)
    lse_ref[0] = jnp.concatenate(lses, axis=0)


def dilated_attention(qk, v, batch, seq, dilation):
    L = seq // dilation
    nb = L // Q_BLOCK
    H = HEADS_PER_GROUP_A
    qk_r = qk.reshape(batch, L, dilation, 2 * GROUP_W).transpose(0, 2, 1, 3).reshape(batch * dilation, L, 2 * GROUP_W)
    vt_r = v.reshape(batch, L, dilation, GROUP_W).transpose(0, 2, 3, 1).reshape(batch * dilation, GROUP_W, L)
    row_blk = (1, Q_BLOCK, GROUP_W)
    col_blk = (1, GROUP_W, Q_BLOCK)
    ot, lse = pl.pallas_call(
        _dilated_kernel,
        grid=(batch * dilation, nb),
        in_specs=[
            pl.BlockSpec(row_blk, lambda s, i: (s, i, 0)),
            pl.BlockSpec(row_blk, lambda s, i: (s, i, 1)),
            pl.BlockSpec(row_blk, lambda s, i: (s, jnp.maximum(i - 1, 0), 1)),
            pl.BlockSpec(col_blk, lambda s, i: (s, 0, i)),
            pl.BlockSpec(col_blk, lambda s, i: (s, 0, jnp.maximum(i - 1, 0))),
        ],
        out_specs=[
            pl.BlockSpec(col_blk, lambda s, i: (s, 0, i)),
            pl.BlockSpec((1, H, Q_BLOCK), lambda s, i: (s, 0, i)),
        ],
        out_shape=[
            jax.ShapeDtypeStruct((batch * dilation, GROUP_W, L), BF16),
            jax.ShapeDtypeStruct((batch * dilation, H, L), F32),
        ],
        compiler_params=_cp(("parallel", "parallel")),
        name=f"dilated_attention_d{dilation}",
    )(qk_r, qk_r, qk_r, vt_r, vt_r)
    o = ot.reshape(batch, dilation, GROUP_W, L).transpose(0, 3, 1, 2).reshape(batch * seq, GROUP_W)
    lse = lse.reshape(batch, dilation, H, L).transpose(0, 3, 1, 2).reshape(batch * seq, H)
    return o, jnp.repeat(lse, HEAD_DIM_A, axis=1)


def _mla_prep_kernel(lat_ref, pos_ref, gq_ref, gkv_ref, wq_ref, wk_ref, wvt_ref, freq_ref, exp_ref, one_ref,
                     q_ref, k_ref, vt_ref):
    HP = N_HEADS_C * HEAD_PAD_C
    lat = lat_ref[...].astype(F32)
    cq = lat[:, :Q_LORA]
    ckr = lat[:, Q_LORA:]
    zq = (cq * lax.rsqrt(jnp.mean(cq * cq, axis=-1, keepdims=True) + EPS) * gq_ref[...]).astype(BF16)
    lane = lax.broadcasted_iota(jnp.int32, ckr.shape, 1)
    is_kv = lane < KV_LORA
    ms = jnp.sum(jnp.where(is_kv, ckr * ckr, 0.0), axis=-1, keepdims=True) * (1.0 / KV_LORA)
    zkv = (ckr * jnp.where(is_kv, lax.rsqrt(ms + EPS) * gkv_ref[...], 1.0)).astype(BF16)
    qq = jnp.dot(zq, wq_ref[...], preferred_element_type=F32)
    kk = jnp.dot(zkv, wk_ref[...], preferred_element_type=F32)
    ang = pos_ref[...].astype(F32) * freq_ref[...]
    cos = jnp.dot(jnp.cos(ang), exp_ref[...], preferred_element_type=F32, precision=HIGHEST) + one_ref[...]
    sin = jnp.dot(jnp.sin(ang), exp_ref[...], preferred_element_type=F32, precision=HIGHEST)
    q_ref[...] = (qq[:, :HP] * cos + qq[:, HP:] * sin).astype(q_ref.dtype)
    k_ref[...] = (kk[:, :HP] * cos + kk[:, HP:] * sin).astype(k_ref.dtype)
    vt_ref[0] = _nt_dot(wvt_ref[...], zkv).astype(vt_ref.dtype)


def _mla_weights(cq_g, ckv_g, w_uq, w_ukv):
    H, HPAD, half = N_HEADS_C, HEAD_PAD_C, QK_ROPE // 2
    scale = (QK_NOPE + QK_ROPE) ** -0.5 * math.log2(math.e)
    wq = w_uq.reshape(Q_LORA, H, QK_NOPE + QK_ROPE) * scale
    q_lin = jnp.pad(wq, ((0, 0), (0, 0), (0, HPAD - QK_NOPE - QK_ROPE)))
    r1, r2 = wq[..., QK_NOPE:QK_NOPE + half], wq[..., QK_NOPE + half:]
    q_sw = jnp.concatenate([jnp.zeros((Q_LORA, H, QK_NOPE), F32), -r2, r1,
                            jnp.zeros((Q_LORA, H, HPAD - QK_NOPE - QK_ROPE), F32)], axis=-1)
    wq_big = jnp.concatenate([q_lin.reshape(Q_LORA, H * HPAD), q_sw.reshape(Q_LORA, H * HPAD)], axis=1)

    rows = LAT_W - Q_LORA
    wkv = w_ukv.reshape(KV_LORA, H, QK_NOPE + V_DIM)
    eye = jnp.eye(QK_ROPE, dtype=F32)
    k_lin = jnp.zeros((rows, H, HPAD), F32)
    k_lin = k_lin.at[:KV_LORA, :, :QK_NOPE].set(wkv[..., :QK_NOPE])
    k_lin = k_lin.at[KV_LORA:KV_LORA + QK_ROPE, :, QK_NOPE:QK_NOPE + QK_ROPE].set(
        jnp.broadcast_to(eye[:, None, :], (QK_ROPE, H, QK_ROPE)))
    swap = jnp.zeros((QK_ROPE, QK_ROPE), F32).at[half:, :half].set(-jnp.eye(half)).at[:half, half:].set(jnp.eye(half))
    k_sw = jnp.zeros((rows, H, HPAD), F32)
    k_sw = k_sw.at[KV_LORA:KV_LORA + QK_ROPE, :, QK_NOPE:QK_NOPE + QK_ROPE].set(
        jnp.broadcast_to(swap[:, None, :], (QK_ROPE, H, QK_ROPE)))
    v_w = jnp.zeros((rows, H, V_DIM), F32).at[:KV_LORA].set(wkv[..., QK_NOPE:])
    wk_big = jnp.concatenate([k_lin.reshape(rows, H * HPAD), k_sw.reshape(rows, H * HPAD)], axis=1)
    wv_t = v_w.reshape(rows, H * V_DIM).T

    gkv = jnp.concatenate([ckv_g, jnp.ones((rows - KV_LORA,), F32)]).reshape(1, rows)
    return cq_g.reshape(1, Q_LORA), gkv, wq_big.astype(BF16), wk_big.astype(BF16), wv_t.astype(BF16)


def _rope_tables():
    half = QK_ROPE // 2
    freqs = (ROPE_THETA ** (-jnp.arange(0, QK_ROPE, 2, dtype=F32) / QK_ROPE)).reshape(1, half)
    expand = np.zeros((half, N_HEADS_C, HEAD_PAD_C), np.float32)
    ones = np.zeros((1, N_HEADS_C, HEAD_PAD_C), np.float32)
    for j in range(half):
        expand[j, :, QK_NOPE + j] = 1.0
        expand[j, :, QK_NOPE + half + j] = 1.0
    ones[0, :, :QK_NOPE] = 1.0
    return freqs, jnp.asarray(expand.reshape(half, -1)), jnp.asarray(ones.reshape(1, -1))


def mla_prep(lat, pos_col, gq, gkv, wq_big, wk_big, wv_t, batch, seq):
    N = lat.shape[0]
    HP = N_HEADS_C * HEAD_PAD_C
    tm = 512
    tpb = seq // tm
    freqs, expand, ones = _rope_tables()
    const = lambda shape: pl.BlockSpec(shape, lambda i: (0, 0))
    return pl.pallas_call(
        _mla_prep_kernel,
        grid=(N // tm,),
        in_specs=[
            pl.BlockSpec((tm, LAT_W), lambda i: (i, 0)),
            pl.BlockSpec((tm, 1), lambda i: (i, 0)),
            const(gq.shape), const(gkv.shape), const(wq_big.shape), const(wk_big.shape), const(wv_t.shape),
            const(freqs.shape), const(expand.shape), const(ones.shape),
        ],
        out_specs=[
            pl.BlockSpec((tm, HP), lambda i: (i, 0)),
            pl.BlockSpec((tm, HP), lambda i: (i, 0)),
            pl.BlockSpec((1, DC, tm), lambda i: (i // tpb, 0, i % tpb)),
        ],
        out_shape=[
            jax.ShapeDtypeStruct((N, HP), BF16),
            jax.ShapeDtypeStruct((N, HP), BF16),
            jax.ShapeDtypeStruct((batch, DC, seq), BF16),
        ],
        compiler_params=_cp(("parallel",), VMEM_LIMIT),
        name="mla_prep",
    )(lat, pos_col, gq, gkv, wq_big, wk_big, wv_t, freqs, expand, ones)


HEADS_PER_STEP_C = 2


def _mla_flash_kernel(qi_ref, ki_ref, q_ref, k_ref, vt_ref, o_ref, m_sc, l_sc, acc_sc):
    t = pl.program_id(2)
    qi, ki = qi_ref[t], ki_ref[t]

    @pl.when(ki == 0)
    def _():
        m_sc[...] = jnp.full(m_sc.shape, NEG, F32)
        l_sc[...] = jnp.zeros(l_sc.shape, F32)
        acc_sc[...] = jnp.zeros(acc_sc.shape, F32)

    def step(masked):
        T = q_ref.shape[1]
        if masked:
            key = lax.broadcasted_iota(jnp.int32, (T, T), 0)
            qry = lax.broadcasted_iota(jnp.int32, (T, T), 1)
            keep = key <= qry
        for h in range(HEADS_PER_STEP_C):
            q = q_ref[0, :, h * HEAD_PAD_C:(h + 1) * HEAD_PAD_C]
            k = k_ref[0, :, h * HEAD_PAD_C:(h + 1) * HEAD_PAD_C]
            vt = vt_ref[0, h * V_DIM:(h + 1) * V_DIM, :]
            st = _nt_dot(k, q)
            if masked:
                st = jnp.where(keep, st, NEG)
            m_prev = m_sc[h]
            m_new = jnp.maximum(m_prev, jnp.max(st, axis=0, keepdims=True))
            alpha = jnp.exp2(m_prev - m_new)
            p = jnp.exp2(st - m_new)
            l_sc[h] = alpha * l_sc[h] + jnp.sum(p, axis=0, keepdims=True)
            acc_sc[h] = alpha * acc_sc[h] + jnp.dot(vt, p.astype(BF16), preferred_element_type=F32)
            m_sc[h] = m_new

    @pl.when(ki < qi)
    def _():
        step(False)

    @pl.when(ki == qi)
    def _():
        step(True)
        ot = jnp.concatenate([acc_sc[h] / l_sc[h] for h in range(HEADS_PER_STEP_C)], axis=0)
        o_ref[0] = ot.T.astype(o_ref.dtype)


def mla_attention(q_all, k_all, vt_all, batch, seq):
    T = 512
    nq = seq // T
    pairs = [(a, b) for a in range(nq) for b in range(a + 1)]
    qi_tab = jnp.asarray([p[0] for p in pairs], jnp.int32)
    ki_tab = jnp.asarray([p[1] for p in pairs], jnp.int32)
    hp = N_HEADS_C // HEADS_PER_STEP_C
    qw = HEADS_PER_STEP_C * HEAD_PAD_C
    vw = HEADS_PER_STEP_C * V_DIM
    q3 = q_all.reshape(batch, seq, -1)
    k3 = k_all.reshape(batch, seq, -1)
    grid_spec = pltpu.PrefetchScalarGridSpec(
        num_scalar_prefetch=2,
        grid=(batch, hp, len(pairs)),
        in_specs=[
            pl.BlockSpec((1, T, qw), lambda b, h, t, qi, ki: (b, qi[t], h)),
            pl.BlockSpec((1, T, qw), lambda b, h, t, qi, ki: (b, ki[t], h)),
            pl.BlockSpec((1, vw, T), lambda b, h, t, qi, ki: (b, h, ki[t])),
        ],
        out_specs=pl.BlockSpec((1, T, vw), lambda b, h, t, qi, ki: (b, qi[t], h)),
        scratch_shapes=[
            pltpu.VMEM((HEADS_PER_STEP_C, 1, T), F32),
            pltpu.VMEM((HEADS_PER_STEP_C, 1, T), F32),
            pltpu.VMEM((HEADS_PER_STEP_C, V_DIM, T), F32),
        ],
    )
    o = pl.pallas_call(
        _mla_flash_kernel,
        grid_spec=grid_spec,
        out_shape=jax.ShapeDtypeStruct((batch, seq, DC), BF16),
        compiler_params=_cp(("parallel", "parallel", "arbitrary")),
        name="mla_attention",
    )(qi_tab, ki_tab, q3, k3, vt_all)
    return o.reshape(batch * seq, DC)


def _mixout_kernel(x_ref, gates_ref, ub_ref, ubh_ref, o1_ref, o2_ref, o3_ref, l1_ref, l2_ref, l3_ref, yc_ref,
                   mod1_ref, mod2_ref, g2_ref, poolw_ref, pscale_ref, woa_ref, wob_ref, woc_ref, wout_ref,
                   rwt_ref, sw1_ref, sw3_ref, sw2_ref,
                   xmid_ref, h2a_ref, h2b_ref, logit_ref, *, tiles_per_batch):
    D = x_ref.shape[1]
    tm = x_ref.shape[0]
    tile = pl.program_id(0) % tiles_per_batch

    l1, l2, l3 = l1_ref[...], l2_ref[...], l3_ref[...]
    mx = jnp.maximum(jnp.maximum(l1, l2), l3)
    e1, e2, e3 = jnp.exp(l1 - mx), jnp.exp(l2 - mx), jnp.exp(l3 - mx)
    ya = (e1 * o1_ref[...].astype(F32) + e2 * o2_ref[...].astype(F32) + e3 * o3_ref[...].astype(F32)) / (e1 + e2 + e3)
    a_out = jnp.dot(ya.astype(BF16), woa_ref[...], preferred_element_type=F32)

    u = ub_ref[...].astype(F32)
    halo = jnp.where(tile > 0, ubh_ref[...].astype(F32), 0.0)
    ext = jnp.concatenate([halo, u], axis=0)
    t_seq = tile * tm + lax.broadcasted_iota(jnp.int32, (tm, 1), 0)
    pooled = []
    for gi, w in enumerate(POOL_WINDOWS):
        sl = slice(gi * POOL_GROUP_DIM, (gi + 1) * POOL_GROUP_DIM)
        acc = ext[:, sl]
        k = 1
        while k < w:
            acc = acc + pltpu.roll(acc, k, axis=0)
            k *= 2
        cnt = jnp.minimum(t_seq + 1, w).astype(F32)
        pg = acc[POOL_HALO:] / cnt - u[:, sl]
        pooled.append(jnp.dot(pg.astype(BF16), poolw_ref[gi], preferred_element_type=F32))
    yb = jnp.concatenate(pooled, axis=1) * pscale_ref[...]
    b_out = jnp.dot(yb.astype(BF16), wob_ref[...], preferred_element_type=F32)
    c_out = jnp.dot(yc_ref[...], woc_ref[...], preferred_element_type=F32)

    g = gates_ref[...].astype(F32)
    mix = (jax.nn.sigmoid(g[:, :D]) * a_out + jax.nn.sigmoid(g[:, D:2 * D]) * b_out
           + jax.nn.sigmoid(g[:, 2 * D:]) * c_out)
    tok = jnp.dot(mix.astype(BF16), wout_ref[...], preferred_element_type=F32)
    xn = x_ref[...] + mod1_ref[0][:, 2 * D:] * tok

    mod2 = mod2_ref[0]
    y = xn * lax.rsqrt(jnp.mean(xn * xn, axis=-1, keepdims=True) + EPS) * g2_ref[...]
    h2 = y * (1.0 + mod2[:, D:2 * D]) + mod2[:, :D]
    h2b = h2.astype(BF16)
    h2a_ref[...], h2b_ref[...] = _pack_row_halves(h2b)
    logit_ref[...] = _nt_dot(rwt_ref[...], h2b)
    hid = _silu(jnp.dot(h2b, sw1_ref[...], preferred_element_type=F32)) * jnp.dot(
        h2b, sw3_ref[...], preferred_element_type=F32)
    shared = jnp.dot(hid.astype(BF16), sw2_ref[...], preferred_element_type=F32)
    xmid_ref[...] = xn + mod2[:, 2 * D:] * shared


def mix_out(x2, gu, dil, yc, mod1, mod2, g2, pool_w, pool_scale, w_oa, w_ob, w_oc, w_out, rwt, sw1, sw3, sw2, seq):
    N, D = x2.shape
    tm = 256
    tpb = seq // tm
    (o1, l1), (o2, l2), (o3, l3) = dil
    row = lambda w, c=0: pl.BlockSpec((tm, w), lambda i: (i, c))
    const2 = lambda a: pl.BlockSpec(a.shape, lambda i: (0,) * a.ndim)
    modspec = pl.BlockSpec((1, 1, 3 * D), lambda i: (i // tpb, 0, 0))
    ub_col = 3 * D // DB
    halo_spec = pl.BlockSpec(
        (POOL_HALO, DB), lambda i: (jnp.maximum(i * (tm // POOL_HALO) - 1, 0), ub_col))
    weights = [g2.reshape(1, D), pool_w, pool_scale.reshape(1, DB), w_oa, w_ob, w_oc, w_out, rwt, sw1, sw3, sw2]
    return pl.pallas_call(
        functools.partial(_mixout_kernel, tiles_per_batch=tpb),
        grid=(N // tm,),
        in_specs=[
            row(D), row(3 * D), row(DB, ub_col), halo_spec,
            row(GROUP_W), row(GROUP_W), row(GROUP_W), row(GROUP_W), row(GROUP_W), row(GROUP_W), row(DC),
            modspec, modspec,
        ] + [const2(a) for a in weights],
        out_specs=[row(D), row(PACK_W), row(PACK_W), pl.BlockSpec((N_EXPERTS, tm), lambda i: (0, i))],
        out_shape=[
            jax.ShapeDtypeStruct((N, D), F32),
            jax.ShapeDtypeStruct((N, PACK_W), jnp.int32),
            jax.ShapeDtypeStruct((N, PACK_W), jnp.int32),
            jax.ShapeDtypeStruct((N_EXPERTS, N), F32),
        ],
        compiler_params=_cp(("parallel",), VMEM_LIMIT),
        name="mix_out",
    )(x2, gu, gu, gu, o1, o2, o3, l1, l2, l3, yc, mod1, mod2, *weights)


def _route_kernel(lg_ref, bias_ref, tri_ref, dest_ref, w_ref, cnt_ref, run_sc, start_sc, *, slot_block):
    G, GS = N_GROUPS, GROUP_SIZE
    scores = jax.nn.sigmoid(lg_ref[...])
    sel = scores + bias_ref[...]
    tn = sel.shape[1]
    eio = lax.broadcasted_iota(jnp.int32, (GS, tn), 0)
    ninf = -jnp.inf

    gs = []
    for g in range(G):
        v = sel[g * GS:(g + 1) * GS]
        m1 = jnp.max(v, axis=0, keepdims=True)
        i1 = jnp.min(jnp.where(v == m1, eio, GS), axis=0, keepdims=True)
        m2 = jnp.max(jnp.where(eio == i1, ninf, v), axis=0, keepdims=True)
        gs.append(m1 + m2)
    gsm = jnp.concatenate(gs, axis=0)
    gio = lax.broadcasted_iota(jnp.int32, (G, tn), 0)
    rank = jnp.zeros((G, tn), jnp.int32)
    for g2 in range(G):
        beats = (gs[g2] > gsm) | ((gs[g2] == gsm) & (g2 < gio))
        rank = rank + beats.astype(jnp.int32)
    gsel = rank < TOPK_GROUPS

    vs = [jnp.where(gsel[g:g + 1], sel[g * GS:(g + 1) * GS], NEG) for g in range(G)]
    eid = [eio + g * GS for g in range(G)]
    chosen = [jnp.zeros((GS, tn), jnp.bool_) for _ in range(G)]
    picks = []
    for _ in range(TOP_K):
        m = functools.reduce(jnp.maximum, [jnp.max(v, axis=0, keepdims=True) for v in vs])
        idx = functools.reduce(jnp.minimum, [
            jnp.min(jnp.where(v == m, e, N_EXPERTS), axis=0, keepdims=True) for v, e in zip(vs, eid)])
        picks.append(idx)
        for g in range(G):
            hit = eid[g] == idx
            chosen[g] = chosen[g] | hit
            vs[g] = jnp.where(hit, ninf, vs[g])
    mask = jnp.concatenate(chosen, axis=0).astype(F32)
    tile_counts = jnp.sum(mask, axis=1, keepdims=True)

    def pick_rows(table):
        rows = []
        for idx in picks:
            parts = [jnp.sum(jnp.where(eid[g] == idx, table[g * GS:(g + 1) * GS], 0.0), axis=0, keepdims=True)
                     for g in range(G)]
            rows.append(functools.reduce(jnp.add, parts))
        return jnp.concatenate(rows, axis=0)

    wk = pick_rows(scores)
    w_ref[0] = wk / jnp.sum(wk, axis=0, keepdims=True) * ROUTED_SCALE

    phase = pl.program_id(0)
    step = pl.program_id(1)

    @pl.when((phase == 0) & (step == 0))
    def _():
        run_sc[...] = jnp.zeros(run_sc.shape, F32)
        start_sc[...] = jnp.zeros(start_sc.shape, F32)

    @pl.when((phase == 1) & (step == 0))
    def _():
        counts = run_sc[...].astype(jnp.int32)
        cnt_ref[...] = jnp.broadcast_to(counts, cnt_ref.shape)
        shift = slot_block.bit_length() - 1
        padded = lax.shift_left(lax.shift_right_logical(counts + (slot_block - 1), shift), shift).astype(F32)
        r = lax.broadcasted_iota(jnp.int32, (N_EXPERTS, N_EXPERTS), 0)
        c = lax.broadcasted_iota(jnp.int32, (N_EXPERTS, N_EXPERTS), 1)
        as_row = jnp.sum(jnp.where(r == c, padded, 0.0), axis=0, keepdims=True)
        start_sc[...] = jnp.sum(jnp.where(c < r, as_row, 0.0), axis=1, keepdims=True)
        run_sc[...] = jnp.zeros(run_sc.shape, F32)

    before = jnp.dot(mask.astype(BF16), tri_ref[...], preferred_element_type=F32) - mask
    slot = start_sc[...] + run_sc[...] + before
    dest_ref[0] = pick_rows(slot).astype(jnp.int32)
    run_sc[...] = run_sc[...] + tile_counts


SLOT_BLOCK = 512


def route(logits_t, bias):
    E, N = logits_t.shape
    tn = 1024
    tri = (jnp.arange(tn)[:, None] <= jnp.arange(tn)[None, :]).astype(BF16)
    tile = lambda r: pl.BlockSpec((r, tn), lambda p, i: (0, i))
    plane = lambda: pl.BlockSpec((1, TOP_K, tn), lambda p, i: (p, 0, i))
    dest, w, cnt = pl.pallas_call(
        functools.partial(_route_kernel, slot_block=SLOT_BLOCK),
        grid=(2, N // tn),
        in_specs=[tile(E), pl.BlockSpec((E, 1), lambda p, i: (0, 0)), pl.BlockSpec((tn, tn), lambda p, i: (0, 0))],
        out_specs=[plane(), plane(), pl.BlockSpec((E, 128), lambda p, i: (0, 0))],
        out_shape=[
            jax.ShapeDtypeStruct((2, TOP_K, N), jnp.int32),
            jax.ShapeDtypeStruct((2, TOP_K, N), F32),
            jax.ShapeDtypeStruct((E, 128), jnp.int32),
        ],
        scratch_shapes=[pltpu.VMEM((E, 1), F32), pltpu.VMEM((E, 1), F32)],
        compiler_params=_cp(("arbitrary", "arbitrary")),
        name="route",
    )(logits_t, bias.reshape(E, 1), tri)
    return dest[1], w[1], cnt[:, 0]


def block_tables(counts, n_tokens):
    E = counts.shape[0]
    blk = SLOT_BLOCK
    nblk = (n_tokens * TOP_K + E * blk) // blk
    per_expert = (counts + blk - 1) // blk
    bend = jnp.cumsum(per_expert)
    b = jnp.arange(nblk, dtype=jnp.int32)
    blk_e = jnp.minimum(jnp.searchsorted(bend, b, side='right'), E - 1).astype(jnp.int32)
    within = b - (bend - per_expert)[blk_e]
    nvalid = jnp.where(b < bend[-1], jnp.clip(counts[blk_e] - within * blk, 0, blk), 0)
    return blk_e, nvalid.astype(jnp.int32)


def _sc_mesh():
    return plsc.VectorSubcoreMesh(core_axis_name="c", subcore_axis_name="s")


SC_WINDOW = 128


def sc_scatter_rows(x, dest, n_slots):
    N, W = x.shape
    K = dest.shape[0]

    @functools.partial(pl.kernel, out_type=jax.ShapeDtypeStruct((n_slots, W), x.dtype), mesh=_sc_mesh(),
                       scratch_types=[])
    def scatter(x_hbm, i_hbm, o_hbm):
        def body(x_vmem, i_vmem):
            for k in range(K):
                pltpu.sync_copy(x_vmem, o_hbm.at[i_vmem.at[k]])

        pltpu.emit_pipeline(
            body,
            grid=(N // SC_WINDOW,),
            in_specs=[pl.BlockSpec((SC_WINDOW, W), lambda i: (i, 0)),
                      pl.BlockSpec((K, SC_WINDOW), lambda i: (0, i))],
            out_specs=[],
            core_axis_name=("c", "s"),
            dimension_semantics=(pltpu.PARALLEL,),
        )(x_hbm, i_hbm)

    return scatter(x, dest)


def sc_gather_rows(y, dest):
    W = y.shape[1]
    K, N = dest.shape

    @functools.partial(pl.kernel, out_type=jax.ShapeDtypeStruct((K, N, W), y.dtype), mesh=_sc_mesh(),
                       scratch_types=[])
    def gather(y_hbm, i_hbm, o_hbm):
        def body(i_vmem, o_vmem):
            pltpu.sync_copy(y_hbm.at[i_vmem.at[0, 0]], o_vmem.at[0])

        pltpu.emit_pipeline(
            body,
            grid=(K, N // SC_WINDOW),
            in_specs=[pl.BlockSpec((1, 1, SC_WINDOW), lambda k, i: (k, 0, i))],
            out_specs=[pl.BlockSpec((1, SC_WINDOW, W), lambda k, i: (k, i, 0))],
            core_axis_name=("c", "s"),
            dimension_semantics=(pltpu.PARALLEL, pltpu.PARALLEL),
        )(i_hbm, o_hbm)

    return gather(y, dest.reshape(K, 1, N))


def _expert_kernel(blk_e_ref, nvalid_ref, xa_ref, xb_ref, w1_ref, w3_ref, w2_ref, ya_ref, yb_ref,
                   w1_sc, w3_sc, w2_sc):
    b = pl.program_id(0)
    nv = nvalid_ref[b]
    prev_e = blk_e_ref[jnp.maximum(b - 1, 0)]

    @pl.when((b == 0) | (blk_e_ref[b] != prev_e))
    def _():
        w1_sc[...] = w1_ref[0].astype(BF16)
        w3_sc[...] = w3_ref[0].astype(BF16)
        w2_sc[...] = w2_ref[0].astype(BF16)

    @pl.when(nv > 0)
    def _():
        x = _unpack_row_halves(xa_ref[...], xb_ref[...])
        rows = lax.broadcasted_iota(jnp.int32, x.shape, 0)
        x = jnp.where(rows < nv, x, 0.0).astype(BF16)
        hid = _silu(jnp.dot(x, w1_sc[...], preferred_element_type=F32)) * jnp.dot(
            x, w3_sc[...], preferred_element_type=F32)
        y = jnp.dot(hid.astype(BF16), w2_sc[...], preferred_element_type=F32)
        ya_ref[...], yb_ref[...] = _pack_row_halves(y)

    @pl.when(nv == 0)
    def _():
        ya_ref[...] = jnp.zeros(ya_ref.shape, ya_ref.dtype)
        yb_ref[...] = jnp.zeros(yb_ref.shape, yb_ref.dtype)


def routed_experts(xa, xb, blk_e, nvalid, w1, w3, w2):
    P = xa.shape[0]
    blk = SLOT_BLOCK
    E, D, FF = w1.shape
    slots = lambda: pl.BlockSpec((blk, PACK_W), lambda b, be, nv: (b, 0))
    grid_spec = pltpu.PrefetchScalarGridSpec(
        num_scalar_prefetch=2,
        grid=(P // blk,),
        in_specs=[
            slots(), slots(),
            pl.BlockSpec((1, D, FF), lambda b, be, nv: (be[b], 0, 0)),
            pl.BlockSpec((1, D, FF), lambda b, be, nv: (be[b], 0, 0)),
            pl.BlockSpec((1, FF, D), lambda b, be, nv: (be[b], 0, 0)),
        ],
        out_specs=[slots(), slots()],
        scratch_shapes=[pltpu.VMEM((D, FF), BF16), pltpu.VMEM((D, FF), BF16), pltpu.VMEM((FF, D), BF16)],
    )
    return pl.pallas_call(
        _expert_kernel,
        grid_spec=grid_spec,
        out_shape=[jax.ShapeDtypeStruct((P, PACK_W), jnp.int32)] * 2,
        compiler_params=_cp(("arbitrary",), VMEM_LIMIT),
        name="routed_experts",
    )(blk_e, nvalid, xa, xb, w1, w3, w2)


def _combine_kernel(xmid_ref, oa_ref, ob_ref, w_ref, mod2_ref, fg_ref, out_ref, *, final):
    D = xmid_ref.shape[1]
    w = w_ref[...]
    acc = w[:, 0:1] * _unpack_row_halves(oa_ref[0], ob_ref[0])
    for k in range(1, TOP_K):
        acc = acc + w[:, k:k + 1] * _unpack_row_halves(oa_ref[k], ob_ref[k])
    x = xmid_ref[...] + mod2_ref[0][:, 2 * D:] * acc
    if final:
        x = x * lax.rsqrt(jnp.mean(x * x, axis=-1, keepdims=True) + EPS) * fg_ref[...]
    out_ref[...] = x


def combine(xmid, oa, ob, w_tok, mod2, final_g, seq, final):
    N, D = xmid.shape
    tm = 256
    tpb = seq // tm
    rows8 = lambda: pl.BlockSpec((TOP_K, tm, PACK_W), lambda i: (0, i, 0))
    return pl.pallas_call(
        functools.partial(_combine_kernel, final=final),
        grid=(N // tm,),
        in_specs=[
            pl.BlockSpec((tm, D), lambda i: (i, 0)),
            rows8(), rows8(),
            pl.BlockSpec((tm, TOP_K), lambda i: (i, 0)),
            pl.BlockSpec((1, 1, 3 * D), lambda i: (i // tpb, 0, 0)),
            pl.BlockSpec((1, D), lambda i: (0, 0)),
        ],
        out_specs=pl.BlockSpec((tm, D), lambda i: (i, 0)),
        out_shape=jax.ShapeDtypeStruct((N, D), F32),
        compiler_params=_cp(("parallel",), VMEM_LIMIT),
        name="combine",
    )(xmid, oa, ob, w_tok, mod2, final_g.reshape(1, D))


def _permute_w_in(w):
    ub = w[:, 3 * DA:3 * DA + DB]
    lat_lo = 3 * DA + DB
    lat_hi = lat_lo + Q_LORA + KV_LORA + QK_ROPE
    lat, gates = w[:, lat_lo:lat_hi], w[:, lat_hi:]
    pad = jnp.zeros((w.shape[0], LAT_W - (lat_hi - lat_lo)), w.dtype)
    parts = [gates, ub, lat, pad]
    for g in range(len(DIL_GROUPS)):
        sl = slice(g * GROUP_W, (g + 1) * GROUP_W)
        parts += [w[:, :DA][:, sl] * (HEAD_DIM_A ** -0.5), w[:, DA:2 * DA][:, sl], w[:, 2 * DA:3 * DA][:, sl]]
    return jnp.concatenate(parts, axis=1).astype(BF16)


def kernel(x, c, positions, ada_mix_w, ada_mix_b, norm_mix_g, w_in, pool_w, pool_scale, cq_norm_g, ckv_norm_g, w_uq, w_ukv, w_oa, w_ob, w_oc, w_out, ada_ffn_w, ada_ffn_b, norm_ffn_g, router_w, router_bias, exp_w1, exp_w3, exp_w2, sh_w1, sh_w3, sh_w2, final_g):
    B, S, D = x.shape
    depth = w_in.shape[0]
    N = B * S
    mod_mix = adaln_rows(c, ada_mix_w, ada_mix_b)
    mod_ffn = adaln_rows(c, ada_ffn_w, ada_ffn_b)
    pos_col = positions.reshape(N, 1)
    x2 = x.reshape(N, D)
    for l in range(depth):
        mod1 = mod_mix[l].reshape(B, 1, 3 * D)
        mod2 = mod_ffn[l].reshape(B, 1, 3 * D)
        gu, lat, *qkv = in_projection(x2, norm_mix_g[l], mod1, _permute_w_in(w_in[l]), S)
        dil = [dilated_attention(qkv[2 * g], qkv[2 * g + 1], B, S, d) for g, (_, d) in enumerate(DIL_GROUPS)]
        mla_w = _mla_weights(cq_norm_g[l], ckv_norm_g[l], w_uq[l], w_ukv[l])
        q_all, k_all, vt_all = mla_prep(lat, pos_col, *mla_w, B, S)
        yc = mla_attention(q_all, k_all, vt_all, B, S)
        xmid, h2a, h2b, logits_t = mix_out(
            x2, gu, dil, yc, mod1, mod2, norm_ffn_g[l], pool_w[l].astype(BF16), pool_scale[l],
            w_oa[l].astype(BF16), w_ob[l].astype(BF16), w_oc[l].astype(BF16), w_out[l].astype(BF16),
            router_w[l].T.astype(BF16), sh_w1[l].astype(BF16), sh_w3[l].astype(BF16), sh_w2[l].astype(BF16), S)
        dest, w_k, counts = route(logits_t, router_bias[l])
        blk_e, nvalid = block_tables(counts, N)
        n_slots = blk_e.shape[0] * SLOT_BLOCK
        xa = sc_scatter_rows(h2a, dest, n_slots)
        xb = sc_scatter_rows(h2b, dest, n_slots)
        ya, yb = routed_experts(xa, xb, blk_e, nvalid, exp_w1[l], exp_w3[l], exp_w2[l])
        oa = sc_gather_rows(ya, dest)
        ob = sc_gather_rows(yb, dest)
        x2 = combine(xmid, oa, ob, w_k.T, mod2, final_g, S, final=(l == depth - 1))
    return x2.reshape(B, S, D)
```

```python
import functools
import math

import jax
import jax.numpy as jnp
from jax import lax
from jax.experimental import pallas as pl
from jax.experimental.pallas import tpu as pltpu
from jax.experimental.pallas import tpu_sc as plsc

F32 = jnp.float32
BF16 = jnp.bfloat16
HIGHEST = lax.Precision.HIGHEST

D_MODEL = 1024
HEAD_DIM_A = 64
HEADS_PER_GROUP_A = 4
DIL_GROUPS = ((128, 1), (512, 4), (2048, 16))
GROUP_W = HEADS_PER_GROUP_A * HEAD_DIM_A
DA = GROUP_W * len(DIL_GROUPS)
POOL_WINDOWS = (2, 4, 8, 16)
POOL_GROUP_DIM = 128
DB = POOL_GROUP_DIM * len(POOL_WINDOWS)
POOL_HALO = 16
N_HEADS_C = 8
QK_NOPE = 64
QK_ROPE = 32
V_DIM = 64
Q_LORA = 384
KV_LORA = 256
DC = N_HEADS_C * V_DIM
HEAD_PAD_C = 128
ROPE_THETA = 10000.0
N_EXPERTS = 64
TOP_K = 8
N_GROUPS = 8
TOPK_GROUPS = 4
GROUP_SIZE = N_EXPERTS // N_GROUPS
EXPERT_FF = 256
ROUTED_SCALE = 2.5
EPS = 1e-6
NEG = -1e30
Q_BLOCK = 128

LAT_W = 768
GU_W = 3 * D_MODEL + DB
IN_OUT_WIDTHS = (GU_W, LAT_W) + (2 * GROUP_W, GROUP_W) * len(DIL_GROUPS)

VMEM_LIMIT = 56 * 1024 * 1024


def _cp(sem, vmem=None):
    return pltpu.CompilerParams(dimension_semantics=sem, vmem_limit_bytes=vmem)


def _silu(v):
    return v * jax.nn.sigmoid(v)


def _nt_dot(a, b):
    return lax.dot_general(a, b, (((1,), (1,)), ((), ())), preferred_element_type=F32)


PACK_W = D_MODEL // 4
_HI_MASK = -65536


def _bf16_bits(v):
    return lax.bitcast_convert_type(v.astype(BF16).astype(F32), jnp.int32)


def _pack_row_halves(v):
    halves = []
    for h in range(2):
        lo = _bf16_bits(v[:, (2 * h) * PACK_W:(2 * h + 1) * PACK_W])
        hi = _bf16_bits(v[:, (2 * h + 1) * PACK_W:(2 * h + 2) * PACK_W])
        halves.append(lax.shift_right_logical(lo, 16) | (hi & _HI_MASK))
    return halves


def _unpack_row_halves(wa, wb):
    parts = []
    for w in (wa, wb):
        parts.append(lax.bitcast_convert_type(lax.shift_left(w, 16), F32))
        parts.append(lax.bitcast_convert_type(w & _HI_MASK, F32))
    return jnp.concatenate(parts, axis=1)


def _adaln_kernel(c_ref, w_ref, b_ref, o_ref):
    s = _silu(c_ref[...])
    o_ref[0] = jnp.dot(s, w_ref[0], preferred_element_type=F32, precision=HIGHEST) + b_ref[0]


def adaln_rows(c, w, b):
    L, D, D3 = w.shape
    B = c.shape[0]
    tn = 1024
    return pl.pallas_call(
        _adaln_kernel,
        grid=(L, D3 // tn),
        in_specs=[
            pl.BlockSpec((B, D), lambda l, j: (0, 0)),
            pl.BlockSpec((1, D, tn), lambda l, j: (l, 0, j)),
            pl.BlockSpec((1, 1, tn), lambda l, j: (l, 0, j)),
        ],
        out_specs=pl.BlockSpec((1, B, tn), lambda l, j: (l, 0, j)),
        out_shape=jax.ShapeDtypeStruct((L, B, D3), F32),
        compiler_params=_cp(("parallel", "parallel")),
        name="adaln_rows",
    )(c, w, b.reshape(L, 1, D3))


def _inproj_kernel(x_ref, g_ref, mod_ref, w_ref, *o_refs, chunk):
    D = x_ref.shape[1]
    x = x_ref[...]
    y = x * lax.rsqrt(jnp.mean(x * x, axis=-1, keepdims=True) + EPS) * g_ref[...]
    mod = mod_ref[0]
    h = (y * (1.0 + mod[:, D:2 * D]) + mod[:, :D]).astype(BF16)
    col = 0
    for o_ref in o_refs:
        width = o_ref.shape[1]
        for c0 in range(0, width, chunk):
            cw = min(chunk, width - c0)
            o_ref[:, c0:c0 + cw] = jnp.dot(
                h, w_ref[:, col + c0:col + c0 + cw], preferred_element_type=F32).astype(o_ref.dtype)
        col += width


def in_projection(x2, g, mod, w, seq):
    N, D = x2.shape
    tm = 512
    tpb = seq // tm
    return pl.pallas_call(
        functools.partial(_inproj_kernel, chunk=512),
        grid=(N // tm,),
        in_specs=[
            pl.BlockSpec((tm, D), lambda i: (i, 0)),
            pl.BlockSpec((1, D), lambda i: (0, 0)),
            pl.BlockSpec((1, 1, 3 * D), lambda i: (i // tpb, 0, 0)),
            pl.BlockSpec(w.shape, lambda i: (0, 0), pipeline_mode=pl.Buffered(1)),
        ],
        out_specs=[pl.BlockSpec((tm, wd), lambda i: (i, 0)) for wd in IN_OUT_WIDTHS],
        out_shape=[jax.ShapeDtypeStruct((N, wd), BF16) for wd in IN_OUT_WIDTHS],
        compiler_params=_cp(("parallel",), VMEM_LIMIT),
        name="in_projection",
    )(x2, g.reshape(1, D), mod, w)


def _dilated_kernel(q_ref, kc_ref, kp_ref, vtc_ref, vtp_ref, ot_ref, lse_ref):
    i = pl.program_id(1)
    T = Q_BLOCK
    key = lax.broadcasted_iota(jnp.int32, (T, T), 0)
    qry = lax.broadcasted_iota(jnp.int32, (T, T), 1)
    valid_c = key <= qry
    near = key >= qry
    for j in range(q_ref.shape[1] // T):
        rows = slice(j * T, (j + 1) * T)
        q, kc, vtc = q_ref[0, rows, :], kc_ref[0, rows, :], vtc_ref[0, :, rows]
        if j == 0:
            kp, vtp, valid_p = kp_ref[0], vtp_ref[0], near & (i > 0)
        else:
            before = slice((j - 1) * T, j * T)
            kp, vtp, valid_p = kc_ref[0, before, :], vtc_ref[0, :, before], near
        outs, lses = [], []
        for h in range(HEADS_PER_GROUP_A):
            sl = slice(h * HEAD_DIM_A, (h + 1) * HEAD_DIM_A)
            qh = q[:, sl]
            sc = jnp.where(valid_c, _nt_dot(kc[:, sl], qh), NEG)
            sp = jnp.where(valid_p, _nt_dot(kp[:, sl], qh), NEG)
            m = jnp.maximum(jnp.max(sc, axis=0, keepdims=True), jnp.max(sp, axis=0, keepdims=True))
            pc = jnp.exp(sc - m)
            pp = jnp.exp(sp - m)
            den = jnp.sum(pc, axis=0, keepdims=True) + jnp.sum(pp, axis=0, keepdims=True)
            o = jnp.dot(vtc[sl, :], pc.astype(BF16), preferred_element_type=F32)
            o = o + jnp.dot(vtp[sl, :], pp.astype(BF16), preferred_element_type=F32)
            outs.append(o / den)
            lses.append(m + jnp.log(den))
        ot_ref[0, :, rows] = jnp.concatenate(outs, axis=0).astype(ot_ref.dtype)
        lse_ref[0, :, rows] = jnp.concatenate(lses, axis=0)


DILATED_RUN = 4


def dilated_attention(qk, v, batch, seq, dilation):
    L = seq // dilation
    nb = L // Q_BLOCK
    run = min(DILATED_RUN, nb)
    H = HEADS_PER_GROUP_A
    qk_r = qk.reshape(batch, L, dilation, 2 * GROUP_W).transpose(0, 2, 1, 3).reshape(batch * dilation, L, 2 * GROUP_W)
    vt_r = v.reshape(batch, L, dilation, GROUP_W).transpose(0, 2, 3, 1).reshape(batch * dilation, GROUP_W, L)
    before = lambda i: jnp.maximum(i * run - 1, 0)
    ot, lse = pl.pallas_call(
        _dilated_kernel,
        grid=(batch * dilation, nb // run),
        in_specs=[
            pl.BlockSpec((1, run * Q_BLOCK, GROUP_W), lambda s, i: (s, i, 0)),
            pl.BlockSpec((1, run * Q_BLOCK, GROUP_W), lambda s, i: (s, i, 1)),
            pl.BlockSpec((1, Q_BLOCK, GROUP_W), lambda s, i: (s, before(i), 1)),
            pl.BlockSpec((1, GROUP_W, run * Q_BLOCK), lambda s, i: (s, 0, i)),
            pl.BlockSpec((1, GROUP_W, Q_BLOCK), lambda s, i: (s, 0, before(i))),
        ],
        out_specs=[
            pl.BlockSpec((1, GROUP_W, run * Q_BLOCK), lambda s, i: (s, 0, i)),
            pl.BlockSpec((1, H, run * Q_BLOCK), lambda s, i: (s, 0, i)),
        ],
        out_shape=[
            jax.ShapeDtypeStruct((batch * dilation, GROUP_W, L), BF16),
            jax.ShapeDtypeStruct((batch * dilation, H, L), F32),
        ],
        compiler_params=_cp(("parallel", "parallel")),
        name=f"dilated_attention_d{dilation}",
    )(qk_r, qk_r, qk_r, vt_r, vt_r)
    o = ot.reshape(batch, dilation, GROUP_W, L).transpose(0, 3, 1, 2).reshape(batch * seq, GROUP_W)
    lse = lse.reshape(batch, dilation, H, L).transpose(0, 3, 1, 2).reshape(batch * seq, H)
    return o, jnp.repeat(lse, HEAD_DIM_A, axis=1)


def _mla_prep_kernel(lat_ref, pos_ref, gq_ref, gkv_ref, wq_ref, wk_ref, wvt_ref, freq_ref,
                     q_ref, k_ref, vt_ref):
    HP = N_HEADS_C * HEAD_PAD_C
    lat = lat_ref[...].astype(F32)
    cq = lat[:, :Q_LORA]
    ckr = lat[:, Q_LORA:]
    zq = (cq * lax.rsqrt(jnp.mean(cq * cq, axis=-1, keepdims=True) + EPS) * gq_ref[...]).astype(BF16)
    lane = lax.broadcasted_iota(jnp.int32, ckr.shape, 1)
    is_kv = lane < KV_LORA
    ms = jnp.sum(jnp.where(is_kv, ckr * ckr, 0.0), axis=-1, keepdims=True) * (1.0 / KV_LORA)
    zkv = (ckr * jnp.where(is_kv, lax.rsqrt(ms + EPS) * gkv_ref[...], 1.0)).astype(BF16)
    qq = jnp.dot(zq, wq_ref[...], preferred_element_type=F32)
    kk = jnp.dot(zkv, wk_ref[...], preferred_element_type=F32)
    ang = pos_ref[...].astype(F32) * freq_ref[...]
    cos, sin = jnp.cos(ang), jnp.sin(ang)
    for h in range(N_HEADS_C):
        lo, hi = h * HEAD_PAD_C, (h + 1) * HEAD_PAD_C
        q_ref[:, lo:hi] = (qq[:, lo:hi] * cos + qq[:, HP + lo:HP + hi] * sin).astype(q_ref.dtype)
        k_ref[:, lo:hi] = (kk[:, lo:hi] * cos + kk[:, HP + lo:HP + hi] * sin).astype(k_ref.dtype)
    vt_ref[0] = _nt_dot(wvt_ref[...], zkv).astype(vt_ref.dtype)


def _mla_weights(cq_g, ckv_g, w_uq, w_ukv):
    H, HPAD, half = N_HEADS_C, HEAD_PAD_C, QK_ROPE // 2
    scale = (QK_NOPE + QK_ROPE) ** -0.5 * math.log2(math.e)
    wq = w_uq.reshape(Q_LORA, H, QK_NOPE + QK_ROPE) * scale
    q_lin = jnp.pad(wq, ((0, 0), (0, 0), (0, HPAD - QK_NOPE - QK_ROPE)))
    r1, r2 = wq[..., QK_NOPE:QK_NOPE + half], wq[..., QK_NOPE + half:]
    q_sw = jnp.concatenate([jnp.zeros((Q_LORA, H, QK_NOPE), F32), -r2, r1,
                            jnp.zeros((Q_LORA, H, HPAD - QK_NOPE - QK_ROPE), F32)], axis=-1)
    wq_big = jnp.concatenate([q_lin.reshape(Q_LORA, H * HPAD), q_sw.reshape(Q_LORA, H * HPAD)], axis=1)

    rows = LAT_W - Q_LORA
    wkv = w_ukv.reshape(KV_LORA, H, QK_NOPE + V_DIM)
    eye = jnp.eye(QK_ROPE, dtype=F32)
    k_lin = jnp.zeros((rows, H, HPAD), F32)
    k_lin = k_lin.at[:KV_LORA, :, :QK_NOPE].set(wkv[..., :QK_NOPE])
    k_lin = k_lin.at[KV_LORA:KV_LORA + QK_ROPE, :, QK_NOPE:QK_NOPE + QK_ROPE].set(
        jnp.broadcast_to(eye[:, None, :], (QK_ROPE, H, QK_ROPE)))
    swap = jnp.zeros((QK_ROPE, QK_ROPE), F32).at[half:, :half].set(-jnp.eye(half)).at[:half, half:].set(jnp.eye(half))
    k_sw = jnp.zeros((rows, H, HPAD), F32)
    k_sw = k_sw.at[KV_LORA:KV_LORA + QK_ROPE, :, QK_NOPE:QK_NOPE + QK_ROPE].set(
        jnp.broadcast_to(swap[:, None, :], (QK_ROPE, H, QK_ROPE)))
    v_w = jnp.zeros((rows, H, V_DIM), F32).at[:KV_LORA].set(wkv[..., QK_NOPE:])
    wk_big = jnp.concatenate([k_lin.reshape(rows, H * HPAD), k_sw.reshape(rows, H * HPAD)], axis=1)
    wv_t = v_w.reshape(rows, H * V_DIM).T

    gkv = jnp.concatenate([ckv_g, jnp.ones((rows - KV_LORA,), F32)]).reshape(1, rows)
    return cq_g.reshape(1, Q_LORA), gkv, wq_big.astype(BF16), wk_big.astype(BF16), wv_t.astype(BF16)


def _rope_lane_freqs():
    half = QK_ROPE // 2
    freqs = ROPE_THETA ** (-jnp.arange(0, QK_ROPE, 2, dtype=F32) / QK_ROPE)
    row = jnp.zeros((HEAD_PAD_C,), F32)
    row = row.at[QK_NOPE:QK_NOPE + half].set(freqs).at[QK_NOPE + half:QK_NOPE + QK_ROPE].set(freqs)
    return row.reshape(1, HEAD_PAD_C)


def mla_prep(lat, pos_col, gq, gkv, wq_big, wk_big, wv_t, batch, seq):
    N = lat.shape[0]
    HP = N_HEADS_C * HEAD_PAD_C
    tm = 512
    tpb = seq // tm
    freqs = _rope_lane_freqs()
    const = lambda shape: pl.BlockSpec(shape, lambda i: (0, 0))
    return pl.pallas_call(
        _mla_prep_kernel,
        grid=(N // tm,),
        in_specs=[
            pl.BlockSpec((tm, LAT_W), lambda i: (i, 0)),
            pl.BlockSpec((tm, 1), lambda i: (i, 0)),
            const(gq.shape), const(gkv.shape), const(wq_big.shape), const(wk_big.shape), const(wv_t.shape),
            const(freqs.shape),
        ],
        out_specs=[
            pl.BlockSpec((tm, HP), lambda i: (i, 0)),
            pl.BlockSpec((tm, HP), lambda i: (i, 0)),
            pl.BlockSpec((1, DC, tm), lambda i: (i // tpb, 0, i % tpb)),
        ],
        out_shape=[
            jax.ShapeDtypeStruct((N, HP), BF16),
            jax.ShapeDtypeStruct((N, HP), BF16),
            jax.ShapeDtypeStruct((batch, DC, seq), BF16),
        ],
        compiler_params=_cp(("parallel",), VMEM_LIMIT),
        name="mla_prep",
    )(lat, pos_col, gq, gkv, wq_big, wk_big, wv_t, freqs)


HEADS_PER_STEP_C = 4


def _mla_flash_kernel(qi_ref, ki_ref, q_ref, k_ref, vt_ref, o_ref, m_sc, l_sc, acc_sc):
    t = pl.program_id(2)
    qi, ki = qi_ref[t], ki_ref[t]

    @pl.when(ki == 0)
    def _():
        m_sc[...] = jnp.full(m_sc.shape, NEG, F32)
        l_sc[...] = jnp.zeros(l_sc.shape, F32)
        acc_sc[...] = jnp.zeros(acc_sc.shape, F32)

    def step(masked):
        T = q_ref.shape[1]
        if masked:
            key = lax.broadcasted_iota(jnp.int32, (T, T), 0)
            qry = lax.broadcasted_iota(jnp.int32, (T, T), 1)
            keep = key <= qry
        for h in range(HEADS_PER_STEP_C):
            q = q_ref[0, :, h * HEAD_PAD_C:(h + 1) * HEAD_PAD_C]
            k = k_ref[0, :, h * HEAD_PAD_C:(h + 1) * HEAD_PAD_C]
            vt = vt_ref[0, h * V_DIM:(h + 1) * V_DIM, :]
            st = _nt_dot(k, q)
            if masked:
                st = jnp.where(keep, st, NEG)
            m_prev = m_sc[h]
            m_new = jnp.maximum(m_prev, jnp.max(st, axis=0, keepdims=True))
            alpha = jnp.exp2(m_prev - m_new)
            p = jnp.exp2(st - m_new)
            l_sc[h] = alpha * l_sc[h] + jnp.sum(p, axis=0, keepdims=True)
            acc_sc[h] = alpha * acc_sc[h] + jnp.dot(vt, p.astype(BF16), preferred_element_type=F32)
            m_sc[h] = m_new

    @pl.when(ki < qi)
    def _():
        step(False)

    @pl.when(ki == qi)
    def _():
        step(True)
        ot = jnp.concatenate([acc_sc[h] / l_sc[h] for h in range(HEADS_PER_STEP_C)], axis=0)
        o_ref[0] = ot.T.astype(o_ref.dtype)


def mla_attention(q_all, k_all, vt_all, batch, seq):
    T = 512
    nq = seq // T
    pairs = [(a, b) for a in range(nq) for b in range(a + 1)]
    qi_tab = jnp.asarray([p[0] for p in pairs], jnp.int32)
    ki_tab = jnp.asarray([p[1] for p in pairs], jnp.int32)
    hp = N_HEADS_C // HEADS_PER_STEP_C
    qw = HEADS_PER_STEP_C * HEAD_PAD_C
    vw = HEADS_PER_STEP_C * V_DIM
    q3 = q_all.reshape(batch, seq, -1)
    k3 = k_all.reshape(batch, seq, -1)
    grid_spec = pltpu.PrefetchScalarGridSpec(
        num_scalar_prefetch=2,
        grid=(batch, hp, len(pairs)),
        in_specs=[
            pl.BlockSpec((1, T, qw), lambda b, h, t, qi, ki: (b, qi[t], h)),
            pl.BlockSpec((1, T, qw), lambda b, h, t, qi, ki: (b, ki[t], h)),
            pl.BlockSpec((1, vw, T), lambda b, h, t, qi, ki: (b, h, ki[t])),
        ],
        out_specs=pl.BlockSpec((1, T, vw), lambda b, h, t, qi, ki: (b, qi[t], h)),
        scratch_shapes=[
            pltpu.VMEM((HEADS_PER_STEP_C, 1, T), F32),
            pltpu.VMEM((HEADS_PER_STEP_C, 1, T), F32),
            pltpu.VMEM((HEADS_PER_STEP_C, V_DIM, T), F32),
        ],
    )
    o = pl.pallas_call(
        _mla_flash_kernel,
        grid_spec=grid_spec,
        out_shape=jax.ShapeDtypeStruct((batch, seq, DC), BF16),
        compiler_params=_cp(("parallel", "parallel", "arbitrary")),
        name="mla_attention",
    )(qi_tab, ki_tab, q3, k3, vt_all)
    return o.reshape(batch * seq, DC)


def _mixout_kernel(x_ref, gates_ref, ub_ref, ubh_ref, o1_ref, o2_ref, o3_ref, l1_ref, l2_ref, l3_ref, yc_ref,
                   mod1_ref, mod2_ref, g2_ref, poolw_ref, pscale_ref, woa_ref, wob_ref, woc_ref, wout_ref,
                   rwt_ref, sw1_ref, sw3_ref, sw2_ref,
                   xmid_ref, h2a_ref, h2b_ref, logit_ref, *, tiles_per_batch):
    D = x_ref.shape[1]
    tm = x_ref.shape[0]
    tile = pl.program_id(0) % tiles_per_batch

    l1, l2, l3 = l1_ref[...], l2_ref[...], l3_ref[...]
    mx = jnp.maximum(jnp.maximum(l1, l2), l3)
    e1, e2, e3 = jnp.exp(l1 - mx), jnp.exp(l2 - mx), jnp.exp(l3 - mx)
    ya = (e1 * o1_ref[...].astype(F32) + e2 * o2_ref[...].astype(F32) + e3 * o3_ref[...].astype(F32)) / (e1 + e2 + e3)
    a_out = jnp.dot(ya.astype(BF16), woa_ref[...], preferred_element_type=F32)

    u = ub_ref[...].astype(F32)
    halo = jnp.where(tile > 0, ubh_ref[...].astype(F32), 0.0)
    ext = jnp.concatenate([halo, u], axis=0)
    t_seq = tile * tm + lax.broadcasted_iota(jnp.int32, (tm, 1), 0)
    pooled = []
    for gi, w in enumerate(POOL_WINDOWS):
        sl = slice(gi * POOL_GROUP_DIM, (gi + 1) * POOL_GROUP_DIM)
        acc = ext[:, sl]
        k = 1
        while k < w:
            acc = acc + pltpu.roll(acc, k, axis=0)
            k *= 2
        cnt = jnp.minimum(t_seq + 1, w).astype(F32)
        pg = acc[POOL_HALO:] / cnt - u[:, sl]
        pooled.append(jnp.dot(pg.astype(BF16), poolw_ref[gi], preferred_element_type=F32))
    yb = jnp.concatenate(pooled, axis=1) * pscale_ref[...]
    b_out = jnp.dot(yb.astype(BF16), wob_ref[...], preferred_element_type=F32)
    c_out = jnp.dot(yc_ref[...], woc_ref[...], preferred_element_type=F32)

    g = gates_ref[...].astype(F32)
    mix = (jax.nn.sigmoid(g[:, :D]) * a_out + jax.nn.sigmoid(g[:, D:2 * D]) * b_out
           + jax.nn.sigmoid(g[:, 2 * D:]) * c_out)
    tok = jnp.dot(mix.astype(BF16), wout_ref[...], preferred_element_type=F32)
    xn = x_ref[...] + mod1_ref[0][:, 2 * D:] * tok

    mod2 = mod2_ref[0]
    y = xn * lax.rsqrt(jnp.mean(xn * xn, axis=-1, keepdims=True) + EPS) * g2_ref[...]
    h2 = y * (1.0 + mod2[:, D:2 * D]) + mod2[:, :D]
    h2b = h2.astype(BF16)
    h2a_ref[...], h2b_ref[...] = _pack_row_halves(h2b)
    logit_ref[...] = _nt_dot(rwt_ref[...], h2b)
    hid = _silu(jnp.dot(h2b, sw1_ref[...], preferred_element_type=F32)) * jnp.dot(
        h2b, sw3_ref[...], preferred_element_type=F32)
    shared = jnp.dot(hid.astype(BF16), sw2_ref[...], preferred_element_type=F32)
    xmid_ref[...] = xn + mod2[:, 2 * D:] * shared


def mix_out(x2, gu, dil, yc, mod1, mod2, g2, pool_w, pool_scale, w_oa, w_ob, w_oc, w_out, rwt, sw1, sw3, sw2, seq):
    N, D = x2.shape
    tm = 256
    tpb = seq // tm
    (o1, l1), (o2, l2), (o3, l3) = dil
    row = lambda w, c=0: pl.BlockSpec((tm, w), lambda i: (i, c))
    const2 = lambda a: pl.BlockSpec(a.shape, lambda i: (0,) * a.ndim)
    modspec = pl.BlockSpec((1, 1, 3 * D), lambda i: (i // tpb, 0, 0))
    ub_col = 3 * D // DB
    halo_spec = pl.BlockSpec(
        (POOL_HALO, DB), lambda i: (jnp.maximum(i * (tm // POOL_HALO) - 1, 0), ub_col))
    weights = [g2.reshape(1, D), pool_w, pool_scale.reshape(1, DB), w_oa, w_ob, w_oc, w_out, rwt, sw1, sw3, sw2]
    return pl.pallas_call(
        functools.partial(_mixout_kernel, tiles_per_batch=tpb),
        grid=(N // tm,),
        in_specs=[
            row(D), row(3 * D), row(DB, ub_col), halo_spec,
            row(GROUP_W), row(GROUP_W), row(GROUP_W), row(GROUP_W), row(GROUP_W), row(GROUP_W), row(DC),
            modspec, modspec,
        ] + [const2(a) for a in weights],
        out_specs=[row(D), row(PACK_W), row(PACK_W), pl.BlockSpec((N_EXPERTS, tm), lambda i: (0, i))],
        out_shape=[
            jax.ShapeDtypeStruct((N, D), F32),
            jax.ShapeDtypeStruct((N, PACK_W), jnp.int32),
            jax.ShapeDtypeStruct((N, PACK_W), jnp.int32),
            jax.ShapeDtypeStruct((N_EXPERTS, N), F32),
        ],
        compiler_params=_cp(("parallel",), VMEM_LIMIT),
        name="mix_out",
    )(x2, gu, gu, gu, o1, o2, o3, l1, l2, l3, yc, mod1, mod2, *weights)


def _route_kernel(lg_ref, bias_ref, tri_ref, dest_ref, w_ref, cnt_ref, run_sc, start_sc, *, slot_block):
    G, GS = N_GROUPS, GROUP_SIZE
    scores = jax.nn.sigmoid(lg_ref[...])
    sel = scores + bias_ref[...]
    tn = sel.shape[1]
    eio = lax.broadcasted_iota(jnp.int32, (GS, tn), 0)
    ninf = -jnp.inf

    gs = []
    for g in range(G):
        v = sel[g * GS:(g + 1) * GS]
        m1 = jnp.max(v, axis=0, keepdims=True)
        i1 = jnp.min(jnp.where(v == m1, eio, GS), axis=0, keepdims=True)
        m2 = jnp.max(jnp.where(eio == i1, ninf, v), axis=0, keepdims=True)
        gs.append(m1 + m2)
    gsm = jnp.concatenate(gs, axis=0)
    gio = lax.broadcasted_iota(jnp.int32, (G, tn), 0)
    rank = jnp.zeros((G, tn), jnp.int32)
    for g2 in range(G):
        beats = (gs[g2] > gsm) | ((gs[g2] == gsm) & (g2 < gio))
        rank = rank + beats.astype(jnp.int32)
    gsel = rank < TOPK_GROUPS

    vs = [jnp.where(gsel[g:g + 1], sel[g * GS:(g + 1) * GS], NEG) for g in range(G)]
    eid = [eio + g * GS for g in range(G)]
    chosen = [jnp.zeros((GS, tn), jnp.bool_) for _ in range(G)]
    picks = []
    for _ in range(TOP_K):
        m = functools.reduce(jnp.maximum, [jnp.max(v, axis=0, keepdims=True) for v in vs])
        idx = functools.reduce(jnp.minimum, [
            jnp.min(jnp.where(v == m, e, N_EXPERTS), axis=0, keepdims=True) for v, e in zip(vs, eid)])
        picks.append(idx)
        for g in range(G):
            hit = eid[g] == idx
            chosen[g] = chosen[g] | hit
            vs[g] = jnp.where(hit, ninf, vs[g])
    mask = jnp.concatenate(chosen, axis=0).astype(F32)
    tile_counts = jnp.sum(mask, axis=1, keepdims=True)

    def pick_rows(table):
        rows = []
        for idx in picks:
            parts = [jnp.sum(jnp.where(eid[g] == idx, table[g * GS:(g + 1) * GS], 0.0), axis=0, keepdims=True)
                     for g in range(G)]
            rows.append(functools.reduce(jnp.add, parts))
        return jnp.concatenate(rows, axis=0)

    wk = pick_rows(scores)
    w_ref[0] = wk / jnp.sum(wk, axis=0, keepdims=True) * ROUTED_SCALE

    phase = pl.program_id(0)
    step = pl.program_id(1)

    @pl.when((phase == 0) & (step == 0))
    def _():
        run_sc[...] = jnp.zeros(run_sc.shape, F32)
        start_sc[...] = jnp.zeros(start_sc.shape, F32)

    @pl.when((phase == 1) & (step == 0))
    def _():
        counts = run_sc[...].astype(jnp.int32)
        cnt_ref[...] = jnp.broadcast_to(counts, cnt_ref.shape)
        shift = slot_block.bit_length() - 1
        padded = lax.shift_left(lax.shift_right_logical(counts + (slot_block - 1), shift), shift).astype(F32)
        r = lax.broadcasted_iota(jnp.int32, (N_EXPERTS, N_EXPERTS), 0)
        c = lax.broadcasted_iota(jnp.int32, (N_EXPERTS, N_EXPERTS), 1)
        as_row = jnp.sum(jnp.where(r == c, padded, 0.0), axis=0, keepdims=True)
        start_sc[...] = jnp.sum(jnp.where(c < r, as_row, 0.0), axis=1, keepdims=True)
        run_sc[...] = jnp.zeros(run_sc.shape, F32)

    before = jnp.dot(mask.astype(BF16), tri_ref[...], preferred_element_type=F32) - mask
    slot = start_sc[...] + run_sc[...] + before
    dest_ref[0] = pick_rows(slot).astype(jnp.int32)
    run_sc[...] = run_sc[...] + tile_counts


SLOT_BLOCK = 512


def route(logits_t, bias):
    E, N = logits_t.shape
    tn = 1024
    tri = (jnp.arange(tn)[:, None] <= jnp.arange(tn)[None, :]).astype(BF16)
    tile = lambda r: pl.BlockSpec((r, tn), lambda p, i: (0, i))
    plane = lambda: pl.BlockSpec((1, TOP_K, tn), lambda p, i: (p, 0, i))
    dest, w, cnt = pl.pallas_call(
        functools.partial(_route_kernel, slot_block=SLOT_BLOCK),
        grid=(2, N // tn),
        in_specs=[tile(E), pl.BlockSpec((E, 1), lambda p, i: (0, 0)), pl.BlockSpec((tn, tn), lambda p, i: (0, 0))],
        out_specs=[plane(), plane(), pl.BlockSpec((E, 128), lambda p, i: (0, 0))],
        out_shape=[
            jax.ShapeDtypeStruct((2, TOP_K, N), jnp.int32),
            jax.ShapeDtypeStruct((2, TOP_K, N), F32),
            jax.ShapeDtypeStruct((E, 128), jnp.int32),
        ],
        scratch_shapes=[pltpu.VMEM((E, 1), F32), pltpu.VMEM((E, 1), F32)],
        compiler_params=_cp(("arbitrary", "arbitrary")),
        name="route",
    )(logits_t, bias.reshape(E, 1), tri)
    return dest[1], w[1], cnt[:, 0]


def block_tables(counts, n_tokens):
    E = counts.shape[0]
    blk = SLOT_BLOCK
    nblk = (n_tokens * TOP_K + E * blk) // blk
    per_expert = (counts + blk - 1) // blk
    bend = jnp.cumsum(per_expert)
    b = jnp.arange(nblk, dtype=jnp.int32)
    blk_e = jnp.minimum(jnp.sum(bend[None, :] <= b[:, None], axis=1), E - 1).astype(jnp.int32)
    within = b - (bend - per_expert)[blk_e]
    nvalid = jnp.where(b < bend[-1], jnp.clip(counts[blk_e] - within * blk, 0, blk), 0)
    return blk_e, nvalid.astype(jnp.int32)


def _sc_mesh():
    return plsc.VectorSubcoreMesh(core_axis_name="c", subcore_axis_name="s")


SC_WINDOW = 128


def sc_scatter_rows(x, dest, n_slots):
    N, W = x.shape
    K = dest.shape[0]

    @functools.partial(pl.kernel, out_type=jax.ShapeDtypeStruct((n_slots, W), x.dtype), mesh=_sc_mesh(),
                       scratch_types=[])
    def scatter(x_hbm, i_hbm, o_hbm):
        def body(x_vmem, i_vmem):
            for k in range(K):
                pltpu.sync_copy(x_vmem, o_hbm.at[i_vmem.at[k]])

        pltpu.emit_pipeline(
            body,
            grid=(N // SC_WINDOW,),
            in_specs=[pl.BlockSpec((SC_WINDOW, W), lambda i: (i, 0)),
                      pl.BlockSpec((K, SC_WINDOW), lambda i: (0, i))],
            out_specs=[],
            core_axis_name=("c", "s"),
            dimension_semantics=(pltpu.PARALLEL,),
        )(x_hbm, i_hbm)

    return scatter(x, dest)


def sc_gather_rows(y, dest):
    W = y.shape[1]
    K, N = dest.shape

    @functools.partial(pl.kernel, out_type=jax.ShapeDtypeStruct((K, N, W), y.dtype), mesh=_sc_mesh(),
                       scratch_types=[])
    def gather(y_hbm, i_hbm, o_hbm):
        def body(i_vmem, o_vmem):
            pltpu.sync_copy(y_hbm.at[i_vmem.at[0, 0]], o_vmem.at[0])

        pltpu.emit_pipeline(
            body,
            grid=(K, N // SC_WINDOW),
            in_specs=[pl.BlockSpec((1, 1, SC_WINDOW), lambda k, i: (k, 0, i))],
            out_specs=[pl.BlockSpec((1, SC_WINDOW, W), lambda k, i: (k, i, 0))],
            core_axis_name=("c", "s"),
            dimension_semantics=(pltpu.PARALLEL, pltpu.PARALLEL),
        )(i_hbm, o_hbm)

    return gather(y, dest.reshape(K, 1, N))


def _expert_kernel(blk_e_ref, nvalid_ref, xa_ref, xb_ref, w1_ref, w3_ref, w2_ref, ya_ref, yb_ref,
                   w1_sc, w3_sc, w2_sc):
    b = pl.program_id(0)
    nv = nvalid_ref[b]
    prev_e = blk_e_ref[jnp.maximum(b - 1, 0)]

    @pl.when((b == 0) | (blk_e_ref[b] != prev_e))
    def _():
        w1_sc[...] = w1_ref[0, 0].astype(BF16)
        w3_sc[...] = w3_ref[0, 0].astype(BF16)
        w2_sc[...] = w2_ref[0, 0].astype(BF16)

    @pl.when(nv > 0)
    def _():
        x = _unpack_row_halves(xa_ref[...], xb_ref[...])
        rows = lax.broadcasted_iota(jnp.int32, x.shape, 0)
        x = jnp.where(rows < nv, x, 0.0).astype(BF16)
        hid = _silu(jnp.dot(x, w1_sc[...], preferred_element_type=F32)) * jnp.dot(
            x, w3_sc[...], preferred_element_type=F32)
        y = jnp.dot(hid.astype(BF16), w2_sc[...], preferred_element_type=F32)
        ya_ref[...], yb_ref[...] = _pack_row_halves(y)

    @pl.when(nv == 0)
    def _():
        ya_ref[...] = jnp.zeros(ya_ref.shape, ya_ref.dtype)
        yb_ref[...] = jnp.zeros(yb_ref.shape, yb_ref.dtype)


def routed_experts(xa, xb, blk_e, nvalid, w1, w3, w2, layer):
    P = xa.shape[0]
    blk = SLOT_BLOCK
    _, E, D, FF = w1.shape
    slots = lambda: pl.BlockSpec((blk, PACK_W), lambda b, be, nv: (b, 0))
    grid_spec = pltpu.PrefetchScalarGridSpec(
        num_scalar_prefetch=2,
        grid=(P // blk,),
        in_specs=[
            slots(), slots(),
            pl.BlockSpec((1, 1, D, FF), lambda b, be, nv: (layer, be[b], 0, 0)),
            pl.BlockSpec((1, 1, D, FF), lambda b, be, nv: (layer, be[b], 0, 0)),
            pl.BlockSpec((1, 1, FF, D), lambda b, be, nv: (layer, be[b], 0, 0)),
        ],
        out_specs=[slots(), slots()],
        scratch_shapes=[pltpu.VMEM((D, FF), BF16), pltpu.VMEM((D, FF), BF16), pltpu.VMEM((FF, D), BF16)],
    )
    return pl.pallas_call(
        _expert_kernel,
        grid_spec=grid_spec,
        out_shape=[jax.ShapeDtypeStruct((P, PACK_W), jnp.int32)] * 2,
        compiler_params=_cp(("arbitrary",), VMEM_LIMIT),
        name="routed_experts",
    )(blk_e, nvalid, xa, xb, w1, w3, w2)


def _combine_kernel(xmid_ref, oa_ref, ob_ref, w_ref, mod2_ref, fg_ref, out_ref, *, final):
    D = xmid_ref.shape[1]
    w = w_ref[...]
    acc = w[:, 0:1] * _unpack_row_halves(oa_ref[0], ob_ref[0])
    for k in range(1, TOP_K):
        acc = acc + w[:, k:k + 1] * _unpack_row_halves(oa_ref[k], ob_ref[k])
    x = xmid_ref[...] + mod2_ref[0][:, 2 * D:] * acc
    if final:
        x = x * lax.rsqrt(jnp.mean(x * x, axis=-1, keepdims=True) + EPS) * fg_ref[...]
    out_ref[...] = x


def combine(xmid, oa, ob, w_tok, mod2, final_g, seq, final):
    N, D = xmid.shape
    tm = 256
    tpb = seq // tm
    rows8 = lambda: pl.BlockSpec((TOP_K, tm, PACK_W), lambda i: (0, i, 0))
    return pl.pallas_call(
        functools.partial(_combine_kernel, final=final),
        grid=(N // tm,),
        in_specs=[
            pl.BlockSpec((tm, D), lambda i: (i, 0)),
            rows8(), rows8(),
            pl.BlockSpec((tm, TOP_K), lambda i: (i, 0)),
            pl.BlockSpec((1, 1, 3 * D), lambda i: (i // tpb, 0, 0)),
            pl.BlockSpec((1, D), lambda i: (0, 0)),
        ],
        out_specs=pl.BlockSpec((tm, D), lambda i: (i, 0)),
        out_shape=jax.ShapeDtypeStruct((N, D), F32),
        compiler_params=_cp(("parallel",), VMEM_LIMIT),
        name="combine",
    )(xmid, oa, ob, w_tok, mod2, final_g.reshape(1, D))


def _permute_w_in(w):
    ub = w[:, 3 * DA:3 * DA + DB]
    lat_lo = 3 * DA + DB
    lat_hi = lat_lo + Q_LORA + KV_LORA + QK_ROPE
    lat, gates = w[:, lat_lo:lat_hi], w[:, lat_hi:]
    pad = jnp.zeros((w.shape[0], LAT_W - (lat_hi - lat_lo)), w.dtype)
    parts = [gates, ub, lat, pad]
    for g in range(len(DIL_GROUPS)):
        sl = slice(g * GROUP_W, (g + 1) * GROUP_W)
        parts += [w[:, :DA][:, sl] * (HEAD_DIM_A ** -0.5), w[:, DA:2 * DA][:, sl], w[:, 2 * DA:3 * DA][:, sl]]
    return jnp.concatenate(parts, axis=1).astype(BF16)


def kernel(x, c, positions, ada_mix_w, ada_mix_b, norm_mix_g, w_in, pool_w, pool_scale, cq_norm_g, ckv_norm_g, w_uq, w_ukv, w_oa, w_ob, w_oc, w_out, ada_ffn_w, ada_ffn_b, norm_ffn_g, router_w, router_bias, exp_w1, exp_w3, exp_w2, sh_w1, sh_w3, sh_w2, final_g):
    B, S, D = x.shape
    depth = w_in.shape[0]
    N = B * S
    mod_mix = adaln_rows(c, ada_mix_w, ada_mix_b)
    mod_ffn = adaln_rows(c, ada_ffn_w, ada_ffn_b)
    pos_col = positions.reshape(N, 1)
    x2 = x.reshape(N, D)
    for l in range(depth):
        mod1 = mod_mix[l].reshape(B, 1, 3 * D)
        mod2 = mod_ffn[l].reshape(B, 1, 3 * D)
        gu, lat, *qkv = in_projection(x2, norm_mix_g[l], mod1, _permute_w_in(w_in[l]), S)
        dil = [dilated_attention(qkv[2 * g], qkv[2 * g + 1], B, S, d) for g, (_, d) in enumerate(DIL_GROUPS)]
        mla_w = _mla_weights(cq_norm_g[l], ckv_norm_g[l], w_uq[l], w_ukv[l])
        q_all, k_all, vt_all = mla_prep(lat, pos_col, *mla_w, B, S)
        yc = mla_attention(q_all, k_all, vt_all, B, S)
        xmid, h2a, h2b, logits_t = mix_out(
            x2, gu, dil, yc, mod1, mod2, norm_ffn_g[l], pool_w[l].astype(BF16), pool_scale[l],
            w_oa[l].astype(BF16), w_ob[l].astype(BF16), w_oc[l].astype(BF16), w_out[l].astype(BF16),
            router_w[l].T.astype(BF16), sh_w1[l].astype(BF16), sh_w3[l].astype(BF16), sh_w2[l].astype(BF16), S)
        dest, w_k, counts = route(logits_t, router_bias[l])
        blk_e, nvalid = block_tables(counts, N)
        n_slots = blk_e.shape[0] * SLOT_BLOCK
        xa = sc_scatter_rows(h2a, dest, n_slots)
        xb = sc_scatter_rows(h2b, dest, n_slots)
        ya, yb = routed_experts(xa, xb, blk_e, nvalid, exp_w1, exp_w3, exp_w2, l)
        oa = sc_gather_rows(ya, dest)
        ob = sc_gather_rows(yb, dest)
        x2 = combine(xmid, oa, ob, w_k.T, mod2, final_g, S, final=(l == depth - 1))
    return x2.reshape(B, S, D)
```

```python
import functools
import math

import jax
import jax.numpy as jnp
from jax import lax
from jax.experimental import pallas as pl
from jax.experimental.pallas import tpu as pltpu
from jax.experimental.pallas import tpu_sc as plsc

F32 = jnp.float32
BF16 = jnp.bfloat16
HIGHEST = lax.Precision.HIGHEST

D_MODEL = 1024
HEAD_DIM_A = 64
HEADS_PER_GROUP_A = 4
DIL_GROUPS = ((128, 1), (512, 4), (2048, 16))
GROUP_W = HEADS_PER_GROUP_A * HEAD_DIM_A
DA = GROUP_W * len(DIL_GROUPS)
POOL_WINDOWS = (2, 4, 8, 16)
POOL_GROUP_DIM = 128
DB = POOL_GROUP_DIM * len(POOL_WINDOWS)
POOL_HALO = 16
N_HEADS_C = 8
QK_NOPE = 64
QK_ROPE = 32
V_DIM = 64
Q_LORA = 384
KV_LORA = 256
DC = N_HEADS_C * V_DIM
HEAD_PAD_C = 128
ROPE_THETA = 10000.0
N_EXPERTS = 64
TOP_K = 8
N_GROUPS = 8
TOPK_GROUPS = 4
GROUP_SIZE = N_EXPERTS // N_GROUPS
EXPERT_FF = 256
ROUTED_SCALE = 2.5
EPS = 1e-6
NEG = -1e30
Q_BLOCK = 128

LAT_W = 768
GU_W = 3 * D_MODEL + DB
IN_OUT_WIDTHS = (GU_W, LAT_W) + (2 * GROUP_W, GROUP_W) * len(DIL_GROUPS)

VMEM_LIMIT = 56 * 1024 * 1024


def _cp(sem, vmem=None):
    return pltpu.CompilerParams(dimension_semantics=sem, vmem_limit_bytes=vmem)


def _silu(v):
    return v * jax.nn.sigmoid(v)


def _nt_dot(a, b):
    return lax.dot_general(a, b, (((1,), (1,)), ((), ())), preferred_element_type=F32)


PACK_W = D_MODEL // 4
_HI_MASK = -65536


def _bf16_bits(v):
    return lax.bitcast_convert_type(v.astype(BF16).astype(F32), jnp.int32)


def _pack_row_halves(v):
    halves = []
    for h in range(2):
        lo = _bf16_bits(v[:, (2 * h) * PACK_W:(2 * h + 1) * PACK_W])
        hi = _bf16_bits(v[:, (2 * h + 1) * PACK_W:(2 * h + 2) * PACK_W])
        halves.append(lax.shift_right_logical(lo, 16) | (hi & _HI_MASK))
    return halves


def _unpack_row_halves(wa, wb):
    parts = []
    for w in (wa, wb):
        parts.append(lax.bitcast_convert_type(lax.shift_left(w, 16), F32))
        parts.append(lax.bitcast_convert_type(w & _HI_MASK, F32))
    return jnp.concatenate(parts, axis=1)


def _adaln_kernel(c_ref, w_ref, b_ref, o_ref):
    s = _silu(c_ref[...])
    o_ref[0] = jnp.dot(s, w_ref[0], preferred_element_type=F32, precision=HIGHEST) + b_ref[0]


def adaln_rows(c, w, b):
    L, D, D3 = w.shape
    B = c.shape[0]
    tn = 1024
    return pl.pallas_call(
        _adaln_kernel,
        grid=(L, D3 // tn),
        in_specs=[
            pl.BlockSpec((B, D), lambda l, j: (0, 0)),
            pl.BlockSpec((1, D, tn), lambda l, j: (l, 0, j)),
            pl.BlockSpec((1, 1, tn), lambda l, j: (l, 0, j)),
        ],
        out_specs=pl.BlockSpec((1, B, tn), lambda l, j: (l, 0, j)),
        out_shape=jax.ShapeDtypeStruct((L, B, D3), F32),
        compiler_params=_cp(("parallel", "parallel")),
        name="adaln_rows",
    )(c, w, b.reshape(L, 1, D3))


def _inproj_kernel(x_ref, g_ref, mod_ref, w_ref, *o_refs, chunk):
    D = x_ref.shape[1]
    x = x_ref[...]
    y = x * lax.rsqrt(jnp.mean(x * x, axis=-1, keepdims=True) + EPS) * g_ref[...]
    mod = mod_ref[0]
    h = (y * (1.0 + mod[:, D:2 * D]) + mod[:, :D]).astype(BF16)
    col = 0
    for o_ref in o_refs:
        width = o_ref.shape[1]
        for c0 in range(0, width, chunk):
            cw = min(chunk, width - c0)
            o_ref[:, c0:c0 + cw] = jnp.dot(
                h, w_ref[:, col + c0:col + c0 + cw], preferred_element_type=F32).astype(o_ref.dtype)
        col += width


def in_projection(x2, g, mod, w, seq):
    N, D = x2.shape
    tm = 512
    tpb = seq // tm
    return pl.pallas_call(
        functools.partial(_inproj_kernel, chunk=512),
        grid=(N // tm,),
        in_specs=[
            pl.BlockSpec((tm, D), lambda i: (i, 0)),
            pl.BlockSpec((1, D), lambda i: (0, 0)),
            pl.BlockSpec((1, 1, 3 * D), lambda i: (i // tpb, 0, 0)),
            pl.BlockSpec(w.shape, lambda i: (0, 0), pipeline_mode=pl.Buffered(1)),
        ],
        out_specs=[pl.BlockSpec((tm, wd), lambda i: (i, 0)) for wd in IN_OUT_WIDTHS],
        out_shape=[jax.ShapeDtypeStruct((N, wd), BF16) for wd in IN_OUT_WIDTHS],
        compiler_params=_cp(("parallel",), VMEM_LIMIT),
        name="in_projection",
    )(x2, g.reshape(1, D), mod, w)


def _dilated_kernel(q_ref, kc_ref, kp_ref, vtc_ref, vtp_ref, ot_ref, lse_ref):
    i = pl.program_id(1)
    T = Q_BLOCK
    key = lax.broadcasted_iota(jnp.int32, (T, T), 0)
    qry = lax.broadcasted_iota(jnp.int32, (T, T), 1)
    valid_c = key <= qry
    near = key >= qry
    run = q_ref.shape[1] // T
    heads = [slice(h * HEAD_DIM_A, (h + 1) * HEAD_DIM_A) for h in range(HEADS_PER_GROUP_A)]

    def blocks(j):
        rows = slice(j * T, (j + 1) * T)
        if j == 0:
            return rows, kc_ref[0, rows, :], vtc_ref[0, :, rows], kp_ref[0], vtp_ref[0], near & (i > 0)
        before = slice((j - 1) * T, j * T)
        return rows, kc_ref[0, rows, :], vtc_ref[0, :, rows], kc_ref[0, before, :], vtc_ref[0, :, before], near

    scores, probs = {}, {}
    for j in range(run):
        rows, kc, _, kp, _, valid_p = blocks(j)
        q = q_ref[0, rows, :]
        for h, sl in enumerate(heads):
            qh = q[:, sl]
            scores[j, h] = (jnp.where(valid_c, _nt_dot(kc[:, sl], qh), NEG),
                            jnp.where(valid_p, _nt_dot(kp[:, sl], qh), NEG))
    for (j, h), (sc, sp) in scores.items():
        m = jnp.maximum(jnp.max(sc, axis=0, keepdims=True), jnp.max(sp, axis=0, keepdims=True))
        pc = jnp.exp(sc - m)
        pp = jnp.exp(sp - m)
        den = jnp.sum(pc, axis=0, keepdims=True) + jnp.sum(pp, axis=0, keepdims=True)
        probs[j, h] = (pc.astype(BF16), pp.astype(BF16), den, m + jnp.log(den))
    for j in range(run):
        rows, _, vtc, _, vtp, _ = blocks(j)
        outs = []
        for h, sl in enumerate(heads):
            pc, pp, den, _ = probs[j, h]
            o = jnp.dot(vtc[sl, :], pc, preferred_element_type=F32) + jnp.dot(vtp[sl, :], pp, preferred_element_type=F32)
            outs.append(o / den)
        ot_ref[0, :, rows] = jnp.concatenate(outs, axis=0).astype(ot_ref.dtype)
        lse_ref[0, :, rows] = jnp.concatenate([probs[j, h][3] for h in range(len(heads))], axis=0)


DILATED_RUN = 4


def dilated_attention(qk, v, batch, seq, dilation):
    L = seq // dilation
    nb = L // Q_BLOCK
    run = min(DILATED_RUN, nb)
    H = HEADS_PER_GROUP_A
    qk_r = qk.reshape(batch, L, dilation, 2 * GROUP_W).transpose(0, 2, 1, 3).reshape(batch * dilation, L, 2 * GROUP_W)
    vt_r = v.reshape(batch, L, dilation, GROUP_W).transpose(0, 2, 3, 1).reshape(batch * dilation, GROUP_W, L)
    before = lambda i: jnp.maximum(i * run - 1, 0)
    ot, lse = pl.pallas_call(
        _dilated_kernel,
        grid=(batch * dilation, nb // run),
        in_specs=[
            pl.BlockSpec((1, run * Q_BLOCK, GROUP_W), lambda s, i: (s, i, 0)),
            pl.BlockSpec((1, run * Q_BLOCK, GROUP_W), lambda s, i: (s, i, 1)),
            pl.BlockSpec((1, Q_BLOCK, GROUP_W), lambda s, i: (s, before(i), 1)),
            pl.BlockSpec((1, GROUP_W, run * Q_BLOCK), lambda s, i: (s, 0, i)),
            pl.BlockSpec((1, GROUP_W, Q_BLOCK), lambda s, i: (s, 0, before(i))),
        ],
        out_specs=[
            pl.BlockSpec((1, GROUP_W, run * Q_BLOCK), lambda s, i: (s, 0, i)),
            pl.BlockSpec((1, H, run * Q_BLOCK), lambda s, i: (s, 0, i)),
        ],
        out_shape=[
            jax.ShapeDtypeStruct((batch * dilation, GROUP_W, L), BF16),
            jax.ShapeDtypeStruct((batch * dilation, H, L), F32),
        ],
        compiler_params=_cp(("parallel", "parallel")),
        name=f"dilated_attention_d{dilation}",
    )(qk_r, qk_r, qk_r, vt_r, vt_r)
    o = ot.reshape(batch, dilation, GROUP_W, L).transpose(0, 3, 1, 2).reshape(batch * seq, GROUP_W)
    lse = lse.reshape(batch, dilation, H, L).transpose(0, 3, 1, 2).reshape(batch * seq, H)
    return o, jnp.repeat(lse, HEAD_DIM_A, axis=1)


def _mla_prep_kernel(lat_ref, pos_ref, gq_ref, gkv_ref, wq_ref, wk_ref, wvt_ref, freq_ref,
                     q_ref, k_ref, vt_ref):
    HP = N_HEADS_C * HEAD_PAD_C
    lat = lat_ref[...].astype(F32)
    cq = lat[:, :Q_LORA]
    ckr = lat[:, Q_LORA:]
    zq = (cq * lax.rsqrt(jnp.mean(cq * cq, axis=-1, keepdims=True) + EPS) * gq_ref[...]).astype(BF16)
    lane = lax.broadcasted_iota(jnp.int32, ckr.shape, 1)
    is_kv = lane < KV_LORA
    ms = jnp.sum(jnp.where(is_kv, ckr * ckr, 0.0), axis=-1, keepdims=True) * (1.0 / KV_LORA)
    zkv = (ckr * jnp.where(is_kv, lax.rsqrt(ms + EPS) * gkv_ref[...], 1.0)).astype(BF16)
    qq = jnp.dot(zq, wq_ref[...], preferred_element_type=F32)
    kk = jnp.dot(zkv, wk_ref[...], preferred_element_type=F32)
    ang = pos_ref[...].astype(F32) * freq_ref[...]
    cos, sin = jnp.cos(ang), jnp.sin(ang)
    for h in range(N_HEADS_C):
        lo, hi = h * HEAD_PAD_C, (h + 1) * HEAD_PAD_C
        q_ref[:, lo:hi] = (qq[:, lo:hi] * cos + qq[:, HP + lo:HP + hi] * sin).astype(q_ref.dtype)
        k_ref[:, lo:hi] = (kk[:, lo:hi] * cos + kk[:, HP + lo:HP + hi] * sin).astype(k_ref.dtype)
    vt_ref[0] = _nt_dot(wvt_ref[...], zkv).astype(vt_ref.dtype)


def _mla_weights(cq_g, ckv_g, w_uq, w_ukv):
    H, HPAD, half = N_HEADS_C, HEAD_PAD_C, QK_ROPE // 2
    scale = (QK_NOPE + QK_ROPE) ** -0.5 * math.log2(math.e)
    wq = w_uq.reshape(Q_LORA, H, QK_NOPE + QK_ROPE) * scale
    q_lin = jnp.pad(wq, ((0, 0), (0, 0), (0, HPAD - QK_NOPE - QK_ROPE)))
    r1, r2 = wq[..., QK_NOPE:QK_NOPE + half], wq[..., QK_NOPE + half:]
    q_sw = jnp.concatenate([jnp.zeros((Q_LORA, H, QK_NOPE), F32), -r2, r1,
                            jnp.zeros((Q_LORA, H, HPAD - QK_NOPE - QK_ROPE), F32)], axis=-1)
    wq_big = jnp.concatenate([q_lin.reshape(Q_LORA, H * HPAD), q_sw.reshape(Q_LORA, H * HPAD)], axis=1)

    rows = LAT_W - Q_LORA
    wkv = w_ukv.reshape(KV_LORA, H, QK_NOPE + V_DIM)
    eye = jnp.eye(QK_ROPE, dtype=F32)
    k_lin = jnp.zeros((rows, H, HPAD), F32)
    k_lin = k_lin.at[:KV_LORA, :, :QK_NOPE].set(wkv[..., :QK_NOPE])
    k_lin = k_lin.at[KV_LORA:KV_LORA + QK_ROPE, :, QK_NOPE:QK_NOPE + QK_ROPE].set(
        jnp.broadcast_to(eye[:, None, :], (QK_ROPE, H, QK_ROPE)))
    swap = jnp.zeros((QK_ROPE, QK_ROPE), F32).at[half:, :half].set(-jnp.eye(half)).at[:half, half:].set(jnp.eye(half))
    k_sw = jnp.zeros((rows, H, HPAD), F32)
    k_sw = k_sw.at[KV_LORA:KV_LORA + QK_ROPE, :, QK_NOPE:QK_NOPE + QK_ROPE].set(
        jnp.broadcast_to(swap[:, None, :], (QK_ROPE, H, QK_ROPE)))
    v_w = jnp.zeros((rows, H, V_DIM), F32).at[:KV_LORA].set(wkv[..., QK_NOPE:])
    wk_big = jnp.concatenate([k_lin.reshape(rows, H * HPAD), k_sw.reshape(rows, H * HPAD)], axis=1)
    wv_t = v_w.reshape(rows, H * V_DIM).T

    gkv = jnp.concatenate([ckv_g, jnp.ones((rows - KV_LORA,), F32)]).reshape(1, rows)
    return cq_g.reshape(1, Q_LORA), gkv, wq_big.astype(BF16), wk_big.astype(BF16), wv_t.astype(BF16)


def _rope_lane_freqs():
    half = QK_ROPE // 2
    freqs = ROPE_THETA ** (-jnp.arange(0, QK_ROPE, 2, dtype=F32) / QK_ROPE)
    row = jnp.zeros((HEAD_PAD_C,), F32)
    row = row.at[QK_NOPE:QK_NOPE + half].set(freqs).at[QK_NOPE + half:QK_NOPE + QK_ROPE].set(freqs)
    return row.reshape(1, HEAD_PAD_C)


def mla_prep(lat, pos_col, gq, gkv, wq_big, wk_big, wv_t, batch, seq):
    N = lat.shape[0]
    HP = N_HEADS_C * HEAD_PAD_C
    tm = 512
    tpb = seq // tm
    freqs = _rope_lane_freqs()
    const = lambda shape: pl.BlockSpec(shape, lambda i: (0, 0))
    return pl.pallas_call(
        _mla_prep_kernel,
        grid=(N // tm,),
        in_specs=[
            pl.BlockSpec((tm, LAT_W), lambda i: (i, 0)),
            pl.BlockSpec((tm, 1), lambda i: (i, 0)),
            const(gq.shape), const(gkv.shape), const(wq_big.shape), const(wk_big.shape), const(wv_t.shape),
            const(freqs.shape),
        ],
        out_specs=[
            pl.BlockSpec((tm, HP), lambda i: (i, 0)),
            pl.BlockSpec((tm, HP), lambda i: (i, 0)),
            pl.BlockSpec((1, DC, tm), lambda i: (i // tpb, 0, i % tpb)),
        ],
        out_shape=[
            jax.ShapeDtypeStruct((N, HP), BF16),
            jax.ShapeDtypeStruct((N, HP), BF16),
            jax.ShapeDtypeStruct((batch, DC, seq), BF16),
        ],
        compiler_params=_cp(("parallel",), VMEM_LIMIT),
        name="mla_prep",
    )(lat, pos_col, gq, gkv, wq_big, wk_big, wv_t, freqs)


HEADS_PER_STEP_C = 8
FLASH_Q_CHUNK = 256


def _mla_flash_kernel(qi_ref, ki_ref, q_ref, k_ref, vt_ref, o_ref, m_sc, l_sc, acc_sc):
    t = pl.program_id(2)
    qi, ki = qi_ref[t], ki_ref[t]

    @pl.when(ki == 0)
    def _():
        m_sc[...] = jnp.full(m_sc.shape, NEG, F32)
        l_sc[...] = jnp.zeros(l_sc.shape, F32)
        acc_sc[...] = jnp.zeros(acc_sc.shape, F32)

    def step(masked):
        T = q_ref.shape[1]
        if masked:
            key = lax.broadcasted_iota(jnp.int32, (T, T), 0)
            qry = lax.broadcasted_iota(jnp.int32, (T, T), 1)
            keep = key <= qry
        chains = [(h, c) for h in range(HEADS_PER_STEP_C) for c in range(T // FLASH_Q_CHUNK)]
        scores, probs, alphas = {}, {}, {}
        for h, c in chains:
            qs = slice(c * FLASH_Q_CHUNK, (c + 1) * FLASH_Q_CHUNK)
            q = q_ref[0, qs, h * HEAD_PAD_C:(h + 1) * HEAD_PAD_C]
            k = k_ref[0, :, h * HEAD_PAD_C:(h + 1) * HEAD_PAD_C]
            st = _nt_dot(k, q)
            scores[h, c] = jnp.where(keep[:, qs], st, NEG) if masked else st
        for h, c in chains:
            qs = slice(c * FLASH_Q_CHUNK, (c + 1) * FLASH_Q_CHUNK)
            st = scores[h, c]
            m_prev = m_sc[h, :, qs]
            m_new = jnp.maximum(m_prev, jnp.max(st, axis=0, keepdims=True))
            alpha = jnp.exp2(m_prev - m_new)
            p = jnp.exp2(st - m_new)
            l_sc[h, :, qs] = alpha * l_sc[h, :, qs] + jnp.sum(p, axis=0, keepdims=True)
            m_sc[h, :, qs] = m_new
            probs[h, c], alphas[h, c] = p.astype(BF16), alpha
        for h, c in chains:
            qs = slice(c * FLASH_Q_CHUNK, (c + 1) * FLASH_Q_CHUNK)
            vt = vt_ref[0, h * V_DIM:(h + 1) * V_DIM, :]
            acc_sc[h, :, qs] = alphas[h, c] * acc_sc[h, :, qs] + jnp.dot(
                vt, probs[h, c], preferred_element_type=F32)

    @pl.when(ki < qi)
    def _():
        step(False)

    @pl.when(ki == qi)
    def _():
        step(True)
        ot = jnp.concatenate([acc_sc[h] / l_sc[h] for h in range(HEADS_PER_STEP_C)], axis=0)
        o_ref[0] = ot.T.astype(o_ref.dtype)


def mla_attention(q_all, k_all, vt_all, batch, seq):
    T = 512
    nq = seq // T
    pairs = [(a, b) for a in range(nq) for b in range(a + 1)]
    qi_tab = jnp.asarray([p[0] for p in pairs], jnp.int32)
    ki_tab = jnp.asarray([p[1] for p in pairs], jnp.int32)
    hp = N_HEADS_C // HEADS_PER_STEP_C
    qw = HEADS_PER_STEP_C * HEAD_PAD_C
    vw = HEADS_PER_STEP_C * V_DIM
    q3 = q_all.reshape(batch, seq, -1)
    k3 = k_all.reshape(batch, seq, -1)
    grid_spec = pltpu.PrefetchScalarGridSpec(
        num_scalar_prefetch=2,
        grid=(batch, hp, len(pairs)),
        in_specs=[
            pl.BlockSpec((1, T, qw), lambda b, h, t, qi, ki: (b, qi[t], h)),
            pl.BlockSpec((1, T, qw), lambda b, h, t, qi, ki: (b, ki[t], h)),
            pl.BlockSpec((1, vw, T), lambda b, h, t, qi, ki: (b, h, ki[t])),
        ],
        out_specs=pl.BlockSpec((1, T, vw), lambda b, h, t, qi, ki: (b, qi[t], h)),
        scratch_shapes=[
            pltpu.VMEM((HEADS_PER_STEP_C, 1, T), F32),
            pltpu.VMEM((HEADS_PER_STEP_C, 1, T), F32),
            pltpu.VMEM((HEADS_PER_STEP_C, V_DIM, T), F32),
        ],
    )
    o = pl.pallas_call(
        _mla_flash_kernel,
        grid_spec=grid_spec,
        out_shape=jax.ShapeDtypeStruct((batch, seq, DC), BF16),
        compiler_params=_cp(("parallel", "parallel", "arbitrary")),
        name="mla_attention",
    )(qi_tab, ki_tab, q3, k3, vt_all)
    return o.reshape(batch * seq, DC)


def _mixout_kernel(x_ref, gates_ref, ub_ref, ubh_ref, o1_ref, o2_ref, o3_ref, l1_ref, l2_ref, l3_ref, yc_ref,
                   mod1_ref, mod2_ref, g2_ref, poolw_ref, pscale_ref, woa_ref, wob_ref, woc_ref, wout_ref,
                   rwt_ref, sw1_ref, sw3_ref, sw2_ref,
                   xmid_ref, h2a_ref, h2b_ref, logit_ref, *, tiles_per_batch):
    D = x_ref.shape[1]
    tm = x_ref.shape[0]
    tile = pl.program_id(0) % tiles_per_batch

    l1, l2, l3 = l1_ref[...], l2_ref[...], l3_ref[...]
    mx = jnp.maximum(jnp.maximum(l1, l2), l3)
    e1, e2, e3 = jnp.exp(l1 - mx), jnp.exp(l2 - mx), jnp.exp(l3 - mx)
    ya = (e1 * o1_ref[...].astype(F32) + e2 * o2_ref[...].astype(F32) + e3 * o3_ref[...].astype(F32)) / (e1 + e2 + e3)
    a_out = jnp.dot(ya.astype(BF16), woa_ref[...], preferred_element_type=F32)

    u = ub_ref[...].astype(F32)
    halo = jnp.where(tile > 0, ubh_ref[...].astype(F32), 0.0)
    ext = jnp.concatenate([halo, u], axis=0)
    t_seq = tile * tm + lax.broadcasted_iota(jnp.int32, (tm, 1), 0)
    pooled = []
    for gi, w in enumerate(POOL_WINDOWS):
        sl = slice(gi * POOL_GROUP_DIM, (gi + 1) * POOL_GROUP_DIM)
        acc = ext[:, sl]
        k = 1
        while k < w:
            acc = acc + pltpu.roll(acc, k, axis=0)
            k *= 2
        cnt = jnp.minimum(t_seq + 1, w).astype(F32)
        pg = acc[POOL_HALO:] / cnt - u[:, sl]
        pooled.append(jnp.dot(pg.astype(BF16), poolw_ref[gi], preferred_element_type=F32))
    yb = jnp.concatenate(pooled, axis=1) * pscale_ref[...]
    b_out = jnp.dot(yb.astype(BF16), wob_ref[...], preferred_element_type=F32)
    c_out = jnp.dot(yc_ref[...], woc_ref[...], preferred_element_type=F32)

    g = gates_ref[...].astype(F32)
    mix = (jax.nn.sigmoid(g[:, :D]) * a_out + jax.nn.sigmoid(g[:, D:2 * D]) * b_out
           + jax.nn.sigmoid(g[:, 2 * D:]) * c_out)
    tok = jnp.dot(mix.astype(BF16), wout_ref[...], preferred_element_type=F32)
    xn = x_ref[...] + mod1_ref[0][:, 2 * D:] * tok

    mod2 = mod2_ref[0]
    y = xn * lax.rsqrt(jnp.mean(xn * xn, axis=-1, keepdims=True) + EPS) * g2_ref[...]
    h2 = y * (1.0 + mod2[:, D:2 * D]) + mod2[:, :D]
    h2b = h2.astype(BF16)
    h2a_ref[...], h2b_ref[...] = _pack_row_halves(h2b)
    logit_ref[...] = _nt_dot(rwt_ref[...], h2b)
    hid = _silu(jnp.dot(h2b, sw1_ref[...], preferred_element_type=F32)) * jnp.dot(
        h2b, sw3_ref[...], preferred_element_type=F32)
    shared = jnp.dot(hid.astype(BF16), sw2_ref[...], preferred_element_type=F32)
    xmid_ref[...] = xn + mod2[:, 2 * D:] * shared


def mix_out(x2, gu, dil, yc, mod1, mod2, g2, pool_w, pool_scale, w_oa, w_ob, w_oc, w_out, rwt, sw1, sw3, sw2, seq):
    N, D = x2.shape
    tm = 256
    tpb = seq // tm
    (o1, l1), (o2, l2), (o3, l3) = dil
    row = lambda w, c=0: pl.BlockSpec((tm, w), lambda i: (i, c))
    const2 = lambda a: pl.BlockSpec(a.shape, lambda i: (0,) * a.ndim)
    modspec = pl.BlockSpec((1, 1, 3 * D), lambda i: (i // tpb, 0, 0))
    ub_col = 3 * D // DB
    halo_spec = pl.BlockSpec(
        (POOL_HALO, DB), lambda i: (jnp.maximum(i * (tm // POOL_HALO) - 1, 0), ub_col))
    weights = [g2.reshape(1, D), pool_w, pool_scale.reshape(1, DB), w_oa, w_ob, w_oc, w_out, rwt, sw1, sw3, sw2]
    return pl.pallas_call(
        functools.partial(_mixout_kernel, tiles_per_batch=tpb),
        grid=(N // tm,),
        in_specs=[
            row(D), row(3 * D), row(DB, ub_col), halo_spec,
            row(GROUP_W), row(GROUP_W), row(GROUP_W), row(GROUP_W), row(GROUP_W), row(GROUP_W), row(DC),
            modspec, modspec,
        ] + [const2(a) for a in weights],
        out_specs=[row(D), row(PACK_W), row(PACK_W), pl.BlockSpec((N_EXPERTS, tm), lambda i: (0, i))],
        out_shape=[
            jax.ShapeDtypeStruct((N, D), F32),
            jax.ShapeDtypeStruct((N, PACK_W), jnp.int32),
            jax.ShapeDtypeStruct((N, PACK_W), jnp.int32),
            jax.ShapeDtypeStruct((N_EXPERTS, N), F32),
        ],
        compiler_params=_cp(("parallel",), VMEM_LIMIT),
        name="mix_out",
    )(x2, gu, gu, gu, o1, o2, o3, l1, l2, l3, yc, mod1, mod2, *weights)


def _route_kernel(lg_ref, bias_ref, tri_ref, dest_ref, w_ref, cnt_ref, run_sc, start_sc, *, slot_block):
    G, GS = N_GROUPS, GROUP_SIZE
    scores = jax.nn.sigmoid(lg_ref[...])
    sel = scores + bias_ref[...]
    tn = sel.shape[1]
    eio = lax.broadcasted_iota(jnp.int32, (GS, tn), 0)
    ninf = -jnp.inf

    gs = []
    for g in range(G):
        v = sel[g * GS:(g + 1) * GS]
        m1 = jnp.max(v, axis=0, keepdims=True)
        i1 = jnp.min(jnp.where(v == m1, eio, GS), axis=0, keepdims=True)
        m2 = jnp.max(jnp.where(eio == i1, ninf, v), axis=0, keepdims=True)
        gs.append(m1 + m2)
    gsm = jnp.concatenate(gs, axis=0)
    gio = lax.broadcasted_iota(jnp.int32, (G, tn), 0)
    rank = jnp.zeros((G, tn), jnp.int32)
    for g2 in range(G):
        beats = (gs[g2] > gsm) | ((gs[g2] == gsm) & (g2 < gio))
        rank = rank + beats.astype(jnp.int32)
    gsel = rank < TOPK_GROUPS

    vs = [jnp.where(gsel[g:g + 1], sel[g * GS:(g + 1) * GS], NEG) for g in range(G)]
    eid = [eio + g * GS for g in range(G)]
    chosen = [jnp.zeros((GS, tn), jnp.bool_) for _ in range(G)]
    picks = []
    for _ in range(TOP_K):
        m = functools.reduce(jnp.maximum, [jnp.max(v, axis=0, keepdims=True) for v in vs])
        idx = functools.reduce(jnp.minimum, [
            jnp.min(jnp.where(v == m, e, N_EXPERTS), axis=0, keepdims=True) for v, e in zip(vs, eid)])
        picks.append(idx)
        for g in range(G):
            hit = eid[g] == idx
            chosen[g] = chosen[g] | hit
            vs[g] = jnp.where(hit, ninf, vs[g])
    mask = jnp.concatenate(chosen, axis=0).astype(F32)
    tile_counts = jnp.sum(mask, axis=1, keepdims=True)

    def pick_rows(table):
        rows = []
        for idx in picks:
            parts = [jnp.sum(jnp.where(eid[g] == idx, table[g * GS:(g + 1) * GS], 0.0), axis=0, keepdims=True)
                     for g in range(G)]
            rows.append(functools.reduce(jnp.add, parts))
        return jnp.concatenate(rows, axis=0)

    wk = pick_rows(scores)
    w_ref[0] = wk / jnp.sum(wk, axis=0, keepdims=True) * ROUTED_SCALE

    phase = pl.program_id(0)
    step = pl.program_id(1)

    @pl.when((phase == 0) & (step == 0))
    def _():
        run_sc[...] = jnp.zeros(run_sc.shape, F32)
        start_sc[...] = jnp.zeros(start_sc.shape, F32)

    @pl.when((phase == 1) & (step == 0))
    def _():
        counts = run_sc[...].astype(jnp.int32)
        cnt_ref[...] = jnp.broadcast_to(counts, cnt_ref.shape)
        shift = slot_block.bit_length() - 1
        padded = lax.shift_left(lax.shift_right_logical(counts + (slot_block - 1), shift), shift).astype(F32)
        r = lax.broadcasted_iota(jnp.int32, (N_EXPERTS, N_EXPERTS), 0)
        c = lax.broadcasted_iota(jnp.int32, (N_EXPERTS, N_EXPERTS), 1)
        as_row = jnp.sum(jnp.where(r == c, padded, 0.0), axis=0, keepdims=True)
        start_sc[...] = jnp.sum(jnp.where(c < r, as_row, 0.0), axis=1, keepdims=True)
        run_sc[...] = jnp.zeros(run_sc.shape, F32)

    before = jnp.dot(mask.astype(BF16), tri_ref[...], preferred_element_type=F32) - mask
    slot = start_sc[...] + run_sc[...] + before
    dest_ref[0] = pick_rows(slot).astype(jnp.int32)
    run_sc[...] = run_sc[...] + tile_counts


SLOT_BLOCK = 512


def route(logits_t, bias):
    E, N = logits_t.shape
    tn = 1024
    tri = (jnp.arange(tn)[:, None] <= jnp.arange(tn)[None, :]).astype(BF16)
    tile = lambda r: pl.BlockSpec((r, tn), lambda p, i: (0, i))
    plane = lambda: pl.BlockSpec((1, TOP_K, tn), lambda p, i: (p, 0, i))
    dest, w, cnt = pl.pallas_call(
        functools.partial(_route_kernel, slot_block=SLOT_BLOCK),
        grid=(2, N // tn),
        in_specs=[tile(E), pl.BlockSpec((E, 1), lambda p, i: (0, 0)), pl.BlockSpec((tn, tn), lambda p, i: (0, 0))],
        out_specs=[plane(), plane(), pl.BlockSpec((E, 128), lambda p, i: (0, 0))],
        out_shape=[
            jax.ShapeDtypeStruct((2, TOP_K, N), jnp.int32),
            jax.ShapeDtypeStruct((2, TOP_K, N), F32),
            jax.ShapeDtypeStruct((E, 128), jnp.int32),
        ],
        scratch_shapes=[pltpu.VMEM((E, 1), F32), pltpu.VMEM((E, 1), F32)],
        compiler_params=_cp(("arbitrary", "arbitrary")),
        name="route",
    )(logits_t, bias.reshape(E, 1), tri)
    return dest[1], w[1], cnt[:, 0]


def block_tables(counts, n_tokens):
    E = counts.shape[0]
    blk = SLOT_BLOCK
    nblk = (n_tokens * TOP_K + E * blk) // blk
    per_expert = (counts + blk - 1) // blk
    bend = jnp.cumsum(per_expert)
    b = jnp.arange(nblk, dtype=jnp.int32)
    blk_e = jnp.minimum(jnp.sum(bend[None, :] <= b[:, None], axis=1), E - 1).astype(jnp.int32)
    within = b - (bend - per_expert)[blk_e]
    nvalid = jnp.where(b < bend[-1], jnp.clip(counts[blk_e] - within * blk, 0, blk), 0)
    return blk_e, nvalid.astype(jnp.int32)


def _sc_mesh():
    return plsc.VectorSubcoreMesh(core_axis_name="c", subcore_axis_name="s")


SC_WINDOW = 128


def sc_scatter_rows(x, dest, n_slots):
    N, W = x.shape
    K = dest.shape[0]

    @functools.partial(pl.kernel, out_type=jax.ShapeDtypeStruct((n_slots, W), x.dtype), mesh=_sc_mesh(),
                       scratch_types=[])
    def scatter(x_hbm, i_hbm, o_hbm):
        def body(x_vmem, i_vmem):
            for k in range(K):
                pltpu.sync_copy(x_vmem, o_hbm.at[i_vmem.at[k]])

        pltpu.emit_pipeline(
            body,
            grid=(N // SC_WINDOW,),
            in_specs=[pl.BlockSpec((SC_WINDOW, W), lambda i: (i, 0)),
                      pl.BlockSpec((K, SC_WINDOW), lambda i: (0, i))],
            out_specs=[],
            core_axis_name=("c", "s"),
            dimension_semantics=(pltpu.PARALLEL,),
        )(x_hbm, i_hbm)

    return scatter(x, dest)


def sc_gather_rows(y, dest):
    W = y.shape[1]
    K, N = dest.shape

    @functools.partial(pl.kernel, out_type=jax.ShapeDtypeStruct((K, N, W), y.dtype), mesh=_sc_mesh(),
                       scratch_types=[])
    def gather(y_hbm, i_hbm, o_hbm):
        def body(i_vmem, o_vmem):
            pltpu.sync_copy(y_hbm.at[i_vmem.at[0, 0]], o_vmem.at[0])

        pltpu.emit_pipeline(
            body,
            grid=(K, N // SC_WINDOW),
            in_specs=[pl.BlockSpec((1, 1, SC_WINDOW), lambda k, i: (k, 0, i))],
            out_specs=[pl.BlockSpec((1, SC_WINDOW, W), lambda k, i: (k, i, 0))],
            core_axis_name=("c", "s"),
            dimension_semantics=(pltpu.PARALLEL, pltpu.PARALLEL),
        )(i_hbm, o_hbm)

    return gather(y, dest.reshape(K, 1, N))


def _expert_kernel(blk_e_ref, nvalid_ref, xa_ref, xb_ref, w1_ref, w3_ref, w2_ref, ya_ref, yb_ref,
                   w1_sc, w3_sc, w2_sc):
    b = pl.program_id(0)
    nv = nvalid_ref[b]
    prev_e = blk_e_ref[jnp.maximum(b - 1, 0)]

    @pl.when((b == 0) | (blk_e_ref[b] != prev_e))
    def _():
        w1_sc[...] = w1_ref[0, 0].astype(BF16)
        w3_sc[...] = w3_ref[0, 0].astype(BF16)
        w2_sc[...] = w2_ref[0, 0].astype(BF16)

    @pl.when(nv > 0)
    def _():
        x = _unpack_row_halves(xa_ref[...], xb_ref[...])
        rows = lax.broadcasted_iota(jnp.int32, x.shape, 0)
        x = jnp.where(rows < nv, x, 0.0).astype(BF16)
        hid = _silu(jnp.dot(x, w1_sc[...], preferred_element_type=F32)) * jnp.dot(
            x, w3_sc[...], preferred_element_type=F32)
        y = jnp.dot(hid.astype(BF16), w2_sc[...], preferred_element_type=F32)
        ya_ref[...], yb_ref[...] = _pack_row_halves(y)

    @pl.when(nv == 0)
    def _():
        ya_ref[...] = jnp.zeros(ya_ref.shape, ya_ref.dtype)
        yb_ref[...] = jnp.zeros(yb_ref.shape, yb_ref.dtype)


def routed_experts(xa, xb, blk_e, nvalid, w1, w3, w2, layer):
    P = xa.shape[0]
    blk = SLOT_BLOCK
    _, E, D, FF = w1.shape
    slots = lambda: pl.BlockSpec((blk, PACK_W), lambda b, be, nv: (b, 0))
    grid_spec = pltpu.PrefetchScalarGridSpec(
        num_scalar_prefetch=2,
        grid=(P // blk,),
        in_specs=[
            slots(), slots(),
            pl.BlockSpec((1, 1, D, FF), lambda b, be, nv: (layer, be[b], 0, 0)),
            pl.BlockSpec((1, 1, D, FF), lambda b, be, nv: (layer, be[b], 0, 0)),
            pl.BlockSpec((1, 1, FF, D), lambda b, be, nv: (layer, be[b], 0, 0)),
        ],
        out_specs=[slots(), slots()],
        scratch_shapes=[pltpu.VMEM((D, FF), BF16), pltpu.VMEM((D, FF), BF16), pltpu.VMEM((FF, D), BF16)],
    )
    return pl.pallas_call(
        _expert_kernel,
        grid_spec=grid_spec,
        out_shape=[jax.ShapeDtypeStruct((P, PACK_W), jnp.int32)] * 2,
        compiler_params=_cp(("arbitrary",), VMEM_LIMIT),
        name="routed_experts",
    )(blk_e, nvalid, xa, xb, w1, w3, w2)


def _combine_kernel(xmid_ref, oa_ref, ob_ref, w_ref, mod2_ref, fg_ref, out_ref, *, final):
    D = xmid_ref.shape[1]
    w = w_ref[...]
    acc = w[:, 0:1] * _unpack_row_halves(oa_ref[0], ob_ref[0])
    for k in range(1, TOP_K):
        acc = acc + w[:, k:k + 1] * _unpack_row_halves(oa_ref[k], ob_ref[k])
    x = xmid_ref[...] + mod2_ref[0][:, 2 * D:] * acc
    if final:
        x = x * lax.rsqrt(jnp.mean(x * x, axis=-1, keepdims=True) + EPS) * fg_ref[...]
    out_ref[...] = x


def combine(xmid, oa, ob, w_tok, mod2, final_g, seq, final):
    N, D = xmid.shape
    tm = 256
    tpb = seq // tm
    rows8 = lambda: pl.BlockSpec((TOP_K, tm, PACK_W), lambda i: (0, i, 0))
    return pl.pallas_call(
        functools.partial(_combine_kernel, final=final),
        grid=(N // tm,),
        in_specs=[
            pl.BlockSpec((tm, D), lambda i: (i, 0)),
            rows8(), rows8(),
            pl.BlockSpec((tm, TOP_K), lambda i: (i, 0)),
            pl.BlockSpec((1, 1, 3 * D), lambda i: (i // tpb, 0, 0)),
            pl.BlockSpec((1, D), lambda i: (0, 0)),
        ],
        out_specs=pl.BlockSpec((tm, D), lambda i: (i, 0)),
        out_shape=jax.ShapeDtypeStruct((N, D), F32),
        compiler_params=_cp(("parallel",), VMEM_LIMIT),
        name="combine",
    )(xmid, oa, ob, w_tok, mod2, final_g.reshape(1, D))


TOKEN_STREAMS = 2


def _permute_w_in(w):
    ub = w[:, 3 * DA:3 * DA + DB]
    lat_lo = 3 * DA + DB
    lat_hi = lat_lo + Q_LORA + KV_LORA + QK_ROPE
    lat, gates = w[:, lat_lo:lat_hi], w[:, lat_hi:]
    pad = jnp.zeros((w.shape[0], LAT_W - (lat_hi - lat_lo)), w.dtype)
    parts = [gates, ub, lat, pad]
    for g in range(len(DIL_GROUPS)):
        sl = slice(g * GROUP_W, (g + 1) * GROUP_W)
        parts += [w[:, :DA][:, sl] * (HEAD_DIM_A ** -0.5), w[:, DA:2 * DA][:, sl], w[:, 2 * DA:3 * DA][:, sl]]
    return jnp.concatenate(parts, axis=1).astype(BF16)


def kernel(x, c, positions, ada_mix_w, ada_mix_b, norm_mix_g, w_in, pool_w, pool_scale, cq_norm_g, ckv_norm_g, w_uq, w_ukv, w_oa, w_ob, w_oc, w_out, ada_ffn_w, ada_ffn_b, norm_ffn_g, router_w, router_bias, exp_w1, exp_w3, exp_w2, sh_w1, sh_w3, sh_w2, final_g):
    B, S, D = x.shape
    depth = w_in.shape[0]
    mod_mix = adaln_rows(c, ada_mix_w, ada_mix_b)
    mod_ffn = adaln_rows(c, ada_ffn_w, ada_ffn_b)
    streams = TOKEN_STREAMS if B % TOKEN_STREAMS == 0 else 1
    Bs = B // streams
    Ns = Bs * S
    xs = [x[s * Bs:(s + 1) * Bs].reshape(Ns, D) for s in range(streams)]
    pos_cols = [positions[s * Bs:(s + 1) * Bs].reshape(Ns, 1) for s in range(streams)]
    for l in range(depth):
        w_in_l = _permute_w_in(w_in[l])
        mla_w = _mla_weights(cq_norm_g[l], ckv_norm_g[l], w_uq[l], w_ukv[l])
        mix_w = (norm_ffn_g[l], pool_w[l].astype(BF16), pool_scale[l],
                 w_oa[l].astype(BF16), w_ob[l].astype(BF16), w_oc[l].astype(BF16), w_out[l].astype(BF16),
                 router_w[l].T.astype(BF16), sh_w1[l].astype(BF16), sh_w3[l].astype(BF16), sh_w2[l].astype(BF16))
        for s in range(streams):
            x2 = xs[s]
            mod1 = mod_mix[l, s * Bs:(s + 1) * Bs].reshape(Bs, 1, 3 * D)
            mod2 = mod_ffn[l, s * Bs:(s + 1) * Bs].reshape(Bs, 1, 3 * D)
            gu, lat, *qkv = in_projection(x2, norm_mix_g[l], mod1, w_in_l, S)
            dil = [dilated_attention(qkv[2 * g], qkv[2 * g + 1], Bs, S, d) for g, (_, d) in enumerate(DIL_GROUPS)]
            q_all, k_all, vt_all = mla_prep(lat, pos_cols[s], *mla_w, Bs, S)
            yc = mla_attention(q_all, k_all, vt_all, Bs, S)
            xmid, h2a, h2b, logits_t = mix_out(x2, gu, dil, yc, mod1, mod2, *mix_w, S)
            dest, w_k, counts = route(logits_t, router_bias[l])
            blk_e, nvalid = block_tables(counts, Ns)
            n_slots = blk_e.shape[0] * SLOT_BLOCK
            xa = sc_scatter_rows(h2a, dest, n_slots)
            xb = sc_scatter_rows(h2b, dest, n_slots)
            ya, yb = routed_experts(xa, xb, blk_e, nvalid, exp_w1, exp_w3, exp_w2, l)
            oa = sc_gather_rows(ya, dest)
            ob = sc_gather_rows(yb, dest)
            xs[s] = combine(xmid, oa, ob, w_k.T, mod2, final_g, S, final=(l == depth - 1))
    return jnp.concatenate(xs, axis=0).reshape(B, S, D)
```

```python
import functools
import math

import jax
import jax.numpy as jnp
from jax import lax
from jax.experimental import pallas as pl
from jax.experimental.pallas import tpu as pltpu
from jax.experimental.pallas import tpu_sc as plsc

F32 = jnp.float32
BF16 = jnp.bfloat16
HIGHEST = lax.Precision.HIGHEST

D_MODEL = 1024
HEAD_DIM_A = 64
HEADS_PER_GROUP_A = 4
DIL_GROUPS = ((128, 1), (512, 4), (2048, 16))
GROUP_W = HEADS_PER_GROUP_A * HEAD_DIM_A
DA = GROUP_W * len(DIL_GROUPS)
POOL_WINDOWS = (2, 4, 8, 16)
POOL_GROUP_DIM = 128
DB = POOL_GROUP_DIM * len(POOL_WINDOWS)
POOL_HALO = 16
N_HEADS_C = 8
QK_NOPE = 64
QK_ROPE = 32
V_DIM = 64
Q_LORA = 384
KV_LORA = 256
DC = N_HEADS_C * V_DIM
HEAD_PAD_C = 128
ROPE_THETA = 10000.0
N_EXPERTS = 64
TOP_K = 8
N_GROUPS = 8
TOPK_GROUPS = 4
GROUP_SIZE = N_EXPERTS // N_GROUPS
EXPERT_FF = 256
ROUTED_SCALE = 2.5
EPS = 1e-6
NEG = -1e30
Q_BLOCK = 128

LAT_W = 768
GU_W = 3 * D_MODEL + DB
IN_OUT_WIDTHS = (GU_W, LAT_W) + (2 * GROUP_W, GROUP_W) * len(DIL_GROUPS)

VMEM_LIMIT = 56 * 1024 * 1024


def _cp(sem, vmem=None):
    return pltpu.CompilerParams(dimension_semantics=sem, vmem_limit_bytes=vmem)


def _silu(v):
    return v * jax.nn.sigmoid(v)


def _nt_dot(a, b):
    return lax.dot_general(a, b, (((1,), (1,)), ((), ())), preferred_element_type=F32)


PACK_W = D_MODEL // 4
_HI_MASK = -65536


def _bf16_bits(v):
    return lax.bitcast_convert_type(v.astype(BF16).astype(F32), jnp.int32)


def _pack_row_halves(v):
    halves = []
    for h in range(2):
        lo = _bf16_bits(v[:, (2 * h) * PACK_W:(2 * h + 1) * PACK_W])
        hi = _bf16_bits(v[:, (2 * h + 1) * PACK_W:(2 * h + 2) * PACK_W])
        halves.append(lax.shift_right_logical(lo, 16) | (hi & _HI_MASK))
    return halves


def _unpack_row_halves(wa, wb):
    parts = []
    for w in (wa, wb):
        parts.append(lax.bitcast_convert_type(lax.shift_left(w, 16), F32))
        parts.append(lax.bitcast_convert_type(w & _HI_MASK, F32))
    return jnp.concatenate(parts, axis=1)


def _adaln_kernel(c_ref, w_ref, b_ref, o_ref):
    s = _silu(c_ref[...])
    o_ref[0] = jnp.dot(s, w_ref[0], preferred_element_type=F32, precision=HIGHEST) + b_ref[0]


def adaln_rows(c, w, b):
    L, D, D3 = w.shape
    B = c.shape[0]
    tn = 1024
    return pl.pallas_call(
        _adaln_kernel,
        grid=(L, D3 // tn),
        in_specs=[
            pl.BlockSpec((B, D), lambda l, j: (0, 0)),
            pl.BlockSpec((1, D, tn), lambda l, j: (l, 0, j)),
            pl.BlockSpec((1, 1, tn), lambda l, j: (l, 0, j)),
        ],
        out_specs=pl.BlockSpec((1, B, tn), lambda l, j: (l, 0, j)),
        out_shape=jax.ShapeDtypeStruct((L, B, D3), F32),
        compiler_params=_cp(("parallel", "parallel")),
        name="adaln_rows",
    )(c, w, b.reshape(L, 1, D3))


def _inproj_kernel(x_ref, g_ref, mod_ref, w_ref, *o_refs, chunk):
    D = x_ref.shape[1]
    x = x_ref[...]
    y = x * lax.rsqrt(jnp.mean(x * x, axis=-1, keepdims=True) + EPS) * g_ref[...]
    mod = mod_ref[0]
    h = (y * (1.0 + mod[:, D:2 * D]) + mod[:, :D]).astype(BF16)
    col = 0
    for o_ref in o_refs:
        width = o_ref.shape[1]
        for c0 in range(0, width, chunk):
            cw = min(chunk, width - c0)
            o_ref[:, c0:c0 + cw] = jnp.dot(
                h, w_ref[:, col + c0:col + c0 + cw], preferred_element_type=F32).astype(o_ref.dtype)
        col += width


def in_projection(x2, g, mod, w, seq, row0=0):
    D = x2.shape[1]
    N = mod.shape[0] * seq
    tm = 512
    tpb = seq // tm
    tile0 = row0 // tm
    return pl.pallas_call(
        functools.partial(_inproj_kernel, chunk=512),
        grid=(N // tm,),
        in_specs=[
            pl.BlockSpec((tm, D), lambda i: (i + tile0, 0)),
            pl.BlockSpec((1, D), lambda i: (0, 0)),
            pl.BlockSpec((1, 1, 3 * D), lambda i: (i // tpb, 0, 0)),
            pl.BlockSpec(w.shape, lambda i: (0, 0), pipeline_mode=pl.Buffered(1)),
        ],
        out_specs=[pl.BlockSpec((tm, wd), lambda i: (i, 0)) for wd in IN_OUT_WIDTHS],
        out_shape=[jax.ShapeDtypeStruct((N, wd), BF16) for wd in IN_OUT_WIDTHS],
        compiler_params=_cp(("parallel",), VMEM_LIMIT),
        name="in_projection",
    )(x2, g.reshape(1, D), mod, w)


def _dilated_kernel(q_ref, kc_ref, kp_ref, vtc_ref, vtp_ref, ot_ref, lse_ref):
    i = pl.program_id(1)
    T = Q_BLOCK
    key = lax.broadcasted_iota(jnp.int32, (T, T), 0)
    qry = lax.broadcasted_iota(jnp.int32, (T, T), 1)
    valid_c = key <= qry
    near = key >= qry
    run = q_ref.shape[1] // T
    heads = [slice(h * HEAD_DIM_A, (h + 1) * HEAD_DIM_A) for h in range(HEADS_PER_GROUP_A)]

    def blocks(j):
        rows = slice(j * T, (j + 1) * T)
        if j == 0:
            return rows, kc_ref[0, rows, :], vtc_ref[0, :, rows], kp_ref[0], vtp_ref[0], near & (i > 0)
        before = slice((j - 1) * T, j * T)
        return rows, kc_ref[0, rows, :], vtc_ref[0, :, rows], kc_ref[0, before, :], vtc_ref[0, :, before], near

    scores, probs = {}, {}
    for j in range(run):
        rows, kc, _, kp, _, valid_p = blocks(j)
        q = q_ref[0, rows, :]
        for h, sl in enumerate(heads):
            qh = q[:, sl]
            scores[j, h] = (jnp.where(valid_c, _nt_dot(kc[:, sl], qh), NEG),
                            jnp.where(valid_p, _nt_dot(kp[:, sl], qh), NEG))
    for (j, h), (sc, sp) in scores.items():
        m = jnp.maximum(jnp.max(sc, axis=0, keepdims=True), jnp.max(sp, axis=0, keepdims=True))
        pc = jnp.exp(sc - m)
        pp = jnp.exp(sp - m)
        den = jnp.sum(pc, axis=0, keepdims=True) + jnp.sum(pp, axis=0, keepdims=True)
        probs[j, h] = (pc.astype(BF16), pp.astype(BF16), den, m + jnp.log(den))
    for j in range(run):
        rows, _, vtc, _, vtp, _ = blocks(j)
        outs = []
        for h, sl in enumerate(heads):
            pc, pp, den, _ = probs[j, h]
            o = jnp.dot(vtc[sl, :], pc, preferred_element_type=F32) + jnp.dot(vtp[sl, :], pp, preferred_element_type=F32)
            outs.append(o / den)
        ot_ref[0, :, rows] = jnp.concatenate(outs, axis=0).astype(ot_ref.dtype)
        lse_ref[0, :, rows] = jnp.concatenate([probs[j, h][3] for h in range(len(heads))], axis=0)


DILATED_RUN = 4


def dilated_attention(qk, v, batch, seq, dilation):
    L = seq // dilation
    nb = L // Q_BLOCK
    run = min(DILATED_RUN, nb)
    H = HEADS_PER_GROUP_A
    qk_r = qk.reshape(batch, L, dilation, 2 * GROUP_W).transpose(0, 2, 1, 3).reshape(batch * dilation, L, 2 * GROUP_W)
    vt_r = v.reshape(batch, L, dilation, GROUP_W).transpose(0, 2, 3, 1).reshape(batch * dilation, GROUP_W, L)
    before = lambda i: jnp.maximum(i * run - 1, 0)
    ot, lse = pl.pallas_call(
        _dilated_kernel,
        grid=(batch * dilation, nb // run),
        in_specs=[
            pl.BlockSpec((1, run * Q_BLOCK, GROUP_W), lambda s, i: (s, i, 0)),
            pl.BlockSpec((1, run * Q_BLOCK, GROUP_W), lambda s, i: (s, i, 1)),
            pl.BlockSpec((1, Q_BLOCK, GROUP_W), lambda s, i: (s, before(i), 1)),
            pl.BlockSpec((1, GROUP_W, run * Q_BLOCK), lambda s, i: (s, 0, i)),
            pl.BlockSpec((1, GROUP_W, Q_BLOCK), lambda s, i: (s, 0, before(i))),
        ],
        out_specs=[
            pl.BlockSpec((1, GROUP_W, run * Q_BLOCK), lambda s, i: (s, 0, i)),
            pl.BlockSpec((1, H, run * Q_BLOCK), lambda s, i: (s, 0, i)),
        ],
        out_shape=[
            jax.ShapeDtypeStruct((batch * dilation, GROUP_W, L), BF16),
            jax.ShapeDtypeStruct((batch * dilation, H, L), F32),
        ],
        compiler_params=_cp(("parallel", "parallel")),
        name=f"dilated_attention_d{dilation}",
    )(qk_r, qk_r, qk_r, vt_r, vt_r)
    o = ot.reshape(batch, dilation, GROUP_W, L).transpose(0, 3, 1, 2).reshape(batch * seq, GROUP_W)
    lse = lse.reshape(batch, dilation, H, L).transpose(0, 3, 1, 2).reshape(batch * seq, H)
    return o, jnp.repeat(lse, HEAD_DIM_A, axis=1)


def _mla_prep_kernel(lat_ref, pos_ref, gq_ref, gkv_ref, wq_ref, wk_ref, wvt_ref, freq_ref,
                     q_ref, k_ref, vt_ref):
    HP = N_HEADS_C * HEAD_PAD_C
    lat = lat_ref[...].astype(F32)
    cq = lat[:, :Q_LORA]
    ckr = lat[:, Q_LORA:]
    zq = (cq * lax.rsqrt(jnp.mean(cq * cq, axis=-1, keepdims=True) + EPS) * gq_ref[...]).astype(BF16)
    lane = lax.broadcasted_iota(jnp.int32, ckr.shape, 1)
    is_kv = lane < KV_LORA
    ms = jnp.sum(jnp.where(is_kv, ckr * ckr, 0.0), axis=-1, keepdims=True) * (1.0 / KV_LORA)
    zkv = (ckr * jnp.where(is_kv, lax.rsqrt(ms + EPS) * gkv_ref[...], 1.0)).astype(BF16)
    qq = jnp.dot(zq, wq_ref[...], preferred_element_type=F32)
    kk = jnp.dot(zkv, wk_ref[...], preferred_element_type=F32)
    ang = pos_ref[...].astype(F32) * freq_ref[...]
    cos, sin = jnp.cos(ang), jnp.sin(ang)
    for h in range(N_HEADS_C):
        lo, hi = h * HEAD_PAD_C, (h + 1) * HEAD_PAD_C
        q_ref[:, lo:hi] = (qq[:, lo:hi] * cos + qq[:, HP + lo:HP + hi] * sin).astype(q_ref.dtype)
        k_ref[:, lo:hi] = (kk[:, lo:hi] * cos + kk[:, HP + lo:HP + hi] * sin).astype(k_ref.dtype)
    vt_ref[0] = _nt_dot(wvt_ref[...], zkv).astype(vt_ref.dtype)


def _mla_weights(cq_g, ckv_g, w_uq, w_ukv):
    H, HPAD, half = N_HEADS_C, HEAD_PAD_C, QK_ROPE // 2
    scale = (QK_NOPE + QK_ROPE) ** -0.5 * math.log2(math.e)
    wq = w_uq.reshape(Q_LORA, H, QK_NOPE + QK_ROPE) * scale
    q_lin = jnp.pad(wq, ((0, 0), (0, 0), (0, HPAD - QK_NOPE - QK_ROPE)))
    r1, r2 = wq[..., QK_NOPE:QK_NOPE + half], wq[..., QK_NOPE + half:]
    q_sw = jnp.concatenate([jnp.zeros((Q_LORA, H, QK_NOPE), F32), -r2, r1,
                            jnp.zeros((Q_LORA, H, HPAD - QK_NOPE - QK_ROPE), F32)], axis=-1)
    wq_big = jnp.concatenate([q_lin.reshape(Q_LORA, H * HPAD), q_sw.reshape(Q_LORA, H * HPAD)], axis=1)

    rows = LAT_W - Q_LORA
    wkv = w_ukv.reshape(KV_LORA, H, QK_NOPE + V_DIM)
    eye = jnp.eye(QK_ROPE, dtype=F32)
    k_lin = jnp.zeros((rows, H, HPAD), F32)
    k_lin = k_lin.at[:KV_LORA, :, :QK_NOPE].set(wkv[..., :QK_NOPE])
    k_lin = k_lin.at[KV_LORA:KV_LORA + QK_ROPE, :, QK_NOPE:QK_NOPE + QK_ROPE].set(
        jnp.broadcast_to(eye[:, None, :], (QK_ROPE, H, QK_ROPE)))
    swap = jnp.zeros((QK_ROPE, QK_ROPE), F32).at[half:, :half].set(-jnp.eye(half)).at[:half, half:].set(jnp.eye(half))
    k_sw = jnp.zeros((rows, H, HPAD), F32)
    k_sw = k_sw.at[KV_LORA:KV_LORA + QK_ROPE, :, QK_NOPE:QK_NOPE + QK_ROPE].set(
        jnp.broadcast_to(swap[:, None, :], (QK_ROPE, H, QK_ROPE)))
    v_w = jnp.zeros((rows, H, V_DIM), F32).at[:KV_LORA].set(wkv[..., QK_NOPE:])
    wk_big = jnp.concatenate([k_lin.reshape(rows, H * HPAD), k_sw.reshape(rows, H * HPAD)], axis=1)
    wv_t = v_w.reshape(rows, H * V_DIM).T

    gkv = jnp.concatenate([ckv_g, jnp.ones((rows - KV_LORA,), F32)]).reshape(1, rows)
    return cq_g.reshape(1, Q_LORA), gkv, wq_big.astype(BF16), wk_big.astype(BF16), wv_t.astype(BF16)


def _rope_lane_freqs():
    half = QK_ROPE // 2
    freqs = ROPE_THETA ** (-jnp.arange(0, QK_ROPE, 2, dtype=F32) / QK_ROPE)
    row = jnp.zeros((HEAD_PAD_C,), F32)
    row = row.at[QK_NOPE:QK_NOPE + half].set(freqs).at[QK_NOPE + half:QK_NOPE + QK_ROPE].set(freqs)
    return row.reshape(1, HEAD_PAD_C)


def mla_prep(lat, pos_col, gq, gkv, wq_big, wk_big, wv_t, batch, seq):
    N = lat.shape[0]
    HP = N_HEADS_C * HEAD_PAD_C
    tm = 512
    tpb = seq // tm
    freqs = _rope_lane_freqs()
    const = lambda shape: pl.BlockSpec(shape, lambda i: (0, 0))
    return pl.pallas_call(
        _mla_prep_kernel,
        grid=(N // tm,),
        in_specs=[
            pl.BlockSpec((tm, LAT_W), lambda i: (i, 0)),
            pl.BlockSpec((tm, 1), lambda i: (i, 0)),
            const(gq.shape), const(gkv.shape), const(wq_big.shape), const(wk_big.shape), const(wv_t.shape),
            const(freqs.shape),
        ],
        out_specs=[
            pl.BlockSpec((tm, HP), lambda i: (i, 0)),
            pl.BlockSpec((tm, HP), lambda i: (i, 0)),
            pl.BlockSpec((1, DC, tm), lambda i: (i // tpb, 0, i % tpb)),
        ],
        out_shape=[
            jax.ShapeDtypeStruct((N, HP), BF16),
            jax.ShapeDtypeStruct((N, HP), BF16),
            jax.ShapeDtypeStruct((batch, DC, seq), BF16),
        ],
        compiler_params=_cp(("parallel",), VMEM_LIMIT),
        name="mla_prep",
    )(lat, pos_col, gq, gkv, wq_big, wk_big, wv_t, freqs)


HEADS_PER_STEP_C = 8
FLASH_Q_CHUNK = 256


def _mla_flash_kernel(qi_ref, ki_ref, q_ref, k_ref, vt_ref, o_ref, m_sc, l_sc, acc_sc):
    t = pl.program_id(2)
    qi, ki = qi_ref[t], ki_ref[t]

    @pl.when(ki == 0)
    def _():
        m_sc[...] = jnp.full(m_sc.shape, NEG, F32)
        l_sc[...] = jnp.zeros(l_sc.shape, F32)
        acc_sc[...] = jnp.zeros(acc_sc.shape, F32)

    def step(masked):
        T = q_ref.shape[1]
        if masked:
            key = lax.broadcasted_iota(jnp.int32, (T, T), 0)
            qry = lax.broadcasted_iota(jnp.int32, (T, T), 1)
            keep = key <= qry
        chains = [(h, c) for h in range(HEADS_PER_STEP_C) for c in range(T // FLASH_Q_CHUNK)]
        scores, probs, alphas = {}, {}, {}
        for h, c in chains:
            qs = slice(c * FLASH_Q_CHUNK, (c + 1) * FLASH_Q_CHUNK)
            q = q_ref[0, qs, h * HEAD_PAD_C:(h + 1) * HEAD_PAD_C]
            k = k_ref[0, :, h * HEAD_PAD_C:(h + 1) * HEAD_PAD_C]
            st = _nt_dot(k, q)
            scores[h, c] = jnp.where(keep[:, qs], st, NEG) if masked else st
        for h, c in chains:
            qs = slice(c * FLASH_Q_CHUNK, (c + 1) * FLASH_Q_CHUNK)
            st = scores[h, c]
            m_prev = m_sc[h, :, qs]
            m_new = jnp.maximum(m_prev, jnp.max(st, axis=0, keepdims=True))
            alpha = jnp.exp2(m_prev - m_new)
            p = jnp.exp2(st - m_new)
            l_sc[h, :, qs] = alpha * l_sc[h, :, qs] + jnp.sum(p, axis=0, keepdims=True)
            m_sc[h, :, qs] = m_new
            probs[h, c], alphas[h, c] = p.astype(BF16), alpha
        for h, c in chains:
            qs = slice(c * FLASH_Q_CHUNK, (c + 1) * FLASH_Q_CHUNK)
            vt = vt_ref[0, h * V_DIM:(h + 1) * V_DIM, :]
            acc_sc[h, :, qs] = alphas[h, c] * acc_sc[h, :, qs] + jnp.dot(
                vt, probs[h, c], preferred_element_type=F32)

    @pl.when(ki < qi)
    def _():
        step(False)

    @pl.when(ki == qi)
    def _():
        step(True)
        ot = jnp.concatenate([acc_sc[h] / l_sc[h] for h in range(HEADS_PER_STEP_C)], axis=0)
        o_ref[0] = ot.T.astype(o_ref.dtype)


def mla_attention(q_all, k_all, vt_all, batch, seq):
    T = 512
    nq = seq // T
    pairs = [(a, b) for a in range(nq) for b in range(a + 1)]
    qi_tab = jnp.asarray([p[0] for p in pairs], jnp.int32)
    ki_tab = jnp.asarray([p[1] for p in pairs], jnp.int32)
    hp = N_HEADS_C // HEADS_PER_STEP_C
    qw = HEADS_PER_STEP_C * HEAD_PAD_C
    vw = HEADS_PER_STEP_C * V_DIM
    q3 = q_all.reshape(batch, seq, -1)
    k3 = k_all.reshape(batch, seq, -1)
    grid_spec = pltpu.PrefetchScalarGridSpec(
        num_scalar_prefetch=2,
        grid=(batch, hp, len(pairs)),
        in_specs=[
            pl.BlockSpec((1, T, qw), lambda b, h, t, qi, ki: (b, qi[t], h)),
            pl.BlockSpec((1, T, qw), lambda b, h, t, qi, ki: (b, ki[t], h)),
            pl.BlockSpec((1, vw, T), lambda b, h, t, qi, ki: (b, h, ki[t])),
        ],
        out_specs=pl.BlockSpec((1, T, vw), lambda b, h, t, qi, ki: (b, qi[t], h)),
        scratch_shapes=[
            pltpu.VMEM((HEADS_PER_STEP_C, 1, T), F32),
            pltpu.VMEM((HEADS_PER_STEP_C, 1, T), F32),
            pltpu.VMEM((HEADS_PER_STEP_C, V_DIM, T), F32),
        ],
    )
    o = pl.pallas_call(
        _mla_flash_kernel,
        grid_spec=grid_spec,
        out_shape=jax.ShapeDtypeStruct((batch, seq, DC), BF16),
        compiler_params=_cp(("parallel", "parallel", "arbitrary")),
        name="mla_attention",
    )(qi_tab, ki_tab, q3, k3, vt_all)
    return o.reshape(batch * seq, DC)


def _mixout_kernel(x_ref, gates_ref, ub_ref, ubh_ref, o1_ref, o2_ref, o3_ref, l1_ref, l2_ref, l3_ref, yc_ref,
                   mod1_ref, mod2_ref, g2_ref, poolw_ref, pscale_ref, woa_ref, wob_ref, woc_ref, wout_ref,
                   rwt_ref, sw1_ref, sw3_ref, sw2_ref,
                   xmid_ref, h2a_ref, h2b_ref, logit_ref, *, tiles_per_batch):
    D = x_ref.shape[1]
    tm = x_ref.shape[0]
    tile = pl.program_id(0) % tiles_per_batch

    l1, l2, l3 = l1_ref[...], l2_ref[...], l3_ref[...]
    mx = jnp.maximum(jnp.maximum(l1, l2), l3)
    e1, e2, e3 = jnp.exp(l1 - mx), jnp.exp(l2 - mx), jnp.exp(l3 - mx)
    ya = (e1 * o1_ref[...].astype(F32) + e2 * o2_ref[...].astype(F32) + e3 * o3_ref[...].astype(F32)) / (e1 + e2 + e3)
    a_out = jnp.dot(ya.astype(BF16), woa_ref[...], preferred_element_type=F32)

    u = ub_ref[...].astype(F32)
    halo = jnp.where(tile > 0, ubh_ref[...].astype(F32), 0.0)
    ext = jnp.concatenate([halo, u], axis=0)
    t_seq = tile * tm + lax.broadcasted_iota(jnp.int32, (tm, 1), 0)
    pooled = []
    for gi, w in enumerate(POOL_WINDOWS):
        sl = slice(gi * POOL_GROUP_DIM, (gi + 1) * POOL_GROUP_DIM)
        acc = ext[:, sl]
        k = 1
        while k < w:
            acc = acc + pltpu.roll(acc, k, axis=0)
            k *= 2
        cnt = jnp.minimum(t_seq + 1, w).astype(F32)
        pg = acc[POOL_HALO:] / cnt - u[:, sl]
        pooled.append(jnp.dot(pg.astype(BF16), poolw_ref[gi], preferred_element_type=F32))
    yb = jnp.concatenate(pooled, axis=1) * pscale_ref[...]
    b_out = jnp.dot(yb.astype(BF16), wob_ref[...], preferred_element_type=F32)
    c_out = jnp.dot(yc_ref[...], woc_ref[...], preferred_element_type=F32)

    g = gates_ref[...].astype(F32)
    mix = (jax.nn.sigmoid(g[:, :D]) * a_out + jax.nn.sigmoid(g[:, D:2 * D]) * b_out
           + jax.nn.sigmoid(g[:, 2 * D:]) * c_out)
    tok = jnp.dot(mix.astype(BF16), wout_ref[...], preferred_element_type=F32)
    xn = x_ref[...] + mod1_ref[0][:, 2 * D:] * tok

    mod2 = mod2_ref[0]
    y = xn * lax.rsqrt(jnp.mean(xn * xn, axis=-1, keepdims=True) + EPS) * g2_ref[...]
    h2 = y * (1.0 + mod2[:, D:2 * D]) + mod2[:, :D]
    h2b = h2.astype(BF16)
    h2a_ref[...], h2b_ref[...] = _pack_row_halves(h2b)
    logit_ref[...] = _nt_dot(rwt_ref[...], h2b)
    hid = _silu(jnp.dot(h2b, sw1_ref[...], preferred_element_type=F32)) * jnp.dot(
        h2b, sw3_ref[...], preferred_element_type=F32)
    shared = jnp.dot(hid.astype(BF16), sw2_ref[...], preferred_element_type=F32)
    xmid_ref[...] = xn + mod2[:, 2 * D:] * shared


def mix_out(x2, gu, dil, yc, mod1, mod2, g2, pool_w, pool_scale, w_oa, w_ob, w_oc, w_out, rwt, sw1, sw3, sw2, seq,
            row0=0):
    D = x2.shape[1]
    N = gu.shape[0]
    tm = 256
    tpb = seq // tm
    tile0 = row0 // tm
    (o1, l1), (o2, l2), (o3, l3) = dil
    row = lambda w, c=0: pl.BlockSpec((tm, w), lambda i: (i, c))
    const2 = lambda a: pl.BlockSpec(a.shape, lambda i: (0,) * a.ndim)
    modspec = pl.BlockSpec((1, 1, 3 * D), lambda i: (i // tpb, 0, 0))
    ub_col = 3 * D // DB
    halo_spec = pl.BlockSpec(
        (POOL_HALO, DB), lambda i: (jnp.maximum(i * (tm // POOL_HALO) - 1, 0), ub_col))
    weights = [g2.reshape(1, D), pool_w, pool_scale.reshape(1, DB), w_oa, w_ob, w_oc, w_out, rwt, sw1, sw3, sw2]
    return pl.pallas_call(
        functools.partial(_mixout_kernel, tiles_per_batch=tpb),
        grid=(N // tm,),
        in_specs=[
            pl.BlockSpec((tm, D), lambda i: (i + tile0, 0)), row(3 * D), row(DB, ub_col), halo_spec,
            row(GROUP_W), row(GROUP_W), row(GROUP_W), row(GROUP_W), row(GROUP_W), row(GROUP_W), row(DC),
            modspec, modspec,
        ] + [const2(a) for a in weights],
        out_specs=[row(D), row(PACK_W), row(PACK_W), pl.BlockSpec((N_EXPERTS, tm), lambda i: (0, i))],
        out_shape=[
            jax.ShapeDtypeStruct((N, D), F32),
            jax.ShapeDtypeStruct((N, PACK_W), jnp.int32),
            jax.ShapeDtypeStruct((N, PACK_W), jnp.int32),
            jax.ShapeDtypeStruct((N_EXPERTS, N), F32),
        ],
        compiler_params=_cp(("parallel",), VMEM_LIMIT),
        name="mix_out",
    )(x2, gu, gu, gu, o1, o2, o3, l1, l2, l3, yc, mod1, mod2, *weights)


def _pick_rows(table, picks):
    G, GS = N_GROUPS, GROUP_SIZE
    eio = lax.broadcasted_iota(jnp.int32, (GS, table.shape[1]), 0)
    rows = []
    for k in range(TOP_K):
        idx = picks[k:k + 1]
        parts = [jnp.sum(jnp.where(eio + g * GS == idx, table[g * GS:(g + 1) * GS], 0.0), axis=0, keepdims=True)
                 for g in range(G)]
        rows.append(functools.reduce(jnp.add, parts))
    return jnp.concatenate(rows, axis=0)


def _route_choose(lg_ref, bias_ref):
    G, GS = N_GROUPS, GROUP_SIZE
    scores = jax.nn.sigmoid(lg_ref[...])
    sel = scores + bias_ref[...]
    tn = sel.shape[1]
    eio = lax.broadcasted_iota(jnp.int32, (GS, tn), 0)
    ninf = -jnp.inf

    gs = []
    for g in range(G):
        v = sel[g * GS:(g + 1) * GS]
        m1 = jnp.max(v, axis=0, keepdims=True)
        i1 = jnp.min(jnp.where(v == m1, eio, GS), axis=0, keepdims=True)
        m2 = jnp.max(jnp.where(eio == i1, ninf, v), axis=0, keepdims=True)
        gs.append(m1 + m2)
    gsm = jnp.concatenate(gs, axis=0)
    gio = lax.broadcasted_iota(jnp.int32, (G, tn), 0)
    rank = jnp.zeros((G, tn), jnp.int32)
    for g2 in range(G):
        beats = (gs[g2] > gsm) | ((gs[g2] == gsm) & (g2 < gio))
        rank = rank + beats.astype(jnp.int32)
    gsel = rank < TOPK_GROUPS

    vs = [jnp.where(gsel[g:g + 1], sel[g * GS:(g + 1) * GS], NEG) for g in range(G)]
    eid = [eio + g * GS for g in range(G)]
    chosen = [jnp.zeros((GS, tn), jnp.bool_) for _ in range(G)]
    picks = []
    for _ in range(TOP_K):
        m = functools.reduce(jnp.maximum, [jnp.max(v, axis=0, keepdims=True) for v in vs])
        idx = functools.reduce(jnp.minimum, [
            jnp.min(jnp.where(v == m, e, N_EXPERTS), axis=0, keepdims=True) for v, e in zip(vs, eid)])
        picks.append(idx)
        for g in range(G):
            hit = eid[g] == idx
            chosen[g] = chosen[g] | hit
            vs[g] = jnp.where(hit, ninf, vs[g])
    mask = jnp.concatenate(chosen, axis=0).astype(F32)
    return scores, jnp.concatenate(picks, axis=0), mask


def _route_kernel(lg_ref, bias_ref, tri_ref, dest_ref, w_ref, cnt_ref, run_sc, start_sc, mask_sc, picks_sc,
                  *, slot_block):
    phase = pl.program_id(0)
    step = pl.program_id(1)
    tn = lg_ref.shape[1]
    cols = pl.ds(pl.multiple_of(step * tn, tn), tn)

    @pl.when(phase == 0)
    def _():
        @pl.when(step == 0)
        def _():
            run_sc[...] = jnp.zeros(run_sc.shape, F32)

        scores, picks, mask = _route_choose(lg_ref, bias_ref)
        wk = _pick_rows(scores, picks)
        w_ref[0] = wk / jnp.sum(wk, axis=0, keepdims=True) * ROUTED_SCALE
        dest_ref[0] = jnp.zeros(dest_ref.shape[1:], dest_ref.dtype)
        mask_sc[:, cols] = mask.astype(BF16)
        picks_sc[:, cols] = picks
        run_sc[...] = run_sc[...] + jnp.sum(mask, axis=1, keepdims=True)

    @pl.when(phase == 1)
    def _():
        @pl.when(step == 0)
        def _():
            counts = run_sc[...].astype(jnp.int32)
            cnt_ref[...] = jnp.broadcast_to(counts, cnt_ref.shape)
            shift = slot_block.bit_length() - 1
            padded = lax.shift_left(lax.shift_right_logical(counts + (slot_block - 1), shift), shift).astype(F32)
            r = lax.broadcasted_iota(jnp.int32, (N_EXPERTS, N_EXPERTS), 0)
            c = lax.broadcasted_iota(jnp.int32, (N_EXPERTS, N_EXPERTS), 1)
            as_row = jnp.sum(jnp.where(r == c, padded, 0.0), axis=0, keepdims=True)
            start_sc[...] = jnp.sum(jnp.where(c < r, as_row, 0.0), axis=1, keepdims=True)
            run_sc[...] = jnp.zeros(run_sc.shape, F32)

        mask_b = mask_sc[:, cols]
        mask = mask_b.astype(F32)
        before = jnp.dot(mask_b, tri_ref[...], preferred_element_type=F32) - mask
        slot = start_sc[...] + run_sc[...] + before
        dest_ref[0] = _pick_rows(slot, picks_sc[:, cols]).astype(jnp.int32)
        w_ref[0] = jnp.zeros(w_ref.shape[1:], w_ref.dtype)
        run_sc[...] = run_sc[...] + jnp.sum(mask, axis=1, keepdims=True)


SLOT_BLOCK = 512


def route(logits_t, bias):
    E, N = logits_t.shape
    tn = 1024
    tri = (jnp.arange(tn)[:, None] <= jnp.arange(tn)[None, :]).astype(BF16)
    plane = lambda: pl.BlockSpec((1, TOP_K, tn), lambda p, i: (p, 0, i))
    dest, w, cnt = pl.pallas_call(
        functools.partial(_route_kernel, slot_block=SLOT_BLOCK),
        grid=(2, N // tn),
        in_specs=[
            pl.BlockSpec((E, tn), lambda p, i: (0, i * (1 - p))),
            pl.BlockSpec((E, 1), lambda p, i: (0, 0)),
            pl.BlockSpec((tn, tn), lambda p, i: (0, 0)),
        ],
        out_specs=[plane(), plane(), pl.BlockSpec((E, 128), lambda p, i: (0, 0))],
        out_shape=[
            jax.ShapeDtypeStruct((2, TOP_K, N), jnp.int32),
            jax.ShapeDtypeStruct((2, TOP_K, N), F32),
            jax.ShapeDtypeStruct((E, 128), jnp.int32),
        ],
        scratch_shapes=[pltpu.VMEM((E, 1), F32), pltpu.VMEM((E, 1), F32),
                        pltpu.VMEM((E, N), BF16), pltpu.VMEM((TOP_K, N), jnp.int32)],
        compiler_params=_cp(("arbitrary", "arbitrary")),
        name="route",
    )(logits_t, bias.reshape(E, 1), tri)
    return dest[1], w[0], cnt[:, 0]


def block_tables(counts, n_tokens):
    E = counts.shape[0]
    blk = SLOT_BLOCK
    nblk = (n_tokens * TOP_K + E * blk) // blk
    per_expert = (counts + blk - 1) // blk
    bend = jnp.cumsum(per_expert)
    bstart = bend - per_expert
    b = jnp.arange(nblk, dtype=jnp.int32)[:, None]
    owns = (bstart[None, :] <= b) & (b < bend[None, :])
    blk_e = jnp.minimum(jnp.sum(bend[None, :] <= b, axis=1), E - 1).astype(jnp.int32)
    rows_left = counts[None, :] - (b - bstart[None, :]) * blk
    nvalid = jnp.sum(jnp.where(owns, jnp.clip(rows_left, 0, blk), 0), axis=1)
    return blk_e, nvalid.astype(jnp.int32)


def _sc_mesh():
    return plsc.VectorSubcoreMesh(core_axis_name="c", subcore_axis_name="s")


SC_WINDOW = 128


def sc_scatter_rows(x, dest, n_slots):
    N, W = x.shape
    K = dest.shape[0]

    @functools.partial(pl.kernel, out_type=jax.ShapeDtypeStruct((n_slots, W), x.dtype), mesh=_sc_mesh(),
                       scratch_types=[])
    def scatter(x_hbm, i_hbm, o_hbm):
        def body(x_vmem, i_vmem):
            for k in range(K):
                pltpu.sync_copy(x_vmem, o_hbm.at[i_vmem.at[k]])

        pltpu.emit_pipeline(
            body,
            grid=(N // SC_WINDOW,),
            in_specs=[pl.BlockSpec((SC_WINDOW, W), lambda i: (i, 0)),
                      pl.BlockSpec((K, SC_WINDOW), lambda i: (0, i))],
            out_specs=[],
            core_axis_name=("c", "s"),
            dimension_semantics=(pltpu.PARALLEL,),
        )(x_hbm, i_hbm)

    return scatter(x, dest)


def sc_gather_rows(y, dest):
    W = y.shape[1]
    K, N = dest.shape

    @functools.partial(pl.kernel, out_type=jax.ShapeDtypeStruct((K, N, W), y.dtype), mesh=_sc_mesh(),
                       scratch_types=[])
    def gather(y_hbm, i_hbm, o_hbm):
        def body(i_vmem, o_vmem):
            pltpu.sync_copy(y_hbm.at[i_vmem.at[0, 0]], o_vmem.at[0])

        pltpu.emit_pipeline(
            body,
            grid=(K, N // SC_WINDOW),
            in_specs=[pl.BlockSpec((1, 1, SC_WINDOW), lambda k, i: (k, 0, i))],
            out_specs=[pl.BlockSpec((1, SC_WINDOW, W), lambda k, i: (k, i, 0))],
            core_axis_name=("c", "s"),
            dimension_semantics=(pltpu.PARALLEL, pltpu.PARALLEL),
        )(i_hbm, o_hbm)

    return gather(y, dest.reshape(K, 1, N))


def _expert_kernel(blk_e_ref, nvalid_ref, xa_ref, xb_ref, w1_ref, w3_ref, w2_ref, ya_ref, yb_ref,
                   w1_sc, w3_sc, w2_sc):
    b = pl.program_id(0)
    nv = nvalid_ref[b]
    prev_e = blk_e_ref[jnp.maximum(b - 1, 0)]

    @pl.when((b == 0) | (blk_e_ref[b] != prev_e))
    def _():
        w1_sc[...] = w1_ref[0, 0].astype(BF16)
        w3_sc[...] = w3_ref[0, 0].astype(BF16)
        w2_sc[...] = w2_ref[0, 0].astype(BF16)

    @pl.when(nv > 0)
    def _():
        x = _unpack_row_halves(xa_ref[...], xb_ref[...])
        rows = lax.broadcasted_iota(jnp.int32, x.shape, 0)
        x = jnp.where(rows < nv, x, 0.0).astype(BF16)
        hid = _silu(jnp.dot(x, w1_sc[...], preferred_element_type=F32)) * jnp.dot(
            x, w3_sc[...], preferred_element_type=F32)
        y = jnp.dot(hid.astype(BF16), w2_sc[...], preferred_element_type=F32)
        ya_ref[...], yb_ref[...] = _pack_row_halves(y)

    @pl.when(nv == 0)
    def _():
        ya_ref[...] = jnp.zeros(ya_ref.shape, ya_ref.dtype)
        yb_ref[...] = jnp.zeros(yb_ref.shape, yb_ref.dtype)


def routed_experts(xa, xb, blk_e, nvalid, w1, w3, w2, layer):
    P = xa.shape[0]
    blk = SLOT_BLOCK
    _, E, D, FF = w1.shape
    slots = lambda: pl.BlockSpec((blk, PACK_W), lambda b, be, nv: (b, 0))
    grid_spec = pltpu.PrefetchScalarGridSpec(
        num_scalar_prefetch=2,
        grid=(P // blk,),
        in_specs=[
            slots(), slots(),
            pl.BlockSpec((1, 1, D, FF), lambda b, be, nv: (layer, be[b], 0, 0)),
            pl.BlockSpec((1, 1, D, FF), lambda b, be, nv: (layer, be[b], 0, 0)),
            pl.BlockSpec((1, 1, FF, D), lambda b, be, nv: (layer, be[b], 0, 0)),
        ],
        out_specs=[slots(), slots()],
        scratch_shapes=[pltpu.VMEM((D, FF), BF16), pltpu.VMEM((D, FF), BF16), pltpu.VMEM((FF, D), BF16)],
    )
    return pl.pallas_call(
        _expert_kernel,
        grid_spec=grid_spec,
        out_shape=[jax.ShapeDtypeStruct((P, PACK_W), jnp.int32)] * 2,
        compiler_params=_cp(("arbitrary",), VMEM_LIMIT),
        name="routed_experts",
    )(blk_e, nvalid, xa, xb, w1, w3, w2)


def _combine_kernel(xmid_ref, oa_ref, ob_ref, w_ref, mod2_ref, fg_ref, *rest, final):
    out_ref = rest[-1]
    D = xmid_ref.shape[1]
    w = w_ref[...]
    acc = w[:, 0:1] * _unpack_row_halves(oa_ref[0], ob_ref[0])
    for k in range(1, TOP_K):
        acc = acc + w[:, k:k + 1] * _unpack_row_halves(oa_ref[k], ob_ref[k])
    x = xmid_ref[...] + mod2_ref[0][:, 2 * D:] * acc
    if final:
        x = x * lax.rsqrt(jnp.mean(x * x, axis=-1, keepdims=True) + EPS) * fg_ref[...]
    out_ref[...] = x


def combine(xmid, oa, ob, w_tok, mod2, final_g, seq, final, out_rows=None, row0=0, out_buf=None):
    N, D = xmid.shape
    tm = 256
    tpb = seq // tm
    tile0 = row0 // tm
    rows8 = lambda: pl.BlockSpec((TOP_K, tm, PACK_W), lambda i: (0, i, 0))
    in_specs = [
        pl.BlockSpec((tm, D), lambda i: (i, 0)),
        rows8(), rows8(),
        pl.BlockSpec((tm, TOP_K), lambda i: (i, 0)),
        pl.BlockSpec((1, 1, 3 * D), lambda i: (i // tpb, 0, 0)),
        pl.BlockSpec((1, D), lambda i: (0, 0)),
    ]
    args = [xmid, oa, ob, w_tok, mod2, final_g.reshape(1, D)]
    aliases = {}
    if out_buf is not None:
        in_specs.append(pl.BlockSpec(memory_space=pl.ANY))
        args.append(out_buf)
        aliases = {len(args) - 1: 0}
    return pl.pallas_call(
        functools.partial(_combine_kernel, final=final),
        grid=(N // tm,),
        in_specs=in_specs,
        out_specs=pl.BlockSpec((tm, D), lambda i: (i + tile0, 0)),
        out_shape=jax.ShapeDtypeStruct((out_rows or N, D), F32),
        input_output_aliases=aliases,
        compiler_params=_cp(("parallel",), VMEM_LIMIT),
        name="combine",
    )(*args)


TOKEN_STREAMS = 2


def _permute_w_in(w):
    ub = w[:, 3 * DA:3 * DA + DB]
    lat_lo = 3 * DA + DB
    lat_hi = lat_lo + Q_LORA + KV_LORA + QK_ROPE
    lat, gates = w[:, lat_lo:lat_hi], w[:, lat_hi:]
    pad = jnp.zeros((w.shape[0], LAT_W - (lat_hi - lat_lo)), w.dtype)
    parts = [gates, ub, lat, pad]
    for g in range(len(DIL_GROUPS)):
        sl = slice(g * GROUP_W, (g + 1) * GROUP_W)
        parts += [w[:, :DA][:, sl] * (HEAD_DIM_A ** -0.5), w[:, DA:2 * DA][:, sl], w[:, 2 * DA:3 * DA][:, sl]]
    return jnp.concatenate(parts, axis=1).astype(BF16)


def kernel(x, c, positions, ada_mix_w, ada_mix_b, norm_mix_g, w_in, pool_w, pool_scale, cq_norm_g, ckv_norm_g, w_uq, w_ukv, w_oa, w_ob, w_oc, w_out, ada_ffn_w, ada_ffn_b, norm_ffn_g, router_w, router_bias, exp_w1, exp_w3, exp_w2, sh_w1, sh_w3, sh_w2, final_g):
    B, S, D = x.shape
    depth = w_in.shape[0]
    mod_mix = adaln_rows(c, ada_mix_w, ada_mix_b)
    mod_ffn = adaln_rows(c, ada_ffn_w, ada_ffn_b)
    streams = TOKEN_STREAMS if B % TOKEN_STREAMS == 0 else 1
    Bs = B // streams
    Ns = Bs * S
    x_all = x.reshape(B * S, D)
    xs = [None] * streams
    out_all = None
    pos_cols = [positions[s * Bs:(s + 1) * Bs].reshape(Ns, 1) for s in range(streams)]
    for l in range(depth):
        last = l == depth - 1
        w_in_l = _permute_w_in(w_in[l])
        mla_w = _mla_weights(cq_norm_g[l], ckv_norm_g[l], w_uq[l], w_ukv[l])
        mix_w = (norm_ffn_g[l], pool_w[l].astype(BF16), pool_scale[l],
                 w_oa[l].astype(BF16), w_ob[l].astype(BF16), w_oc[l].astype(BF16), w_out[l].astype(BF16),
                 router_w[l].T.astype(BF16), sh_w1[l].astype(BF16), sh_w3[l].astype(BF16), sh_w2[l].astype(BF16))
        for s in range(streams):
            x2, row0 = (x_all, s * Ns) if l == 0 else (xs[s], 0)
            mod1 = mod_mix[l, s * Bs:(s + 1) * Bs].reshape(Bs, 1, 3 * D)
            mod2 = mod_ffn[l, s * Bs:(s + 1) * Bs].reshape(Bs, 1, 3 * D)
            gu, lat, *qkv = in_projection(x2, norm_mix_g[l], mod1, w_in_l, S, row0)
            dil = [dilated_attention(qkv[2 * g], qkv[2 * g + 1], Bs, S, d) for g, (_, d) in enumerate(DIL_GROUPS)]
            q_all, k_all, vt_all = mla_prep(lat, pos_cols[s], *mla_w, Bs, S)
            yc = mla_attention(q_all, k_all, vt_all, Bs, S)
            xmid, h2a, h2b, logits_t = mix_out(x2, gu, dil, yc, mod1, mod2, *mix_w, S, row0)
            dest, w_k, counts = route(logits_t, router_bias[l])
            blk_e, nvalid = block_tables(counts, Ns)
            n_slots = blk_e.shape[0] * SLOT_BLOCK
            xa = sc_scatter_rows(h2a, dest, n_slots)
            xb = sc_scatter_rows(h2b, dest, n_slots)
            ya, yb = routed_experts(xa, xb, blk_e, nvalid, exp_w1, exp_w3, exp_w2, l)
            oa = sc_gather_rows(ya, dest)
            ob = sc_gather_rows(yb, dest)
            if last:
                out_all = combine(xmid, oa, ob, w_k.T, mod2, final_g, S, True, B * S, s * Ns, out_all)
            else:
                xs[s] = combine(xmid, oa, ob, w_k.T, mod2, final_g, S, False)
    return out_all.reshape(B, S, D)
```

```python
import functools
import math

import jax
import jax.numpy as jnp
from jax import lax
from jax.experimental import pallas as pl
from jax.experimental.pallas import tpu as pltpu
from jax.experimental.pallas import tpu_sc as plsc

F32 = jnp.float32
BF16 = jnp.bfloat16
HIGHEST = lax.Precision.HIGHEST

D_MODEL = 1024
HEAD_DIM_A = 64
HEADS_PER_GROUP_A = 4
DIL_GROUPS = ((128, 1), (512, 4), (2048, 16))
GROUP_W = HEADS_PER_GROUP_A * HEAD_DIM_A
DA = GROUP_W * len(DIL_GROUPS)
POOL_WINDOWS = (2, 4, 8, 16)
POOL_GROUP_DIM = 128
DB = POOL_GROUP_DIM * len(POOL_WINDOWS)
POOL_HALO = 16
N_HEADS_C = 8
QK_NOPE = 64
QK_ROPE = 32
V_DIM = 64
Q_LORA = 384
KV_LORA = 256
DC = N_HEADS_C * V_DIM
HEAD_PAD_C = 128
ROPE_THETA = 10000.0
N_EXPERTS = 64
TOP_K = 8
N_GROUPS = 8
TOPK_GROUPS = 4
GROUP_SIZE = N_EXPERTS // N_GROUPS
EXPERT_FF = 256
ROUTED_SCALE = 2.5
EPS = 1e-6
NEG = -1e30
Q_BLOCK = 128

LAT_W = 768
GU_W = 3 * D_MODEL + DB
IN_OUT_WIDTHS = (GU_W, LAT_W) + (2 * GROUP_W, GROUP_W) * len(DIL_GROUPS)

VMEM_LIMIT = 56 * 1024 * 1024


def _cp(sem, vmem=None):
    return pltpu.CompilerParams(dimension_semantics=sem, vmem_limit_bytes=vmem)


def _silu(v):
    return v * jax.nn.sigmoid(v)


def _nt_dot(a, b):
    return lax.dot_general(a, b, (((1,), (1,)), ((), ())), preferred_element_type=F32)


PACK_W = D_MODEL // 4
_HI_MASK = -65536


def _bf16_bits(v):
    return lax.bitcast_convert_type(v.astype(BF16).astype(F32), jnp.int32)


def _pack_row_halves(v):
    halves = []
    for h in range(2):
        lo = _bf16_bits(v[:, (2 * h) * PACK_W:(2 * h + 1) * PACK_W])
        hi = _bf16_bits(v[:, (2 * h + 1) * PACK_W:(2 * h + 2) * PACK_W])
        halves.append(lax.shift_right_logical(lo, 16) | (hi & _HI_MASK))
    return halves


def _unpack_row_halves(wa, wb):
    parts = []
    for w in (wa, wb):
        parts.append(lax.bitcast_convert_type(lax.shift_left(w, 16), F32))
        parts.append(lax.bitcast_convert_type(w & _HI_MASK, F32))
    return jnp.concatenate(parts, axis=1)


def _adaln_kernel(c_ref, w_ref, b_ref, o_ref):
    s = _silu(c_ref[...])
    o_ref[0] = jnp.dot(s, w_ref[0], preferred_element_type=F32, precision=HIGHEST) + b_ref[0]


def adaln_rows(c, w, b):
    L, D, D3 = w.shape
    B = c.shape[0]
    tn = 1024
    return pl.pallas_call(
        _adaln_kernel,
        grid=(L, D3 // tn),
        in_specs=[
            pl.BlockSpec((B, D), lambda l, j: (0, 0)),
            pl.BlockSpec((1, D, tn), lambda l, j: (l, 0, j)),
            pl.BlockSpec((1, 1, tn), lambda l, j: (l, 0, j)),
        ],
        out_specs=pl.BlockSpec((1, B, tn), lambda l, j: (l, 0, j)),
        out_shape=jax.ShapeDtypeStruct((L, B, D3), F32),
        compiler_params=_cp(("parallel", "parallel")),
        name="adaln_rows",
    )(c, w, b.reshape(L, 1, D3))


LANES = 128


def _inproj_kernel(x_ref, g_ref, mod_ref, w_ref, *refs, chunk):
    o_refs, scr = refs[:-1], refs[-1]
    D = x_ref.shape[1]
    x = x_ref[...]
    y = x * lax.rsqrt(jnp.mean(x * x, axis=-1, keepdims=True) + EPS) * g_ref[...]
    mod = mod_ref[0]
    h = (y * (1.0 + mod[:, D:2 * D]) + mod[:, :D]).astype(BF16)
    col = 0
    for o_ref in o_refs:
        width = o_ref.shape[-1]
        if o_ref.ndim == 2:
            for c0 in range(0, width, chunk):
                cw = min(chunk, width - c0)
                o_ref[:, c0:c0 + cw] = jnp.dot(
                    h, w_ref[:, col + c0:col + c0 + cw], preferred_element_type=F32).astype(o_ref.dtype)
        else:
            dil, rows = o_ref.shape[1], o_ref.shape[2]
            z = jnp.dot(h, w_ref[:, col:col + width], preferred_element_type=F32)
            if dil == 1:
                o_ref[0, 0] = z.astype(o_ref.dtype)
            else:
                for c in range(width // LANES):
                    scr[c] = z[:, c * LANES:(c + 1) * LANES]
                for r in range(dil):
                    o_ref[0, r] = jnp.concatenate(
                        [scr[c, pl.ds(r, rows, stride=dil), :] for c in range(width // LANES)],
                        axis=1).astype(o_ref.dtype)
        col += width


def in_projection(x2, g, mod, w, seq, row0=0):
    D = x2.shape[1]
    B = mod.shape[0]
    N = B * seq
    tm = 512
    tpb = seq // tm
    tile0 = row0 // tm
    out_specs = [pl.BlockSpec((tm, wd), lambda i: (i, 0)) for wd in IN_OUT_WIDTHS[:2]]
    out_shape = [jax.ShapeDtypeStruct((N, wd), BF16) for wd in IN_OUT_WIDTHS[:2]]
    for grp, (_, dil) in enumerate(DIL_GROUPS):
        for wd in IN_OUT_WIDTHS[2 + 2 * grp:4 + 2 * grp]:
            out_specs.append(pl.BlockSpec((1, dil, tm // dil, wd), lambda i: (i // tpb, 0, i % tpb, 0)))
            out_shape.append(jax.ShapeDtypeStruct((B, dil, seq // dil, wd), BF16))
    return pl.pallas_call(
        functools.partial(_inproj_kernel, chunk=512),
        grid=(N // tm,),
        in_specs=[
            pl.BlockSpec((tm, D), lambda i: (i + tile0, 0)),
            pl.BlockSpec((1, D), lambda i: (0, 0)),
            pl.BlockSpec((1, 1, 3 * D), lambda i: (i // tpb, 0, 0)),
            pl.BlockSpec(w.shape, lambda i: (0, 0), pipeline_mode=pl.Buffered(1)),
        ],
        out_specs=out_specs,
        out_shape=out_shape,
        scratch_shapes=[pltpu.VMEM((max(IN_OUT_WIDTHS[2:]) // LANES, tm, LANES), F32)],
        compiler_params=_cp(("parallel",), VMEM_LIMIT),
        name="in_projection",
    )(x2, g.reshape(1, D), mod, w)


def _dilated_kernel(q_ref, kc_ref, kp_ref, vc_ref, vp_ref, o_ref, lse_ref):
    i = pl.program_id(1)
    T = Q_BLOCK
    key = lax.broadcasted_iota(jnp.int32, (T, T), 0)
    qry = lax.broadcasted_iota(jnp.int32, (T, T), 1)
    valid_c = key <= qry
    near = key >= qry
    run = q_ref.shape[1] // T
    heads = [slice(h * HEAD_DIM_A, (h + 1) * HEAD_DIM_A) for h in range(HEADS_PER_GROUP_A)]

    def transposed(v):
        return v.astype(F32).T.astype(BF16)

    vts = [transposed(vc_ref[0, j * T:(j + 1) * T, :]) for j in range(run)]
    vt_before = transposed(vp_ref[0])

    def blocks(j):
        rows = slice(j * T, (j + 1) * T)
        if j == 0:
            return rows, kc_ref[0, rows, :], vts[0], kp_ref[0], vt_before, near & (i > 0)
        before = slice((j - 1) * T, j * T)
        return rows, kc_ref[0, rows, :], vts[j], kc_ref[0, before, :], vts[j - 1], near

    scores, probs = {}, {}
    for j in range(run):
        rows, kc, _, kp, _, valid_p = blocks(j)
        q = q_ref[0, rows, :]
        for h, sl in enumerate(heads):
            qh = q[:, sl]
            scores[j, h] = (jnp.where(valid_c, _nt_dot(kc[:, sl], qh), NEG),
                            jnp.where(valid_p, _nt_dot(kp[:, sl], qh), NEG))
    for (j, h), (sc, sp) in scores.items():
        m = jnp.maximum(jnp.max(sc, axis=0, keepdims=True), jnp.max(sp, axis=0, keepdims=True))
        pc = jnp.exp(sc - m)
        pp = jnp.exp(sp - m)
        den = jnp.sum(pc, axis=0, keepdims=True) + jnp.sum(pp, axis=0, keepdims=True)
        probs[j, h] = (pc.astype(BF16), pp.astype(BF16), den, m + jnp.log(den))
    for j in range(run):
        rows, _, vtc, _, vtp, _ = blocks(j)
        outs = []
        for h, sl in enumerate(heads):
            pc, pp, den, _ = probs[j, h]
            o = jnp.dot(vtc[sl, :], pc, preferred_element_type=F32) + jnp.dot(vtp[sl, :], pp, preferred_element_type=F32)
            outs.append(o / den)
        o_ref[0, rows, :] = jnp.concatenate(outs, axis=0).T.astype(o_ref.dtype)
        spread = LSE_LANES // len(heads)
        lse_t = jnp.concatenate([jnp.broadcast_to(probs[j, h][3], (spread, T)) for h in range(len(heads))], axis=0)
        lse_ref[0, rows, :] = lse_t.T


DILATED_RUN = 4


LSE_LANES = 128


def dilated_attention(qk, v):
    batch, dilation, L, _ = qk.shape
    nb = L // Q_BLOCK
    run = min(DILATED_RUN, nb)
    qk_r = qk.reshape(batch * dilation, L, 2 * GROUP_W)
    v_r = v.reshape(batch * dilation, L, GROUP_W)
    before = lambda i: jnp.maximum(i * run - 1, 0)
    o, lse = pl.pallas_call(
        _dilated_kernel,
        grid=(batch * dilation, nb // run),
        in_specs=[
            pl.BlockSpec((1, run * Q_BLOCK, GROUP_W), lambda s, i: (s, i, 0)),
            pl.BlockSpec((1, run * Q_BLOCK, GROUP_W), lambda s, i: (s, i, 1)),
            pl.BlockSpec((1, Q_BLOCK, GROUP_W), lambda s, i: (s, before(i), 1)),
            pl.BlockSpec((1, run * Q_BLOCK, GROUP_W), lambda s, i: (s, i, 0)),
            pl.BlockSpec((1, Q_BLOCK, GROUP_W), lambda s, i: (s, before(i), 0)),
        ],
        out_specs=[
            pl.BlockSpec((1, run * Q_BLOCK, GROUP_W), lambda s, i: (s, i, 0)),
            pl.BlockSpec((1, run * Q_BLOCK, LSE_LANES), lambda s, i: (s, i, 0)),
        ],
        out_shape=[
            jax.ShapeDtypeStruct((batch * dilation, L, GROUP_W), BF16),
            jax.ShapeDtypeStruct((batch * dilation, L, LSE_LANES), F32),
        ],
        compiler_params=_cp(("parallel", "parallel")),
        name=f"dilated_attention_d{dilation}",
    )(qk_r, qk_r, qk_r, v_r, v_r)
    return o.reshape(batch, dilation, L, GROUP_W), lse.reshape(batch, dilation, L, LSE_LANES)


def _mla_prep_kernel(lat_ref, pos_ref, gq_ref, gkv_ref, wq_ref, wk_ref, wvt_ref, freq_ref,
                     q_ref, k_ref, vt_ref):
    HP = N_HEADS_C * HEAD_PAD_C
    lat = lat_ref[...].astype(F32)
    cq = lat[:, :Q_LORA]
    ckr = lat[:, Q_LORA:]
    zq = (cq * lax.rsqrt(jnp.mean(cq * cq, axis=-1, keepdims=True) + EPS) * gq_ref[...]).astype(BF16)
    lane = lax.broadcasted_iota(jnp.int32, ckr.shape, 1)
    is_kv = lane < KV_LORA
    ms = jnp.sum(jnp.where(is_kv, ckr * ckr, 0.0), axis=-1, keepdims=True) * (1.0 / KV_LORA)
    zkv = (ckr * jnp.where(is_kv, lax.rsqrt(ms + EPS) * gkv_ref[...], 1.0)).astype(BF16)
    qq = jnp.dot(zq, wq_ref[...], preferred_element_type=F32)
    kk = jnp.dot(zkv, wk_ref[...], preferred_element_type=F32)
    ang = pos_ref[...].astype(F32) * freq_ref[...]
    cos, sin = jnp.cos(ang), jnp.sin(ang)
    for h in range(N_HEADS_C):
        lo, hi = h * HEAD_PAD_C, (h + 1) * HEAD_PAD_C
        q_ref[:, lo:hi] = (qq[:, lo:hi] * cos + qq[:, HP + lo:HP + hi] * sin).astype(q_ref.dtype)
        k_ref[:, lo:hi] = (kk[:, lo:hi] * cos + kk[:, HP + lo:HP + hi] * sin).astype(k_ref.dtype)
    vt_ref[0] = _nt_dot(wvt_ref[...], zkv).astype(vt_ref.dtype)


def _mla_weights(cq_g, ckv_g, w_uq, w_ukv):
    H, HPAD, half = N_HEADS_C, HEAD_PAD_C, QK_ROPE // 2
    scale = (QK_NOPE + QK_ROPE) ** -0.5 * math.log2(math.e)
    wq = w_uq.reshape(Q_LORA, H, QK_NOPE + QK_ROPE) * scale
    q_lin = jnp.pad(wq, ((0, 0), (0, 0), (0, HPAD - QK_NOPE - QK_ROPE)))
    r1, r2 = wq[..., QK_NOPE:QK_NOPE + half], wq[..., QK_NOPE + half:]
    q_sw = jnp.concatenate([jnp.zeros((Q_LORA, H, QK_NOPE), F32), -r2, r1,
                            jnp.zeros((Q_LORA, H, HPAD - QK_NOPE - QK_ROPE), F32)], axis=-1)
    wq_big = jnp.concatenate([q_lin.reshape(Q_LORA, H * HPAD), q_sw.reshape(Q_LORA, H * HPAD)], axis=1)

    rows = LAT_W - Q_LORA
    wkv = w_ukv.reshape(KV_LORA, H, QK_NOPE + V_DIM)
    eye = jnp.eye(QK_ROPE, dtype=F32)
    k_lin = jnp.zeros((rows, H, HPAD), F32)
    k_lin = k_lin.at[:KV_LORA, :, :QK_NOPE].set(wkv[..., :QK_NOPE])
    k_lin = k_lin.at[KV_LORA:KV_LORA + QK_ROPE, :, QK_NOPE:QK_NOPE + QK_ROPE].set(
        jnp.broadcast_to(eye[:, None, :], (QK_ROPE, H, QK_ROPE)))
    swap = jnp.zeros((QK_ROPE, QK_ROPE), F32).at[half:, :half].set(-jnp.eye(half)).at[:half, half:].set(jnp.eye(half))
    k_sw = jnp.zeros((rows, H, HPAD), F32)
    k_sw = k_sw.at[KV_LORA:KV_LORA + QK_ROPE, :, QK_NOPE:QK_NOPE + QK_ROPE].set(
        jnp.broadcast_to(swap[:, None, :], (QK_ROPE, H, QK_ROPE)))
    v_w = jnp.zeros((rows, H, V_DIM), F32).at[:KV_LORA].set(wkv[..., QK_NOPE:])
    wk_big = jnp.concatenate([k_lin.reshape(rows, H * HPAD), k_sw.reshape(rows, H * HPAD)], axis=1)
    wv_t = v_w.reshape(rows, H * V_DIM).T

    gkv = jnp.concatenate([ckv_g, jnp.ones((rows - KV_LORA,), F32)]).reshape(1, rows)
    return cq_g.reshape(1, Q_LORA), gkv, wq_big.astype(BF16), wk_big.astype(BF16), wv_t.astype(BF16)


def _rope_lane_freqs():
    half = QK_ROPE // 2
    freqs = ROPE_THETA ** (-jnp.arange(0, QK_ROPE, 2, dtype=F32) / QK_ROPE)
    row = jnp.zeros((HEAD_PAD_C,), F32)
    row = row.at[QK_NOPE:QK_NOPE + half].set(freqs).at[QK_NOPE + half:QK_NOPE + QK_ROPE].set(freqs)
    return row.reshape(1, HEAD_PAD_C)


def mla_prep(lat, pos_col, gq, gkv, wq_big, wk_big, wv_t, batch, seq):
    N = lat.shape[0]
    HP = N_HEADS_C * HEAD_PAD_C
    tm = 512
    tpb = seq // tm
    freqs = _rope_lane_freqs()
    const = lambda shape: pl.BlockSpec(shape, lambda i: (0, 0))
    return pl.pallas_call(
        _mla_prep_kernel,
        grid=(N // tm,),
        in_specs=[
            pl.BlockSpec((tm, LAT_W), lambda i: (i, 0)),
            pl.BlockSpec((tm, 1), lambda i: (i, 0)),
            const(gq.shape), const(gkv.shape), const(wq_big.shape), const(wk_big.shape), const(wv_t.shape),
            const(freqs.shape),
        ],
        out_specs=[
            pl.BlockSpec((tm, HP), lambda i: (i, 0)),
            pl.BlockSpec((tm, HP), lambda i: (i, 0)),
            pl.BlockSpec((1, DC, tm), lambda i: (i // tpb, 0, i % tpb)),
        ],
        out_shape=[
            jax.ShapeDtypeStruct((N, HP), BF16),
            jax.ShapeDtypeStruct((N, HP), BF16),
            jax.ShapeDtypeStruct((batch, DC, seq), BF16),
        ],
        compiler_params=_cp(("parallel",), VMEM_LIMIT),
        name="mla_prep",
    )(lat, pos_col, gq, gkv, wq_big, wk_big, wv_t, freqs)


HEADS_PER_STEP_C = 8
FLASH_Q_CHUNK = 256


def _mla_flash_kernel(qi_ref, ki_ref, q_ref, k_ref, vt_ref, o_ref, m_sc, l_sc, acc_sc):
    t = pl.program_id(2)
    qi, ki = qi_ref[t], ki_ref[t]

    @pl.when(ki == 0)
    def _():
        m_sc[...] = jnp.full(m_sc.shape, NEG, F32)
        l_sc[...] = jnp.zeros(l_sc.shape, F32)
        acc_sc[...] = jnp.zeros(acc_sc.shape, F32)

    def step(masked):
        T = q_ref.shape[1]
        if masked:
            key = lax.broadcasted_iota(jnp.int32, (T, T), 0)
            qry = lax.broadcasted_iota(jnp.int32, (T, T), 1)
            keep = key <= qry
        chains = [(h, c) for h in range(HEADS_PER_STEP_C) for c in range(T // FLASH_Q_CHUNK)]
        scores, probs, alphas = {}, {}, {}

        def qk(h, c):
            qs = slice(c * FLASH_Q_CHUNK, (c + 1) * FLASH_Q_CHUNK)
            q = q_ref[0, qs, h * HEAD_PAD_C:(h + 1) * HEAD_PAD_C]
            k = k_ref[0, :, h * HEAD_PAD_C:(h + 1) * HEAD_PAD_C]
            st = _nt_dot(k, q)
            scores[h, c] = jnp.where(keep[:, qs], st, NEG) if masked else st

        def softmax(h, c):
            qs = slice(c * FLASH_Q_CHUNK, (c + 1) * FLASH_Q_CHUNK)
            st = scores.pop((h, c))
            m_prev = m_sc[h, :, qs]
            m_new = jnp.maximum(m_prev, jnp.max(st, axis=0, keepdims=True))
            alpha = jnp.exp2(m_prev - m_new)
            p = jnp.exp2(st - m_new)
            l_sc[h, :, qs] = alpha * l_sc[h, :, qs] + jnp.sum(p, axis=0, keepdims=True)
            m_sc[h, :, qs] = m_new
            probs[h, c], alphas[h, c] = p.astype(BF16), alpha

        def pv(h, c):
            qs = slice(c * FLASH_Q_CHUNK, (c + 1) * FLASH_Q_CHUNK)
            vt = vt_ref[0, h * V_DIM:(h + 1) * V_DIM, :]
            acc_sc[h, :, qs] = alphas.pop((h, c)) * acc_sc[h, :, qs] + jnp.dot(
                vt, probs.pop((h, c)), preferred_element_type=F32)

        for phase in (qk, softmax, pv):
            for ch in chains:
                phase(*ch)

    @pl.when(ki < qi)
    def _():
        step(False)

    @pl.when(ki == qi)
    def _():
        step(True)
        ot = jnp.concatenate([acc_sc[h] / l_sc[h] for h in range(HEADS_PER_STEP_C)], axis=0)
        o_ref[0] = ot.T.astype(o_ref.dtype)


def mla_attention(q_all, k_all, vt_all, batch, seq):
    T = 512
    nq = seq // T
    pairs = [(a, b) for a in range(nq) for b in range(a + 1)]
    qi_tab = jnp.asarray([p[0] for p in pairs], jnp.int32)
    ki_tab = jnp.asarray([p[1] for p in pairs], jnp.int32)
    hp = N_HEADS_C // HEADS_PER_STEP_C
    qw = HEADS_PER_STEP_C * HEAD_PAD_C
    vw = HEADS_PER_STEP_C * V_DIM
    q3 = q_all.reshape(batch, seq, -1)
    k3 = k_all.reshape(batch, seq, -1)
    grid_spec = pltpu.PrefetchScalarGridSpec(
        num_scalar_prefetch=2,
        grid=(batch, hp, len(pairs)),
        in_specs=[
            pl.BlockSpec((1, T, qw), lambda b, h, t, qi, ki: (b, qi[t], h)),
            pl.BlockSpec((1, T, qw), lambda b, h, t, qi, ki: (b, ki[t], h)),
            pl.BlockSpec((1, vw, T), lambda b, h, t, qi, ki: (b, h, ki[t])),
        ],
        out_specs=pl.BlockSpec((1, T, vw), lambda b, h, t, qi, ki: (b, qi[t], h)),
        scratch_shapes=[
            pltpu.VMEM((HEADS_PER_STEP_C, 1, T), F32),
            pltpu.VMEM((HEADS_PER_STEP_C, 1, T), F32),
            pltpu.VMEM((HEADS_PER_STEP_C, V_DIM, T), F32),
        ],
    )
    o = pl.pallas_call(
        _mla_flash_kernel,
        grid_spec=grid_spec,
        out_shape=jax.ShapeDtypeStruct((batch, seq, DC), BF16),
        compiler_params=_cp(("parallel", "parallel", "arbitrary")),
        name="mla_attention",
    )(qi_tab, ki_tab, q3, k3, vt_all)
    return o.reshape(batch * seq, DC)


def _mixout_kernel(x_ref, gates_ref, ub_ref, ubh_ref, o1_ref, o2_ref, o3_ref, l1_ref, l2_ref, l3_ref, yc_ref,
                   mod1_ref, mod2_ref, g2_ref, poolw_ref, pscale_ref, woa_ref, wob_ref, woc_ref, wout_ref,
                   rwt_ref, sw1_ref, sw3_ref, sw2_ref, spread_ref,
                   xmid_ref, h2a_ref, h2b_ref, logit_ref, *scratch, tiles_per_batch):
    D = x_ref.shape[1]
    tm = x_ref.shape[0]
    tile = pl.program_id(0) % tiles_per_batch
    o_scrs, l_scrs = scratch[:3], scratch[3:]

    def token_order(ref, scr):
        dil, rows, width = ref.shape[1:]
        if dil == 1:
            return ref[0, 0].astype(F32)
        for r in range(dil):
            v = ref[0, r].astype(F32)
            for c in range(width // LANES):
                scr[c, pl.ds(r, rows, stride=dil), :] = v[:, c * LANES:(c + 1) * LANES]
        return jnp.concatenate([scr[c] for c in range(width // LANES)], axis=1)

    outs = [token_order(r, s) for r, s in zip((o1_ref, o2_ref, o3_ref), o_scrs)]
    l1, l2, l3 = [token_order(r, s) for r, s in zip((l1_ref, l2_ref, l3_ref), l_scrs)]
    mx = jnp.maximum(jnp.maximum(l1, l2), l3)
    es = [jnp.exp(l1 - mx), jnp.exp(l2 - mx), jnp.exp(l3 - mx)]
    inv = 1.0 / (es[0] + es[1] + es[2])
    ya = jnp.zeros((tm, GROUP_W), F32)
    for e, o in zip(es, outs):
        w = e * inv
        w_hi = w.astype(BF16)
        w_lo = (w - w_hi.astype(F32)).astype(BF16)
        w_wide = (jnp.dot(w_hi, spread_ref[...], preferred_element_type=F32)
                  + jnp.dot(w_lo, spread_ref[...], preferred_element_type=F32))
        ya = ya + w_wide * o
    a_out = jnp.dot(ya.astype(BF16), woa_ref[...], preferred_element_type=F32)

    u = ub_ref[...].astype(F32)
    halo = jnp.where(tile > 0, ubh_ref[...].astype(F32), 0.0)
    ext = jnp.concatenate([halo, u], axis=0)
    t_seq = tile * tm + lax.broadcasted_iota(jnp.int32, (tm, 1), 0)
    pooled = []
    for gi, w in enumerate(POOL_WINDOWS):
        sl = slice(gi * POOL_GROUP_DIM, (gi + 1) * POOL_GROUP_DIM)
        acc = ext[:, sl]
        k = 1
        while k < w:
            acc = acc + pltpu.roll(acc, k, axis=0)
            k *= 2
        cnt = jnp.minimum(t_seq + 1, w).astype(F32)
        pg = acc[POOL_HALO:] / cnt - u[:, sl]
        pooled.append(jnp.dot(pg.astype(BF16), poolw_ref[gi], preferred_element_type=F32))
    yb = jnp.concatenate(pooled, axis=1) * pscale_ref[...]
    b_out = jnp.dot(yb.astype(BF16), wob_ref[...], preferred_element_type=F32)
    c_out = jnp.dot(yc_ref[...], woc_ref[...], preferred_element_type=F32)

    g = gates_ref[...].astype(F32)
    mix = (jax.nn.sigmoid(g[:, :D]) * a_out + jax.nn.sigmoid(g[:, D:2 * D]) * b_out
           + jax.nn.sigmoid(g[:, 2 * D:]) * c_out)
    tok = jnp.dot(mix.astype(BF16), wout_ref[...], preferred_element_type=F32)
    xn = x_ref[...] + mod1_ref[0][:, 2 * D:] * tok

    mod2 = mod2_ref[0]
    y = xn * lax.rsqrt(jnp.mean(xn * xn, axis=-1, keepdims=True) + EPS) * g2_ref[...]
    h2 = y * (1.0 + mod2[:, D:2 * D]) + mod2[:, :D]
    h2b = h2.astype(BF16)
    h2a_ref[...], h2b_ref[...] = _pack_row_halves(h2b)
    logit_ref[...] = _nt_dot(rwt_ref[...], h2b)
    hid = _silu(jnp.dot(h2b, sw1_ref[...], preferred_element_type=F32)) * jnp.dot(
        h2b, sw3_ref[...], preferred_element_type=F32)
    shared = jnp.dot(hid.astype(BF16), sw2_ref[...], preferred_element_type=F32)
    xmid_ref[...] = xn + mod2[:, 2 * D:] * shared


def mix_out(x2, gu, dil, yc, mod1, mod2, g2, pool_w, pool_scale, w_oa, w_ob, w_oc, w_out, rwt, sw1, sw3, sw2, seq,
            row0=0):
    D = x2.shape[1]
    N = gu.shape[0]
    tm = 256
    tpb = seq // tm
    tile0 = row0 // tm
    (o1, l1), (o2, l2), (o3, l3) = dil
    row = lambda w, c=0: pl.BlockSpec((tm, w), lambda i: (i, c))
    by_residue = lambda a: pl.BlockSpec(
        (1, a.shape[1], tm // a.shape[1], a.shape[3]), lambda i: (i // tpb, 0, i % tpb, 0))
    heads = HEADS_PER_GROUP_A
    spread = (jnp.arange(LSE_LANES)[:, None] == (jnp.arange(GROUP_W)[None, :] // HEAD_DIM_A) * (LSE_LANES // heads)
              ).astype(BF16)
    const2 = lambda a: pl.BlockSpec(a.shape, lambda i: (0,) * a.ndim)
    modspec = pl.BlockSpec((1, 1, 3 * D), lambda i: (i // tpb, 0, 0))
    ub_col = 3 * D // DB
    halo_spec = pl.BlockSpec(
        (POOL_HALO, DB), lambda i: (jnp.maximum(i * (tm // POOL_HALO) - 1, 0), ub_col))
    weights = [g2.reshape(1, D), pool_w, pool_scale.reshape(1, DB), w_oa, w_ob, w_oc, w_out, rwt, sw1, sw3, sw2,
               spread]
    return pl.pallas_call(
        functools.partial(_mixout_kernel, tiles_per_batch=tpb),
        grid=(N // tm,),
        in_specs=[
            pl.BlockSpec((tm, D), lambda i: (i + tile0, 0)), row(3 * D), row(DB, ub_col), halo_spec,
            by_residue(o1), by_residue(o2), by_residue(o3), by_residue(l1), by_residue(l2), by_residue(l3), row(DC),
            modspec, modspec,
        ] + [const2(a) for a in weights],
        scratch_shapes=[pltpu.VMEM((GROUP_W // LANES, tm, LANES), F32)] * 3
        + [pltpu.VMEM((LSE_LANES // LANES, tm, LANES), F32)] * 3,
        out_specs=[row(D), row(PACK_W), row(PACK_W), pl.BlockSpec((N_EXPERTS, tm), lambda i: (0, i))],
        out_shape=[
            jax.ShapeDtypeStruct((N, D), F32),
            jax.ShapeDtypeStruct((N, PACK_W), jnp.int32),
            jax.ShapeDtypeStruct((N, PACK_W), jnp.int32),
            jax.ShapeDtypeStruct((N_EXPERTS, N), F32),
        ],
        compiler_params=_cp(("parallel",), VMEM_LIMIT),
        name="mix_out",
    )(x2, gu, gu, gu, o1, o2, o3, l1, l2, l3, yc, mod1, mod2, *weights)


def _pick_rows(table, picks):
    G, GS = N_GROUPS, GROUP_SIZE
    eio = lax.broadcasted_iota(jnp.int32, (GS, table.shape[1]), 0)
    rows = []
    for k in range(TOP_K):
        idx = picks[k:k + 1]
        parts = [jnp.sum(jnp.where(eio + g * GS == idx, table[g * GS:(g + 1) * GS], 0.0), axis=0, keepdims=True)
                 for g in range(G)]
        rows.append(functools.reduce(jnp.add, parts))
    return jnp.concatenate(rows, axis=0)


def _route_choose(lg_ref, bias_ref):
    G, GS = N_GROUPS, GROUP_SIZE
    scores = jax.nn.sigmoid(lg_ref[...])
    sel = scores + bias_ref[...]
    tn = sel.shape[1]
    eio = lax.broadcasted_iota(jnp.int32, (GS, tn), 0)
    ninf = -jnp.inf

    gs = []
    for g in range(G):
        v = sel[g * GS:(g + 1) * GS]
        m1 = jnp.max(v, axis=0, keepdims=True)
        i1 = jnp.min(jnp.where(v == m1, eio, GS), axis=0, keepdims=True)
        m2 = jnp.max(jnp.where(eio == i1, ninf, v), axis=0, keepdims=True)
        gs.append(m1 + m2)
    gsm = jnp.concatenate(gs, axis=0)
    gio = lax.broadcasted_iota(jnp.int32, (G, tn), 0)
    rank = jnp.zeros((G, tn), jnp.int32)
    for g2 in range(G):
        beats = (gs[g2] > gsm) | ((gs[g2] == gsm) & (g2 < gio))
        rank = rank + beats.astype(jnp.int32)
    gsel = rank < TOPK_GROUPS

    vs = [jnp.where(gsel[g:g + 1], sel[g * GS:(g + 1) * GS], NEG) for g in range(G)]
    eid = [eio + g * GS for g in range(G)]
    chosen = [jnp.zeros((GS, tn), jnp.bool_) for _ in range(G)]
    picks = []
    for _ in range(TOP_K):
        m = functools.reduce(jnp.maximum, [jnp.max(v, axis=0, keepdims=True) for v in vs])
        idx = functools.reduce(jnp.minimum, [
            jnp.min(jnp.where(v == m, e, N_EXPERTS), axis=0, keepdims=True) for v, e in zip(vs, eid)])
        picks.append(idx)
        for g in range(G):
            hit = eid[g] == idx
            chosen[g] = chosen[g] | hit
            vs[g] = jnp.where(hit, ninf, vs[g])
    mask = jnp.concatenate(chosen, axis=0).astype(F32)
    return scores, jnp.concatenate(picks, axis=0), mask


def _route_kernel(lg_ref, bias_ref, tri_ref, dest_ref, w_ref, cnt_ref, run_sc, start_sc, mask_sc, picks_sc,
                  *, slot_block):
    phase = pl.program_id(0)
    step = pl.program_id(1)
    tn = lg_ref.shape[1]
    cols = pl.ds(pl.multiple_of(step * tn, tn), tn)

    @pl.when(phase == 0)
    def _():
        @pl.when(step == 0)
        def _():
            run_sc[...] = jnp.zeros(run_sc.shape, F32)

        scores, picks, mask = _route_choose(lg_ref, bias_ref)
        wk = _pick_rows(scores, picks)
        w_ref[0] = wk / jnp.sum(wk, axis=0, keepdims=True) * ROUTED_SCALE
        dest_ref[0] = jnp.zeros(dest_ref.shape[1:], dest_ref.dtype)
        mask_sc[:, cols] = mask.astype(BF16)
        picks_sc[:, cols] = picks
        run_sc[...] = run_sc[...] + jnp.sum(mask, axis=1, keepdims=True)

    @pl.when(phase == 1)
    def _():
        @pl.when(step == 0)
        def _():
            counts = run_sc[...].astype(jnp.int32)
            cnt_ref[...] = jnp.broadcast_to(counts, cnt_ref.shape)
            shift = slot_block.bit_length() - 1
            padded = lax.shift_left(lax.shift_right_logical(counts + (slot_block - 1), shift), shift).astype(F32)
            r = lax.broadcasted_iota(jnp.int32, (N_EXPERTS, N_EXPERTS), 0)
            c = lax.broadcasted_iota(jnp.int32, (N_EXPERTS, N_EXPERTS), 1)
            as_row = jnp.sum(jnp.where(r == c, padded, 0.0), axis=0, keepdims=True)
            start_sc[...] = jnp.sum(jnp.where(c < r, as_row, 0.0), axis=1, keepdims=True)
            run_sc[...] = jnp.zeros(run_sc.shape, F32)

        mask_b = mask_sc[:, cols]
        mask = mask_b.astype(F32)
        before = jnp.dot(mask_b, tri_ref[...], preferred_element_type=F32) - mask
        slot = start_sc[...] + run_sc[...] + before
        dest_ref[0] = _pick_rows(slot, picks_sc[:, cols]).astype(jnp.int32)
        w_ref[0] = jnp.zeros(w_ref.shape[1:], w_ref.dtype)
        run_sc[...] = run_sc[...] + jnp.sum(mask, axis=1, keepdims=True)


SLOT_BLOCK = 512


def route(logits_t, bias):
    E, N = logits_t.shape
    tn = 1024
    tri = (jnp.arange(tn)[:, None] <= jnp.arange(tn)[None, :]).astype(BF16)
    plane = lambda: pl.BlockSpec((1, TOP_K, tn), lambda p, i: (p, 0, i))
    dest, w, cnt = pl.pallas_call(
        functools.partial(_route_kernel, slot_block=SLOT_BLOCK),
        grid=(2, N // tn),
        in_specs=[
            pl.BlockSpec((E, tn), lambda p, i: (0, i * (1 - p))),
            pl.BlockSpec((E, 1), lambda p, i: (0, 0)),
            pl.BlockSpec((tn, tn), lambda p, i: (0, 0)),
        ],
        out_specs=[plane(), plane(), pl.BlockSpec((E, 128), lambda p, i: (0, 0))],
        out_shape=[
            jax.ShapeDtypeStruct((2, TOP_K, N), jnp.int32),
            jax.ShapeDtypeStruct((2, TOP_K, N), F32),
            jax.ShapeDtypeStruct((E, 128), jnp.int32),
        ],
        scratch_shapes=[pltpu.VMEM((E, 1), F32), pltpu.VMEM((E, 1), F32),
                        pltpu.VMEM((E, N), BF16), pltpu.VMEM((TOP_K, N), jnp.int32)],
        compiler_params=_cp(("arbitrary", "arbitrary")),
        name="route",
    )(logits_t, bias.reshape(E, 1), tri)
    return dest[1], w[0], cnt[:, 0]


def block_tables(counts, n_tokens):
    E = counts.shape[0]
    blk = SLOT_BLOCK
    nblk = (n_tokens * TOP_K + E * blk) // blk
    per_expert = (counts + blk - 1) // blk
    bend = jnp.cumsum(per_expert)
    bstart = bend - per_expert
    b = jnp.arange(nblk, dtype=jnp.int32)[:, None]
    owns = (bstart[None, :] <= b) & (b < bend[None, :])
    blk_e = jnp.minimum(jnp.sum(bend[None, :] <= b, axis=1), E - 1).astype(jnp.int32)
    rows_left = counts[None, :] - (b - bstart[None, :]) * blk
    nvalid = jnp.sum(jnp.where(owns, jnp.clip(rows_left, 0, blk), 0), axis=1)
    return blk_e, nvalid.astype(jnp.int32)


def _sc_mesh():
    return plsc.VectorSubcoreMesh(core_axis_name="c", subcore_axis_name="s")


SC_WINDOW = 128


def sc_scatter_rows(x, dest, n_slots):
    N, W = x.shape
    K = dest.shape[0]

    @functools.partial(pl.kernel, out_type=jax.ShapeDtypeStruct((n_slots, W), x.dtype), mesh=_sc_mesh(),
                       scratch_types=[])
    def scatter(x_hbm, i_hbm, o_hbm):
        def body(x_vmem, i_vmem):
            for k in range(K):
                pltpu.sync_copy(x_vmem, o_hbm.at[i_vmem.at[k]])

        pltpu.emit_pipeline(
            body,
            grid=(N // SC_WINDOW,),
            in_specs=[pl.BlockSpec((SC_WINDOW, W), lambda i: (i, 0)),
                      pl.BlockSpec((K, SC_WINDOW), lambda i: (0, i))],
            out_specs=[],
            core_axis_name=("c", "s"),
            dimension_semantics=(pltpu.PARALLEL,),
        )(x_hbm, i_hbm)

    return scatter(x, dest)


def sc_gather_rows(y, dest):
    W = y.shape[1]
    K, N = dest.shape

    @functools.partial(pl.kernel, out_type=jax.ShapeDtypeStruct((K, N, W), y.dtype), mesh=_sc_mesh(),
                       scratch_types=[])
    def gather(y_hbm, i_hbm, o_hbm):
        def body(i_vmem, o_vmem):
            pltpu.sync_copy(y_hbm.at[i_vmem.at[0, 0]], o_vmem.at[0])

        pltpu.emit_pipeline(
            body,
            grid=(K, N // SC_WINDOW),
            in_specs=[pl.BlockSpec((1, 1, SC_WINDOW), lambda k, i: (k, 0, i))],
            out_specs=[pl.BlockSpec((1, SC_WINDOW, W), lambda k, i: (k, i, 0))],
            core_axis_name=("c", "s"),
            dimension_semantics=(pltpu.PARALLEL, pltpu.PARALLEL),
        )(i_hbm, o_hbm)

    return gather(y, dest.reshape(K, 1, N))


def _expert_kernel(blk_e_ref, nvalid_ref, xa_ref, xb_ref, w1_ref, w3_ref, w2_ref, ya_ref, yb_ref,
                   w1_sc, w3_sc, w2_sc):
    b = pl.program_id(0)
    nv = nvalid_ref[b]
    prev_e = blk_e_ref[jnp.maximum(b - 1, 0)]

    @pl.when((b == 0) | (blk_e_ref[b] != prev_e))
    def _():
        w1_sc[...] = w1_ref[0, 0].astype(BF16)
        w3_sc[...] = w3_ref[0, 0].astype(BF16)
        w2_sc[...] = w2_ref[0, 0].astype(BF16)

    @pl.when(nv > 0)
    def _():
        x = _unpack_row_halves(xa_ref[...], xb_ref[...])
        rows = lax.broadcasted_iota(jnp.int32, x.shape, 0)
        x = jnp.where(rows < nv, x, 0.0).astype(BF16)
        hid = _silu(jnp.dot(x, w1_sc[...], preferred_element_type=F32)) * jnp.dot(
            x, w3_sc[...], preferred_element_type=F32)
        y = jnp.dot(hid.astype(BF16), w2_sc[...], preferred_element_type=F32)
        ya_ref[...], yb_ref[...] = _pack_row_halves(y)

    @pl.when(nv == 0)
    def _():
        ya_ref[...] = jnp.zeros(ya_ref.shape, ya_ref.dtype)
        yb_ref[...] = jnp.zeros(yb_ref.shape, yb_ref.dtype)


def routed_experts(xa, xb, blk_e, nvalid, w1, w3, w2, layer):
    P = xa.shape[0]
    blk = SLOT_BLOCK
    _, E, D, FF = w1.shape
    slots = lambda: pl.BlockSpec((blk, PACK_W), lambda b, be, nv: (b, 0))
    grid_spec = pltpu.PrefetchScalarGridSpec(
        num_scalar_prefetch=2,
        grid=(P // blk,),
        in_specs=[
            slots(), slots(),
            pl.BlockSpec((1, 1, D, FF), lambda b, be, nv: (layer, be[b], 0, 0)),
            pl.BlockSpec((1, 1, D, FF), lambda b, be, nv: (layer, be[b], 0, 0)),
            pl.BlockSpec((1, 1, FF, D), lambda b, be, nv: (layer, be[b], 0, 0)),
        ],
        out_specs=[slots(), slots()],
        scratch_shapes=[pltpu.VMEM((D, FF), BF16), pltpu.VMEM((D, FF), BF16), pltpu.VMEM((FF, D), BF16)],
    )
    return pl.pallas_call(
        _expert_kernel,
        grid_spec=grid_spec,
        out_shape=[jax.ShapeDtypeStruct((P, PACK_W), jnp.int32)] * 2,
        compiler_params=_cp(("arbitrary",), VMEM_LIMIT),
        name="routed_experts",
    )(blk_e, nvalid, xa, xb, w1, w3, w2)


def _combine_kernel(xmid_ref, oa_ref, ob_ref, w_ref, mod2_ref, fg_ref, *rest, final):
    out_ref = rest[-1]
    D = xmid_ref.shape[1]
    w = w_ref[...]
    acc = w[:, 0:1] * _unpack_row_halves(oa_ref[0], ob_ref[0])
    for k in range(1, TOP_K):
        acc = acc + w[:, k:k + 1] * _unpack_row_halves(oa_ref[k], ob_ref[k])
    x = xmid_ref[...] + mod2_ref[0][:, 2 * D:] * acc
    if final:
        x = x * lax.rsqrt(jnp.mean(x * x, axis=-1, keepdims=True) + EPS) * fg_ref[...]
    out_ref[...] = x


def combine(xmid, oa, ob, w_tok, mod2, final_g, seq, final, out_rows=None, row0=0, out_buf=None):
    N, D = xmid.shape
    tm = 256
    tpb = seq // tm
    tile0 = row0 // tm
    rows8 = lambda: pl.BlockSpec((TOP_K, tm, PACK_W), lambda i: (0, i, 0))
    in_specs = [
        pl.BlockSpec((tm, D), lambda i: (i, 0)),
        rows8(), rows8(),
        pl.BlockSpec((tm, TOP_K), lambda i: (i, 0)),
        pl.BlockSpec((1, 1, 3 * D), lambda i: (i // tpb, 0, 0)),
        pl.BlockSpec((1, D), lambda i: (0, 0)),
    ]
    args = [xmid, oa, ob, w_tok, mod2, final_g.reshape(1, D)]
    aliases = {}
    if out_buf is not None:
        in_specs.append(pl.BlockSpec(memory_space=pl.ANY))
        args.append(out_buf)
        aliases = {len(args) - 1: 0}
    return pl.pallas_call(
        functools.partial(_combine_kernel, final=final),
        grid=(N // tm,),
        in_specs=in_specs,
        out_specs=pl.BlockSpec((tm, D), lambda i: (i + tile0, 0)),
        out_shape=jax.ShapeDtypeStruct((out_rows or N, D), F32),
        input_output_aliases=aliases,
        compiler_params=_cp(("parallel",), VMEM_LIMIT),
        name="combine",
    )(*args)


TOKEN_STREAMS = 2


def _permute_w_in(w):
    ub = w[:, 3 * DA:3 * DA + DB]
    lat_lo = 3 * DA + DB
    lat_hi = lat_lo + Q_LORA + KV_LORA + QK_ROPE
    lat, gates = w[:, lat_lo:lat_hi], w[:, lat_hi:]
    pad = jnp.zeros((w.shape[0], LAT_W - (lat_hi - lat_lo)), w.dtype)
    parts = [gates, ub, lat, pad]
    for g in range(len(DIL_GROUPS)):
        sl = slice(g * GROUP_W, (g + 1) * GROUP_W)
        parts += [w[:, :DA][:, sl] * (HEAD_DIM_A ** -0.5), w[:, DA:2 * DA][:, sl], w[:, 2 * DA:3 * DA][:, sl]]
    return jnp.concatenate(parts, axis=1).astype(BF16)


def kernel(x, c, positions, ada_mix_w, ada_mix_b, norm_mix_g, w_in, pool_w, pool_scale, cq_norm_g, ckv_norm_g, w_uq, w_ukv, w_oa, w_ob, w_oc, w_out, ada_ffn_w, ada_ffn_b, norm_ffn_g, router_w, router_bias, exp_w1, exp_w3, exp_w2, sh_w1, sh_w3, sh_w2, final_g):
    B, S, D = x.shape
    depth = w_in.shape[0]
    mod_mix = adaln_rows(c, ada_mix_w, ada_mix_b)
    mod_ffn = adaln_rows(c, ada_ffn_w, ada_ffn_b)
    streams = TOKEN_STREAMS if B % TOKEN_STREAMS == 0 else 1
    Bs = B // streams
    Ns = Bs * S
    x_all = x.reshape(B * S, D)
    xs = [None] * streams
    out_all = None
    pos_cols = [positions[s * Bs:(s + 1) * Bs].reshape(Ns, 1) for s in range(streams)]
    for l in range(depth):
        last = l == depth - 1
        w_in_l = _permute_w_in(w_in[l])
        mla_w = _mla_weights(cq_norm_g[l], ckv_norm_g[l], w_uq[l], w_ukv[l])
        mix_w = (norm_ffn_g[l], pool_w[l].astype(BF16), pool_scale[l],
                 w_oa[l].astype(BF16), w_ob[l].astype(BF16), w_oc[l].astype(BF16), w_out[l].astype(BF16),
                 router_w[l].T.astype(BF16), sh_w1[l].astype(BF16), sh_w3[l].astype(BF16), sh_w2[l].astype(BF16))
        for s in range(streams):
            x2, row0 = (x_all, s * Ns) if l == 0 else (xs[s], 0)
            mod1 = mod_mix[l, s * Bs:(s + 1) * Bs].reshape(Bs, 1, 3 * D)
            mod2 = mod_ffn[l, s * Bs:(s + 1) * Bs].reshape(Bs, 1, 3 * D)
            gu, lat, *qkv = in_projection(x2, norm_mix_g[l], mod1, w_in_l, S, row0)
            dil = [dilated_attention(qkv[2 * g], qkv[2 * g + 1]) for g in range(len(DIL_GROUPS))]
            q_all, k_all, vt_all = mla_prep(lat, pos_cols[s], *mla_w, Bs, S)
            yc = mla_attention(q_all, k_all, vt_all, Bs, S)
            xmid, h2a, h2b, logits_t = mix_out(x2, gu, dil, yc, mod1, mod2, *mix_w, S, row0)
            dest, w_k, counts = route(logits_t, router_bias[l])
            blk_e, nvalid = block_tables(counts, Ns)
            n_slots = blk_e.shape[0] * SLOT_BLOCK
            xa = sc_scatter_rows(h2a, dest, n_slots)
            xb = sc_scatter_rows(h2b, dest, n_slots)
            ya, yb = routed_experts(xa, xb, blk_e, nvalid, exp_w1, exp_w3, exp_w2, l)
            oa = sc_gather_rows(ya, dest)
            ob = sc_gather_rows(yb, dest)
            if last:
                out_all = combine(xmid, oa, ob, w_k.T, mod2, final_g, S, True, B * S, s * Ns, out_all)
            else:
                xs[s] = combine(xmid, oa, ob, w_k.T, mod2, final_g, S, False)
    return out_all.reshape(B, S, D)
```

```python
import functools
import math

import jax
import jax.numpy as jnp
from jax import lax
from jax.experimental import pallas as pl
from jax.experimental.pallas import tpu as pltpu
from jax.experimental.pallas import tpu_sc as plsc

F32 = jnp.float32
BF16 = jnp.bfloat16
HIGHEST = lax.Precision.HIGHEST

D_MODEL = 1024
HEAD_DIM_A = 64
HEADS_PER_GROUP_A = 4
DIL_GROUPS = ((128, 1), (512, 4), (2048, 16))
GROUP_W = HEADS_PER_GROUP_A * HEAD_DIM_A
DA = GROUP_W * len(DIL_GROUPS)
POOL_WINDOWS = (2, 4, 8, 16)
POOL_GROUP_DIM = 128
DB = POOL_GROUP_DIM * len(POOL_WINDOWS)
POOL_HALO = 16
N_HEADS_C = 8
QK_NOPE = 64
QK_ROPE = 32
V_DIM = 64
Q_LORA = 384
KV_LORA = 256
DC = N_HEADS_C * V_DIM
HEAD_PAD_C = 128
ROPE_THETA = 10000.0
N_EXPERTS = 64
TOP_K = 8
N_GROUPS = 8
TOPK_GROUPS = 4
GROUP_SIZE = N_EXPERTS // N_GROUPS
EXPERT_FF = 256
ROUTED_SCALE = 2.5
EPS = 1e-6
NEG = -1e30
Q_BLOCK = 128

LAT_W = 768
GU_W = 3 * D_MODEL + DB
IN_OUT_WIDTHS = (GU_W, LAT_W) + (2 * GROUP_W, GROUP_W) * len(DIL_GROUPS)

VMEM_LIMIT = 56 * 1024 * 1024


def _cp(sem, vmem=None):
    return pltpu.CompilerParams(dimension_semantics=sem, vmem_limit_bytes=vmem)


def _silu(v):
    return v * jax.nn.sigmoid(v)


def _nt_dot(a, b):
    return lax.dot_general(a, b, (((1,), (1,)), ((), ())), preferred_element_type=F32)


PACK_W = D_MODEL // 4
_HI_MASK = -65536


def _bf16_bits(v):
    return lax.bitcast_convert_type(v.astype(BF16).astype(F32), jnp.int32)


def _pack_row_halves(v):
    halves = []
    for h in range(2):
        lo = _bf16_bits(v[:, (2 * h) * PACK_W:(2 * h + 1) * PACK_W])
        hi = _bf16_bits(v[:, (2 * h + 1) * PACK_W:(2 * h + 2) * PACK_W])
        halves.append(lax.shift_right_logical(lo, 16) | (hi & _HI_MASK))
    return halves


def _unpack_row_halves(wa, wb):
    parts = []
    for w in (wa, wb):
        parts.append(lax.bitcast_convert_type(lax.shift_left(w, 16), F32))
        parts.append(lax.bitcast_convert_type(w & _HI_MASK, F32))
    return jnp.concatenate(parts, axis=1)


def _adaln_kernel(c_ref, w_ref, b_ref, o_ref):
    s = _silu(c_ref[...])
    o_ref[0] = jnp.dot(s, w_ref[0], preferred_element_type=F32, precision=HIGHEST) + b_ref[0]


def adaln_rows(c, w, b):
    L, D, D3 = w.shape
    B = c.shape[0]
    tn = 1024
    return pl.pallas_call(
        _adaln_kernel,
        grid=(L, D3 // tn),
        in_specs=[
            pl.BlockSpec((B, D), lambda l, j: (0, 0)),
            pl.BlockSpec((1, D, tn), lambda l, j: (l, 0, j)),
            pl.BlockSpec((1, 1, tn), lambda l, j: (l, 0, j)),
        ],
        out_specs=pl.BlockSpec((1, B, tn), lambda l, j: (l, 0, j)),
        out_shape=jax.ShapeDtypeStruct((L, B, D3), F32),
        compiler_params=_cp(("parallel", "parallel")),
        name="adaln_rows",
    )(c, w, b.reshape(L, 1, D3))


LANES = 128


def _inproj_kernel(x_ref, g_ref, mod_ref, w_ref, *refs, chunk):
    o_refs, scr = refs[:-1], refs[-1]
    D = x_ref.shape[1]
    x = x_ref[...]
    y = x * lax.rsqrt(jnp.mean(x * x, axis=-1, keepdims=True) + EPS) * g_ref[...]
    mod = mod_ref[0]
    h = (y * (1.0 + mod[:, D:2 * D]) + mod[:, :D]).astype(BF16)
    col = 0
    for o_ref in o_refs:
        width = o_ref.shape[-1]
        if o_ref.ndim == 2:
            for c0 in range(0, width, chunk):
                cw = min(chunk, width - c0)
                o_ref[:, c0:c0 + cw] = jnp.dot(
                    h, w_ref[:, col + c0:col + c0 + cw], preferred_element_type=F32).astype(o_ref.dtype)
        else:
            dil, rows = o_ref.shape[1], o_ref.shape[2]
            z = jnp.dot(h, w_ref[:, col:col + width], preferred_element_type=F32)
            if dil == 1:
                o_ref[0, 0] = z.astype(o_ref.dtype)
            else:
                for c in range(width // LANES):
                    scr[c] = z[:, c * LANES:(c + 1) * LANES]
                for r in range(dil):
                    o_ref[0, r] = jnp.concatenate(
                        [scr[c, pl.ds(r, rows, stride=dil), :] for c in range(width // LANES)],
                        axis=1).astype(o_ref.dtype)
        col += width


def in_projection(x2, g, mod, w, seq, row0=0):
    D = x2.shape[1]
    B = mod.shape[0]
    N = B * seq
    tm = 512
    tpb = seq // tm
    tile0 = row0 // tm
    out_specs = [pl.BlockSpec((tm, wd), lambda i: (i, 0)) for wd in IN_OUT_WIDTHS[:2]]
    out_shape = [jax.ShapeDtypeStruct((N, wd), BF16) for wd in IN_OUT_WIDTHS[:2]]
    for grp, (_, dil) in enumerate(DIL_GROUPS):
        for wd in IN_OUT_WIDTHS[2 + 2 * grp:4 + 2 * grp]:
            out_specs.append(pl.BlockSpec((1, dil, tm // dil, wd), lambda i: (i // tpb, 0, i % tpb, 0)))
            out_shape.append(jax.ShapeDtypeStruct((B, dil, seq // dil, wd), BF16))
    return pl.pallas_call(
        functools.partial(_inproj_kernel, chunk=512),
        grid=(N // tm,),
        in_specs=[
            pl.BlockSpec((tm, D), lambda i: (i + tile0, 0)),
            pl.BlockSpec((1, D), lambda i: (0, 0)),
            pl.BlockSpec((1, 1, 3 * D), lambda i: (i // tpb, 0, 0)),
            pl.BlockSpec(w.shape, lambda i: (0, 0), pipeline_mode=pl.Buffered(1)),
        ],
        out_specs=out_specs,
        out_shape=out_shape,
        scratch_shapes=[pltpu.VMEM((max(IN_OUT_WIDTHS[2:]) // LANES, tm, LANES), F32)],
        compiler_params=_cp(("parallel",), VMEM_LIMIT),
        name="in_projection",
    )(x2, g.reshape(1, D), mod, w)


def _dilated_kernel(q_ref, kc_ref, kp_ref, vc_ref, vp_ref, o_ref, lse_ref):
    i = pl.program_id(1)
    T = Q_BLOCK
    key = lax.broadcasted_iota(jnp.int32, (T, T), 0)
    qry = lax.broadcasted_iota(jnp.int32, (T, T), 1)
    valid_c = key <= qry
    near = key >= qry
    run = q_ref.shape[1] // T
    heads = [slice(h * HEAD_DIM_A, (h + 1) * HEAD_DIM_A) for h in range(HEADS_PER_GROUP_A)]

    def transposed(v):
        return v.astype(F32).T.astype(BF16)

    vts = [transposed(vc_ref[0, j * T:(j + 1) * T, :]) for j in range(run)]
    vt_before = transposed(vp_ref[0])

    def blocks(j):
        rows = slice(j * T, (j + 1) * T)
        if j == 0:
            return rows, kc_ref[0, rows, :], vts[0], kp_ref[0], vt_before, near & (i > 0)
        before = slice((j - 1) * T, j * T)
        return rows, kc_ref[0, rows, :], vts[j], kc_ref[0, before, :], vts[j - 1], near

    scores, probs = {}, {}
    for j in range(run):
        rows, kc, _, kp, _, valid_p = blocks(j)
        q = q_ref[0, rows, :]
        for h, sl in enumerate(heads):
            qh = q[:, sl]
            scores[j, h] = (jnp.where(valid_c, _nt_dot(kc[:, sl], qh), NEG),
                            jnp.where(valid_p, _nt_dot(kp[:, sl], qh), NEG))
    for (j, h), (sc, sp) in scores.items():
        m = jnp.maximum(jnp.max(sc, axis=0, keepdims=True), jnp.max(sp, axis=0, keepdims=True))
        pc = jnp.exp(sc - m)
        pp = jnp.exp(sp - m)
        den = jnp.sum(pc, axis=0, keepdims=True) + jnp.sum(pp, axis=0, keepdims=True)
        probs[j, h] = (pc.astype(BF16), pp.astype(BF16), den, m + jnp.log(den))
    for j in range(run):
        rows, _, vtc, _, vtp, _ = blocks(j)
        outs = []
        for h, sl in enumerate(heads):
            pc, pp, den, _ = probs[j, h]
            o = jnp.dot(vtc[sl, :], pc, preferred_element_type=F32) + jnp.dot(vtp[sl, :], pp, preferred_element_type=F32)
            outs.append(o / den)
        o_ref[0, rows, :] = jnp.concatenate(outs, axis=0).T.astype(o_ref.dtype)
        spread = LSE_LANES // len(heads)
        lse_t = jnp.concatenate([jnp.broadcast_to(probs[j, h][3], (spread, T)) for h in range(len(heads))], axis=0)
        lse_ref[0, rows, :] = lse_t.T


DILATED_RUN = 4


LSE_LANES = 128


def dilated_attention(qk, v):
    batch, dilation, L, _ = qk.shape
    nb = L // Q_BLOCK
    run = min(DILATED_RUN, nb)
    qk_r = qk.reshape(batch * dilation, L, 2 * GROUP_W)
    v_r = v.reshape(batch * dilation, L, GROUP_W)
    before = lambda i: jnp.maximum(i * run - 1, 0)
    o, lse = pl.pallas_call(
        _dilated_kernel,
        grid=(batch * dilation, nb // run),
        in_specs=[
            pl.BlockSpec((1, run * Q_BLOCK, GROUP_W), lambda s, i: (s, i, 0)),
            pl.BlockSpec((1, run * Q_BLOCK, GROUP_W), lambda s, i: (s, i, 1)),
            pl.BlockSpec((1, Q_BLOCK, GROUP_W), lambda s, i: (s, before(i), 1)),
            pl.BlockSpec((1, run * Q_BLOCK, GROUP_W), lambda s, i: (s, i, 0)),
            pl.BlockSpec((1, Q_BLOCK, GROUP_W), lambda s, i: (s, before(i), 0)),
        ],
        out_specs=[
            pl.BlockSpec((1, run * Q_BLOCK, GROUP_W), lambda s, i: (s, i, 0)),
            pl.BlockSpec((1, run * Q_BLOCK, LSE_LANES), lambda s, i: (s, i, 0)),
        ],
        out_shape=[
            jax.ShapeDtypeStruct((batch * dilation, L, GROUP_W), BF16),
            jax.ShapeDtypeStruct((batch * dilation, L, LSE_LANES), F32),
        ],
        compiler_params=_cp(("parallel", "parallel")),
        name=f"dilated_attention_d{dilation}",
    )(qk_r, qk_r, qk_r, v_r, v_r)
    return o.reshape(batch, dilation, L, GROUP_W), lse.reshape(batch, dilation, L, LSE_LANES)


def _mla_prep_kernel(lat_ref, pos_ref, gq_ref, gkv_ref, wq_ref, wk_ref, wvt_ref, freq_ref, spread_ref, one_ref,
                     q_ref, k_ref, vt_ref):
    HP = N_HEADS_C * HEAD_PAD_C
    lat = lat_ref[...].astype(F32)
    cq = lat[:, :Q_LORA]
    ckr = lat[:, Q_LORA:]
    zq = (cq * lax.rsqrt(jnp.mean(cq * cq, axis=-1, keepdims=True) + EPS) * gq_ref[...]).astype(BF16)
    lane = lax.broadcasted_iota(jnp.int32, ckr.shape, 1)
    is_kv = lane < KV_LORA
    ms = jnp.sum(jnp.where(is_kv, ckr * ckr, 0.0), axis=-1, keepdims=True) * (1.0 / KV_LORA)
    zkv = (ckr * jnp.where(is_kv, lax.rsqrt(ms + EPS) * gkv_ref[...], 1.0)).astype(BF16)
    qq = jnp.dot(zq, wq_ref[...], preferred_element_type=F32)
    kk = jnp.dot(zkv, wk_ref[...], preferred_element_type=F32)
    ang_t = freq_ref[...] * pos_ref[0].astype(F32)

    def to_lanes(t):
        hi = t.astype(BF16)
        lo = (t - hi.astype(F32)).astype(BF16)
        tn_dot = lambda a: lax.dot_general(a, spread_ref[...], (((0,), (0,)), ((), ())), preferred_element_type=F32)
        return tn_dot(hi) + tn_dot(lo)

    cos = to_lanes(jnp.cos(ang_t)) + one_ref[...]
    sin = to_lanes(jnp.sin(ang_t))
    for h in range(N_HEADS_C):
        lo, hi = h * HEAD_PAD_C, (h + 1) * HEAD_PAD_C
        q_ref[:, lo:hi] = (qq[:, lo:hi] * cos + qq[:, HP + lo:HP + hi] * sin).astype(q_ref.dtype)
        k_ref[:, lo:hi] = (kk[:, lo:hi] * cos + kk[:, HP + lo:HP + hi] * sin).astype(k_ref.dtype)
    vt_ref[0] = _nt_dot(wvt_ref[...], zkv).astype(vt_ref.dtype)


def _mla_weights(cq_g, ckv_g, w_uq, w_ukv):
    H, HPAD, half = N_HEADS_C, HEAD_PAD_C, QK_ROPE // 2
    scale = (QK_NOPE + QK_ROPE) ** -0.5 * math.log2(math.e)
    wq = w_uq.reshape(Q_LORA, H, QK_NOPE + QK_ROPE) * scale
    q_lin = jnp.pad(wq, ((0, 0), (0, 0), (0, HPAD - QK_NOPE - QK_ROPE)))
    r1, r2 = wq[..., QK_NOPE:QK_NOPE + half], wq[..., QK_NOPE + half:]
    q_sw = jnp.concatenate([jnp.zeros((Q_LORA, H, QK_NOPE), F32), -r2, r1,
                            jnp.zeros((Q_LORA, H, HPAD - QK_NOPE - QK_ROPE), F32)], axis=-1)
    wq_big = jnp.concatenate([q_lin.reshape(Q_LORA, H * HPAD), q_sw.reshape(Q_LORA, H * HPAD)], axis=1)

    rows = LAT_W - Q_LORA
    wkv = w_ukv.reshape(KV_LORA, H, QK_NOPE + V_DIM)
    eye = jnp.eye(QK_ROPE, dtype=F32)
    k_lin = jnp.zeros((rows, H, HPAD), F32)
    k_lin = k_lin.at[:KV_LORA, :, :QK_NOPE].set(wkv[..., :QK_NOPE])
    k_lin = k_lin.at[KV_LORA:KV_LORA + QK_ROPE, :, QK_NOPE:QK_NOPE + QK_ROPE].set(
        jnp.broadcast_to(eye[:, None, :], (QK_ROPE, H, QK_ROPE)))
    swap = jnp.zeros((QK_ROPE, QK_ROPE), F32).at[half:, :half].set(-jnp.eye(half)).at[:half, half:].set(jnp.eye(half))
    k_sw = jnp.zeros((rows, H, HPAD), F32)
    k_sw = k_sw.at[KV_LORA:KV_LORA + QK_ROPE, :, QK_NOPE:QK_NOPE + QK_ROPE].set(
        jnp.broadcast_to(swap[:, None, :], (QK_ROPE, H, QK_ROPE)))
    v_w = jnp.zeros((rows, H, V_DIM), F32).at[:KV_LORA].set(wkv[..., QK_NOPE:])
    wk_big = jnp.concatenate([k_lin.reshape(rows, H * HPAD), k_sw.reshape(rows, H * HPAD)], axis=1)
    wv_t = v_w.reshape(rows, H * V_DIM).T

    gkv = jnp.concatenate([ckv_g, jnp.ones((rows - KV_LORA,), F32)]).reshape(1, rows)
    return cq_g.reshape(1, Q_LORA), gkv, wq_big.astype(BF16), wk_big.astype(BF16), wv_t.astype(BF16)


def _rope_tables():
    half = QK_ROPE // 2
    freqs = (ROPE_THETA ** (-jnp.arange(0, QK_ROPE, 2, dtype=F32) / QK_ROPE)).reshape(half, 1)
    lane = jnp.arange(HEAD_PAD_C)[None, :]
    j = jnp.arange(half)[:, None]
    spread = (lane == QK_NOPE + j) | (lane == QK_NOPE + half + j)
    off_rope = ~jnp.any(spread, axis=0, keepdims=True)
    return freqs, spread.astype(BF16), off_rope.astype(F32)


def mla_prep(lat, positions, gq, gkv, wq_big, wk_big, wv_t, batch, seq):
    N = lat.shape[0]
    HP = N_HEADS_C * HEAD_PAD_C
    tm = 512
    tpb = seq // tm
    freqs, spread, off_rope = _rope_tables()
    pos_rows = positions.reshape(N // tm, 1, tm)
    const = lambda shape: pl.BlockSpec(shape, lambda i: (0, 0))
    return pl.pallas_call(
        _mla_prep_kernel,
        grid=(N // tm,),
        in_specs=[
            pl.BlockSpec((tm, LAT_W), lambda i: (i, 0)),
            pl.BlockSpec((1, 1, tm), lambda i: (i, 0, 0)),
            const(gq.shape), const(gkv.shape), const(wq_big.shape), const(wk_big.shape), const(wv_t.shape),
            const(freqs.shape), const(spread.shape), const(off_rope.shape),
        ],
        out_specs=[
            pl.BlockSpec((tm, HP), lambda i: (i, 0)),
            pl.BlockSpec((tm, HP), lambda i: (i, 0)),
            pl.BlockSpec((1, DC, tm), lambda i: (i // tpb, 0, i % tpb)),
        ],
        out_shape=[
            jax.ShapeDtypeStruct((N, HP), BF16),
            jax.ShapeDtypeStruct((N, HP), BF16),
            jax.ShapeDtypeStruct((batch, DC, seq), BF16),
        ],
        compiler_params=_cp(("parallel",), VMEM_LIMIT),
        name="mla_prep",
    )(lat, pos_rows, gq, gkv, wq_big, wk_big, wv_t, freqs, spread, off_rope)


HEADS_PER_STEP_C = 8
FLASH_Q_CHUNK = 256


def _mla_flash_kernel(qi_ref, ki_ref, q_ref, k_ref, vt_ref, o_ref, m_sc, l_sc, acc_sc):
    t = pl.program_id(2)
    qi, ki = qi_ref[t], ki_ref[t]

    @pl.when(ki == 0)
    def _():
        m_sc[...] = jnp.full(m_sc.shape, NEG, F32)
        l_sc[...] = jnp.zeros(l_sc.shape, F32)
        acc_sc[...] = jnp.zeros(acc_sc.shape, F32)

    def step(masked):
        T = q_ref.shape[1]
        if masked:
            key = lax.broadcasted_iota(jnp.int32, (T, T), 0)
            qry = lax.broadcasted_iota(jnp.int32, (T, T), 1)
            keep = key <= qry
        chains = [(h, c) for h in range(HEADS_PER_STEP_C) for c in range(T // FLASH_Q_CHUNK)]
        scores, probs, alphas = {}, {}, {}

        def qk(h, c):
            qs = slice(c * FLASH_Q_CHUNK, (c + 1) * FLASH_Q_CHUNK)
            q = q_ref[0, qs, h * HEAD_PAD_C:(h + 1) * HEAD_PAD_C]
            k = k_ref[0, :, h * HEAD_PAD_C:(h + 1) * HEAD_PAD_C]
            st = _nt_dot(k, q)
            scores[h, c] = jnp.where(keep[:, qs], st, NEG) if masked else st

        def softmax(h, c):
            qs = slice(c * FLASH_Q_CHUNK, (c + 1) * FLASH_Q_CHUNK)
            st = scores.pop((h, c))
            m_prev = m_sc[h, :, qs]
            m_new = jnp.maximum(m_prev, jnp.max(st, axis=0, keepdims=True))
            alpha = jnp.exp2(m_prev - m_new)
            p = jnp.exp2(st - m_new)
            l_sc[h, :, qs] = alpha * l_sc[h, :, qs] + jnp.sum(p, axis=0, keepdims=True)
            m_sc[h, :, qs] = m_new
            probs[h, c], alphas[h, c] = p.astype(BF16), alpha

        def pv(h, c):
            qs = slice(c * FLASH_Q_CHUNK, (c + 1) * FLASH_Q_CHUNK)
            vt = vt_ref[0, h * V_DIM:(h + 1) * V_DIM, :]
            acc_sc[h, :, qs] = alphas.pop((h, c)) * acc_sc[h, :, qs] + jnp.dot(
                vt, probs.pop((h, c)), preferred_element_type=F32)

        for phase in (qk, softmax, pv):
            for ch in chains:
                phase(*ch)

    @pl.when(ki < qi)
    def _():
        step(False)

    @pl.when(ki == qi)
    def _():
        step(True)
        ot = jnp.concatenate([acc_sc[h] / l_sc[h] for h in range(HEADS_PER_STEP_C)], axis=0)
        o_ref[0] = ot.T.astype(o_ref.dtype)


def mla_attention(q_all, k_all, vt_all, batch, seq):
    T = 512
    nq = seq // T
    pairs = [(a, b) for a in range(nq) for b in range(a + 1)]
    qi_tab = jnp.asarray([p[0] for p in pairs], jnp.int32)
    ki_tab = jnp.asarray([p[1] for p in pairs], jnp.int32)
    hp = N_HEADS_C // HEADS_PER_STEP_C
    qw = HEADS_PER_STEP_C * HEAD_PAD_C
    vw = HEADS_PER_STEP_C * V_DIM
    q3 = q_all.reshape(batch, seq, -1)
    k3 = k_all.reshape(batch, seq, -1)
    grid_spec = pltpu.PrefetchScalarGridSpec(
        num_scalar_prefetch=2,
        grid=(batch, hp, len(pairs)),
        in_specs=[
            pl.BlockSpec((1, T, qw), lambda b, h, t, qi, ki: (b, qi[t], h)),
            pl.BlockSpec((1, T, qw), lambda b, h, t, qi, ki: (b, ki[t], h)),
            pl.BlockSpec((1, vw, T), lambda b, h, t, qi, ki: (b, h, ki[t])),
        ],
        out_specs=pl.BlockSpec((1, T, vw), lambda b, h, t, qi, ki: (b, qi[t], h)),
        scratch_shapes=[
            pltpu.VMEM((HEADS_PER_STEP_C, 1, T), F32),
            pltpu.VMEM((HEADS_PER_STEP_C, 1, T), F32),
            pltpu.VMEM((HEADS_PER_STEP_C, V_DIM, T), F32),
        ],
    )
    o = pl.pallas_call(
        _mla_flash_kernel,
        grid_spec=grid_spec,
        out_shape=jax.ShapeDtypeStruct((batch, seq, DC), BF16),
        compiler_params=_cp(("parallel", "parallel", "arbitrary")),
        name="mla_attention",
    )(qi_tab, ki_tab, q3, k3, vt_all)
    return o.reshape(batch * seq, DC)


def _mixout_kernel(x_ref, gates_ref, ub_ref, ubh_ref, o1_ref, o2_ref, o3_ref, l1_ref, l2_ref, l3_ref, yc_ref,
                   mod1_ref, mod2_ref, g2_ref, poolw_ref, pscale_ref, woa_ref, wob_ref, woc_ref, wout_ref,
                   rwt_ref, sw1_ref, sw3_ref, sw2_ref, spread_ref,
                   xmid_ref, h2a_ref, h2b_ref, logit_ref, *scratch, tiles_per_batch):
    D = x_ref.shape[1]
    tm = x_ref.shape[0]
    tile = pl.program_id(0) % tiles_per_batch
    o_scrs, l_scrs = scratch[:3], scratch[3:]

    def token_order(ref, scr):
        dil, rows, width = ref.shape[1:]
        if dil == 1:
            return ref[0, 0].astype(F32)
        for r in range(dil):
            v = ref[0, r].astype(F32)
            for c in range(width // LANES):
                scr[c, pl.ds(r, rows, stride=dil), :] = v[:, c * LANES:(c + 1) * LANES]
        return jnp.concatenate([scr[c] for c in range(width // LANES)], axis=1)

    outs = [token_order(r, s) for r, s in zip((o1_ref, o2_ref, o3_ref), o_scrs)]
    l1, l2, l3 = [token_order(r, s) for r, s in zip((l1_ref, l2_ref, l3_ref), l_scrs)]
    mx = jnp.maximum(jnp.maximum(l1, l2), l3)
    es = [jnp.exp(l1 - mx), jnp.exp(l2 - mx), jnp.exp(l3 - mx)]
    inv = 1.0 / (es[0] + es[1] + es[2])
    ya = jnp.zeros((tm, GROUP_W), F32)
    for e, o in zip(es, outs):
        w = e * inv
        w_hi = w.astype(BF16)
        w_lo = (w - w_hi.astype(F32)).astype(BF16)
        w_wide = (jnp.dot(w_hi, spread_ref[...], preferred_element_type=F32)
                  + jnp.dot(w_lo, spread_ref[...], preferred_element_type=F32))
        ya = ya + w_wide * o
    a_out = jnp.dot(ya.astype(BF16), woa_ref[...], preferred_element_type=F32)

    u = ub_ref[...].astype(F32)
    halo = jnp.where(tile > 0, ubh_ref[...].astype(F32), 0.0)
    ext = jnp.concatenate([halo, u], axis=0)
    t_seq = tile * tm + lax.broadcasted_iota(jnp.int32, (tm, 1), 0)
    pooled = []
    for gi, w in enumerate(POOL_WINDOWS):
        sl = slice(gi * POOL_GROUP_DIM, (gi + 1) * POOL_GROUP_DIM)
        acc = ext[:, sl]
        k = 1
        while k < w:
            acc = acc + pltpu.roll(acc, k, axis=0)
            k *= 2
        cnt = jnp.minimum(t_seq + 1, w).astype(F32)
        pg = acc[POOL_HALO:] / cnt - u[:, sl]
        pooled.append(jnp.dot(pg.astype(BF16), poolw_ref[gi], preferred_element_type=F32))
    yb = jnp.concatenate(pooled, axis=1) * pscale_ref[...]
    b_out = jnp.dot(yb.astype(BF16), wob_ref[...], preferred_element_type=F32)
    c_out = jnp.dot(yc_ref[...], woc_ref[...], preferred_element_type=F32)

    g = gates_ref[...].astype(F32)
    mix = (jax.nn.sigmoid(g[:, :D]) * a_out + jax.nn.sigmoid(g[:, D:2 * D]) * b_out
           + jax.nn.sigmoid(g[:, 2 * D:]) * c_out)
    tok = jnp.dot(mix.astype(BF16), wout_ref[...], preferred_element_type=F32)
    xn = x_ref[...] + mod1_ref[0][:, 2 * D:] * tok

    mod2 = mod2_ref[0]
    y = xn * lax.rsqrt(jnp.mean(xn * xn, axis=-1, keepdims=True) + EPS) * g2_ref[...]
    h2 = y * (1.0 + mod2[:, D:2 * D]) + mod2[:, :D]
    h2b = h2.astype(BF16)
    h2a_ref[...], h2b_ref[...] = _pack_row_halves(h2b)
    logit_ref[...] = _nt_dot(rwt_ref[...], h2b)
    hid = _silu(jnp.dot(h2b, sw1_ref[...], preferred_element_type=F32)) * jnp.dot(
        h2b, sw3_ref[...], preferred_element_type=F32)
    shared = jnp.dot(hid.astype(BF16), sw2_ref[...], preferred_element_type=F32)
    xmid_ref[...] = xn + mod2[:, 2 * D:] * shared


def mix_out(x2, gu, dil, yc, mod1, mod2, g2, pool_w, pool_scale, w_oa, w_ob, w_oc, w_out, rwt, sw1, sw3, sw2, seq,
            row0=0):
    D = x2.shape[1]
    N = gu.shape[0]
    tm = 512
    tpb = seq // tm
    tile0 = row0 // tm
    (o1, l1), (o2, l2), (o3, l3) = dil
    row = lambda w, c=0: pl.BlockSpec((tm, w), lambda i: (i, c))
    by_residue = lambda a: pl.BlockSpec(
        (1, a.shape[1], tm // a.shape[1], a.shape[3]), lambda i: (i // tpb, 0, i % tpb, 0))
    heads = HEADS_PER_GROUP_A
    spread = (jnp.arange(LSE_LANES)[:, None] == (jnp.arange(GROUP_W)[None, :] // HEAD_DIM_A) * (LSE_LANES // heads)
              ).astype(BF16)
    const2 = lambda a: pl.BlockSpec(a.shape, lambda i: (0,) * a.ndim, pipeline_mode=pl.Buffered(1))
    modspec = pl.BlockSpec((1, 1, 3 * D), lambda i: (i // tpb, 0, 0))
    ub_col = 3 * D // DB
    halo_spec = pl.BlockSpec(
        (POOL_HALO, DB), lambda i: (jnp.maximum(i * (tm // POOL_HALO) - 1, 0), ub_col))
    weights = [g2.reshape(1, D), pool_w, pool_scale.reshape(1, DB), w_oa, w_ob, w_oc, w_out, rwt, sw1, sw3, sw2,
               spread]
    return pl.pallas_call(
        functools.partial(_mixout_kernel, tiles_per_batch=tpb),
        grid=(N // tm,),
        in_specs=[
            pl.BlockSpec((tm, D), lambda i: (i + tile0, 0)), row(3 * D), row(DB, ub_col), halo_spec,
            by_residue(o1), by_residue(o2), by_residue(o3), by_residue(l1), by_residue(l2), by_residue(l3), row(DC),
            modspec, modspec,
        ] + [const2(a) for a in weights],
        scratch_shapes=[pltpu.VMEM((GROUP_W // LANES, tm, LANES), F32)] * 3
        + [pltpu.VMEM((LSE_LANES // LANES, tm, LANES), F32)] * 3,
        out_specs=[row(D), row(PACK_W), row(PACK_W), pl.BlockSpec((N_EXPERTS, tm), lambda i: (0, i))],
        out_shape=[
            jax.ShapeDtypeStruct((N, D), F32),
            jax.ShapeDtypeStruct((N, PACK_W), jnp.int32),
            jax.ShapeDtypeStruct((N, PACK_W), jnp.int32),
            jax.ShapeDtypeStruct((N_EXPERTS, N), F32),
        ],
        compiler_params=_cp(("parallel",), VMEM_LIMIT),
        name="mix_out",
    )(x2, gu, gu, gu, o1, o2, o3, l1, l2, l3, yc, mod1, mod2, *weights)


def _pick_rows(table, picks):
    G, GS = N_GROUPS, GROUP_SIZE
    eio = lax.broadcasted_iota(jnp.int32, (GS, table.shape[1]), 0)
    rows = []
    for k in range(TOP_K):
        idx = picks[k:k + 1]
        parts = [jnp.sum(jnp.where(eio + g * GS == idx, table[g * GS:(g + 1) * GS], 0.0), axis=0, keepdims=True)
                 for g in range(G)]
        rows.append(functools.reduce(jnp.add, parts))
    return jnp.concatenate(rows, axis=0)


def _route_choose(lg_ref, bias_ref):
    G, GS = N_GROUPS, GROUP_SIZE
    scores = jax.nn.sigmoid(lg_ref[...])
    sel = scores + bias_ref[...]
    tn = sel.shape[1]
    eio = lax.broadcasted_iota(jnp.int32, (GS, tn), 0)
    ninf = -jnp.inf

    gs = []
    for g in range(G):
        v = sel[g * GS:(g + 1) * GS]
        m1 = jnp.max(v, axis=0, keepdims=True)
        i1 = jnp.min(jnp.where(v == m1, eio, GS), axis=0, keepdims=True)
        m2 = jnp.max(jnp.where(eio == i1, ninf, v), axis=0, keepdims=True)
        gs.append(m1 + m2)
    gsm = jnp.concatenate(gs, axis=0)
    gio = lax.broadcasted_iota(jnp.int32, (G, tn), 0)
    rank = jnp.zeros((G, tn), jnp.int32)
    for g2 in range(G):
        beats = (gs[g2] > gsm) | ((gs[g2] == gsm) & (g2 < gio))
        rank = rank + beats.astype(jnp.int32)
    gsel = rank < TOPK_GROUPS

    vs = [jnp.where(gsel[g:g + 1], sel[g * GS:(g + 1) * GS], NEG) for g in range(G)]
    eid = [eio + g * GS for g in range(G)]
    chosen = [jnp.zeros((GS, tn), jnp.bool_) for _ in range(G)]
    picks = []
    for _ in range(TOP_K):
        m = functools.reduce(jnp.maximum, [jnp.max(v, axis=0, keepdims=True) for v in vs])
        idx = functools.reduce(jnp.minimum, [
            jnp.min(jnp.where(v == m, e, N_EXPERTS), axis=0, keepdims=True) for v, e in zip(vs, eid)])
        picks.append(idx)
        for g in range(G):
            hit = eid[g] == idx
            chosen[g] = chosen[g] | hit
            vs[g] = jnp.where(hit, ninf, vs[g])
    mask = jnp.concatenate(chosen, axis=0).astype(F32)
    return scores, jnp.concatenate(picks, axis=0), mask


def _route_kernel(lg_ref, bias_ref, tri_ref, dest_ref, w_ref, cnt_ref, run_sc, start_sc, mask_sc, picks_sc,
                  *, slot_block):
    phase = pl.program_id(0)
    step = pl.program_id(1)
    tn = lg_ref.shape[1]
    cols = pl.ds(pl.multiple_of(step * tn, tn), tn)

    @pl.when(phase == 0)
    def _():
        @pl.when(step == 0)
        def _():
            run_sc[...] = jnp.zeros(run_sc.shape, F32)

        scores, picks, mask = _route_choose(lg_ref, bias_ref)
        wk = _pick_rows(scores, picks)
        w_ref[0] = wk / jnp.sum(wk, axis=0, keepdims=True) * ROUTED_SCALE
        dest_ref[0] = jnp.zeros(dest_ref.shape[1:], dest_ref.dtype)
        mask_sc[:, cols] = mask.astype(BF16)
        picks_sc[:, cols] = picks
        run_sc[...] = run_sc[...] + jnp.sum(mask, axis=1, keepdims=True)

    @pl.when(phase == 1)
    def _():
        @pl.when(step == 0)
        def _():
            counts = run_sc[...].astype(jnp.int32)
            cnt_ref[...] = jnp.broadcast_to(counts, cnt_ref.shape)
            shift = slot_block.bit_length() - 1
            padded = lax.shift_left(lax.shift_right_logical(counts + (slot_block - 1), shift), shift).astype(F32)
            r = lax.broadcasted_iota(jnp.int32, (N_EXPERTS, N_EXPERTS), 0)
            c = lax.broadcasted_iota(jnp.int32, (N_EXPERTS, N_EXPERTS), 1)
            as_row = jnp.sum(jnp.where(r == c, padded, 0.0), axis=0, keepdims=True)
            start_sc[...] = jnp.sum(jnp.where(c < r, as_row, 0.0), axis=1, keepdims=True)
            run_sc[...] = jnp.zeros(run_sc.shape, F32)

        mask_b = mask_sc[:, cols]
        mask = mask_b.astype(F32)
        before = jnp.dot(mask_b, tri_ref[...], preferred_element_type=F32) - mask
        slot = start_sc[...] + run_sc[...] + before
        dest_ref[0] = _pick_rows(slot, picks_sc[:, cols]).astype(jnp.int32)
        w_ref[0] = jnp.zeros(w_ref.shape[1:], w_ref.dtype)
        run_sc[...] = run_sc[...] + jnp.sum(mask, axis=1, keepdims=True)


SLOT_BLOCK = 512


def route(logits_t, bias):
    E, N = logits_t.shape
    tn = 1024
    tri = (jnp.arange(tn)[:, None] <= jnp.arange(tn)[None, :]).astype(BF16)
    plane = lambda: pl.BlockSpec((1, TOP_K, tn), lambda p, i: (p, 0, i))
    dest, w, cnt = pl.pallas_call(
        functools.partial(_route_kernel, slot_block=SLOT_BLOCK),
        grid=(2, N // tn),
        in_specs=[
            pl.BlockSpec((E, tn), lambda p, i: (0, i * (1 - p))),
            pl.BlockSpec((E, 1), lambda p, i: (0, 0)),
            pl.BlockSpec((tn, tn), lambda p, i: (0, 0)),
        ],
        out_specs=[plane(), plane(), pl.BlockSpec((E, 128), lambda p, i: (0, 0))],
        out_shape=[
            jax.ShapeDtypeStruct((2, TOP_K, N), jnp.int32),
            jax.ShapeDtypeStruct((2, TOP_K, N), F32),
            jax.ShapeDtypeStruct((E, 128), jnp.int32),
        ],
        scratch_shapes=[pltpu.VMEM((E, 1), F32), pltpu.VMEM((E, 1), F32),
                        pltpu.VMEM((E, N), BF16), pltpu.VMEM((TOP_K, N), jnp.int32)],
        compiler_params=_cp(("arbitrary", "arbitrary")),
        name="route",
    )(logits_t, bias.reshape(E, 1), tri)
    return dest[1], w[0], cnt[:, 0]


def block_tables(counts, n_tokens):
    E = counts.shape[0]
    blk = SLOT_BLOCK
    nblk = (n_tokens * TOP_K + E * blk) // blk
    per_expert = (counts + blk - 1) // blk
    bend = jnp.cumsum(per_expert)
    bstart = bend - per_expert
    b = jnp.arange(nblk, dtype=jnp.int32)[:, None]
    owns = (bstart[None, :] <= b) & (b < bend[None, :])
    blk_e = jnp.minimum(jnp.sum(bend[None, :] <= b, axis=1), E - 1).astype(jnp.int32)
    rows_left = counts[None, :] - (b - bstart[None, :]) * blk
    nvalid = jnp.sum(jnp.where(owns, jnp.clip(rows_left, 0, blk), 0), axis=1)
    return blk_e, nvalid.astype(jnp.int32)


def _sc_mesh():
    return plsc.VectorSubcoreMesh(core_axis_name="c", subcore_axis_name="s")


SC_WINDOW = 128


def sc_scatter_rows(x, dest, n_slots):
    N, W = x.shape
    K = dest.shape[0]

    @functools.partial(pl.kernel, out_type=jax.ShapeDtypeStruct((n_slots, W), x.dtype), mesh=_sc_mesh(),
                       scratch_types=[])
    def scatter(x_hbm, i_hbm, o_hbm):
        def body(x_vmem, i_vmem):
            for k in range(K):
                pltpu.sync_copy(x_vmem, o_hbm.at[i_vmem.at[k]])

        pltpu.emit_pipeline(
            body,
            grid=(N // SC_WINDOW,),
            in_specs=[pl.BlockSpec((SC_WINDOW, W), lambda i: (i, 0)),
                      pl.BlockSpec((K, SC_WINDOW), lambda i: (0, i))],
            out_specs=[],
            core_axis_name=("c", "s"),
            dimension_semantics=(pltpu.PARALLEL,),
        )(x_hbm, i_hbm)

    return scatter(x, dest)


def sc_gather_rows(y, dest):
    W = y.shape[1]
    K, N = dest.shape

    @functools.partial(pl.kernel, out_type=jax.ShapeDtypeStruct((K, N, W), y.dtype), mesh=_sc_mesh(),
                       scratch_types=[])
    def gather(y_hbm, i_hbm, o_hbm):
        def body(i_vmem, o_vmem):
            pltpu.sync_copy(y_hbm.at[i_vmem.at[0, 0]], o_vmem.at[0])

        pltpu.emit_pipeline(
            body,
            grid=(K, N // SC_WINDOW),
            in_specs=[pl.BlockSpec((1, 1, SC_WINDOW), lambda k, i: (k, 0, i))],
            out_specs=[pl.BlockSpec((1, SC_WINDOW, W), lambda k, i: (k, i, 0))],
            core_axis_name=("c", "s"),
            dimension_semantics=(pltpu.PARALLEL, pltpu.PARALLEL),
        )(i_hbm, o_hbm)

    return gather(y, dest.reshape(K, 1, N))


def _expert_kernel(blk_e_ref, nvalid_ref, xa_ref, xb_ref, w1_ref, w3_ref, w2_ref, ya_ref, yb_ref,
                   w1_sc, w3_sc, w2_sc):
    b = pl.program_id(0)
    nv = nvalid_ref[b]
    prev_e = blk_e_ref[jnp.maximum(b - 1, 0)]

    @pl.when((b == 0) | (blk_e_ref[b] != prev_e))
    def _():
        w1_sc[...] = w1_ref[0, 0].astype(BF16)
        w3_sc[...] = w3_ref[0, 0].astype(BF16)
        w2_sc[...] = w2_ref[0, 0].astype(BF16)

    @pl.when(nv > 0)
    def _():
        x = _unpack_row_halves(xa_ref[...], xb_ref[...])
        rows = lax.broadcasted_iota(jnp.int32, x.shape, 0)
        x = jnp.where(rows < nv, x, 0.0).astype(BF16)
        hid = _silu(jnp.dot(x, w1_sc[...], preferred_element_type=F32)) * jnp.dot(
            x, w3_sc[...], preferred_element_type=F32)
        y = jnp.dot(hid.astype(BF16), w2_sc[...], preferred_element_type=F32)
        ya_ref[...], yb_ref[...] = _pack_row_halves(y)

    @pl.when(nv == 0)
    def _():
        ya_ref[...] = jnp.zeros(ya_ref.shape, ya_ref.dtype)
        yb_ref[...] = jnp.zeros(yb_ref.shape, yb_ref.dtype)


def routed_experts(xa, xb, blk_e, nvalid, w1, w3, w2, layer):
    P = xa.shape[0]
    blk = SLOT_BLOCK
    _, E, D, FF = w1.shape
    slots = lambda: pl.BlockSpec((blk, PACK_W), lambda b, be, nv: (b, 0))
    grid_spec = pltpu.PrefetchScalarGridSpec(
        num_scalar_prefetch=2,
        grid=(P // blk,),
        in_specs=[
            slots(), slots(),
            pl.BlockSpec((1, 1, D, FF), lambda b, be, nv: (layer, be[b], 0, 0)),
            pl.BlockSpec((1, 1, D, FF), lambda b, be, nv: (layer, be[b], 0, 0)),
            pl.BlockSpec((1, 1, FF, D), lambda b, be, nv: (layer, be[b], 0, 0)),
        ],
        out_specs=[slots(), slots()],
        scratch_shapes=[pltpu.VMEM((D, FF), BF16), pltpu.VMEM((D, FF), BF16), pltpu.VMEM((FF, D), BF16)],
    )
    return pl.pallas_call(
        _expert_kernel,
        grid_spec=grid_spec,
        out_shape=[jax.ShapeDtypeStruct((P, PACK_W), jnp.int32)] * 2,
        compiler_params=_cp(("arbitrary",), VMEM_LIMIT),
        name="routed_experts",
    )(blk_e, nvalid, xa, xb, w1, w3, w2)


def _combine_kernel(xmid_ref, oa_ref, ob_ref, w_ref, mod2_ref, fg_ref, *rest, final):
    out_ref = rest[-1]
    D = xmid_ref.shape[1]
    w = w_ref[...]
    acc = w[:, 0:1] * _unpack_row_halves(oa_ref[0], ob_ref[0])
    for k in range(1, TOP_K):
        acc = acc + w[:, k:k + 1] * _unpack_row_halves(oa_ref[k], ob_ref[k])
    x = xmid_ref[...] + mod2_ref[0][:, 2 * D:] * acc
    if final:
        x = x * lax.rsqrt(jnp.mean(x * x, axis=-1, keepdims=True) + EPS) * fg_ref[...]
    out_ref[...] = x


def combine(xmid, oa, ob, w_tok, mod2, final_g, seq, final, out_rows=None, row0=0, out_buf=None):
    N, D = xmid.shape
    tm = 256
    tpb = seq // tm
    tile0 = row0 // tm
    rows8 = lambda: pl.BlockSpec((TOP_K, tm, PACK_W), lambda i: (0, i, 0))
    in_specs = [
        pl.BlockSpec((tm, D), lambda i: (i, 0)),
        rows8(), rows8(),
        pl.BlockSpec((tm, TOP_K), lambda i: (i, 0)),
        pl.BlockSpec((1, 1, 3 * D), lambda i: (i // tpb, 0, 0)),
        pl.BlockSpec((1, D), lambda i: (0, 0)),
    ]
    args = [xmid, oa, ob, w_tok, mod2, final_g.reshape(1, D)]
    aliases = {}
    if out_buf is not None:
        in_specs.append(pl.BlockSpec(memory_space=pl.ANY))
        args.append(out_buf)
        aliases = {len(args) - 1: 0}
    return pl.pallas_call(
        functools.partial(_combine_kernel, final=final),
        grid=(N // tm,),
        in_specs=in_specs,
        out_specs=pl.BlockSpec((tm, D), lambda i: (i + tile0, 0)),
        out_shape=jax.ShapeDtypeStruct((out_rows or N, D), F32),
        input_output_aliases=aliases,
        compiler_params=_cp(("parallel",), VMEM_LIMIT),
        name="combine",
    )(*args)


TOKEN_STREAMS = 2


def _permute_w_in(w):
    ub = w[:, 3 * DA:3 * DA + DB]
    lat_lo = 3 * DA + DB
    lat_hi = lat_lo + Q_LORA + KV_LORA + QK_ROPE
    lat, gates = w[:, lat_lo:lat_hi], w[:, lat_hi:]
    pad = jnp.zeros((w.shape[0], LAT_W - (lat_hi - lat_lo)), w.dtype)
    parts = [gates, ub, lat, pad]
    for g in range(len(DIL_GROUPS)):
        sl = slice(g * GROUP_W, (g + 1) * GROUP_W)
        parts += [w[:, :DA][:, sl] * (HEAD_DIM_A ** -0.5), w[:, DA:2 * DA][:, sl], w[:, 2 * DA:3 * DA][:, sl]]
    return jnp.concatenate(parts, axis=1).astype(BF16)


def kernel(x, c, positions, ada_mix_w, ada_mix_b, norm_mix_g, w_in, pool_w, pool_scale, cq_norm_g, ckv_norm_g, w_uq, w_ukv, w_oa, w_ob, w_oc, w_out, ada_ffn_w, ada_ffn_b, norm_ffn_g, router_w, router_bias, exp_w1, exp_w3, exp_w2, sh_w1, sh_w3, sh_w2, final_g):
    B, S, D = x.shape
    depth = w_in.shape[0]
    mod_mix = adaln_rows(c, ada_mix_w, ada_mix_b)
    mod_ffn = adaln_rows(c, ada_ffn_w, ada_ffn_b)
    streams = TOKEN_STREAMS if B % TOKEN_STREAMS == 0 else 1
    Bs = B // streams
    Ns = Bs * S
    x_all = x.reshape(B * S, D)
    xs = [None] * streams
    out_all = None
    pos_s = [positions[s * Bs:(s + 1) * Bs] for s in range(streams)]
    for l in range(depth):
        last = l == depth - 1
        w_in_l = _permute_w_in(w_in[l])
        mla_w = _mla_weights(cq_norm_g[l], ckv_norm_g[l], w_uq[l], w_ukv[l])
        mix_w = (norm_ffn_g[l], pool_w[l].astype(BF16), pool_scale[l],
                 w_oa[l].astype(BF16), w_ob[l].astype(BF16), w_oc[l].astype(BF16), w_out[l].astype(BF16),
                 router_w[l].T.astype(BF16), sh_w1[l].astype(BF16), sh_w3[l].astype(BF16), sh_w2[l].astype(BF16))
        for s in range(streams):
            x2, row0 = (x_all, s * Ns) if l == 0 else (xs[s], 0)
            mod1 = mod_mix[l, s * Bs:(s + 1) * Bs].reshape(Bs, 1, 3 * D)
            mod2 = mod_ffn[l, s * Bs:(s + 1) * Bs].reshape(Bs, 1, 3 * D)
            gu, lat, *qkv = in_projection(x2, norm_mix_g[l], mod1, w_in_l, S, row0)
            dil = [dilated_attention(qkv[2 * g], qkv[2 * g + 1]) for g in range(len(DIL_GROUPS))]
            q_all, k_all, vt_all = mla_prep(lat, pos_s[s], *mla_w, Bs, S)
            yc = mla_attention(q_all, k_all, vt_all, Bs, S)
            xmid, h2a, h2b, logits_t = mix_out(x2, gu, dil, yc, mod1, mod2, *mix_w, S, row0)
            dest, w_k, counts = route(logits_t, router_bias[l])
            blk_e, nvalid = block_tables(counts, Ns)
            n_slots = blk_e.shape[0] * SLOT_BLOCK
            xa = sc_scatter_rows(h2a, dest, n_slots)
            xb = sc_scatter_rows(h2b, dest, n_slots)
            ya, yb = routed_experts(xa, xb, blk_e, nvalid, exp_w1, exp_w3, exp_w2, l)
            oa = sc_gather_rows(ya, dest)
            ob = sc_gather_rows(yb, dest)
            if last:
                out_all = combine(xmid, oa, ob, w_k.T, mod2, final_g, S, True, B * S, s * Ns, out_all)
            else:
                xs[s] = combine(xmid, oa, ob, w_k.T, mod2, final_g, S, False)
    return out_all.reshape(B, S, D)
```

```python
import functools
import math

import jax
import jax.numpy as jnp
from jax import lax
from jax.experimental import pallas as pl
from jax.experimental.pallas import tpu as pltpu
from jax.experimental.pallas import tpu_sc as plsc

F32 = jnp.float32
BF16 = jnp.bfloat16
HIGHEST = lax.Precision.HIGHEST

D_MODEL = 1024
HEAD_DIM_A = 64
HEADS_PER_GROUP_A = 4
DIL_GROUPS = ((128, 1), (512, 4), (2048, 16))
GROUP_W = HEADS_PER_GROUP_A * HEAD_DIM_A
DA = GROUP_W * len(DIL_GROUPS)
POOL_WINDOWS = (2, 4, 8, 16)
POOL_GROUP_DIM = 128
DB = POOL_GROUP_DIM * len(POOL_WINDOWS)
POOL_HALO = 16
N_HEADS_C = 8
QK_NOPE = 64
QK_ROPE = 32
V_DIM = 64
Q_LORA = 384
KV_LORA = 256
DC = N_HEADS_C * V_DIM
HEAD_PAD_C = 128
ROPE_THETA = 10000.0
N_EXPERTS = 64
TOP_K = 8
N_GROUPS = 8
TOPK_GROUPS = 4
GROUP_SIZE = N_EXPERTS // N_GROUPS
EXPERT_FF = 256
ROUTED_SCALE = 2.5
EPS = 1e-6
NEG = -1e30
Q_BLOCK = 128

LAT_W = 768
GU_W = 3 * D_MODEL + DB
IN_OUT_WIDTHS = (GU_W, LAT_W) + (2 * GROUP_W, GROUP_W) * len(DIL_GROUPS)

VMEM_LIMIT = 56 * 1024 * 1024


def _cp(sem, vmem=None):
    return pltpu.CompilerParams(dimension_semantics=sem, vmem_limit_bytes=vmem)


def _silu(v):
    return v * jax.nn.sigmoid(v)


def _nt_dot(a, b):
    return lax.dot_general(a, b, (((1,), (1,)), ((), ())), preferred_element_type=F32)


PACK_W = D_MODEL // 4
_HI_MASK = -65536


def _bf16_bits(v):
    return lax.bitcast_convert_type(v.astype(BF16).astype(F32), jnp.int32)


def _pack_row_halves(v):
    halves = []
    for h in range(2):
        lo = _bf16_bits(v[:, (2 * h) * PACK_W:(2 * h + 1) * PACK_W])
        hi = _bf16_bits(v[:, (2 * h + 1) * PACK_W:(2 * h + 2) * PACK_W])
        halves.append(lax.shift_right_logical(lo, 16) | (hi & _HI_MASK))
    return halves


def _unpack_row_halves(wa, wb):
    parts = []
    for w in (wa, wb):
        parts.append(lax.bitcast_convert_type(lax.shift_left(w, 16), F32))
        parts.append(lax.bitcast_convert_type(w & _HI_MASK, F32))
    return jnp.concatenate(parts, axis=1)


def _adaln_kernel(c_ref, w_ref, b_ref, o_ref):
    s = _silu(c_ref[...])
    o_ref[0] = jnp.dot(s, w_ref[0], preferred_element_type=F32, precision=HIGHEST) + b_ref[0]


def adaln_rows(c, w, b):
    L, D, D3 = w.shape
    B = c.shape[0]
    tn = 1024
    return pl.pallas_call(
        _adaln_kernel,
        grid=(L, D3 // tn),
        in_specs=[
            pl.BlockSpec((B, D), lambda l, j: (0, 0)),
            pl.BlockSpec((1, D, tn), lambda l, j: (l, 0, j)),
            pl.BlockSpec((1, 1, tn), lambda l, j: (l, 0, j)),
        ],
        out_specs=pl.BlockSpec((1, B, tn), lambda l, j: (l, 0, j)),
        out_shape=jax.ShapeDtypeStruct((L, B, D3), F32),
        compiler_params=_cp(("parallel", "parallel")),
        name="adaln_rows",
    )(c, w, b.reshape(L, 1, D3))


LANES = 128


def _inproj_kernel(x_ref, g_ref, mod_ref, w_ref, *refs, chunk):
    o_refs, scr = refs[:-1], refs[-1]
    D = x_ref.shape[1]
    x = x_ref[...]
    y = x * lax.rsqrt(jnp.mean(x * x, axis=-1, keepdims=True) + EPS) * g_ref[...]
    mod = mod_ref[0]
    h = (y * (1.0 + mod[:, D:2 * D]) + mod[:, :D]).astype(BF16)
    col = 0
    for o_ref in o_refs:
        width = o_ref.shape[-1]
        if o_ref.ndim == 2:
            for c0 in range(0, width, chunk):
                cw = min(chunk, width - c0)
                o_ref[:, c0:c0 + cw] = jnp.dot(
                    h, w_ref[:, col + c0:col + c0 + cw], preferred_element_type=F32).astype(o_ref.dtype)
        else:
            dil, rows = o_ref.shape[1], o_ref.shape[2]
            z = jnp.dot(h, w_ref[:, col:col + width], preferred_element_type=F32)
            if dil == 1:
                o_ref[0, 0] = z.astype(o_ref.dtype)
            else:
                for c in range(width // LANES):
                    scr[c] = z[:, c * LANES:(c + 1) * LANES]
                for r in range(dil):
                    o_ref[0, r] = jnp.concatenate(
                        [scr[c, pl.ds(r, rows, stride=dil), :] for c in range(width // LANES)],
                        axis=1).astype(o_ref.dtype)
        col += width


def in_projection(x2, g, mod, w, seq, row0=0):
    D = x2.shape[1]
    B = mod.shape[0]
    N = B * seq
    tm = 512
    tpb = seq // tm
    tile0 = row0 // tm
    out_specs = [pl.BlockSpec((tm, wd), lambda i: (i, 0)) for wd in IN_OUT_WIDTHS[:2]]
    out_shape = [jax.ShapeDtypeStruct((N, wd), BF16) for wd in IN_OUT_WIDTHS[:2]]
    for grp, (_, dil) in enumerate(DIL_GROUPS):
        for wd in IN_OUT_WIDTHS[2 + 2 * grp:4 + 2 * grp]:
            out_specs.append(pl.BlockSpec((1, dil, tm // dil, wd), lambda i: (i // tpb, 0, i % tpb, 0)))
            out_shape.append(jax.ShapeDtypeStruct((B, dil, seq // dil, wd), BF16))
    return pl.pallas_call(
        functools.partial(_inproj_kernel, chunk=512),
        grid=(N // tm,),
        in_specs=[
            pl.BlockSpec((tm, D), lambda i: (i + tile0, 0)),
            pl.BlockSpec((1, D), lambda i: (0, 0)),
            pl.BlockSpec((1, 1, 3 * D), lambda i: (i // tpb, 0, 0)),
            pl.BlockSpec(w.shape, lambda i: (0, 0), pipeline_mode=pl.Buffered(1)),
        ],
        out_specs=out_specs,
        out_shape=out_shape,
        scratch_shapes=[pltpu.VMEM((max(IN_OUT_WIDTHS[2:]) // LANES, tm, LANES), F32)],
        compiler_params=_cp(("parallel",), VMEM_LIMIT),
        name="in_projection",
    )(x2, g.reshape(1, D), mod, w)


def _dilated_kernel(q_ref, kc_ref, kp_ref, vc_ref, vp_ref, o_ref, lse_ref):
    i = pl.program_id(1)
    T = Q_BLOCK
    key = lax.broadcasted_iota(jnp.int32, (T, T), 0)
    qry = lax.broadcasted_iota(jnp.int32, (T, T), 1)
    valid_c = key <= qry
    near = key >= qry
    run = q_ref.shape[1] // T
    heads = [slice(h * HEAD_DIM_A, (h + 1) * HEAD_DIM_A) for h in range(HEADS_PER_GROUP_A)]

    def transposed(v):
        return v.astype(F32).T.astype(BF16)

    vts = [transposed(vc_ref[0, j * T:(j + 1) * T, :]) for j in range(run)]
    vt_before = transposed(vp_ref[0])

    def blocks(j):
        rows = slice(j * T, (j + 1) * T)
        if j == 0:
            return rows, kc_ref[0, rows, :], vts[0], kp_ref[0], vt_before, near & (i > 0)
        before = slice((j - 1) * T, j * T)
        return rows, kc_ref[0, rows, :], vts[j], kc_ref[0, before, :], vts[j - 1], near

    scores, probs = {}, {}
    for j in range(run):
        rows, kc, _, kp, _, valid_p = blocks(j)
        q = q_ref[0, rows, :]
        for h, sl in enumerate(heads):
            qh = q[:, sl]
            scores[j, h] = (jnp.where(valid_c, _nt_dot(kc[:, sl], qh), NEG),
                            jnp.where(valid_p, _nt_dot(kp[:, sl], qh), NEG))
    for (j, h), (sc, sp) in scores.items():
        m = jnp.maximum(jnp.max(sc, axis=0, keepdims=True), jnp.max(sp, axis=0, keepdims=True))
        pc = jnp.exp(sc - m)
        pp = jnp.exp(sp - m)
        den = jnp.sum(pc, axis=0, keepdims=True) + jnp.sum(pp, axis=0, keepdims=True)
        probs[j, h] = (pc.astype(BF16), pp.astype(BF16), den, m + jnp.log(den))
    for j in range(run):
        rows, _, vtc, _, vtp, _ = blocks(j)
        outs = []
        for h, sl in enumerate(heads):
            pc, pp, den, _ = probs[j, h]
            o = jnp.dot(vtc[sl, :], pc, preferred_element_type=F32) + jnp.dot(vtp[sl, :], pp, preferred_element_type=F32)
            outs.append(o / den)
        o_ref[0, rows, :] = jnp.concatenate(outs, axis=0).T.astype(o_ref.dtype)
        spread = LSE_LANES // len(heads)
        lse_t = jnp.concatenate([jnp.broadcast_to(probs[j, h][3], (spread, T)) for h in range(len(heads))], axis=0)
        lse_ref[0, rows, :] = lse_t.T


DILATED_RUN = 4


LSE_LANES = 128


def dilated_attention(qk, v):
    batch, dilation, L, _ = qk.shape
    nb = L // Q_BLOCK
    run = min(DILATED_RUN, nb)
    qk_r = qk.reshape(batch * dilation, L, 2 * GROUP_W)
    v_r = v.reshape(batch * dilation, L, GROUP_W)
    before = lambda i: jnp.maximum(i * run - 1, 0)
    o, lse = pl.pallas_call(
        _dilated_kernel,
        grid=(batch * dilation, nb // run),
        in_specs=[
            pl.BlockSpec((1, run * Q_BLOCK, GROUP_W), lambda s, i: (s, i, 0)),
            pl.BlockSpec((1, run * Q_BLOCK, GROUP_W), lambda s, i: (s, i, 1)),
            pl.BlockSpec((1, Q_BLOCK, GROUP_W), lambda s, i: (s, before(i), 1)),
            pl.BlockSpec((1, run * Q_BLOCK, GROUP_W), lambda s, i: (s, i, 0)),
            pl.BlockSpec((1, Q_BLOCK, GROUP_W), lambda s, i: (s, before(i), 0)),
        ],
        out_specs=[
            pl.BlockSpec((1, run * Q_BLOCK, GROUP_W), lambda s, i: (s, i, 0)),
            pl.BlockSpec((1, run * Q_BLOCK, LSE_LANES), lambda s, i: (s, i, 0)),
        ],
        out_shape=[
            jax.ShapeDtypeStruct((batch * dilation, L, GROUP_W), BF16),
            jax.ShapeDtypeStruct((batch * dilation, L, LSE_LANES), F32),
        ],
        compiler_params=_cp(("parallel", "parallel")),
        name=f"dilated_attention_d{dilation}",
    )(qk_r, qk_r, qk_r, v_r, v_r)
    return o.reshape(batch, dilation, L, GROUP_W), lse.reshape(batch, dilation, L, LSE_LANES)


def _mla_prep_kernel(lat_ref, pos_ref, gq_ref, gkv_ref, wq_ref, wk_ref, wvt_ref, freq_ref, spread_ref, one_ref,
                     q_ref, k_ref, vt_ref):
    HP = N_HEADS_C * HEAD_PAD_C
    lat = lat_ref[...].astype(F32)
    cq = lat[:, :Q_LORA]
    ckr = lat[:, Q_LORA:]
    zq = (cq * lax.rsqrt(jnp.mean(cq * cq, axis=-1, keepdims=True) + EPS) * gq_ref[...]).astype(BF16)
    lane = lax.broadcasted_iota(jnp.int32, ckr.shape, 1)
    is_kv = lane < KV_LORA
    ms = jnp.sum(jnp.where(is_kv, ckr * ckr, 0.0), axis=-1, keepdims=True) * (1.0 / KV_LORA)
    zkv = (ckr * jnp.where(is_kv, lax.rsqrt(ms + EPS) * gkv_ref[...], 1.0)).astype(BF16)
    qq = jnp.dot(zq, wq_ref[...], preferred_element_type=F32)
    kk = jnp.dot(zkv, wk_ref[...], preferred_element_type=F32)
    ang_t = freq_ref[...] * pos_ref[0].astype(F32)

    def to_lanes(t):
        hi = t.astype(BF16)
        lo = (t - hi.astype(F32)).astype(BF16)
        tn_dot = lambda a: lax.dot_general(a, spread_ref[...], (((0,), (0,)), ((), ())), preferred_element_type=F32)
        return tn_dot(hi) + tn_dot(lo)

    cos = to_lanes(jnp.cos(ang_t)) + one_ref[...]
    sin = to_lanes(jnp.sin(ang_t))
    for h in range(N_HEADS_C):
        lo, hi = h * HEAD_PAD_C, (h + 1) * HEAD_PAD_C
        q_ref[:, lo:hi] = (qq[:, lo:hi] * cos + qq[:, HP + lo:HP + hi] * sin).astype(q_ref.dtype)
        k_ref[:, lo:hi] = (kk[:, lo:hi] * cos + kk[:, HP + lo:HP + hi] * sin).astype(k_ref.dtype)
    vt_ref[0] = _nt_dot(wvt_ref[...], zkv).astype(vt_ref.dtype)


def _mla_weights(cq_g, ckv_g, w_uq, w_ukv):
    H, HPAD, half = N_HEADS_C, HEAD_PAD_C, QK_ROPE // 2
    scale = (QK_NOPE + QK_ROPE) ** -0.5 * math.log2(math.e)
    wq = w_uq.reshape(Q_LORA, H, QK_NOPE + QK_ROPE) * scale
    q_lin = jnp.pad(wq, ((0, 0), (0, 0), (0, HPAD - QK_NOPE - QK_ROPE)))
    r1, r2 = wq[..., QK_NOPE:QK_NOPE + half], wq[..., QK_NOPE + half:]
    q_sw = jnp.concatenate([jnp.zeros((Q_LORA, H, QK_NOPE), F32), -r2, r1,
                            jnp.zeros((Q_LORA, H, HPAD - QK_NOPE - QK_ROPE), F32)], axis=-1)
    wq_big = jnp.concatenate([q_lin.reshape(Q_LORA, H * HPAD), q_sw.reshape(Q_LORA, H * HPAD)], axis=1)

    rows = LAT_W - Q_LORA
    wkv = w_ukv.reshape(KV_LORA, H, QK_NOPE + V_DIM)
    eye = jnp.eye(QK_ROPE, dtype=F32)
    k_lin = jnp.zeros((rows, H, HPAD), F32)
    k_lin = k_lin.at[:KV_LORA, :, :QK_NOPE].set(wkv[..., :QK_NOPE])
    k_lin = k_lin.at[KV_LORA:KV_LORA + QK_ROPE, :, QK_NOPE:QK_NOPE + QK_ROPE].set(
        jnp.broadcast_to(eye[:, None, :], (QK_ROPE, H, QK_ROPE)))
    swap = jnp.zeros((QK_ROPE, QK_ROPE), F32).at[half:, :half].set(-jnp.eye(half)).at[:half, half:].set(jnp.eye(half))
    k_sw = jnp.zeros((rows, H, HPAD), F32)
    k_sw = k_sw.at[KV_LORA:KV_LORA + QK_ROPE, :, QK_NOPE:QK_NOPE + QK_ROPE].set(
        jnp.broadcast_to(swap[:, None, :], (QK_ROPE, H, QK_ROPE)))
    v_w = jnp.zeros((rows, H, V_DIM), F32).at[:KV_LORA].set(wkv[..., QK_NOPE:])
    wk_big = jnp.concatenate([k_lin.reshape(rows, H * HPAD), k_sw.reshape(rows, H * HPAD)], axis=1)
    wv_t = v_w.reshape(rows, H * V_DIM).T

    gkv = jnp.concatenate([ckv_g, jnp.ones((rows - KV_LORA,), F32)]).reshape(1, rows)
    return cq_g.reshape(1, Q_LORA), gkv, wq_big.astype(BF16), wk_big.astype(BF16), wv_t.astype(BF16)


def _rope_tables():
    half = QK_ROPE // 2
    freqs = (ROPE_THETA ** (-jnp.arange(0, QK_ROPE, 2, dtype=F32) / QK_ROPE)).reshape(half, 1)
    lane = jnp.arange(HEAD_PAD_C)[None, :]
    j = jnp.arange(half)[:, None]
    spread = (lane == QK_NOPE + j) | (lane == QK_NOPE + half + j)
    off_rope = ~jnp.any(spread, axis=0, keepdims=True)
    return freqs, spread.astype(BF16), off_rope.astype(F32)


def mla_prep(lat, positions, gq, gkv, wq_big, wk_big, wv_t, batch, seq):
    N = lat.shape[0]
    HP = N_HEADS_C * HEAD_PAD_C
    tm = 512
    tpb = seq // tm
    freqs, spread, off_rope = _rope_tables()
    pos_rows = positions.reshape(N // tm, 1, tm)
    const = lambda shape: pl.BlockSpec(shape, lambda i: (0, 0))
    return pl.pallas_call(
        _mla_prep_kernel,
        grid=(N // tm,),
        in_specs=[
            pl.BlockSpec((tm, LAT_W), lambda i: (i, 0)),
            pl.BlockSpec((1, 1, tm), lambda i: (i, 0, 0)),
            const(gq.shape), const(gkv.shape), const(wq_big.shape), const(wk_big.shape), const(wv_t.shape),
            const(freqs.shape), const(spread.shape), const(off_rope.shape),
        ],
        out_specs=[
            pl.BlockSpec((tm, HP), lambda i: (i, 0)),
            pl.BlockSpec((tm, HP), lambda i: (i, 0)),
            pl.BlockSpec((1, DC, tm), lambda i: (i // tpb, 0, i % tpb)),
        ],
        out_shape=[
            jax.ShapeDtypeStruct((N, HP), BF16),
            jax.ShapeDtypeStruct((N, HP), BF16),
            jax.ShapeDtypeStruct((batch, DC, seq), BF16),
        ],
        compiler_params=_cp(("parallel",), VMEM_LIMIT),
        name="mla_prep",
    )(lat, pos_rows, gq, gkv, wq_big, wk_big, wv_t, freqs, spread, off_rope)


HEADS_PER_STEP_C = 8
FLASH_Q_CHUNK = 256


def _mla_flash_kernel(qi_ref, ki_ref, q_ref, k_ref, vt_ref, o_ref, m_sc, l_sc, acc_sc):
    t = pl.program_id(2)
    qi, ki = qi_ref[t], ki_ref[t]

    @pl.when(ki == 0)
    def _():
        m_sc[...] = jnp.full(m_sc.shape, NEG, F32)
        l_sc[...] = jnp.zeros(l_sc.shape, F32)
        acc_sc[...] = jnp.zeros(acc_sc.shape, F32)

    def step(masked):
        T = q_ref.shape[1]
        if masked:
            key = lax.broadcasted_iota(jnp.int32, (T, T), 0)
            qry = lax.broadcasted_iota(jnp.int32, (T, T), 1)
            keep = key <= qry
        chains = [(h, c) for h in range(HEADS_PER_STEP_C) for c in range(T // FLASH_Q_CHUNK)]
        scores, probs, alphas = {}, {}, {}

        def keys_for(c):
            return (c + 1) * FLASH_Q_CHUNK if masked else T

        def qk(h, c):
            qs = slice(c * FLASH_Q_CHUNK, (c + 1) * FLASH_Q_CHUNK)
            q = q_ref[0, qs, h * HEAD_PAD_C:(h + 1) * HEAD_PAD_C]
            k = k_ref[0, :keys_for(c), h * HEAD_PAD_C:(h + 1) * HEAD_PAD_C]
            st = _nt_dot(k, q)
            scores[h, c] = jnp.where(keep[:keys_for(c), qs], st, NEG) if masked else st

        def softmax(h, c):
            qs = slice(c * FLASH_Q_CHUNK, (c + 1) * FLASH_Q_CHUNK)
            st = scores.pop((h, c))
            m_prev = m_sc[h, :, qs]
            m_new = jnp.maximum(m_prev, jnp.max(st, axis=0, keepdims=True))
            alpha = jnp.exp2(m_prev - m_new)
            p = jnp.exp2(st - m_new)
            l_sc[h, :, qs] = alpha * l_sc[h, :, qs] + jnp.sum(p, axis=0, keepdims=True)
            m_sc[h, :, qs] = m_new
            probs[h, c], alphas[h, c] = p.astype(BF16), alpha

        def pv(h, c):
            qs = slice(c * FLASH_Q_CHUNK, (c + 1) * FLASH_Q_CHUNK)
            vt = vt_ref[0, h * V_DIM:(h + 1) * V_DIM, :keys_for(c)]
            acc_sc[h, :, qs] = alphas.pop((h, c)) * acc_sc[h, :, qs] + jnp.dot(
                vt, probs.pop((h, c)), preferred_element_type=F32)

        for phase in (qk, softmax, pv):
            for ch in chains:
                phase(*ch)

    @pl.when(ki < qi)
    def _():
        step(False)

    @pl.when(ki == qi)
    def _():
        step(True)
        ot = jnp.concatenate([acc_sc[h] / l_sc[h] for h in range(HEADS_PER_STEP_C)], axis=0)
        o_ref[0] = ot.T.astype(o_ref.dtype)


def mla_attention(q_all, k_all, vt_all, batch, seq):
    T = 512
    nq = seq // T
    pairs = [(a, b) for a in range(nq) for b in range(a + 1)]
    qi_tab = jnp.asarray([p[0] for p in pairs], jnp.int32)
    ki_tab = jnp.asarray([p[1] for p in pairs], jnp.int32)
    hp = N_HEADS_C // HEADS_PER_STEP_C
    qw = HEADS_PER_STEP_C * HEAD_PAD_C
    vw = HEADS_PER_STEP_C * V_DIM
    q3 = q_all.reshape(batch, seq, -1)
    k3 = k_all.reshape(batch, seq, -1)
    grid_spec = pltpu.PrefetchScalarGridSpec(
        num_scalar_prefetch=2,
        grid=(batch, hp, len(pairs)),
        in_specs=[
            pl.BlockSpec((1, T, qw), lambda b, h, t, qi, ki: (b, qi[t], h)),
            pl.BlockSpec((1, T, qw), lambda b, h, t, qi, ki: (b, ki[t], h)),
            pl.BlockSpec((1, vw, T), lambda b, h, t, qi, ki: (b, h, ki[t])),
        ],
        out_specs=pl.BlockSpec((1, T, vw), lambda b, h, t, qi, ki: (b, qi[t], h)),
        scratch_shapes=[
            pltpu.VMEM((HEADS_PER_STEP_C, 1, T), F32),
            pltpu.VMEM((HEADS_PER_STEP_C, 1, T), F32),
            pltpu.VMEM((HEADS_PER_STEP_C, V_DIM, T), F32),
        ],
    )
    o = pl.pallas_call(
        _mla_flash_kernel,
        grid_spec=grid_spec,
        out_shape=jax.ShapeDtypeStruct((batch, seq, DC), BF16),
        compiler_params=_cp(("parallel", "parallel", "arbitrary")),
        name="mla_attention",
    )(qi_tab, ki_tab, q3, k3, vt_all)
    return o.reshape(batch * seq, DC)


def _mixout_kernel(x_ref, gates_ref, ub_ref, ubh_ref, o1_ref, o2_ref, o3_ref, l1_ref, l2_ref, l3_ref, yc_ref,
                   mod1_ref, mod2_ref, g2_ref, poolw_ref, pscale_ref, woa_ref, wob_ref, woc_ref, wout_ref,
                   rwt_ref, sw1_ref, sw3_ref, sw2_ref, spread_ref,
                   xmid_ref, h2a_ref, h2b_ref, logit_ref, *scratch, tiles_per_batch):
    D = x_ref.shape[1]
    tm = x_ref.shape[0]
    tile = pl.program_id(0) % tiles_per_batch
    o_scrs, l_scrs = scratch[:3], scratch[3:]

    def token_order(ref, scr):
        dil, rows, width = ref.shape[1:]
        if dil == 1:
            return ref[0, 0].astype(F32)
        for r in range(dil):
            v = ref[0, r].astype(F32)
            for c in range(width // LANES):
                scr[c, pl.ds(r, rows, stride=dil), :] = v[:, c * LANES:(c + 1) * LANES]
        return jnp.concatenate([scr[c] for c in range(width // LANES)], axis=1)

    outs = [token_order(r, s) for r, s in zip((o1_ref, o2_ref, o3_ref), o_scrs)]
    l1, l2, l3 = [token_order(r, s) for r, s in zip((l1_ref, l2_ref, l3_ref), l_scrs)]
    mx = jnp.maximum(jnp.maximum(l1, l2), l3)
    es = [jnp.exp(l1 - mx), jnp.exp(l2 - mx), jnp.exp(l3 - mx)]
    inv = 1.0 / (es[0] + es[1] + es[2])
    ya = jnp.zeros((tm, GROUP_W), F32)
    for e, o in zip(es, outs):
        w = e * inv
        w_hi = w.astype(BF16)
        w_lo = (w - w_hi.astype(F32)).astype(BF16)
        w_wide = (jnp.dot(w_hi, spread_ref[...], preferred_element_type=F32)
                  + jnp.dot(w_lo, spread_ref[...], preferred_element_type=F32))
        ya = ya + w_wide * o
    a_out = jnp.dot(ya.astype(BF16), woa_ref[...], preferred_element_type=F32)

    u = ub_ref[...].astype(F32)
    halo = jnp.where(tile > 0, ubh_ref[...].astype(F32), 0.0)
    ext = jnp.concatenate([halo, u], axis=0)
    t_seq = tile * tm + lax.broadcasted_iota(jnp.int32, (tm, 1), 0)
    pooled = []
    for gi, w in enumerate(POOL_WINDOWS):
        sl = slice(gi * POOL_GROUP_DIM, (gi + 1) * POOL_GROUP_DIM)
        acc = ext[:, sl]
        k = 1
        while k < w:
            acc = acc + pltpu.roll(acc, k, axis=0)
            k *= 2
        cnt = jnp.minimum(t_seq + 1, w).astype(F32)
        pg = acc[POOL_HALO:] / cnt - u[:, sl]
        pooled.append(jnp.dot(pg.astype(BF16), poolw_ref[gi], preferred_element_type=F32))
    yb = jnp.concatenate(pooled, axis=1) * pscale_ref[...]
    b_out = jnp.dot(yb.astype(BF16), wob_ref[...], preferred_element_type=F32)
    c_out = jnp.dot(yc_ref[...], woc_ref[...], preferred_element_type=F32)

    g = gates_ref[...].astype(F32)
    mix = (jax.nn.sigmoid(g[:, :D]) * a_out + jax.nn.sigmoid(g[:, D:2 * D]) * b_out
           + jax.nn.sigmoid(g[:, 2 * D:]) * c_out)
    tok = jnp.dot(mix.astype(BF16), wout_ref[...], preferred_element_type=F32)
    xn = x_ref[...] + mod1_ref[0][:, 2 * D:] * tok

    mod2 = mod2_ref[0]
    y = xn * lax.rsqrt(jnp.mean(xn * xn, axis=-1, keepdims=True) + EPS) * g2_ref[...]
    h2 = y * (1.0 + mod2[:, D:2 * D]) + mod2[:, :D]
    h2b = h2.astype(BF16)
    h2a_ref[...], h2b_ref[...] = _pack_row_halves(h2b)
    logit_ref[...] = _nt_dot(rwt_ref[...], h2b)
    hid = _silu(jnp.dot(h2b, sw1_ref[...], preferred_element_type=F32)) * jnp.dot(
        h2b, sw3_ref[...], preferred_element_type=F32)
    shared = jnp.dot(hid.astype(BF16), sw2_ref[...], preferred_element_type=F32)
    xmid_ref[...] = xn + mod2[:, 2 * D:] * shared


def mix_out(x2, gu, dil, yc, mod1, mod2, g2, pool_w, pool_scale, w_oa, w_ob, w_oc, w_out, rwt, sw1, sw3, sw2, seq,
            row0=0):
    D = x2.shape[1]
    N = gu.shape[0]
    tm = 512
    tpb = seq // tm
    tile0 = row0 // tm
    (o1, l1), (o2, l2), (o3, l3) = dil
    row = lambda w, c=0: pl.BlockSpec((tm, w), lambda i: (i, c))
    by_residue = lambda a: pl.BlockSpec(
        (1, a.shape[1], tm // a.shape[1], a.shape[3]), lambda i: (i // tpb, 0, i % tpb, 0))
    heads = HEADS_PER_GROUP_A
    spread = (jnp.arange(LSE_LANES)[:, None] == (jnp.arange(GROUP_W)[None, :] // HEAD_DIM_A) * (LSE_LANES // heads)
              ).astype(BF16)
    const2 = lambda a: pl.BlockSpec(a.shape, lambda i: (0,) * a.ndim, pipeline_mode=pl.Buffered(1))
    modspec = pl.BlockSpec((1, 1, 3 * D), lambda i: (i // tpb, 0, 0))
    ub_col = 3 * D // DB
    halo_spec = pl.BlockSpec(
        (POOL_HALO, DB), lambda i: (jnp.maximum(i * (tm // POOL_HALO) - 1, 0), ub_col))
    weights = [g2.reshape(1, D), pool_w, pool_scale.reshape(1, DB), w_oa, w_ob, w_oc, w_out, rwt, sw1, sw3, sw2,
               spread]
    return pl.pallas_call(
        functools.partial(_mixout_kernel, tiles_per_batch=tpb),
        grid=(N // tm,),
        in_specs=[
            pl.BlockSpec((tm, D), lambda i: (i + tile0, 0)), row(3 * D), row(DB, ub_col), halo_spec,
            by_residue(o1), by_residue(o2), by_residue(o3), by_residue(l1), by_residue(l2), by_residue(l3), row(DC),
            modspec, modspec,
        ] + [const2(a) for a in weights],
        scratch_shapes=[pltpu.VMEM((GROUP_W // LANES, tm, LANES), F32)] * 3
        + [pltpu.VMEM((LSE_LANES // LANES, tm, LANES), F32)] * 3,
        out_specs=[row(D), row(PACK_W), row(PACK_W), pl.BlockSpec((N_EXPERTS, tm), lambda i: (0, i))],
        out_shape=[
            jax.ShapeDtypeStruct((N, D), F32),
            jax.ShapeDtypeStruct((N, PACK_W), jnp.int32),
            jax.ShapeDtypeStruct((N, PACK_W), jnp.int32),
            jax.ShapeDtypeStruct((N_EXPERTS, N), F32),
        ],
        compiler_params=_cp(("parallel",), VMEM_LIMIT),
        name="mix_out",
    )(x2, gu, gu, gu, o1, o2, o3, l1, l2, l3, yc, mod1, mod2, *weights)


def _pick_rows(table, picks):
    G, GS = N_GROUPS, GROUP_SIZE
    eio = lax.broadcasted_iota(jnp.int32, (GS, table.shape[1]), 0)
    rows = []
    for k in range(TOP_K):
        idx = picks[k:k + 1]
        parts = [jnp.sum(jnp.where(eio + g * GS == idx, table[g * GS:(g + 1) * GS], 0.0), axis=0, keepdims=True)
                 for g in range(G)]
        rows.append(functools.reduce(jnp.add, parts))
    return jnp.concatenate(rows, axis=0)


def _route_choose(lg_ref, bias_ref):
    G, GS = N_GROUPS, GROUP_SIZE
    scores = jax.nn.sigmoid(lg_ref[...])
    sel = scores + bias_ref[...]
    tn = sel.shape[1]
    eio = lax.broadcasted_iota(jnp.int32, (GS, tn), 0)
    ninf = -jnp.inf

    gs = []
    for g in range(G):
        v = sel[g * GS:(g + 1) * GS]
        m1 = jnp.max(v, axis=0, keepdims=True)
        i1 = jnp.min(jnp.where(v == m1, eio, GS), axis=0, keepdims=True)
        m2 = jnp.max(jnp.where(eio == i1, ninf, v), axis=0, keepdims=True)
        gs.append(m1 + m2)
    gsm = jnp.concatenate(gs, axis=0)
    gio = lax.broadcasted_iota(jnp.int32, (G, tn), 0)
    rank = jnp.zeros((G, tn), jnp.int32)
    for g2 in range(G):
        beats = (gs[g2] > gsm) | ((gs[g2] == gsm) & (g2 < gio))
        rank = rank + beats.astype(jnp.int32)
    gsel = rank < TOPK_GROUPS

    vs = [jnp.where(gsel[g:g + 1], sel[g * GS:(g + 1) * GS], NEG) for g in range(G)]
    eid = [eio + g * GS for g in range(G)]
    chosen = [jnp.zeros((GS, tn), jnp.bool_) for _ in range(G)]
    picks = []
    for _ in range(TOP_K):
        m = functools.reduce(jnp.maximum, [jnp.max(v, axis=0, keepdims=True) for v in vs])
        idx = functools.reduce(jnp.minimum, [
            jnp.min(jnp.where(v == m, e, N_EXPERTS), axis=0, keepdims=True) for v, e in zip(vs, eid)])
        picks.append(idx)
        for g in range(G):
            hit = eid[g] == idx
            chosen[g] = chosen[g] | hit
            vs[g] = jnp.where(hit, ninf, vs[g])
    mask = jnp.concatenate(chosen, axis=0).astype(F32)
    return scores, jnp.concatenate(picks, axis=0), mask


def _route_kernel(lg_ref, bias_ref, tri_ref, dest_ref, w_ref, cnt_ref, run_sc, start_sc, mask_sc, picks_sc,
                  *, slot_block):
    phase = pl.program_id(0)
    step = pl.program_id(1)
    tn = lg_ref.shape[1]
    cols = pl.ds(pl.multiple_of(step * tn, tn), tn)

    @pl.when(phase == 0)
    def _():
        @pl.when(step == 0)
        def _():
            run_sc[...] = jnp.zeros(run_sc.shape, F32)

        scores, picks, mask = _route_choose(lg_ref, bias_ref)
        wk = _pick_rows(scores, picks)
        w_ref[0] = wk / jnp.sum(wk, axis=0, keepdims=True) * ROUTED_SCALE
        dest_ref[0] = jnp.zeros(dest_ref.shape[1:], dest_ref.dtype)
        mask_sc[:, cols] = mask.astype(BF16)
        picks_sc[:, cols] = picks
        run_sc[...] = run_sc[...] + jnp.sum(mask, axis=1, keepdims=True)

    @pl.when(phase == 1)
    def _():
        @pl.when(step == 0)
        def _():
            counts = run_sc[...].astype(jnp.int32)
            cnt_ref[...] = jnp.broadcast_to(counts, cnt_ref.shape)
            shift = slot_block.bit_length() - 1
            padded = lax.shift_left(lax.shift_right_logical(counts + (slot_block - 1), shift), shift).astype(F32)
            r = lax.broadcasted_iota(jnp.int32, (N_EXPERTS, N_EXPERTS), 0)
            c = lax.broadcasted_iota(jnp.int32, (N_EXPERTS, N_EXPERTS), 1)
            as_row = jnp.sum(jnp.where(r == c, padded, 0.0), axis=0, keepdims=True)
            start_sc[...] = jnp.sum(jnp.where(c < r, as_row, 0.0), axis=1, keepdims=True)
            run_sc[...] = jnp.zeros(run_sc.shape, F32)

        mask_b = mask_sc[:, cols]
        mask = mask_b.astype(F32)
        before = jnp.dot(mask_b, tri_ref[...], preferred_element_type=F32) - mask
        slot = start_sc[...] + run_sc[...] + before
        dest_ref[0] = _pick_rows(slot, picks_sc[:, cols]).astype(jnp.int32)
        w_ref[0] = jnp.zeros(w_ref.shape[1:], w_ref.dtype)
        run_sc[...] = run_sc[...] + jnp.sum(mask, axis=1, keepdims=True)


SLOT_BLOCK = 512


def route(logits_t, bias):
    E, N = logits_t.shape
    tn = 1024
    tri = (jnp.arange(tn)[:, None] <= jnp.arange(tn)[None, :]).astype(BF16)
    plane = lambda: pl.BlockSpec((1, TOP_K, tn), lambda p, i: (p, 0, i))
    dest, w, cnt = pl.pallas_call(
        functools.partial(_route_kernel, slot_block=SLOT_BLOCK),
        grid=(2, N // tn),
        in_specs=[
            pl.BlockSpec((E, tn), lambda p, i: (0, i * (1 - p))),
            pl.BlockSpec((E, 1), lambda p, i: (0, 0)),
            pl.BlockSpec((tn, tn), lambda p, i: (0, 0)),
        ],
        out_specs=[plane(), plane(), pl.BlockSpec((E, 128), lambda p, i: (0, 0))],
        out_shape=[
            jax.ShapeDtypeStruct((2, TOP_K, N), jnp.int32),
            jax.ShapeDtypeStruct((2, TOP_K, N), F32),
            jax.ShapeDtypeStruct((E, 128), jnp.int32),
        ],
        scratch_shapes=[pltpu.VMEM((E, 1), F32), pltpu.VMEM((E, 1), F32),
                        pltpu.VMEM((E, N), BF16), pltpu.VMEM((TOP_K, N), jnp.int32)],
        compiler_params=_cp(("arbitrary", "arbitrary")),
        name="route",
    )(logits_t, bias.reshape(E, 1), tri)
    return dest[1], w[0], cnt[:, 0]


def block_tables(counts, n_tokens):
    E = counts.shape[0]
    blk = SLOT_BLOCK
    nblk = (n_tokens * TOP_K + E * blk) // blk
    per_expert = (counts + blk - 1) // blk
    bend = jnp.cumsum(per_expert)
    bstart = bend - per_expert
    b = jnp.arange(nblk, dtype=jnp.int32)[:, None]
    owns = (bstart[None, :] <= b) & (b < bend[None, :])
    blk_e = jnp.minimum(jnp.sum(bend[None, :] <= b, axis=1), E - 1).astype(jnp.int32)
    rows_left = counts[None, :] - (b - bstart[None, :]) * blk
    nvalid = jnp.sum(jnp.where(owns, jnp.clip(rows_left, 0, blk), 0), axis=1)
    return blk_e, nvalid.astype(jnp.int32)


def _sc_mesh():
    return plsc.VectorSubcoreMesh(core_axis_name="c", subcore_axis_name="s")


SC_WINDOW = 128


def sc_scatter_rows(x, dest, n_slots):
    N, W = x.shape
    K = dest.shape[0]

    @functools.partial(pl.kernel, out_type=jax.ShapeDtypeStruct((n_slots, W), x.dtype), mesh=_sc_mesh(),
                       scratch_types=[])
    def scatter(x_hbm, i_hbm, o_hbm):
        def body(x_vmem, i_vmem):
            for k in range(K):
                pltpu.sync_copy(x_vmem, o_hbm.at[i_vmem.at[k]])

        pltpu.emit_pipeline(
            body,
            grid=(N // SC_WINDOW,),
            in_specs=[pl.BlockSpec((SC_WINDOW, W), lambda i: (i, 0)),
                      pl.BlockSpec((K, SC_WINDOW), lambda i: (0, i))],
            out_specs=[],
            core_axis_name=("c", "s"),
            dimension_semantics=(pltpu.PARALLEL,),
        )(x_hbm, i_hbm)

    return scatter(x, dest)


def sc_gather_rows(y, dest):
    W = y.shape[1]
    K, N = dest.shape

    @functools.partial(pl.kernel, out_type=jax.ShapeDtypeStruct((K, N, W), y.dtype), mesh=_sc_mesh(),
                       scratch_types=[])
    def gather(y_hbm, i_hbm, o_hbm):
        def body(i_vmem, o_vmem):
            pltpu.sync_copy(y_hbm.at[i_vmem.at[0, 0]], o_vmem.at[0])

        pltpu.emit_pipeline(
            body,
            grid=(K, N // SC_WINDOW),
            in_specs=[pl.BlockSpec((1, 1, SC_WINDOW), lambda k, i: (k, 0, i))],
            out_specs=[pl.BlockSpec((1, SC_WINDOW, W), lambda k, i: (k, i, 0))],
            core_axis_name=("c", "s"),
            dimension_semantics=(pltpu.PARALLEL, pltpu.PARALLEL),
        )(i_hbm, o_hbm)

    return gather(y, dest.reshape(K, 1, N))


def _expert_kernel(blk_e_ref, nvalid_ref, xa_ref, xb_ref, w1_ref, w3_ref, w2_ref, ya_ref, yb_ref,
                   w1_sc, w3_sc, w2_sc):
    b = pl.program_id(0)
    nv = nvalid_ref[b]
    prev_e = blk_e_ref[jnp.maximum(b - 1, 0)]

    @pl.when((b == 0) | (blk_e_ref[b] != prev_e))
    def _():
        w1_sc[...] = w1_ref[0, 0].astype(BF16)
        w3_sc[...] = w3_ref[0, 0].astype(BF16)
        w2_sc[...] = w2_ref[0, 0].astype(BF16)

    blk = xa_ref.shape[0]
    half = blk // 2

    def ffn(n):
        x = _unpack_row_halves(xa_ref[:n, :], xb_ref[:n, :])
        rows = lax.broadcasted_iota(jnp.int32, x.shape, 0)
        x = jnp.where(rows < nv, x, 0.0).astype(BF16)
        hid = _silu(jnp.dot(x, w1_sc[...], preferred_element_type=F32)) * jnp.dot(
            x, w3_sc[...], preferred_element_type=F32)
        y = jnp.dot(hid.astype(BF16), w2_sc[...], preferred_element_type=F32)
        ya_ref[:n, :], yb_ref[:n, :] = _pack_row_halves(y)
        if n < blk:
            ya_ref[n:, :] = jnp.zeros((blk - n, ya_ref.shape[1]), ya_ref.dtype)
            yb_ref[n:, :] = jnp.zeros((blk - n, yb_ref.shape[1]), yb_ref.dtype)

    @pl.when(nv > half)
    def _():
        ffn(blk)

    @pl.when((nv > 0) & (nv <= half))
    def _():
        ffn(half)

    @pl.when(nv == 0)
    def _():
        ya_ref[...] = jnp.zeros(ya_ref.shape, ya_ref.dtype)
        yb_ref[...] = jnp.zeros(yb_ref.shape, yb_ref.dtype)


def routed_experts(xa, xb, blk_e, nvalid, w1, w3, w2, layer):
    P = xa.shape[0]
    blk = SLOT_BLOCK
    _, E, D, FF = w1.shape
    slots = lambda: pl.BlockSpec((blk, PACK_W), lambda b, be, nv: (b, 0))
    grid_spec = pltpu.PrefetchScalarGridSpec(
        num_scalar_prefetch=2,
        grid=(P // blk,),
        in_specs=[
            slots(), slots(),
            pl.BlockSpec((1, 1, D, FF), lambda b, be, nv: (layer, be[b], 0, 0)),
            pl.BlockSpec((1, 1, D, FF), lambda b, be, nv: (layer, be[b], 0, 0)),
            pl.BlockSpec((1, 1, FF, D), lambda b, be, nv: (layer, be[b], 0, 0)),
        ],
        out_specs=[slots(), slots()],
        scratch_shapes=[pltpu.VMEM((D, FF), BF16), pltpu.VMEM((D, FF), BF16), pltpu.VMEM((FF, D), BF16)],
    )
    return pl.pallas_call(
        _expert_kernel,
        grid_spec=grid_spec,
        out_shape=[jax.ShapeDtypeStruct((P, PACK_W), jnp.int32)] * 2,
        compiler_params=_cp(("arbitrary",), VMEM_LIMIT),
        name="routed_experts",
    )(blk_e, nvalid, xa, xb, w1, w3, w2)


def _combine_kernel(xmid_ref, oa_ref, ob_ref, w_ref, mod2_ref, fg_ref, *rest, final):
    out_ref = rest[-1]
    D = xmid_ref.shape[1]
    w = w_ref[...]
    acc = w[:, 0:1] * _unpack_row_halves(oa_ref[0], ob_ref[0])
    for k in range(1, TOP_K):
        acc = acc + w[:, k:k + 1] * _unpack_row_halves(oa_ref[k], ob_ref[k])
    x = xmid_ref[...] + mod2_ref[0][:, 2 * D:] * acc
    if final:
        x = x * lax.rsqrt(jnp.mean(x * x, axis=-1, keepdims=True) + EPS) * fg_ref[...]
    out_ref[...] = x


def combine(xmid, oa, ob, w_tok, mod2, final_g, seq, final, out_rows=None, row0=0, out_buf=None):
    N, D = xmid.shape
    tm = 256
    tpb = seq // tm
    tile0 = row0 // tm
    rows8 = lambda: pl.BlockSpec((TOP_K, tm, PACK_W), lambda i: (0, i, 0))
    in_specs = [
        pl.BlockSpec((tm, D), lambda i: (i, 0)),
        rows8(), rows8(),
        pl.BlockSpec((tm, TOP_K), lambda i: (i, 0)),
        pl.BlockSpec((1, 1, 3 * D), lambda i: (i // tpb, 0, 0)),
        pl.BlockSpec((1, D), lambda i: (0, 0)),
    ]
    args = [xmid, oa, ob, w_tok, mod2, final_g.reshape(1, D)]
    aliases = {}
    if out_buf is not None:
        in_specs.append(pl.BlockSpec(memory_space=pl.ANY))
        args.append(out_buf)
        aliases = {len(args) - 1: 0}
    return pl.pallas_call(
        functools.partial(_combine_kernel, final=final),
        grid=(N // tm,),
        in_specs=in_specs,
        out_specs=pl.BlockSpec((tm, D), lambda i: (i + tile0, 0)),
        out_shape=jax.ShapeDtypeStruct((out_rows or N, D), F32),
        input_output_aliases=aliases,
        compiler_params=_cp(("parallel",), VMEM_LIMIT),
        name="combine",
    )(*args)


TOKEN_STREAMS = 2


def _permute_w_in(w):
    ub = w[:, 3 * DA:3 * DA + DB]
    lat_lo = 3 * DA + DB
    lat_hi = lat_lo + Q_LORA + KV_LORA + QK_ROPE
    lat, gates = w[:, lat_lo:lat_hi], w[:, lat_hi:]
    pad = jnp.zeros((w.shape[0], LAT_W - (lat_hi - lat_lo)), w.dtype)
    parts = [gates, ub, lat, pad]
    for g in range(len(DIL_GROUPS)):
        sl = slice(g * GROUP_W, (g + 1) * GROUP_W)
        parts += [w[:, :DA][:, sl] * (HEAD_DIM_A ** -0.5), w[:, DA:2 * DA][:, sl], w[:, 2 * DA:3 * DA][:, sl]]
    return jnp.concatenate(parts, axis=1).astype(BF16)


def kernel(x, c, positions, ada_mix_w, ada_mix_b, norm_mix_g, w_in, pool_w, pool_scale, cq_norm_g, ckv_norm_g, w_uq, w_ukv, w_oa, w_ob, w_oc, w_out, ada_ffn_w, ada_ffn_b, norm_ffn_g, router_w, router_bias, exp_w1, exp_w3, exp_w2, sh_w1, sh_w3, sh_w2, final_g):
    B, S, D = x.shape
    depth = w_in.shape[0]
    mod_mix = adaln_rows(c, ada_mix_w, ada_mix_b)
    mod_ffn = adaln_rows(c, ada_ffn_w, ada_ffn_b)
    streams = TOKEN_STREAMS if B % TOKEN_STREAMS == 0 else 1
    Bs = B // streams
    Ns = Bs * S
    x_all = x.reshape(B * S, D)
    xs = [None] * streams
    out_all = None
    pos_s = [positions[s * Bs:(s + 1) * Bs] for s in range(streams)]
    for l in range(depth):
        last = l == depth - 1
        w_in_l = _permute_w_in(w_in[l])
        mla_w = _mla_weights(cq_norm_g[l], ckv_norm_g[l], w_uq[l], w_ukv[l])
        mix_w = (norm_ffn_g[l], pool_w[l].astype(BF16), pool_scale[l],
                 w_oa[l].astype(BF16), w_ob[l].astype(BF16), w_oc[l].astype(BF16), w_out[l].astype(BF16),
                 router_w[l].T.astype(BF16), sh_w1[l].astype(BF16), sh_w3[l].astype(BF16), sh_w2[l].astype(BF16))
        for s in range(streams):
            x2, row0 = (x_all, s * Ns) if l == 0 else (xs[s], 0)
            mod1 = mod_mix[l, s * Bs:(s + 1) * Bs].reshape(Bs, 1, 3 * D)
            mod2 = mod_ffn[l, s * Bs:(s + 1) * Bs].reshape(Bs, 1, 3 * D)
            gu, lat, *qkv = in_projection(x2, norm_mix_g[l], mod1, w_in_l, S, row0)
            dil = [dilated_attention(qkv[2 * g], qkv[2 * g + 1]) for g in range(len(DIL_GROUPS))]
            q_all, k_all, vt_all = mla_prep(lat, pos_s[s], *mla_w, Bs, S)
            yc = mla_attention(q_all, k_all, vt_all, Bs, S)
            xmid, h2a, h2b, logits_t = mix_out(x2, gu, dil, yc, mod1, mod2, *mix_w, S, row0)
            dest, w_k, counts = route(logits_t, router_bias[l])
            blk_e, nvalid = block_tables(counts, Ns)
            n_slots = blk_e.shape[0] * SLOT_BLOCK
            xa = sc_scatter_rows(h2a, dest, n_slots)
            xb = sc_scatter_rows(h2b, dest, n_slots)
            ya, yb = routed_experts(xa, xb, blk_e, nvalid, exp_w1, exp_w3, exp_w2, l)
            oa = sc_gather_rows(ya, dest)
            ob = sc_gather_rows(yb, dest)
            if last:
                out_all = combine(xmid, oa, ob, w_k.T, mod2, final_g, S, True, B * S, s * Ns, out_all)
            else:
                xs[s] = combine(xmid, oa, ob, w_k.T, mod2, final_g, S, False)
    return out_all.reshape(B, S, D)
```

```python
import functools
import math

import jax
import jax.numpy as jnp
from jax import lax
from jax.experimental import pallas as pl
from jax.experimental.pallas import tpu as pltpu
from jax.experimental.pallas import tpu_sc as plsc

F32 = jnp.float32
BF16 = jnp.bfloat16
HIGHEST = lax.Precision.HIGHEST

D_MODEL = 1024
HEAD_DIM_A = 64
HEADS_PER_GROUP_A = 4
DIL_GROUPS = ((128, 1), (512, 4), (2048, 16))
GROUP_W = HEADS_PER_GROUP_A * HEAD_DIM_A
DA = GROUP_W * len(DIL_GROUPS)
POOL_WINDOWS = (2, 4, 8, 16)
POOL_GROUP_DIM = 128
DB = POOL_GROUP_DIM * len(POOL_WINDOWS)
POOL_HALO = 16
N_HEADS_C = 8
QK_NOPE = 64
QK_ROPE = 32
V_DIM = 64
Q_LORA = 384
KV_LORA = 256
DC = N_HEADS_C * V_DIM
HEAD_PAD_C = 128
ROPE_THETA = 10000.0
N_EXPERTS = 64
TOP_K = 8
N_GROUPS = 8
TOPK_GROUPS = 4
GROUP_SIZE = N_EXPERTS // N_GROUPS
EXPERT_FF = 256
ROUTED_SCALE = 2.5
EPS = 1e-6
NEG = -1e30
Q_BLOCK = 128

LAT_W = 768
GU_W = 3 * D_MODEL + DB
IN_OUT_WIDTHS = (GU_W, LAT_W) + (2 * GROUP_W, GROUP_W) * len(DIL_GROUPS)

VMEM_LIMIT = 56 * 1024 * 1024


def _cp(sem, vmem=None):
    return pltpu.CompilerParams(dimension_semantics=sem, vmem_limit_bytes=vmem)


def _silu(v):
    return v * jax.nn.sigmoid(v)


def _nt_dot(a, b):
    return lax.dot_general(a, b, (((1,), (1,)), ((), ())), preferred_element_type=F32)


PACK_W = D_MODEL // 4
_HI_MASK = -65536


def _bf16_bits(v):
    return lax.bitcast_convert_type(v.astype(BF16).astype(F32), jnp.int32)


def _pack_row_halves(v):
    halves = []
    for h in range(2):
        lo = _bf16_bits(v[:, (2 * h) * PACK_W:(2 * h + 1) * PACK_W])
        hi = _bf16_bits(v[:, (2 * h + 1) * PACK_W:(2 * h + 2) * PACK_W])
        halves.append(lax.shift_right_logical(lo, 16) | (hi & _HI_MASK))
    return halves


def _unpack_row_halves(wa, wb):
    parts = []
    for w in (wa, wb):
        parts.append(lax.bitcast_convert_type(lax.shift_left(w, 16), F32))
        parts.append(lax.bitcast_convert_type(w & _HI_MASK, F32))
    return jnp.concatenate(parts, axis=1)


def _adaln_kernel(c_ref, w_ref, b_ref, o_ref):
    s = _silu(c_ref[...])
    o_ref[0] = jnp.dot(s, w_ref[0], preferred_element_type=F32, precision=HIGHEST) + b_ref[0]


def adaln_rows(c, w, b):
    L, D, D3 = w.shape
    B = c.shape[0]
    tn = 1024
    return pl.pallas_call(
        _adaln_kernel,
        grid=(L, D3 // tn),
        in_specs=[
            pl.BlockSpec((B, D), lambda l, j: (0, 0)),
            pl.BlockSpec((1, D, tn), lambda l, j: (l, 0, j)),
            pl.BlockSpec((1, 1, tn), lambda l, j: (l, 0, j)),
        ],
        out_specs=pl.BlockSpec((1, B, tn), lambda l, j: (l, 0, j)),
        out_shape=jax.ShapeDtypeStruct((L, B, D3), F32),
        compiler_params=_cp(("parallel", "parallel")),
        name="adaln_rows",
    )(c, w, b.reshape(L, 1, D3))


LANES = 128


def _inproj_kernel(x_ref, g_ref, mod_ref, w_ref, *refs, chunk):
    o_refs, scr = refs[:-1], refs[-1]
    D = x_ref.shape[1]
    x = x_ref[...]
    y = x * lax.rsqrt(jnp.mean(x * x, axis=-1, keepdims=True) + EPS) * g_ref[...]
    mod = mod_ref[0]
    h = (y * (1.0 + mod[:, D:2 * D]) + mod[:, :D]).astype(BF16)
    col = 0
    for o_ref in o_refs:
        width = o_ref.shape[-1]
        if o_ref.ndim == 2:
            for c0 in range(0, width, chunk):
                cw = min(chunk, width - c0)
                o_ref[:, c0:c0 + cw] = jnp.dot(
                    h, w_ref[:, col + c0:col + c0 + cw], preferred_element_type=F32).astype(o_ref.dtype)
        else:
            dil, rows = o_ref.shape[1], o_ref.shape[2]
            z = jnp.dot(h, w_ref[:, col:col + width], preferred_element_type=F32)
            if dil == 1:
                o_ref[0, 0] = z.astype(o_ref.dtype)
            else:
                for c in range(width // LANES):
                    scr[c] = z[:, c * LANES:(c + 1) * LANES]
                for r in range(dil):
                    o_ref[0, r] = jnp.concatenate(
                        [scr[c, pl.ds(r, rows, stride=dil), :] for c in range(width // LANES)],
                        axis=1).astype(o_ref.dtype)
        col += width


def in_projection(x2, g, mod, w, seq, row0=0):
    D = x2.shape[1]
    B = mod.shape[0]
    N = B * seq
    tm = 512
    tpb = seq // tm
    tile0 = row0 // tm
    out_specs = [pl.BlockSpec((tm, wd), lambda i: (i, 0)) for wd in IN_OUT_WIDTHS[:2]]
    out_shape = [jax.ShapeDtypeStruct((N, wd), BF16) for wd in IN_OUT_WIDTHS[:2]]
    for grp, (_, dil) in enumerate(DIL_GROUPS):
        for wd in IN_OUT_WIDTHS[2 + 2 * grp:4 + 2 * grp]:
            out_specs.append(pl.BlockSpec((1, dil, tm // dil, wd), lambda i: (i // tpb, 0, i % tpb, 0)))
            out_shape.append(jax.ShapeDtypeStruct((B, dil, seq // dil, wd), BF16))
    return pl.pallas_call(
        functools.partial(_inproj_kernel, chunk=512),
        grid=(N // tm,),
        in_specs=[
            pl.BlockSpec((tm, D), lambda i: (i + tile0, 0)),
            pl.BlockSpec((1, D), lambda i: (0, 0)),
            pl.BlockSpec((1, 1, 3 * D), lambda i: (i // tpb, 0, 0)),
            pl.BlockSpec(w.shape, lambda i: (0, 0), pipeline_mode=pl.Buffered(1)),
        ],
        out_specs=out_specs,
        out_shape=out_shape,
        scratch_shapes=[pltpu.VMEM((max(IN_OUT_WIDTHS[2:]) // LANES, tm, LANES), F32)],
        compiler_params=_cp(("parallel",), VMEM_LIMIT),
        name="in_projection",
    )(x2, g.reshape(1, D), mod, w)


def _dilated_kernel(q_ref, kc_ref, kp_ref, vc_ref, vp_ref, o_ref, lse_ref):
    i = pl.program_id(1)
    T = Q_BLOCK
    key = lax.broadcasted_iota(jnp.int32, (T, T), 0)
    qry = lax.broadcasted_iota(jnp.int32, (T, T), 1)
    valid_c = key <= qry
    near = key >= qry
    seqs, run = q_ref.shape[0], q_ref.shape[1] // T
    heads = [slice(h * HEAD_DIM_A, (h + 1) * HEAD_DIM_A) for h in range(HEADS_PER_GROUP_A)]

    def transposed(v):
        return v.astype(F32).T.astype(BF16)

    vts = {(s, j): transposed(vc_ref[s, j * T:(j + 1) * T, :]) for s in range(seqs) for j in range(run)}
    vt_before = [transposed(vp_ref[s]) for s in range(seqs)]

    def blocks(s, j):
        rows = slice(j * T, (j + 1) * T)
        if j == 0:
            return rows, kc_ref[s, rows, :], vts[s, 0], kp_ref[s], vt_before[s], near & (i > 0)
        before = slice((j - 1) * T, j * T)
        return rows, kc_ref[s, rows, :], vts[s, j], kc_ref[s, before, :], vts[s, j - 1], near

    scores, probs = {}, {}
    for s in range(seqs):
        for j in range(run):
            rows, kc, _, kp, _, valid_p = blocks(s, j)
            q = q_ref[s, rows, :]
            for h, sl in enumerate(heads):
                qh = q[:, sl]
                scores[s, j, h] = (jnp.where(valid_c, _nt_dot(kc[:, sl], qh), NEG),
                                   jnp.where(valid_p, _nt_dot(kp[:, sl], qh), NEG))
    for chain, (sc, sp) in scores.items():
        m = jnp.maximum(jnp.max(sc, axis=0, keepdims=True), jnp.max(sp, axis=0, keepdims=True))
        pc = jnp.exp(sc - m)
        pp = jnp.exp(sp - m)
        den = jnp.sum(pc, axis=0, keepdims=True) + jnp.sum(pp, axis=0, keepdims=True)
        probs[chain] = (pc.astype(BF16), pp.astype(BF16), den, m + jnp.log(den))
    spread = LSE_LANES // len(heads)
    for s in range(seqs):
        for j in range(run):
            rows, _, vtc, _, vtp, _ = blocks(s, j)
            outs = []
            for h, sl in enumerate(heads):
                pc, pp, den, _ = probs[s, j, h]
                o = (jnp.dot(vtc[sl, :], pc, preferred_element_type=F32)
                     + jnp.dot(vtp[sl, :], pp, preferred_element_type=F32))
                outs.append(o / den)
            o_ref[s, rows, :] = jnp.concatenate(outs, axis=0).T.astype(o_ref.dtype)
            lse_t = jnp.concatenate(
                [jnp.broadcast_to(probs[s, j, h][3], (spread, T)) for h in range(len(heads))], axis=0)
            lse_ref[s, rows, :] = lse_t.T


DILATED_RUN = 8


LSE_LANES = 128


def dilated_attention(qk, v):
    batch, dilation, L, _ = qk.shape
    nb = L // Q_BLOCK
    run = min(DILATED_RUN, nb)
    seqs = DILATED_RUN // run
    qk_r = qk.reshape(batch * dilation, L, 2 * GROUP_W)
    v_r = v.reshape(batch * dilation, L, GROUP_W)
    before = lambda i: jnp.maximum(i * run - 1, 0)
    o, lse = pl.pallas_call(
        _dilated_kernel,
        grid=(batch * dilation // seqs, nb // run),
        in_specs=[
            pl.BlockSpec((seqs, run * Q_BLOCK, GROUP_W), lambda s, i: (s, i, 0)),
            pl.BlockSpec((seqs, run * Q_BLOCK, GROUP_W), lambda s, i: (s, i, 1)),
            pl.BlockSpec((seqs, Q_BLOCK, GROUP_W), lambda s, i: (s, before(i), 1)),
            pl.BlockSpec((seqs, run * Q_BLOCK, GROUP_W), lambda s, i: (s, i, 0)),
            pl.BlockSpec((seqs, Q_BLOCK, GROUP_W), lambda s, i: (s, before(i), 0)),
        ],
        out_specs=[
            pl.BlockSpec((seqs, run * Q_BLOCK, GROUP_W), lambda s, i: (s, i, 0)),
            pl.BlockSpec((seqs, run * Q_BLOCK, LSE_LANES), lambda s, i: (s, i, 0)),
        ],
        out_shape=[
            jax.ShapeDtypeStruct((batch * dilation, L, GROUP_W), BF16),
            jax.ShapeDtypeStruct((batch * dilation, L, LSE_LANES), F32),
        ],
        compiler_params=_cp(("parallel", "parallel")),
        name=f"dilated_attention_d{dilation}",
    )(qk_r, qk_r, qk_r, v_r, v_r)
    return o.reshape(batch, dilation, L, GROUP_W), lse.reshape(batch, dilation, L, LSE_LANES)


def _mla_prep_kernel(lat_ref, pos_ref, gq_ref, gkv_ref, wq_ref, wk_ref, wvt_ref, freq_ref, spread_ref, one_ref,
                     q_ref, k_ref, vt_ref):
    HP = N_HEADS_C * HEAD_PAD_C
    lat = lat_ref[...].astype(F32)
    cq = lat[:, :Q_LORA]
    ckr = lat[:, Q_LORA:]
    zq = (cq * lax.rsqrt(jnp.mean(cq * cq, axis=-1, keepdims=True) + EPS) * gq_ref[...]).astype(BF16)
    lane = lax.broadcasted_iota(jnp.int32, ckr.shape, 1)
    is_kv = lane < KV_LORA
    ms = jnp.sum(jnp.where(is_kv, ckr * ckr, 0.0), axis=-1, keepdims=True) * (1.0 / KV_LORA)
    zkv = (ckr * jnp.where(is_kv, lax.rsqrt(ms + EPS) * gkv_ref[...], 1.0)).astype(BF16)
    qq = jnp.dot(zq, wq_ref[...], preferred_element_type=F32)
    kk = jnp.dot(zkv, wk_ref[...], preferred_element_type=F32)
    ang_t = freq_ref[...] * pos_ref[0].astype(F32)

    def to_lanes(t):
        hi = t.astype(BF16)
        lo = (t - hi.astype(F32)).astype(BF16)
        tn_dot = lambda a: lax.dot_general(a, spread_ref[...], (((0,), (0,)), ((), ())), preferred_element_type=F32)
        return tn_dot(hi) + tn_dot(lo)

    cos = to_lanes(jnp.cos(ang_t)) + one_ref[...]
    sin = to_lanes(jnp.sin(ang_t))
    for h in range(N_HEADS_C):
        lo, hi = h * HEAD_PAD_C, (h + 1) * HEAD_PAD_C
        q_ref[:, lo:hi] = (qq[:, lo:hi] * cos + qq[:, HP + lo:HP + hi] * sin).astype(q_ref.dtype)
        k_ref[:, lo:hi] = (kk[:, lo:hi] * cos + kk[:, HP + lo:HP + hi] * sin).astype(k_ref.dtype)
    vt_ref[0] = _nt_dot(wvt_ref[...], zkv).astype(vt_ref.dtype)


def _mla_weights(cq_g, ckv_g, w_uq, w_ukv):
    H, HPAD, half = N_HEADS_C, HEAD_PAD_C, QK_ROPE // 2
    scale = (QK_NOPE + QK_ROPE) ** -0.5 * math.log2(math.e)
    wq = w_uq.reshape(Q_LORA, H, QK_NOPE + QK_ROPE) * scale
    q_lin = jnp.pad(wq, ((0, 0), (0, 0), (0, HPAD - QK_NOPE - QK_ROPE)))
    r1, r2 = wq[..., QK_NOPE:QK_NOPE + half], wq[..., QK_NOPE + half:]
    q_sw = jnp.concatenate([jnp.zeros((Q_LORA, H, QK_NOPE), F32), -r2, r1,
                            jnp.zeros((Q_LORA, H, HPAD - QK_NOPE - QK_ROPE), F32)], axis=-1)
    wq_big = jnp.concatenate([q_lin.reshape(Q_LORA, H * HPAD), q_sw.reshape(Q_LORA, H * HPAD)], axis=1)

    rows = LAT_W - Q_LORA
    wkv = w_ukv.reshape(KV_LORA, H, QK_NOPE + V_DIM)
    eye = jnp.eye(QK_ROPE, dtype=F32)
    k_lin = jnp.zeros((rows, H, HPAD), F32)
    k_lin = k_lin.at[:KV_LORA, :, :QK_NOPE].set(wkv[..., :QK_NOPE])
    k_lin = k_lin.at[KV_LORA:KV_LORA + QK_ROPE, :, QK_NOPE:QK_NOPE + QK_ROPE].set(
        jnp.broadcast_to(eye[:, None, :], (QK_ROPE, H, QK_ROPE)))
    swap = jnp.zeros((QK_ROPE, QK_ROPE), F32).at[half:, :half].set(-jnp.eye(half)).at[:half, half:].set(jnp.eye(half))
    k_sw = jnp.zeros((rows, H, HPAD), F32)
    k_sw = k_sw.at[KV_LORA:KV_LORA + QK_ROPE, :, QK_NOPE:QK_NOPE + QK_ROPE].set(
        jnp.broadcast_to(swap[:, None, :], (QK_ROPE, H, QK_ROPE)))
    v_w = jnp.zeros((rows, H, V_DIM), F32).at[:KV_LORA].set(wkv[..., QK_NOPE:])
    wk_big = jnp.concatenate([k_lin.reshape(rows, H * HPAD), k_sw.reshape(rows, H * HPAD)], axis=1)
    wv_t = v_w.reshape(rows, H * V_DIM).T

    gkv = jnp.concatenate([ckv_g, jnp.ones((rows - KV_LORA,), F32)]).reshape(1, rows)
    return cq_g.reshape(1, Q_LORA), gkv, wq_big.astype(BF16), wk_big.astype(BF16), wv_t.astype(BF16)


def _rope_tables():
    half = QK_ROPE // 2
    freqs = (ROPE_THETA ** (-jnp.arange(0, QK_ROPE, 2, dtype=F32) / QK_ROPE)).reshape(half, 1)
    lane = jnp.arange(HEAD_PAD_C)[None, :]
    j = jnp.arange(half)[:, None]
    spread = (lane == QK_NOPE + j) | (lane == QK_NOPE + half + j)
    off_rope = ~jnp.any(spread, axis=0, keepdims=True)
    return freqs, spread.astype(BF16), off_rope.astype(F32)


def mla_prep(lat, positions, gq, gkv, wq_big, wk_big, wv_t, batch, seq):
    N = lat.shape[0]
    HP = N_HEADS_C * HEAD_PAD_C
    tm = 512
    tpb = seq // tm
    freqs, spread, off_rope = _rope_tables()
    pos_rows = positions.reshape(N // tm, 1, tm)
    const = lambda shape: pl.BlockSpec(shape, lambda i: (0, 0))
    return pl.pallas_call(
        _mla_prep_kernel,
        grid=(N // tm,),
        in_specs=[
            pl.BlockSpec((tm, LAT_W), lambda i: (i, 0)),
            pl.BlockSpec((1, 1, tm), lambda i: (i, 0, 0)),
            const(gq.shape), const(gkv.shape), const(wq_big.shape), const(wk_big.shape), const(wv_t.shape),
            const(freqs.shape), const(spread.shape), const(off_rope.shape),
        ],
        out_specs=[
            pl.BlockSpec((tm, HP), lambda i: (i, 0)),
            pl.BlockSpec((tm, HP), lambda i: (i, 0)),
            pl.BlockSpec((1, DC, tm), lambda i: (i // tpb, 0, i % tpb)),
        ],
        out_shape=[
            jax.ShapeDtypeStruct((N, HP), BF16),
            jax.ShapeDtypeStruct((N, HP), BF16),
            jax.ShapeDtypeStruct((batch, DC, seq), BF16),
        ],
        compiler_params=_cp(("parallel",), VMEM_LIMIT),
        name="mla_prep",
    )(lat, pos_rows, gq, gkv, wq_big, wk_big, wv_t, freqs, spread, off_rope)


HEADS_PER_STEP_C = 8
FLASH_Q_CHUNK = 256


def _mla_flash_kernel(qi_ref, ki_ref, q_ref, k_ref, vt_ref, o_ref, m_sc, l_sc, acc_sc):
    t = pl.program_id(2)
    qi, ki = qi_ref[t], ki_ref[t]

    @pl.when(ki == 0)
    def _():
        m_sc[...] = jnp.full(m_sc.shape, NEG, F32)
        l_sc[...] = jnp.zeros(l_sc.shape, F32)
        acc_sc[...] = jnp.zeros(acc_sc.shape, F32)

    def step(masked):
        T = q_ref.shape[1]
        if masked:
            key = lax.broadcasted_iota(jnp.int32, (T, T), 0)
            qry = lax.broadcasted_iota(jnp.int32, (T, T), 1)
            keep = key <= qry
        chains = [(h, c) for h in range(HEADS_PER_STEP_C) for c in range(T // FLASH_Q_CHUNK)]
        scores, probs, alphas = {}, {}, {}

        def keys_for(c):
            return (c + 1) * FLASH_Q_CHUNK if masked else T

        def qk(h, c):
            qs = slice(c * FLASH_Q_CHUNK, (c + 1) * FLASH_Q_CHUNK)
            q = q_ref[0, qs, h * HEAD_PAD_C:(h + 1) * HEAD_PAD_C]
            k = k_ref[0, :keys_for(c), h * HEAD_PAD_C:(h + 1) * HEAD_PAD_C]
            st = _nt_dot(k, q)
            scores[h, c] = jnp.where(keep[:keys_for(c), qs], st, NEG) if masked else st

        def softmax(h, c):
            qs = slice(c * FLASH_Q_CHUNK, (c + 1) * FLASH_Q_CHUNK)
            st = scores.pop((h, c))
            m_prev = m_sc[h, :, qs]
            m_new = jnp.maximum(m_prev, jnp.max(st, axis=0, keepdims=True))
            alpha = jnp.exp2(m_prev - m_new)
            p = jnp.exp2(st - m_new)
            l_sc[h, :, qs] = alpha * l_sc[h, :, qs] + jnp.sum(p, axis=0, keepdims=True)
            m_sc[h, :, qs] = m_new
            probs[h, c], alphas[h, c] = p.astype(BF16), alpha

        def pv(h, c):
            qs = slice(c * FLASH_Q_CHUNK, (c + 1) * FLASH_Q_CHUNK)
            vt = vt_ref[0, h * V_DIM:(h + 1) * V_DIM, :keys_for(c)]
            acc_sc[h, :, qs] = alphas.pop((h, c)) * acc_sc[h, :, qs] + jnp.dot(
                vt, probs.pop((h, c)), preferred_element_type=F32)

        for phase in (qk, softmax, pv):
            for ch in chains:
                phase(*ch)

    @pl.when(ki < qi)
    def _():
        step(False)

    @pl.when(ki == qi)
    def _():
        step(True)
        ot = jnp.concatenate([acc_sc[h] / l_sc[h] for h in range(HEADS_PER_STEP_C)], axis=0)
        o_ref[0] = ot.T.astype(o_ref.dtype)


def mla_attention(q_all, k_all, vt_all, batch, seq):
    T = 512
    nq = seq // T
    pairs = [(a, b) for a in range(nq) for b in range(a + 1)]
    qi_tab = jnp.asarray([p[0] for p in pairs], jnp.int32)
    ki_tab = jnp.asarray([p[1] for p in pairs], jnp.int32)
    hp = N_HEADS_C // HEADS_PER_STEP_C
    qw = HEADS_PER_STEP_C * HEAD_PAD_C
    vw = HEADS_PER_STEP_C * V_DIM
    q3 = q_all.reshape(batch, seq, -1)
    k3 = k_all.reshape(batch, seq, -1)
    grid_spec = pltpu.PrefetchScalarGridSpec(
        num_scalar_prefetch=2,
        grid=(batch, hp, len(pairs)),
        in_specs=[
            pl.BlockSpec((1, T, qw), lambda b, h, t, qi, ki: (b, qi[t], h)),
            pl.BlockSpec((1, T, qw), lambda b, h, t, qi, ki: (b, ki[t], h)),
            pl.BlockSpec((1, vw, T), lambda b, h, t, qi, ki: (b, h, ki[t])),
        ],
        out_specs=pl.BlockSpec((1, T, vw), lambda b, h, t, qi, ki: (b, qi[t], h)),
        scratch_shapes=[
            pltpu.VMEM((HEADS_PER_STEP_C, 1, T), F32),
            pltpu.VMEM((HEADS_PER_STEP_C, 1, T), F32),
            pltpu.VMEM((HEADS_PER_STEP_C, V_DIM, T), F32),
        ],
    )
    o = pl.pallas_call(
        _mla_flash_kernel,
        grid_spec=grid_spec,
        out_shape=jax.ShapeDtypeStruct((batch, seq, DC), BF16),
        compiler_params=_cp(("parallel", "parallel", "arbitrary")),
        name="mla_attention",
    )(qi_tab, ki_tab, q3, k3, vt_all)
    return o.reshape(batch * seq, DC)


def _mixout_kernel(x_ref, gates_ref, ub_ref, ubh_ref, o1_ref, o2_ref, o3_ref, l1_ref, l2_ref, l3_ref, yc_ref,
                   mod1_ref, mod2_ref, g2_ref, poolw_ref, pscale_ref, woa_ref, wob_ref, woc_ref, wout_ref,
                   rwt_ref, sw1_ref, sw3_ref, sw2_ref, spread_ref,
                   xmid_ref, h2a_ref, h2b_ref, logit_ref, *scratch, tiles_per_batch):
    D = x_ref.shape[1]
    tm = x_ref.shape[0]
    tile = pl.program_id(0) % tiles_per_batch
    o_scrs, l_scrs = scratch[:3], scratch[3:]

    def token_order(ref, scr):
        dil, rows, width = ref.shape[1:]
        if dil == 1:
            return ref[0, 0].astype(F32)
        for r in range(dil):
            v = ref[0, r].astype(F32)
            for c in range(width // LANES):
                scr[c, pl.ds(r, rows, stride=dil), :] = v[:, c * LANES:(c + 1) * LANES]
        return jnp.concatenate([scr[c] for c in range(width // LANES)], axis=1)

    outs = [token_order(r, s) for r, s in zip((o1_ref, o2_ref, o3_ref), o_scrs)]
    l1, l2, l3 = [token_order(r, s) for r, s in zip((l1_ref, l2_ref, l3_ref), l_scrs)]
    mx = jnp.maximum(jnp.maximum(l1, l2), l3)
    es = [jnp.exp(l1 - mx), jnp.exp(l2 - mx), jnp.exp(l3 - mx)]
    inv = 1.0 / (es[0] + es[1] + es[2])
    ya = jnp.zeros((tm, GROUP_W), F32)
    for e, o in zip(es, outs):
        w = e * inv
        w_hi = w.astype(BF16)
        w_lo = (w - w_hi.astype(F32)).astype(BF16)
        w_wide = (jnp.dot(w_hi, spread_ref[...], preferred_element_type=F32)
                  + jnp.dot(w_lo, spread_ref[...], preferred_element_type=F32))
        ya = ya + w_wide * o
    a_out = jnp.dot(ya.astype(BF16), woa_ref[...], preferred_element_type=F32)

    u = ub_ref[...].astype(F32)
    halo = jnp.where(tile > 0, ubh_ref[...].astype(F32), 0.0)
    ext = jnp.concatenate([halo, u], axis=0)
    t_seq = tile * tm + lax.broadcasted_iota(jnp.int32, (tm, 1), 0)
    pooled = []
    for gi, w in enumerate(POOL_WINDOWS):
        sl = slice(gi * POOL_GROUP_DIM, (gi + 1) * POOL_GROUP_DIM)
        acc = ext[:, sl]
        k = 1
        while k < w:
            acc = acc + pltpu.roll(acc, k, axis=0)
            k *= 2
        cnt = jnp.minimum(t_seq + 1, w).astype(F32)
        pg = acc[POOL_HALO:] / cnt - u[:, sl]
        pooled.append(jnp.dot(pg.astype(BF16), poolw_ref[gi], preferred_element_type=F32))
    yb = jnp.concatenate(pooled, axis=1) * pscale_ref[...]
    b_out = jnp.dot(yb.astype(BF16), wob_ref[...], preferred_element_type=F32)
    c_out = jnp.dot(yc_ref[...], woc_ref[...], preferred_element_type=F32)

    g = gates_ref[...].astype(F32)
    mix = (jax.nn.sigmoid(g[:, :D]) * a_out + jax.nn.sigmoid(g[:, D:2 * D]) * b_out
           + jax.nn.sigmoid(g[:, 2 * D:]) * c_out)
    tok = jnp.dot(mix.astype(BF16), wout_ref[...], preferred_element_type=F32)
    xn = x_ref[...] + mod1_ref[0][:, 2 * D:] * tok

    mod2 = mod2_ref[0]
    y = xn * lax.rsqrt(jnp.mean(xn * xn, axis=-1, keepdims=True) + EPS) * g2_ref[...]
    h2 = y * (1.0 + mod2[:, D:2 * D]) + mod2[:, :D]
    h2b = h2.astype(BF16)
    h2a_ref[...], h2b_ref[...] = _pack_row_halves(h2b)
    logit_ref[...] = _nt_dot(rwt_ref[...], h2b)
    hid = _silu(jnp.dot(h2b, sw1_ref[...], preferred_element_type=F32)) * jnp.dot(
        h2b, sw3_ref[...], preferred_element_type=F32)
    shared = jnp.dot(hid.astype(BF16), sw2_ref[...], preferred_element_type=F32)
    xmid_ref[...] = xn + mod2[:, 2 * D:] * shared


def mix_out(x2, gu, dil, yc, mod1, mod2, g2, pool_w, pool_scale, w_oa, w_ob, w_oc, w_out, rwt, sw1, sw3, sw2, seq,
            row0=0):
    D = x2.shape[1]
    N = gu.shape[0]
    tm = 512
    tpb = seq // tm
    tile0 = row0 // tm
    (o1, l1), (o2, l2), (o3, l3) = dil
    row = lambda w, c=0: pl.BlockSpec((tm, w), lambda i: (i, c))
    by_residue = lambda a: pl.BlockSpec(
        (1, a.shape[1], tm // a.shape[1], a.shape[3]), lambda i: (i // tpb, 0, i % tpb, 0))
    heads = HEADS_PER_GROUP_A
    spread = (jnp.arange(LSE_LANES)[:, None] == (jnp.arange(GROUP_W)[None, :] // HEAD_DIM_A) * (LSE_LANES // heads)
              ).astype(BF16)
    const2 = lambda a: pl.BlockSpec(a.shape, lambda i: (0,) * a.ndim, pipeline_mode=pl.Buffered(1))
    modspec = pl.BlockSpec((1, 1, 3 * D), lambda i: (i // tpb, 0, 0))
    ub_col = 3 * D // DB
    halo_spec = pl.BlockSpec(
        (POOL_HALO, DB), lambda i: (jnp.maximum(i * (tm // POOL_HALO) - 1, 0), ub_col))
    weights = [g2.reshape(1, D), pool_w, pool_scale.reshape(1, DB), w_oa, w_ob, w_oc, w_out, rwt, sw1, sw3, sw2,
               spread]
    return pl.pallas_call(
        functools.partial(_mixout_kernel, tiles_per_batch=tpb),
        grid=(N // tm,),
        in_specs=[
            pl.BlockSpec((tm, D), lambda i: (i + tile0, 0)), row(3 * D), row(DB, ub_col), halo_spec,
            by_residue(o1), by_residue(o2), by_residue(o3), by_residue(l1), by_residue(l2), by_residue(l3), row(DC),
            modspec, modspec,
        ] + [const2(a) for a in weights],
        scratch_shapes=[pltpu.VMEM((GROUP_W // LANES, tm, LANES), F32)] * 3
        + [pltpu.VMEM((LSE_LANES // LANES, tm, LANES), F32)] * 3,
        out_specs=[row(D), row(PACK_W), row(PACK_W), pl.BlockSpec((N_EXPERTS, tm), lambda i: (0, i))],
        out_shape=[
            jax.ShapeDtypeStruct((N, D), F32),
            jax.ShapeDtypeStruct((N, PACK_W), jnp.int32),
            jax.ShapeDtypeStruct((N, PACK_W), jnp.int32),
            jax.ShapeDtypeStruct((N_EXPERTS, N), F32),
        ],
        compiler_params=_cp(("parallel",), VMEM_LIMIT),
        name="mix_out",
    )(x2, gu, gu, gu, o1, o2, o3, l1, l2, l3, yc, mod1, mod2, *weights)


def _pick_rows(table, picks):
    G, GS = N_GROUPS, GROUP_SIZE
    eio = lax.broadcasted_iota(jnp.int32, (GS, table.shape[1]), 0)
    rows = []
    for k in range(TOP_K):
        idx = picks[k:k + 1]
        parts = [jnp.sum(jnp.where(eio + g * GS == idx, table[g * GS:(g + 1) * GS], 0.0), axis=0, keepdims=True)
                 for g in range(G)]
        rows.append(functools.reduce(jnp.add, parts))
    return jnp.concatenate(rows, axis=0)


def _route_choose(lg_ref, bias_ref):
    G, GS = N_GROUPS, GROUP_SIZE
    scores = jax.nn.sigmoid(lg_ref[...])
    sel = scores + bias_ref[...]
    tn = sel.shape[1]
    eio = lax.broadcasted_iota(jnp.int32, (GS, tn), 0)
    ninf = -jnp.inf

    gs = []
    for g in range(G):
        v = sel[g * GS:(g + 1) * GS]
        m1 = jnp.max(v, axis=0, keepdims=True)
        i1 = jnp.min(jnp.where(v == m1, eio, GS), axis=0, keepdims=True)
        m2 = jnp.max(jnp.where(eio == i1, ninf, v), axis=0, keepdims=True)
        gs.append(m1 + m2)
    gsm = jnp.concatenate(gs, axis=0)
    gio = lax.broadcasted_iota(jnp.int32, (G, tn), 0)
    rank = jnp.zeros((G, tn), jnp.int32)
    for g2 in range(G):
        beats = (gs[g2] > gsm) | ((gs[g2] == gsm) & (g2 < gio))
        rank = rank + beats.astype(jnp.int32)
    gsel = rank < TOPK_GROUPS

    vs = [jnp.where(gsel[g:g + 1], sel[g * GS:(g + 1) * GS], NEG) for g in range(G)]
    eid = [eio + g * GS for g in range(G)]
    chosen = [jnp.zeros((GS, tn), jnp.bool_) for _ in range(G)]
    picks = []
    for _ in range(TOP_K):
        m = functools.reduce(jnp.maximum, [jnp.max(v, axis=0, keepdims=True) for v in vs])
        idx = functools.reduce(jnp.minimum, [
            jnp.min(jnp.where(v == m, e, N_EXPERTS), axis=0, keepdims=True) for v, e in zip(vs, eid)])
        picks.append(idx)
        for g in range(G):
            hit = eid[g] == idx
            chosen[g] = chosen[g] | hit
            vs[g] = jnp.where(hit, ninf, vs[g])
    mask = jnp.concatenate(chosen, axis=0).astype(F32)
    return scores, jnp.concatenate(picks, axis=0), mask


def _route_kernel(lg_ref, bias_ref, tri_ref, dest_ref, w_ref, cnt_ref, run_sc, start_sc, mask_sc, picks_sc,
                  *, slot_block):
    phase = pl.program_id(0)
    step = pl.program_id(1)
    tn = lg_ref.shape[1]
    cols = pl.ds(pl.multiple_of(step * tn, tn), tn)

    @pl.when(phase == 0)
    def _():
        @pl.when(step == 0)
        def _():
            run_sc[...] = jnp.zeros(run_sc.shape, F32)

        scores, picks, mask = _route_choose(lg_ref, bias_ref)
        wk = _pick_rows(scores, picks)
        w_ref[0] = wk / jnp.sum(wk, axis=0, keepdims=True) * ROUTED_SCALE
        dest_ref[0] = jnp.zeros(dest_ref.shape[1:], dest_ref.dtype)
        mask_sc[:, cols] = mask.astype(BF16)
        picks_sc[:, cols] = picks
        run_sc[...] = run_sc[...] + jnp.sum(mask, axis=1, keepdims=True)

    @pl.when(phase == 1)
    def _():
        @pl.when(step == 0)
        def _():
            counts = run_sc[...].astype(jnp.int32)
            cnt_ref[...] = jnp.broadcast_to(counts, cnt_ref.shape)
            shift = slot_block.bit_length() - 1
            padded = lax.shift_left(lax.shift_right_logical(counts + (slot_block - 1), shift), shift).astype(F32)
            r = lax.broadcasted_iota(jnp.int32, (N_EXPERTS, N_EXPERTS), 0)
            c = lax.broadcasted_iota(jnp.int32, (N_EXPERTS, N_EXPERTS), 1)
            as_row = jnp.sum(jnp.where(r == c, padded, 0.0), axis=0, keepdims=True)
            start_sc[...] = jnp.sum(jnp.where(c < r, as_row, 0.0), axis=1, keepdims=True)
            run_sc[...] = jnp.zeros(run_sc.shape, F32)

        mask_b = mask_sc[:, cols]
        mask = mask_b.astype(F32)
        before = jnp.dot(mask_b, tri_ref[...], preferred_element_type=F32) - mask
        slot = start_sc[...] + run_sc[...] + before
        dest_ref[0] = _pick_rows(slot, picks_sc[:, cols]).astype(jnp.int32)
        w_ref[0] = jnp.zeros(w_ref.shape[1:], w_ref.dtype)
        run_sc[...] = run_sc[...] + jnp.sum(mask, axis=1, keepdims=True)


SLOT_BLOCK = 512


def route(logits_t, bias):
    E, N = logits_t.shape
    tn = 1024
    tri = (jnp.arange(tn)[:, None] <= jnp.arange(tn)[None, :]).astype(BF16)
    plane = lambda: pl.BlockSpec((1, TOP_K, tn), lambda p, i: (p, 0, i))
    dest, w, cnt = pl.pallas_call(
        functools.partial(_route_kernel, slot_block=SLOT_BLOCK),
        grid=(2, N // tn),
        in_specs=[
            pl.BlockSpec((E, tn), lambda p, i: (0, i * (1 - p))),
            pl.BlockSpec((E, 1), lambda p, i: (0, 0)),
            pl.BlockSpec((tn, tn), lambda p, i: (0, 0)),
        ],
        out_specs=[plane(), plane(), pl.BlockSpec((E, 128), lambda p, i: (0, 0))],
        out_shape=[
            jax.ShapeDtypeStruct((2, TOP_K, N), jnp.int32),
            jax.ShapeDtypeStruct((2, TOP_K, N), F32),
            jax.ShapeDtypeStruct((E, 128), jnp.int32),
        ],
        scratch_shapes=[pltpu.VMEM((E, 1), F32), pltpu.VMEM((E, 1), F32),
                        pltpu.VMEM((E, N), BF16), pltpu.VMEM((TOP_K, N), jnp.int32)],
        compiler_params=_cp(("arbitrary", "arbitrary")),
        name="route",
    )(logits_t, bias.reshape(E, 1), tri)
    return dest[1], w[0], cnt[:, 0]


def block_tables(counts, n_tokens):
    E = counts.shape[0]
    blk = SLOT_BLOCK
    nblk = (n_tokens * TOP_K + E * blk) // blk
    per_expert = (counts + blk - 1) // blk
    bend = jnp.cumsum(per_expert)
    bstart = bend - per_expert
    b = jnp.arange(nblk, dtype=jnp.int32)[:, None]
    owns = (bstart[None, :] <= b) & (b < bend[None, :])
    blk_e = jnp.minimum(jnp.sum(bend[None, :] <= b, axis=1), E - 1).astype(jnp.int32)
    rows_left = counts[None, :] - (b - bstart[None, :]) * blk
    nvalid = jnp.sum(jnp.where(owns, jnp.clip(rows_left, 0, blk), 0), axis=1)
    return blk_e, nvalid.astype(jnp.int32)


def _sc_mesh():
    return plsc.VectorSubcoreMesh(core_axis_name="c", subcore_axis_name="s")


SC_WINDOW = 128


def sc_scatter_rows(x, dest, n_slots):
    N, W = x.shape
    K = dest.shape[0]

    @functools.partial(pl.kernel, out_type=jax.ShapeDtypeStruct((n_slots, W), x.dtype), mesh=_sc_mesh(),
                       scratch_types=[])
    def scatter(x_hbm, i_hbm, o_hbm):
        def body(x_vmem, i_vmem):
            for k in range(K):
                pltpu.sync_copy(x_vmem, o_hbm.at[i_vmem.at[k]])

        pltpu.emit_pipeline(
            body,
            grid=(N // SC_WINDOW,),
            in_specs=[pl.BlockSpec((SC_WINDOW, W), lambda i: (i, 0)),
                      pl.BlockSpec((K, SC_WINDOW), lambda i: (0, i))],
            out_specs=[],
            core_axis_name=("c", "s"),
            dimension_semantics=(pltpu.PARALLEL,),
        )(x_hbm, i_hbm)

    return scatter(x, dest)


def sc_gather_rows(y, dest):
    W = y.shape[1]
    K, N = dest.shape

    @functools.partial(pl.kernel, out_type=jax.ShapeDtypeStruct((K, N, W), y.dtype), mesh=_sc_mesh(),
                       scratch_types=[])
    def gather(y_hbm, i_hbm, o_hbm):
        def body(i_vmem, o_vmem):
            pltpu.sync_copy(y_hbm.at[i_vmem.at[0, 0]], o_vmem.at[0])

        pltpu.emit_pipeline(
            body,
            grid=(K, N // SC_WINDOW),
            in_specs=[pl.BlockSpec((1, 1, SC_WINDOW), lambda k, i: (k, 0, i))],
            out_specs=[pl.BlockSpec((1, SC_WINDOW, W), lambda k, i: (k, i, 0))],
            core_axis_name=("c", "s"),
            dimension_semantics=(pltpu.PARALLEL, pltpu.PARALLEL),
        )(i_hbm, o_hbm)

    return gather(y, dest.reshape(K, 1, N))


def _expert_kernel(blk_e_ref, nvalid_ref, xa_ref, xb_ref, w1_ref, w3_ref, w2_ref, ya_ref, yb_ref,
                   w1_sc, w3_sc, w2_sc):
    b = pl.program_id(0)
    nv = nvalid_ref[b]
    prev_e = blk_e_ref[jnp.maximum(b - 1, 0)]

    @pl.when((b == 0) | (blk_e_ref[b] != prev_e))
    def _():
        w1_sc[...] = w1_ref[0, 0].astype(BF16)
        w3_sc[...] = w3_ref[0, 0].astype(BF16)
        w2_sc[...] = w2_ref[0, 0].astype(BF16)

    @pl.when(nv > 0)
    def _():
        x = _unpack_row_halves(xa_ref[...], xb_ref[...])
        rows = lax.broadcasted_iota(jnp.int32, x.shape, 0)
        x = jnp.where(rows < nv, x, 0.0).astype(BF16)
        hid = _silu(jnp.dot(x, w1_sc[...], preferred_element_type=F32)) * jnp.dot(
            x, w3_sc[...], preferred_element_type=F32)
        y = jnp.dot(hid.astype(BF16), w2_sc[...], preferred_element_type=F32)
        ya_ref[...], yb_ref[...] = _pack_row_halves(y)

    @pl.when(nv == 0)
    def _():
        ya_ref[...] = jnp.zeros(ya_ref.shape, ya_ref.dtype)
        yb_ref[...] = jnp.zeros(yb_ref.shape, yb_ref.dtype)


def routed_experts(xa, xb, blk_e, nvalid, w1, w3, w2, layer):
    P = xa.shape[0]
    blk = SLOT_BLOCK
    _, E, D, FF = w1.shape
    slots = lambda: pl.BlockSpec((blk, PACK_W), lambda b, be, nv: (b, 0))
    grid_spec = pltpu.PrefetchScalarGridSpec(
        num_scalar_prefetch=2,
        grid=(P // blk,),
        in_specs=[
            slots(), slots(),
            pl.BlockSpec((1, 1, D, FF), lambda b, be, nv: (layer, be[b], 0, 0)),
            pl.BlockSpec((1, 1, D, FF), lambda b, be, nv: (layer, be[b], 0, 0)),
            pl.BlockSpec((1, 1, FF, D), lambda b, be, nv: (layer, be[b], 0, 0)),
        ],
        out_specs=[slots(), slots()],
        scratch_shapes=[pltpu.VMEM((D, FF), BF16), pltpu.VMEM((D, FF), BF16), pltpu.VMEM((FF, D), BF16)],
    )
    return pl.pallas_call(
        _expert_kernel,
        grid_spec=grid_spec,
        out_shape=[jax.ShapeDtypeStruct((P, PACK_W), jnp.int32)] * 2,
        compiler_params=_cp(("arbitrary",), VMEM_LIMIT),
        name="routed_experts",
    )(blk_e, nvalid, xa, xb, w1, w3, w2)


def _combine_kernel(xmid_ref, oa_ref, ob_ref, w_ref, mod2_ref, fg_ref, *rest, final):
    out_ref = rest[-1]
    D = xmid_ref.shape[1]
    w = w_ref[...]
    acc = w[:, 0:1] * _unpack_row_halves(oa_ref[0], ob_ref[0])
    for k in range(1, TOP_K):
        acc = acc + w[:, k:k + 1] * _unpack_row_halves(oa_ref[k], ob_ref[k])
    x = xmid_ref[...] + mod2_ref[0][:, 2 * D:] * acc
    if final:
        x = x * lax.rsqrt(jnp.mean(x * x, axis=-1, keepdims=True) + EPS) * fg_ref[...]
    out_ref[...] = x


def combine(xmid, oa, ob, w_tok, mod2, final_g, seq, final, out_rows=None, row0=0, out_buf=None):
    N, D = xmid.shape
    tm = 256
    tpb = seq // tm
    tile0 = row0 // tm
    rows8 = lambda: pl.BlockSpec((TOP_K, tm, PACK_W), lambda i: (0, i, 0))
    in_specs = [
        pl.BlockSpec((tm, D), lambda i: (i, 0)),
        rows8(), rows8(),
        pl.BlockSpec((tm, TOP_K), lambda i: (i, 0)),
        pl.BlockSpec((1, 1, 3 * D), lambda i: (i // tpb, 0, 0)),
        pl.BlockSpec((1, D), lambda i: (0, 0)),
    ]
    args = [xmid, oa, ob, w_tok, mod2, final_g.reshape(1, D)]
    aliases = {}
    if out_buf is not None:
        in_specs.append(pl.BlockSpec(memory_space=pl.ANY))
        args.append(out_buf)
        aliases = {len(args) - 1: 0}
    return pl.pallas_call(
        functools.partial(_combine_kernel, final=final),
        grid=(N // tm,),
        in_specs=in_specs,
        out_specs=pl.BlockSpec((tm, D), lambda i: (i + tile0, 0)),
        out_shape=jax.ShapeDtypeStruct((out_rows or N, D), F32),
        input_output_aliases=aliases,
        compiler_params=_cp(("parallel",), VMEM_LIMIT),
        name="combine",
    )(*args)


TOKEN_STREAMS = 2


def _permute_w_in(w):
    ub = w[:, 3 * DA:3 * DA + DB]
    lat_lo = 3 * DA + DB
    lat_hi = lat_lo + Q_LORA + KV_LORA + QK_ROPE
    lat, gates = w[:, lat_lo:lat_hi], w[:, lat_hi:]
    pad = jnp.zeros((w.shape[0], LAT_W - (lat_hi - lat_lo)), w.dtype)
    parts = [gates, ub, lat, pad]
    for g in range(len(DIL_GROUPS)):
        sl = slice(g * GROUP_W, (g + 1) * GROUP_W)
        parts += [w[:, :DA][:, sl] * (HEAD_DIM_A ** -0.5), w[:, DA:2 * DA][:, sl], w[:, 2 * DA:3 * DA][:, sl]]
    return jnp.concatenate(parts, axis=1).astype(BF16)


def kernel(x, c, positions, ada_mix_w, ada_mix_b, norm_mix_g, w_in, pool_w, pool_scale, cq_norm_g, ckv_norm_g, w_uq, w_ukv, w_oa, w_ob, w_oc, w_out, ada_ffn_w, ada_ffn_b, norm_ffn_g, router_w, router_bias, exp_w1, exp_w3, exp_w2, sh_w1, sh_w3, sh_w2, final_g):
    B, S, D = x.shape
    depth = w_in.shape[0]
    mod_mix = adaln_rows(c, ada_mix_w, ada_mix_b)
    mod_ffn = adaln_rows(c, ada_ffn_w, ada_ffn_b)
    streams = TOKEN_STREAMS if B % TOKEN_STREAMS == 0 else 1
    Bs = B // streams
    Ns = Bs * S
    x_all = x.reshape(B * S, D)
    xs = [None] * streams
    out_all = None
    pos_s = [positions[s * Bs:(s + 1) * Bs] for s in range(streams)]
    for l in range(depth):
        last = l == depth - 1
        w_in_l = _permute_w_in(w_in[l])
        mla_w = _mla_weights(cq_norm_g[l], ckv_norm_g[l], w_uq[l], w_ukv[l])
        mix_w = (norm_ffn_g[l], pool_w[l].astype(BF16), pool_scale[l],
                 w_oa[l].astype(BF16), w_ob[l].astype(BF16), w_oc[l].astype(BF16), w_out[l].astype(BF16),
                 router_w[l].T.astype(BF16), sh_w1[l].astype(BF16), sh_w3[l].astype(BF16), sh_w2[l].astype(BF16))
        for s in range(streams):
            x2, row0 = (x_all, s * Ns) if l == 0 else (xs[s], 0)
            mod1 = mod_mix[l, s * Bs:(s + 1) * Bs].reshape(Bs, 1, 3 * D)
            mod2 = mod_ffn[l, s * Bs:(s + 1) * Bs].reshape(Bs, 1, 3 * D)
            gu, lat, *qkv = in_projection(x2, norm_mix_g[l], mod1, w_in_l, S, row0)
            dil = [dilated_attention(qkv[2 * g], qkv[2 * g + 1]) for g in range(len(DIL_GROUPS))]
            q_all, k_all, vt_all = mla_prep(lat, pos_s[s], *mla_w, Bs, S)
            yc = mla_attention(q_all, k_all, vt_all, Bs, S)
            xmid, h2a, h2b, logits_t = mix_out(x2, gu, dil, yc, mod1, mod2, *mix_w, S, row0)
            dest, w_k, counts = route(logits_t, router_bias[l])
            blk_e, nvalid = block_tables(counts, Ns)
            n_slots = blk_e.shape[0] * SLOT_BLOCK
            xa = sc_scatter_rows(h2a, dest, n_slots)
            xb = sc_scatter_rows(h2b, dest, n_slots)
            ya, yb = routed_experts(xa, xb, blk_e, nvalid, exp_w1, exp_w3, exp_w2, l)
            oa = sc_gather_rows(ya, dest)
            ob = sc_gather_rows(yb, dest)
            if last:
                out_all = combine(xmid, oa, ob, w_k.T, mod2, final_g, S, True, B * S, s * Ns, out_all)
            else:
                xs[s] = combine(xmid, oa, ob, w_k.T, mod2, final_g, S, False)
    return out_all.reshape(B, S, D)
```

```python
import functools
import math

import jax
import jax.numpy as jnp
from jax import lax
from jax.experimental import pallas as pl
from jax.experimental.pallas import tpu as pltpu
from jax.experimental.pallas import tpu_sc as plsc

F32 = jnp.float32
BF16 = jnp.bfloat16
HIGHEST = lax.Precision.HIGHEST

D_MODEL = 1024
HEAD_DIM_A = 64
HEADS_PER_GROUP_A = 4
DIL_GROUPS = ((128, 1), (512, 4), (2048, 16))
GROUP_W = HEADS_PER_GROUP_A * HEAD_DIM_A
DA = GROUP_W * len(DIL_GROUPS)
POOL_WINDOWS = (2, 4, 8, 16)
POOL_GROUP_DIM = 128
DB = POOL_GROUP_DIM * len(POOL_WINDOWS)
POOL_HALO = 16
N_HEADS_C = 8
QK_NOPE = 64
QK_ROPE = 32
V_DIM = 64
Q_LORA = 384
KV_LORA = 256
DC = N_HEADS_C * V_DIM
HEAD_PAD_C = 128
ROPE_THETA = 10000.0
N_EXPERTS = 64
TOP_K = 8
N_GROUPS = 8
TOPK_GROUPS = 4
GROUP_SIZE = N_EXPERTS // N_GROUPS
EXPERT_FF = 256
ROUTED_SCALE = 2.5
EPS = 1e-6
NEG = -1e30
Q_BLOCK = 128

LAT_W = 768
GU_W = 3 * D_MODEL + DB
IN_OUT_WIDTHS = (GU_W, LAT_W) + (2 * GROUP_W, GROUP_W) * len(DIL_GROUPS)

VMEM_LIMIT = 56 * 1024 * 1024


def _cp(sem, vmem=None):
    return pltpu.CompilerParams(dimension_semantics=sem, vmem_limit_bytes=vmem)


def _silu(v):
    return v * jax.nn.sigmoid(v)


def _nt_dot(a, b):
    return lax.dot_general(a, b, (((1,), (1,)), ((), ())), preferred_element_type=F32)


PACK_W = D_MODEL // 4
_HI_MASK = -65536


def _bf16_bits(v):
    return lax.bitcast_convert_type(v.astype(BF16).astype(F32), jnp.int32)


def _pack_row_halves(v):
    halves = []
    for h in range(2):
        lo = _bf16_bits(v[:, (2 * h) * PACK_W:(2 * h + 1) * PACK_W])
        hi = _bf16_bits(v[:, (2 * h + 1) * PACK_W:(2 * h + 2) * PACK_W])
        halves.append(lax.shift_right_logical(lo, 16) | (hi & _HI_MASK))
    return halves


def _unpack_row_halves(wa, wb):
    parts = []
    for w in (wa, wb):
        parts.append(lax.bitcast_convert_type(lax.shift_left(w, 16), F32))
        parts.append(lax.bitcast_convert_type(w & _HI_MASK, F32))
    return jnp.concatenate(parts, axis=1)


def _adaln_kernel(c_ref, w_ref, b_ref, o_ref):
    s = _silu(c_ref[...])
    o_ref[0] = jnp.dot(s, w_ref[0], preferred_element_type=F32, precision=HIGHEST) + b_ref[0]


def adaln_rows(c, w, b):
    L, D, D3 = w.shape
    B = c.shape[0]
    tn = 1024
    return pl.pallas_call(
        _adaln_kernel,
        grid=(L, D3 // tn),
        in_specs=[
            pl.BlockSpec((B, D), lambda l, j: (0, 0)),
            pl.BlockSpec((1, D, tn), lambda l, j: (l, 0, j)),
            pl.BlockSpec((1, 1, tn), lambda l, j: (l, 0, j)),
        ],
        out_specs=pl.BlockSpec((1, B, tn), lambda l, j: (l, 0, j)),
        out_shape=jax.ShapeDtypeStruct((L, B, D3), F32),
        compiler_params=_cp(("parallel", "parallel")),
        name="adaln_rows",
    )(c, w, b.reshape(L, 1, D3))


LANES = 128


def _inproj_kernel(x_ref, g_ref, mod_ref, w_ref, *refs, chunk):
    o_refs, scr = refs[:-1], refs[-1]
    D = x_ref.shape[1]
    x = x_ref[...]
    y = x * lax.rsqrt(jnp.mean(x * x, axis=-1, keepdims=True) + EPS) * g_ref[...]
    mod = mod_ref[0]
    h = (y * (1.0 + mod[:, D:2 * D]) + mod[:, :D]).astype(BF16)
    col = 0
    for o_ref in o_refs:
        width = o_ref.shape[-1]
        if o_ref.ndim == 2:
            for c0 in range(0, width, chunk):
                cw = min(chunk, width - c0)
                o_ref[:, c0:c0 + cw] = jnp.dot(
                    h, w_ref[:, col + c0:col + c0 + cw], preferred_element_type=F32).astype(o_ref.dtype)
        else:
            dil, rows = o_ref.shape[1], o_ref.shape[2]
            z = jnp.dot(h, w_ref[:, col:col + width], preferred_element_type=F32)
            if dil == 1:
                o_ref[0, 0] = z.astype(o_ref.dtype)
            else:
                for c in range(width // LANES):
                    scr[c] = z[:, c * LANES:(c + 1) * LANES]
                for r in range(dil):
                    o_ref[0, r] = jnp.concatenate(
                        [scr[c, pl.ds(r, rows, stride=dil), :] for c in range(width // LANES)],
                        axis=1).astype(o_ref.dtype)
        col += width


def in_projection(x2, g, mod, w, seq, row0=0):
    D = x2.shape[1]
    B = mod.shape[0]
    N = B * seq
    tm = 512
    tpb = seq // tm
    tile0 = row0 // tm
    out_specs = [pl.BlockSpec((tm, wd), lambda i: (i, 0)) for wd in IN_OUT_WIDTHS[:2]]
    out_shape = [jax.ShapeDtypeStruct((N, wd), BF16) for wd in IN_OUT_WIDTHS[:2]]
    for grp, (_, dil) in enumerate(DIL_GROUPS):
        for wd in IN_OUT_WIDTHS[2 + 2 * grp:4 + 2 * grp]:
            out_specs.append(pl.BlockSpec((1, dil, tm // dil, wd), lambda i: (i // tpb, 0, i % tpb, 0)))
            out_shape.append(jax.ShapeDtypeStruct((B, dil, seq // dil, wd), BF16))
    return pl.pallas_call(
        functools.partial(_inproj_kernel, chunk=512),
        grid=(N // tm,),
        in_specs=[
            pl.BlockSpec((tm, D), lambda i: (i + tile0, 0)),
            pl.BlockSpec((1, D), lambda i: (0, 0)),
            pl.BlockSpec((1, 1, 3 * D), lambda i: (i // tpb, 0, 0)),
            pl.BlockSpec(w.shape, lambda i: (0, 0), pipeline_mode=pl.Buffered(1)),
        ],
        out_specs=out_specs,
        out_shape=out_shape,
        scratch_shapes=[pltpu.VMEM((max(IN_OUT_WIDTHS[2:]) // LANES, tm, LANES), F32)],
        compiler_params=_cp(("parallel",), VMEM_LIMIT),
        name="in_projection",
    )(x2, g.reshape(1, D), mod, w)


def _dilated_kernel(q_ref, kc_ref, kp_ref, vc_ref, vp_ref, o_ref, lse_ref):
    i = pl.program_id(1)
    T = Q_BLOCK
    key = lax.broadcasted_iota(jnp.int32, (T, T), 0)
    qry = lax.broadcasted_iota(jnp.int32, (T, T), 1)
    valid_c = key <= qry
    near = key >= qry
    seqs, run = q_ref.shape[0], q_ref.shape[1] // T
    heads = [slice(h * HEAD_DIM_A, (h + 1) * HEAD_DIM_A) for h in range(HEADS_PER_GROUP_A)]

    def transposed(v):
        return v.astype(F32).T.astype(BF16)

    vts = {(s, j): transposed(vc_ref[s, j * T:(j + 1) * T, :]) for s in range(seqs) for j in range(run)}
    vt_before = [transposed(vp_ref[s]) for s in range(seqs)]

    def blocks(s, j):
        rows = slice(j * T, (j + 1) * T)
        if j == 0:
            return rows, kc_ref[s, rows, :], vts[s, 0], kp_ref[s], vt_before[s], near & (i > 0)
        before = slice((j - 1) * T, j * T)
        return rows, kc_ref[s, rows, :], vts[s, j], kc_ref[s, before, :], vts[s, j - 1], near

    scores, probs = {}, {}
    for s in range(seqs):
        for j in range(run):
            rows, kc, _, kp, _, valid_p = blocks(s, j)
            q = q_ref[s, rows, :]
            for h, sl in enumerate(heads):
                qh = q[:, sl]
                scores[s, j, h] = (jnp.where(valid_c, _nt_dot(kc[:, sl], qh), NEG),
                                   jnp.where(valid_p, _nt_dot(kp[:, sl], qh), NEG))
    for chain, (sc, sp) in scores.items():
        m = jnp.maximum(jnp.max(sc, axis=0, keepdims=True), jnp.max(sp, axis=0, keepdims=True))
        pc = jnp.exp(sc - m)
        pp = jnp.exp(sp - m)
        den = jnp.sum(pc, axis=0, keepdims=True) + jnp.sum(pp, axis=0, keepdims=True)
        probs[chain] = (pc.astype(BF16), pp.astype(BF16), den, m + jnp.log(den))
    spread = LSE_LANES // len(heads)
    for s in range(seqs):
        for j in range(run):
            rows, _, vtc, _, vtp, _ = blocks(s, j)
            outs = []
            for h, sl in enumerate(heads):
                pc, pp, den, _ = probs[s, j, h]
                o = (jnp.dot(vtc[sl, :], pc, preferred_element_type=F32)
                     + jnp.dot(vtp[sl, :], pp, preferred_element_type=F32))
                outs.append(o / den)
            o_ref[s, rows, :] = jnp.concatenate(outs, axis=0).T.astype(o_ref.dtype)
            lse_t = jnp.concatenate(
                [jnp.broadcast_to(probs[s, j, h][3], (spread, T)) for h in range(len(heads))], axis=0)
            lse_ref[s, rows, :] = lse_t.T


DILATED_RUN = 8


LSE_LANES = 128


def dilated_attention(qk, v):
    batch, dilation, L, _ = qk.shape
    nb = L // Q_BLOCK
    run = min(DILATED_RUN, nb)
    seqs = DILATED_RUN // run
    qk_r = qk.reshape(batch * dilation, L, 2 * GROUP_W)
    v_r = v.reshape(batch * dilation, L, GROUP_W)
    before = lambda i: jnp.maximum(i * run - 1, 0)
    o, lse = pl.pallas_call(
        _dilated_kernel,
        grid=(batch * dilation // seqs, nb // run),
        in_specs=[
            pl.BlockSpec((seqs, run * Q_BLOCK, GROUP_W), lambda s, i: (s, i, 0)),
            pl.BlockSpec((seqs, run * Q_BLOCK, GROUP_W), lambda s, i: (s, i, 1)),
            pl.BlockSpec((seqs, Q_BLOCK, GROUP_W), lambda s, i: (s, before(i), 1)),
            pl.BlockSpec((seqs, run * Q_BLOCK, GROUP_W), lambda s, i: (s, i, 0)),
            pl.BlockSpec((seqs, Q_BLOCK, GROUP_W), lambda s, i: (s, before(i), 0)),
        ],
        out_specs=[
            pl.BlockSpec((seqs, run * Q_BLOCK, GROUP_W), lambda s, i: (s, i, 0)),
            pl.BlockSpec((seqs, run * Q_BLOCK, LSE_LANES), lambda s, i: (s, i, 0)),
        ],
        out_shape=[
            jax.ShapeDtypeStruct((batch * dilation, L, GROUP_W), BF16),
            jax.ShapeDtypeStruct((batch * dilation, L, LSE_LANES), F32),
        ],
        compiler_params=_cp(("parallel", "parallel")),
        name=f"dilated_attention_d{dilation}",
    )(qk_r, qk_r, qk_r, v_r, v_r)
    return o.reshape(batch, dilation, L, GROUP_W), lse.reshape(batch, dilation, L, LSE_LANES)


def _mla_prep_kernel(lat_ref, pos_ref, gq_ref, gkv_ref, wq_ref, wk_ref, wvt_ref, freq_ref, spread_ref, one_ref,
                     q_ref, k_ref, vt_ref):
    HP = N_HEADS_C * HEAD_PAD_C
    lat = lat_ref[...].astype(F32)
    cq = lat[:, :Q_LORA]
    ckr = lat[:, Q_LORA:]
    zq = (cq * lax.rsqrt(jnp.mean(cq * cq, axis=-1, keepdims=True) + EPS) * gq_ref[...]).astype(BF16)
    lane = lax.broadcasted_iota(jnp.int32, ckr.shape, 1)
    is_kv = lane < KV_LORA
    ms = jnp.sum(jnp.where(is_kv, ckr * ckr, 0.0), axis=-1, keepdims=True) * (1.0 / KV_LORA)
    zkv = (ckr * jnp.where(is_kv, lax.rsqrt(ms + EPS) * gkv_ref[...], 1.0)).astype(BF16)
    qq = jnp.dot(zq, wq_ref[...], preferred_element_type=F32)
    kk = jnp.dot(zkv, wk_ref[...], preferred_element_type=F32)
    ang_t = freq_ref[...] * pos_ref[0].astype(F32)

    def to_lanes(t):
        hi = t.astype(BF16)
        lo = (t - hi.astype(F32)).astype(BF16)
        tn_dot = lambda a: lax.dot_general(a, spread_ref[...], (((0,), (0,)), ((), ())), preferred_element_type=F32)
        return tn_dot(hi) + tn_dot(lo)

    cos = to_lanes(jnp.cos(ang_t)) + one_ref[...]
    sin = to_lanes(jnp.sin(ang_t))
    for h in range(N_HEADS_C):
        lo, hi = h * HEAD_PAD_C, (h + 1) * HEAD_PAD_C
        q_ref[:, lo:hi] = (qq[:, lo:hi] * cos + qq[:, HP + lo:HP + hi] * sin).astype(q_ref.dtype)
        k_ref[:, lo:hi] = (kk[:, lo:hi] * cos + kk[:, HP + lo:HP + hi] * sin).astype(k_ref.dtype)
    vt_ref[0] = _nt_dot(wvt_ref[...], zkv).astype(vt_ref.dtype)


def _mla_weights(cq_g, ckv_g, w_uq, w_ukv):
    H, HPAD, half = N_HEADS_C, HEAD_PAD_C, QK_ROPE // 2
    scale = (QK_NOPE + QK_ROPE) ** -0.5 * math.log2(math.e)
    wq = w_uq.reshape(Q_LORA, H, QK_NOPE + QK_ROPE) * scale
    q_lin = jnp.pad(wq, ((0, 0), (0, 0), (0, HPAD - QK_NOPE - QK_ROPE)))
    r1, r2 = wq[..., QK_NOPE:QK_NOPE + half], wq[..., QK_NOPE + half:]
    q_sw = jnp.concatenate([jnp.zeros((Q_LORA, H, QK_NOPE), F32), -r2, r1,
                            jnp.zeros((Q_LORA, H, HPAD - QK_NOPE - QK_ROPE), F32)], axis=-1)
    wq_big = jnp.concatenate([q_lin.reshape(Q_LORA, H * HPAD), q_sw.reshape(Q_LORA, H * HPAD)], axis=1)

    rows = LAT_W - Q_LORA
    wkv = w_ukv.reshape(KV_LORA, H, QK_NOPE + V_DIM)
    eye = jnp.eye(QK_ROPE, dtype=F32)
    k_lin = jnp.zeros((rows, H, HPAD), F32)
    k_lin = k_lin.at[:KV_LORA, :, :QK_NOPE].set(wkv[..., :QK_NOPE])
    k_lin = k_lin.at[KV_LORA:KV_LORA + QK_ROPE, :, QK_NOPE:QK_NOPE + QK_ROPE].set(
        jnp.broadcast_to(eye[:, None, :], (QK_ROPE, H, QK_ROPE)))
    swap = jnp.zeros((QK_ROPE, QK_ROPE), F32).at[half:, :half].set(-jnp.eye(half)).at[:half, half:].set(jnp.eye(half))
    k_sw = jnp.zeros((rows, H, HPAD), F32)
    k_sw = k_sw.at[KV_LORA:KV_LORA + QK_ROPE, :, QK_NOPE:QK_NOPE + QK_ROPE].set(
        jnp.broadcast_to(swap[:, None, :], (QK_ROPE, H, QK_ROPE)))
    v_w = jnp.zeros((rows, H, V_DIM), F32).at[:KV_LORA].set(wkv[..., QK_NOPE:])
    wk_big = jnp.concatenate([k_lin.reshape(rows, H * HPAD), k_sw.reshape(rows, H * HPAD)], axis=1)
    wv_t = v_w.reshape(rows, H * V_DIM).T

    gkv = jnp.concatenate([ckv_g, jnp.ones((rows - KV_LORA,), F32)]).reshape(1, rows)
    return cq_g.reshape(1, Q_LORA), gkv, wq_big.astype(BF16), wk_big.astype(BF16), wv_t.astype(BF16)


def _rope_tables():
    half = QK_ROPE // 2
    freqs = (ROPE_THETA ** (-jnp.arange(0, QK_ROPE, 2, dtype=F32) / QK_ROPE)).reshape(half, 1)
    lane = jnp.arange(HEAD_PAD_C)[None, :]
    j = jnp.arange(half)[:, None]
    spread = (lane == QK_NOPE + j) | (lane == QK_NOPE + half + j)
    off_rope = ~jnp.any(spread, axis=0, keepdims=True)
    return freqs, spread.astype(BF16), off_rope.astype(F32)


def mla_prep(lat, positions, gq, gkv, wq_big, wk_big, wv_t, batch, seq):
    N = lat.shape[0]
    HP = N_HEADS_C * HEAD_PAD_C
    tm = 512
    tpb = seq // tm
    freqs, spread, off_rope = _rope_tables()
    pos_rows = positions.reshape(N // tm, 1, tm)
    const = lambda shape: pl.BlockSpec(shape, lambda i: (0, 0))
    return pl.pallas_call(
        _mla_prep_kernel,
        grid=(N // tm,),
        in_specs=[
            pl.BlockSpec((tm, LAT_W), lambda i: (i, 0)),
            pl.BlockSpec((1, 1, tm), lambda i: (i, 0, 0)),
            const(gq.shape), const(gkv.shape), const(wq_big.shape), const(wk_big.shape), const(wv_t.shape),
            const(freqs.shape), const(spread.shape), const(off_rope.shape),
        ],
        out_specs=[
            pl.BlockSpec((tm, HP), lambda i: (i, 0)),
            pl.BlockSpec((tm, HP), lambda i: (i, 0)),
            pl.BlockSpec((1, DC, tm), lambda i: (i // tpb, 0, i % tpb)),
        ],
        out_shape=[
            jax.ShapeDtypeStruct((N, HP), BF16),
            jax.ShapeDtypeStruct((N, HP), BF16),
            jax.ShapeDtypeStruct((batch, DC, seq), BF16),
        ],
        compiler_params=_cp(("parallel",), VMEM_LIMIT),
        name="mla_prep",
    )(lat, pos_rows, gq, gkv, wq_big, wk_big, wv_t, freqs, spread, off_rope)


HEADS_PER_STEP_C = 8
FLASH_Q_CHUNK = 256


def _mla_flash_kernel(qi_ref, ki_ref, q_ref, k_ref, vt_ref, o_ref, m_sc, l_sc, acc_sc):
    t = pl.program_id(2)
    qi, ki = qi_ref[t], ki_ref[t]

    @pl.when(ki == 0)
    def _():
        m_sc[...] = jnp.full(m_sc.shape, NEG, F32)
        l_sc[...] = jnp.zeros(l_sc.shape, F32)
        acc_sc[...] = jnp.zeros(acc_sc.shape, F32)

    def step(masked):
        T = q_ref.shape[1]
        if masked:
            key = lax.broadcasted_iota(jnp.int32, (T, T), 0)
            qry = lax.broadcasted_iota(jnp.int32, (T, T), 1)
            keep = key <= qry
        chains = [(h, c) for h in range(HEADS_PER_STEP_C) for c in range(T // FLASH_Q_CHUNK)]
        scores, probs, alphas = {}, {}, {}

        def keys_for(c):
            return (c + 1) * FLASH_Q_CHUNK if masked else T

        def qk(h, c):
            qs = slice(c * FLASH_Q_CHUNK, (c + 1) * FLASH_Q_CHUNK)
            q = q_ref[0, qs, h * HEAD_PAD_C:(h + 1) * HEAD_PAD_C]
            k = k_ref[0, :keys_for(c), h * HEAD_PAD_C:(h + 1) * HEAD_PAD_C]
            st = _nt_dot(k, q)
            scores[h, c] = jnp.where(keep[:keys_for(c), qs], st, NEG) if masked else st

        def softmax(h, c):
            qs = slice(c * FLASH_Q_CHUNK, (c + 1) * FLASH_Q_CHUNK)
            st = scores.pop((h, c))
            m_prev = m_sc[h, :, qs]
            m_new = jnp.maximum(m_prev, jnp.max(st, axis=0, keepdims=True))
            alpha = jnp.exp2(m_prev - m_new)
            p = jnp.exp2(st - m_new)
            l_sc[h, :, qs] = alpha * l_sc[h, :, qs] + jnp.sum(p, axis=0, keepdims=True)
            m_sc[h, :, qs] = m_new
            probs[h, c], alphas[h, c] = p.astype(BF16), alpha

        def pv(h, c):
            qs = slice(c * FLASH_Q_CHUNK, (c + 1) * FLASH_Q_CHUNK)
            vt = vt_ref[0, h * V_DIM:(h + 1) * V_DIM, :keys_for(c)]
            acc_sc[h, :, qs] = alphas.pop((h, c)) * acc_sc[h, :, qs] + jnp.dot(
                vt, probs.pop((h, c)), preferred_element_type=F32)

        for phase in (qk, softmax, pv):
            for ch in chains:
                phase(*ch)

    @pl.when(ki < qi)
    def _():
        step(False)

    @pl.when(ki == qi)
    def _():
        step(True)
        ot = jnp.concatenate([acc_sc[h] / l_sc[h] for h in range(HEADS_PER_STEP_C)], axis=0)
        o_ref[0] = ot.T.astype(o_ref.dtype)


def mla_attention(q_all, k_all, vt_all, batch, seq):
    T = 512
    nq = seq // T
    pairs = [(a, b) for a in range(nq) for b in range(a + 1)]
    qi_tab = jnp.asarray([p[0] for p in pairs], jnp.int32)
    ki_tab = jnp.asarray([p[1] for p in pairs], jnp.int32)
    hp = N_HEADS_C // HEADS_PER_STEP_C
    qw = HEADS_PER_STEP_C * HEAD_PAD_C
    vw = HEADS_PER_STEP_C * V_DIM
    q3 = q_all.reshape(batch, seq, -1)
    k3 = k_all.reshape(batch, seq, -1)
    grid_spec = pltpu.PrefetchScalarGridSpec(
        num_scalar_prefetch=2,
        grid=(batch, hp, len(pairs)),
        in_specs=[
            pl.BlockSpec((1, T, qw), lambda b, h, t, qi, ki: (b, qi[t], h)),
            pl.BlockSpec((1, T, qw), lambda b, h, t, qi, ki: (b, ki[t], h)),
            pl.BlockSpec((1, vw, T), lambda b, h, t, qi, ki: (b, h, ki[t])),
        ],
        out_specs=pl.BlockSpec((1, T, vw), lambda b, h, t, qi, ki: (b, qi[t], h)),
        scratch_shapes=[
            pltpu.VMEM((HEADS_PER_STEP_C, 1, T), F32),
            pltpu.VMEM((HEADS_PER_STEP_C, 1, T), F32),
            pltpu.VMEM((HEADS_PER_STEP_C, V_DIM, T), F32),
        ],
    )
    o = pl.pallas_call(
        _mla_flash_kernel,
        grid_spec=grid_spec,
        out_shape=jax.ShapeDtypeStruct((batch, seq, DC), BF16),
        compiler_params=_cp(("parallel", "parallel", "arbitrary")),
        name="mla_attention",
    )(qi_tab, ki_tab, q3, k3, vt_all)
    return o.reshape(batch * seq, DC)


def _mixout_kernel(x_ref, gates_ref, ub_ref, ubh_ref, o1_ref, o2_ref, o3_ref, l1_ref, l2_ref, l3_ref, yc_ref,
                   mod1_ref, mod2_ref, g2_ref, poolw_ref, pscale_ref, woa_ref, wob_ref, woc_ref, wout_ref,
                   rwt_ref, sw1_ref, sw3_ref, sw2_ref, spread_ref,
                   xmid_ref, h2a_ref, h2b_ref, logit_ref, *scratch, tiles_per_batch):
    D = x_ref.shape[1]
    tm = x_ref.shape[0]
    tile = pl.program_id(0) % tiles_per_batch
    o_scrs, l_scrs = scratch[:3], scratch[3:]

    def token_order(ref, scr):
        dil, rows, width = ref.shape[1:]
        if dil == 1:
            return ref[0, 0].astype(F32)
        for r in range(dil):
            v = ref[0, r].astype(F32)
            for c in range(width // LANES):
                scr[c, pl.ds(r, rows, stride=dil), :] = v[:, c * LANES:(c + 1) * LANES]
        return jnp.concatenate([scr[c] for c in range(width // LANES)], axis=1)

    outs = [token_order(r, s) for r, s in zip((o1_ref, o2_ref, o3_ref), o_scrs)]
    l1, l2, l3 = [token_order(r, s) for r, s in zip((l1_ref, l2_ref, l3_ref), l_scrs)]
    mx = jnp.maximum(jnp.maximum(l1, l2), l3)
    es = [jnp.exp(l1 - mx), jnp.exp(l2 - mx), jnp.exp(l3 - mx)]
    inv = 1.0 / (es[0] + es[1] + es[2])
    ya = jnp.zeros((tm, GROUP_W), F32)
    for e, o in zip(es, outs):
        w = e * inv
        w_hi = w.astype(BF16)
        w_lo = (w - w_hi.astype(F32)).astype(BF16)
        w_wide = (jnp.dot(w_hi, spread_ref[...], preferred_element_type=F32)
                  + jnp.dot(w_lo, spread_ref[...], preferred_element_type=F32))
        ya = ya + w_wide * o
    a_out = jnp.dot(ya.astype(BF16), woa_ref[...], preferred_element_type=F32)

    u = ub_ref[...].astype(F32)
    halo = jnp.where(tile > 0, ubh_ref[...].astype(F32), 0.0)
    ext = jnp.concatenate([halo, u], axis=0)
    t_seq = tile * tm + lax.broadcasted_iota(jnp.int32, (tm, 1), 0)
    pooled = []
    for gi, w in enumerate(POOL_WINDOWS):
        sl = slice(gi * POOL_GROUP_DIM, (gi + 1) * POOL_GROUP_DIM)
        acc = ext[:, sl]
        k = 1
        while k < w:
            acc = acc + pltpu.roll(acc, k, axis=0)
            k *= 2
        cnt = jnp.minimum(t_seq + 1, w).astype(F32)
        pg = acc[POOL_HALO:] / cnt - u[:, sl]
        pooled.append(jnp.dot(pg.astype(BF16), poolw_ref[gi], preferred_element_type=F32))
    yb = jnp.concatenate(pooled, axis=1) * pscale_ref[...]
    b_out = jnp.dot(yb.astype(BF16), wob_ref[...], preferred_element_type=F32)
    c_out = jnp.dot(yc_ref[...], woc_ref[...], preferred_element_type=F32)

    g = gates_ref[...].astype(F32)
    mix = (jax.nn.sigmoid(g[:, :D]) * a_out + jax.nn.sigmoid(g[:, D:2 * D]) * b_out
           + jax.nn.sigmoid(g[:, 2 * D:]) * c_out)
    tok = jnp.dot(mix.astype(BF16), wout_ref[...], preferred_element_type=F32)
    xn = x_ref[...] + mod1_ref[0][:, 2 * D:] * tok

    mod2 = mod2_ref[0]
    y = xn * lax.rsqrt(jnp.mean(xn * xn, axis=-1, keepdims=True) + EPS) * g2_ref[...]
    h2 = y * (1.0 + mod2[:, D:2 * D]) + mod2[:, :D]
    h2b = h2.astype(BF16)
    h2a_ref[...], h2b_ref[...] = _pack_row_halves(h2b)
    logit_ref[...] = _nt_dot(rwt_ref[...], h2b)
    hid = _silu(jnp.dot(h2b, sw1_ref[...], preferred_element_type=F32)) * jnp.dot(
        h2b, sw3_ref[...], preferred_element_type=F32)
    shared = jnp.dot(hid.astype(BF16), sw2_ref[...], preferred_element_type=F32)
    xmid_ref[...] = xn + mod2[:, 2 * D:] * shared


def mix_out(x2, gu, dil, yc, mod1, mod2, g2, pool_w, pool_scale, w_oa, w_ob, w_oc, w_out, rwt, sw1, sw3, sw2, seq,
            row0=0):
    D = x2.shape[1]
    N = gu.shape[0]
    tm = 512
    tpb = seq // tm
    tile0 = row0 // tm
    (o1, l1), (o2, l2), (o3, l3) = dil
    row = lambda w, c=0: pl.BlockSpec((tm, w), lambda i: (i, c))
    by_residue = lambda a: pl.BlockSpec(
        (1, a.shape[1], tm // a.shape[1], a.shape[3]), lambda i: (i // tpb, 0, i % tpb, 0))
    heads = HEADS_PER_GROUP_A
    spread = (jnp.arange(LSE_LANES)[:, None] == (jnp.arange(GROUP_W)[None, :] // HEAD_DIM_A) * (LSE_LANES // heads)
              ).astype(BF16)
    const2 = lambda a: pl.BlockSpec(a.shape, lambda i: (0,) * a.ndim, pipeline_mode=pl.Buffered(1))
    modspec = pl.BlockSpec((1, 1, 3 * D), lambda i: (i // tpb, 0, 0))
    ub_col = 3 * D // DB
    halo_spec = pl.BlockSpec(
        (POOL_HALO, DB), lambda i: (jnp.maximum(i * (tm // POOL_HALO) - 1, 0), ub_col))
    weights = [g2.reshape(1, D), pool_w, pool_scale.reshape(1, DB), w_oa, w_ob, w_oc, w_out, rwt, sw1, sw3, sw2,
               spread]
    return pl.pallas_call(
        functools.partial(_mixout_kernel, tiles_per_batch=tpb),
        grid=(N // tm,),
        in_specs=[
            pl.BlockSpec((tm, D), lambda i: (i + tile0, 0)), row(3 * D), row(DB, ub_col), halo_spec,
            by_residue(o1), by_residue(o2), by_residue(o3), by_residue(l1), by_residue(l2), by_residue(l3), row(DC),
            modspec, modspec,
        ] + [const2(a) for a in weights],
        scratch_shapes=[pltpu.VMEM((GROUP_W // LANES, tm, LANES), F32)] * 3
        + [pltpu.VMEM((LSE_LANES // LANES, tm, LANES), F32)] * 3,
        out_specs=[row(D), row(PACK_W), row(PACK_W), pl.BlockSpec((N_EXPERTS, tm), lambda i: (0, i))],
        out_shape=[
            jax.ShapeDtypeStruct((N, D), F32),
            jax.ShapeDtypeStruct((N, PACK_W), jnp.int32),
            jax.ShapeDtypeStruct((N, PACK_W), jnp.int32),
            jax.ShapeDtypeStruct((N_EXPERTS, N), F32),
        ],
        compiler_params=_cp(("parallel",), VMEM_LIMIT),
        name="mix_out",
    )(x2, gu, gu, gu, o1, o2, o3, l1, l2, l3, yc, mod1, mod2, *weights)


def _pick_rows(table, picks):
    G, GS = N_GROUPS, GROUP_SIZE
    eio = lax.broadcasted_iota(jnp.int32, (GS, table.shape[1]), 0)
    rows = []
    for k in range(TOP_K):
        idx = picks[k:k + 1]
        parts = [jnp.sum(jnp.where(eio + g * GS == idx, table[g * GS:(g + 1) * GS], 0.0), axis=0, keepdims=True)
                 for g in range(G)]
        rows.append(functools.reduce(jnp.add, parts))
    return jnp.concatenate(rows, axis=0)


def _route_choose(lg_ref, bias_ref):
    G, GS = N_GROUPS, GROUP_SIZE
    scores = jax.nn.sigmoid(lg_ref[...])
    sel = scores + bias_ref[...]
    tn = sel.shape[1]
    eio = lax.broadcasted_iota(jnp.int32, (GS, tn), 0)
    ninf = -jnp.inf

    gs = []
    for g in range(G):
        v = sel[g * GS:(g + 1) * GS]
        m1 = jnp.max(v, axis=0, keepdims=True)
        i1 = jnp.min(jnp.where(v == m1, eio, GS), axis=0, keepdims=True)
        m2 = jnp.max(jnp.where(eio == i1, ninf, v), axis=0, keepdims=True)
        gs.append(m1 + m2)
    gsm = jnp.concatenate(gs, axis=0)
    gio = lax.broadcasted_iota(jnp.int32, (G, tn), 0)
    rank = jnp.zeros((G, tn), jnp.int32)
    for g2 in range(G):
        beats = (gs[g2] > gsm) | ((gs[g2] == gsm) & (g2 < gio))
        rank = rank + beats.astype(jnp.int32)
    gsel = rank < TOPK_GROUPS

    vs = [jnp.where(gsel[g:g + 1], sel[g * GS:(g + 1) * GS], NEG) for g in range(G)]
    eid = [eio + g * GS for g in range(G)]
    chosen = [jnp.zeros((GS, tn), jnp.bool_) for _ in range(G)]
    picks = []
    for _ in range(TOP_K):
        m = functools.reduce(jnp.maximum, [jnp.max(v, axis=0, keepdims=True) for v in vs])
        idx = functools.reduce(jnp.minimum, [
            jnp.min(jnp.where(v == m, e, N_EXPERTS), axis=0, keepdims=True) for v, e in zip(vs, eid)])
        picks.append(idx)
        for g in range(G):
            hit = eid[g] == idx
            chosen[g] = chosen[g] | hit
            vs[g] = jnp.where(hit, ninf, vs[g])
    mask = jnp.concatenate(chosen, axis=0).astype(F32)
    return scores, jnp.concatenate(picks, axis=0), mask


def _route_kernel(lg_ref, bias_ref, tri_ref, dest_ref, w_ref, cnt_ref, run_sc, start_sc, mask_sc, picks_sc,
                  *, slot_block):
    phase = pl.program_id(0)
    step = pl.program_id(1)
    tn = lg_ref.shape[1]
    cols = pl.ds(pl.multiple_of(step * tn, tn), tn)

    @pl.when(phase == 0)
    def _():
        @pl.when(step == 0)
        def _():
            run_sc[...] = jnp.zeros(run_sc.shape, F32)

        scores, picks, mask = _route_choose(lg_ref, bias_ref)
        wk = _pick_rows(scores, picks)
        w_ref[0] = wk / jnp.sum(wk, axis=0, keepdims=True) * ROUTED_SCALE
        dest_ref[0] = jnp.zeros(dest_ref.shape[1:], dest_ref.dtype)
        mask_sc[:, cols] = mask.astype(BF16)
        picks_sc[:, cols] = picks
        run_sc[...] = run_sc[...] + jnp.sum(mask, axis=1, keepdims=True)

    @pl.when(phase == 1)
    def _():
        @pl.when(step == 0)
        def _():
            counts = run_sc[...].astype(jnp.int32)
            cnt_ref[...] = jnp.broadcast_to(counts, cnt_ref.shape)
            shift = slot_block.bit_length() - 1
            padded = lax.shift_left(lax.shift_right_logical(counts + (slot_block - 1), shift), shift).astype(F32)
            r = lax.broadcasted_iota(jnp.int32, (N_EXPERTS, N_EXPERTS), 0)
            c = lax.broadcasted_iota(jnp.int32, (N_EXPERTS, N_EXPERTS), 1)
            as_row = jnp.sum(jnp.where(r == c, padded, 0.0), axis=0, keepdims=True)
            start_sc[...] = jnp.sum(jnp.where(c < r, as_row, 0.0), axis=1, keepdims=True)
            run_sc[...] = jnp.zeros(run_sc.shape, F32)

        mask_b = mask_sc[:, cols]
        mask = mask_b.astype(F32)
        before = jnp.dot(mask_b, tri_ref[...], preferred_element_type=F32) - mask
        slot = start_sc[...] + run_sc[...] + before
        dest_ref[0] = _pick_rows(slot, picks_sc[:, cols]).astype(jnp.int32)
        w_ref[0] = jnp.zeros(w_ref.shape[1:], w_ref.dtype)
        run_sc[...] = run_sc[...] + jnp.sum(mask, axis=1, keepdims=True)


SLOT_BLOCK = 512


def route(logits_t, bias):
    E, N = logits_t.shape
    tn = 1024
    tri = (jnp.arange(tn)[:, None] <= jnp.arange(tn)[None, :]).astype(BF16)
    plane = lambda: pl.BlockSpec((1, TOP_K, tn), lambda p, i: (p, 0, i))
    dest, w, cnt = pl.pallas_call(
        functools.partial(_route_kernel, slot_block=SLOT_BLOCK),
        grid=(2, N // tn),
        in_specs=[
            pl.BlockSpec((E, tn), lambda p, i: (0, i * (1 - p))),
            pl.BlockSpec((E, 1), lambda p, i: (0, 0)),
            pl.BlockSpec((tn, tn), lambda p, i: (0, 0)),
        ],
        out_specs=[plane(), plane(), pl.BlockSpec((E, 128), lambda p, i: (0, 0))],
        out_shape=[
            jax.ShapeDtypeStruct((2, TOP_K, N), jnp.int32),
            jax.ShapeDtypeStruct((2, TOP_K, N), F32),
            jax.ShapeDtypeStruct((E, 128), jnp.int32),
        ],
        scratch_shapes=[pltpu.VMEM((E, 1), F32), pltpu.VMEM((E, 1), F32),
                        pltpu.VMEM((E, N), BF16), pltpu.VMEM((TOP_K, N), jnp.int32)],
        compiler_params=_cp(("arbitrary", "arbitrary")),
        name="route",
    )(logits_t, bias.reshape(E, 1), tri)
    return dest[1], w[0], cnt[:, 0]


def block_tables(counts, n_tokens):
    E = counts.shape[0]
    blk = SLOT_BLOCK
    nblk = (n_tokens * TOP_K + E * blk) // blk
    per_expert = (counts + blk - 1) // blk
    bend = jnp.cumsum(per_expert)
    bstart = bend - per_expert
    b = jnp.arange(nblk, dtype=jnp.int32)[:, None]
    owns = (bstart[None, :] <= b) & (b < bend[None, :])
    blk_e = jnp.minimum(jnp.sum(bend[None, :] <= b, axis=1), E - 1).astype(jnp.int32)
    rows_left = counts[None, :] - (b - bstart[None, :]) * blk
    nvalid = jnp.sum(jnp.where(owns, jnp.clip(rows_left, 0, blk), 0), axis=1)
    return blk_e, nvalid.astype(jnp.int32)


def _sc_mesh():
    return plsc.VectorSubcoreMesh(core_axis_name="c", subcore_axis_name="s")


SC_WINDOW = 128


def sc_scatter_rows(x, dest, n_slots):
    N, W = x.shape
    K = dest.shape[0]

    @functools.partial(pl.kernel, out_type=jax.ShapeDtypeStruct((n_slots, W), x.dtype), mesh=_sc_mesh(),
                       scratch_types=[])
    def scatter(x_hbm, i_hbm, o_hbm):
        def body(x_vmem, i_vmem):
            for k in range(K):
                pltpu.sync_copy(x_vmem, o_hbm.at[i_vmem.at[k]])

        pltpu.emit_pipeline(
            body,
            grid=(N // SC_WINDOW,),
            in_specs=[pl.BlockSpec((SC_WINDOW, W), lambda i: (i, 0)),
                      pl.BlockSpec((K, SC_WINDOW), lambda i: (0, i))],
            out_specs=[],
            core_axis_name=("c", "s"),
            dimension_semantics=(pltpu.PARALLEL,),
        )(x_hbm, i_hbm)

    return scatter(x, dest)


SC_LANES = 16
SC_GATHER_TOKENS = 8


def sc_weighted_gather(y, dest, wts):
    W = y.shape[1]
    K, N = dest.shape
    G, L = SC_GATHER_TOKENS, SC_LANES
    batches = SC_WINDOW // G

    @functools.partial(
        pl.kernel, out_type=jax.ShapeDtypeStruct((N, W), y.dtype), mesh=_sc_mesh(),
        scratch_types=[pltpu.VMEM((2, K, G, W), y.dtype), pltpu.SemaphoreType.DMA((2,))],
        compiler_params=pltpu.CompilerParams(needs_layout_passes=False))
    def gather(y_hbm, i_hbm, w_hbm, o_hbm, rows2, sems):
        def body(i_vmem, w_vmem, o_vmem):
            def fetch(batch, slot):
                return [pltpu.make_async_copy(y_hbm.at[i_vmem.at[k, pl.ds(batch * G, G)]], rows2.at[slot, k],
                                              sems.at[slot]) for k in range(K)]

            for c in fetch(0, 0):
                c.start()

            @pl.loop(0, batches)
            def _(batch):
                slot = batch % 2

                @pl.when(batch + 1 < batches)
                def _():
                    for c in fetch(batch + 1, 1 - slot):
                        c.start()

                for c in fetch(batch, slot):
                    c.wait()
                rows = rows2.at[slot]

                @pl.loop(0, G)
                def _(t):
                    tok = jnp.full((L,), batch * G + t, jnp.int32)
                    wk = [plsc.load_gather(w_vmem, [jnp.full((L,), k, jnp.int32), tok]) for k in range(K)]

                    @plsc.parallel_loop(0, W // L, unroll=4)
                    def _(j):
                        lo = jnp.zeros((L,), F32)
                        hi = jnp.zeros((L,), F32)
                        for k in range(K):
                            pair = plsc.bitcast(rows[k, t, pl.ds(j * L, L)], BF16)
                            a, b = plsc.unpack(pair, format=plsc.PackFormat.INTERLEAVED)
                            lo = lo + wk[k] * a
                            hi = hi + wk[k] * b
                        o_vmem[batch * G + t, pl.ds(j * L, L)] = plsc.bitcast(
                            plsc.pack(lo, hi, format=plsc.PackFormat.INTERLEAVED), y.dtype)

        pltpu.emit_pipeline(
            body,
            grid=(N // SC_WINDOW,),
            in_specs=[pl.BlockSpec((K, SC_WINDOW), lambda i: (0, i)),
                      pl.BlockSpec((K, SC_WINDOW), lambda i: (0, i))],
            out_specs=[pl.BlockSpec((SC_WINDOW, W), lambda i: (i, 0))],
            core_axis_name=("c", "s"),
            dimension_semantics=(pltpu.PARALLEL,),
        )(i_hbm, w_hbm, o_hbm)

    return gather(y, dest, wts)


def _expert_kernel(blk_e_ref, nvalid_ref, xa_ref, xb_ref, w1_ref, w3_ref, w2_ref, ya_ref, yb_ref,
                   w1_sc, w3_sc, w2_sc):
    b = pl.program_id(0)
    nv = nvalid_ref[b]
    prev_e = blk_e_ref[jnp.maximum(b - 1, 0)]

    @pl.when((b == 0) | (blk_e_ref[b] != prev_e))
    def _():
        w1_sc[...] = w1_ref[0, 0].astype(BF16)
        w3_sc[...] = w3_ref[0, 0].astype(BF16)
        w2_sc[...] = w2_ref[0, 0].astype(BF16)

    @pl.when(nv > 0)
    def _():
        x = _unpack_row_halves(xa_ref[...], xb_ref[...])
        rows = lax.broadcasted_iota(jnp.int32, x.shape, 0)
        x = jnp.where(rows < nv, x, 0.0).astype(BF16)
        hid = _silu(jnp.dot(x, w1_sc[...], preferred_element_type=F32)) * jnp.dot(
            x, w3_sc[...], preferred_element_type=F32)
        y = jnp.dot(hid.astype(BF16), w2_sc[...], preferred_element_type=F32)
        ya_ref[...], yb_ref[...] = _pack_row_halves(y)

    @pl.when(nv == 0)
    def _():
        ya_ref[...] = jnp.zeros(ya_ref.shape, ya_ref.dtype)
        yb_ref[...] = jnp.zeros(yb_ref.shape, yb_ref.dtype)


def routed_experts(xa, xb, blk_e, nvalid, w1, w3, w2, layer):
    P = xa.shape[0]
    blk = SLOT_BLOCK
    _, E, D, FF = w1.shape
    slots = lambda: pl.BlockSpec((blk, PACK_W), lambda b, be, nv: (b, 0))
    grid_spec = pltpu.PrefetchScalarGridSpec(
        num_scalar_prefetch=2,
        grid=(P // blk,),
        in_specs=[
            slots(), slots(),
            pl.BlockSpec((1, 1, D, FF), lambda b, be, nv: (layer, be[b], 0, 0)),
            pl.BlockSpec((1, 1, D, FF), lambda b, be, nv: (layer, be[b], 0, 0)),
            pl.BlockSpec((1, 1, FF, D), lambda b, be, nv: (layer, be[b], 0, 0)),
        ],
        out_specs=[slots(), slots()],
        scratch_shapes=[pltpu.VMEM((D, FF), BF16), pltpu.VMEM((D, FF), BF16), pltpu.VMEM((FF, D), BF16)],
    )
    return pl.pallas_call(
        _expert_kernel,
        grid_spec=grid_spec,
        out_shape=[jax.ShapeDtypeStruct((P, PACK_W), jnp.int32)] * 2,
        compiler_params=_cp(("arbitrary",), VMEM_LIMIT),
        name="routed_experts",
    )(blk_e, nvalid, xa, xb, w1, w3, w2)


def _combine_kernel(xmid_ref, ra_ref, rb_ref, mod2_ref, fg_ref, *rest, final):
    out_ref = rest[-1]
    D = xmid_ref.shape[1]
    x = xmid_ref[...] + mod2_ref[0][:, 2 * D:] * _unpack_row_halves(ra_ref[...], rb_ref[...])
    if final:
        x = x * lax.rsqrt(jnp.mean(x * x, axis=-1, keepdims=True) + EPS) * fg_ref[...]
    out_ref[...] = x


def combine(xmid, ra, rb, mod2, final_g, seq, final, out_rows=None, row0=0, out_buf=None):
    N, D = xmid.shape
    tm = 512
    tpb = seq // tm
    tile0 = row0 // tm
    in_specs = [
        pl.BlockSpec((tm, D), lambda i: (i, 0)),
        pl.BlockSpec((tm, PACK_W), lambda i: (i, 0)),
        pl.BlockSpec((tm, PACK_W), lambda i: (i, 0)),
        pl.BlockSpec((1, 1, 3 * D), lambda i: (i // tpb, 0, 0)),
        pl.BlockSpec((1, D), lambda i: (0, 0)),
    ]
    args = [xmid, ra, rb, mod2, final_g.reshape(1, D)]
    aliases = {}
    if out_buf is not None:
        in_specs.append(pl.BlockSpec(memory_space=pl.ANY))
        args.append(out_buf)
        aliases = {len(args) - 1: 0}
    return pl.pallas_call(
        functools.partial(_combine_kernel, final=final),
        grid=(N // tm,),
        in_specs=in_specs,
        out_specs=pl.BlockSpec((tm, D), lambda i: (i + tile0, 0)),
        out_shape=jax.ShapeDtypeStruct((out_rows or N, D), F32),
        input_output_aliases=aliases,
        compiler_params=_cp(("parallel",), VMEM_LIMIT),
        name="combine",
    )(*args)


TOKEN_STREAMS = 2


def _permute_w_in(w):
    ub = w[:, 3 * DA:3 * DA + DB]
    lat_lo = 3 * DA + DB
    lat_hi = lat_lo + Q_LORA + KV_LORA + QK_ROPE
    lat, gates = w[:, lat_lo:lat_hi], w[:, lat_hi:]
    pad = jnp.zeros((w.shape[0], LAT_W - (lat_hi - lat_lo)), w.dtype)
    parts = [gates, ub, lat, pad]
    for g in range(len(DIL_GROUPS)):
        sl = slice(g * GROUP_W, (g + 1) * GROUP_W)
        parts += [w[:, :DA][:, sl] * (HEAD_DIM_A ** -0.5), w[:, DA:2 * DA][:, sl], w[:, 2 * DA:3 * DA][:, sl]]
    return jnp.concatenate(parts, axis=1).astype(BF16)


def kernel(x, c, positions, ada_mix_w, ada_mix_b, norm_mix_g, w_in, pool_w, pool_scale, cq_norm_g, ckv_norm_g, w_uq, w_ukv, w_oa, w_ob, w_oc, w_out, ada_ffn_w, ada_ffn_b, norm_ffn_g, router_w, router_bias, exp_w1, exp_w3, exp_w2, sh_w1, sh_w3, sh_w2, final_g):
    B, S, D = x.shape
    depth = w_in.shape[0]
    mod_mix = adaln_rows(c, ada_mix_w, ada_mix_b)
    mod_ffn = adaln_rows(c, ada_ffn_w, ada_ffn_b)
    streams = TOKEN_STREAMS if B % TOKEN_STREAMS == 0 else 1
    Bs = B // streams
    Ns = Bs * S
    x_all = x.reshape(B * S, D)
    xs = [None] * streams
    out_all = None
    pos_s = [positions[s * Bs:(s + 1) * Bs] for s in range(streams)]
    for l in range(depth):
        last = l == depth - 1
        w_in_l = _permute_w_in(w_in[l])
        mla_w = _mla_weights(cq_norm_g[l], ckv_norm_g[l], w_uq[l], w_ukv[l])
        mix_w = (norm_ffn_g[l], pool_w[l].astype(BF16), pool_scale[l],
                 w_oa[l].astype(BF16), w_ob[l].astype(BF16), w_oc[l].astype(BF16), w_out[l].astype(BF16),
                 router_w[l].T.astype(BF16), sh_w1[l].astype(BF16), sh_w3[l].astype(BF16), sh_w2[l].astype(BF16))
        for s in range(streams):
            x2, row0 = (x_all, s * Ns) if l == 0 else (xs[s], 0)
            mod1 = mod_mix[l, s * Bs:(s + 1) * Bs].reshape(Bs, 1, 3 * D)
            mod2 = mod_ffn[l, s * Bs:(s + 1) * Bs].reshape(Bs, 1, 3 * D)
            gu, lat, *qkv = in_projection(x2, norm_mix_g[l], mod1, w_in_l, S, row0)
            dil = [dilated_attention(qkv[2 * g], qkv[2 * g + 1]) for g in range(len(DIL_GROUPS))]
            q_all, k_all, vt_all = mla_prep(lat, pos_s[s], *mla_w, Bs, S)
            yc = mla_attention(q_all, k_all, vt_all, Bs, S)
            xmid, h2a, h2b, logits_t = mix_out(x2, gu, dil, yc, mod1, mod2, *mix_w, S, row0)
            dest, w_k, counts = route(logits_t, router_bias[l])
            blk_e, nvalid = block_tables(counts, Ns)
            n_slots = blk_e.shape[0] * SLOT_BLOCK
            xa = sc_scatter_rows(h2a, dest, n_slots)
            xb = sc_scatter_rows(h2b, dest, n_slots)
            ya, yb = routed_experts(xa, xb, blk_e, nvalid, exp_w1, exp_w3, exp_w2, l)
            ra = sc_weighted_gather(ya, dest, w_k)
            rb = sc_weighted_gather(yb, dest, w_k)
            if last:
                out_all = combine(xmid, ra, rb, mod2, final_g, S, True, B * S, s * Ns, out_all)
            else:
                xs[s] = combine(xmid, ra, rb, mod2, final_g, S, False)
    return out_all.reshape(B, S, D)
```

```python
import functools
import math

import jax
import jax.numpy as jnp
from jax import lax
from jax.experimental import pallas as pl
from jax.experimental.pallas import tpu as pltpu
from jax.experimental.pallas import tpu_sc as plsc

F32 = jnp.float32
BF16 = jnp.bfloat16
HIGHEST = lax.Precision.HIGHEST

D_MODEL = 1024
HEAD_DIM_A = 64
HEADS_PER_GROUP_A = 4
DIL_GROUPS = ((128, 1), (512, 4), (2048, 16))
GROUP_W = HEADS_PER_GROUP_A * HEAD_DIM_A
DA = GROUP_W * len(DIL_GROUPS)
POOL_WINDOWS = (2, 4, 8, 16)
POOL_GROUP_DIM = 128
DB = POOL_GROUP_DIM * len(POOL_WINDOWS)
POOL_HALO = 16
N_HEADS_C = 8
QK_NOPE = 64
QK_ROPE = 32
V_DIM = 64
Q_LORA = 384
KV_LORA = 256
DC = N_HEADS_C * V_DIM
HEAD_PAD_C = 128
ROPE_THETA = 10000.0
N_EXPERTS = 64
TOP_K = 8
N_GROUPS = 8
TOPK_GROUPS = 4
GROUP_SIZE = N_EXPERTS // N_GROUPS
EXPERT_FF = 256
ROUTED_SCALE = 2.5
EPS = 1e-6
NEG = -1e30
Q_BLOCK = 128

LAT_W = 768
GU_W = 3 * D_MODEL + DB
IN_OUT_WIDTHS = (GU_W, LAT_W) + (2 * GROUP_W, GROUP_W) * len(DIL_GROUPS)

VMEM_LIMIT = 56 * 1024 * 1024


def _cp(sem, vmem=None):
    return pltpu.CompilerParams(dimension_semantics=sem, vmem_limit_bytes=vmem)


def _silu(v):
    return v * jax.nn.sigmoid(v)


def _nt_dot(a, b):
    return lax.dot_general(a, b, (((1,), (1,)), ((), ())), preferred_element_type=F32)


PACK_W = D_MODEL // 4
_HI_MASK = -65536


def _bf16_bits(v):
    return lax.bitcast_convert_type(v.astype(BF16).astype(F32), jnp.int32)


def _pack_row_halves(v):
    halves = []
    for h in range(2):
        lo = _bf16_bits(v[:, (2 * h) * PACK_W:(2 * h + 1) * PACK_W])
        hi = _bf16_bits(v[:, (2 * h + 1) * PACK_W:(2 * h + 2) * PACK_W])
        halves.append(lax.shift_right_logical(lo, 16) | (hi & _HI_MASK))
    return halves


def _unpack_row_halves(wa, wb):
    parts = []
    for w in (wa, wb):
        parts.append(lax.bitcast_convert_type(lax.shift_left(w, 16), F32))
        parts.append(lax.bitcast_convert_type(w & _HI_MASK, F32))
    return jnp.concatenate(parts, axis=1)


def _adaln_kernel(c_ref, w_ref, b_ref, o_ref):
    s = _silu(c_ref[...])
    o_ref[0] = jnp.dot(s, w_ref[0], preferred_element_type=F32, precision=HIGHEST) + b_ref[0]


def adaln_rows(c, w, b):
    L, D, D3 = w.shape
    B = c.shape[0]
    tn = 1024
    return pl.pallas_call(
        _adaln_kernel,
        grid=(L, D3 // tn),
        in_specs=[
            pl.BlockSpec((B, D), lambda l, j: (0, 0)),
            pl.BlockSpec((1, D, tn), lambda l, j: (l, 0, j)),
            pl.BlockSpec((1, 1, tn), lambda l, j: (l, 0, j)),
        ],
        out_specs=pl.BlockSpec((1, B, tn), lambda l, j: (l, 0, j)),
        out_shape=jax.ShapeDtypeStruct((L, B, D3), F32),
        compiler_params=_cp(("parallel", "parallel")),
        name="adaln_rows",
    )(c, w, b.reshape(L, 1, D3))


LANES = 128


def _inproj_kernel(x_ref, g_ref, mod_ref, w_ref, *refs, chunk):
    o_refs, scr = refs[:-1], refs[-1]
    D = x_ref.shape[1]
    x = x_ref[...]
    y = x * lax.rsqrt(jnp.mean(x * x, axis=-1, keepdims=True) + EPS) * g_ref[...]
    mod = mod_ref[0]
    h = (y * (1.0 + mod[:, D:2 * D]) + mod[:, :D]).astype(BF16)
    col = 0
    for o_ref in o_refs:
        width = o_ref.shape[-1]
        if o_ref.ndim == 2:
            for c0 in range(0, width, chunk):
                cw = min(chunk, width - c0)
                o_ref[:, c0:c0 + cw] = jnp.dot(
                    h, w_ref[:, col + c0:col + c0 + cw], preferred_element_type=F32).astype(o_ref.dtype)
        else:
            dil, rows = o_ref.shape[1], o_ref.shape[2]
            z = jnp.dot(h, w_ref[:, col:col + width], preferred_element_type=F32)
            if dil == 1:
                o_ref[0, 0] = z.astype(o_ref.dtype)
            else:
                for c in range(width // LANES):
                    scr[c] = z[:, c * LANES:(c + 1) * LANES]
                for r in range(dil):
                    o_ref[0, r] = jnp.concatenate(
                        [scr[c, pl.ds(r, rows, stride=dil), :] for c in range(width // LANES)],
                        axis=1).astype(o_ref.dtype)
        col += width


def in_projection(x2, g, mod, w, seq, row0=0):
    D = x2.shape[1]
    B = mod.shape[0]
    N = B * seq
    tm = 512
    tpb = seq // tm
    tile0 = row0 // tm
    out_specs = [pl.BlockSpec((tm, wd), lambda i: (i, 0)) for wd in IN_OUT_WIDTHS[:2]]
    out_shape = [jax.ShapeDtypeStruct((N, wd), BF16) for wd in IN_OUT_WIDTHS[:2]]
    for grp, (_, dil) in enumerate(DIL_GROUPS):
        for wd in IN_OUT_WIDTHS[2 + 2 * grp:4 + 2 * grp]:
            out_specs.append(pl.BlockSpec((1, dil, tm // dil, wd), lambda i: (i // tpb, 0, i % tpb, 0)))
            out_shape.append(jax.ShapeDtypeStruct((B, dil, seq // dil, wd), BF16))
    return pl.pallas_call(
        functools.partial(_inproj_kernel, chunk=512),
        grid=(N // tm,),
        in_specs=[
            pl.BlockSpec((tm, D), lambda i: (i + tile0, 0)),
            pl.BlockSpec((1, D), lambda i: (0, 0)),
            pl.BlockSpec((1, 1, 3 * D), lambda i: (i // tpb, 0, 0)),
            pl.BlockSpec(w.shape, lambda i: (0, 0), pipeline_mode=pl.Buffered(1)),
        ],
        out_specs=out_specs,
        out_shape=out_shape,
        scratch_shapes=[pltpu.VMEM((max(IN_OUT_WIDTHS[2:]) // LANES, tm, LANES), F32)],
        compiler_params=_cp(("parallel",), VMEM_LIMIT),
        name="in_projection",
    )(x2, g.reshape(1, D), mod, w)


def _dilated_kernel(q_ref, kc_ref, kp_ref, vc_ref, vp_ref, o_ref, lse_ref):
    i = pl.program_id(1)
    T = Q_BLOCK
    key = lax.broadcasted_iota(jnp.int32, (T, T), 0)
    qry = lax.broadcasted_iota(jnp.int32, (T, T), 1)
    valid_c = key <= qry
    near = key >= qry
    seqs, run = q_ref.shape[0], q_ref.shape[1] // T
    heads = [slice(h * HEAD_DIM_A, (h + 1) * HEAD_DIM_A) for h in range(HEADS_PER_GROUP_A)]

    def transposed(v):
        return v.astype(F32).T.astype(BF16)

    vts = {(s, j): transposed(vc_ref[s, j * T:(j + 1) * T, :]) for s in range(seqs) for j in range(run)}
    vt_before = [transposed(vp_ref[s]) for s in range(seqs)]

    def blocks(s, j):
        rows = slice(j * T, (j + 1) * T)
        if j == 0:
            return rows, kc_ref[s, rows, :], vts[s, 0], kp_ref[s], vt_before[s], near & (i > 0)
        before = slice((j - 1) * T, j * T)
        return rows, kc_ref[s, rows, :], vts[s, j], kc_ref[s, before, :], vts[s, j - 1], near

    scores, probs = {}, {}
    for s in range(seqs):
        for j in range(run):
            rows, kc, _, kp, _, valid_p = blocks(s, j)
            q = q_ref[s, rows, :]
            for h, sl in enumerate(heads):
                qh = q[:, sl]
                scores[s, j, h] = (jnp.where(valid_c, _nt_dot(kc[:, sl], qh), NEG),
                                   jnp.where(valid_p, _nt_dot(kp[:, sl], qh), NEG))
    for chain, (sc, sp) in scores.items():
        m = jnp.maximum(jnp.max(sc, axis=0, keepdims=True), jnp.max(sp, axis=0, keepdims=True))
        pc = jnp.exp(sc - m)
        pp = jnp.exp(sp - m)
        den = jnp.sum(pc, axis=0, keepdims=True) + jnp.sum(pp, axis=0, keepdims=True)
        probs[chain] = (pc.astype(BF16), pp.astype(BF16), den, m + jnp.log(den))
    spread = LSE_LANES // len(heads)
    for s in range(seqs):
        for j in range(run):
            rows, _, vtc, _, vtp, _ = blocks(s, j)
            outs = []
            for h, sl in enumerate(heads):
                pc, pp, den, _ = probs[s, j, h]
                o = (jnp.dot(vtc[sl, :], pc, preferred_element_type=F32)
                     + jnp.dot(vtp[sl, :], pp, preferred_element_type=F32))
                outs.append(o / den)
            o_ref[s, rows, :] = jnp.concatenate(outs, axis=0).T.astype(o_ref.dtype)
            lse_t = jnp.concatenate(
                [jnp.broadcast_to(probs[s, j, h][3], (spread, T)) for h in range(len(heads))], axis=0)
            lse_ref[s, rows, :] = lse_t.T


DILATED_RUN = 8


LSE_LANES = 128


def dilated_attention(qk, v):
    batch, dilation, L, _ = qk.shape
    nb = L // Q_BLOCK
    run = min(DILATED_RUN, nb)
    seqs = DILATED_RUN // run
    qk_r = qk.reshape(batch * dilation, L, 2 * GROUP_W)
    v_r = v.reshape(batch * dilation, L, GROUP_W)
    before = lambda i: jnp.maximum(i * run - 1, 0)
    o, lse = pl.pallas_call(
        _dilated_kernel,
        grid=(batch * dilation // seqs, nb // run),
        in_specs=[
            pl.BlockSpec((seqs, run * Q_BLOCK, GROUP_W), lambda s, i: (s, i, 0)),
            pl.BlockSpec((seqs, run * Q_BLOCK, GROUP_W), lambda s, i: (s, i, 1)),
            pl.BlockSpec((seqs, Q_BLOCK, GROUP_W), lambda s, i: (s, before(i), 1)),
            pl.BlockSpec((seqs, run * Q_BLOCK, GROUP_W), lambda s, i: (s, i, 0)),
            pl.BlockSpec((seqs, Q_BLOCK, GROUP_W), lambda s, i: (s, before(i), 0)),
        ],
        out_specs=[
            pl.BlockSpec((seqs, run * Q_BLOCK, GROUP_W), lambda s, i: (s, i, 0)),
            pl.BlockSpec((seqs, run * Q_BLOCK, LSE_LANES), lambda s, i: (s, i, 0)),
        ],
        out_shape=[
            jax.ShapeDtypeStruct((batch * dilation, L, GROUP_W), BF16),
            jax.ShapeDtypeStruct((batch * dilation, L, LSE_LANES), F32),
        ],
        compiler_params=_cp(("parallel", "parallel")),
        name=f"dilated_attention_d{dilation}",
    )(qk_r, qk_r, qk_r, v_r, v_r)
    return o.reshape(batch, dilation, L, GROUP_W), lse.reshape(batch, dilation, L, LSE_LANES)


def _mla_prep_kernel(lat_ref, pos_ref, gq_ref, gkv_ref, wq_ref, wk_ref, wvt_ref, freq_ref, spread_ref, one_ref,
                     q_ref, k_ref, vt_ref):
    HP = N_HEADS_C * HEAD_PAD_C
    lat = lat_ref[...].astype(F32)
    cq = lat[:, :Q_LORA]
    ckr = lat[:, Q_LORA:]
    zq = (cq * lax.rsqrt(jnp.mean(cq * cq, axis=-1, keepdims=True) + EPS) * gq_ref[...]).astype(BF16)
    lane = lax.broadcasted_iota(jnp.int32, ckr.shape, 1)
    is_kv = lane < KV_LORA
    ms = jnp.sum(jnp.where(is_kv, ckr * ckr, 0.0), axis=-1, keepdims=True) * (1.0 / KV_LORA)
    zkv = (ckr * jnp.where(is_kv, lax.rsqrt(ms + EPS) * gkv_ref[...], 1.0)).astype(BF16)
    qq = jnp.dot(zq, wq_ref[...], preferred_element_type=F32)
    kk = jnp.dot(zkv, wk_ref[...], preferred_element_type=F32)
    ang_t = freq_ref[...] * pos_ref[0].astype(F32)

    def to_lanes(t):
        hi = t.astype(BF16)
        lo = (t - hi.astype(F32)).astype(BF16)
        tn_dot = lambda a: lax.dot_general(a, spread_ref[...], (((0,), (0,)), ((), ())), preferred_element_type=F32)
        return tn_dot(hi) + tn_dot(lo)

    cos = to_lanes(jnp.cos(ang_t)) + one_ref[...]
    sin = to_lanes(jnp.sin(ang_t))
    for h in range(N_HEADS_C):
        lo, hi = h * HEAD_PAD_C, (h + 1) * HEAD_PAD_C
        q_ref[:, lo:hi] = (qq[:, lo:hi] * cos + qq[:, HP + lo:HP + hi] * sin).astype(q_ref.dtype)
        k_ref[:, lo:hi] = (kk[:, lo:hi] * cos + kk[:, HP + lo:HP + hi] * sin).astype(k_ref.dtype)
    vt_ref[0] = _nt_dot(wvt_ref[...], zkv).astype(vt_ref.dtype)


def _mla_weights(cq_g, ckv_g, w_uq, w_ukv):
    H, HPAD, half = N_HEADS_C, HEAD_PAD_C, QK_ROPE // 2
    scale = (QK_NOPE + QK_ROPE) ** -0.5 * math.log2(math.e)
    wq = w_uq.reshape(Q_LORA, H, QK_NOPE + QK_ROPE) * scale
    q_lin = jnp.pad(wq, ((0, 0), (0, 0), (0, HPAD - QK_NOPE - QK_ROPE)))
    r1, r2 = wq[..., QK_NOPE:QK_NOPE + half], wq[..., QK_NOPE + half:]
    q_sw = jnp.concatenate([jnp.zeros((Q_LORA, H, QK_NOPE), F32), -r2, r1,
                            jnp.zeros((Q_LORA, H, HPAD - QK_NOPE - QK_ROPE), F32)], axis=-1)
    wq_big = jnp.concatenate([q_lin.reshape(Q_LORA, H * HPAD), q_sw.reshape(Q_LORA, H * HPAD)], axis=1)

    rows = LAT_W - Q_LORA
    wkv = w_ukv.reshape(KV_LORA, H, QK_NOPE + V_DIM)
    eye = jnp.eye(QK_ROPE, dtype=F32)
    k_lin = jnp.zeros((rows, H, HPAD), F32)
    k_lin = k_lin.at[:KV_LORA, :, :QK_NOPE].set(wkv[..., :QK_NOPE])
    k_lin = k_lin.at[KV_LORA:KV_LORA + QK_ROPE, :, QK_NOPE:QK_NOPE + QK_ROPE].set(
        jnp.broadcast_to(eye[:, None, :], (QK_ROPE, H, QK_ROPE)))
    swap = jnp.zeros((QK_ROPE, QK_ROPE), F32).at[half:, :half].set(-jnp.eye(half)).at[:half, half:].set(jnp.eye(half))
    k_sw = jnp.zeros((rows, H, HPAD), F32)
    k_sw = k_sw.at[KV_LORA:KV_LORA + QK_ROPE, :, QK_NOPE:QK_NOPE + QK_ROPE].set(
        jnp.broadcast_to(swap[:, None, :], (QK_ROPE, H, QK_ROPE)))
    v_w = jnp.zeros((rows, H, V_DIM), F32).at[:KV_LORA].set(wkv[..., QK_NOPE:])
    wk_big = jnp.concatenate([k_lin.reshape(rows, H * HPAD), k_sw.reshape(rows, H * HPAD)], axis=1)
    wv_t = v_w.reshape(rows, H * V_DIM).T

    gkv = jnp.concatenate([ckv_g, jnp.ones((rows - KV_LORA,), F32)]).reshape(1, rows)
    return cq_g.reshape(1, Q_LORA), gkv, wq_big.astype(BF16), wk_big.astype(BF16), wv_t.astype(BF16)


def _rope_tables():
    half = QK_ROPE // 2
    freqs = (ROPE_THETA ** (-jnp.arange(0, QK_ROPE, 2, dtype=F32) / QK_ROPE)).reshape(half, 1)
    lane = jnp.arange(HEAD_PAD_C)[None, :]
    j = jnp.arange(half)[:, None]
    spread = (lane == QK_NOPE + j) | (lane == QK_NOPE + half + j)
    off_rope = ~jnp.any(spread, axis=0, keepdims=True)
    return freqs, spread.astype(BF16), off_rope.astype(F32)


def mla_prep(lat, positions, gq, gkv, wq_big, wk_big, wv_t, batch, seq):
    N = lat.shape[0]
    HP = N_HEADS_C * HEAD_PAD_C
    tm = 512
    tpb = seq // tm
    freqs, spread, off_rope = _rope_tables()
    pos_rows = positions.reshape(N // tm, 1, tm)
    const = lambda shape: pl.BlockSpec(shape, lambda i: (0, 0))
    return pl.pallas_call(
        _mla_prep_kernel,
        grid=(N // tm,),
        in_specs=[
            pl.BlockSpec((tm, LAT_W), lambda i: (i, 0)),
            pl.BlockSpec((1, 1, tm), lambda i: (i, 0, 0)),
            const(gq.shape), const(gkv.shape), const(wq_big.shape), const(wk_big.shape), const(wv_t.shape),
            const(freqs.shape), const(spread.shape), const(off_rope.shape),
        ],
        out_specs=[
            pl.BlockSpec((tm, HP), lambda i: (i, 0)),
            pl.BlockSpec((tm, HP), lambda i: (i, 0)),
            pl.BlockSpec((1, DC, tm), lambda i: (i // tpb, 0, i % tpb)),
        ],
        out_shape=[
            jax.ShapeDtypeStruct((N, HP), BF16),
            jax.ShapeDtypeStruct((N, HP), BF16),
            jax.ShapeDtypeStruct((batch, DC, seq), BF16),
        ],
        compiler_params=_cp(("parallel",), VMEM_LIMIT),
        name="mla_prep",
    )(lat, pos_rows, gq, gkv, wq_big, wk_big, wv_t, freqs, spread, off_rope)


HEADS_PER_STEP_C = 8
FLASH_Q_CHUNK = 256


def _mla_flash_kernel(qi_ref, ki_ref, q_ref, k_ref, vt_ref, o_ref, m_sc, l_sc, acc_sc):
    t = pl.program_id(2)
    qi, ki = qi_ref[t], ki_ref[t]

    @pl.when(ki == 0)
    def _():
        m_sc[...] = jnp.full(m_sc.shape, NEG, F32)
        l_sc[...] = jnp.zeros(l_sc.shape, F32)
        acc_sc[...] = jnp.zeros(acc_sc.shape, F32)

    def step(masked):
        T = q_ref.shape[1]
        if masked:
            key = lax.broadcasted_iota(jnp.int32, (T, T), 0)
            qry = lax.broadcasted_iota(jnp.int32, (T, T), 1)
            keep = key <= qry
        chains = [(h, c) for h in range(HEADS_PER_STEP_C) for c in range(T // FLASH_Q_CHUNK)]
        scores, probs, alphas = {}, {}, {}

        def keys_for(c):
            return (c + 1) * FLASH_Q_CHUNK if masked else T

        def qk(h, c):
            qs = slice(c * FLASH_Q_CHUNK, (c + 1) * FLASH_Q_CHUNK)
            q = q_ref[0, qs, h * HEAD_PAD_C:(h + 1) * HEAD_PAD_C]
            k = k_ref[0, :keys_for(c), h * HEAD_PAD_C:(h + 1) * HEAD_PAD_C]
            st = _nt_dot(k, q)
            scores[h, c] = jnp.where(keep[:keys_for(c), qs], st, NEG) if masked else st

        def softmax(h, c):
            qs = slice(c * FLASH_Q_CHUNK, (c + 1) * FLASH_Q_CHUNK)
            st = scores.pop((h, c))
            m_prev = m_sc[h, :, qs]
            m_new = jnp.maximum(m_prev, jnp.max(st, axis=0, keepdims=True))
            alpha = jnp.exp2(m_prev - m_new)
            p = jnp.exp2(st - m_new)
            l_sc[h, :, qs] = alpha * l_sc[h, :, qs] + jnp.sum(p, axis=0, keepdims=True)
            m_sc[h, :, qs] = m_new
            probs[h, c], alphas[h, c] = p.astype(BF16), alpha

        def pv(h, c):
            qs = slice(c * FLASH_Q_CHUNK, (c + 1) * FLASH_Q_CHUNK)
            vt = vt_ref[0, h * V_DIM:(h + 1) * V_DIM, :keys_for(c)]
            acc_sc[h, :, qs] = alphas.pop((h, c)) * acc_sc[h, :, qs] + jnp.dot(
                vt, probs.pop((h, c)), preferred_element_type=F32)

        for phase in (qk, softmax, pv):
            for ch in chains:
                phase(*ch)

    @pl.when(ki < qi)
    def _():
        step(False)

    @pl.when(ki == qi)
    def _():
        step(True)
        ot = jnp.concatenate([acc_sc[h] / l_sc[h] for h in range(HEADS_PER_STEP_C)], axis=0)
        o_ref[0] = ot.T.astype(o_ref.dtype)


def mla_attention(q_all, k_all, vt_all, batch, seq):
    T = 512
    nq = seq // T
    pairs = [(a, b) for a in range(nq) for b in range(a + 1)]
    qi_tab = jnp.asarray([p[0] for p in pairs], jnp.int32)
    ki_tab = jnp.asarray([p[1] for p in pairs], jnp.int32)
    hp = N_HEADS_C // HEADS_PER_STEP_C
    qw = HEADS_PER_STEP_C * HEAD_PAD_C
    vw = HEADS_PER_STEP_C * V_DIM
    q3 = q_all.reshape(batch, seq, -1)
    k3 = k_all.reshape(batch, seq, -1)
    grid_spec = pltpu.PrefetchScalarGridSpec(
        num_scalar_prefetch=2,
        grid=(batch, hp, len(pairs)),
        in_specs=[
            pl.BlockSpec((1, T, qw), lambda b, h, t, qi, ki: (b, qi[t], h)),
            pl.BlockSpec((1, T, qw), lambda b, h, t, qi, ki: (b, ki[t], h)),
            pl.BlockSpec((1, vw, T), lambda b, h, t, qi, ki: (b, h, ki[t])),
        ],
        out_specs=pl.BlockSpec((1, T, vw), lambda b, h, t, qi, ki: (b, qi[t], h)),
        scratch_shapes=[
            pltpu.VMEM((HEADS_PER_STEP_C, 1, T), F32),
            pltpu.VMEM((HEADS_PER_STEP_C, 1, T), F32),
            pltpu.VMEM((HEADS_PER_STEP_C, V_DIM, T), F32),
        ],
    )
    o = pl.pallas_call(
        _mla_flash_kernel,
        grid_spec=grid_spec,
        out_shape=jax.ShapeDtypeStruct((batch, seq, DC), BF16),
        compiler_params=_cp(("parallel", "parallel", "arbitrary")),
        name="mla_attention",
    )(qi_tab, ki_tab, q3, k3, vt_all)
    return o.reshape(batch * seq, DC)


def _mixout_kernel(x_ref, gates_ref, ub_ref, ubh_ref, o1_ref, o2_ref, o3_ref, l1_ref, l2_ref, l3_ref, yc_ref,
                   mod1_ref, mod2_ref, g2_ref, poolw_ref, pscale_ref, woa_ref, wob_ref, woc_ref, wout_ref,
                   rwt_ref, sw1_ref, sw3_ref, sw2_ref, spread_ref,
                   xmid_ref, h2a_ref, h2b_ref, logit_ref, *scratch, tiles_per_batch):
    D = x_ref.shape[1]
    tm = x_ref.shape[0]
    tile = pl.program_id(0) % tiles_per_batch
    o_scrs, l_scrs = scratch[:3], scratch[3:]

    def token_order(ref, scr):
        dil, rows, width = ref.shape[1:]
        if dil == 1:
            return ref[0, 0].astype(F32)
        for r in range(dil):
            v = ref[0, r].astype(F32)
            for c in range(width // LANES):
                scr[c, pl.ds(r, rows, stride=dil), :] = v[:, c * LANES:(c + 1) * LANES]
        return jnp.concatenate([scr[c] for c in range(width // LANES)], axis=1)

    outs = [token_order(r, s) for r, s in zip((o1_ref, o2_ref, o3_ref), o_scrs)]
    l1, l2, l3 = [token_order(r, s) for r, s in zip((l1_ref, l2_ref, l3_ref), l_scrs)]
    mx = jnp.maximum(jnp.maximum(l1, l2), l3)
    es = [jnp.exp(l1 - mx), jnp.exp(l2 - mx), jnp.exp(l3 - mx)]
    inv = 1.0 / (es[0] + es[1] + es[2])
    ya = jnp.zeros((tm, GROUP_W), F32)
    for e, o in zip(es, outs):
        w = e * inv
        w_hi = w.astype(BF16)
        w_lo = (w - w_hi.astype(F32)).astype(BF16)
        w_wide = (jnp.dot(w_hi, spread_ref[...], preferred_element_type=F32)
                  + jnp.dot(w_lo, spread_ref[...], preferred_element_type=F32))
        ya = ya + w_wide * o
    a_out = jnp.dot(ya.astype(BF16), woa_ref[...], preferred_element_type=F32)

    u = ub_ref[...].astype(F32)
    halo = jnp.where(tile > 0, ubh_ref[...].astype(F32), 0.0)
    ext = jnp.concatenate([halo, u], axis=0)
    t_seq = tile * tm + lax.broadcasted_iota(jnp.int32, (tm, 1), 0)
    pooled = []
    for gi, w in enumerate(POOL_WINDOWS):
        sl = slice(gi * POOL_GROUP_DIM, (gi + 1) * POOL_GROUP_DIM)
        acc = ext[:, sl]
        k = 1
        while k < w:
            acc = acc + pltpu.roll(acc, k, axis=0)
            k *= 2
        cnt = jnp.minimum(t_seq + 1, w).astype(F32)
        pg = acc[POOL_HALO:] / cnt - u[:, sl]
        pooled.append(jnp.dot(pg.astype(BF16), poolw_ref[gi], preferred_element_type=F32))
    yb = jnp.concatenate(pooled, axis=1) * pscale_ref[...]
    b_out = jnp.dot(yb.astype(BF16), wob_ref[...], preferred_element_type=F32)
    c_out = jnp.dot(yc_ref[...], woc_ref[...], preferred_element_type=F32)

    g = gates_ref[...].astype(F32)
    mix = (jax.nn.sigmoid(g[:, :D]) * a_out + jax.nn.sigmoid(g[:, D:2 * D]) * b_out
           + jax.nn.sigmoid(g[:, 2 * D:]) * c_out)
    tok = jnp.dot(mix.astype(BF16), wout_ref[...], preferred_element_type=F32)
    xn = x_ref[...] + mod1_ref[0][:, 2 * D:] * tok

    mod2 = mod2_ref[0]
    y = xn * lax.rsqrt(jnp.mean(xn * xn, axis=-1, keepdims=True) + EPS) * g2_ref[...]
    h2 = y * (1.0 + mod2[:, D:2 * D]) + mod2[:, :D]
    h2b = h2.astype(BF16)
    h2a_ref[...], h2b_ref[...] = _pack_row_halves(h2b)
    logit_ref[...] = _nt_dot(rwt_ref[...], h2b)
    hid = _silu(jnp.dot(h2b, sw1_ref[...], preferred_element_type=F32)) * jnp.dot(
        h2b, sw3_ref[...], preferred_element_type=F32)
    shared = jnp.dot(hid.astype(BF16), sw2_ref[...], preferred_element_type=F32)
    xmid_ref[...] = xn + mod2[:, 2 * D:] * shared


def mix_out(x2, gu, dil, yc, mod1, mod2, g2, pool_w, pool_scale, w_oa, w_ob, w_oc, w_out, rwt, sw1, sw3, sw2, seq,
            row0=0):
    D = x2.shape[1]
    N = gu.shape[0]
    tm = 512
    tpb = seq // tm
    tile0 = row0 // tm
    (o1, l1), (o2, l2), (o3, l3) = dil
    row = lambda w, c=0: pl.BlockSpec((tm, w), lambda i: (i, c))
    by_residue = lambda a: pl.BlockSpec(
        (1, a.shape[1], tm // a.shape[1], a.shape[3]), lambda i: (i // tpb, 0, i % tpb, 0))
    heads = HEADS_PER_GROUP_A
    spread = (jnp.arange(LSE_LANES)[:, None] == (jnp.arange(GROUP_W)[None, :] // HEAD_DIM_A) * (LSE_LANES // heads)
              ).astype(BF16)
    const2 = lambda a: pl.BlockSpec(a.shape, lambda i: (0,) * a.ndim, pipeline_mode=pl.Buffered(1))
    modspec = pl.BlockSpec((1, 1, 3 * D), lambda i: (i // tpb, 0, 0))
    ub_col = 3 * D // DB
    halo_spec = pl.BlockSpec(
        (POOL_HALO, DB), lambda i: (jnp.maximum(i * (tm // POOL_HALO) - 1, 0), ub_col))
    weights = [g2.reshape(1, D), pool_w, pool_scale.reshape(1, DB), w_oa, w_ob, w_oc, w_out, rwt, sw1, sw3, sw2,
               spread]
    return pl.pallas_call(
        functools.partial(_mixout_kernel, tiles_per_batch=tpb),
        grid=(N // tm,),
        in_specs=[
            pl.BlockSpec((tm, D), lambda i: (i + tile0, 0)), row(3 * D), row(DB, ub_col), halo_spec,
            by_residue(o1), by_residue(o2), by_residue(o3), by_residue(l1), by_residue(l2), by_residue(l3), row(DC),
            modspec, modspec,
        ] + [const2(a) for a in weights],
        scratch_shapes=[pltpu.VMEM((GROUP_W // LANES, tm, LANES), F32)] * 3
        + [pltpu.VMEM((LSE_LANES // LANES, tm, LANES), F32)] * 3,
        out_specs=[row(D), row(PACK_W), row(PACK_W), pl.BlockSpec((N_EXPERTS, tm), lambda i: (0, i))],
        out_shape=[
            jax.ShapeDtypeStruct((N, D), F32),
            jax.ShapeDtypeStruct((N, PACK_W), jnp.int32),
            jax.ShapeDtypeStruct((N, PACK_W), jnp.int32),
            jax.ShapeDtypeStruct((N_EXPERTS, N), F32),
        ],
        compiler_params=_cp(("parallel",), VMEM_LIMIT),
        name="mix_out",
    )(x2, gu, gu, gu, o1, o2, o3, l1, l2, l3, yc, mod1, mod2, *weights)


def _pick_rows(table, picks):
    G, GS = N_GROUPS, GROUP_SIZE
    eio = lax.broadcasted_iota(jnp.int32, (GS, table.shape[1]), 0)
    rows = []
    for k in range(TOP_K):
        idx = picks[k:k + 1]
        parts = [jnp.sum(jnp.where(eio + g * GS == idx, table[g * GS:(g + 1) * GS], 0.0), axis=0, keepdims=True)
                 for g in range(G)]
        rows.append(functools.reduce(jnp.add, parts))
    return jnp.concatenate(rows, axis=0)


def _route_choose(lg_ref, bias_ref):
    G, GS = N_GROUPS, GROUP_SIZE
    scores = jax.nn.sigmoid(lg_ref[...])
    sel = scores + bias_ref[...]
    tn = sel.shape[1]
    eio = lax.broadcasted_iota(jnp.int32, (GS, tn), 0)
    ninf = -jnp.inf

    gs = []
    for g in range(G):
        v = sel[g * GS:(g + 1) * GS]
        m1 = jnp.max(v, axis=0, keepdims=True)
        i1 = jnp.min(jnp.where(v == m1, eio, GS), axis=0, keepdims=True)
        m2 = jnp.max(jnp.where(eio == i1, ninf, v), axis=0, keepdims=True)
        gs.append(m1 + m2)
    gsm = jnp.concatenate(gs, axis=0)
    gio = lax.broadcasted_iota(jnp.int32, (G, tn), 0)
    rank = jnp.zeros((G, tn), jnp.int32)
    for g2 in range(G):
        beats = (gs[g2] > gsm) | ((gs[g2] == gsm) & (g2 < gio))
        rank = rank + beats.astype(jnp.int32)
    gsel = rank < TOPK_GROUPS

    vs = [jnp.where(gsel[g:g + 1], sel[g * GS:(g + 1) * GS], NEG) for g in range(G)]
    eid = [eio + g * GS for g in range(G)]
    chosen = [jnp.zeros((GS, tn), jnp.bool_) for _ in range(G)]
    picks = []
    for _ in range(TOP_K):
        m = functools.reduce(jnp.maximum, [jnp.max(v, axis=0, keepdims=True) for v in vs])
        idx = functools.reduce(jnp.minimum, [
            jnp.min(jnp.where(v == m, e, N_EXPERTS), axis=0, keepdims=True) for v, e in zip(vs, eid)])
        picks.append(idx)
        for g in range(G):
            hit = eid[g] == idx
            chosen[g] = chosen[g] | hit
            vs[g] = jnp.where(hit, ninf, vs[g])
    mask = jnp.concatenate(chosen, axis=0).astype(F32)
    return scores, jnp.concatenate(picks, axis=0), mask


def _route_kernel(lg_ref, bias_ref, tri_ref, dest_ref, w_ref, cnt_ref, run_sc, start_sc, mask_sc, picks_sc,
                  *, slot_block):
    phase = pl.program_id(0)
    step = pl.program_id(1)
    tn = lg_ref.shape[1]
    cols = pl.ds(pl.multiple_of(step * tn, tn), tn)

    @pl.when(phase == 0)
    def _():
        @pl.when(step == 0)
        def _():
            run_sc[...] = jnp.zeros(run_sc.shape, F32)

        scores, picks, mask = _route_choose(lg_ref, bias_ref)
        wk = _pick_rows(scores, picks)
        w_ref[0] = wk / jnp.sum(wk, axis=0, keepdims=True) * ROUTED_SCALE
        dest_ref[0] = jnp.zeros(dest_ref.shape[1:], dest_ref.dtype)
        mask_sc[:, cols] = mask.astype(BF16)
        picks_sc[:, cols] = picks
        run_sc[...] = run_sc[...] + jnp.sum(mask, axis=1, keepdims=True)

    @pl.when(phase == 1)
    def _():
        @pl.when(step == 0)
        def _():
            counts = run_sc[...].astype(jnp.int32)
            cnt_ref[...] = jnp.broadcast_to(counts, cnt_ref.shape)
            shift = slot_block.bit_length() - 1
            padded = lax.shift_left(lax.shift_right_logical(counts + (slot_block - 1), shift), shift).astype(F32)
            r = lax.broadcasted_iota(jnp.int32, (N_EXPERTS, N_EXPERTS), 0)
            c = lax.broadcasted_iota(jnp.int32, (N_EXPERTS, N_EXPERTS), 1)
            as_row = jnp.sum(jnp.where(r == c, padded, 0.0), axis=0, keepdims=True)
            start_sc[...] = jnp.sum(jnp.where(c < r, as_row, 0.0), axis=1, keepdims=True)
            run_sc[...] = jnp.zeros(run_sc.shape, F32)

        mask_b = mask_sc[:, cols]
        mask = mask_b.astype(F32)
        before = jnp.dot(mask_b, tri_ref[...], preferred_element_type=F32) - mask
        slot = start_sc[...] + run_sc[...] + before
        dest_ref[0] = _pick_rows(slot, picks_sc[:, cols]).astype(jnp.int32)
        w_ref[0] = jnp.zeros(w_ref.shape[1:], w_ref.dtype)
        run_sc[...] = run_sc[...] + jnp.sum(mask, axis=1, keepdims=True)


SLOT_BLOCK = 512


def route(logits_t, bias):
    E, N = logits_t.shape
    tn = 1024
    tri = (jnp.arange(tn)[:, None] <= jnp.arange(tn)[None, :]).astype(BF16)
    plane = lambda: pl.BlockSpec((1, TOP_K, tn), lambda p, i: (p, 0, i))
    dest, w, cnt = pl.pallas_call(
        functools.partial(_route_kernel, slot_block=SLOT_BLOCK),
        grid=(2, N // tn),
        in_specs=[
            pl.BlockSpec((E, tn), lambda p, i: (0, i * (1 - p))),
            pl.BlockSpec((E, 1), lambda p, i: (0, 0)),
            pl.BlockSpec((tn, tn), lambda p, i: (0, 0)),
        ],
        out_specs=[plane(), plane(), pl.BlockSpec((E, 128), lambda p, i: (0, 0))],
        out_shape=[
            jax.ShapeDtypeStruct((2, TOP_K, N), jnp.int32),
            jax.ShapeDtypeStruct((2, TOP_K, N), F32),
            jax.ShapeDtypeStruct((E, 128), jnp.int32),
        ],
        scratch_shapes=[pltpu.VMEM((E, 1), F32), pltpu.VMEM((E, 1), F32),
                        pltpu.VMEM((E, N), BF16), pltpu.VMEM((TOP_K, N), jnp.int32)],
        compiler_params=_cp(("arbitrary", "arbitrary")),
        name="route",
    )(logits_t, bias.reshape(E, 1), tri)
    return dest[1], w[0], cnt[:, 0]


def block_tables(counts, n_tokens):
    E = counts.shape[0]
    blk = SLOT_BLOCK
    nblk = (n_tokens * TOP_K + E * blk) // blk
    per_expert = (counts + blk - 1) // blk
    bend = jnp.cumsum(per_expert)
    bstart = bend - per_expert
    b = jnp.arange(nblk, dtype=jnp.int32)[:, None]
    owns = (bstart[None, :] <= b) & (b < bend[None, :])
    blk_e = jnp.minimum(jnp.sum(bend[None, :] <= b, axis=1), E - 1).astype(jnp.int32)
    rows_left = counts[None, :] - (b - bstart[None, :]) * blk
    nvalid = jnp.sum(jnp.where(owns, jnp.clip(rows_left, 0, blk), 0), axis=1)
    return blk_e, nvalid.astype(jnp.int32)


def _sc_mesh():
    return plsc.VectorSubcoreMesh(core_axis_name="c", subcore_axis_name="s")


SC_WINDOW = 128


def sc_scatter_rows(x, dest, n_slots):
    N, W = x.shape
    K = dest.shape[0]

    @functools.partial(pl.kernel, out_type=jax.ShapeDtypeStruct((n_slots, W), x.dtype), mesh=_sc_mesh(),
                       scratch_types=[])
    def scatter(x_hbm, i_hbm, o_hbm):
        def body(x_vmem, i_vmem):
            for k in range(K):
                pltpu.sync_copy(x_vmem, o_hbm.at[i_vmem.at[k]])

        pltpu.emit_pipeline(
            body,
            grid=(N // SC_WINDOW,),
            in_specs=[pl.BlockSpec((SC_WINDOW, W), lambda i: (i, 0)),
                      pl.BlockSpec((K, SC_WINDOW), lambda i: (0, i))],
            out_specs=[],
            core_axis_name=("c", "s"),
            dimension_semantics=(pltpu.PARALLEL,),
        )(x_hbm, i_hbm)

    return scatter(x, dest)


SC_LANES = 16
SC_GATHER_TOKENS = 8


def sc_weighted_gather(y, dest, wts):
    W = y.shape[1]
    K, N = dest.shape
    G, L = SC_GATHER_TOKENS, SC_LANES
    batches = SC_WINDOW // G

    @functools.partial(
        pl.kernel, out_type=jax.ShapeDtypeStruct((N, W), y.dtype), mesh=_sc_mesh(),
        scratch_types=[pltpu.VMEM((2, K, G, W), y.dtype), pltpu.SemaphoreType.DMA((2,))],
        compiler_params=pltpu.CompilerParams(needs_layout_passes=False))
    def gather(y_hbm, i_hbm, w_hbm, o_hbm, rows2, sems):
        def body(i_vmem, w_vmem, o_vmem):
            def fetch(batch, slot):
                return [pltpu.make_async_copy(y_hbm.at[i_vmem.at[k, pl.ds(batch * G, G)]], rows2.at[slot, k],
                                              sems.at[slot]) for k in range(K)]

            for c in fetch(0, 0):
                c.start()

            @pl.loop(0, batches)
            def _(batch):
                slot = batch % 2

                @pl.when(batch + 1 < batches)
                def _():
                    for c in fetch(batch + 1, 1 - slot):
                        c.start()

                for c in fetch(batch, slot):
                    c.wait()
                rows = rows2.at[slot]

                @pl.loop(0, G)
                def _(t):
                    tok = jnp.full((L,), batch * G + t, jnp.int32)
                    wk = [plsc.load_gather(w_vmem, [jnp.full((L,), k, jnp.int32), tok]) for k in range(K)]

                    @plsc.parallel_loop(0, W // L, unroll=4)
                    def _(j):
                        lo = jnp.zeros((L,), F32)
                        hi = jnp.zeros((L,), F32)
                        for k in range(K):
                            pair = plsc.bitcast(rows[k, t, pl.ds(j * L, L)], BF16)
                            a, b = plsc.unpack(pair, format=plsc.PackFormat.INTERLEAVED)
                            lo = lo + wk[k] * a
                            hi = hi + wk[k] * b
                        o_vmem[batch * G + t, pl.ds(j * L, L)] = plsc.bitcast(
                            plsc.pack(lo, hi, format=plsc.PackFormat.INTERLEAVED), y.dtype)

        pltpu.emit_pipeline(
            body,
            grid=(N // SC_WINDOW,),
            in_specs=[pl.BlockSpec((K, SC_WINDOW), lambda i: (0, i)),
                      pl.BlockSpec((K, SC_WINDOW), lambda i: (0, i))],
            out_specs=[pl.BlockSpec((SC_WINDOW, W), lambda i: (i, 0))],
            core_axis_name=("c", "s"),
            dimension_semantics=(pltpu.PARALLEL,),
        )(i_hbm, w_hbm, o_hbm)

    return gather(y, dest, wts)


EXPERT_INPUT_SLOTS = 3


def _expert_kernel(blk_e_ref, nvalid_ref, xa_hbm, xb_hbm, w1_ref, w3_ref, w2_ref, ya_ref, yb_ref,
                   w1_sc, w3_sc, w2_sc, xa_buf, xb_buf, sems):
    b = pl.program_id(0)
    nb = pl.num_programs(0)
    nv = nvalid_ref[b]
    prev_e = blk_e_ref[jnp.maximum(b - 1, 0)]
    blk = xa_buf.shape[1]
    ring = EXPERT_INPUT_SLOTS

    def fetch(block, slot):
        rows = pl.ds(pl.multiple_of(block * blk, blk), blk)
        return (pltpu.make_async_copy(xa_hbm.at[rows], xa_buf.at[slot], sems.at[slot, 0]),
                pltpu.make_async_copy(xb_hbm.at[rows], xb_buf.at[slot], sems.at[slot, 1]))

    @pl.when(b == 0)
    def _():
        for i in range(ring - 1):
            for c in fetch(i, i):
                c.start()

    ahead = b + (ring - 1)

    @pl.when(ahead < nb)
    def _():
        for c in fetch(ahead, ahead % ring):
            c.start()

    slot = b % ring
    for c in fetch(b, slot):
        c.wait()

    @pl.when((b == 0) | (blk_e_ref[b] != prev_e))
    def _():
        w1_sc[...] = w1_ref[0, 0].astype(BF16)
        w3_sc[...] = w3_ref[0, 0].astype(BF16)
        w2_sc[...] = w2_ref[0, 0].astype(BF16)

    @pl.when(nv > 0)
    def _():
        x = _unpack_row_halves(xa_buf[slot], xb_buf[slot])
        rows = lax.broadcasted_iota(jnp.int32, x.shape, 0)
        x = jnp.where(rows < nv, x, 0.0).astype(BF16)
        hid = _silu(jnp.dot(x, w1_sc[...], preferred_element_type=F32)) * jnp.dot(
            x, w3_sc[...], preferred_element_type=F32)
        y = jnp.dot(hid.astype(BF16), w2_sc[...], preferred_element_type=F32)
        ya_ref[...], yb_ref[...] = _pack_row_halves(y)

    @pl.when(nv == 0)
    def _():
        ya_ref[...] = jnp.zeros(ya_ref.shape, ya_ref.dtype)
        yb_ref[...] = jnp.zeros(yb_ref.shape, yb_ref.dtype)


def routed_experts(xa, xb, blk_e, nvalid, w1, w3, w2, layer):
    P = xa.shape[0]
    blk = SLOT_BLOCK
    _, E, D, FF = w1.shape
    slots = lambda: pl.BlockSpec((blk, PACK_W), lambda b, be, nv: (b, 0))
    grid_spec = pltpu.PrefetchScalarGridSpec(
        num_scalar_prefetch=2,
        grid=(P // blk,),
        in_specs=[
            pl.BlockSpec(memory_space=pl.ANY), pl.BlockSpec(memory_space=pl.ANY),
            pl.BlockSpec((1, 1, D, FF), lambda b, be, nv: (layer, be[b], 0, 0)),
            pl.BlockSpec((1, 1, D, FF), lambda b, be, nv: (layer, be[b], 0, 0)),
            pl.BlockSpec((1, 1, FF, D), lambda b, be, nv: (layer, be[b], 0, 0)),
        ],
        out_specs=[slots(), slots()],
        scratch_shapes=[
            pltpu.VMEM((D, FF), BF16), pltpu.VMEM((D, FF), BF16), pltpu.VMEM((FF, D), BF16),
            pltpu.VMEM((EXPERT_INPUT_SLOTS, blk, PACK_W), jnp.int32),
            pltpu.VMEM((EXPERT_INPUT_SLOTS, blk, PACK_W), jnp.int32),
            pltpu.SemaphoreType.DMA((EXPERT_INPUT_SLOTS, 2)),
        ],
    )
    return pl.pallas_call(
        _expert_kernel,
        grid_spec=grid_spec,
        out_shape=[jax.ShapeDtypeStruct((P, PACK_W), jnp.int32)] * 2,
        compiler_params=_cp(("arbitrary",), VMEM_LIMIT),
        name="routed_experts",
    )(blk_e, nvalid, xa, xb, w1, w3, w2)


def _combine_kernel(xmid_ref, ra_ref, rb_ref, mod2_ref, fg_ref, *rest, final):
    out_ref = rest[-1]
    D = xmid_ref.shape[1]
    x = xmid_ref[...] + mod2_ref[0][:, 2 * D:] * _unpack_row_halves(ra_ref[...], rb_ref[...])
    if final:
        x = x * lax.rsqrt(jnp.mean(x * x, axis=-1, keepdims=True) + EPS) * fg_ref[...]
    out_ref[...] = x


def combine(xmid, ra, rb, mod2, final_g, seq, final, out_rows=None, row0=0, out_buf=None):
    N, D = xmid.shape
    tm = 512
    tpb = seq // tm
    tile0 = row0 // tm
    in_specs = [
        pl.BlockSpec((tm, D), lambda i: (i, 0)),
        pl.BlockSpec((tm, PACK_W), lambda i: (i, 0)),
        pl.BlockSpec((tm, PACK_W), lambda i: (i, 0)),
        pl.BlockSpec((1, 1, 3 * D), lambda i: (i // tpb, 0, 0)),
        pl.BlockSpec((1, D), lambda i: (0, 0)),
    ]
    args = [xmid, ra, rb, mod2, final_g.reshape(1, D)]
    aliases = {}
    if out_buf is not None:
        in_specs.append(pl.BlockSpec(memory_space=pl.ANY))
        args.append(out_buf)
        aliases = {len(args) - 1: 0}
    return pl.pallas_call(
        functools.partial(_combine_kernel, final=final),
        grid=(N // tm,),
        in_specs=in_specs,
        out_specs=pl.BlockSpec((tm, D), lambda i: (i + tile0, 0)),
        out_shape=jax.ShapeDtypeStruct((out_rows or N, D), F32),
        input_output_aliases=aliases,
        compiler_params=_cp(("parallel",), VMEM_LIMIT),
        name="combine",
    )(*args)


TOKEN_STREAMS = 2


def _permute_w_in(w):
    ub = w[:, 3 * DA:3 * DA + DB]
    lat_lo = 3 * DA + DB
    lat_hi = lat_lo + Q_LORA + KV_LORA + QK_ROPE
    lat, gates = w[:, lat_lo:lat_hi], w[:, lat_hi:]
    pad = jnp.zeros((w.shape[0], LAT_W - (lat_hi - lat_lo)), w.dtype)
    parts = [gates, ub, lat, pad]
    for g in range(len(DIL_GROUPS)):
        sl = slice(g * GROUP_W, (g + 1) * GROUP_W)
        parts += [w[:, :DA][:, sl] * (HEAD_DIM_A ** -0.5), w[:, DA:2 * DA][:, sl], w[:, 2 * DA:3 * DA][:, sl]]
    return jnp.concatenate(parts, axis=1).astype(BF16)


def kernel(x, c, positions, ada_mix_w, ada_mix_b, norm_mix_g, w_in, pool_w, pool_scale, cq_norm_g, ckv_norm_g, w_uq, w_ukv, w_oa, w_ob, w_oc, w_out, ada_ffn_w, ada_ffn_b, norm_ffn_g, router_w, router_bias, exp_w1, exp_w3, exp_w2, sh_w1, sh_w3, sh_w2, final_g):
    B, S, D = x.shape
    depth = w_in.shape[0]
    mod_mix = adaln_rows(c, ada_mix_w, ada_mix_b)
    mod_ffn = adaln_rows(c, ada_ffn_w, ada_ffn_b)
    streams = TOKEN_STREAMS if B % TOKEN_STREAMS == 0 else 1
    Bs = B // streams
    Ns = Bs * S
    x_all = x.reshape(B * S, D)
    xs = [None] * streams
    out_all = None
    pos_s = [positions[s * Bs:(s + 1) * Bs] for s in range(streams)]
    for l in range(depth):
        last = l == depth - 1
        w_in_l = _permute_w_in(w_in[l])
        mla_w = _mla_weights(cq_norm_g[l], ckv_norm_g[l], w_uq[l], w_ukv[l])
        mix_w = (norm_ffn_g[l], pool_w[l].astype(BF16), pool_scale[l],
                 w_oa[l].astype(BF16), w_ob[l].astype(BF16), w_oc[l].astype(BF16), w_out[l].astype(BF16),
                 router_w[l].T.astype(BF16), sh_w1[l].astype(BF16), sh_w3[l].astype(BF16), sh_w2[l].astype(BF16))
        for s in range(streams):
            x2, row0 = (x_all, s * Ns) if l == 0 else (xs[s], 0)
            mod1 = mod_mix[l, s * Bs:(s + 1) * Bs].reshape(Bs, 1, 3 * D)
            mod2 = mod_ffn[l, s * Bs:(s + 1) * Bs].reshape(Bs, 1, 3 * D)
            gu, lat, *qkv = in_projection(x2, norm_mix_g[l], mod1, w_in_l, S, row0)
            dil = [dilated_attention(qkv[2 * g], qkv[2 * g + 1]) for g in range(len(DIL_GROUPS))]
            q_all, k_all, vt_all = mla_prep(lat, pos_s[s], *mla_w, Bs, S)
            yc = mla_attention(q_all, k_all, vt_all, Bs, S)
            xmid, h2a, h2b, logits_t = mix_out(x2, gu, dil, yc, mod1, mod2, *mix_w, S, row0)
            dest, w_k, counts = route(logits_t, router_bias[l])
            blk_e, nvalid = block_tables(counts, Ns)
            n_slots = blk_e.shape[0] * SLOT_BLOCK
            xa = sc_scatter_rows(h2a, dest, n_slots)
            xb = sc_scatter_rows(h2b, dest, n_slots)
            ya, yb = routed_experts(xa, xb, blk_e, nvalid, exp_w1, exp_w3, exp_w2, l)
            ra = sc_weighted_gather(ya, dest, w_k)
            rb = sc_weighted_gather(yb, dest, w_k)
            if last:
                out_all = combine(xmid, ra, rb, mod2, final_g, S, True, B * S, s * Ns, out_all)
            else:
                xs[s] = combine(xmid, ra, rb, mod2, final_g, S, False)
    return out_all.reshape(B, S, D)
```

```python
import functools
import math

import jax
import jax.numpy as jnp
from jax import lax
from jax.experimental import pallas as pl
from jax.experimental.pallas import tpu as pltpu
from jax.experimental.pallas import tpu_sc as plsc

F32 = jnp.float32
BF16 = jnp.bfloat16
HIGHEST = lax.Precision.HIGHEST

D_MODEL = 1024
HEAD_DIM_A = 64
HEADS_PER_GROUP_A = 4
DIL_GROUPS = ((128, 1), (512, 4), (2048, 16))
GROUP_W = HEADS_PER_GROUP_A * HEAD_DIM_A
DA = GROUP_W * len(DIL_GROUPS)
POOL_WINDOWS = (2, 4, 8, 16)
POOL_GROUP_DIM = 128
DB = POOL_GROUP_DIM * len(POOL_WINDOWS)
POOL_HALO = 16
N_HEADS_C = 8
QK_NOPE = 64
QK_ROPE = 32
V_DIM = 64
Q_LORA = 384
KV_LORA = 256
DC = N_HEADS_C * V_DIM
HEAD_PAD_C = 128
ROPE_THETA = 10000.0
N_EXPERTS = 64
TOP_K = 8
N_GROUPS = 8
TOPK_GROUPS = 4
GROUP_SIZE = N_EXPERTS // N_GROUPS
EXPERT_FF = 256
ROUTED_SCALE = 2.5
EPS = 1e-6
NEG = -1e30
Q_BLOCK = 128

LAT_W = 768
GU_W = 3 * D_MODEL + DB
IN_OUT_WIDTHS = (GU_W, LAT_W) + (2 * GROUP_W, GROUP_W) * len(DIL_GROUPS)

VMEM_LIMIT = 56 * 1024 * 1024


def _cp(sem, vmem=None):
    return pltpu.CompilerParams(dimension_semantics=sem, vmem_limit_bytes=vmem)


def _silu(v):
    return v * jax.nn.sigmoid(v)


def _nt_dot(a, b):
    return lax.dot_general(a, b, (((1,), (1,)), ((), ())), preferred_element_type=F32)


PACK_W = D_MODEL // 4
_HI_MASK = -65536


def _bf16_bits(v):
    return lax.bitcast_convert_type(v.astype(BF16).astype(F32), jnp.int32)


def _pack_row_halves(v):
    halves = []
    for h in range(2):
        lo = _bf16_bits(v[:, (2 * h) * PACK_W:(2 * h + 1) * PACK_W])
        hi = _bf16_bits(v[:, (2 * h + 1) * PACK_W:(2 * h + 2) * PACK_W])
        halves.append(lax.shift_right_logical(lo, 16) | (hi & _HI_MASK))
    return halves


def _unpack_row_halves(wa, wb):
    parts = []
    for w in (wa, wb):
        parts.append(lax.bitcast_convert_type(lax.shift_left(w, 16), F32))
        parts.append(lax.bitcast_convert_type(w & _HI_MASK, F32))
    return jnp.concatenate(parts, axis=1)


def _adaln_kernel(c_ref, w_ref, b_ref, o_ref):
    s = _silu(c_ref[...])
    o_ref[0] = jnp.dot(s, w_ref[0], preferred_element_type=F32, precision=HIGHEST) + b_ref[0]


def adaln_rows(c, w, b):
    L, D, D3 = w.shape
    B = c.shape[0]
    tn = 1024
    return pl.pallas_call(
        _adaln_kernel,
        grid=(L, D3 // tn),
        in_specs=[
            pl.BlockSpec((B, D), lambda l, j: (0, 0)),
            pl.BlockSpec((1, D, tn), lambda l, j: (l, 0, j)),
            pl.BlockSpec((1, 1, tn), lambda l, j: (l, 0, j)),
        ],
        out_specs=pl.BlockSpec((1, B, tn), lambda l, j: (l, 0, j)),
        out_shape=jax.ShapeDtypeStruct((L, B, D3), F32),
        compiler_params=_cp(("parallel", "parallel")),
        name="adaln_rows",
    )(c, w, b.reshape(L, 1, D3))


LANES = 128


def _inproj_kernel(x_ref, g_ref, mod_ref, w_ref, *refs, chunk):
    o_refs, scr = refs[:-1], refs[-1]
    D = x_ref.shape[1]
    x = x_ref[...]
    y = x * lax.rsqrt(jnp.mean(x * x, axis=-1, keepdims=True) + EPS) * g_ref[...]
    mod = mod_ref[0]
    h = (y * (1.0 + mod[:, D:2 * D]) + mod[:, :D]).astype(BF16)
    col = 0
    for o_ref in o_refs:
        width = o_ref.shape[-1]
        if o_ref.ndim == 2:
            for c0 in range(0, width, chunk):
                cw = min(chunk, width - c0)
                o_ref[:, c0:c0 + cw] = jnp.dot(
                    h, w_ref[:, col + c0:col + c0 + cw], preferred_element_type=F32).astype(o_ref.dtype)
        else:
            dil, rows = o_ref.shape[1], o_ref.shape[2]
            z = jnp.dot(h, w_ref[:, col:col + width], preferred_element_type=F32)
            if dil == 1:
                o_ref[0, 0] = z.astype(o_ref.dtype)
            else:
                for c in range(width // LANES):
                    scr[c] = z[:, c * LANES:(c + 1) * LANES]
                for r in range(dil):
                    o_ref[0, r] = jnp.concatenate(
                        [scr[c, pl.ds(r, rows, stride=dil), :] for c in range(width // LANES)],
                        axis=1).astype(o_ref.dtype)
        col += width


def in_projection(x2, g, mod, w, seq, row0=0):
    D = x2.shape[1]
    B = mod.shape[0]
    N = B * seq
    tm = 512
    tpb = seq // tm
    tile0 = row0 // tm
    out_specs = [pl.BlockSpec((tm, wd), lambda i: (i, 0)) for wd in IN_OUT_WIDTHS[:2]]
    out_shape = [jax.ShapeDtypeStruct((N, wd), BF16) for wd in IN_OUT_WIDTHS[:2]]
    for grp, (_, dil) in enumerate(DIL_GROUPS):
        for wd in IN_OUT_WIDTHS[2 + 2 * grp:4 + 2 * grp]:
            out_specs.append(pl.BlockSpec((1, dil, tm // dil, wd), lambda i: (i // tpb, 0, i % tpb, 0)))
            out_shape.append(jax.ShapeDtypeStruct((B, dil, seq // dil, wd), BF16))
    return pl.pallas_call(
        functools.partial(_inproj_kernel, chunk=512),
        grid=(N // tm,),
        in_specs=[
            pl.BlockSpec((tm, D), lambda i: (i + tile0, 0)),
            pl.BlockSpec((1, D), lambda i: (0, 0)),
            pl.BlockSpec((1, 1, 3 * D), lambda i: (i // tpb, 0, 0)),
            pl.BlockSpec(w.shape, lambda i: (0, 0), pipeline_mode=pl.Buffered(1)),
        ],
        out_specs=out_specs,
        out_shape=out_shape,
        scratch_shapes=[pltpu.VMEM((max(IN_OUT_WIDTHS[2:]) // LANES, tm, LANES), F32)],
        compiler_params=_cp(("parallel",), VMEM_LIMIT),
        name="in_projection",
    )(x2, g.reshape(1, D), mod, w)


def _dilated_kernel(q_ref, kc_ref, kp_ref, vc_ref, vp_ref, o_ref, lse_ref):
    i = pl.program_id(1)
    T = Q_BLOCK
    key = lax.broadcasted_iota(jnp.int32, (T, T), 0)
    qry = lax.broadcasted_iota(jnp.int32, (T, T), 1)
    valid_c = key <= qry
    near = key >= qry
    seqs, run = q_ref.shape[0], q_ref.shape[1] // T
    heads = [slice(h * HEAD_DIM_A, (h + 1) * HEAD_DIM_A) for h in range(HEADS_PER_GROUP_A)]

    def transposed(v):
        return v.astype(F32).T.astype(BF16)

    vts = {(s, j): transposed(vc_ref[s, j * T:(j + 1) * T, :]) for s in range(seqs) for j in range(run)}
    vt_before = [transposed(vp_ref[s]) for s in range(seqs)]

    def blocks(s, j):
        rows = slice(j * T, (j + 1) * T)
        if j == 0:
            return rows, kc_ref[s, rows, :], vts[s, 0], kp_ref[s], vt_before[s], near & (i > 0)
        before = slice((j - 1) * T, j * T)
        return rows, kc_ref[s, rows, :], vts[s, j], kc_ref[s, before, :], vts[s, j - 1], near

    scores, probs = {}, {}
    for s in range(seqs):
        for j in range(run):
            rows, kc, _, kp, _, valid_p = blocks(s, j)
            q = q_ref[s, rows, :]
            for h, sl in enumerate(heads):
                qh = q[:, sl]
                scores[s, j, h] = (jnp.where(valid_c, _nt_dot(kc[:, sl], qh), NEG),
                                   jnp.where(valid_p, _nt_dot(kp[:, sl], qh), NEG))
    for chain, (sc, sp) in scores.items():
        m = jnp.maximum(jnp.max(sc, axis=0, keepdims=True), jnp.max(sp, axis=0, keepdims=True))
        pc = jnp.exp(sc - m)
        pp = jnp.exp(sp - m)
        den = jnp.sum(pc, axis=0, keepdims=True) + jnp.sum(pp, axis=0, keepdims=True)
        probs[chain] = (pc.astype(BF16), pp.astype(BF16), den, m + jnp.log(den))
    spread = LSE_LANES // len(heads)
    for s in range(seqs):
        for j in range(run):
            rows, _, vtc, _, vtp, _ = blocks(s, j)
            outs = []
            for h, sl in enumerate(heads):
                pc, pp, den, _ = probs[s, j, h]
                o = (jnp.dot(vtc[sl, :], pc, preferred_element_type=F32)
                     + jnp.dot(vtp[sl, :], pp, preferred_element_type=F32))
                outs.append(o / den)
            o_ref[s, rows, :] = jnp.concatenate(outs, axis=0).T.astype(o_ref.dtype)
            lse_t = jnp.concatenate(
                [jnp.broadcast_to(probs[s, j, h][3], (spread, T)) for h in range(len(heads))], axis=0)
            lse_ref[s, rows, :] = lse_t.T


DILATED_RUN = 8


LSE_LANES = 128


def dilated_attention(qk, v):
    batch, dilation, L, _ = qk.shape
    nb = L // Q_BLOCK
    run = min(DILATED_RUN, nb)
    seqs = DILATED_RUN // run
    qk_r = qk.reshape(batch * dilation, L, 2 * GROUP_W)
    v_r = v.reshape(batch * dilation, L, GROUP_W)
    before = lambda i: jnp.maximum(i * run - 1, 0)
    o, lse = pl.pallas_call(
        _dilated_kernel,
        grid=(batch * dilation // seqs, nb // run),
        in_specs=[
            pl.BlockSpec((seqs, run * Q_BLOCK, GROUP_W), lambda s, i: (s, i, 0)),
            pl.BlockSpec((seqs, run * Q_BLOCK, GROUP_W), lambda s, i: (s, i, 1)),
            pl.BlockSpec((seqs, Q_BLOCK, GROUP_W), lambda s, i: (s, before(i), 1)),
            pl.BlockSpec((seqs, run * Q_BLOCK, GROUP_W), lambda s, i: (s, i, 0)),
            pl.BlockSpec((seqs, Q_BLOCK, GROUP_W), lambda s, i: (s, before(i), 0)),
        ],
        out_specs=[
            pl.BlockSpec((seqs, run * Q_BLOCK, GROUP_W), lambda s, i: (s, i, 0)),
            pl.BlockSpec((seqs, run * Q_BLOCK, LSE_LANES), lambda s, i: (s, i, 0)),
        ],
        out_shape=[
            jax.ShapeDtypeStruct((batch * dilation, L, GROUP_W), BF16),
            jax.ShapeDtypeStruct((batch * dilation, L, LSE_LANES), F32),
        ],
        compiler_params=_cp(("parallel", "parallel")),
        name=f"dilated_attention_d{dilation}",
    )(qk_r, qk_r, qk_r, v_r, v_r)
    return o.reshape(batch, dilation, L, GROUP_W), lse.reshape(batch, dilation, L, LSE_LANES)


def _mla_prep_kernel(lat_ref, pos_ref, gq_ref, gkv_ref, wq_ref, wk_ref, wvt_ref, freq_ref, spread_ref, one_ref,
                     q_ref, k_ref, vt_ref):
    HP = N_HEADS_C * HEAD_PAD_C
    lat = lat_ref[...].astype(F32)
    cq = lat[:, :Q_LORA]
    ckr = lat[:, Q_LORA:]
    zq = (cq * lax.rsqrt(jnp.mean(cq * cq, axis=-1, keepdims=True) + EPS) * gq_ref[...]).astype(BF16)
    lane = lax.broadcasted_iota(jnp.int32, ckr.shape, 1)
    is_kv = lane < KV_LORA
    ms = jnp.sum(jnp.where(is_kv, ckr * ckr, 0.0), axis=-1, keepdims=True) * (1.0 / KV_LORA)
    zkv = (ckr * jnp.where(is_kv, lax.rsqrt(ms + EPS) * gkv_ref[...], 1.0)).astype(BF16)
    qq = jnp.dot(zq, wq_ref[...], preferred_element_type=F32)
    kk = jnp.dot(zkv, wk_ref[...], preferred_element_type=F32)
    ang_t = freq_ref[...] * pos_ref[0].astype(F32)

    def to_lanes(t):
        hi = t.astype(BF16)
        lo = (t - hi.astype(F32)).astype(BF16)
        tn_dot = lambda a: lax.dot_general(a, spread_ref[...], (((0,), (0,)), ((), ())), preferred_element_type=F32)
        return tn_dot(hi) + tn_dot(lo)

    cos = to_lanes(jnp.cos(ang_t)) + one_ref[...]
    sin = to_lanes(jnp.sin(ang_t))
    for h in range(N_HEADS_C):
        lo, hi = h * HEAD_PAD_C, (h + 1) * HEAD_PAD_C
        q_ref[:, lo:hi] = (qq[:, lo:hi] * cos + qq[:, HP + lo:HP + hi] * sin).astype(q_ref.dtype)
        k_ref[:, lo:hi] = (kk[:, lo:hi] * cos + kk[:, HP + lo:HP + hi] * sin).astype(k_ref.dtype)
    vt_ref[0] = _nt_dot(wvt_ref[...], zkv).astype(vt_ref.dtype)


def _mla_weights(cq_g, ckv_g, w_uq, w_ukv):
    H, HPAD, half = N_HEADS_C, HEAD_PAD_C, QK_ROPE // 2
    scale = (QK_NOPE + QK_ROPE) ** -0.5 * math.log2(math.e)
    wq = w_uq.reshape(Q_LORA, H, QK_NOPE + QK_ROPE) * scale
    q_lin = jnp.pad(wq, ((0, 0), (0, 0), (0, HPAD - QK_NOPE - QK_ROPE)))
    r1, r2 = wq[..., QK_NOPE:QK_NOPE + half], wq[..., QK_NOPE + half:]
    q_sw = jnp.concatenate([jnp.zeros((Q_LORA, H, QK_NOPE), F32), -r2, r1,
                            jnp.zeros((Q_LORA, H, HPAD - QK_NOPE - QK_ROPE), F32)], axis=-1)
    wq_big = jnp.concatenate([q_lin.reshape(Q_LORA, H * HPAD), q_sw.reshape(Q_LORA, H * HPAD)], axis=1)

    rows = LAT_W - Q_LORA
    wkv = w_ukv.reshape(KV_LORA, H, QK_NOPE + V_DIM)
    eye = jnp.eye(QK_ROPE, dtype=F32)
    k_lin = jnp.zeros((rows, H, HPAD), F32)
    k_lin = k_lin.at[:KV_LORA, :, :QK_NOPE].set(wkv[..., :QK_NOPE])
    k_lin = k_lin.at[KV_LORA:KV_LORA + QK_ROPE, :, QK_NOPE:QK_NOPE + QK_ROPE].set(
        jnp.broadcast_to(eye[:, None, :], (QK_ROPE, H, QK_ROPE)))
    swap = jnp.zeros((QK_ROPE, QK_ROPE), F32).at[half:, :half].set(-jnp.eye(half)).at[:half, half:].set(jnp.eye(half))
    k_sw = jnp.zeros((rows, H, HPAD), F32)
    k_sw = k_sw.at[KV_LORA:KV_LORA + QK_ROPE, :, QK_NOPE:QK_NOPE + QK_ROPE].set(
        jnp.broadcast_to(swap[:, None, :], (QK_ROPE, H, QK_ROPE)))
    v_w = jnp.zeros((rows, H, V_DIM), F32).at[:KV_LORA].set(wkv[..., QK_NOPE:])
    wk_big = jnp.concatenate([k_lin.reshape(rows, H * HPAD), k_sw.reshape(rows, H * HPAD)], axis=1)
    wv_t = v_w.reshape(rows, H * V_DIM).T

    gkv = jnp.concatenate([ckv_g, jnp.ones((rows - KV_LORA,), F32)]).reshape(1, rows)
    return cq_g.reshape(1, Q_LORA), gkv, wq_big.astype(BF16), wk_big.astype(BF16), wv_t.astype(BF16)


def _rope_tables():
    half = QK_ROPE // 2
    freqs = (ROPE_THETA ** (-jnp.arange(0, QK_ROPE, 2, dtype=F32) / QK_ROPE)).reshape(half, 1)
    lane = jnp.arange(HEAD_PAD_C)[None, :]
    j = jnp.arange(half)[:, None]
    spread = (lane == QK_NOPE + j) | (lane == QK_NOPE + half + j)
    off_rope = ~jnp.any(spread, axis=0, keepdims=True)
    return freqs, spread.astype(BF16), off_rope.astype(F32)


def mla_prep(lat, positions, gq, gkv, wq_big, wk_big, wv_t, batch, seq):
    N = lat.shape[0]
    HP = N_HEADS_C * HEAD_PAD_C
    tm = 512
    tpb = seq // tm
    freqs, spread, off_rope = _rope_tables()
    pos_rows = positions.reshape(N // tm, 1, tm)
    const = lambda shape: pl.BlockSpec(shape, lambda i: (0, 0))
    return pl.pallas_call(
        _mla_prep_kernel,
        grid=(N // tm,),
        in_specs=[
            pl.BlockSpec((tm, LAT_W), lambda i: (i, 0)),
            pl.BlockSpec((1, 1, tm), lambda i: (i, 0, 0)),
            const(gq.shape), const(gkv.shape), const(wq_big.shape), const(wk_big.shape), const(wv_t.shape),
            const(freqs.shape), const(spread.shape), const(off_rope.shape),
        ],
        out_specs=[
            pl.BlockSpec((tm, HP), lambda i: (i, 0)),
            pl.BlockSpec((tm, HP), lambda i: (i, 0)),
            pl.BlockSpec((1, DC, tm), lambda i: (i // tpb, 0, i % tpb)),
        ],
        out_shape=[
            jax.ShapeDtypeStruct((N, HP), BF16),
            jax.ShapeDtypeStruct((N, HP), BF16),
            jax.ShapeDtypeStruct((batch, DC, seq), BF16),
        ],
        compiler_params=_cp(("parallel",), VMEM_LIMIT),
        name="mla_prep",
    )(lat, pos_rows, gq, gkv, wq_big, wk_big, wv_t, freqs, spread, off_rope)


HEADS_PER_STEP_C = 8
FLASH_Q_CHUNK = 256


def _mla_flash_kernel(qi_ref, ki_ref, q_ref, k_ref, vt_ref, o_ref, m_sc, l_sc, acc_sc):
    t = pl.program_id(2)
    qi, ki = qi_ref[t], ki_ref[t]

    @pl.when(ki == 0)
    def _():
        m_sc[...] = jnp.full(m_sc.shape, NEG, F32)
        l_sc[...] = jnp.zeros(l_sc.shape, F32)
        acc_sc[...] = jnp.zeros(acc_sc.shape, F32)

    def step(masked):
        T = q_ref.shape[1]
        if masked:
            key = lax.broadcasted_iota(jnp.int32, (T, T), 0)
            qry = lax.broadcasted_iota(jnp.int32, (T, T), 1)
            keep = key <= qry
        chains = [(h, c) for h in range(HEADS_PER_STEP_C) for c in range(T // FLASH_Q_CHUNK)]
        scores, probs, alphas = {}, {}, {}

        def keys_for(c):
            return (c + 1) * FLASH_Q_CHUNK if masked else T

        def qk(h, c):
            qs = slice(c * FLASH_Q_CHUNK, (c + 1) * FLASH_Q_CHUNK)
            q = q_ref[0, qs, h * HEAD_PAD_C:(h + 1) * HEAD_PAD_C]
            k = k_ref[0, :keys_for(c), h * HEAD_PAD_C:(h + 1) * HEAD_PAD_C]
            st = _nt_dot(k, q)
            scores[h, c] = jnp.where(keep[:keys_for(c), qs], st, NEG) if masked else st

        def softmax(h, c):
            qs = slice(c * FLASH_Q_CHUNK, (c + 1) * FLASH_Q_CHUNK)
            st = scores.pop((h, c))
            m_prev = m_sc[h, :, qs]
            m_new = jnp.maximum(m_prev, jnp.max(st, axis=0, keepdims=True))
            alpha = jnp.exp2(m_prev - m_new)
            p = jnp.exp2(st - m_new)
            l_sc[h, :, qs] = alpha * l_sc[h, :, qs] + jnp.sum(p, axis=0, keepdims=True)
            m_sc[h, :, qs] = m_new
            probs[h, c], alphas[h, c] = p.astype(BF16), alpha

        def pv(h, c):
            qs = slice(c * FLASH_Q_CHUNK, (c + 1) * FLASH_Q_CHUNK)
            vt = vt_ref[0, h * V_DIM:(h + 1) * V_DIM, :keys_for(c)]
            acc_sc[h, :, qs] = alphas.pop((h, c)) * acc_sc[h, :, qs] + jnp.dot(
                vt, probs.pop((h, c)), preferred_element_type=F32)

        for phase in (qk, softmax, pv):
            for ch in chains:
                phase(*ch)

    @pl.when(ki < qi)
    def _():
        step(False)

    @pl.when(ki == qi)
    def _():
        step(True)
        ot = jnp.concatenate([acc_sc[h] / l_sc[h] for h in range(HEADS_PER_STEP_C)], axis=0)
        o_ref[0] = ot.T.astype(o_ref.dtype)


def mla_attention(q_all, k_all, vt_all, batch, seq):
    T = 512
    nq = seq // T
    pairs = [(a, b) for a in range(nq) for b in range(a + 1)]
    qi_tab = jnp.asarray([p[0] for p in pairs], jnp.int32)
    ki_tab = jnp.asarray([p[1] for p in pairs], jnp.int32)
    hp = N_HEADS_C // HEADS_PER_STEP_C
    qw = HEADS_PER_STEP_C * HEAD_PAD_C
    vw = HEADS_PER_STEP_C * V_DIM
    q3 = q_all.reshape(batch, seq, -1)
    k3 = k_all.reshape(batch, seq, -1)
    grid_spec = pltpu.PrefetchScalarGridSpec(
        num_scalar_prefetch=2,
        grid=(batch, hp, len(pairs)),
        in_specs=[
            pl.BlockSpec((1, T, qw), lambda b, h, t, qi, ki: (b, qi[t], h)),
            pl.BlockSpec((1, T, qw), lambda b, h, t, qi, ki: (b, ki[t], h)),
            pl.BlockSpec((1, vw, T), lambda b, h, t, qi, ki: (b, h, ki[t])),
        ],
        out_specs=pl.BlockSpec((1, T, vw), lambda b, h, t, qi, ki: (b, qi[t], h)),
        scratch_shapes=[
            pltpu.VMEM((HEADS_PER_STEP_C, 1, T), F32),
            pltpu.VMEM((HEADS_PER_STEP_C, 1, T), F32),
            pltpu.VMEM((HEADS_PER_STEP_C, V_DIM, T), F32),
        ],
    )
    o = pl.pallas_call(
        _mla_flash_kernel,
        grid_spec=grid_spec,
        out_shape=jax.ShapeDtypeStruct((batch, seq, DC), BF16),
        compiler_params=_cp(("parallel", "parallel", "arbitrary")),
        name="mla_attention",
    )(qi_tab, ki_tab, q3, k3, vt_all)
    return o.reshape(batch * seq, DC)


def _mixout_kernel(x_ref, gates_ref, ub_ref, ubh_ref, o1_ref, o2_ref, o3_ref, l1_ref, l2_ref, l3_ref, yc_ref,
                   mod1_ref, mod2_ref, g2_ref, poolw_ref, pscale_ref, woa_ref, wob_ref, woc_ref, wout_ref,
                   rwt_ref, sw1_ref, sw3_ref, sw2_ref, spread_ref,
                   xmid_ref, h2a_ref, h2b_ref, logit_ref, *scratch, tiles_per_batch):
    D = x_ref.shape[1]
    tm = x_ref.shape[0]
    tile = pl.program_id(0) % tiles_per_batch
    o_scrs, l_scrs = scratch[:3], scratch[3:]

    def token_order(ref, scr):
        dil, rows, width = ref.shape[1:]
        if dil == 1:
            return ref[0, 0].astype(F32)
        for r in range(dil):
            v = ref[0, r].astype(F32)
            for c in range(width // LANES):
                scr[c, pl.ds(r, rows, stride=dil), :] = v[:, c * LANES:(c + 1) * LANES]
        return jnp.concatenate([scr[c] for c in range(width // LANES)], axis=1)

    outs = [token_order(r, s) for r, s in zip((o1_ref, o2_ref, o3_ref), o_scrs)]
    l1, l2, l3 = [token_order(r, s) for r, s in zip((l1_ref, l2_ref, l3_ref), l_scrs)]
    mx = jnp.maximum(jnp.maximum(l1, l2), l3)
    es = [jnp.exp(l1 - mx), jnp.exp(l2 - mx), jnp.exp(l3 - mx)]
    inv = 1.0 / (es[0] + es[1] + es[2])
    ya = jnp.zeros((tm, GROUP_W), F32)
    for e, o in zip(es, outs):
        w = e * inv
        w_hi = w.astype(BF16)
        w_lo = (w - w_hi.astype(F32)).astype(BF16)
        w_wide = (jnp.dot(w_hi, spread_ref[...], preferred_element_type=F32)
                  + jnp.dot(w_lo, spread_ref[...], preferred_element_type=F32))
        ya = ya + w_wide * o
    a_out = jnp.dot(ya.astype(BF16), woa_ref[...], preferred_element_type=F32)

    u = ub_ref[...].astype(F32)
    halo = jnp.where(tile > 0, ubh_ref[...].astype(F32), 0.0)
    ext = jnp.concatenate([halo, u], axis=0)
    t_seq = tile * tm + lax.broadcasted_iota(jnp.int32, (tm, 1), 0)
    pooled = []
    for gi, w in enumerate(POOL_WINDOWS):
        sl = slice(gi * POOL_GROUP_DIM, (gi + 1) * POOL_GROUP_DIM)
        acc = ext[:, sl]
        k = 1
        while k < w:
            acc = acc + pltpu.roll(acc, k, axis=0)
            k *= 2
        cnt = jnp.minimum(t_seq + 1, w).astype(F32)
        pg = acc[POOL_HALO:] / cnt - u[:, sl]
        pooled.append(jnp.dot(pg.astype(BF16), poolw_ref[gi], preferred_element_type=F32))
    yb = jnp.concatenate(pooled, axis=1) * pscale_ref[...]
    b_out = jnp.dot(yb.astype(BF16), wob_ref[...], preferred_element_type=F32)
    c_out = jnp.dot(yc_ref[...], woc_ref[...], preferred_element_type=F32)

    g = gates_ref[...].astype(F32)
    mix = (jax.nn.sigmoid(g[:, :D]) * a_out + jax.nn.sigmoid(g[:, D:2 * D]) * b_out
           + jax.nn.sigmoid(g[:, 2 * D:]) * c_out)
    tok = jnp.dot(mix.astype(BF16), wout_ref[...], preferred_element_type=F32)
    xn = x_ref[...] + mod1_ref[0][:, 2 * D:] * tok

    mod2 = mod2_ref[0]
    y = xn * lax.rsqrt(jnp.mean(xn * xn, axis=-1, keepdims=True) + EPS) * g2_ref[...]
    h2 = y * (1.0 + mod2[:, D:2 * D]) + mod2[:, :D]
    h2b = h2.astype(BF16)
    h2a_ref[...], h2b_ref[...] = _pack_row_halves(h2b)
    logit_ref[...] = _nt_dot(rwt_ref[...], h2b)
    hid = _silu(jnp.dot(h2b, sw1_ref[...], preferred_element_type=F32)) * jnp.dot(
        h2b, sw3_ref[...], preferred_element_type=F32)
    shared = jnp.dot(hid.astype(BF16), sw2_ref[...], preferred_element_type=F32)
    xmid_ref[...] = xn + mod2[:, 2 * D:] * shared


def mix_out(x2, gu, dil, yc, mod1, mod2, g2, pool_w, pool_scale, w_oa, w_ob, w_oc, w_out, rwt, sw1, sw3, sw2, seq,
            row0=0):
    D = x2.shape[1]
    N = gu.shape[0]
    tm = 512
    tpb = seq // tm
    tile0 = row0 // tm
    (o1, l1), (o2, l2), (o3, l3) = dil
    row = lambda w, c=0: pl.BlockSpec((tm, w), lambda i: (i, c))
    by_residue = lambda a: pl.BlockSpec(
        (1, a.shape[1], tm // a.shape[1], a.shape[3]), lambda i: (i // tpb, 0, i % tpb, 0))
    heads = HEADS_PER_GROUP_A
    spread = (jnp.arange(LSE_LANES)[:, None] == (jnp.arange(GROUP_W)[None, :] // HEAD_DIM_A) * (LSE_LANES // heads)
              ).astype(BF16)
    const2 = lambda a: pl.BlockSpec(a.shape, lambda i: (0,) * a.ndim, pipeline_mode=pl.Buffered(1))
    modspec = pl.BlockSpec((1, 1, 3 * D), lambda i: (i // tpb, 0, 0))
    ub_col = 3 * D // DB
    halo_spec = pl.BlockSpec(
        (POOL_HALO, DB), lambda i: (jnp.maximum(i * (tm // POOL_HALO) - 1, 0), ub_col))
    weights = [g2.reshape(1, D), pool_w, pool_scale.reshape(1, DB), w_oa, w_ob, w_oc, w_out, rwt, sw1, sw3, sw2,
               spread]
    return pl.pallas_call(
        functools.partial(_mixout_kernel, tiles_per_batch=tpb),
        grid=(N // tm,),
        in_specs=[
            pl.BlockSpec((tm, D), lambda i: (i + tile0, 0)), row(3 * D), row(DB, ub_col), halo_spec,
            by_residue(o1), by_residue(o2), by_residue(o3), by_residue(l1), by_residue(l2), by_residue(l3), row(DC),
            modspec, modspec,
        ] + [const2(a) for a in weights],
        scratch_shapes=[pltpu.VMEM((GROUP_W // LANES, tm, LANES), F32)] * 3
        + [pltpu.VMEM((LSE_LANES // LANES, tm, LANES), F32)] * 3,
        out_specs=[row(D), row(PACK_W), row(PACK_W), pl.BlockSpec((N_EXPERTS, tm), lambda i: (0, i))],
        out_shape=[
            jax.ShapeDtypeStruct((N, D), F32),
            jax.ShapeDtypeStruct((N, PACK_W), jnp.int32),
            jax.ShapeDtypeStruct((N, PACK_W), jnp.int32),
            jax.ShapeDtypeStruct((N_EXPERTS, N), F32),
        ],
        compiler_params=_cp(("parallel",), VMEM_LIMIT),
        name="mix_out",
    )(x2, gu, gu, gu, o1, o2, o3, l1, l2, l3, yc, mod1, mod2, *weights)


def _pick_rows(table, picks):
    G, GS = N_GROUPS, GROUP_SIZE
    eio = lax.broadcasted_iota(jnp.int32, (GS, table.shape[1]), 0)
    rows = []
    for k in range(TOP_K):
        idx = picks[k:k + 1]
        parts = [jnp.sum(jnp.where(eio + g * GS == idx, table[g * GS:(g + 1) * GS], 0.0), axis=0, keepdims=True)
                 for g in range(G)]
        rows.append(functools.reduce(jnp.add, parts))
    return jnp.concatenate(rows, axis=0)


def _route_choose(lg_ref, bias_ref):
    G, GS = N_GROUPS, GROUP_SIZE
    scores = jax.nn.sigmoid(lg_ref[...])
    sel = scores + bias_ref[...]
    tn = sel.shape[1]
    eio = lax.broadcasted_iota(jnp.int32, (GS, tn), 0)
    ninf = -jnp.inf

    gs = []
    for g in range(G):
        v = sel[g * GS:(g + 1) * GS]
        m1 = jnp.max(v, axis=0, keepdims=True)
        i1 = jnp.min(jnp.where(v == m1, eio, GS), axis=0, keepdims=True)
        m2 = jnp.max(jnp.where(eio == i1, ninf, v), axis=0, keepdims=True)
        gs.append(m1 + m2)
    gsm = jnp.concatenate(gs, axis=0)
    gio = lax.broadcasted_iota(jnp.int32, (G, tn), 0)
    rank = jnp.zeros((G, tn), jnp.int32)
    for g2 in range(G):
        beats = (gs[g2] > gsm) | ((gs[g2] == gsm) & (g2 < gio))
        rank = rank + beats.astype(jnp.int32)
    gsel = rank < TOPK_GROUPS

    vs = [jnp.where(gsel[g:g + 1], sel[g * GS:(g + 1) * GS], NEG) for g in range(G)]
    eid = [eio + g * GS for g in range(G)]
    chosen = [jnp.zeros((GS, tn), jnp.bool_) for _ in range(G)]
    picks = []
    for _ in range(TOP_K):
        m = functools.reduce(jnp.maximum, [jnp.max(v, axis=0, keepdims=True) for v in vs])
        idx = functools.reduce(jnp.minimum, [
            jnp.min(jnp.where(v == m, e, N_EXPERTS), axis=0, keepdims=True) for v, e in zip(vs, eid)])
        picks.append(idx)
        for g in range(G):
            hit = eid[g] == idx
            chosen[g] = chosen[g] | hit
            vs[g] = jnp.where(hit, ninf, vs[g])
    mask = jnp.concatenate(chosen, axis=0).astype(F32)
    return scores, jnp.concatenate(picks, axis=0), mask


def _route_kernel(lg_ref, bias_ref, tri_ref, dest_ref, w_ref, cnt_ref, run_sc, start_sc, mask_sc, picks_sc,
                  *, slot_block):
    phase = pl.program_id(0)
    step = pl.program_id(1)
    tn = lg_ref.shape[1]
    cols = pl.ds(pl.multiple_of(step * tn, tn), tn)

    @pl.when(phase == 0)
    def _():
        @pl.when(step == 0)
        def _():
            run_sc[...] = jnp.zeros(run_sc.shape, F32)

        scores, picks, mask = _route_choose(lg_ref, bias_ref)
        wk = _pick_rows(scores, picks)
        w_ref[0] = wk / jnp.sum(wk, axis=0, keepdims=True) * ROUTED_SCALE
        dest_ref[0] = jnp.zeros(dest_ref.shape[1:], dest_ref.dtype)
        mask_sc[:, cols] = mask.astype(BF16)
        picks_sc[:, cols] = picks
        run_sc[...] = run_sc[...] + jnp.sum(mask, axis=1, keepdims=True)

    @pl.when(phase == 1)
    def _():
        @pl.when(step == 0)
        def _():
            counts = run_sc[...].astype(jnp.int32)
            cnt_ref[...] = jnp.broadcast_to(counts, cnt_ref.shape)
            shift = slot_block.bit_length() - 1
            padded = lax.shift_left(lax.shift_right_logical(counts + (slot_block - 1), shift), shift).astype(F32)
            r = lax.broadcasted_iota(jnp.int32, (N_EXPERTS, N_EXPERTS), 0)
            c = lax.broadcasted_iota(jnp.int32, (N_EXPERTS, N_EXPERTS), 1)
            as_row = jnp.sum(jnp.where(r == c, padded, 0.0), axis=0, keepdims=True)
            start_sc[...] = jnp.sum(jnp.where(c < r, as_row, 0.0), axis=1, keepdims=True)
            run_sc[...] = jnp.zeros(run_sc.shape, F32)

        mask_b = mask_sc[:, cols]
        mask = mask_b.astype(F32)
        before = jnp.dot(mask_b, tri_ref[...], preferred_element_type=F32) - mask
        slot = start_sc[...] + run_sc[...] + before
        dest_ref[0] = _pick_rows(slot, picks_sc[:, cols]).astype(jnp.int32)
        w_ref[0] = jnp.zeros(w_ref.shape[1:], w_ref.dtype)
        run_sc[...] = run_sc[...] + jnp.sum(mask, axis=1, keepdims=True)


SLOT_BLOCK = 512


def route(logits_t, bias):
    E, N = logits_t.shape
    tn = 1024
    tri = (jnp.arange(tn)[:, None] <= jnp.arange(tn)[None, :]).astype(BF16)
    plane = lambda: pl.BlockSpec((1, TOP_K, tn), lambda p, i: (p, 0, i))
    dest, w, cnt = pl.pallas_call(
        functools.partial(_route_kernel, slot_block=SLOT_BLOCK),
        grid=(2, N // tn),
        in_specs=[
            pl.BlockSpec((E, tn), lambda p, i: (0, i * (1 - p))),
            pl.BlockSpec((E, 1), lambda p, i: (0, 0)),
            pl.BlockSpec((tn, tn), lambda p, i: (0, 0)),
        ],
        out_specs=[plane(), plane(), pl.BlockSpec((E, 128), lambda p, i: (0, 0))],
        out_shape=[
            jax.ShapeDtypeStruct((2, TOP_K, N), jnp.int32),
            jax.ShapeDtypeStruct((2, TOP_K, N), F32),
            jax.ShapeDtypeStruct((E, 128), jnp.int32),
        ],
        scratch_shapes=[pltpu.VMEM((E, 1), F32), pltpu.VMEM((E, 1), F32),
                        pltpu.VMEM((E, N), BF16), pltpu.VMEM((TOP_K, N), jnp.int32)],
        compiler_params=_cp(("arbitrary", "arbitrary")),
        name="route",
    )(logits_t, bias.reshape(E, 1), tri)
    return dest[1], w[0], cnt[:, 0]


def block_tables(counts, n_tokens):
    E = counts.shape[0]
    blk = SLOT_BLOCK
    nblk = (n_tokens * TOP_K + E * blk) // blk
    per_expert = (counts + blk - 1) // blk
    bend = jnp.cumsum(per_expert)
    bstart = bend - per_expert
    b = jnp.arange(nblk, dtype=jnp.int32)[:, None]
    owns = (bstart[None, :] <= b) & (b < bend[None, :])
    blk_e = jnp.minimum(jnp.sum(bend[None, :] <= b, axis=1), E - 1).astype(jnp.int32)
    rows_left = counts[None, :] - (b - bstart[None, :]) * blk
    nvalid = jnp.sum(jnp.where(owns, jnp.clip(rows_left, 0, blk), 0), axis=1)
    first = jnp.concatenate([jnp.ones((1,), jnp.bool_), blk_e[1:] != blk_e[:-1]])
    run_parity = ((jnp.cumsum(first.astype(jnp.int32)) - 1) % 2).astype(jnp.int32)
    later = blk_e[None, :] > blk_e[:, None]
    next_e = jnp.min(jnp.where(later, blk_e[None, :], E), axis=1).astype(jnp.int32)
    return blk_e, nvalid.astype(jnp.int32), run_parity, next_e


def _sc_mesh():
    return plsc.VectorSubcoreMesh(core_axis_name="c", subcore_axis_name="s")


SC_WINDOW = 128


def sc_scatter_rows(x, dest, n_slots):
    N, W = x.shape
    K = dest.shape[0]

    @functools.partial(pl.kernel, out_type=jax.ShapeDtypeStruct((n_slots, W), x.dtype), mesh=_sc_mesh(),
                       scratch_types=[])
    def scatter(x_hbm, i_hbm, o_hbm):
        def body(x_vmem, i_vmem):
            for k in range(K):
                pltpu.sync_copy(x_vmem, o_hbm.at[i_vmem.at[k]])

        pltpu.emit_pipeline(
            body,
            grid=(N // SC_WINDOW,),
            in_specs=[pl.BlockSpec((SC_WINDOW, W), lambda i: (i, 0)),
                      pl.BlockSpec((K, SC_WINDOW), lambda i: (0, i))],
            out_specs=[],
            core_axis_name=("c", "s"),
            dimension_semantics=(pltpu.PARALLEL,),
        )(x_hbm, i_hbm)

    return scatter(x, dest)


SC_LANES = 16
SC_GATHER_TOKENS = 8


def sc_weighted_gather(y, dest, wts):
    W = y.shape[1]
    K, N = dest.shape
    G, L = SC_GATHER_TOKENS, SC_LANES
    batches = SC_WINDOW // G

    @functools.partial(
        pl.kernel, out_type=jax.ShapeDtypeStruct((N, W), y.dtype), mesh=_sc_mesh(),
        scratch_types=[pltpu.VMEM((2, K, G, W), y.dtype), pltpu.SemaphoreType.DMA((2,))],
        compiler_params=pltpu.CompilerParams(needs_layout_passes=False))
    def gather(y_hbm, i_hbm, w_hbm, o_hbm, rows2, sems):
        def body(i_vmem, w_vmem, o_vmem):
            def fetch(batch, slot):
                return [pltpu.make_async_copy(y_hbm.at[i_vmem.at[k, pl.ds(batch * G, G)]], rows2.at[slot, k],
                                              sems.at[slot]) for k in range(K)]

            for c in fetch(0, 0):
                c.start()

            @pl.loop(0, batches)
            def _(batch):
                slot = batch % 2

                @pl.when(batch + 1 < batches)
                def _():
                    for c in fetch(batch + 1, 1 - slot):
                        c.start()

                for c in fetch(batch, slot):
                    c.wait()
                rows = rows2.at[slot]

                @pl.loop(0, G)
                def _(t):
                    tok = jnp.full((L,), batch * G + t, jnp.int32)
                    wk = [plsc.load_gather(w_vmem, [jnp.full((L,), k, jnp.int32), tok]) for k in range(K)]

                    @plsc.parallel_loop(0, W // L, unroll=4)
                    def _(j):
                        lo = jnp.zeros((L,), F32)
                        hi = jnp.zeros((L,), F32)
                        for k in range(K):
                            pair = plsc.bitcast(rows[k, t, pl.ds(j * L, L)], BF16)
                            a, b = plsc.unpack(pair, format=plsc.PackFormat.INTERLEAVED)
                            lo = lo + wk[k] * a
                            hi = hi + wk[k] * b
                        o_vmem[batch * G + t, pl.ds(j * L, L)] = plsc.bitcast(
                            plsc.pack(lo, hi, format=plsc.PackFormat.INTERLEAVED), y.dtype)

        pltpu.emit_pipeline(
            body,
            grid=(N // SC_WINDOW,),
            in_specs=[pl.BlockSpec((K, SC_WINDOW), lambda i: (0, i)),
                      pl.BlockSpec((K, SC_WINDOW), lambda i: (0, i))],
            out_specs=[pl.BlockSpec((SC_WINDOW, W), lambda i: (i, 0))],
            core_axis_name=("c", "s"),
            dimension_semantics=(pltpu.PARALLEL,),
        )(i_hbm, w_hbm, o_hbm)

    return gather(y, dest, wts)


EXPERT_INPUT_SLOTS = 3


def _expert_kernel(blk_e_ref, nvalid_ref, parity_ref, next_e_ref, xa_hbm, xb_hbm, w1_hbm, w3_hbm, w2_hbm,
                   ya_ref, yb_ref, w1_sc, w3_sc, w2_sc, xa_buf, xb_buf, sems, w1_st, w3_st, w2_st, wsems,
                   *, layer):
    b = pl.program_id(0)
    nb = pl.num_programs(0)
    nv = nvalid_ref[b]
    prev_e = blk_e_ref[jnp.maximum(b - 1, 0)]
    blk = xa_buf.shape[1]
    ring = EXPERT_INPUT_SLOTS
    n_experts = w1_hbm.shape[1]

    def fetch_weights(e, par):
        return (pltpu.make_async_copy(w1_hbm.at[layer, e], w1_st.at[par], wsems.at[par, 0]),
                pltpu.make_async_copy(w3_hbm.at[layer, e], w3_st.at[par], wsems.at[par, 1]),
                pltpu.make_async_copy(w2_hbm.at[layer, e], w2_st.at[par], wsems.at[par, 2]))

    def fetch(block, slot):
        rows = pl.ds(pl.multiple_of(block * blk, blk), blk)
        return (pltpu.make_async_copy(xa_hbm.at[rows], xa_buf.at[slot], sems.at[slot, 0]),
                pltpu.make_async_copy(xb_hbm.at[rows], xb_buf.at[slot], sems.at[slot, 1]))

    @pl.when(b == 0)
    def _():
        for i in range(ring - 1):
            for c in fetch(i, i):
                c.start()

    ahead = b + (ring - 1)

    @pl.when(ahead < nb)
    def _():
        for c in fetch(ahead, ahead % ring):
            c.start()

    slot = b % ring
    for c in fetch(b, slot):
        c.wait()

    @pl.when(b == 0)
    def _():
        for c in fetch_weights(blk_e_ref[0], 0):
            c.start()

    @pl.when((b == 0) | (blk_e_ref[b] != prev_e))
    def _():
        par = parity_ref[b]
        nxt = next_e_ref[b]

        @pl.when(nxt < n_experts)
        def _():
            for c in fetch_weights(nxt, 1 - par):
                c.start()

        for c in fetch_weights(blk_e_ref[b], par):
            c.wait()
        w1_sc[...] = w1_st[par].astype(BF16)
        w3_sc[...] = w3_st[par].astype(BF16)
        w2_sc[...] = w2_st[par].astype(BF16)

    @pl.when(nv > 0)
    def _():
        x = _unpack_row_halves(xa_buf[slot], xb_buf[slot])
        rows = lax.broadcasted_iota(jnp.int32, x.shape, 0)
        x = jnp.where(rows < nv, x, 0.0).astype(BF16)
        hid = _silu(jnp.dot(x, w1_sc[...], preferred_element_type=F32)) * jnp.dot(
            x, w3_sc[...], preferred_element_type=F32)
        y = jnp.dot(hid.astype(BF16), w2_sc[...], preferred_element_type=F32)
        ya_ref[...], yb_ref[...] = _pack_row_halves(y)

    @pl.when(nv == 0)
    def _():
        ya_ref[...] = jnp.zeros(ya_ref.shape, ya_ref.dtype)
        yb_ref[...] = jnp.zeros(yb_ref.shape, yb_ref.dtype)


def routed_experts(xa, xb, tables, w1, w3, w2, layer):
    P = xa.shape[0]
    blk = SLOT_BLOCK
    _, E, D, FF = w1.shape
    slots = lambda: pl.BlockSpec((blk, PACK_W), lambda b, *_: (b, 0))
    grid_spec = pltpu.PrefetchScalarGridSpec(
        num_scalar_prefetch=len(tables),
        grid=(P // blk,),
        in_specs=[pl.BlockSpec(memory_space=pl.ANY)] * 5,
        out_specs=[slots(), slots()],
        scratch_shapes=[
            pltpu.VMEM((D, FF), BF16), pltpu.VMEM((D, FF), BF16), pltpu.VMEM((FF, D), BF16),
            pltpu.VMEM((EXPERT_INPUT_SLOTS, blk, PACK_W), jnp.int32),
            pltpu.VMEM((EXPERT_INPUT_SLOTS, blk, PACK_W), jnp.int32),
            pltpu.SemaphoreType.DMA((EXPERT_INPUT_SLOTS, 2)),
            pltpu.VMEM((2, D, FF), F32), pltpu.VMEM((2, D, FF), F32), pltpu.VMEM((2, FF, D), F32),
            pltpu.SemaphoreType.DMA((2, 3)),
        ],
    )
    return pl.pallas_call(
        functools.partial(_expert_kernel, layer=layer),
        grid_spec=grid_spec,
        out_shape=[jax.ShapeDtypeStruct((P, PACK_W), jnp.int32)] * 2,
        compiler_params=_cp(("arbitrary",), VMEM_LIMIT),
        name="routed_experts",
    )(*tables, xa, xb, w1, w3, w2)


def _combine_kernel(xmid_ref, ra_ref, rb_ref, mod2_ref, fg_ref, *rest, final):
    out_ref = rest[-1]
    D = xmid_ref.shape[1]
    x = xmid_ref[...] + mod2_ref[0][:, 2 * D:] * _unpack_row_halves(ra_ref[...], rb_ref[...])
    if final:
        x = x * lax.rsqrt(jnp.mean(x * x, axis=-1, keepdims=True) + EPS) * fg_ref[...]
    out_ref[...] = x


def combine(xmid, ra, rb, mod2, final_g, seq, final, out_rows=None, row0=0, out_buf=None):
    N, D = xmid.shape
    tm = 512
    tpb = seq // tm
    tile0 = row0 // tm
    in_specs = [
        pl.BlockSpec((tm, D), lambda i: (i, 0)),
        pl.BlockSpec((tm, PACK_W), lambda i: (i, 0)),
        pl.BlockSpec((tm, PACK_W), lambda i: (i, 0)),
        pl.BlockSpec((1, 1, 3 * D), lambda i: (i // tpb, 0, 0)),
        pl.BlockSpec((1, D), lambda i: (0, 0)),
    ]
    args = [xmid, ra, rb, mod2, final_g.reshape(1, D)]
    aliases = {}
    if out_buf is not None:
        in_specs.append(pl.BlockSpec(memory_space=pl.ANY))
        args.append(out_buf)
        aliases = {len(args) - 1: 0}
    return pl.pallas_call(
        functools.partial(_combine_kernel, final=final),
        grid=(N // tm,),
        in_specs=in_specs,
        out_specs=pl.BlockSpec((tm, D), lambda i: (i + tile0, 0)),
        out_shape=jax.ShapeDtypeStruct((out_rows or N, D), F32),
        input_output_aliases=aliases,
        compiler_params=_cp(("parallel",), VMEM_LIMIT),
        name="combine",
    )(*args)


TOKEN_STREAMS = 2


def _permute_w_in(w):
    ub = w[:, 3 * DA:3 * DA + DB]
    lat_lo = 3 * DA + DB
    lat_hi = lat_lo + Q_LORA + KV_LORA + QK_ROPE
    lat, gates = w[:, lat_lo:lat_hi], w[:, lat_hi:]
    pad = jnp.zeros((w.shape[0], LAT_W - (lat_hi - lat_lo)), w.dtype)
    parts = [gates, ub, lat, pad]
    for g in range(len(DIL_GROUPS)):
        sl = slice(g * GROUP_W, (g + 1) * GROUP_W)
        parts += [w[:, :DA][:, sl] * (HEAD_DIM_A ** -0.5), w[:, DA:2 * DA][:, sl], w[:, 2 * DA:3 * DA][:, sl]]
    return jnp.concatenate(parts, axis=1).astype(BF16)


def kernel(x, c, positions, ada_mix_w, ada_mix_b, norm_mix_g, w_in, pool_w, pool_scale, cq_norm_g, ckv_norm_g, w_uq, w_ukv, w_oa, w_ob, w_oc, w_out, ada_ffn_w, ada_ffn_b, norm_ffn_g, router_w, router_bias, exp_w1, exp_w3, exp_w2, sh_w1, sh_w3, sh_w2, final_g):
    B, S, D = x.shape
    depth = w_in.shape[0]
    mod_mix = adaln_rows(c, ada_mix_w, ada_mix_b)
    mod_ffn = adaln_rows(c, ada_ffn_w, ada_ffn_b)
    streams = TOKEN_STREAMS if B % TOKEN_STREAMS == 0 else 1
    Bs = B // streams
    Ns = Bs * S
    x_all = x.reshape(B * S, D)
    xs = [None] * streams
    out_all = None
    pos_s = [positions[s * Bs:(s + 1) * Bs] for s in range(streams)]
    for l in range(depth):
        last = l == depth - 1
        w_in_l = _permute_w_in(w_in[l])
        mla_w = _mla_weights(cq_norm_g[l], ckv_norm_g[l], w_uq[l], w_ukv[l])
        mix_w = (norm_ffn_g[l], pool_w[l].astype(BF16), pool_scale[l],
                 w_oa[l].astype(BF16), w_ob[l].astype(BF16), w_oc[l].astype(BF16), w_out[l].astype(BF16),
                 router_w[l].T.astype(BF16), sh_w1[l].astype(BF16), sh_w3[l].astype(BF16), sh_w2[l].astype(BF16))
        for s in range(streams):
            x2, row0 = (x_all, s * Ns) if l == 0 else (xs[s], 0)
            mod1 = mod_mix[l, s * Bs:(s + 1) * Bs].reshape(Bs, 1, 3 * D)
            mod2 = mod_ffn[l, s * Bs:(s + 1) * Bs].reshape(Bs, 1, 3 * D)
            gu, lat, *qkv = in_projection(x2, norm_mix_g[l], mod1, w_in_l, S, row0)
            dil = [dilated_attention(qkv[2 * g], qkv[2 * g + 1]) for g in range(len(DIL_GROUPS))]
            q_all, k_all, vt_all = mla_prep(lat, pos_s[s], *mla_w, Bs, S)
            yc = mla_attention(q_all, k_all, vt_all, Bs, S)
            xmid, h2a, h2b, logits_t = mix_out(x2, gu, dil, yc, mod1, mod2, *mix_w, S, row0)
            dest, w_k, counts = route(logits_t, router_bias[l])
            tables = block_tables(counts, Ns)
            n_slots = tables[0].shape[0] * SLOT_BLOCK
            xa = sc_scatter_rows(h2a, dest, n_slots)
            xb = sc_scatter_rows(h2b, dest, n_slots)
            ya, yb = routed_experts(xa, xb, tables, exp_w1, exp_w3, exp_w2, l)
            ra = sc_weighted_gather(ya, dest, w_k)
            rb = sc_weighted_gather(yb, dest, w_k)
            if last:
                out_all = combine(xmid, ra, rb, mod2, final_g, S, True, B * S, s * Ns, out_all)
            else:
                xs[s] = combine(xmid, ra, rb, mod2, final_g, S, False)
    return out_all.reshape(B, S, D)
```

```python
import functools
import math

import jax
import jax.numpy as jnp
from jax import lax
from jax.experimental import pallas as pl
from jax.experimental.pallas import tpu as pltpu
from jax.experimental.pallas import tpu_sc as plsc

F32 = jnp.float32
BF16 = jnp.bfloat16
HIGHEST = lax.Precision.HIGHEST

D_MODEL = 1024
HEAD_DIM_A = 64
HEADS_PER_GROUP_A = 4
DIL_GROUPS = ((128, 1), (512, 4), (2048, 16))
GROUP_W = HEADS_PER_GROUP_A * HEAD_DIM_A
DA = GROUP_W * len(DIL_GROUPS)
POOL_WINDOWS = (2, 4, 8, 16)
POOL_GROUP_DIM = 128
DB = POOL_GROUP_DIM * len(POOL_WINDOWS)
POOL_HALO = 16
N_HEADS_C = 8
QK_NOPE = 64
QK_ROPE = 32
V_DIM = 64
Q_LORA = 384
KV_LORA = 256
DC = N_HEADS_C * V_DIM
HEAD_PAD_C = 128
ROPE_THETA = 10000.0
N_EXPERTS = 64
TOP_K = 8
N_GROUPS = 8
TOPK_GROUPS = 4
GROUP_SIZE = N_EXPERTS // N_GROUPS
ROUTED_SCALE = 2.5
EPS = 1e-6
NEG = -1e30
Q_BLOCK = 128

LAT_W = 768
GU_W = 3 * D_MODEL + DB
IN_OUT_WIDTHS = (GU_W, LAT_W) + (2 * GROUP_W, GROUP_W) * len(DIL_GROUPS)

VMEM_LIMIT = 56 * 1024 * 1024


def _cp(sem, vmem=None):
    return pltpu.CompilerParams(dimension_semantics=sem, vmem_limit_bytes=vmem)


def _silu(v):
    return v * jax.nn.sigmoid(v)


def _nt_dot(a, b):
    return lax.dot_general(a, b, (((1,), (1,)), ((), ())), preferred_element_type=F32)


PACK_W = D_MODEL // 4
_HI_MASK = -65536


def _bf16_bits(v):
    return lax.bitcast_convert_type(v.astype(BF16).astype(F32), jnp.int32)


def _pack_row_halves(v):
    halves = []
    for h in range(2):
        lo = _bf16_bits(v[:, (2 * h) * PACK_W:(2 * h + 1) * PACK_W])
        hi = _bf16_bits(v[:, (2 * h + 1) * PACK_W:(2 * h + 2) * PACK_W])
        halves.append(lax.shift_right_logical(lo, 16) | (hi & _HI_MASK))
    return halves


def _unpack_row_halves(wa, wb):
    parts = []
    for w in (wa, wb):
        parts.append(lax.bitcast_convert_type(lax.shift_left(w, 16), F32))
        parts.append(lax.bitcast_convert_type(w & _HI_MASK, F32))
    return jnp.concatenate(parts, axis=1)


def _adaln_kernel(c_ref, w_ref, b_ref, o_ref):
    s = _silu(c_ref[...])
    o_ref[0] = jnp.dot(s, w_ref[0], preferred_element_type=F32, precision=HIGHEST) + b_ref[0]


def adaln_rows(c, w, b):
    L, D, D3 = w.shape
    B = c.shape[0]
    tn = 1024
    return pl.pallas_call(
        _adaln_kernel,
        grid=(L, D3 // tn),
        in_specs=[
            pl.BlockSpec((B, D), lambda l, j: (0, 0)),
            pl.BlockSpec((1, D, tn), lambda l, j: (l, 0, j)),
            pl.BlockSpec((1, 1, tn), lambda l, j: (l, 0, j)),
        ],
        out_specs=pl.BlockSpec((1, B, tn), lambda l, j: (l, 0, j)),
        out_shape=jax.ShapeDtypeStruct((L, B, D3), F32),
        compiler_params=_cp(("parallel", "parallel")),
        name="adaln_rows",
    )(c, w, b.reshape(L, 1, D3))


LANES = 128


def _inproj_kernel(x_ref, g_ref, mod_ref, w_ref, *refs, chunk, pending):
    o_refs, scr = refs[:-1], refs[-1]
    D = x_ref.shape[1]
    x = x_ref[...]
    if pending:
        ra_ref, rb_ref, gate_ref, x_out_ref, *o_refs = o_refs
        x = x + gate_ref[0][:, 2 * D:] * _unpack_row_halves(ra_ref[...], rb_ref[...])
        x_out_ref[...] = x
    y = x * lax.rsqrt(jnp.mean(x * x, axis=-1, keepdims=True) + EPS) * g_ref[...]
    mod = mod_ref[0]
    h = (y * (1.0 + mod[:, D:2 * D]) + mod[:, :D]).astype(BF16)
    col = 0
    for o_ref in o_refs:
        width = o_ref.shape[-1]
        if o_ref.ndim == 2:
            for c0 in range(0, width, chunk):
                cw = min(chunk, width - c0)
                o_ref[:, c0:c0 + cw] = jnp.dot(
                    h, w_ref[:, col + c0:col + c0 + cw], preferred_element_type=F32).astype(o_ref.dtype)
        else:
            dil, rows = o_ref.shape[1], o_ref.shape[2]
            z = jnp.dot(h, w_ref[:, col:col + width], preferred_element_type=F32)
            if dil == 1:
                o_ref[0, 0] = z.astype(o_ref.dtype)
            else:
                for c in range(width // LANES):
                    scr[c] = z[:, c * LANES:(c + 1) * LANES]
                for r in range(dil):
                    o_ref[0, r] = jnp.concatenate(
                        [scr[c, pl.ds(r, rows, stride=dil), :] for c in range(width // LANES)],
                        axis=1).astype(o_ref.dtype)
        col += width


def in_projection(x2, g, mod, w, seq, row0=0, pending=None):
    D = x2.shape[1]
    B = mod.shape[0]
    N = B * seq
    tm = 512
    tpb = seq // tm
    tile0 = row0 // tm
    out_specs = [pl.BlockSpec((tm, wd), lambda i: (i, 0)) for wd in IN_OUT_WIDTHS[:2]]
    out_shape = [jax.ShapeDtypeStruct((N, wd), BF16) for wd in IN_OUT_WIDTHS[:2]]
    for grp, (_, dil) in enumerate(DIL_GROUPS):
        for wd in IN_OUT_WIDTHS[2 + 2 * grp:4 + 2 * grp]:
            out_specs.append(pl.BlockSpec((1, dil, tm // dil, wd), lambda i: (i // tpb, 0, i % tpb, 0)))
            out_shape.append(jax.ShapeDtypeStruct((B, dil, seq // dil, wd), BF16))
    in_specs = [
        pl.BlockSpec((tm, D), lambda i: (i + tile0, 0)),
        pl.BlockSpec((1, D), lambda i: (0, 0)),
        pl.BlockSpec((1, 1, 3 * D), lambda i: (i // tpb, 0, 0)),
        pl.BlockSpec(w.shape, lambda i: (0, 0), pipeline_mode=pl.Buffered(1)),
    ]
    args = [x2, g.reshape(1, D), mod, w]
    if pending is not None:
        in_specs += [pl.BlockSpec((tm, PACK_W), lambda i: (i, 0)), pl.BlockSpec((tm, PACK_W), lambda i: (i, 0)),
                     pl.BlockSpec((1, 1, 3 * D), lambda i: (i // tpb, 0, 0))]
        args += list(pending)
        out_specs.insert(0, pl.BlockSpec((tm, D), lambda i: (i, 0)))
        out_shape.insert(0, jax.ShapeDtypeStruct((N, D), F32))
    return pl.pallas_call(
        functools.partial(_inproj_kernel, chunk=512, pending=pending is not None),
        grid=(N // tm,),
        in_specs=in_specs,
        out_specs=out_specs,
        out_shape=out_shape,
        scratch_shapes=[pltpu.VMEM((max(IN_OUT_WIDTHS[2:]) // LANES, tm, LANES), F32)],
        compiler_params=_cp(("parallel",), VMEM_LIMIT),
        name="in_projection",
    )(*args)


def _dilated_kernel(q_ref, kc_ref, kp_ref, vc_ref, vp_ref, o_ref, lse_ref):
    i = pl.program_id(1)
    T = Q_BLOCK
    key = lax.broadcasted_iota(jnp.int32, (T, T), 0)
    qry = lax.broadcasted_iota(jnp.int32, (T, T), 1)
    valid_c = key <= qry
    near = key >= qry
    seqs, run = q_ref.shape[0], q_ref.shape[1] // T
    heads = [slice(h * HEAD_DIM_A, (h + 1) * HEAD_DIM_A) for h in range(HEADS_PER_GROUP_A)]

    def transposed(v):
        return v.astype(F32).T.astype(BF16)

    vts = {(s, j): transposed(vc_ref[s, j * T:(j + 1) * T, :]) for s in range(seqs) for j in range(run)}
    vt_before = [transposed(vp_ref[s]) for s in range(seqs)]

    def blocks(s, j):
        rows = slice(j * T, (j + 1) * T)
        if j == 0:
            return rows, kc_ref[s, rows, :], vts[s, 0], kp_ref[s], vt_before[s], near & (i > 0)
        before = slice((j - 1) * T, j * T)
        return rows, kc_ref[s, rows, :], vts[s, j], kc_ref[s, before, :], vts[s, j - 1], near

    scores, probs = {}, {}
    for s in range(seqs):
        for j in range(run):
            rows, kc, _, kp, _, valid_p = blocks(s, j)
            q = q_ref[s, rows, :]
            for h, sl in enumerate(heads):
                qh = q[:, sl]
                scores[s, j, h] = (jnp.where(valid_c, _nt_dot(kc[:, sl], qh), NEG),
                                   jnp.where(valid_p, _nt_dot(kp[:, sl], qh), NEG))
    for chain, (sc, sp) in scores.items():
        m = jnp.maximum(jnp.max(sc, axis=0, keepdims=True), jnp.max(sp, axis=0, keepdims=True))
        pc = jnp.exp(sc - m)
        pp = jnp.exp(sp - m)
        den = jnp.sum(pc, axis=0, keepdims=True) + jnp.sum(pp, axis=0, keepdims=True)
        probs[chain] = (pc.astype(BF16), pp.astype(BF16), den, m + jnp.log(den))
    spread = LSE_LANES // len(heads)
    for s in range(seqs):
        for j in range(run):
            rows, _, vtc, _, vtp, _ = blocks(s, j)
            outs = []
            for h, sl in enumerate(heads):
                pc, pp, den, _ = probs[s, j, h]
                o = (jnp.dot(vtc[sl, :], pc, preferred_element_type=F32)
                     + jnp.dot(vtp[sl, :], pp, preferred_element_type=F32))
                outs.append(o / den)
            o_ref[s, rows, :] = jnp.concatenate(outs, axis=0).T.astype(o_ref.dtype)
            lse_t = jnp.concatenate(
                [jnp.broadcast_to(probs[s, j, h][3], (spread, T)) for h in range(len(heads))], axis=0)
            lse_ref[s, rows, :] = lse_t.T


DILATED_RUN = 8


LSE_LANES = 128


def dilated_attention(qk, v):
    batch, dilation, L, _ = qk.shape
    nb = L // Q_BLOCK
    run = min(DILATED_RUN, nb)
    seqs = DILATED_RUN // run
    qk_r = qk.reshape(batch * dilation, L, 2 * GROUP_W)
    v_r = v.reshape(batch * dilation, L, GROUP_W)
    before = lambda i: jnp.maximum(i * run - 1, 0)
    o, lse = pl.pallas_call(
        _dilated_kernel,
        grid=(batch * dilation // seqs, nb // run),
        in_specs=[
            pl.BlockSpec((seqs, run * Q_BLOCK, GROUP_W), lambda s, i: (s, i, 0)),
            pl.BlockSpec((seqs, run * Q_BLOCK, GROUP_W), lambda s, i: (s, i, 1)),
            pl.BlockSpec((seqs, Q_BLOCK, GROUP_W), lambda s, i: (s, before(i), 1)),
            pl.BlockSpec((seqs, run * Q_BLOCK, GROUP_W), lambda s, i: (s, i, 0)),
            pl.BlockSpec((seqs, Q_BLOCK, GROUP_W), lambda s, i: (s, before(i), 0)),
        ],
        out_specs=[
            pl.BlockSpec((seqs, run * Q_BLOCK, GROUP_W), lambda s, i: (s, i, 0)),
            pl.BlockSpec((seqs, run * Q_BLOCK, LSE_LANES), lambda s, i: (s, i, 0)),
        ],
        out_shape=[
            jax.ShapeDtypeStruct((batch * dilation, L, GROUP_W), BF16),
            jax.ShapeDtypeStruct((batch * dilation, L, LSE_LANES), F32),
        ],
        compiler_params=_cp(("parallel", "parallel")),
        name=f"dilated_attention_d{dilation}",
    )(qk_r, qk_r, qk_r, v_r, v_r)
    return o.reshape(batch, dilation, L, GROUP_W), lse.reshape(batch, dilation, L, LSE_LANES)


def _mla_prep_kernel(lat_ref, pos_ref, gq_ref, gkv_ref, wq_ref, wk_ref, wvt_ref, freq_ref, spread_ref, one_ref,
                     q_ref, k_ref, vt_ref):
    HP = N_HEADS_C * HEAD_PAD_C
    lat = lat_ref[...].astype(F32)
    cq = lat[:, :Q_LORA]
    ckr = lat[:, Q_LORA:]
    zq = (cq * lax.rsqrt(jnp.mean(cq * cq, axis=-1, keepdims=True) + EPS) * gq_ref[...]).astype(BF16)
    lane = lax.broadcasted_iota(jnp.int32, ckr.shape, 1)
    is_kv = lane < KV_LORA
    ms = jnp.sum(jnp.where(is_kv, ckr * ckr, 0.0), axis=-1, keepdims=True) * (1.0 / KV_LORA)
    zkv = (ckr * jnp.where(is_kv, lax.rsqrt(ms + EPS) * gkv_ref[...], 1.0)).astype(BF16)
    qq = jnp.dot(zq, wq_ref[...], preferred_element_type=F32)
    kk = jnp.dot(zkv, wk_ref[:, :HP], preferred_element_type=F32)
    kk_sw = jnp.dot(zkv[:, KV_LORA:], wk_ref[KV_LORA:, HP:], preferred_element_type=F32)
    ang_t = freq_ref[...] * pos_ref[0].astype(F32)

    def to_lanes(t):
        hi = t.astype(BF16)
        lo = (t - hi.astype(F32)).astype(BF16)
        tn_dot = lambda a: lax.dot_general(a, spread_ref[...], (((0,), (0,)), ((), ())), preferred_element_type=F32)
        return tn_dot(hi) + tn_dot(lo)

    cos = to_lanes(jnp.cos(ang_t)) + one_ref[...]
    sin = to_lanes(jnp.sin(ang_t))
    for h in range(N_HEADS_C):
        lo, hi = h * HEAD_PAD_C, (h + 1) * HEAD_PAD_C
        q_ref[:, lo:hi] = (qq[:, lo:hi] * cos + qq[:, HP + lo:HP + hi] * sin).astype(q_ref.dtype)
        k_ref[:, lo:hi] = (kk[:, lo:hi] * cos + kk_sw[:, lo:hi] * sin).astype(k_ref.dtype)
    vt_ref[0] = _nt_dot(wvt_ref[...], zkv).astype(vt_ref.dtype)


def _mla_weights(cq_g, ckv_g, w_uq, w_ukv):
    H, HPAD, half = N_HEADS_C, HEAD_PAD_C, QK_ROPE // 2
    scale = (QK_NOPE + QK_ROPE) ** -0.5 * math.log2(math.e)
    wq = w_uq.reshape(Q_LORA, H, QK_NOPE + QK_ROPE) * scale
    q_lin = jnp.pad(wq, ((0, 0), (0, 0), (0, HPAD - QK_NOPE - QK_ROPE)))
    r1, r2 = wq[..., QK_NOPE:QK_NOPE + half], wq[..., QK_NOPE + half:]
    q_sw = jnp.concatenate([jnp.zeros((Q_LORA, H, QK_NOPE), F32), -r2, r1,
                            jnp.zeros((Q_LORA, H, HPAD - QK_NOPE - QK_ROPE), F32)], axis=-1)
    wq_big = jnp.concatenate([q_lin.reshape(Q_LORA, H * HPAD), q_sw.reshape(Q_LORA, H * HPAD)], axis=1)

    rows = LAT_W - Q_LORA
    wkv = w_ukv.reshape(KV_LORA, H, QK_NOPE + V_DIM)
    eye = jnp.eye(QK_ROPE, dtype=F32)
    k_lin = jnp.zeros((rows, H, HPAD), F32)
    k_lin = k_lin.at[:KV_LORA, :, :QK_NOPE].set(wkv[..., :QK_NOPE])
    k_lin = k_lin.at[KV_LORA:KV_LORA + QK_ROPE, :, QK_NOPE:QK_NOPE + QK_ROPE].set(
        jnp.broadcast_to(eye[:, None, :], (QK_ROPE, H, QK_ROPE)))
    swap = jnp.zeros((QK_ROPE, QK_ROPE), F32).at[half:, :half].set(-jnp.eye(half)).at[:half, half:].set(jnp.eye(half))
    k_sw = jnp.zeros((rows, H, HPAD), F32)
    k_sw = k_sw.at[KV_LORA:KV_LORA + QK_ROPE, :, QK_NOPE:QK_NOPE + QK_ROPE].set(
        jnp.broadcast_to(swap[:, None, :], (QK_ROPE, H, QK_ROPE)))
    v_w = jnp.zeros((rows, H, V_DIM), F32).at[:KV_LORA].set(wkv[..., QK_NOPE:])
    wk_big = jnp.concatenate([k_lin.reshape(rows, H * HPAD), k_sw.reshape(rows, H * HPAD)], axis=1)
    wv_t = v_w.reshape(rows, H * V_DIM).T

    gkv = jnp.concatenate([ckv_g, jnp.ones((rows - KV_LORA,), F32)]).reshape(1, rows)
    return cq_g.reshape(1, Q_LORA), gkv, wq_big.astype(BF16), wk_big.astype(BF16), wv_t.astype(BF16)


def _rope_tables():
    half = QK_ROPE // 2
    freqs = (ROPE_THETA ** (-jnp.arange(0, QK_ROPE, 2, dtype=F32) / QK_ROPE)).reshape(half, 1)
    lane = jnp.arange(HEAD_PAD_C)[None, :]
    j = jnp.arange(half)[:, None]
    spread = (lane == QK_NOPE + j) | (lane == QK_NOPE + half + j)
    off_rope = ~jnp.any(spread, axis=0, keepdims=True)
    return freqs, spread.astype(BF16), off_rope.astype(F32)


def mla_prep(lat, positions, gq, gkv, wq_big, wk_big, wv_t, batch, seq):
    N = lat.shape[0]
    HP = N_HEADS_C * HEAD_PAD_C
    tm = 512
    tpb = seq // tm
    freqs, spread, off_rope = _rope_tables()
    pos_rows = positions.reshape(N // tm, 1, tm)
    const = lambda shape: pl.BlockSpec(shape, lambda i: (0, 0))
    return pl.pallas_call(
        _mla_prep_kernel,
        grid=(N // tm,),
        in_specs=[
            pl.BlockSpec((tm, LAT_W), lambda i: (i, 0)),
            pl.BlockSpec((1, 1, tm), lambda i: (i, 0, 0)),
            const(gq.shape), const(gkv.shape), const(wq_big.shape), const(wk_big.shape), const(wv_t.shape),
            const(freqs.shape), const(spread.shape), const(off_rope.shape),
        ],
        out_specs=[
            pl.BlockSpec((tm, HP), lambda i: (i, 0)),
            pl.BlockSpec((tm, HP), lambda i: (i, 0)),
            pl.BlockSpec((1, DC, tm), lambda i: (i // tpb, 0, i % tpb)),
        ],
        out_shape=[
            jax.ShapeDtypeStruct((N, HP), BF16),
            jax.ShapeDtypeStruct((N, HP), BF16),
            jax.ShapeDtypeStruct((batch, DC, seq), BF16),
        ],
        compiler_params=_cp(("parallel",), VMEM_LIMIT),
        name="mla_prep",
    )(lat, pos_rows, gq, gkv, wq_big, wk_big, wv_t, freqs, spread, off_rope)


HEADS_PER_STEP_C = 8
FLASH_Q_CHUNK = 256


def _mla_flash_kernel(qi_ref, ki_ref, q_ref, k_ref, vt_ref, o_ref, m_sc, l_sc, acc_sc):
    t = pl.program_id(2)
    qi, ki = qi_ref[t], ki_ref[t]

    @pl.when(ki == 0)
    def _():
        m_sc[...] = jnp.full(m_sc.shape, NEG, F32)
        l_sc[...] = jnp.zeros(l_sc.shape, F32)
        acc_sc[...] = jnp.zeros(acc_sc.shape, F32)

    def step(masked):
        T = q_ref.shape[1]
        if masked:
            key = lax.broadcasted_iota(jnp.int32, (T, T), 0)
            qry = lax.broadcasted_iota(jnp.int32, (T, T), 1)
            keep = key <= qry
        chains = [(h, c) for h in range(HEADS_PER_STEP_C) for c in range(T // FLASH_Q_CHUNK)]
        scores, probs, alphas = {}, {}, {}

        def keys_for(c):
            return (c + 1) * FLASH_Q_CHUNK if masked else T

        def qk(h, c):
            qs = slice(c * FLASH_Q_CHUNK, (c + 1) * FLASH_Q_CHUNK)
            q = q_ref[0, qs, h * HEAD_PAD_C:(h + 1) * HEAD_PAD_C]
            k = k_ref[0, :keys_for(c), h * HEAD_PAD_C:(h + 1) * HEAD_PAD_C]
            st = _nt_dot(k, q)
            scores[h, c] = jnp.where(keep[:keys_for(c), qs], st, NEG) if masked else st

        def softmax(h, c):
            qs = slice(c * FLASH_Q_CHUNK, (c + 1) * FLASH_Q_CHUNK)
            st = scores.pop((h, c))
            m_prev = m_sc[h, :, qs]
            m_new = jnp.maximum(m_prev, jnp.max(st, axis=0, keepdims=True))
            alpha = jnp.exp2(m_prev - m_new)
            p = jnp.exp2(st - m_new)
            l_sc[h, :, qs] = alpha * l_sc[h, :, qs] + jnp.sum(p, axis=0, keepdims=True)
            m_sc[h, :, qs] = m_new
            probs[h, c], alphas[h, c] = p.astype(BF16), alpha

        def pv(h, c):
            qs = slice(c * FLASH_Q_CHUNK, (c + 1) * FLASH_Q_CHUNK)
            vt = vt_ref[0, h * V_DIM:(h + 1) * V_DIM, :keys_for(c)]
            acc_sc[h, :, qs] = alphas.pop((h, c)) * acc_sc[h, :, qs] + jnp.dot(
                vt, probs.pop((h, c)), preferred_element_type=F32)

        for phase in (qk, softmax, pv):
            for ch in chains:
                phase(*ch)

    @pl.when(ki < qi)
    def _():
        step(False)

    @pl.when(ki == qi)
    def _():
        step(True)
        ot = jnp.concatenate([acc_sc[h] / l_sc[h] for h in range(HEADS_PER_STEP_C)], axis=0)
        o_ref[0] = ot.T.astype(o_ref.dtype)


def mla_attention(q_all, k_all, vt_all, batch, seq):
    T = 512
    nq = seq // T
    pairs = [(a, b) for a in range(nq) for b in range(a + 1)]
    qi_tab = jnp.asarray([p[0] for p in pairs], jnp.int32)
    ki_tab = jnp.asarray([p[1] for p in pairs], jnp.int32)
    hp = N_HEADS_C // HEADS_PER_STEP_C
    qw = HEADS_PER_STEP_C * HEAD_PAD_C
    vw = HEADS_PER_STEP_C * V_DIM
    q3 = q_all.reshape(batch, seq, -1)
    k3 = k_all.reshape(batch, seq, -1)
    grid_spec = pltpu.PrefetchScalarGridSpec(
        num_scalar_prefetch=2,
        grid=(batch, hp, len(pairs)),
        in_specs=[
            pl.BlockSpec((1, T, qw), lambda b, h, t, qi, ki: (b, qi[t], h)),
            pl.BlockSpec((1, T, qw), lambda b, h, t, qi, ki: (b, ki[t], h)),
            pl.BlockSpec((1, vw, T), lambda b, h, t, qi, ki: (b, h, ki[t])),
        ],
        out_specs=pl.BlockSpec((1, T, vw), lambda b, h, t, qi, ki: (b, qi[t], h)),
        scratch_shapes=[
            pltpu.VMEM((HEADS_PER_STEP_C, 1, T), F32),
            pltpu.VMEM((HEADS_PER_STEP_C, 1, T), F32),
            pltpu.VMEM((HEADS_PER_STEP_C, V_DIM, T), F32),
        ],
    )
    o = pl.pallas_call(
        _mla_flash_kernel,
        grid_spec=grid_spec,
        out_shape=jax.ShapeDtypeStruct((batch, seq, DC), BF16),
        compiler_params=_cp(("parallel", "parallel", "arbitrary")),
        name="mla_attention",
    )(qi_tab, ki_tab, q3, k3, vt_all)
    return o.reshape(batch * seq, DC)


def _mixout_kernel(x_ref, gates_ref, ub_ref, ubh_ref, o1_ref, o2_ref, o3_ref, l1_ref, l2_ref, l3_ref, yc_ref,
                   mod1_ref, mod2_ref, g2_ref, poolw_ref, pscale_ref, woa_ref, wob_ref, woc_ref, wout_ref,
                   rwt_ref, sw1_ref, sw3_ref, sw2_ref, spread_ref,
                   xmid_ref, h2a_ref, h2b_ref, logit_ref, *scratch, tiles_per_batch):
    D = x_ref.shape[1]
    tm = x_ref.shape[0]
    tile = pl.program_id(0) % tiles_per_batch
    o_scrs, l_scrs = scratch[:3], scratch[3:]

    def token_order(ref, scr):
        dil, rows, width = ref.shape[1:]
        if dil == 1:
            return ref[0, 0].astype(F32)
        for r in range(dil):
            v = ref[0, r].astype(F32)
            for c in range(width // LANES):
                scr[c, pl.ds(r, rows, stride=dil), :] = v[:, c * LANES:(c + 1) * LANES]
        return jnp.concatenate([scr[c] for c in range(width // LANES)], axis=1)

    outs = [token_order(r, s) for r, s in zip((o1_ref, o2_ref, o3_ref), o_scrs)]
    l1, l2, l3 = [token_order(r, s) for r, s in zip((l1_ref, l2_ref, l3_ref), l_scrs)]
    mx = jnp.maximum(jnp.maximum(l1, l2), l3)
    es = [jnp.exp(l1 - mx), jnp.exp(l2 - mx), jnp.exp(l3 - mx)]
    inv = 1.0 / (es[0] + es[1] + es[2])
    ya = jnp.zeros((tm, GROUP_W), F32)
    for e, o in zip(es, outs):
        w = e * inv
        w_hi = w.astype(BF16)
        w_lo = (w - w_hi.astype(F32)).astype(BF16)
        w_wide = (jnp.dot(w_hi, spread_ref[...], preferred_element_type=F32)
                  + jnp.dot(w_lo, spread_ref[...], preferred_element_type=F32))
        ya = ya + w_wide * o
    a_out = jnp.dot(ya.astype(BF16), woa_ref[...], preferred_element_type=F32)

    u = ub_ref[...].astype(F32)
    halo = jnp.where(tile > 0, ubh_ref[...].astype(F32), 0.0)
    ext = jnp.concatenate([halo, u], axis=0)
    t_seq = tile * tm + lax.broadcasted_iota(jnp.int32, (tm, 1), 0)
    pooled = []
    for gi, w in enumerate(POOL_WINDOWS):
        sl = slice(gi * POOL_GROUP_DIM, (gi + 1) * POOL_GROUP_DIM)
        acc = ext[:, sl]
        k = 1
        while k < w:
            acc = acc + pltpu.roll(acc, k, axis=0)
            k *= 2
        cnt = jnp.minimum(t_seq + 1, w).astype(F32)
        pg = acc[POOL_HALO:] / cnt - u[:, sl]
        pooled.append(jnp.dot(pg.astype(BF16), poolw_ref[gi], preferred_element_type=F32))
    yb = jnp.concatenate(pooled, axis=1) * pscale_ref[...]
    b_out = jnp.dot(yb.astype(BF16), wob_ref[...], preferred_element_type=F32)
    c_out = jnp.dot(yc_ref[...], woc_ref[...], preferred_element_type=F32)

    g = gates_ref[...].astype(F32)
    mix = (jax.nn.sigmoid(g[:, :D]) * a_out + jax.nn.sigmoid(g[:, D:2 * D]) * b_out
           + jax.nn.sigmoid(g[:, 2 * D:]) * c_out)
    tok = jnp.dot(mix.astype(BF16), wout_ref[...], preferred_element_type=F32)
    xn = x_ref[...] + mod1_ref[0][:, 2 * D:] * tok

    mod2 = mod2_ref[0]
    y = xn * lax.rsqrt(jnp.mean(xn * xn, axis=-1, keepdims=True) + EPS) * g2_ref[...]
    h2 = y * (1.0 + mod2[:, D:2 * D]) + mod2[:, :D]
    h2b = h2.astype(BF16)
    h2a_ref[...], h2b_ref[...] = _pack_row_halves(h2b)
    logit_ref[...] = _nt_dot(rwt_ref[...], h2b)
    hid = _silu(jnp.dot(h2b, sw1_ref[...], preferred_element_type=F32)) * jnp.dot(
        h2b, sw3_ref[...], preferred_element_type=F32)
    shared = jnp.dot(hid.astype(BF16), sw2_ref[...], preferred_element_type=F32)
    xmid_ref[...] = xn + mod2[:, 2 * D:] * shared


def mix_out(x2, gu, dil, yc, mod1, mod2, g2, pool_w, pool_scale, w_oa, w_ob, w_oc, w_out, rwt, sw1, sw3, sw2, seq,
            row0=0):
    D = x2.shape[1]
    N = gu.shape[0]
    tm = 512
    tpb = seq // tm
    tile0 = row0 // tm
    (o1, l1), (o2, l2), (o3, l3) = dil
    row = lambda w, c=0: pl.BlockSpec((tm, w), lambda i: (i, c))
    by_residue = lambda a: pl.BlockSpec(
        (1, a.shape[1], tm // a.shape[1], a.shape[3]), lambda i: (i // tpb, 0, i % tpb, 0))
    heads = HEADS_PER_GROUP_A
    spread = (jnp.arange(LSE_LANES)[:, None] == (jnp.arange(GROUP_W)[None, :] // HEAD_DIM_A) * (LSE_LANES // heads)
              ).astype(BF16)
    const2 = lambda a: pl.BlockSpec(a.shape, lambda i: (0,) * a.ndim, pipeline_mode=pl.Buffered(1))
    modspec = pl.BlockSpec((1, 1, 3 * D), lambda i: (i // tpb, 0, 0))
    ub_col = 3 * D // DB
    halo_spec = pl.BlockSpec(
        (POOL_HALO, DB), lambda i: (jnp.maximum(i * (tm // POOL_HALO) - 1, 0), ub_col))
    weights = [g2.reshape(1, D), pool_w, pool_scale.reshape(1, DB), w_oa, w_ob, w_oc, w_out, rwt, sw1, sw3, sw2,
               spread]
    return pl.pallas_call(
        functools.partial(_mixout_kernel, tiles_per_batch=tpb),
        grid=(N // tm,),
        in_specs=[
            pl.BlockSpec((tm, D), lambda i: (i + tile0, 0)), row(3 * D), row(DB, ub_col), halo_spec,
            by_residue(o1), by_residue(o2), by_residue(o3), by_residue(l1), by_residue(l2), by_residue(l3), row(DC),
            modspec, modspec,
        ] + [const2(a) for a in weights],
        scratch_shapes=[pltpu.VMEM((GROUP_W // LANES, tm, LANES), F32)] * 3
        + [pltpu.VMEM((LSE_LANES // LANES, tm, LANES), F32)] * 3,
        out_specs=[row(D), row(PACK_W), row(PACK_W), pl.BlockSpec((N_EXPERTS, tm), lambda i: (0, i))],
        out_shape=[
            jax.ShapeDtypeStruct((N, D), F32),
            jax.ShapeDtypeStruct((N, PACK_W), jnp.int32),
            jax.ShapeDtypeStruct((N, PACK_W), jnp.int32),
            jax.ShapeDtypeStruct((N_EXPERTS, N), F32),
        ],
        compiler_params=_cp(("parallel",), VMEM_LIMIT),
        name="mix_out",
    )(x2, gu, gu, gu, o1, o2, o3, l1, l2, l3, yc, mod1, mod2, *weights)


def _pick_rows(table, picks):
    G, GS = N_GROUPS, GROUP_SIZE
    eio = lax.broadcasted_iota(jnp.int32, (GS, table.shape[1]), 0)
    rows = []
    for k in range(TOP_K):
        idx = picks[k:k + 1]
        parts = [jnp.sum(jnp.where(eio + g * GS == idx, table[g * GS:(g + 1) * GS], 0.0), axis=0, keepdims=True)
                 for g in range(G)]
        rows.append(functools.reduce(jnp.add, parts))
    return jnp.concatenate(rows, axis=0)


def _route_choose(lg_ref, bias_ref):
    G, GS = N_GROUPS, GROUP_SIZE
    scores = jax.nn.sigmoid(lg_ref[...])
    sel = scores + bias_ref[...]
    tn = sel.shape[1]
    eio = lax.broadcasted_iota(jnp.int32, (GS, tn), 0)
    ninf = -jnp.inf

    gs = []
    for g in range(G):
        v = sel[g * GS:(g + 1) * GS]
        m1 = jnp.max(v, axis=0, keepdims=True)
        i1 = jnp.min(jnp.where(v == m1, eio, GS), axis=0, keepdims=True)
        m2 = jnp.max(jnp.where(eio == i1, ninf, v), axis=0, keepdims=True)
        gs.append(m1 + m2)
    gsm = jnp.concatenate(gs, axis=0)
    gio = lax.broadcasted_iota(jnp.int32, (G, tn), 0)
    rank = jnp.zeros((G, tn), jnp.int32)
    for g2 in range(G):
        beats = (gs[g2] > gsm) | ((gs[g2] == gsm) & (g2 < gio))
        rank = rank + beats.astype(jnp.int32)
    gsel = rank < TOPK_GROUPS

    vs = [jnp.where(gsel[g:g + 1], sel[g * GS:(g + 1) * GS], NEG) for g in range(G)]
    eid = [eio + g * GS for g in range(G)]
    chosen = [jnp.zeros((GS, tn), jnp.bool_) for _ in range(G)]
    picks = []
    for _ in range(TOP_K):
        m = functools.reduce(jnp.maximum, [jnp.max(v, axis=0, keepdims=True) for v in vs])
        idx = functools.reduce(jnp.minimum, [
            jnp.min(jnp.where(v == m, e, N_EXPERTS), axis=0, keepdims=True) for v, e in zip(vs, eid)])
        picks.append(idx)
        for g in range(G):
            hit = eid[g] == idx
            chosen[g] = chosen[g] | hit
            vs[g] = jnp.where(hit, ninf, vs[g])
    mask = jnp.concatenate(chosen, axis=0).astype(F32)
    return scores, jnp.concatenate(picks, axis=0), mask


def _route_kernel(lg_ref, bias_ref, tri_ref, dest_ref, w_ref, cnt_ref, run_sc, start_sc, mask_sc, picks_sc,
                  *, slot_block):
    phase = pl.program_id(0)
    step = pl.program_id(1)
    tn = lg_ref.shape[1]
    cols = pl.ds(pl.multiple_of(step * tn, tn), tn)

    @pl.when(phase == 0)
    def _():
        @pl.when(step == 0)
        def _():
            run_sc[...] = jnp.zeros(run_sc.shape, F32)

        scores, picks, mask = _route_choose(lg_ref, bias_ref)
        wk = _pick_rows(scores, picks)
        w_ref[0] = wk / jnp.sum(wk, axis=0, keepdims=True) * ROUTED_SCALE
        dest_ref[0] = jnp.zeros(dest_ref.shape[1:], dest_ref.dtype)
        mask_sc[:, cols] = mask.astype(BF16)
        picks_sc[:, cols] = picks
        run_sc[...] = run_sc[...] + jnp.sum(mask, axis=1, keepdims=True)

    @pl.when(phase == 1)
    def _():
        @pl.when(step == 0)
        def _():
            counts = run_sc[...].astype(jnp.int32)
            cnt_ref[...] = jnp.broadcast_to(counts, cnt_ref.shape)
            shift = slot_block.bit_length() - 1
            padded = lax.shift_left(lax.shift_right_logical(counts + (slot_block - 1), shift), shift).astype(F32)
            r = lax.broadcasted_iota(jnp.int32, (N_EXPERTS, N_EXPERTS), 0)
            c = lax.broadcasted_iota(jnp.int32, (N_EXPERTS, N_EXPERTS), 1)
            as_row = jnp.sum(jnp.where(r == c, padded, 0.0), axis=0, keepdims=True)
            start_sc[...] = jnp.sum(jnp.where(c < r, as_row, 0.0), axis=1, keepdims=True)
            run_sc[...] = jnp.zeros(run_sc.shape, F32)

        mask_b = mask_sc[:, cols]
        mask = mask_b.astype(F32)
        before = jnp.dot(mask_b, tri_ref[...], preferred_element_type=F32) - mask
        slot = start_sc[...] + run_sc[...] + before
        dest_ref[0] = _pick_rows(slot, picks_sc[:, cols]).astype(jnp.int32)
        w_ref[0] = jnp.zeros(w_ref.shape[1:], w_ref.dtype)
        run_sc[...] = run_sc[...] + jnp.sum(mask, axis=1, keepdims=True)


SLOT_BLOCK = 512


def route(logits_t, bias):
    E, N = logits_t.shape
    tn = 1024
    tri = (jnp.arange(tn)[:, None] <= jnp.arange(tn)[None, :]).astype(BF16)
    plane = lambda: pl.BlockSpec((1, TOP_K, tn), lambda p, i: (p, 0, i))
    dest, w, cnt = pl.pallas_call(
        functools.partial(_route_kernel, slot_block=SLOT_BLOCK),
        grid=(2, N // tn),
        in_specs=[
            pl.BlockSpec((E, tn), lambda p, i: (0, i * (1 - p))),
            pl.BlockSpec((E, 1), lambda p, i: (0, 0)),
            pl.BlockSpec((tn, tn), lambda p, i: (0, 0)),
        ],
        out_specs=[plane(), plane(), pl.BlockSpec((E, 128), lambda p, i: (0, 0))],
        out_shape=[
            jax.ShapeDtypeStruct((2, TOP_K, N), jnp.int32),
            jax.ShapeDtypeStruct((2, TOP_K, N), F32),
            jax.ShapeDtypeStruct((E, 128), jnp.int32),
        ],
        scratch_shapes=[pltpu.VMEM((E, 1), F32), pltpu.VMEM((E, 1), F32),
                        pltpu.VMEM((E, N), BF16), pltpu.VMEM((TOP_K, N), jnp.int32)],
        compiler_params=_cp(("arbitrary", "arbitrary")),
        name="route",
    )(logits_t, bias.reshape(E, 1), tri)
    return dest[1], w[0], cnt[:, 0]


def block_tables(counts, n_tokens):
    E = counts.shape[0]
    blk = SLOT_BLOCK
    nblk = (n_tokens * TOP_K + E * blk) // blk
    per_expert = (counts + blk - 1) // blk
    bend = jnp.cumsum(per_expert)
    bstart = bend - per_expert
    b = jnp.arange(nblk, dtype=jnp.int32)[:, None]
    owns = (bstart[None, :] <= b) & (b < bend[None, :])
    blk_e = jnp.minimum(jnp.sum(bend[None, :] <= b, axis=1), E - 1).astype(jnp.int32)
    rows_left = counts[None, :] - (b - bstart[None, :]) * blk
    nvalid = jnp.sum(jnp.where(owns, jnp.clip(rows_left, 0, blk), 0), axis=1)
    first = jnp.concatenate([jnp.ones((1,), jnp.bool_), blk_e[1:] != blk_e[:-1]])
    run_parity = ((jnp.cumsum(first.astype(jnp.int32)) - 1) % 2).astype(jnp.int32)
    later = blk_e[None, :] > blk_e[:, None]
    next_e = jnp.min(jnp.where(later, blk_e[None, :], E), axis=1).astype(jnp.int32)
    return blk_e, nvalid.astype(jnp.int32), run_parity, next_e


def _sc_mesh():
    return plsc.VectorSubcoreMesh(core_axis_name="c", subcore_axis_name="s")


SC_WINDOW = 128


def sc_scatter_rows(x, dest, n_slots):
    N, W = x.shape
    K = dest.shape[0]

    @functools.partial(pl.kernel, out_type=jax.ShapeDtypeStruct((n_slots, W), x.dtype), mesh=_sc_mesh(),
                       scratch_types=[])
    def scatter(x_hbm, i_hbm, o_hbm):
        def body(x_vmem, i_vmem):
            for k in range(K):
                pltpu.sync_copy(x_vmem, o_hbm.at[i_vmem.at[k]])

        pltpu.emit_pipeline(
            body,
            grid=(N // SC_WINDOW,),
            in_specs=[pl.BlockSpec((SC_WINDOW, W), lambda i: (i, 0)),
                      pl.BlockSpec((K, SC_WINDOW), lambda i: (0, i))],
            out_specs=[],
            core_axis_name=("c", "s"),
            dimension_semantics=(pltpu.PARALLEL,),
        )(x_hbm, i_hbm)

    return scatter(x, dest)


SC_LANES = 16
SC_GATHER_TOKENS = 8


def sc_weighted_gather(y, dest, wts):
    W = y.shape[1]
    K, N = dest.shape
    G, L = SC_GATHER_TOKENS, SC_LANES
    batches = SC_WINDOW // G

    @functools.partial(
        pl.kernel, out_type=jax.ShapeDtypeStruct((N, W), y.dtype), mesh=_sc_mesh(),
        scratch_types=[pltpu.VMEM((2, K, G, W), y.dtype), pltpu.SemaphoreType.DMA((2,))],
        compiler_params=pltpu.CompilerParams(needs_layout_passes=False))
    def gather(y_hbm, i_hbm, w_hbm, o_hbm, rows2, sems):
        def body(i_vmem, w_vmem, o_vmem):
            def fetch(batch, slot):
                return [pltpu.make_async_copy(y_hbm.at[i_vmem.at[k, pl.ds(batch * G, G)]], rows2.at[slot, k],
                                              sems.at[slot]) for k in range(K)]

            for c in fetch(0, 0):
                c.start()

            @pl.loop(0, batches)
            def _(batch):
                slot = batch % 2

                @pl.when(batch + 1 < batches)
                def _():
                    for c in fetch(batch + 1, 1 - slot):
                        c.start()

                for c in fetch(batch, slot):
                    c.wait()
                rows = rows2.at[slot]

                @pl.loop(0, G)
                def _(t):
                    tok = jnp.full((L,), batch * G + t, jnp.int32)
                    wk = [plsc.load_gather(w_vmem, [jnp.full((L,), k, jnp.int32), tok]) for k in range(K)]

                    @plsc.parallel_loop(0, W // L, unroll=4)
                    def _(j):
                        lo = jnp.zeros((L,), F32)
                        hi = jnp.zeros((L,), F32)
                        for k in range(K):
                            pair = plsc.bitcast(rows[k, t, pl.ds(j * L, L)], BF16)
                            a, b = plsc.unpack(pair, format=plsc.PackFormat.INTERLEAVED)
                            lo = lo + wk[k] * a
                            hi = hi + wk[k] * b
                        o_vmem[batch * G + t, pl.ds(j * L, L)] = plsc.bitcast(
                            plsc.pack(lo, hi, format=plsc.PackFormat.INTERLEAVED), y.dtype)

        pltpu.emit_pipeline(
            body,
            grid=(N // SC_WINDOW,),
            in_specs=[pl.BlockSpec((K, SC_WINDOW), lambda i: (0, i)),
                      pl.BlockSpec((K, SC_WINDOW), lambda i: (0, i))],
            out_specs=[pl.BlockSpec((SC_WINDOW, W), lambda i: (i, 0))],
            core_axis_name=("c", "s"),
            dimension_semantics=(pltpu.PARALLEL,),
        )(i_hbm, w_hbm, o_hbm)

    return gather(y, dest, wts)


EXPERT_INPUT_SLOTS = 3


def _expert_kernel(blk_e_ref, nvalid_ref, parity_ref, next_e_ref, xa_hbm, xb_hbm, w1_hbm, w3_hbm, w2_hbm,
                   ya_ref, yb_ref, w1_sc, w3_sc, w2_sc, xa_buf, xb_buf, sems, w1_st, w3_st, w2_st, wsems,
                   *, layer):
    b = pl.program_id(0)
    nb = pl.num_programs(0)
    nv = nvalid_ref[b]
    prev_e = blk_e_ref[jnp.maximum(b - 1, 0)]
    blk = xa_buf.shape[1]
    ring = EXPERT_INPUT_SLOTS
    n_experts = w1_hbm.shape[1]

    def fetch_weights(e, par):
        return (pltpu.make_async_copy(w1_hbm.at[layer, e], w1_st.at[par], wsems.at[par, 0]),
                pltpu.make_async_copy(w3_hbm.at[layer, e], w3_st.at[par], wsems.at[par, 1]),
                pltpu.make_async_copy(w2_hbm.at[layer, e], w2_st.at[par], wsems.at[par, 2]))

    def fetch(block, slot):
        rows = pl.ds(pl.multiple_of(block * blk, blk), blk)
        return (pltpu.make_async_copy(xa_hbm.at[rows], xa_buf.at[slot], sems.at[slot, 0]),
                pltpu.make_async_copy(xb_hbm.at[rows], xb_buf.at[slot], sems.at[slot, 1]))

    @pl.when(b == 0)
    def _():
        for i in range(ring - 1):
            for c in fetch(i, i):
                c.start()

    ahead = b + (ring - 1)

    @pl.when(ahead < nb)
    def _():
        for c in fetch(ahead, ahead % ring):
            c.start()

    slot = b % ring
    for c in fetch(b, slot):
        c.wait()

    @pl.when(b == 0)
    def _():
        for c in fetch_weights(blk_e_ref[0], 0):
            c.start()

    @pl.when((b == 0) | (blk_e_ref[b] != prev_e))
    def _():
        par = parity_ref[b]
        nxt = next_e_ref[b]

        @pl.when(nxt < n_experts)
        def _():
            for c in fetch_weights(nxt, 1 - par):
                c.start()

        for c in fetch_weights(blk_e_ref[b], par):
            c.wait()
        w1_sc[...] = w1_st[par].astype(BF16)
        w3_sc[...] = w3_st[par].astype(BF16)
        w2_sc[...] = w2_st[par].astype(BF16)

    @pl.when(nv > 0)
    def _():
        x = _unpack_row_halves(xa_buf[slot], xb_buf[slot])
        rows = lax.broadcasted_iota(jnp.int32, x.shape, 0)
        x = jnp.where(rows < nv, x, 0.0).astype(BF16)
        hid = _silu(jnp.dot(x, w1_sc[...], preferred_element_type=F32)) * jnp.dot(
            x, w3_sc[...], preferred_element_type=F32)
        y = jnp.dot(hid.astype(BF16), w2_sc[...], preferred_element_type=F32)
        ya_ref[...], yb_ref[...] = _pack_row_halves(y)

    @pl.when(nv == 0)
    def _():
        ya_ref[...] = jnp.zeros(ya_ref.shape, ya_ref.dtype)
        yb_ref[...] = jnp.zeros(yb_ref.shape, yb_ref.dtype)


def routed_experts(xa, xb, tables, w1, w3, w2, layer):
    P = xa.shape[0]
    blk = SLOT_BLOCK
    _, E, D, FF = w1.shape
    slots = lambda: pl.BlockSpec((blk, PACK_W), lambda b, *_: (b, 0))
    grid_spec = pltpu.PrefetchScalarGridSpec(
        num_scalar_prefetch=len(tables),
        grid=(P // blk,),
        in_specs=[pl.BlockSpec(memory_space=pl.ANY)] * 5,
        out_specs=[slots(), slots()],
        scratch_shapes=[
            pltpu.VMEM((D, FF), BF16), pltpu.VMEM((D, FF), BF16), pltpu.VMEM((FF, D), BF16),
            pltpu.VMEM((EXPERT_INPUT_SLOTS, blk, PACK_W), jnp.int32),
            pltpu.VMEM((EXPERT_INPUT_SLOTS, blk, PACK_W), jnp.int32),
            pltpu.SemaphoreType.DMA((EXPERT_INPUT_SLOTS, 2)),
            pltpu.VMEM((2, D, FF), F32), pltpu.VMEM((2, D, FF), F32), pltpu.VMEM((2, FF, D), F32),
            pltpu.SemaphoreType.DMA((2, 3)),
        ],
    )
    return pl.pallas_call(
        functools.partial(_expert_kernel, layer=layer),
        grid_spec=grid_spec,
        out_shape=[jax.ShapeDtypeStruct((P, PACK_W), jnp.int32)] * 2,
        compiler_params=_cp(("arbitrary",), VMEM_LIMIT),
        name="routed_experts",
    )(*tables, xa, xb, w1, w3, w2)


def _combine_kernel(xmid_ref, ra_ref, rb_ref, mod2_ref, fg_ref, *rest, final):
    out_ref = rest[-1]
    D = xmid_ref.shape[1]
    x = xmid_ref[...] + mod2_ref[0][:, 2 * D:] * _unpack_row_halves(ra_ref[...], rb_ref[...])
    if final:
        x = x * lax.rsqrt(jnp.mean(x * x, axis=-1, keepdims=True) + EPS) * fg_ref[...]
    out_ref[...] = x


def combine(xmid, ra, rb, mod2, final_g, seq, final, out_rows=None, row0=0, out_buf=None):
    N, D = xmid.shape
    tm = 512
    tpb = seq // tm
    tile0 = row0 // tm
    in_specs = [
        pl.BlockSpec((tm, D), lambda i: (i, 0)),
        pl.BlockSpec((tm, PACK_W), lambda i: (i, 0)),
        pl.BlockSpec((tm, PACK_W), lambda i: (i, 0)),
        pl.BlockSpec((1, 1, 3 * D), lambda i: (i // tpb, 0, 0)),
        pl.BlockSpec((1, D), lambda i: (0, 0)),
    ]
    args = [xmid, ra, rb, mod2, final_g.reshape(1, D)]
    aliases = {}
    if out_buf is not None:
        in_specs.append(pl.BlockSpec(memory_space=pl.ANY))
        args.append(out_buf)
        aliases = {len(args) - 1: 0}
    return pl.pallas_call(
        functools.partial(_combine_kernel, final=final),
        grid=(N // tm,),
        in_specs=in_specs,
        out_specs=pl.BlockSpec((tm, D), lambda i: (i + tile0, 0)),
        out_shape=jax.ShapeDtypeStruct((out_rows or N, D), F32),
        input_output_aliases=aliases,
        compiler_params=_cp(("parallel",), VMEM_LIMIT),
        name="combine",
    )(*args)


TOKEN_STREAMS = 2


def _permute_w_in(w):
    ub = w[:, 3 * DA:3 * DA + DB]
    lat_lo = 3 * DA + DB
    lat_hi = lat_lo + Q_LORA + KV_LORA + QK_ROPE
    lat, gates = w[:, lat_lo:lat_hi], w[:, lat_hi:]
    pad = jnp.zeros((w.shape[0], LAT_W - (lat_hi - lat_lo)), w.dtype)
    parts = [gates, ub, lat, pad]
    for g in range(len(DIL_GROUPS)):
        sl = slice(g * GROUP_W, (g + 1) * GROUP_W)
        parts += [w[:, :DA][:, sl] * (HEAD_DIM_A ** -0.5), w[:, DA:2 * DA][:, sl], w[:, 2 * DA:3 * DA][:, sl]]
    return jnp.concatenate(parts, axis=1).astype(BF16)


def kernel(x, c, positions, ada_mix_w, ada_mix_b, norm_mix_g, w_in, pool_w, pool_scale, cq_norm_g, ckv_norm_g, w_uq, w_ukv, w_oa, w_ob, w_oc, w_out, ada_ffn_w, ada_ffn_b, norm_ffn_g, router_w, router_bias, exp_w1, exp_w3, exp_w2, sh_w1, sh_w3, sh_w2, final_g):
    B, S, D = x.shape
    depth = w_in.shape[0]
    mod_mix = adaln_rows(c, ada_mix_w, ada_mix_b)
    mod_ffn = adaln_rows(c, ada_ffn_w, ada_ffn_b)
    streams = TOKEN_STREAMS if B % TOKEN_STREAMS == 0 else 1
    Bs = B // streams
    Ns = Bs * S
    x_all = x.reshape(B * S, D)
    xs = [None] * streams
    out_all = None
    pos_s = [positions[s * Bs:(s + 1) * Bs] for s in range(streams)]
    for l in range(depth):
        last = l == depth - 1
        w_in_l = _permute_w_in(w_in[l])
        mla_w = _mla_weights(cq_norm_g[l], ckv_norm_g[l], w_uq[l], w_ukv[l])
        mix_w = (norm_ffn_g[l], pool_w[l].astype(BF16), pool_scale[l],
                 w_oa[l].astype(BF16), w_ob[l].astype(BF16), w_oc[l].astype(BF16), w_out[l].astype(BF16),
                 router_w[l].T.astype(BF16), sh_w1[l].astype(BF16), sh_w3[l].astype(BF16), sh_w2[l].astype(BF16))
        for s in range(streams):
            mod1 = mod_mix[l, s * Bs:(s + 1) * Bs].reshape(Bs, 1, 3 * D)
            mod2 = mod_ffn[l, s * Bs:(s + 1) * Bs].reshape(Bs, 1, 3 * D)
            if l == 0:
                x2, row0 = x_all, s * Ns
                gu, lat, *qkv = in_projection(x2, norm_mix_g[l], mod1, w_in_l, S, row0)
            else:
                row0 = 0
                x2, gu, lat, *qkv = in_projection(xs[s][0], norm_mix_g[l], mod1, w_in_l, S, 0, xs[s][1:])
            dil = [dilated_attention(qkv[2 * g], qkv[2 * g + 1]) for g in range(len(DIL_GROUPS))]
            q_all, k_all, vt_all = mla_prep(lat, pos_s[s], *mla_w, Bs, S)
            yc = mla_attention(q_all, k_all, vt_all, Bs, S)
            xmid, h2a, h2b, logits_t = mix_out(x2, gu, dil, yc, mod1, mod2, *mix_w, S, row0)
            dest, w_k, counts = route(logits_t, router_bias[l])
            tables = block_tables(counts, Ns)
            n_slots = tables[0].shape[0] * SLOT_BLOCK
            xa = sc_scatter_rows(h2a, dest, n_slots)
            xb = sc_scatter_rows(h2b, dest, n_slots)
            ya, yb = routed_experts(xa, xb, tables, exp_w1, exp_w3, exp_w2, l)
            ra = sc_weighted_gather(ya, dest, w_k)
            rb = sc_weighted_gather(yb, dest, w_k)
            if last:
                out_all = combine(xmid, ra, rb, mod2, final_g, S, True, B * S, s * Ns, out_all)
            else:
                xs[s] = (xmid, ra, rb, mod2)
    return out_all.reshape(B, S, D)
```

```python
import functools
import math

import jax
import jax.numpy as jnp
from jax import lax
from jax.experimental import pallas as pl
from jax.experimental.pallas import tpu as pltpu
from jax.experimental.pallas import tpu_sc as plsc

F32 = jnp.float32
BF16 = jnp.bfloat16
HIGHEST = lax.Precision.HIGHEST

D_MODEL = 1024
HEAD_DIM_A = 64
HEADS_PER_GROUP_A = 4
DIL_GROUPS = ((128, 1), (512, 4), (2048, 16))
GROUP_W = HEADS_PER_GROUP_A * HEAD_DIM_A
DA = GROUP_W * len(DIL_GROUPS)
POOL_WINDOWS = (2, 4, 8, 16)
POOL_GROUP_DIM = 128
DB = POOL_GROUP_DIM * len(POOL_WINDOWS)
POOL_HALO = 16
N_HEADS_C = 8
QK_NOPE = 64
QK_ROPE = 32
V_DIM = 64
Q_LORA = 384
KV_LORA = 256
DC = N_HEADS_C * V_DIM
HEAD_PAD_C = 128
ROPE_THETA = 10000.0
N_EXPERTS = 64
TOP_K = 8
N_GROUPS = 8
TOPK_GROUPS = 4
GROUP_SIZE = N_EXPERTS // N_GROUPS
ROUTED_SCALE = 2.5
EPS = 1e-6
NEG = -1e30
Q_BLOCK = 128

LAT_W = 768
GU_W = 3 * D_MODEL + DB
IN_OUT_WIDTHS = (GU_W, LAT_W) + (2 * GROUP_W, GROUP_W) * len(DIL_GROUPS)

VMEM_LIMIT = 56 * 1024 * 1024


def _cp(sem, vmem=None):
    return pltpu.CompilerParams(dimension_semantics=sem, vmem_limit_bytes=vmem)


def _silu(v):
    return v * jax.nn.sigmoid(v)


def _nt_dot(a, b):
    return lax.dot_general(a, b, (((1,), (1,)), ((), ())), preferred_element_type=F32)


PACK_W = D_MODEL // 4
_HI_MASK = -65536


def _bf16_bits(v):
    return lax.bitcast_convert_type(v.astype(BF16).astype(F32), jnp.int32)


def _pack_row_halves(v):
    halves = []
    for h in range(2):
        lo = _bf16_bits(v[:, (2 * h) * PACK_W:(2 * h + 1) * PACK_W])
        hi = _bf16_bits(v[:, (2 * h + 1) * PACK_W:(2 * h + 2) * PACK_W])
        halves.append(lax.shift_right_logical(lo, 16) | (hi & _HI_MASK))
    return halves


def _unpack_row_halves(wa, wb):
    parts = []
    for w in (wa, wb):
        parts.append(lax.bitcast_convert_type(lax.shift_left(w, 16), F32))
        parts.append(lax.bitcast_convert_type(w & _HI_MASK, F32))
    return jnp.concatenate(parts, axis=1)


def _adaln_kernel(c_ref, w_ref, b_ref, o_ref):
    s = _silu(c_ref[...])
    o_ref[0] = jnp.dot(s, w_ref[0], preferred_element_type=F32, precision=HIGHEST) + b_ref[0]


def adaln_rows(c, w, b):
    L, D, D3 = w.shape
    B = c.shape[0]
    tn = 1024
    return pl.pallas_call(
        _adaln_kernel,
        grid=(L, D3 // tn),
        in_specs=[
            pl.BlockSpec((B, D), lambda l, j: (0, 0)),
            pl.BlockSpec((1, D, tn), lambda l, j: (l, 0, j)),
            pl.BlockSpec((1, 1, tn), lambda l, j: (l, 0, j)),
        ],
        out_specs=pl.BlockSpec((1, B, tn), lambda l, j: (l, 0, j)),
        out_shape=jax.ShapeDtypeStruct((L, B, D3), F32),
        compiler_params=_cp(("parallel", "parallel")),
        name="adaln_rows",
    )(c, w, b.reshape(L, 1, D3))


LANES = 128


def _inproj_kernel(x_ref, g_ref, mod_ref, w_ref, *refs, chunk, pending):
    o_refs, scr = refs[:-1], refs[-1]
    D = x_ref.shape[1]
    x = x_ref[...]
    if pending:
        ra_ref, rb_ref, gate_ref, x_out_ref, *o_refs = o_refs
        x = x + gate_ref[0][:, 2 * D:] * _unpack_row_halves(ra_ref[...], rb_ref[...])
        x_out_ref[...] = x
    y = x * lax.rsqrt(jnp.mean(x * x, axis=-1, keepdims=True) + EPS) * g_ref[...]
    mod = mod_ref[0]
    h = (y * (1.0 + mod[:, D:2 * D]) + mod[:, :D]).astype(BF16)
    col = 0
    for o_ref in o_refs:
        width = o_ref.shape[-1]
        if o_ref.ndim == 2:
            for c0 in range(0, width, chunk):
                cw = min(chunk, width - c0)
                o_ref[:, c0:c0 + cw] = jnp.dot(
                    h, w_ref[:, col + c0:col + c0 + cw], preferred_element_type=F32).astype(o_ref.dtype)
        else:
            dil, rows = o_ref.shape[1], o_ref.shape[2]
            z = jnp.dot(h, w_ref[:, col:col + width], preferred_element_type=F32)
            if dil == 1:
                o_ref[0, 0] = z.astype(o_ref.dtype)
            else:
                for c in range(width // LANES):
                    scr[c] = z[:, c * LANES:(c + 1) * LANES]
                for r in range(dil):
                    o_ref[0, r] = jnp.concatenate(
                        [scr[c, pl.ds(r, rows, stride=dil), :] for c in range(width // LANES)],
                        axis=1).astype(o_ref.dtype)
        col += width


def in_projection(x2, g, mod, w, seq, row0=0, pending=None):
    D = x2.shape[1]
    B = mod.shape[0]
    N = B * seq
    tm = 512
    tpb = seq // tm
    tile0 = row0 // tm
    out_specs = [pl.BlockSpec((tm, wd), lambda i: (i, 0)) for wd in IN_OUT_WIDTHS[:2]]
    out_shape = [jax.ShapeDtypeStruct((N, wd), BF16) for wd in IN_OUT_WIDTHS[:2]]
    for grp, (_, dil) in enumerate(DIL_GROUPS):
        for wd in IN_OUT_WIDTHS[2 + 2 * grp:4 + 2 * grp]:
            out_specs.append(pl.BlockSpec((1, dil, tm // dil, wd), lambda i: (i // tpb, 0, i % tpb, 0)))
            out_shape.append(jax.ShapeDtypeStruct((B, dil, seq // dil, wd), BF16))
    in_specs = [
        pl.BlockSpec((tm, D), lambda i: (i + tile0, 0)),
        pl.BlockSpec((1, D), lambda i: (0, 0)),
        pl.BlockSpec((1, 1, 3 * D), lambda i: (i // tpb, 0, 0)),
        pl.BlockSpec(w.shape, lambda i: (0, 0), pipeline_mode=pl.Buffered(1)),
    ]
    args = [x2, g.reshape(1, D), mod, w]
    if pending is not None:
        in_specs += [pl.BlockSpec((tm, PACK_W), lambda i: (i, 0)), pl.BlockSpec((tm, PACK_W), lambda i: (i, 0)),
                     pl.BlockSpec((1, 1, 3 * D), lambda i: (i // tpb, 0, 0))]
        args += list(pending)
        out_specs.insert(0, pl.BlockSpec((tm, D), lambda i: (i, 0)))
        out_shape.insert(0, jax.ShapeDtypeStruct((N, D), F32))
    return pl.pallas_call(
        functools.partial(_inproj_kernel, chunk=512, pending=pending is not None),
        grid=(N // tm,),
        in_specs=in_specs,
        out_specs=out_specs,
        out_shape=out_shape,
        scratch_shapes=[pltpu.VMEM((max(IN_OUT_WIDTHS[2:]) // LANES, tm, LANES), F32)],
        compiler_params=_cp(("parallel",), VMEM_LIMIT),
        name="in_projection",
    )(*args)


def _dilated_kernel(q_ref, kc_ref, kp_ref, vc_ref, vp_ref, o_ref, lse_ref):
    i = pl.program_id(1)
    T = Q_BLOCK
    key = lax.broadcasted_iota(jnp.int32, (T, T), 0)
    qry = lax.broadcasted_iota(jnp.int32, (T, T), 1)
    valid_c = key <= qry
    near = key >= qry
    seqs, run = q_ref.shape[0], q_ref.shape[1] // T
    heads = [slice(h * HEAD_DIM_A, (h + 1) * HEAD_DIM_A) for h in range(HEADS_PER_GROUP_A)]

    def transposed(v):
        return v.astype(F32).T.astype(BF16)

    vts = {(s, j): transposed(vc_ref[s, j * T:(j + 1) * T, :]) for s in range(seqs) for j in range(run)}
    vt_before = [transposed(vp_ref[s]) for s in range(seqs)]

    def blocks(s, j):
        rows = slice(j * T, (j + 1) * T)
        if j == 0:
            return rows, kc_ref[s, rows, :], vts[s, 0], kp_ref[s], vt_before[s], near & (i > 0)
        before = slice((j - 1) * T, j * T)
        return rows, kc_ref[s, rows, :], vts[s, j], kc_ref[s, before, :], vts[s, j - 1], near

    scores, probs = {}, {}
    for s in range(seqs):
        for j in range(run):
            rows, kc, _, kp, _, valid_p = blocks(s, j)
            q = q_ref[s, rows, :]
            for h, sl in enumerate(heads):
                qh = q[:, sl]
                scores[s, j, h] = (jnp.where(valid_c, _nt_dot(kc[:, sl], qh), NEG),
                                   jnp.where(valid_p, _nt_dot(kp[:, sl], qh), NEG))
    for chain, (sc, sp) in scores.items():
        m = jnp.maximum(jnp.max(sc, axis=0, keepdims=True), jnp.max(sp, axis=0, keepdims=True))
        pc = jnp.exp(sc - m)
        pp = jnp.exp(sp - m)
        den = jnp.sum(pc, axis=0, keepdims=True) + jnp.sum(pp, axis=0, keepdims=True)
        probs[chain] = (pc.astype(BF16), pp.astype(BF16), den, m + jnp.log(den))
    spread = LSE_LANES // len(heads)
    for s in range(seqs):
        for j in range(run):
            rows, _, vtc, _, vtp, _ = blocks(s, j)
            outs = []
            for h, sl in enumerate(heads):
                pc, pp, den, _ = probs[s, j, h]
                o = (jnp.dot(vtc[sl, :], pc, preferred_element_type=F32)
                     + jnp.dot(vtp[sl, :], pp, preferred_element_type=F32))
                outs.append(o / den)
            o_ref[s, rows, :] = jnp.concatenate(outs, axis=0).T.astype(o_ref.dtype)
            lse_t = jnp.concatenate(
                [jnp.broadcast_to(probs[s, j, h][3], (spread, T)) for h in range(len(heads))], axis=0)
            lse_ref[s, rows, :] = lse_t.T


DILATED_RUN = 8


LSE_LANES = 128


def dilated_attention(qk, v):
    batch, dilation, L, _ = qk.shape
    nb = L // Q_BLOCK
    run = min(DILATED_RUN, nb)
    seqs = DILATED_RUN // run
    qk_r = qk.reshape(batch * dilation, L, 2 * GROUP_W)
    v_r = v.reshape(batch * dilation, L, GROUP_W)
    before = lambda i: jnp.maximum(i * run - 1, 0)
    o, lse = pl.pallas_call(
        _dilated_kernel,
        grid=(batch * dilation // seqs, nb // run),
        in_specs=[
            pl.BlockSpec((seqs, run * Q_BLOCK, GROUP_W), lambda s, i: (s, i, 0)),
            pl.BlockSpec((seqs, run * Q_BLOCK, GROUP_W), lambda s, i: (s, i, 1)),
            pl.BlockSpec((seqs, Q_BLOCK, GROUP_W), lambda s, i: (s, before(i), 1)),
            pl.BlockSpec((seqs, run * Q_BLOCK, GROUP_W), lambda s, i: (s, i, 0)),
            pl.BlockSpec((seqs, Q_BLOCK, GROUP_W), lambda s, i: (s, before(i), 0)),
        ],
        out_specs=[
            pl.BlockSpec((seqs, run * Q_BLOCK, GROUP_W), lambda s, i: (s, i, 0)),
            pl.BlockSpec((seqs, run * Q_BLOCK, LSE_LANES), lambda s, i: (s, i, 0)),
        ],
        out_shape=[
            jax.ShapeDtypeStruct((batch * dilation, L, GROUP_W), BF16),
            jax.ShapeDtypeStruct((batch * dilation, L, LSE_LANES), F32),
        ],
        compiler_params=_cp(("parallel", "parallel")),
        name=f"dilated_attention_d{dilation}",
    )(qk_r, qk_r, qk_r, v_r, v_r)
    return o.reshape(batch, dilation, L, GROUP_W), lse.reshape(batch, dilation, L, LSE_LANES)


def _mla_prep_kernel(lat_ref, pos_ref, gq_ref, gkv_ref, wq_ref, wk_ref, wvt_ref, freq_ref, spread_ref, one_ref,
                     q_ref, k_ref, vt_ref):
    HP = N_HEADS_C * HEAD_PAD_C
    lat = lat_ref[...].astype(F32)
    cq = lat[:, :Q_LORA]
    ckr = lat[:, Q_LORA:]
    zq = (cq * lax.rsqrt(jnp.mean(cq * cq, axis=-1, keepdims=True) + EPS) * gq_ref[...]).astype(BF16)
    lane = lax.broadcasted_iota(jnp.int32, ckr.shape, 1)
    is_kv = lane < KV_LORA
    ms = jnp.sum(jnp.where(is_kv, ckr * ckr, 0.0), axis=-1, keepdims=True) * (1.0 / KV_LORA)
    zkv = (ckr * jnp.where(is_kv, lax.rsqrt(ms + EPS) * gkv_ref[...], 1.0)).astype(BF16)
    qq = jnp.dot(zq, wq_ref[...], preferred_element_type=F32)
    kk = jnp.dot(zkv, wk_ref[:, :HP], preferred_element_type=F32)
    kk_sw = jnp.dot(zkv[:, KV_LORA:], wk_ref[KV_LORA:, HP:], preferred_element_type=F32)
    ang_t = freq_ref[...] * pos_ref[0].astype(F32)

    def to_lanes(t):
        hi = t.astype(BF16)
        lo = (t - hi.astype(F32)).astype(BF16)
        tn_dot = lambda a: lax.dot_general(a, spread_ref[...], (((0,), (0,)), ((), ())), preferred_element_type=F32)
        return tn_dot(hi) + tn_dot(lo)

    cos = to_lanes(jnp.cos(ang_t)) + one_ref[...]
    sin = to_lanes(jnp.sin(ang_t))
    for h in range(N_HEADS_C):
        lo, hi = h * HEAD_PAD_C, (h + 1) * HEAD_PAD_C
        q_ref[:, lo:hi] = (qq[:, lo:hi] * cos + qq[:, HP + lo:HP + hi] * sin).astype(q_ref.dtype)
        k_ref[:, lo:hi] = (kk[:, lo:hi] * cos + kk_sw[:, lo:hi] * sin).astype(k_ref.dtype)
    vt_ref[0] = _nt_dot(wvt_ref[...], zkv).astype(vt_ref.dtype)


def _mla_weights(cq_g, ckv_g, w_uq, w_ukv):
    H, HPAD, half = N_HEADS_C, HEAD_PAD_C, QK_ROPE // 2
    scale = (QK_NOPE + QK_ROPE) ** -0.5 * math.log2(math.e)
    wq = w_uq.reshape(Q_LORA, H, QK_NOPE + QK_ROPE) * scale
    q_lin = jnp.pad(wq, ((0, 0), (0, 0), (0, HPAD - QK_NOPE - QK_ROPE)))
    r1, r2 = wq[..., QK_NOPE:QK_NOPE + half], wq[..., QK_NOPE + half:]
    q_sw = jnp.concatenate([jnp.zeros((Q_LORA, H, QK_NOPE), F32), -r2, r1,
                            jnp.zeros((Q_LORA, H, HPAD - QK_NOPE - QK_ROPE), F32)], axis=-1)
    wq_big = jnp.concatenate([q_lin.reshape(Q_LORA, H * HPAD), q_sw.reshape(Q_LORA, H * HPAD)], axis=1)

    rows = LAT_W - Q_LORA
    wkv = w_ukv.reshape(KV_LORA, H, QK_NOPE + V_DIM)
    eye = jnp.eye(QK_ROPE, dtype=F32)
    k_lin = jnp.zeros((rows, H, HPAD), F32)
    k_lin = k_lin.at[:KV_LORA, :, :QK_NOPE].set(wkv[..., :QK_NOPE])
    k_lin = k_lin.at[KV_LORA:KV_LORA + QK_ROPE, :, QK_NOPE:QK_NOPE + QK_ROPE].set(
        jnp.broadcast_to(eye[:, None, :], (QK_ROPE, H, QK_ROPE)))
    swap = jnp.zeros((QK_ROPE, QK_ROPE), F32).at[half:, :half].set(-jnp.eye(half)).at[:half, half:].set(jnp.eye(half))
    k_sw = jnp.zeros((rows, H, HPAD), F32)
    k_sw = k_sw.at[KV_LORA:KV_LORA + QK_ROPE, :, QK_NOPE:QK_NOPE + QK_ROPE].set(
        jnp.broadcast_to(swap[:, None, :], (QK_ROPE, H, QK_ROPE)))
    v_w = jnp.zeros((rows, H, V_DIM), F32).at[:KV_LORA].set(wkv[..., QK_NOPE:])
    wk_big = jnp.concatenate([k_lin.reshape(rows, H * HPAD), k_sw.reshape(rows, H * HPAD)], axis=1)
    wv_t = v_w.reshape(rows, H * V_DIM).T

    gkv = jnp.concatenate([ckv_g, jnp.ones((rows - KV_LORA,), F32)]).reshape(1, rows)
    return cq_g.reshape(1, Q_LORA), gkv, wq_big.astype(BF16), wk_big.astype(BF16), wv_t.astype(BF16)


def _rope_tables():
    half = QK_ROPE // 2
    freqs = (ROPE_THETA ** (-jnp.arange(0, QK_ROPE, 2, dtype=F32) / QK_ROPE)).reshape(half, 1)
    lane = jnp.arange(HEAD_PAD_C)[None, :]
    j = jnp.arange(half)[:, None]
    spread = (lane == QK_NOPE + j) | (lane == QK_NOPE + half + j)
    off_rope = ~jnp.any(spread, axis=0, keepdims=True)
    return freqs, spread.astype(BF16), off_rope.astype(F32)


def mla_prep(lat, positions, gq, gkv, wq_big, wk_big, wv_t, batch, seq):
    N = lat.shape[0]
    HP = N_HEADS_C * HEAD_PAD_C
    tm = 512
    tpb = seq // tm
    freqs, spread, off_rope = _rope_tables()
    pos_rows = positions.reshape(N // tm, 1, tm)
    const = lambda shape: pl.BlockSpec(shape, lambda i: (0, 0))
    return pl.pallas_call(
        _mla_prep_kernel,
        grid=(N // tm,),
        in_specs=[
            pl.BlockSpec((tm, LAT_W), lambda i: (i, 0)),
            pl.BlockSpec((1, 1, tm), lambda i: (i, 0, 0)),
            const(gq.shape), const(gkv.shape), const(wq_big.shape), const(wk_big.shape), const(wv_t.shape),
            const(freqs.shape), const(spread.shape), const(off_rope.shape),
        ],
        out_specs=[
            pl.BlockSpec((tm, HP), lambda i: (i, 0)),
            pl.BlockSpec((tm, HP), lambda i: (i, 0)),
            pl.BlockSpec((1, DC, tm), lambda i: (i // tpb, 0, i % tpb)),
        ],
        out_shape=[
            jax.ShapeDtypeStruct((N, HP), BF16),
            jax.ShapeDtypeStruct((N, HP), BF16),
            jax.ShapeDtypeStruct((batch, DC, seq), BF16),
        ],
        compiler_params=_cp(("parallel",), VMEM_LIMIT),
        name="mla_prep",
    )(lat, pos_rows, gq, gkv, wq_big, wk_big, wv_t, freqs, spread, off_rope)


HEADS_PER_STEP_C = 8
FLASH_Q_CHUNK = 256


def _mla_flash_kernel(qi_ref, ki_ref, q_ref, k_ref, vt_ref, o_ref, m_sc, l_sc, acc_sc):
    t = pl.program_id(2)
    qi, ki = qi_ref[t], ki_ref[t]

    @pl.when(ki == 0)
    def _():
        m_sc[...] = jnp.full(m_sc.shape, NEG, F32)
        l_sc[...] = jnp.zeros(l_sc.shape, F32)
        acc_sc[...] = jnp.zeros(acc_sc.shape, F32)

    def step(masked):
        T = q_ref.shape[1]
        if masked:
            key = lax.broadcasted_iota(jnp.int32, (T, T), 0)
            qry = lax.broadcasted_iota(jnp.int32, (T, T), 1)
            keep = key <= qry
        chains = [(h, c) for h in range(HEADS_PER_STEP_C) for c in range(T // FLASH_Q_CHUNK)]
        scores, probs, alphas = {}, {}, {}

        def keys_for(c):
            return (c + 1) * FLASH_Q_CHUNK if masked else T

        def qk(h, c):
            qs = slice(c * FLASH_Q_CHUNK, (c + 1) * FLASH_Q_CHUNK)
            q = q_ref[0, qs, h * HEAD_PAD_C:(h + 1) * HEAD_PAD_C]
            k = k_ref[0, :keys_for(c), h * HEAD_PAD_C:(h + 1) * HEAD_PAD_C]
            st = _nt_dot(k, q)
            scores[h, c] = jnp.where(keep[:keys_for(c), qs], st, NEG) if masked else st

        def softmax(h, c):
            qs = slice(c * FLASH_Q_CHUNK, (c + 1) * FLASH_Q_CHUNK)
            st = scores.pop((h, c))
            m_prev = m_sc[h, :, qs]
            m_new = jnp.maximum(m_prev, jnp.max(st, axis=0, keepdims=True))
            alpha = jnp.exp2(m_prev - m_new)
            p = jnp.exp2(st - m_new)
            l_sc[h, :, qs] = alpha * l_sc[h, :, qs] + jnp.sum(p, axis=0, keepdims=True)
            m_sc[h, :, qs] = m_new
            probs[h, c], alphas[h, c] = p.astype(BF16), alpha

        def pv(h, c):
            qs = slice(c * FLASH_Q_CHUNK, (c + 1) * FLASH_Q_CHUNK)
            vt = vt_ref[0, h * V_DIM:(h + 1) * V_DIM, :keys_for(c)]
            acc_sc[h, :, qs] = alphas.pop((h, c)) * acc_sc[h, :, qs] + jnp.dot(
                vt, probs.pop((h, c)), preferred_element_type=F32)

        for phase in (qk, softmax, pv):
            for ch in chains:
                phase(*ch)

    @pl.when(ki < qi)
    def _():
        step(False)

    @pl.when(ki == qi)
    def _():
        step(True)
        ot = jnp.concatenate([acc_sc[h] / l_sc[h] for h in range(HEADS_PER_STEP_C)], axis=0)
        o_ref[0] = ot.T.astype(o_ref.dtype)


def mla_attention(q_all, k_all, vt_all, batch, seq):
    T = 512
    nq = seq // T
    pairs = [(a, b) for a in range(nq) for b in range(a + 1)]
    qi_tab = jnp.asarray([p[0] for p in pairs], jnp.int32)
    ki_tab = jnp.asarray([p[1] for p in pairs], jnp.int32)
    hp = N_HEADS_C // HEADS_PER_STEP_C
    qw = HEADS_PER_STEP_C * HEAD_PAD_C
    vw = HEADS_PER_STEP_C * V_DIM
    q3 = q_all.reshape(batch, seq, -1)
    k3 = k_all.reshape(batch, seq, -1)
    grid_spec = pltpu.PrefetchScalarGridSpec(
        num_scalar_prefetch=2,
        grid=(batch, hp, len(pairs)),
        in_specs=[
            pl.BlockSpec((1, T, qw), lambda b, h, t, qi, ki: (b, qi[t], h)),
            pl.BlockSpec((1, T, qw), lambda b, h, t, qi, ki: (b, ki[t], h)),
            pl.BlockSpec((1, vw, T), lambda b, h, t, qi, ki: (b, h, ki[t])),
        ],
        out_specs=pl.BlockSpec((1, T, vw), lambda b, h, t, qi, ki: (b, qi[t], h)),
        scratch_shapes=[
            pltpu.VMEM((HEADS_PER_STEP_C, 1, T), F32),
            pltpu.VMEM((HEADS_PER_STEP_C, 1, T), F32),
            pltpu.VMEM((HEADS_PER_STEP_C, V_DIM, T), F32),
        ],
    )
    o = pl.pallas_call(
        _mla_flash_kernel,
        grid_spec=grid_spec,
        out_shape=jax.ShapeDtypeStruct((batch, seq, DC), BF16),
        compiler_params=_cp(("parallel", "parallel", "arbitrary")),
        name="mla_attention",
    )(qi_tab, ki_tab, q3, k3, vt_all)
    return o.reshape(batch * seq, DC)


def _mixout_kernel(x_ref, gates_ref, ub_ref, ubh_ref, o1_ref, o2_ref, o3_ref, l1_ref, l2_ref, l3_ref, yc_ref,
                   mod1_ref, mod2_ref, g2_ref, poolw_ref, pscale_ref, woa_ref, wob_ref, woc_ref, wout_ref,
                   rwt_ref, sw1_ref, sw3_ref, sw2_ref, spread_ref,
                   xmid_ref, h2a_ref, h2b_ref, logit_ref, *scratch, tiles_per_batch):
    D = x_ref.shape[1]
    tm = x_ref.shape[0]
    tile = pl.program_id(0) % tiles_per_batch
    o_scrs, l_scrs = scratch[:3], scratch[3:]

    def token_order(ref, scr):
        dil, rows, width = ref.shape[1:]
        if dil == 1:
            return ref[0, 0].astype(F32)
        for r in range(dil):
            v = ref[0, r].astype(F32)
            for c in range(width // LANES):
                scr[c, pl.ds(r, rows, stride=dil), :] = v[:, c * LANES:(c + 1) * LANES]
        return jnp.concatenate([scr[c] for c in range(width // LANES)], axis=1)

    outs = [token_order(r, s) for r, s in zip((o1_ref, o2_ref, o3_ref), o_scrs)]
    l1, l2, l3 = [token_order(r, s) for r, s in zip((l1_ref, l2_ref, l3_ref), l_scrs)]
    mx = jnp.maximum(jnp.maximum(l1, l2), l3)
    es = [jnp.exp(l1 - mx), jnp.exp(l2 - mx), jnp.exp(l3 - mx)]
    inv = 1.0 / (es[0] + es[1] + es[2])
    ya = jnp.zeros((tm, GROUP_W), F32)
    for e, o in zip(es, outs):
        w = e * inv
        w_hi = w.astype(BF16)
        w_lo = (w - w_hi.astype(F32)).astype(BF16)
        w_wide = (jnp.dot(w_hi, spread_ref[...], preferred_element_type=F32)
                  + jnp.dot(w_lo, spread_ref[...], preferred_element_type=F32))
        ya = ya + w_wide * o
    a_out = jnp.dot(ya.astype(BF16), woa_ref[...], preferred_element_type=F32)

    u = ub_ref[...].astype(F32)
    halo = jnp.where(tile > 0, ubh_ref[...].astype(F32), 0.0)
    ext = jnp.concatenate([halo, u], axis=0)
    t_seq = tile * tm + lax.broadcasted_iota(jnp.int32, (tm, 1), 0)
    pooled = []
    for gi, w in enumerate(POOL_WINDOWS):
        sl = slice(gi * POOL_GROUP_DIM, (gi + 1) * POOL_GROUP_DIM)
        acc = ext[:, sl]
        k = 1
        while k < w:
            acc = acc + pltpu.roll(acc, k, axis=0)
            k *= 2
        cnt = jnp.minimum(t_seq + 1, w).astype(F32)
        pg = acc[POOL_HALO:] / cnt - u[:, sl]
        pooled.append(jnp.dot(pg.astype(BF16), poolw_ref[gi], preferred_element_type=F32))
    yb = jnp.concatenate(pooled, axis=1) * pscale_ref[...]
    b_out = jnp.dot(yb.astype(BF16), wob_ref[...], preferred_element_type=F32)
    c_out = jnp.dot(yc_ref[...], woc_ref[...], preferred_element_type=F32)

    g = gates_ref[...].astype(F32)
    mix = (jax.nn.sigmoid(g[:, :D]) * a_out + jax.nn.sigmoid(g[:, D:2 * D]) * b_out
           + jax.nn.sigmoid(g[:, 2 * D:]) * c_out)
    tok = jnp.dot(mix.astype(BF16), wout_ref[...], preferred_element_type=F32)
    xn = x_ref[...] + mod1_ref[0][:, 2 * D:] * tok

    mod2 = mod2_ref[0]
    y = xn * lax.rsqrt(jnp.mean(xn * xn, axis=-1, keepdims=True) + EPS) * g2_ref[...]
    h2 = y * (1.0 + mod2[:, D:2 * D]) + mod2[:, :D]
    h2b = h2.astype(BF16)
    h2a_ref[...], h2b_ref[...] = _pack_row_halves(h2b)
    logit_ref[...] = _nt_dot(rwt_ref[...], h2b)
    hid = _silu(jnp.dot(h2b, sw1_ref[...], preferred_element_type=F32)) * jnp.dot(
        h2b, sw3_ref[...], preferred_element_type=F32)
    shared = jnp.dot(hid.astype(BF16), sw2_ref[...], preferred_element_type=F32)
    xmid_ref[...] = xn + mod2[:, 2 * D:] * shared


def mix_out(x2, gu, dil, yc, mod1, mod2, g2, pool_w, pool_scale, w_oa, w_ob, w_oc, w_out, rwt, sw1, sw3, sw2, seq,
            row0=0):
    D = x2.shape[1]
    N = gu.shape[0]
    tm = 512
    tpb = seq // tm
    tile0 = row0 // tm
    (o1, l1), (o2, l2), (o3, l3) = dil
    row = lambda w, c=0: pl.BlockSpec((tm, w), lambda i: (i, c))
    by_residue = lambda a: pl.BlockSpec(
        (1, a.shape[1], tm // a.shape[1], a.shape[3]), lambda i: (i // tpb, 0, i % tpb, 0))
    heads = HEADS_PER_GROUP_A
    spread = (jnp.arange(LSE_LANES)[:, None] == (jnp.arange(GROUP_W)[None, :] // HEAD_DIM_A) * (LSE_LANES // heads)
              ).astype(BF16)
    const2 = lambda a: pl.BlockSpec(a.shape, lambda i: (0,) * a.ndim, pipeline_mode=pl.Buffered(1))
    modspec = pl.BlockSpec((1, 1, 3 * D), lambda i: (i // tpb, 0, 0))
    ub_col = 3 * D // DB
    halo_spec = pl.BlockSpec(
        (POOL_HALO, DB), lambda i: (jnp.maximum(i * (tm // POOL_HALO) - 1, 0), ub_col))
    weights = [g2.reshape(1, D), pool_w, pool_scale.reshape(1, DB), w_oa, w_ob, w_oc, w_out, rwt, sw1, sw3, sw2,
               spread]
    return pl.pallas_call(
        functools.partial(_mixout_kernel, tiles_per_batch=tpb),
        grid=(N // tm,),
        in_specs=[
            pl.BlockSpec((tm, D), lambda i: (i + tile0, 0)), row(3 * D), row(DB, ub_col), halo_spec,
            by_residue(o1), by_residue(o2), by_residue(o3), by_residue(l1), by_residue(l2), by_residue(l3), row(DC),
            modspec, modspec,
        ] + [const2(a) for a in weights],
        scratch_shapes=[pltpu.VMEM((GROUP_W // LANES, tm, LANES), F32)] * 3
        + [pltpu.VMEM((LSE_LANES // LANES, tm, LANES), F32)] * 3,
        out_specs=[row(D), row(PACK_W), row(PACK_W), pl.BlockSpec((N_EXPERTS, tm), lambda i: (0, i))],
        out_shape=[
            jax.ShapeDtypeStruct((N, D), F32),
            jax.ShapeDtypeStruct((N, PACK_W), jnp.int32),
            jax.ShapeDtypeStruct((N, PACK_W), jnp.int32),
            jax.ShapeDtypeStruct((N_EXPERTS, N), F32),
        ],
        compiler_params=_cp(("parallel",), VMEM_LIMIT),
        name="mix_out",
    )(x2, gu, gu, gu, o1, o2, o3, l1, l2, l3, yc, mod1, mod2, *weights)


def _pick_rows(table, picks):
    G, GS = N_GROUPS, GROUP_SIZE
    eio = lax.broadcasted_iota(jnp.int32, (GS, table.shape[1]), 0)
    rows = []
    for k in range(TOP_K):
        idx = picks[k:k + 1]
        parts = [jnp.where(eio + g * GS == idx, table[g * GS:(g + 1) * GS], 0.0) for g in range(G)]
        rows.append(jnp.sum(functools.reduce(jnp.add, parts), axis=0, keepdims=True))
    return jnp.concatenate(rows, axis=0)


def _route_choose(lg_ref, bias_ref):
    G, GS = N_GROUPS, GROUP_SIZE
    scores = jax.nn.sigmoid(lg_ref[...])
    sel = scores + bias_ref[...]
    tn = sel.shape[1]
    eio = lax.broadcasted_iota(jnp.int32, (GS, tn), 0)
    ninf = -jnp.inf

    gs = []
    for g in range(G):
        v = sel[g * GS:(g + 1) * GS]
        m1 = jnp.max(v, axis=0, keepdims=True)
        i1 = jnp.min(jnp.where(v == m1, eio, GS), axis=0, keepdims=True)
        m2 = jnp.max(jnp.where(eio == i1, ninf, v), axis=0, keepdims=True)
        gs.append(m1 + m2)
    gsm = jnp.concatenate(gs, axis=0)
    gio = lax.broadcasted_iota(jnp.int32, (G, tn), 0)
    rank = jnp.zeros((G, tn), jnp.int32)
    for g2 in range(G):
        beats = (gs[g2] > gsm) | ((gs[g2] == gsm) & (g2 < gio))
        rank = rank + beats.astype(jnp.int32)
    gsel = rank < TOPK_GROUPS

    vs = [jnp.where(gsel[g:g + 1], sel[g * GS:(g + 1) * GS], NEG) for g in range(G)]
    eid = [eio + g * GS for g in range(G)]
    chosen = [jnp.zeros((GS, tn), jnp.bool_) for _ in range(G)]
    picks = []
    for _ in range(TOP_K):
        m = jnp.max(functools.reduce(jnp.maximum, vs), axis=0, keepdims=True)
        idx = jnp.min(functools.reduce(jnp.minimum, [jnp.where(v == m, e, N_EXPERTS) for v, e in zip(vs, eid)]),
                      axis=0, keepdims=True)
        picks.append(idx)
        for g in range(G):
            hit = eid[g] == idx
            chosen[g] = chosen[g] | hit
            vs[g] = jnp.where(hit, ninf, vs[g])
    mask = jnp.concatenate(chosen, axis=0).astype(F32)
    return scores, jnp.concatenate(picks, axis=0), mask


def _route_kernel(lg_ref, bias_ref, tri_ref, dest_ref, w_ref, cnt_ref, run_sc, start_sc, mask_sc, picks_sc,
                  *, slot_block):
    phase = pl.program_id(0)
    step = pl.program_id(1)
    tn = lg_ref.shape[1]
    cols = pl.ds(pl.multiple_of(step * tn, tn), tn)

    @pl.when(phase == 0)
    def _():
        @pl.when(step == 0)
        def _():
            run_sc[...] = jnp.zeros(run_sc.shape, F32)

        scores, picks, mask = _route_choose(lg_ref, bias_ref)
        wk = _pick_rows(scores, picks)
        w_ref[0] = wk / jnp.sum(wk, axis=0, keepdims=True) * ROUTED_SCALE
        dest_ref[0] = jnp.zeros(dest_ref.shape[1:], dest_ref.dtype)
        mask_sc[:, cols] = mask.astype(BF16)
        picks_sc[:, cols] = picks
        run_sc[...] = run_sc[...] + jnp.sum(mask, axis=1, keepdims=True)

    @pl.when(phase == 1)
    def _():
        @pl.when(step == 0)
        def _():
            counts = run_sc[...].astype(jnp.int32)
            cnt_ref[...] = jnp.broadcast_to(counts, cnt_ref.shape)
            shift = slot_block.bit_length() - 1
            padded = lax.shift_left(lax.shift_right_logical(counts + (slot_block - 1), shift), shift).astype(F32)
            r = lax.broadcasted_iota(jnp.int32, (N_EXPERTS, N_EXPERTS), 0)
            c = lax.broadcasted_iota(jnp.int32, (N_EXPERTS, N_EXPERTS), 1)
            as_row = jnp.sum(jnp.where(r == c, padded, 0.0), axis=0, keepdims=True)
            start_sc[...] = jnp.sum(jnp.where(c < r, as_row, 0.0), axis=1, keepdims=True)
            run_sc[...] = jnp.zeros(run_sc.shape, F32)

        mask_b = mask_sc[:, cols]
        mask = mask_b.astype(F32)
        before = jnp.dot(mask_b, tri_ref[...], preferred_element_type=F32) - mask
        slot = start_sc[...] + run_sc[...] + before
        dest_ref[0] = _pick_rows(slot, picks_sc[:, cols]).astype(jnp.int32)
        w_ref[0] = jnp.zeros(w_ref.shape[1:], w_ref.dtype)
        run_sc[...] = run_sc[...] + jnp.sum(mask, axis=1, keepdims=True)


SLOT_BLOCK = 512


def route(logits_t, bias):
    E, N = logits_t.shape
    tn = 1024
    tri = (jnp.arange(tn)[:, None] <= jnp.arange(tn)[None, :]).astype(BF16)
    plane = lambda: pl.BlockSpec((1, TOP_K, tn), lambda p, i: (p, 0, i))
    dest, w, cnt = pl.pallas_call(
        functools.partial(_route_kernel, slot_block=SLOT_BLOCK),
        grid=(2, N // tn),
        in_specs=[
            pl.BlockSpec((E, tn), lambda p, i: (0, i * (1 - p))),
            pl.BlockSpec((E, 1), lambda p, i: (0, 0)),
            pl.BlockSpec((tn, tn), lambda p, i: (0, 0)),
        ],
        out_specs=[plane(), plane(), pl.BlockSpec((E, 128), lambda p, i: (0, 0))],
        out_shape=[
            jax.ShapeDtypeStruct((2, TOP_K, N), jnp.int32),
            jax.ShapeDtypeStruct((2, TOP_K, N), F32),
            jax.ShapeDtypeStruct((E, 128), jnp.int32),
        ],
        scratch_shapes=[pltpu.VMEM((E, 1), F32), pltpu.VMEM((E, 1), F32),
                        pltpu.VMEM((E, N), BF16), pltpu.VMEM((TOP_K, N), jnp.int32)],
        compiler_params=_cp(("arbitrary", "arbitrary")),
        name="route",
    )(logits_t, bias.reshape(E, 1), tri)
    return dest[1], w[0], cnt[:, 0]


def block_tables(counts, n_tokens):
    E = counts.shape[0]
    blk = SLOT_BLOCK
    nblk = (n_tokens * TOP_K + E * blk) // blk
    per_expert = (counts + blk - 1) // blk
    bend = jnp.cumsum(per_expert)
    bstart = bend - per_expert
    b = jnp.arange(nblk, dtype=jnp.int32)[:, None]
    owns = (bstart[None, :] <= b) & (b < bend[None, :])
    blk_e = jnp.minimum(jnp.sum(bend[None, :] <= b, axis=1), E - 1).astype(jnp.int32)
    rows_left = counts[None, :] - (b - bstart[None, :]) * blk
    nvalid = jnp.sum(jnp.where(owns, jnp.clip(rows_left, 0, blk), 0), axis=1)
    first = jnp.concatenate([jnp.ones((1,), jnp.bool_), blk_e[1:] != blk_e[:-1]])
    run_parity = ((jnp.cumsum(first.astype(jnp.int32)) - 1) % 2).astype(jnp.int32)
    later = blk_e[None, :] > blk_e[:, None]
    next_e = jnp.min(jnp.where(later, blk_e[None, :], E), axis=1).astype(jnp.int32)
    return blk_e, nvalid.astype(jnp.int32), run_parity, next_e


def _sc_mesh():
    return plsc.VectorSubcoreMesh(core_axis_name="c", subcore_axis_name="s")


SC_WINDOW = 128


def sc_scatter_rows(x, dest, n_slots):
    N, W = x.shape
    K = dest.shape[0]

    @functools.partial(pl.kernel, out_type=jax.ShapeDtypeStruct((n_slots, W), x.dtype), mesh=_sc_mesh(),
                       scratch_types=[])
    def scatter(x_hbm, i_hbm, o_hbm):
        def body(x_vmem, i_vmem):
            for k in range(K):
                pltpu.sync_copy(x_vmem, o_hbm.at[i_vmem.at[k]])

        pltpu.emit_pipeline(
            body,
            grid=(N // SC_WINDOW,),
            in_specs=[pl.BlockSpec((SC_WINDOW, W), lambda i: (i, 0)),
                      pl.BlockSpec((K, SC_WINDOW), lambda i: (0, i))],
            out_specs=[],
            core_axis_name=("c", "s"),
            dimension_semantics=(pltpu.PARALLEL,),
        )(x_hbm, i_hbm)

    return scatter(x, dest)


SC_LANES = 16
SC_GATHER_TOKENS = 8


def sc_weighted_gather(y, dest, wts):
    W = y.shape[1]
    K, N = dest.shape
    G, L = SC_GATHER_TOKENS, SC_LANES
    batches = SC_WINDOW // G

    @functools.partial(
        pl.kernel, out_type=jax.ShapeDtypeStruct((N, W), y.dtype), mesh=_sc_mesh(),
        scratch_types=[pltpu.VMEM((2, K, G, W), y.dtype), pltpu.SemaphoreType.DMA((2,))],
        compiler_params=pltpu.CompilerParams(needs_layout_passes=False))
    def gather(y_hbm, i_hbm, w_hbm, o_hbm, rows2, sems):
        def body(i_vmem, w_vmem, o_vmem):
            def fetch(batch, slot):
                return [pltpu.make_async_copy(y_hbm.at[i_vmem.at[k, pl.ds(batch * G, G)]], rows2.at[slot, k],
                                              sems.at[slot]) for k in range(K)]

            for c in fetch(0, 0):
                c.start()

            @pl.loop(0, batches)
            def _(batch):
                slot = batch % 2

                @pl.when(batch + 1 < batches)
                def _():
                    for c in fetch(batch + 1, 1 - slot):
                        c.start()

                for c in fetch(batch, slot):
                    c.wait()
                rows = rows2.at[slot]

                @pl.loop(0, G)
                def _(t):
                    tok = jnp.full((L,), batch * G + t, jnp.int32)
                    wk = [plsc.load_gather(w_vmem, [jnp.full((L,), k, jnp.int32), tok]) for k in range(K)]

                    @plsc.parallel_loop(0, W // L, unroll=4)
                    def _(j):
                        lo = jnp.zeros((L,), F32)
                        hi = jnp.zeros((L,), F32)
                        for k in range(K):
                            pair = plsc.bitcast(rows[k, t, pl.ds(j * L, L)], BF16)
                            a, b = plsc.unpack(pair, format=plsc.PackFormat.INTERLEAVED)
                            lo = lo + wk[k] * a
                            hi = hi + wk[k] * b
                        o_vmem[batch * G + t, pl.ds(j * L, L)] = plsc.bitcast(
                            plsc.pack(lo, hi, format=plsc.PackFormat.INTERLEAVED), y.dtype)

        pltpu.emit_pipeline(
            body,
            grid=(N // SC_WINDOW,),
            in_specs=[pl.BlockSpec((K, SC_WINDOW), lambda i: (0, i)),
                      pl.BlockSpec((K, SC_WINDOW), lambda i: (0, i))],
            out_specs=[pl.BlockSpec((SC_WINDOW, W), lambda i: (i, 0))],
            core_axis_name=("c", "s"),
            dimension_semantics=(pltpu.PARALLEL,),
        )(i_hbm, w_hbm, o_hbm)

    return gather(y, dest, wts)


EXPERT_INPUT_SLOTS = 3


def _expert_kernel(blk_e_ref, nvalid_ref, parity_ref, next_e_ref, xa_hbm, xb_hbm, w1_hbm, w3_hbm, w2_hbm,
                   ya_ref, yb_ref, w1_sc, w3_sc, w2_sc, xa_buf, xb_buf, sems, w1_st, w3_st, w2_st, wsems,
                   *, layer):
    b = pl.program_id(0)
    nb = pl.num_programs(0)
    nv = nvalid_ref[b]
    prev_e = blk_e_ref[jnp.maximum(b - 1, 0)]
    blk = xa_buf.shape[1]
    ring = EXPERT_INPUT_SLOTS
    n_experts = w1_hbm.shape[1]

    def fetch_weights(e, par):
        return (pltpu.make_async_copy(w1_hbm.at[layer, e], w1_st.at[par], wsems.at[par, 0]),
                pltpu.make_async_copy(w3_hbm.at[layer, e], w3_st.at[par], wsems.at[par, 1]),
                pltpu.make_async_copy(w2_hbm.at[layer, e], w2_st.at[par], wsems.at[par, 2]))

    def fetch(block, slot):
        rows = pl.ds(pl.multiple_of(block * blk, blk), blk)
        return (pltpu.make_async_copy(xa_hbm.at[rows], xa_buf.at[slot], sems.at[slot, 0]),
                pltpu.make_async_copy(xb_hbm.at[rows], xb_buf.at[slot], sems.at[slot, 1]))

    @pl.when(b == 0)
    def _():
        for i in range(ring - 1):
            for c in fetch(i, i):
                c.start()

    ahead = b + (ring - 1)

    @pl.when(ahead < nb)
    def _():
        for c in fetch(ahead, ahead % ring):
            c.start()

    slot = b % ring
    for c in fetch(b, slot):
        c.wait()

    @pl.when(b == 0)
    def _():
        for c in fetch_weights(blk_e_ref[0], 0):
            c.start()

    @pl.when((b == 0) | (blk_e_ref[b] != prev_e))
    def _():
        par = parity_ref[b]
        nxt = next_e_ref[b]

        @pl.when(nxt < n_experts)
        def _():
            for c in fetch_weights(nxt, 1 - par):
                c.start()

        for c in fetch_weights(blk_e_ref[b], par):
            c.wait()
        w1_sc[...] = w1_st[par].astype(BF16)
        w3_sc[...] = w3_st[par].astype(BF16)
        w2_sc[...] = w2_st[par].astype(BF16)

    @pl.when(nv > 0)
    def _():
        x = _unpack_row_halves(xa_buf[slot], xb_buf[slot])
        rows = lax.broadcasted_iota(jnp.int32, x.shape, 0)
        x = jnp.where(rows < nv, x, 0.0).astype(BF16)
        hid = _silu(jnp.dot(x, w1_sc[...], preferred_element_type=F32)) * jnp.dot(
            x, w3_sc[...], preferred_element_type=F32)
        y = jnp.dot(hid.astype(BF16), w2_sc[...], preferred_element_type=F32)
        ya_ref[...], yb_ref[...] = _pack_row_halves(y)

    @pl.when(nv == 0)
    def _():
        ya_ref[...] = jnp.zeros(ya_ref.shape, ya_ref.dtype)
        yb_ref[...] = jnp.zeros(yb_ref.shape, yb_ref.dtype)


def routed_experts(xa, xb, tables, w1, w3, w2, layer):
    P = xa.shape[0]
    blk = SLOT_BLOCK
    _, E, D, FF = w1.shape
    slots = lambda: pl.BlockSpec((blk, PACK_W), lambda b, *_: (b, 0))
    grid_spec = pltpu.PrefetchScalarGridSpec(
        num_scalar_prefetch=len(tables),
        grid=(P // blk,),
        in_specs=[pl.BlockSpec(memory_space=pl.ANY)] * 5,
        out_specs=[slots(), slots()],
        scratch_shapes=[
            pltpu.VMEM((D, FF), BF16), pltpu.VMEM((D, FF), BF16), pltpu.VMEM((FF, D), BF16),
            pltpu.VMEM((EXPERT_INPUT_SLOTS, blk, PACK_W), jnp.int32),
            pltpu.VMEM((EXPERT_INPUT_SLOTS, blk, PACK_W), jnp.int32),
            pltpu.SemaphoreType.DMA((EXPERT_INPUT_SLOTS, 2)),
            pltpu.VMEM((2, D, FF), F32), pltpu.VMEM((2, D, FF), F32), pltpu.VMEM((2, FF, D), F32),
            pltpu.SemaphoreType.DMA((2, 3)),
        ],
    )
    return pl.pallas_call(
        functools.partial(_expert_kernel, layer=layer),
        grid_spec=grid_spec,
        out_shape=[jax.ShapeDtypeStruct((P, PACK_W), jnp.int32)] * 2,
        compiler_params=_cp(("arbitrary",), VMEM_LIMIT),
        name="routed_experts",
    )(*tables, xa, xb, w1, w3, w2)


def _combine_kernel(xmid_ref, ra_ref, rb_ref, mod2_ref, fg_ref, *rest):
    out_ref = rest[-1]
    D = xmid_ref.shape[1]
    x = xmid_ref[...] + mod2_ref[0][:, 2 * D:] * _unpack_row_halves(ra_ref[...], rb_ref[...])
    out_ref[...] = x * lax.rsqrt(jnp.mean(x * x, axis=-1, keepdims=True) + EPS) * fg_ref[...]


def combine(xmid, ra, rb, mod2, final_g, seq, out_rows=None, row0=0, out_buf=None):
    N, D = xmid.shape
    tm = 512
    tpb = seq // tm
    tile0 = row0 // tm
    in_specs = [
        pl.BlockSpec((tm, D), lambda i: (i, 0)),
        pl.BlockSpec((tm, PACK_W), lambda i: (i, 0)),
        pl.BlockSpec((tm, PACK_W), lambda i: (i, 0)),
        pl.BlockSpec((1, 1, 3 * D), lambda i: (i // tpb, 0, 0)),
        pl.BlockSpec((1, D), lambda i: (0, 0)),
    ]
    args = [xmid, ra, rb, mod2, final_g.reshape(1, D)]
    aliases = {}
    if out_buf is not None:
        in_specs.append(pl.BlockSpec(memory_space=pl.ANY))
        args.append(out_buf)
        aliases = {len(args) - 1: 0}
    return pl.pallas_call(
        _combine_kernel,
        grid=(N // tm,),
        in_specs=in_specs,
        out_specs=pl.BlockSpec((tm, D), lambda i: (i + tile0, 0)),
        out_shape=jax.ShapeDtypeStruct((out_rows or N, D), F32),
        input_output_aliases=aliases,
        compiler_params=_cp(("parallel",), VMEM_LIMIT),
        name="combine",
    )(*args)


TOKEN_STREAMS = 2


def _permute_w_in(w):
    ub = w[:, 3 * DA:3 * DA + DB]
    lat_lo = 3 * DA + DB
    lat_hi = lat_lo + Q_LORA + KV_LORA + QK_ROPE
    lat, gates = w[:, lat_lo:lat_hi], w[:, lat_hi:]
    pad = jnp.zeros((w.shape[0], LAT_W - (lat_hi - lat_lo)), w.dtype)
    parts = [gates, ub, lat, pad]
    for g in range(len(DIL_GROUPS)):
        sl = slice(g * GROUP_W, (g + 1) * GROUP_W)
        parts += [w[:, :DA][:, sl] * (HEAD_DIM_A ** -0.5), w[:, DA:2 * DA][:, sl], w[:, 2 * DA:3 * DA][:, sl]]
    return jnp.concatenate(parts, axis=1).astype(BF16)


def kernel(x, c, positions, ada_mix_w, ada_mix_b, norm_mix_g, w_in, pool_w, pool_scale, cq_norm_g, ckv_norm_g, w_uq, w_ukv, w_oa, w_ob, w_oc, w_out, ada_ffn_w, ada_ffn_b, norm_ffn_g, router_w, router_bias, exp_w1, exp_w3, exp_w2, sh_w1, sh_w3, sh_w2, final_g):
    B, S, D = x.shape
    depth = w_in.shape[0]
    mod_mix = adaln_rows(c, ada_mix_w, ada_mix_b)
    mod_ffn = adaln_rows(c, ada_ffn_w, ada_ffn_b)
    streams = TOKEN_STREAMS if B % TOKEN_STREAMS == 0 else 1
    Bs = B // streams
    Ns = Bs * S
    x_all = x.reshape(B * S, D)
    xs = [None] * streams
    out_all = None
    pos_s = [positions[s * Bs:(s + 1) * Bs] for s in range(streams)]
    for l in range(depth):
        last = l == depth - 1
        w_in_l = _permute_w_in(w_in[l])
        mla_w = _mla_weights(cq_norm_g[l], ckv_norm_g[l], w_uq[l], w_ukv[l])
        mix_w = (norm_ffn_g[l], pool_w[l].astype(BF16), pool_scale[l],
                 w_oa[l].astype(BF16), w_ob[l].astype(BF16), w_oc[l].astype(BF16), w_out[l].astype(BF16),
                 router_w[l].T.astype(BF16), sh_w1[l].astype(BF16), sh_w3[l].astype(BF16), sh_w2[l].astype(BF16))
        for s in range(streams):
            mod1 = mod_mix[l, s * Bs:(s + 1) * Bs].reshape(Bs, 1, 3 * D)
            mod2 = mod_ffn[l, s * Bs:(s + 1) * Bs].reshape(Bs, 1, 3 * D)
            if l == 0:
                x2, row0 = x_all, s * Ns
                gu, lat, *qkv = in_projection(x2, norm_mix_g[l], mod1, w_in_l, S, row0)
            else:
                row0 = 0
                x2, gu, lat, *qkv = in_projection(xs[s][0], norm_mix_g[l], mod1, w_in_l, S, 0, xs[s][1:])
            dil = [dilated_attention(qkv[2 * g], qkv[2 * g + 1]) for g in range(len(DIL_GROUPS))]
            q_all, k_all, vt_all = mla_prep(lat, pos_s[s], *mla_w, Bs, S)
            yc = mla_attention(q_all, k_all, vt_all, Bs, S)
            xmid, h2a, h2b, logits_t = mix_out(x2, gu, dil, yc, mod1, mod2, *mix_w, S, row0)
            dest, w_k, counts = route(logits_t, router_bias[l])
            tables = block_tables(counts, Ns)
            n_slots = tables[0].shape[0] * SLOT_BLOCK
            xa = sc_scatter_rows(h2a, dest, n_slots)
            xb = sc_scatter_rows(h2b, dest, n_slots)
            ya, yb = routed_experts(xa, xb, tables, exp_w1, exp_w3, exp_w2, l)
            ra = sc_weighted_gather(ya, dest, w_k)
            rb = sc_weighted_gather(yb, dest, w_k)
            if last:
                out_all = combine(xmid, ra, rb, mod2, final_g, S, B * S, s * Ns, out_all)
            else:
                xs[s] = (xmid, ra, rb, mod2)
    return out_all.reshape(B, S, D)
```

```python
import functools
import math

import jax
import jax.numpy as jnp
from jax import lax
from jax.experimental import pallas as pl
from jax.experimental.pallas import tpu as pltpu
from jax.experimental.pallas import tpu_sc as plsc

F32 = jnp.float32
BF16 = jnp.bfloat16
HIGHEST = lax.Precision.HIGHEST

D_MODEL = 1024
HEAD_DIM_A = 64
HEADS_PER_GROUP_A = 4
DIL_GROUPS = ((128, 1), (512, 4), (2048, 16))
GROUP_W = HEADS_PER_GROUP_A * HEAD_DIM_A
DA = GROUP_W * len(DIL_GROUPS)
POOL_WINDOWS = (2, 4, 8, 16)
POOL_GROUP_DIM = 128
DB = POOL_GROUP_DIM * len(POOL_WINDOWS)
POOL_HALO = 16
N_HEADS_C = 8
QK_NOPE = 64
QK_ROPE = 32
V_DIM = 64
Q_LORA = 384
KV_LORA = 256
DC = N_HEADS_C * V_DIM
HEAD_PAD_C = 128
ROPE_THETA = 10000.0
N_EXPERTS = 64
TOP_K = 8
N_GROUPS = 8
TOPK_GROUPS = 4
GROUP_SIZE = N_EXPERTS // N_GROUPS
ROUTED_SCALE = 2.5
EPS = 1e-6
NEG = -1e30
Q_BLOCK = 128

LAT_W = 768
GU_W = 3 * D_MODEL + DB
IN_OUT_WIDTHS = (GU_W, LAT_W) + (2 * GROUP_W, GROUP_W) * len(DIL_GROUPS)

VMEM_LIMIT = 56 * 1024 * 1024


def _cp(sem, vmem=None):
    return pltpu.CompilerParams(dimension_semantics=sem, vmem_limit_bytes=vmem)


def _silu(v):
    return v * jax.nn.sigmoid(v)


def _nt_dot(a, b):
    return lax.dot_general(a, b, (((1,), (1,)), ((), ())), preferred_element_type=F32)


PACK_W = D_MODEL // 4
_HI_MASK = -65536


def _bf16_bits(v):
    return lax.bitcast_convert_type(v.astype(BF16).astype(F32), jnp.int32)


def _pack_row_halves(v):
    halves = []
    for h in range(2):
        lo = _bf16_bits(v[:, (2 * h) * PACK_W:(2 * h + 1) * PACK_W])
        hi = _bf16_bits(v[:, (2 * h + 1) * PACK_W:(2 * h + 2) * PACK_W])
        halves.append(lax.shift_right_logical(lo, 16) | (hi & _HI_MASK))
    return halves


def _unpack_row_halves(wa, wb):
    parts = []
    for w in (wa, wb):
        parts.append(lax.bitcast_convert_type(lax.shift_left(w, 16), F32))
        parts.append(lax.bitcast_convert_type(w & _HI_MASK, F32))
    return jnp.concatenate(parts, axis=1)


def _adaln_kernel(c_ref, w_ref, b_ref, o_ref):
    s = _silu(c_ref[...])
    o_ref[0] = jnp.dot(s, w_ref[0], preferred_element_type=F32, precision=HIGHEST) + b_ref[0]


def adaln_rows(c, w, b):
    L, D, D3 = w.shape
    B = c.shape[0]
    tn = 1024
    return pl.pallas_call(
        _adaln_kernel,
        grid=(L, D3 // tn),
        in_specs=[
            pl.BlockSpec((B, D), lambda l, j: (0, 0)),
            pl.BlockSpec((1, D, tn), lambda l, j: (l, 0, j)),
            pl.BlockSpec((1, 1, tn), lambda l, j: (l, 0, j)),
        ],
        out_specs=pl.BlockSpec((1, B, tn), lambda l, j: (l, 0, j)),
        out_shape=jax.ShapeDtypeStruct((L, B, D3), F32),
        compiler_params=_cp(("parallel", "parallel")),
        name="adaln_rows",
    )(c, w, b.reshape(L, 1, D3))


LANES = 128


def _inproj_kernel(x_ref, g_ref, mod_ref, w_ref, *refs, chunk, pending):
    o_refs, scr = refs[:-1], refs[-1]
    D = x_ref.shape[1]
    x = x_ref[...]
    if pending:
        ra_ref, rb_ref, gate_ref, x_out_ref, *o_refs = o_refs
        x = x + gate_ref[0][:, 2 * D:] * _unpack_row_halves(ra_ref[...], rb_ref[...])
        x_out_ref[...] = x
    y = x * lax.rsqrt(jnp.mean(x * x, axis=-1, keepdims=True) + EPS) * g_ref[...]
    mod = mod_ref[0]
    h = (y * (1.0 + mod[:, D:2 * D]) + mod[:, :D]).astype(BF16)
    col = 0
    for o_ref in o_refs:
        width = o_ref.shape[-1]
        if o_ref.ndim == 2:
            for c0 in range(0, width, chunk):
                cw = min(chunk, width - c0)
                o_ref[:, c0:c0 + cw] = jnp.dot(
                    h, w_ref[:, col + c0:col + c0 + cw], preferred_element_type=F32).astype(o_ref.dtype)
        else:
            dil, rows = o_ref.shape[1], o_ref.shape[2]
            z = jnp.dot(h, w_ref[:, col:col + width], preferred_element_type=F32)
            if dil == 1:
                o_ref[0, 0] = z.astype(o_ref.dtype)
            else:
                for c in range(width // LANES):
                    scr[c] = z[:, c * LANES:(c + 1) * LANES]
                for r in range(dil):
                    o_ref[0, r] = jnp.concatenate(
                        [scr[c, pl.ds(r, rows, stride=dil), :] for c in range(width // LANES)],
                        axis=1).astype(o_ref.dtype)
        col += width


def in_projection(x2, g, mod, w, seq, row0=0, pending=None):
    D = x2.shape[1]
    B = mod.shape[0]
    N = B * seq
    tm = 512
    tpb = seq // tm
    tile0 = row0 // tm
    out_specs = [pl.BlockSpec((tm, wd), lambda i: (i, 0)) for wd in IN_OUT_WIDTHS[:2]]
    out_shape = [jax.ShapeDtypeStruct((N, wd), BF16) for wd in IN_OUT_WIDTHS[:2]]
    for grp, (_, dil) in enumerate(DIL_GROUPS):
        for wd in IN_OUT_WIDTHS[2 + 2 * grp:4 + 2 * grp]:
            out_specs.append(pl.BlockSpec((1, dil, tm // dil, wd), lambda i: (i // tpb, 0, i % tpb, 0)))
            out_shape.append(jax.ShapeDtypeStruct((B, dil, seq // dil, wd), BF16))
    in_specs = [
        pl.BlockSpec((tm, D), lambda i: (i + tile0, 0)),
        pl.BlockSpec((1, D), lambda i: (0, 0)),
        pl.BlockSpec((1, 1, 3 * D), lambda i: (i // tpb, 0, 0)),
        pl.BlockSpec(w.shape, lambda i: (0, 0), pipeline_mode=pl.Buffered(1)),
    ]
    args = [x2, g.reshape(1, D), mod, w]
    if pending is not None:
        in_specs += [pl.BlockSpec((tm, PACK_W), lambda i: (i, 0)), pl.BlockSpec((tm, PACK_W), lambda i: (i, 0)),
                     pl.BlockSpec((1, 1, 3 * D), lambda i: (i // tpb, 0, 0))]
        args += list(pending)
        out_specs.insert(0, pl.BlockSpec((tm, D), lambda i: (i, 0)))
        out_shape.insert(0, jax.ShapeDtypeStruct((N, D), F32))
    return pl.pallas_call(
        functools.partial(_inproj_kernel, chunk=512, pending=pending is not None),
        grid=(N // tm,),
        in_specs=in_specs,
        out_specs=out_specs,
        out_shape=out_shape,
        scratch_shapes=[pltpu.VMEM((max(IN_OUT_WIDTHS[2:]) // LANES, tm, LANES), F32)],
        compiler_params=_cp(("parallel",), VMEM_LIMIT),
        name="in_projection",
    )(*args)


def _dilated_kernel(q_ref, kc_ref, kp_ref, vc_ref, vp_ref, o_ref, lse_ref):
    i = pl.program_id(1)
    T = Q_BLOCK
    key = lax.broadcasted_iota(jnp.int32, (T, T), 0)
    qry = lax.broadcasted_iota(jnp.int32, (T, T), 1)
    valid_c = key <= qry
    near = key >= qry
    seqs, run = q_ref.shape[0], q_ref.shape[1] // T
    heads = [slice(h * HEAD_DIM_A, (h + 1) * HEAD_DIM_A) for h in range(HEADS_PER_GROUP_A)]

    def transposed(v):
        return v.astype(F32).T.astype(BF16)

    vts = {(s, j): transposed(vc_ref[s, j * T:(j + 1) * T, :]) for s in range(seqs) for j in range(run)}
    vt_before = [transposed(vp_ref[s]) for s in range(seqs)]

    def blocks(s, j):
        rows = slice(j * T, (j + 1) * T)
        if j == 0:
            return rows, kc_ref[s, rows, :], vts[s, 0], kp_ref[s], vt_before[s], near & (i > 0)
        before = slice((j - 1) * T, j * T)
        return rows, kc_ref[s, rows, :], vts[s, j], kc_ref[s, before, :], vts[s, j - 1], near

    scores, probs = {}, {}
    for s in range(seqs):
        for j in range(run):
            rows, kc, _, kp, _, valid_p = blocks(s, j)
            q = q_ref[s, rows, :]
            for h, sl in enumerate(heads):
                qh = q[:, sl]
                scores[s, j, h] = (jnp.where(valid_c, _nt_dot(kc[:, sl], qh), NEG),
                                   jnp.where(valid_p, _nt_dot(kp[:, sl], qh), NEG))
    for chain, (sc, sp) in scores.items():
        m = jnp.maximum(jnp.max(sc, axis=0, keepdims=True), jnp.max(sp, axis=0, keepdims=True))
        pc = jnp.exp(sc - m)
        pp = jnp.exp(sp - m)
        den = jnp.sum(pc, axis=0, keepdims=True) + jnp.sum(pp, axis=0, keepdims=True)
        probs[chain] = (pc.astype(BF16), pp.astype(BF16), den, m + jnp.log(den))
    spread = LSE_LANES // len(heads)
    for s in range(seqs):
        for j in range(run):
            rows, _, vtc, _, vtp, _ = blocks(s, j)
            outs = []
            for h, sl in enumerate(heads):
                pc, pp, den, _ = probs[s, j, h]
                o = (jnp.dot(vtc[sl, :], pc, preferred_element_type=F32)
                     + jnp.dot(vtp[sl, :], pp, preferred_element_type=F32))
                outs.append(o / den)
            o_ref[s, rows, :] = jnp.concatenate(outs, axis=0).T.astype(o_ref.dtype)
            lse_t = jnp.concatenate(
                [jnp.broadcast_to(probs[s, j, h][3], (spread, T)) for h in range(len(heads))], axis=0)
            lse_ref[s, rows, :] = lse_t.T


DILATED_RUN = 8


LSE_LANES = 128


def dilated_attention(qk, v):
    batch, dilation, L, _ = qk.shape
    nb = L // Q_BLOCK
    run = min(DILATED_RUN, nb)
    seqs = DILATED_RUN // run
    qk_r = qk.reshape(batch * dilation, L, 2 * GROUP_W)
    v_r = v.reshape(batch * dilation, L, GROUP_W)
    before = lambda i: jnp.maximum(i * run - 1, 0)
    o, lse = pl.pallas_call(
        _dilated_kernel,
        grid=(batch * dilation // seqs, nb // run),
        in_specs=[
            pl.BlockSpec((seqs, run * Q_BLOCK, GROUP_W), lambda s, i: (s, i, 0)),
            pl.BlockSpec((seqs, run * Q_BLOCK, GROUP_W), lambda s, i: (s, i, 1)),
            pl.BlockSpec((seqs, Q_BLOCK, GROUP_W), lambda s, i: (s, before(i), 1)),
            pl.BlockSpec((seqs, run * Q_BLOCK, GROUP_W), lambda s, i: (s, i, 0)),
            pl.BlockSpec((seqs, Q_BLOCK, GROUP_W), lambda s, i: (s, before(i), 0)),
        ],
        out_specs=[
            pl.BlockSpec((seqs, run * Q_BLOCK, GROUP_W), lambda s, i: (s, i, 0)),
            pl.BlockSpec((seqs, run * Q_BLOCK, LSE_LANES), lambda s, i: (s, i, 0)),
        ],
        out_shape=[
            jax.ShapeDtypeStruct((batch * dilation, L, GROUP_W), BF16),
            jax.ShapeDtypeStruct((batch * dilation, L, LSE_LANES), F32),
        ],
        compiler_params=_cp(("parallel", "parallel")),
        name=f"dilated_attention_d{dilation}",
    )(qk_r, qk_r, qk_r, v_r, v_r)
    return o.reshape(batch, dilation, L, GROUP_W), lse.reshape(batch, dilation, L, LSE_LANES)


def _mla_prep_kernel(lat_ref, pos_ref, gq_ref, gkv_ref, wq_ref, wk_ref, wvt_ref, freq_ref, spread_ref, one_ref,
                     q_ref, k_ref, vt_ref):
    HP = N_HEADS_C * HEAD_PAD_C
    lat = lat_ref[...].astype(F32)
    cq = lat[:, :Q_LORA]
    ckr = lat[:, Q_LORA:]
    zq = (cq * lax.rsqrt(jnp.mean(cq * cq, axis=-1, keepdims=True) + EPS) * gq_ref[...]).astype(BF16)
    lane = lax.broadcasted_iota(jnp.int32, ckr.shape, 1)
    is_kv = lane < KV_LORA
    ms = jnp.sum(jnp.where(is_kv, ckr * ckr, 0.0), axis=-1, keepdims=True) * (1.0 / KV_LORA)
    zkv = (ckr * jnp.where(is_kv, lax.rsqrt(ms + EPS) * gkv_ref[...], 1.0)).astype(BF16)
    qq = jnp.dot(zq, wq_ref[...], preferred_element_type=F32)
    kk = jnp.dot(zkv, wk_ref[:, :HP], preferred_element_type=F32)
    kk_sw = jnp.dot(zkv[:, KV_LORA:], wk_ref[KV_LORA:, HP:], preferred_element_type=F32)
    ang_t = freq_ref[...] * pos_ref[0].astype(F32)

    def to_lanes(t):
        hi = t.astype(BF16)
        lo = (t - hi.astype(F32)).astype(BF16)
        tn_dot = lambda a: lax.dot_general(a, spread_ref[...], (((0,), (0,)), ((), ())), preferred_element_type=F32)
        return tn_dot(hi) + tn_dot(lo)

    cos = to_lanes(jnp.cos(ang_t)) + one_ref[...]
    sin = to_lanes(jnp.sin(ang_t))
    for h in range(N_HEADS_C):
        lo, hi = h * HEAD_PAD_C, (h + 1) * HEAD_PAD_C
        q_ref[:, lo:hi] = (qq[:, lo:hi] * cos + qq[:, HP + lo:HP + hi] * sin).astype(q_ref.dtype)
        k_ref[:, lo:hi] = (kk[:, lo:hi] * cos + kk_sw[:, lo:hi] * sin).astype(k_ref.dtype)
    vt_ref[0] = _nt_dot(wvt_ref[...], zkv).astype(vt_ref.dtype)


def _mla_weights(cq_g, ckv_g, w_uq, w_ukv):
    H, HPAD, half = N_HEADS_C, HEAD_PAD_C, QK_ROPE // 2
    scale = (QK_NOPE + QK_ROPE) ** -0.5 * math.log2(math.e)
    wq = w_uq.reshape(Q_LORA, H, QK_NOPE + QK_ROPE) * scale
    q_lin = jnp.pad(wq, ((0, 0), (0, 0), (0, HPAD - QK_NOPE - QK_ROPE)))
    r1, r2 = wq[..., QK_NOPE:QK_NOPE + half], wq[..., QK_NOPE + half:]
    q_sw = jnp.concatenate([jnp.zeros((Q_LORA, H, QK_NOPE), F32), -r2, r1,
                            jnp.zeros((Q_LORA, H, HPAD - QK_NOPE - QK_ROPE), F32)], axis=-1)
    wq_big = jnp.concatenate([q_lin.reshape(Q_LORA, H * HPAD), q_sw.reshape(Q_LORA, H * HPAD)], axis=1)

    rows = LAT_W - Q_LORA
    wkv = w_ukv.reshape(KV_LORA, H, QK_NOPE + V_DIM)
    eye = jnp.eye(QK_ROPE, dtype=F32)
    k_lin = jnp.zeros((rows, H, HPAD), F32)
    k_lin = k_lin.at[:KV_LORA, :, :QK_NOPE].set(wkv[..., :QK_NOPE])
    k_lin = k_lin.at[KV_LORA:KV_LORA + QK_ROPE, :, QK_NOPE:QK_NOPE + QK_ROPE].set(
        jnp.broadcast_to(eye[:, None, :], (QK_ROPE, H, QK_ROPE)))
    swap = jnp.zeros((QK_ROPE, QK_ROPE), F32).at[half:, :half].set(-jnp.eye(half)).at[:half, half:].set(jnp.eye(half))
    k_sw = jnp.zeros((rows, H, HPAD), F32)
    k_sw = k_sw.at[KV_LORA:KV_LORA + QK_ROPE, :, QK_NOPE:QK_NOPE + QK_ROPE].set(
        jnp.broadcast_to(swap[:, None, :], (QK_ROPE, H, QK_ROPE)))
    v_w = jnp.zeros((rows, H, V_DIM), F32).at[:KV_LORA].set(wkv[..., QK_NOPE:])
    wk_big = jnp.concatenate([k_lin.reshape(rows, H * HPAD), k_sw.reshape(rows, H * HPAD)], axis=1)
    wv_t = v_w.reshape(rows, H * V_DIM).T

    gkv = jnp.concatenate([ckv_g, jnp.ones((rows - KV_LORA,), F32)]).reshape(1, rows)
    return cq_g.reshape(1, Q_LORA), gkv, wq_big.astype(BF16), wk_big.astype(BF16), wv_t.astype(BF16)


def _rope_tables():
    half = QK_ROPE // 2
    freqs = (ROPE_THETA ** (-jnp.arange(0, QK_ROPE, 2, dtype=F32) / QK_ROPE)).reshape(half, 1)
    lane = jnp.arange(HEAD_PAD_C)[None, :]
    j = jnp.arange(half)[:, None]
    spread = (lane == QK_NOPE + j) | (lane == QK_NOPE + half + j)
    off_rope = ~jnp.any(spread, axis=0, keepdims=True)
    return freqs, spread.astype(BF16), off_rope.astype(F32)


def mla_prep(lat, positions, gq, gkv, wq_big, wk_big, wv_t, batch, seq):
    N = lat.shape[0]
    HP = N_HEADS_C * HEAD_PAD_C
    tm = 512
    tpb = seq // tm
    freqs, spread, off_rope = _rope_tables()
    pos_rows = positions.reshape(N // tm, 1, tm)
    const = lambda shape: pl.BlockSpec(shape, lambda i: (0, 0))
    return pl.pallas_call(
        _mla_prep_kernel,
        grid=(N // tm,),
        in_specs=[
            pl.BlockSpec((tm, LAT_W), lambda i: (i, 0)),
            pl.BlockSpec((1, 1, tm), lambda i: (i, 0, 0)),
            const(gq.shape), const(gkv.shape), const(wq_big.shape), const(wk_big.shape), const(wv_t.shape),
            const(freqs.shape), const(spread.shape), const(off_rope.shape),
        ],
        out_specs=[
            pl.BlockSpec((tm, HP), lambda i: (i, 0)),
            pl.BlockSpec((tm, HP), lambda i: (i, 0)),
            pl.BlockSpec((1, DC, tm), lambda i: (i // tpb, 0, i % tpb)),
        ],
        out_shape=[
            jax.ShapeDtypeStruct((N, HP), BF16),
            jax.ShapeDtypeStruct((N, HP), BF16),
            jax.ShapeDtypeStruct((batch, DC, seq), BF16),
        ],
        compiler_params=_cp(("parallel",), VMEM_LIMIT),
        name="mla_prep",
    )(lat, pos_rows, gq, gkv, wq_big, wk_big, wv_t, freqs, spread, off_rope)


HEADS_PER_STEP_C = 8
FLASH_Q_CHUNK = 256


def _mla_flash_kernel(qi_ref, ki_ref, q_ref, k_ref, vt_ref, o_ref, m_sc, l_sc, acc_sc):
    t = pl.program_id(2)
    qi, ki = qi_ref[t], ki_ref[t]

    @pl.when(ki == 0)
    def _():
        m_sc[...] = jnp.full(m_sc.shape, NEG, F32)
        l_sc[...] = jnp.zeros(l_sc.shape, F32)
        acc_sc[...] = jnp.zeros(acc_sc.shape, F32)

    def step(masked):
        T = q_ref.shape[1]
        if masked:
            key = lax.broadcasted_iota(jnp.int32, (T, T), 0)
            qry = lax.broadcasted_iota(jnp.int32, (T, T), 1)
            keep = key <= qry
        chains = [(h, c) for h in range(HEADS_PER_STEP_C) for c in range(T // FLASH_Q_CHUNK)]
        scores, probs, alphas = {}, {}, {}

        def keys_for(c):
            return (c + 1) * FLASH_Q_CHUNK if masked else T

        def qk(h, c):
            qs = slice(c * FLASH_Q_CHUNK, (c + 1) * FLASH_Q_CHUNK)
            q = q_ref[0, qs, h * HEAD_PAD_C:(h + 1) * HEAD_PAD_C]
            k = k_ref[0, :keys_for(c), h * HEAD_PAD_C:(h + 1) * HEAD_PAD_C]
            st = _nt_dot(k, q)
            scores[h, c] = jnp.where(keep[:keys_for(c), qs], st, NEG) if masked else st

        def softmax(h, c):
            qs = slice(c * FLASH_Q_CHUNK, (c + 1) * FLASH_Q_CHUNK)
            st = scores.pop((h, c))
            m_prev = m_sc[h, :, qs]
            m_new = jnp.maximum(m_prev, jnp.max(st, axis=0, keepdims=True))
            alpha = jnp.exp2(m_prev - m_new)
            p = jnp.exp2(st - m_new)
            l_sc[h, :, qs] = alpha * l_sc[h, :, qs] + jnp.sum(p, axis=0, keepdims=True)
            m_sc[h, :, qs] = m_new
            probs[h, c], alphas[h, c] = p.astype(BF16), alpha

        def pv(h, c):
            qs = slice(c * FLASH_Q_CHUNK, (c + 1) * FLASH_Q_CHUNK)
            vt = vt_ref[0, h * V_DIM:(h + 1) * V_DIM, :keys_for(c)]
            acc_sc[h, :, qs] = alphas.pop((h, c)) * acc_sc[h, :, qs] + jnp.dot(
                vt, probs.pop((h, c)), preferred_element_type=F32)

        for phase in (qk, softmax, pv):
            for ch in chains:
                phase(*ch)

    @pl.when(ki < qi)
    def _():
        step(False)

    @pl.when(ki == qi)
    def _():
        step(True)
        ot = jnp.concatenate([acc_sc[h] / l_sc[h] for h in range(HEADS_PER_STEP_C)], axis=0)
        o_ref[0] = ot.T.astype(o_ref.dtype)


def mla_attention(q_all, k_all, vt_all, batch, seq):
    T = 512
    nq = seq // T
    pairs = [(a, b) for a in range(nq) for b in range(a + 1)]
    qi_tab = jnp.asarray([p[0] for p in pairs], jnp.int32)
    ki_tab = jnp.asarray([p[1] for p in pairs], jnp.int32)
    hp = N_HEADS_C // HEADS_PER_STEP_C
    qw = HEADS_PER_STEP_C * HEAD_PAD_C
    vw = HEADS_PER_STEP_C * V_DIM
    q3 = q_all.reshape(batch, seq, -1)
    k3 = k_all.reshape(batch, seq, -1)
    grid_spec = pltpu.PrefetchScalarGridSpec(
        num_scalar_prefetch=2,
        grid=(batch, hp, len(pairs)),
        in_specs=[
            pl.BlockSpec((1, T, qw), lambda b, h, t, qi, ki: (b, qi[t], h)),
            pl.BlockSpec((1, T, qw), lambda b, h, t, qi, ki: (b, ki[t], h)),
            pl.BlockSpec((1, vw, T), lambda b, h, t, qi, ki: (b, h, ki[t])),
        ],
        out_specs=pl.BlockSpec((1, T, vw), lambda b, h, t, qi, ki: (b, qi[t], h)),
        scratch_shapes=[
            pltpu.VMEM((HEADS_PER_STEP_C, 1, T), F32),
            pltpu.VMEM((HEADS_PER_STEP_C, 1, T), F32),
            pltpu.VMEM((HEADS_PER_STEP_C, V_DIM, T), F32),
        ],
    )
    o = pl.pallas_call(
        _mla_flash_kernel,
        grid_spec=grid_spec,
        out_shape=jax.ShapeDtypeStruct((batch, seq, DC), BF16),
        compiler_params=_cp(("parallel", "parallel", "arbitrary")),
        name="mla_attention",
    )(qi_tab, ki_tab, q3, k3, vt_all)
    return o.reshape(batch * seq, DC)


def _mixout_kernel(x_ref, gates_ref, ub_ref, ubh_ref, o1_ref, o2_ref, o3_ref, l1_ref, l2_ref, l3_ref, yc_ref,
                   mod1_ref, mod2_ref, g2_ref, poolw_ref, pscale_ref, woa_ref, wob_ref, woc_ref, wout_ref,
                   rwt_ref, sw1_ref, sw3_ref, sw2_ref, spread_ref,
                   xmid_ref, h2a_ref, h2b_ref, logit_ref, *scratch, tiles_per_batch):
    D = x_ref.shape[1]
    tm = x_ref.shape[0]
    tile = pl.program_id(0) % tiles_per_batch
    o_scrs, l_scrs = scratch[:3], scratch[3:]

    def token_order(ref, scr):
        dil, rows, width = ref.shape[1:]
        if dil == 1:
            return ref[0, 0].astype(F32)
        for r in range(dil):
            v = ref[0, r].astype(F32)
            for c in range(width // LANES):
                scr[c, pl.ds(r, rows, stride=dil), :] = v[:, c * LANES:(c + 1) * LANES]
        return jnp.concatenate([scr[c] for c in range(width // LANES)], axis=1)

    outs = [token_order(r, s) for r, s in zip((o1_ref, o2_ref, o3_ref), o_scrs)]
    l1, l2, l3 = [token_order(r, s) for r, s in zip((l1_ref, l2_ref, l3_ref), l_scrs)]
    mx = jnp.maximum(jnp.maximum(l1, l2), l3)
    es = [jnp.exp(l1 - mx), jnp.exp(l2 - mx), jnp.exp(l3 - mx)]
    inv = 1.0 / (es[0] + es[1] + es[2])
    ya = jnp.zeros((tm, GROUP_W), F32)
    for e, o in zip(es, outs):
        w = e * inv
        w_hi = w.astype(BF16)
        w_lo = (w - w_hi.astype(F32)).astype(BF16)
        w_wide = (jnp.dot(w_hi, spread_ref[...], preferred_element_type=F32)
                  + jnp.dot(w_lo, spread_ref[...], preferred_element_type=F32))
        ya = ya + w_wide * o
    a_out = jnp.dot(ya.astype(BF16), woa_ref[...], preferred_element_type=F32)

    u = ub_ref[...].astype(F32)
    halo = jnp.where(tile > 0, ubh_ref[...].astype(F32), 0.0)
    ext = jnp.concatenate([halo, u], axis=0)
    t_seq = tile * tm + lax.broadcasted_iota(jnp.int32, (tm, 1), 0)
    pooled = []
    for gi, w in enumerate(POOL_WINDOWS):
        sl = slice(gi * POOL_GROUP_DIM, (gi + 1) * POOL_GROUP_DIM)
        acc = ext[:, sl]
        k = 1
        while k < w:
            acc = acc + pltpu.roll(acc, k, axis=0)
            k *= 2
        cnt = jnp.minimum(t_seq + 1, w).astype(F32)
        pg = acc[POOL_HALO:] / cnt - u[:, sl]
        pooled.append(jnp.dot(pg.astype(BF16), poolw_ref[gi], preferred_element_type=F32))
    yb = jnp.concatenate(pooled, axis=1) * pscale_ref[...]
    b_out = jnp.dot(yb.astype(BF16), wob_ref[...], preferred_element_type=F32)
    c_out = jnp.dot(yc_ref[...], woc_ref[...], preferred_element_type=F32)

    g = gates_ref[...].astype(F32)
    mix = (jax.nn.sigmoid(g[:, :D]) * a_out + jax.nn.sigmoid(g[:, D:2 * D]) * b_out
           + jax.nn.sigmoid(g[:, 2 * D:]) * c_out)
    tok = jnp.dot(mix.astype(BF16), wout_ref[...], preferred_element_type=F32)
    xn = x_ref[...] + mod1_ref[0][:, 2 * D:] * tok

    mod2 = mod2_ref[0]
    y = xn * lax.rsqrt(jnp.mean(xn * xn, axis=-1, keepdims=True) + EPS) * g2_ref[...]
    h2 = y * (1.0 + mod2[:, D:2 * D]) + mod2[:, :D]
    h2b = h2.astype(BF16)
    h2a_ref[...], h2b_ref[...] = _pack_row_halves(h2b)
    logit_ref[...] = _nt_dot(rwt_ref[...], h2b)
    hid = _silu(jnp.dot(h2b, sw1_ref[...], preferred_element_type=F32)) * jnp.dot(
        h2b, sw3_ref[...], preferred_element_type=F32)
    shared = jnp.dot(hid.astype(BF16), sw2_ref[...], preferred_element_type=F32)
    xmid_ref[...] = xn + mod2[:, 2 * D:] * shared


def mix_out(x2, gu, dil, yc, mod1, mod2, g2, pool_w, pool_scale, w_oa, w_ob, w_oc, w_out, rwt, sw1, sw3, sw2, seq,
            row0=0):
    D = x2.shape[1]
    N = gu.shape[0]
    tm = 512
    tpb = seq // tm
    tile0 = row0 // tm
    (o1, l1), (o2, l2), (o3, l3) = dil
    row = lambda w, c=0: pl.BlockSpec((tm, w), lambda i: (i, c))
    by_residue = lambda a: pl.BlockSpec(
        (1, a.shape[1], tm // a.shape[1], a.shape[3]), lambda i: (i // tpb, 0, i % tpb, 0))
    heads = HEADS_PER_GROUP_A
    spread = (jnp.arange(LSE_LANES)[:, None] == (jnp.arange(GROUP_W)[None, :] // HEAD_DIM_A) * (LSE_LANES // heads)
              ).astype(BF16)
    const2 = lambda a: pl.BlockSpec(a.shape, lambda i: (0,) * a.ndim, pipeline_mode=pl.Buffered(1))
    modspec = pl.BlockSpec((1, 1, 3 * D), lambda i: (i // tpb, 0, 0))
    ub_col = 3 * D // DB
    halo_spec = pl.BlockSpec(
        (POOL_HALO, DB), lambda i: (jnp.maximum(i * (tm // POOL_HALO) - 1, 0), ub_col))
    weights = [g2.reshape(1, D), pool_w, pool_scale.reshape(1, DB), w_oa, w_ob, w_oc, w_out, rwt, sw1, sw3, sw2,
               spread]
    return pl.pallas_call(
        functools.partial(_mixout_kernel, tiles_per_batch=tpb),
        grid=(N // tm,),
        in_specs=[
            pl.BlockSpec((tm, D), lambda i: (i + tile0, 0)), row(3 * D), row(DB, ub_col), halo_spec,
            by_residue(o1), by_residue(o2), by_residue(o3), by_residue(l1), by_residue(l2), by_residue(l3), row(DC),
            modspec, modspec,
        ] + [const2(a) for a in weights],
        scratch_shapes=[pltpu.VMEM((GROUP_W // LANES, tm, LANES), F32)] * 3
        + [pltpu.VMEM((LSE_LANES // LANES, tm, LANES), F32)] * 3,
        out_specs=[row(D), row(PACK_W), row(PACK_W), pl.BlockSpec((N_EXPERTS, tm), lambda i: (0, i))],
        out_shape=[
            jax.ShapeDtypeStruct((N, D), F32),
            jax.ShapeDtypeStruct((N, PACK_W), jnp.int32),
            jax.ShapeDtypeStruct((N, PACK_W), jnp.int32),
            jax.ShapeDtypeStruct((N_EXPERTS, N), F32),
        ],
        compiler_params=_cp(("parallel",), VMEM_LIMIT),
        name="mix_out",
    )(x2, gu, gu, gu, o1, o2, o3, l1, l2, l3, yc, mod1, mod2, *weights)


def _pick_rows(table, picks):
    G, GS = N_GROUPS, GROUP_SIZE
    eio = lax.broadcasted_iota(jnp.int32, (GS, table.shape[1]), 0)
    rows = []
    for k in range(TOP_K):
        idx = picks[k:k + 1]
        parts = [jnp.where(eio + g * GS == idx, table[g * GS:(g + 1) * GS], 0.0) for g in range(G)]
        rows.append(jnp.sum(functools.reduce(jnp.add, parts), axis=0, keepdims=True))
    return jnp.concatenate(rows, axis=0)


def _route_choose(lg_ref, bias_ref):
    G, GS = N_GROUPS, GROUP_SIZE
    scores = jax.nn.sigmoid(lg_ref[...])
    sel = scores + bias_ref[...]
    tn = sel.shape[1]
    eio = lax.broadcasted_iota(jnp.int32, (GS, tn), 0)
    ninf = -jnp.inf

    gs = []
    for g in range(G):
        v = sel[g * GS:(g + 1) * GS]
        m1 = jnp.max(v, axis=0, keepdims=True)
        i1 = jnp.min(jnp.where(v == m1, eio, GS), axis=0, keepdims=True)
        m2 = jnp.max(jnp.where(eio == i1, ninf, v), axis=0, keepdims=True)
        gs.append(m1 + m2)
    gsm = jnp.concatenate(gs, axis=0)
    gio = lax.broadcasted_iota(jnp.int32, (G, tn), 0)
    rank = jnp.zeros((G, tn), jnp.int32)
    for g2 in range(G):
        beats = (gs[g2] > gsm) | ((gs[g2] == gsm) & (g2 < gio))
        rank = rank + beats.astype(jnp.int32)
    gsel = rank < TOPK_GROUPS

    vs = [jnp.where(gsel[g:g + 1], sel[g * GS:(g + 1) * GS], NEG) for g in range(G)]
    eid = [eio + g * GS for g in range(G)]
    chosen = [jnp.zeros((GS, tn), jnp.bool_) for _ in range(G)]
    picks = []
    for _ in range(TOP_K):
        m = jnp.max(functools.reduce(jnp.maximum, vs), axis=0, keepdims=True)
        idx = jnp.min(functools.reduce(jnp.minimum, [jnp.where(v == m, e, N_EXPERTS) for v, e in zip(vs, eid)]),
                      axis=0, keepdims=True)
        picks.append(idx)
        for g in range(G):
            hit = eid[g] == idx
            chosen[g] = chosen[g] | hit
            vs[g] = jnp.where(hit, ninf, vs[g])
    mask = jnp.concatenate(chosen, axis=0).astype(F32)
    return scores, jnp.concatenate(picks, axis=0), mask


def _route_kernel(lg_ref, bias_ref, tri_ref, dest_ref, w_ref, cnt_ref, run_sc, start_sc, mask_sc, picks_sc,
                  *, slot_block):
    phase = pl.program_id(0)
    step = pl.program_id(1)
    tn = lg_ref.shape[1]
    cols = pl.ds(pl.multiple_of(step * tn, tn), tn)

    @pl.when(phase == 0)
    def _():
        @pl.when(step == 0)
        def _():
            run_sc[...] = jnp.zeros(run_sc.shape, F32)

        scores, picks, mask = _route_choose(lg_ref, bias_ref)
        wk = _pick_rows(scores, picks)
        w_ref[0] = wk / jnp.sum(wk, axis=0, keepdims=True) * ROUTED_SCALE
        dest_ref[0] = jnp.zeros(dest_ref.shape[1:], dest_ref.dtype)
        mask_sc[:, cols] = mask.astype(BF16)
        picks_sc[:, cols] = picks
        run_sc[...] = run_sc[...] + jnp.sum(mask, axis=1, keepdims=True)

    @pl.when(phase == 1)
    def _():
        @pl.when(step == 0)
        def _():
            counts = run_sc[...].astype(jnp.int32)
            cnt_ref[...] = jnp.broadcast_to(counts, cnt_ref.shape)
            shift = slot_block.bit_length() - 1
            padded = lax.shift_left(lax.shift_right_logical(counts + (slot_block - 1), shift), shift).astype(F32)
            r = lax.broadcasted_iota(jnp.int32, (N_EXPERTS, N_EXPERTS), 0)
            c = lax.broadcasted_iota(jnp.int32, (N_EXPERTS, N_EXPERTS), 1)
            as_row = jnp.sum(jnp.where(r == c, padded, 0.0), axis=0, keepdims=True)
            start_sc[...] = jnp.sum(jnp.where(c < r, as_row, 0.0), axis=1, keepdims=True)
            run_sc[...] = jnp.zeros(run_sc.shape, F32)

        mask_b = mask_sc[:, cols]
        mask = mask_b.astype(F32)
        before = jnp.dot(mask_b, tri_ref[...], preferred_element_type=F32) - mask
        slot = start_sc[...] + run_sc[...] + before
        dest_ref[0] = _pick_rows(slot, picks_sc[:, cols]).astype(jnp.int32)
        w_ref[0] = jnp.zeros(w_ref.shape[1:], w_ref.dtype)
        run_sc[...] = run_sc[...] + jnp.sum(mask, axis=1, keepdims=True)


SLOT_BLOCK = 512


def route(logits_t, bias):
    E, N = logits_t.shape
    tn = 1024
    tri = (jnp.arange(tn)[:, None] <= jnp.arange(tn)[None, :]).astype(BF16)
    plane = lambda: pl.BlockSpec((1, TOP_K, tn), lambda p, i: (p, 0, i))
    dest, w, cnt = pl.pallas_call(
        functools.partial(_route_kernel, slot_block=SLOT_BLOCK),
        grid=(2, N // tn),
        in_specs=[
            pl.BlockSpec((E, tn), lambda p, i: (0, i * (1 - p))),
            pl.BlockSpec((E, 1), lambda p, i: (0, 0)),
            pl.BlockSpec((tn, tn), lambda p, i: (0, 0)),
        ],
        out_specs=[plane(), plane(), pl.BlockSpec((E, 128), lambda p, i: (0, 0))],
        out_shape=[
            jax.ShapeDtypeStruct((2, TOP_K, N), jnp.int32),
            jax.ShapeDtypeStruct((2, TOP_K, N), F32),
            jax.ShapeDtypeStruct((E, 128), jnp.int32),
        ],
        scratch_shapes=[pltpu.VMEM((E, 1), F32), pltpu.VMEM((E, 1), F32),
                        pltpu.VMEM((E, N), BF16), pltpu.VMEM((TOP_K, N), jnp.int32)],
        compiler_params=_cp(("arbitrary", "arbitrary")),
        name="route",
    )(logits_t, bias.reshape(E, 1), tri)
    return dest[1], w[0], cnt[:, 0]


def block_tables(counts, n_tokens):
    E = counts.shape[0]
    blk = SLOT_BLOCK
    nblk = (n_tokens * TOP_K + E * blk) // blk
    per_expert = (counts + blk - 1) // blk
    bend = jnp.cumsum(per_expert)
    bstart = bend - per_expert
    b = jnp.arange(nblk, dtype=jnp.int32)[:, None]
    owns = (bstart[None, :] <= b) & (b < bend[None, :])
    blk_e = jnp.minimum(jnp.sum(bend[None, :] <= b, axis=1), E - 1).astype(jnp.int32)
    rows_left = counts[None, :] - (b - bstart[None, :]) * blk
    nvalid = jnp.sum(jnp.where(owns, jnp.clip(rows_left, 0, blk), 0), axis=1)
    first = jnp.concatenate([jnp.ones((1,), jnp.bool_), blk_e[1:] != blk_e[:-1]])
    run_parity = ((jnp.cumsum(first.astype(jnp.int32)) - 1) % 2).astype(jnp.int32)
    later = blk_e[None, :] > blk_e[:, None]
    next_e = jnp.min(jnp.where(later, blk_e[None, :], E), axis=1).astype(jnp.int32)
    return blk_e, nvalid.astype(jnp.int32), run_parity, next_e


def _sc_mesh():
    return plsc.VectorSubcoreMesh(core_axis_name="c", subcore_axis_name="s")


SC_WINDOW = 128


def sc_scatter_rows(x, dest, n_slots):
    N, W = x.shape
    K = dest.shape[0]

    @functools.partial(pl.kernel, out_type=jax.ShapeDtypeStruct((n_slots, W), x.dtype), mesh=_sc_mesh(),
                       scratch_types=[])
    def scatter(x_hbm, i_hbm, o_hbm):
        def body(x_vmem, i_vmem):
            for k in range(K):
                pltpu.sync_copy(x_vmem, o_hbm.at[i_vmem.at[k]])

        pltpu.emit_pipeline(
            body,
            grid=(N // SC_WINDOW,),
            in_specs=[pl.BlockSpec((SC_WINDOW, W), lambda i: (i, 0)),
                      pl.BlockSpec((K, SC_WINDOW), lambda i: (0, i))],
            out_specs=[],
            core_axis_name=("c", "s"),
            dimension_semantics=(pltpu.PARALLEL,),
        )(x_hbm, i_hbm)

    return scatter(x, dest)


SC_LANES = 16
SC_GATHER_TOKENS = 8


def sc_weighted_gather(y, dest, wts):
    W = y.shape[1]
    K, N = dest.shape
    G, L = SC_GATHER_TOKENS, SC_LANES
    batches = SC_WINDOW // G

    @functools.partial(
        pl.kernel, out_type=jax.ShapeDtypeStruct((N, W), y.dtype), mesh=_sc_mesh(),
        scratch_types=[pltpu.VMEM((2, K, G, W), y.dtype), pltpu.SemaphoreType.DMA((2,))],
        compiler_params=pltpu.CompilerParams(needs_layout_passes=False))
    def gather(y_hbm, i_hbm, w_hbm, o_hbm, rows2, sems):
        def body(i_vmem, w_vmem, o_vmem):
            def fetch(batch, slot):
                return [pltpu.make_async_copy(y_hbm.at[i_vmem.at[k, pl.ds(batch * G, G)]], rows2.at[slot, k],
                                              sems.at[slot]) for k in range(K)]

            for c in fetch(0, 0):
                c.start()

            @pl.loop(0, batches)
            def _(batch):
                slot = batch % 2

                @pl.when(batch + 1 < batches)
                def _():
                    for c in fetch(batch + 1, 1 - slot):
                        c.start()

                for c in fetch(batch, slot):
                    c.wait()
                rows = rows2.at[slot]

                @pl.loop(0, G)
                def _(t):
                    tok = jnp.full((L,), batch * G + t, jnp.int32)
                    wk = [plsc.load_gather(w_vmem, [jnp.full((L,), k, jnp.int32), tok]) for k in range(K)]

                    @plsc.parallel_loop(0, W // L, unroll=W // L)
                    def _(j):
                        lo = jnp.zeros((L,), F32)
                        hi = jnp.zeros((L,), F32)
                        for k in range(K):
                            pair = plsc.bitcast(rows[k, t, pl.ds(j * L, L)], BF16)
                            a, b = plsc.unpack(pair, format=plsc.PackFormat.INTERLEAVED)
                            lo = lo + wk[k] * a
                            hi = hi + wk[k] * b
                        o_vmem[batch * G + t, pl.ds(j * L, L)] = plsc.bitcast(
                            plsc.pack(lo, hi, format=plsc.PackFormat.INTERLEAVED), y.dtype)

        pltpu.emit_pipeline(
            body,
            grid=(N // SC_WINDOW,),
            in_specs=[pl.BlockSpec((K, SC_WINDOW), lambda i: (0, i)),
                      pl.BlockSpec((K, SC_WINDOW), lambda i: (0, i))],
            out_specs=[pl.BlockSpec((SC_WINDOW, W), lambda i: (i, 0))],
            core_axis_name=("c", "s"),
            dimension_semantics=(pltpu.PARALLEL,),
        )(i_hbm, w_hbm, o_hbm)

    return gather(y, dest, wts)


EXPERT_INPUT_SLOTS = 3


def _expert_kernel(blk_e_ref, nvalid_ref, parity_ref, next_e_ref, xa_hbm, xb_hbm, w1_hbm, w3_hbm, w2_hbm,
                   ya_ref, yb_ref, w1_sc, w3_sc, w2_sc, xa_buf, xb_buf, sems, w1_st, w3_st, w2_st, wsems,
                   *, layer):
    b = pl.program_id(0)
    nb = pl.num_programs(0)
    nv = nvalid_ref[b]
    prev_e = blk_e_ref[jnp.maximum(b - 1, 0)]
    blk = xa_buf.shape[1]
    ring = EXPERT_INPUT_SLOTS
    n_experts = w1_hbm.shape[1]

    def fetch_weights(e, par):
        return (pltpu.make_async_copy(w1_hbm.at[layer, e], w1_st.at[par], wsems.at[par, 0]),
                pltpu.make_async_copy(w3_hbm.at[layer, e], w3_st.at[par], wsems.at[par, 1]),
                pltpu.make_async_copy(w2_hbm.at[layer, e], w2_st.at[par], wsems.at[par, 2]))

    def fetch(block, slot):
        rows = pl.ds(pl.multiple_of(block * blk, blk), blk)
        return (pltpu.make_async_copy(xa_hbm.at[rows], xa_buf.at[slot], sems.at[slot, 0]),
                pltpu.make_async_copy(xb_hbm.at[rows], xb_buf.at[slot], sems.at[slot, 1]))

    @pl.when(b == 0)
    def _():
        for i in range(ring - 1):
            for c in fetch(i, i):
                c.start()

    ahead = b + (ring - 1)

    @pl.when(ahead < nb)
    def _():
        for c in fetch(ahead, ahead % ring):
            c.start()

    slot = b % ring
    for c in fetch(b, slot):
        c.wait()

    @pl.when(b == 0)
    def _():
        for c in fetch_weights(blk_e_ref[0], 0):
            c.start()

    @pl.when((b == 0) | (blk_e_ref[b] != prev_e))
    def _():
        par = parity_ref[b]
        nxt = next_e_ref[b]

        @pl.when(nxt < n_experts)
        def _():
            for c in fetch_weights(nxt, 1 - par):
                c.start()

        for c in fetch_weights(blk_e_ref[b], par):
            c.wait()
        w1_sc[...] = w1_st[par].astype(BF16)
        w3_sc[...] = w3_st[par].astype(BF16)
        w2_sc[...] = w2_st[par].astype(BF16)

    @pl.when(nv > 0)
    def _():
        x = _unpack_row_halves(xa_buf[slot], xb_buf[slot])
        rows = lax.broadcasted_iota(jnp.int32, x.shape, 0)
        x = jnp.where(rows < nv, x, 0.0).astype(BF16)
        hid = _silu(jnp.dot(x, w1_sc[...], preferred_element_type=F32)) * jnp.dot(
            x, w3_sc[...], preferred_element_type=F32)
        y = jnp.dot(hid.astype(BF16), w2_sc[...], preferred_element_type=F32)
        ya_ref[...], yb_ref[...] = _pack_row_halves(y)

    @pl.when(nv == 0)
    def _():
        ya_ref[...] = jnp.zeros(ya_ref.shape, ya_ref.dtype)
        yb_ref[...] = jnp.zeros(yb_ref.shape, yb_ref.dtype)


def routed_experts(xa, xb, tables, w1, w3, w2, layer):
    P = xa.shape[0]
    blk = SLOT_BLOCK
    _, E, D, FF = w1.shape
    slots = lambda: pl.BlockSpec((blk, PACK_W), lambda b, *_: (b, 0))
    grid_spec = pltpu.PrefetchScalarGridSpec(
        num_scalar_prefetch=len(tables),
        grid=(P // blk,),
        in_specs=[pl.BlockSpec(memory_space=pl.ANY)] * 5,
        out_specs=[slots(), slots()],
        scratch_shapes=[
            pltpu.VMEM((D, FF), BF16), pltpu.VMEM((D, FF), BF16), pltpu.VMEM((FF, D), BF16),
            pltpu.VMEM((EXPERT_INPUT_SLOTS, blk, PACK_W), jnp.int32),
            pltpu.VMEM((EXPERT_INPUT_SLOTS, blk, PACK_W), jnp.int32),
            pltpu.SemaphoreType.DMA((EXPERT_INPUT_SLOTS, 2)),
            pltpu.VMEM((2, D, FF), F32), pltpu.VMEM((2, D, FF), F32), pltpu.VMEM((2, FF, D), F32),
            pltpu.SemaphoreType.DMA((2, 3)),
        ],
    )
    return pl.pallas_call(
        functools.partial(_expert_kernel, layer=layer),
        grid_spec=grid_spec,
        out_shape=[jax.ShapeDtypeStruct((P, PACK_W), jnp.int32)] * 2,
        compiler_params=_cp(("arbitrary",), VMEM_LIMIT),
        name="routed_experts",
    )(*tables, xa, xb, w1, w3, w2)


def _combine_kernel(xmid_ref, ra_ref, rb_ref, mod2_ref, fg_ref, *rest):
    out_ref = rest[-1]
    D = xmid_ref.shape[1]
    x = xmid_ref[...] + mod2_ref[0][:, 2 * D:] * _unpack_row_halves(ra_ref[...], rb_ref[...])
    out_ref[...] = x * lax.rsqrt(jnp.mean(x * x, axis=-1, keepdims=True) + EPS) * fg_ref[...]


def combine(xmid, ra, rb, mod2, final_g, seq, out_rows=None, row0=0, out_buf=None):
    N, D = xmid.shape
    tm = 512
    tpb = seq // tm
    tile0 = row0 // tm
    in_specs = [
        pl.BlockSpec((tm, D), lambda i: (i, 0)),
        pl.BlockSpec((tm, PACK_W), lambda i: (i, 0)),
        pl.BlockSpec((tm, PACK_W), lambda i: (i, 0)),
        pl.BlockSpec((1, 1, 3 * D), lambda i: (i // tpb, 0, 0)),
        pl.BlockSpec((1, D), lambda i: (0, 0)),
    ]
    args = [xmid, ra, rb, mod2, final_g.reshape(1, D)]
    aliases = {}
    if out_buf is not None:
        in_specs.append(pl.BlockSpec(memory_space=pl.ANY))
        args.append(out_buf)
        aliases = {len(args) - 1: 0}
    return pl.pallas_call(
        _combine_kernel,
        grid=(N // tm,),
        in_specs=in_specs,
        out_specs=pl.BlockSpec((tm, D), lambda i: (i + tile0, 0)),
        out_shape=jax.ShapeDtypeStruct((out_rows or N, D), F32),
        input_output_aliases=aliases,
        compiler_params=_cp(("parallel",), VMEM_LIMIT),
        name="combine",
    )(*args)


TOKEN_STREAMS = 2


def _permute_w_in(w):
    ub = w[:, 3 * DA:3 * DA + DB]
    lat_lo = 3 * DA + DB
    lat_hi = lat_lo + Q_LORA + KV_LORA + QK_ROPE
    lat, gates = w[:, lat_lo:lat_hi], w[:, lat_hi:]
    pad = jnp.zeros((w.shape[0], LAT_W - (lat_hi - lat_lo)), w.dtype)
    parts = [gates, ub, lat, pad]
    for g in range(len(DIL_GROUPS)):
        sl = slice(g * GROUP_W, (g + 1) * GROUP_W)
        parts += [w[:, :DA][:, sl] * (HEAD_DIM_A ** -0.5), w[:, DA:2 * DA][:, sl], w[:, 2 * DA:3 * DA][:, sl]]
    return jnp.concatenate(parts, axis=1).astype(BF16)


def kernel(x, c, positions, ada_mix_w, ada_mix_b, norm_mix_g, w_in, pool_w, pool_scale, cq_norm_g, ckv_norm_g, w_uq, w_ukv, w_oa, w_ob, w_oc, w_out, ada_ffn_w, ada_ffn_b, norm_ffn_g, router_w, router_bias, exp_w1, exp_w3, exp_w2, sh_w1, sh_w3, sh_w2, final_g):
    B, S, D = x.shape
    depth = w_in.shape[0]
    mod_mix = adaln_rows(c, ada_mix_w, ada_mix_b)
    mod_ffn = adaln_rows(c, ada_ffn_w, ada_ffn_b)
    streams = TOKEN_STREAMS if B % TOKEN_STREAMS == 0 else 1
    Bs = B // streams
    Ns = Bs * S
    x_all = x.reshape(B * S, D)
    xs = [None] * streams
    out_all = None
    pos_s = [positions[s * Bs:(s + 1) * Bs] for s in range(streams)]
    for l in range(depth):
        last = l == depth - 1
        w_in_l = _permute_w_in(w_in[l])
        mla_w = _mla_weights(cq_norm_g[l], ckv_norm_g[l], w_uq[l], w_ukv[l])
        mix_w = (norm_ffn_g[l], pool_w[l].astype(BF16), pool_scale[l],
                 w_oa[l].astype(BF16), w_ob[l].astype(BF16), w_oc[l].astype(BF16), w_out[l].astype(BF16),
                 router_w[l].T.astype(BF16), sh_w1[l].astype(BF16), sh_w3[l].astype(BF16), sh_w2[l].astype(BF16))
        for s in range(streams):
            mod1 = mod_mix[l, s * Bs:(s + 1) * Bs].reshape(Bs, 1, 3 * D)
            mod2 = mod_ffn[l, s * Bs:(s + 1) * Bs].reshape(Bs, 1, 3 * D)
            if l == 0:
                x2, row0 = x_all, s * Ns
                gu, lat, *qkv = in_projection(x2, norm_mix_g[l], mod1, w_in_l, S, row0)
            else:
                row0 = 0
                x2, gu, lat, *qkv = in_projection(xs[s][0], norm_mix_g[l], mod1, w_in_l, S, 0, xs[s][1:])
            dil = [dilated_attention(qkv[2 * g], qkv[2 * g + 1]) for g in range(len(DIL_GROUPS))]
            q_all, k_all, vt_all = mla_prep(lat, pos_s[s], *mla_w, Bs, S)
            yc = mla_attention(q_all, k_all, vt_all, Bs, S)
            xmid, h2a, h2b, logits_t = mix_out(x2, gu, dil, yc, mod1, mod2, *mix_w, S, row0)
            dest, w_k, counts = route(logits_t, router_bias[l])
            tables = block_tables(counts, Ns)
            n_slots = tables[0].shape[0] * SLOT_BLOCK
            xa = sc_scatter_rows(h2a, dest, n_slots)
            xb = sc_scatter_rows(h2b, dest, n_slots)
            ya, yb = routed_experts(xa, xb, tables, exp_w1, exp_w3, exp_w2, l)
            ra = sc_weighted_gather(ya, dest, w_k)
            rb = sc_weighted_gather(yb, dest, w_k)
            if last:
                out_all = combine(xmid, ra, rb, mod2, final_g, S, B * S, s * Ns, out_all)
            else:
                xs[s] = (xmid, ra, rb, mod2)
    return out_all.reshape(B, S, D)
```

```python
import functools
import math

import jax
import jax.numpy as jnp
from jax import lax
from jax.experimental import pallas as pl
from jax.experimental.pallas import tpu as pltpu
from jax.experimental.pallas import tpu_sc as plsc

F32 = jnp.float32
BF16 = jnp.bfloat16
HIGHEST = lax.Precision.HIGHEST

D_MODEL = 1024
HEAD_DIM_A = 64
HEADS_PER_GROUP_A = 4
DIL_GROUPS = ((128, 1), (512, 4), (2048, 16))
GROUP_W = HEADS_PER_GROUP_A * HEAD_DIM_A
DA = GROUP_W * len(DIL_GROUPS)
POOL_WINDOWS = (2, 4, 8, 16)
POOL_GROUP_DIM = 128
DB = POOL_GROUP_DIM * len(POOL_WINDOWS)
POOL_HALO = 16
N_HEADS_C = 8
QK_NOPE = 64
QK_ROPE = 32
V_DIM = 64
Q_LORA = 384
KV_LORA = 256
DC = N_HEADS_C * V_DIM
HEAD_PAD_C = 128
ROPE_THETA = 10000.0
N_EXPERTS = 64
TOP_K = 8
N_GROUPS = 8
TOPK_GROUPS = 4
GROUP_SIZE = N_EXPERTS // N_GROUPS
ROUTED_SCALE = 2.5
EPS = 1e-6
NEG = -1e30
Q_BLOCK = 128

LAT_W = 768
GU_W = 3 * D_MODEL + DB
IN_OUT_WIDTHS = (GU_W, LAT_W) + (2 * GROUP_W, GROUP_W) * len(DIL_GROUPS)

VMEM_LIMIT = 56 * 1024 * 1024


def _cp(sem, vmem=None):
    return pltpu.CompilerParams(dimension_semantics=sem, vmem_limit_bytes=vmem)


def _silu(v):
    return v * jax.nn.sigmoid(v)


def _nt_dot(a, b):
    return lax.dot_general(a, b, (((1,), (1,)), ((), ())), preferred_element_type=F32)


PACK_W = D_MODEL // 4
_HI_MASK = -65536


def _bf16_bits(v):
    return lax.bitcast_convert_type(v.astype(BF16).astype(F32), jnp.int32)


def _pack_row_halves(v):
    halves = []
    for h in range(2):
        lo = _bf16_bits(v[:, (2 * h) * PACK_W:(2 * h + 1) * PACK_W])
        hi = _bf16_bits(v[:, (2 * h + 1) * PACK_W:(2 * h + 2) * PACK_W])
        halves.append(lax.shift_right_logical(lo, 16) | (hi & _HI_MASK))
    return halves


def _unpack_row_halves(wa, wb):
    parts = []
    for w in (wa, wb):
        parts.append(lax.bitcast_convert_type(lax.shift_left(w, 16), F32))
        parts.append(lax.bitcast_convert_type(w & _HI_MASK, F32))
    return jnp.concatenate(parts, axis=1)


def _adaln_kernel(c_ref, w_ref, b_ref, o_ref):
    s = _silu(c_ref[...])
    o_ref[0] = jnp.dot(s, w_ref[0], preferred_element_type=F32, precision=HIGHEST) + b_ref[0]


def adaln_rows(c, w, b):
    L, D, D3 = w.shape
    B = c.shape[0]
    tn = 1024
    return pl.pallas_call(
        _adaln_kernel,
        grid=(L, D3 // tn),
        in_specs=[
            pl.BlockSpec((B, D), lambda l, j: (0, 0)),
            pl.BlockSpec((1, D, tn), lambda l, j: (l, 0, j)),
            pl.BlockSpec((1, 1, tn), lambda l, j: (l, 0, j)),
        ],
        out_specs=pl.BlockSpec((1, B, tn), lambda l, j: (l, 0, j)),
        out_shape=jax.ShapeDtypeStruct((L, B, D3), F32),
        compiler_params=_cp(("parallel", "parallel")),
        name="adaln_rows",
    )(c, w, b.reshape(L, 1, D3))


LANES = 128


def _inproj_kernel(x_ref, g_ref, mod_ref, w_ref, *refs, chunk, pending):
    o_refs, scr = refs[:-1], refs[-1]
    D = x_ref.shape[1]
    x = x_ref[...]
    if pending:
        ra_ref, rb_ref, gate_ref, x_out_ref, *o_refs = o_refs
        x = x + gate_ref[0][:, 2 * D:] * _unpack_row_halves(ra_ref[...], rb_ref[...])
        x_out_ref[...] = x
    y = x * lax.rsqrt(jnp.mean(x * x, axis=-1, keepdims=True) + EPS) * g_ref[...]
    mod = mod_ref[0]
    h = (y * (1.0 + mod[:, D:2 * D]) + mod[:, :D]).astype(BF16)
    col = 0
    for o_ref in o_refs:
        width = o_ref.shape[-1]
        if o_ref.ndim == 2:
            for c0 in range(0, width, chunk):
                cw = min(chunk, width - c0)
                o_ref[:, c0:c0 + cw] = jnp.dot(
                    h, w_ref[:, col + c0:col + c0 + cw], preferred_element_type=F32).astype(o_ref.dtype)
        else:
            dil, rows = o_ref.shape[1], o_ref.shape[2]
            z = jnp.dot(h, w_ref[:, col:col + width], preferred_element_type=F32)
            if dil == 1:
                o_ref[0, 0] = z.astype(o_ref.dtype)
            else:
                for c in range(width // LANES):
                    scr[c] = z[:, c * LANES:(c + 1) * LANES]
                for r in range(dil):
                    o_ref[0, r] = jnp.concatenate(
                        [scr[c, pl.ds(r, rows, stride=dil), :] for c in range(width // LANES)],
                        axis=1).astype(o_ref.dtype)
        col += width


def in_projection(x2, g, mod, w, seq, row0=0, pending=None):
    D = x2.shape[1]
    B = mod.shape[0]
    N = B * seq
    tm = 512
    tpb = seq // tm
    tile0 = row0 // tm
    out_specs = [pl.BlockSpec((tm, wd), lambda i: (i, 0)) for wd in IN_OUT_WIDTHS[:2]]
    out_shape = [jax.ShapeDtypeStruct((N, wd), BF16) for wd in IN_OUT_WIDTHS[:2]]
    for grp, (_, dil) in enumerate(DIL_GROUPS):
        for wd in IN_OUT_WIDTHS[2 + 2 * grp:4 + 2 * grp]:
            out_specs.append(pl.BlockSpec((1, dil, tm // dil, wd), lambda i: (i // tpb, 0, i % tpb, 0)))
            out_shape.append(jax.ShapeDtypeStruct((B, dil, seq // dil, wd), BF16))
    in_specs = [
        pl.BlockSpec((tm, D), lambda i: (i + tile0, 0)),
        pl.BlockSpec((1, D), lambda i: (0, 0)),
        pl.BlockSpec((1, 1, 3 * D), lambda i: (i // tpb, 0, 0)),
        pl.BlockSpec(w.shape, lambda i: (0, 0), pipeline_mode=pl.Buffered(1)),
    ]
    args = [x2, g.reshape(1, D), mod, w]
    if pending is not None:
        in_specs += [pl.BlockSpec((tm, PACK_W), lambda i: (i, 0)), pl.BlockSpec((tm, PACK_W), lambda i: (i, 0)),
                     pl.BlockSpec((1, 1, 3 * D), lambda i: (i // tpb, 0, 0))]
        args += list(pending)
        out_specs.insert(0, pl.BlockSpec((tm, D), lambda i: (i, 0)))
        out_shape.insert(0, jax.ShapeDtypeStruct((N, D), F32))
    return pl.pallas_call(
        functools.partial(_inproj_kernel, chunk=512, pending=pending is not None),
        grid=(N // tm,),
        in_specs=in_specs,
        out_specs=out_specs,
        out_shape=out_shape,
        scratch_shapes=[pltpu.VMEM((max(IN_OUT_WIDTHS[2:]) // LANES, tm, LANES), F32)],
        compiler_params=_cp(("parallel",), VMEM_LIMIT),
        name="in_projection",
    )(*args)


def _dilated_kernel(q_ref, kc_ref, kp_ref, vc_ref, vp_ref, o_ref, lse_ref):
    i = pl.program_id(1)
    T = Q_BLOCK
    key = lax.broadcasted_iota(jnp.int32, (T, T), 0)
    qry = lax.broadcasted_iota(jnp.int32, (T, T), 1)
    valid_c = key <= qry
    near = key >= qry
    seqs, run = q_ref.shape[0], q_ref.shape[1] // T
    heads = [slice(h * HEAD_DIM_A, (h + 1) * HEAD_DIM_A) for h in range(HEADS_PER_GROUP_A)]

    def transposed(v):
        return v.astype(F32).T.astype(BF16)

    vts = {(s, j): transposed(vc_ref[s, j * T:(j + 1) * T, :]) for s in range(seqs) for j in range(run)}
    vt_before = [transposed(vp_ref[s]) for s in range(seqs)]

    def blocks(s, j):
        rows = slice(j * T, (j + 1) * T)
        if j == 0:
            return rows, kc_ref[s, rows, :], vts[s, 0], kp_ref[s], vt_before[s], near & (i > 0)
        before = slice((j - 1) * T, j * T)
        return rows, kc_ref[s, rows, :], vts[s, j], kc_ref[s, before, :], vts[s, j - 1], near

    scores, probs = {}, {}
    for s in range(seqs):
        for j in range(run):
            rows, kc, _, kp, _, valid_p = blocks(s, j)
            q = q_ref[s, rows, :]
            for h, sl in enumerate(heads):
                qh = q[:, sl]
                scores[s, j, h] = (jnp.where(valid_c, _nt_dot(kc[:, sl], qh), NEG),
                                   jnp.where(valid_p, _nt_dot(kp[:, sl], qh), NEG))
    for chain, (sc, sp) in scores.items():
        m = jnp.maximum(jnp.max(sc, axis=0, keepdims=True), jnp.max(sp, axis=0, keepdims=True))
        pc = jnp.exp(sc - m)
        pp = jnp.exp(sp - m)
        den = jnp.sum(pc, axis=0, keepdims=True) + jnp.sum(pp, axis=0, keepdims=True)
        probs[chain] = (pc.astype(BF16), pp.astype(BF16), den, m + jnp.log(den))
    spread = LSE_LANES // len(heads)
    for s in range(seqs):
        for j in range(run):
            rows, _, vtc, _, vtp, _ = blocks(s, j)
            outs = []
            for h, sl in enumerate(heads):
                pc, pp, den, _ = probs[s, j, h]
                o = (jnp.dot(vtc[sl, :], pc, preferred_element_type=F32)
                     + jnp.dot(vtp[sl, :], pp, preferred_element_type=F32))
                outs.append(o / den)
            o_ref[s, rows, :] = jnp.concatenate(outs, axis=0).T.astype(o_ref.dtype)
            lse_t = jnp.concatenate(
                [jnp.broadcast_to(probs[s, j, h][3], (spread, T)) for h in range(len(heads))], axis=0)
            lse_ref[s, rows, :] = lse_t.T


DILATED_RUN = 8


LSE_LANES = 128


def dilated_attention(qk, v):
    batch, dilation, L, _ = qk.shape
    nb = L // Q_BLOCK
    run = min(DILATED_RUN, nb)
    seqs = DILATED_RUN // run
    qk_r = qk.reshape(batch * dilation, L, 2 * GROUP_W)
    v_r = v.reshape(batch * dilation, L, GROUP_W)
    before = lambda i: jnp.maximum(i * run - 1, 0)
    o, lse = pl.pallas_call(
        _dilated_kernel,
        grid=(batch * dilation // seqs, nb // run),
        in_specs=[
            pl.BlockSpec((seqs, run * Q_BLOCK, GROUP_W), lambda s, i: (s, i, 0)),
            pl.BlockSpec((seqs, run * Q_BLOCK, GROUP_W), lambda s, i: (s, i, 1)),
            pl.BlockSpec((seqs, Q_BLOCK, GROUP_W), lambda s, i: (s, before(i), 1)),
            pl.BlockSpec((seqs, run * Q_BLOCK, GROUP_W), lambda s, i: (s, i, 0)),
            pl.BlockSpec((seqs, Q_BLOCK, GROUP_W), lambda s, i: (s, before(i), 0)),
        ],
        out_specs=[
            pl.BlockSpec((seqs, run * Q_BLOCK, GROUP_W), lambda s, i: (s, i, 0)),
            pl.BlockSpec((seqs, run * Q_BLOCK, LSE_LANES), lambda s, i: (s, i, 0)),
        ],
        out_shape=[
            jax.ShapeDtypeStruct((batch * dilation, L, GROUP_W), BF16),
            jax.ShapeDtypeStruct((batch * dilation, L, LSE_LANES), F32),
        ],
        compiler_params=_cp(("parallel", "parallel")),
        name=f"dilated_attention_d{dilation}",
    )(qk_r, qk_r, qk_r, v_r, v_r)
    return o.reshape(batch, dilation, L, GROUP_W), lse.reshape(batch, dilation, L, LSE_LANES)


def _mla_prep_kernel(lat_ref, pos_ref, gq_ref, gkv_ref, wq_ref, wk_ref, wvt_ref, freq_ref, spread_ref, one_ref,
                     q_ref, k_ref, vt_ref):
    HP = N_HEADS_C * HEAD_PAD_C
    lat = lat_ref[...].astype(F32)
    cq = lat[:, :Q_LORA]
    ckr = lat[:, Q_LORA:]
    zq = (cq * lax.rsqrt(jnp.mean(cq * cq, axis=-1, keepdims=True) + EPS) * gq_ref[...]).astype(BF16)
    lane = lax.broadcasted_iota(jnp.int32, ckr.shape, 1)
    is_kv = lane < KV_LORA
    ms = jnp.sum(jnp.where(is_kv, ckr * ckr, 0.0), axis=-1, keepdims=True) * (1.0 / KV_LORA)
    zkv = (ckr * jnp.where(is_kv, lax.rsqrt(ms + EPS) * gkv_ref[...], 1.0)).astype(BF16)
    qq = jnp.dot(zq, wq_ref[...], preferred_element_type=F32)
    kk = jnp.dot(zkv, wk_ref[:, :HP], preferred_element_type=F32)
    kk_sw = jnp.dot(zkv[:, KV_LORA:], wk_ref[KV_LORA:, HP:], preferred_element_type=F32)
    ang_t = freq_ref[...] * pos_ref[0].astype(F32)

    def to_lanes(t):
        hi = t.astype(BF16)
        lo = (t - hi.astype(F32)).astype(BF16)
        tn_dot = lambda a: lax.dot_general(a, spread_ref[...], (((0,), (0,)), ((), ())), preferred_element_type=F32)
        return tn_dot(hi) + tn_dot(lo)

    cos = to_lanes(jnp.cos(ang_t)) + one_ref[...]
    sin = to_lanes(jnp.sin(ang_t))
    for h in range(N_HEADS_C):
        lo, hi = h * HEAD_PAD_C, (h + 1) * HEAD_PAD_C
        q_ref[:, lo:hi] = (qq[:, lo:hi] * cos + qq[:, HP + lo:HP + hi] * sin).astype(q_ref.dtype)
        k_ref[:, lo:hi] = (kk[:, lo:hi] * cos + kk_sw[:, lo:hi] * sin).astype(k_ref.dtype)
    vt_ref[0] = _nt_dot(wvt_ref[...], zkv).astype(vt_ref.dtype)


def _mla_weights(cq_g, ckv_g, w_uq, w_ukv):
    H, HPAD, half = N_HEADS_C, HEAD_PAD_C, QK_ROPE // 2
    scale = (QK_NOPE + QK_ROPE) ** -0.5 * math.log2(math.e)
    wq = w_uq.reshape(Q_LORA, H, QK_NOPE + QK_ROPE) * scale
    q_lin = jnp.pad(wq, ((0, 0), (0, 0), (0, HPAD - QK_NOPE - QK_ROPE)))
    r1, r2 = wq[..., QK_NOPE:QK_NOPE + half], wq[..., QK_NOPE + half:]
    q_sw = jnp.concatenate([jnp.zeros((Q_LORA, H, QK_NOPE), F32), -r2, r1,
                            jnp.zeros((Q_LORA, H, HPAD - QK_NOPE - QK_ROPE), F32)], axis=-1)
    wq_big = jnp.concatenate([q_lin.reshape(Q_LORA, H * HPAD), q_sw.reshape(Q_LORA, H * HPAD)], axis=1)

    rows = LAT_W - Q_LORA
    wkv = w_ukv.reshape(KV_LORA, H, QK_NOPE + V_DIM)
    eye = jnp.eye(QK_ROPE, dtype=F32)
    k_lin = jnp.zeros((rows, H, HPAD), F32)
    k_lin = k_lin.at[:KV_LORA, :, :QK_NOPE].set(wkv[..., :QK_NOPE])
    k_lin = k_lin.at[KV_LORA:KV_LORA + QK_ROPE, :, QK_NOPE:QK_NOPE + QK_ROPE].set(
        jnp.broadcast_to(eye[:, None, :], (QK_ROPE, H, QK_ROPE)))
    swap = jnp.zeros((QK_ROPE, QK_ROPE), F32).at[half:, :half].set(-jnp.eye(half)).at[:half, half:].set(jnp.eye(half))
    k_sw = jnp.zeros((rows, H, HPAD), F32)
    k_sw = k_sw.at[KV_LORA:KV_LORA + QK_ROPE, :, QK_NOPE:QK_NOPE + QK_ROPE].set(
        jnp.broadcast_to(swap[:, None, :], (QK_ROPE, H, QK_ROPE)))
    v_w = jnp.zeros((rows, H, V_DIM), F32).at[:KV_LORA].set(wkv[..., QK_NOPE:])
    wk_big = jnp.concatenate([k_lin.reshape(rows, H * HPAD), k_sw.reshape(rows, H * HPAD)], axis=1)
    wv_t = v_w.reshape(rows, H * V_DIM).T

    gkv = jnp.concatenate([ckv_g, jnp.ones((rows - KV_LORA,), F32)]).reshape(1, rows)
    return cq_g.reshape(1, Q_LORA), gkv, wq_big.astype(BF16), wk_big.astype(BF16), wv_t.astype(BF16)


def _rope_tables():
    half = QK_ROPE // 2
    freqs = (ROPE_THETA ** (-jnp.arange(0, QK_ROPE, 2, dtype=F32) / QK_ROPE)).reshape(half, 1)
    lane = jnp.arange(HEAD_PAD_C)[None, :]
    j = jnp.arange(half)[:, None]
    spread = (lane == QK_NOPE + j) | (lane == QK_NOPE + half + j)
    off_rope = ~jnp.any(spread, axis=0, keepdims=True)
    return freqs, spread.astype(BF16), off_rope.astype(F32)


def mla_prep(lat, positions, gq, gkv, wq_big, wk_big, wv_t, batch, seq):
    N = lat.shape[0]
    HP = N_HEADS_C * HEAD_PAD_C
    tm = 512
    tpb = seq // tm
    freqs, spread, off_rope = _rope_tables()
    pos_rows = positions.reshape(N // tm, 1, tm)
    const = lambda shape: pl.BlockSpec(shape, lambda i: (0, 0))
    return pl.pallas_call(
        _mla_prep_kernel,
        grid=(N // tm,),
        in_specs=[
            pl.BlockSpec((tm, LAT_W), lambda i: (i, 0)),
            pl.BlockSpec((1, 1, tm), lambda i: (i, 0, 0)),
            const(gq.shape), const(gkv.shape), const(wq_big.shape), const(wk_big.shape), const(wv_t.shape),
            const(freqs.shape), const(spread.shape), const(off_rope.shape),
        ],
        out_specs=[
            pl.BlockSpec((tm, HP), lambda i: (i, 0)),
            pl.BlockSpec((tm, HP), lambda i: (i, 0)),
            pl.BlockSpec((1, DC, tm), lambda i: (i // tpb, 0, i % tpb)),
        ],
        out_shape=[
            jax.ShapeDtypeStruct((N, HP), BF16),
            jax.ShapeDtypeStruct((N, HP), BF16),
            jax.ShapeDtypeStruct((batch, DC, seq), BF16),
        ],
        compiler_params=_cp(("parallel",), VMEM_LIMIT),
        name="mla_prep",
    )(lat, pos_rows, gq, gkv, wq_big, wk_big, wv_t, freqs, spread, off_rope)


HEADS_PER_STEP_C = 8
FLASH_Q_CHUNK = 256


def _mla_flash_kernel(qi_ref, ki_ref, q_ref, k_ref, vt_ref, o_ref, m_sc, l_sc, acc_sc):
    t = pl.program_id(2)
    qi, ki = qi_ref[t], ki_ref[t]

    @pl.when(ki == 0)
    def _():
        m_sc[...] = jnp.full(m_sc.shape, NEG, F32)
        l_sc[...] = jnp.zeros(l_sc.shape, F32)
        acc_sc[...] = jnp.zeros(acc_sc.shape, F32)

    def step(masked):
        T = q_ref.shape[1]
        if masked:
            key = lax.broadcasted_iota(jnp.int32, (T, T), 0)
            qry = lax.broadcasted_iota(jnp.int32, (T, T), 1)
            keep = key <= qry
        chains = [(h, c) for h in range(HEADS_PER_STEP_C) for c in range(T // FLASH_Q_CHUNK)]
        scores, probs, alphas = {}, {}, {}

        def keys_for(c):
            return (c + 1) * FLASH_Q_CHUNK if masked else T

        def qk(h, c):
            qs = slice(c * FLASH_Q_CHUNK, (c + 1) * FLASH_Q_CHUNK)
            q = q_ref[0, qs, h * HEAD_PAD_C:(h + 1) * HEAD_PAD_C]
            k = k_ref[0, :keys_for(c), h * HEAD_PAD_C:(h + 1) * HEAD_PAD_C]
            st = _nt_dot(k, q)
            scores[h, c] = jnp.where(keep[:keys_for(c), qs], st, NEG) if masked else st

        def softmax(h, c):
            qs = slice(c * FLASH_Q_CHUNK, (c + 1) * FLASH_Q_CHUNK)
            st = scores.pop((h, c))
            m_prev = m_sc[h, :, qs]
            m_new = jnp.maximum(m_prev, jnp.max(st, axis=0, keepdims=True))
            alpha = jnp.exp2(m_prev - m_new)
            p = jnp.exp2(st - m_new)
            l_sc[h, :, qs] = alpha * l_sc[h, :, qs] + jnp.sum(p, axis=0, keepdims=True)
            m_sc[h, :, qs] = m_new
            probs[h, c], alphas[h, c] = p.astype(BF16), alpha

        def pv(h, c):
            qs = slice(c * FLASH_Q_CHUNK, (c + 1) * FLASH_Q_CHUNK)
            vt = vt_ref[0, h * V_DIM:(h + 1) * V_DIM, :keys_for(c)]
            acc_sc[h, :, qs] = alphas.pop((h, c)) * acc_sc[h, :, qs] + jnp.dot(
                vt, probs.pop((h, c)), preferred_element_type=F32)

        for phase in (qk, softmax, pv):
            for ch in chains:
                phase(*ch)

    @pl.when(ki < qi)
    def _():
        step(False)

    @pl.when(ki == qi)
    def _():
        step(True)
        ot = jnp.concatenate([acc_sc[h] / l_sc[h] for h in range(HEADS_PER_STEP_C)], axis=0)
        o_ref[0] = ot.T.astype(o_ref.dtype)


def mla_attention(q_all, k_all, vt_all, batch, seq):
    T = 512
    nq = seq // T
    pairs = [(a, b) for a in range(nq) for b in range(a + 1)]
    qi_tab = jnp.asarray([p[0] for p in pairs], jnp.int32)
    ki_tab = jnp.asarray([p[1] for p in pairs], jnp.int32)
    hp = N_HEADS_C // HEADS_PER_STEP_C
    qw = HEADS_PER_STEP_C * HEAD_PAD_C
    vw = HEADS_PER_STEP_C * V_DIM
    q3 = q_all.reshape(batch, seq, -1)
    k3 = k_all.reshape(batch, seq, -1)
    grid_spec = pltpu.PrefetchScalarGridSpec(
        num_scalar_prefetch=2,
        grid=(batch, hp, len(pairs)),
        in_specs=[
            pl.BlockSpec((1, T, qw), lambda b, h, t, qi, ki: (b, qi[t], h)),
            pl.BlockSpec((1, T, qw), lambda b, h, t, qi, ki: (b, ki[t], h)),
            pl.BlockSpec((1, vw, T), lambda b, h, t, qi, ki: (b, h, ki[t])),
        ],
        out_specs=pl.BlockSpec((1, T, vw), lambda b, h, t, qi, ki: (b, qi[t], h)),
        scratch_shapes=[
            pltpu.VMEM((HEADS_PER_STEP_C, 1, T), F32),
            pltpu.VMEM((HEADS_PER_STEP_C, 1, T), F32),
            pltpu.VMEM((HEADS_PER_STEP_C, V_DIM, T), F32),
        ],
    )
    o = pl.pallas_call(
        _mla_flash_kernel,
        grid_spec=grid_spec,
        out_shape=jax.ShapeDtypeStruct((batch, seq, DC), BF16),
        compiler_params=_cp(("parallel", "parallel", "arbitrary")),
        name="mla_attention",
    )(qi_tab, ki_tab, q3, k3, vt_all)
    return o.reshape(batch * seq, DC)


MIX_CHUNK = 256


def _mixout_kernel(x_ref, gates_ref, ub_ref, ubh_ref, o1_ref, o2_ref, o3_ref, l1_ref, l2_ref, l3_ref, yc_ref,
                   mod1_ref, mod2_ref, g2_ref, poolw_ref, pscale_ref, woa_ref, wob_ref, woc_ref, wout_ref,
                   rwt_ref, sw1_ref, sw3_ref, sw2_ref, spread_ref,
                   xmid_ref, h2a_ref, h2b_ref, logit_ref, *scratch, tiles_per_batch):
    D = x_ref.shape[1]
    tm = x_ref.shape[0]
    tile = pl.program_id(0) % tiles_per_batch
    o_scrs, l_scrs = scratch[:3], scratch[3:]

    def token_order(ref, scr):
        dil, rows, width = ref.shape[1:]
        if dil == 1:
            return ref[0, 0].astype(F32)
        for r in range(dil):
            v = ref[0, r].astype(F32)
            for c in range(width // LANES):
                scr[c, pl.ds(r, rows, stride=dil), :] = v[:, c * LANES:(c + 1) * LANES]
        return jnp.concatenate([scr[c] for c in range(width // LANES)], axis=1)

    outs = [token_order(r, s) for r, s in zip((o1_ref, o2_ref, o3_ref), o_scrs)]
    l1, l2, l3 = [token_order(r, s) for r, s in zip((l1_ref, l2_ref, l3_ref), l_scrs)]
    mx = jnp.maximum(jnp.maximum(l1, l2), l3)
    es = [jnp.exp(l1 - mx), jnp.exp(l2 - mx), jnp.exp(l3 - mx)]
    inv = 1.0 / (es[0] + es[1] + es[2])
    ya = jnp.zeros((tm, GROUP_W), F32)
    for e, o in zip(es, outs):
        w = e * inv
        w_hi = w.astype(BF16)
        w_lo = (w - w_hi.astype(F32)).astype(BF16)
        w_wide = (jnp.dot(w_hi, spread_ref[...], preferred_element_type=F32)
                  + jnp.dot(w_lo, spread_ref[...], preferred_element_type=F32))
        ya = ya + w_wide * o
    ya_b = ya.astype(BF16)

    u = ub_ref[...].astype(F32)
    halo = jnp.where(tile > 0, ubh_ref[...].astype(F32), 0.0)
    ext = jnp.concatenate([halo, u], axis=0)
    t_seq = tile * tm + lax.broadcasted_iota(jnp.int32, (tm, 1), 0)
    pooled = []
    for gi, w in enumerate(POOL_WINDOWS):
        sl = slice(gi * POOL_GROUP_DIM, (gi + 1) * POOL_GROUP_DIM)
        acc = ext[:, sl]
        k = 1
        while k < w:
            acc = acc + pltpu.roll(acc, k, axis=0)
            k *= 2
        cnt = jnp.minimum(t_seq + 1, w).astype(F32)
        pg = acc[POOL_HALO:] / cnt - u[:, sl]
        pooled.append(jnp.dot(pg.astype(BF16), poolw_ref[gi], preferred_element_type=F32))
    yb = jnp.concatenate(pooled, axis=1) * pscale_ref[...]
    yb_b = yb.astype(BF16)
    yc_b = yc_ref[...]

    mix_chunks = []
    for c0 in range(0, D, MIX_CHUNK):
        cs = slice(c0, c0 + MIX_CHUNK)
        gate = lambda j: jax.nn.sigmoid(gates_ref[:, j * D + c0:j * D + c0 + MIX_CHUNK].astype(F32))
        mix_c = (gate(0) * jnp.dot(ya_b, woa_ref[:, cs], preferred_element_type=F32)
                 + gate(1) * jnp.dot(yb_b, wob_ref[:, cs], preferred_element_type=F32)
                 + gate(2) * jnp.dot(yc_b, woc_ref[:, cs], preferred_element_type=F32))
        mix_chunks.append(mix_c.astype(BF16))
    tok = jnp.dot(jnp.concatenate(mix_chunks, axis=1), wout_ref[...], preferred_element_type=F32)
    xn = x_ref[...] + mod1_ref[0][:, 2 * D:] * tok

    mod2 = mod2_ref[0]
    y = xn * lax.rsqrt(jnp.mean(xn * xn, axis=-1, keepdims=True) + EPS) * g2_ref[...]
    h2 = y * (1.0 + mod2[:, D:2 * D]) + mod2[:, :D]
    h2b = h2.astype(BF16)
    h2a_ref[...], h2b_ref[...] = _pack_row_halves(h2b)
    logit_ref[...] = _nt_dot(rwt_ref[...], h2b)
    hid = _silu(jnp.dot(h2b, sw1_ref[...], preferred_element_type=F32)) * jnp.dot(
        h2b, sw3_ref[...], preferred_element_type=F32)
    shared = jnp.dot(hid.astype(BF16), sw2_ref[...], preferred_element_type=F32)
    xmid_ref[...] = xn + mod2[:, 2 * D:] * shared


def mix_out(x2, gu, dil, yc, mod1, mod2, g2, pool_w, pool_scale, w_oa, w_ob, w_oc, w_out, rwt, sw1, sw3, sw2, seq,
            row0=0):
    D = x2.shape[1]
    N = gu.shape[0]
    tm = 512
    tpb = seq // tm
    tile0 = row0 // tm
    (o1, l1), (o2, l2), (o3, l3) = dil
    row = lambda w, c=0: pl.BlockSpec((tm, w), lambda i: (i, c))
    by_residue = lambda a: pl.BlockSpec(
        (1, a.shape[1], tm // a.shape[1], a.shape[3]), lambda i: (i // tpb, 0, i % tpb, 0))
    heads = HEADS_PER_GROUP_A
    spread = (jnp.arange(LSE_LANES)[:, None] == (jnp.arange(GROUP_W)[None, :] // HEAD_DIM_A) * (LSE_LANES // heads)
              ).astype(BF16)
    const2 = lambda a: pl.BlockSpec(a.shape, lambda i: (0,) * a.ndim, pipeline_mode=pl.Buffered(1))
    modspec = pl.BlockSpec((1, 1, 3 * D), lambda i: (i // tpb, 0, 0))
    ub_col = 3 * D // DB
    halo_spec = pl.BlockSpec(
        (POOL_HALO, DB), lambda i: (jnp.maximum(i * (tm // POOL_HALO) - 1, 0), ub_col))
    weights = [g2.reshape(1, D), pool_w, pool_scale.reshape(1, DB), w_oa, w_ob, w_oc, w_out, rwt, sw1, sw3, sw2,
               spread]
    return pl.pallas_call(
        functools.partial(_mixout_kernel, tiles_per_batch=tpb),
        grid=(N // tm,),
        in_specs=[
            pl.BlockSpec((tm, D), lambda i: (i + tile0, 0)), row(3 * D), row(DB, ub_col), halo_spec,
            by_residue(o1), by_residue(o2), by_residue(o3), by_residue(l1), by_residue(l2), by_residue(l3), row(DC),
            modspec, modspec,
        ] + [const2(a) for a in weights],
        scratch_shapes=[pltpu.VMEM((GROUP_W // LANES, tm, LANES), F32)] * 3
        + [pltpu.VMEM((LSE_LANES // LANES, tm, LANES), F32)] * 3,
        out_specs=[row(D), row(PACK_W), row(PACK_W), pl.BlockSpec((N_EXPERTS, tm), lambda i: (0, i))],
        out_shape=[
            jax.ShapeDtypeStruct((N, D), F32),
            jax.ShapeDtypeStruct((N, PACK_W), jnp.int32),
            jax.ShapeDtypeStruct((N, PACK_W), jnp.int32),
            jax.ShapeDtypeStruct((N_EXPERTS, N), F32),
        ],
        compiler_params=_cp(("parallel",), VMEM_LIMIT),
        name="mix_out",
    )(x2, gu, gu, gu, o1, o2, o3, l1, l2, l3, yc, mod1, mod2, *weights)


def _pick_rows(table, picks):
    G, GS = N_GROUPS, GROUP_SIZE
    eio = lax.broadcasted_iota(jnp.int32, (GS, table.shape[1]), 0)
    rows = []
    for k in range(TOP_K):
        idx = picks[k:k + 1]
        parts = [jnp.where(eio + g * GS == idx, table[g * GS:(g + 1) * GS], 0.0) for g in range(G)]
        rows.append(jnp.sum(functools.reduce(jnp.add, parts), axis=0, keepdims=True))
    return jnp.concatenate(rows, axis=0)


def _route_choose(lg_ref, bias_ref):
    G, GS = N_GROUPS, GROUP_SIZE
    scores = jax.nn.sigmoid(lg_ref[...])
    sel = scores + bias_ref[...]
    tn = sel.shape[1]
    eio = lax.broadcasted_iota(jnp.int32, (GS, tn), 0)
    ninf = -jnp.inf

    gs = []
    for g in range(G):
        v = sel[g * GS:(g + 1) * GS]
        m1 = jnp.max(v, axis=0, keepdims=True)
        i1 = jnp.min(jnp.where(v == m1, eio, GS), axis=0, keepdims=True)
        m2 = jnp.max(jnp.where(eio == i1, ninf, v), axis=0, keepdims=True)
        gs.append(m1 + m2)
    gsm = jnp.concatenate(gs, axis=0)
    gio = lax.broadcasted_iota(jnp.int32, (G, tn), 0)
    rank = jnp.zeros((G, tn), jnp.int32)
    for g2 in range(G):
        beats = (gs[g2] > gsm) | ((gs[g2] == gsm) & (g2 < gio))
        rank = rank + beats.astype(jnp.int32)
    gsel = rank < TOPK_GROUPS

    vs = [jnp.where(gsel[g:g + 1], sel[g * GS:(g + 1) * GS], NEG) for g in range(G)]
    eid = [eio + g * GS for g in range(G)]
    chosen = [jnp.zeros((GS, tn), jnp.bool_) for _ in range(G)]
    picks = []
    for _ in range(TOP_K):
        m = jnp.max(functools.reduce(jnp.maximum, vs), axis=0, keepdims=True)
        idx = jnp.min(functools.reduce(jnp.minimum, [jnp.where(v == m, e, N_EXPERTS) for v, e in zip(vs, eid)]),
                      axis=0, keepdims=True)
        picks.append(idx)
        for g in range(G):
            hit = eid[g] == idx
            chosen[g] = chosen[g] | hit
            vs[g] = jnp.where(hit, ninf, vs[g])
    mask = jnp.concatenate(chosen, axis=0).astype(F32)
    return scores, jnp.concatenate(picks, axis=0), mask


def _route_kernel(lg_ref, bias_ref, tri_ref, dest_ref, w_ref, cnt_ref, run_sc, start_sc, mask_sc, picks_sc,
                  *, slot_block):
    phase = pl.program_id(0)
    step = pl.program_id(1)
    tn = lg_ref.shape[1]
    cols = pl.ds(pl.multiple_of(step * tn, tn), tn)

    @pl.when(phase == 0)
    def _():
        @pl.when(step == 0)
        def _():
            run_sc[...] = jnp.zeros(run_sc.shape, F32)

        scores, picks, mask = _route_choose(lg_ref, bias_ref)
        wk = _pick_rows(scores, picks)
        w_ref[0] = wk / jnp.sum(wk, axis=0, keepdims=True) * ROUTED_SCALE
        dest_ref[0] = jnp.zeros(dest_ref.shape[1:], dest_ref.dtype)
        mask_sc[:, cols] = mask.astype(BF16)
        picks_sc[:, cols] = picks
        run_sc[...] = run_sc[...] + jnp.sum(mask, axis=1, keepdims=True)

    @pl.when(phase == 1)
    def _():
        @pl.when(step == 0)
        def _():
            counts = run_sc[...].astype(jnp.int32)
            cnt_ref[...] = jnp.broadcast_to(counts, cnt_ref.shape)
            shift = slot_block.bit_length() - 1
            padded = lax.shift_left(lax.shift_right_logical(counts + (slot_block - 1), shift), shift).astype(F32)
            r = lax.broadcasted_iota(jnp.int32, (N_EXPERTS, N_EXPERTS), 0)
            c = lax.broadcasted_iota(jnp.int32, (N_EXPERTS, N_EXPERTS), 1)
            as_row = jnp.sum(jnp.where(r == c, padded, 0.0), axis=0, keepdims=True)
            start_sc[...] = jnp.sum(jnp.where(c < r, as_row, 0.0), axis=1, keepdims=True)
            run_sc[...] = jnp.zeros(run_sc.shape, F32)

        mask_b = mask_sc[:, cols]
        mask = mask_b.astype(F32)
        before = jnp.dot(mask_b, tri_ref[...], preferred_element_type=F32) - mask
        slot = start_sc[...] + run_sc[...] + before
        dest_ref[0] = _pick_rows(slot, picks_sc[:, cols]).astype(jnp.int32)
        w_ref[0] = jnp.zeros(w_ref.shape[1:], w_ref.dtype)
        run_sc[...] = run_sc[...] + jnp.sum(mask, axis=1, keepdims=True)


SLOT_BLOCK = 512


def route(logits_t, bias):
    E, N = logits_t.shape
    tn = 1024
    tri = (jnp.arange(tn)[:, None] <= jnp.arange(tn)[None, :]).astype(BF16)
    plane = lambda: pl.BlockSpec((1, TOP_K, tn), lambda p, i: (p, 0, i))
    dest, w, cnt = pl.pallas_call(
        functools.partial(_route_kernel, slot_block=SLOT_BLOCK),
        grid=(2, N // tn),
        in_specs=[
            pl.BlockSpec((E, tn), lambda p, i: (0, i * (1 - p))),
            pl.BlockSpec((E, 1), lambda p, i: (0, 0)),
            pl.BlockSpec((tn, tn), lambda p, i: (0, 0)),
        ],
        out_specs=[plane(), plane(), pl.BlockSpec((E, 128), lambda p, i: (0, 0))],
        out_shape=[
            jax.ShapeDtypeStruct((2, TOP_K, N), jnp.int32),
            jax.ShapeDtypeStruct((2, TOP_K, N), F32),
            jax.ShapeDtypeStruct((E, 128), jnp.int32),
        ],
        scratch_shapes=[pltpu.VMEM((E, 1), F32), pltpu.VMEM((E, 1), F32),
                        pltpu.VMEM((E, N), BF16), pltpu.VMEM((TOP_K, N), jnp.int32)],
        compiler_params=_cp(("arbitrary", "arbitrary")),
        name="route",
    )(logits_t, bias.reshape(E, 1), tri)
    return dest[1], w[0], cnt[:, 0]


def block_tables(counts, n_tokens):
    E = counts.shape[0]
    blk = SLOT_BLOCK
    nblk = (n_tokens * TOP_K + E * blk) // blk
    per_expert = (counts + blk - 1) // blk
    bend = jnp.cumsum(per_expert)
    bstart = bend - per_expert
    b = jnp.arange(nblk, dtype=jnp.int32)[:, None]
    owns = (bstart[None, :] <= b) & (b < bend[None, :])
    blk_e = jnp.minimum(jnp.sum(bend[None, :] <= b, axis=1), E - 1).astype(jnp.int32)
    rows_left = counts[None, :] - (b - bstart[None, :]) * blk
    nvalid = jnp.sum(jnp.where(owns, jnp.clip(rows_left, 0, blk), 0), axis=1)
    first = jnp.concatenate([jnp.ones((1,), jnp.bool_), blk_e[1:] != blk_e[:-1]])
    run_parity = ((jnp.cumsum(first.astype(jnp.int32)) - 1) % 2).astype(jnp.int32)
    later = blk_e[None, :] > blk_e[:, None]
    next_e = jnp.min(jnp.where(later, blk_e[None, :], E), axis=1).astype(jnp.int32)
    return blk_e, nvalid.astype(jnp.int32), run_parity, next_e


def _sc_mesh():
    return plsc.VectorSubcoreMesh(core_axis_name="c", subcore_axis_name="s")


SC_WINDOW = 128


def sc_scatter_rows(x, dest, n_slots):
    N, W = x.shape
    K = dest.shape[0]

    @functools.partial(pl.kernel, out_type=jax.ShapeDtypeStruct((n_slots, W), x.dtype), mesh=_sc_mesh(),
                       scratch_types=[])
    def scatter(x_hbm, i_hbm, o_hbm):
        def body(x_vmem, i_vmem):
            for k in range(K):
                pltpu.sync_copy(x_vmem, o_hbm.at[i_vmem.at[k]])

        pltpu.emit_pipeline(
            body,
            grid=(N // SC_WINDOW,),
            in_specs=[pl.BlockSpec((SC_WINDOW, W), lambda i: (i, 0)),
                      pl.BlockSpec((K, SC_WINDOW), lambda i: (0, i))],
            out_specs=[],
            core_axis_name=("c", "s"),
            dimension_semantics=(pltpu.PARALLEL,),
        )(x_hbm, i_hbm)

    return scatter(x, dest)


SC_LANES = 16
SC_GATHER_TOKENS = 8


def sc_weighted_gather(y, dest, wts):
    W = y.shape[1]
    K, N = dest.shape
    G, L = SC_GATHER_TOKENS, SC_LANES
    batches = SC_WINDOW // G

    @functools.partial(
        pl.kernel, out_type=jax.ShapeDtypeStruct((N, W), y.dtype), mesh=_sc_mesh(),
        scratch_types=[pltpu.VMEM((2, K, G, W), y.dtype), pltpu.SemaphoreType.DMA((2,))],
        compiler_params=pltpu.CompilerParams(needs_layout_passes=False))
    def gather(y_hbm, i_hbm, w_hbm, o_hbm, rows2, sems):
        def body(i_vmem, w_vmem, o_vmem):
            def fetch(batch, slot):
                return [pltpu.make_async_copy(y_hbm.at[i_vmem.at[k, pl.ds(batch * G, G)]], rows2.at[slot, k],
                                              sems.at[slot]) for k in range(K)]

            for c in fetch(0, 0):
                c.start()

            @pl.loop(0, batches)
            def _(batch):
                slot = batch % 2

                @pl.when(batch + 1 < batches)
                def _():
                    for c in fetch(batch + 1, 1 - slot):
                        c.start()

                for c in fetch(batch, slot):
                    c.wait()
                rows = rows2.at[slot]

                @pl.loop(0, G)
                def _(t):
                    tok = jnp.full((L,), batch * G + t, jnp.int32)
                    wk = [plsc.load_gather(w_vmem, [jnp.full((L,), k, jnp.int32), tok]) for k in range(K)]

                    @plsc.parallel_loop(0, W // L, unroll=W // L)
                    def _(j):
                        lo = jnp.zeros((L,), F32)
                        hi = jnp.zeros((L,), F32)
                        for k in range(K):
                            pair = plsc.bitcast(rows[k, t, pl.ds(j * L, L)], BF16)
                            a, b = plsc.unpack(pair, format=plsc.PackFormat.INTERLEAVED)
                            lo = lo + wk[k] * a
                            hi = hi + wk[k] * b
                        o_vmem[batch * G + t, pl.ds(j * L, L)] = plsc.bitcast(
                            plsc.pack(lo, hi, format=plsc.PackFormat.INTERLEAVED), y.dtype)

        pltpu.emit_pipeline(
            body,
            grid=(N // SC_WINDOW,),
            in_specs=[pl.BlockSpec((K, SC_WINDOW), lambda i: (0, i)),
                      pl.BlockSpec((K, SC_WINDOW), lambda i: (0, i))],
            out_specs=[pl.BlockSpec((SC_WINDOW, W), lambda i: (i, 0))],
            core_axis_name=("c", "s"),
            dimension_semantics=(pltpu.PARALLEL,),
        )(i_hbm, w_hbm, o_hbm)

    return gather(y, dest, wts)


EXPERT_INPUT_SLOTS = 3
EXPERT_BLOCKS_PER_STEP = 2


def _expert_kernel(blk_e_ref, nvalid_ref, parity_ref, next_e_ref, xa_hbm, xb_hbm, w1_hbm, w3_hbm, w2_hbm,
                   ya_ref, yb_ref, w1_sc, w3_sc, w2_sc, xa_buf, xb_buf, sems, w1_st, w3_st, w2_st, wsems,
                   *, layer):
    blk = xa_buf.shape[1]
    ring = EXPERT_INPUT_SLOTS
    n_experts = w1_hbm.shape[1]
    per_step = ya_ref.shape[0] // blk
    nb = pl.num_programs(0) * per_step

    def fetch_weights(e, par):
        return (pltpu.make_async_copy(w1_hbm.at[layer, e], w1_st.at[par], wsems.at[par, 0]),
                pltpu.make_async_copy(w3_hbm.at[layer, e], w3_st.at[par], wsems.at[par, 1]),
                pltpu.make_async_copy(w2_hbm.at[layer, e], w2_st.at[par], wsems.at[par, 2]))

    def fetch(block, slot):
        rows = pl.ds(pl.multiple_of(block * blk, blk), blk)
        return (pltpu.make_async_copy(xa_hbm.at[rows], xa_buf.at[slot], sems.at[slot, 0]),
                pltpu.make_async_copy(xb_hbm.at[rows], xb_buf.at[slot], sems.at[slot, 1]))

    def one_block(b, out_rows):
        nv = nvalid_ref[b]
        prev_e = blk_e_ref[jnp.maximum(b - 1, 0)]

        @pl.when(b == 0)
        def _():
            for i in range(ring - 1):
                for c in fetch(i, i):
                    c.start()

        ahead = b + (ring - 1)

        @pl.when(ahead < nb)
        def _():
            for c in fetch(ahead, ahead % ring):
                c.start()

        slot = b % ring
        for c in fetch(b, slot):
            c.wait()

        @pl.when(b == 0)
        def _():
            for c in fetch_weights(blk_e_ref[0], 0):
                c.start()

        @pl.when((b == 0) | (blk_e_ref[b] != prev_e))
        def _():
            par = parity_ref[b]
            nxt = next_e_ref[b]

            @pl.when(nxt < n_experts)
            def _():
                for c in fetch_weights(nxt, 1 - par):
                    c.start()

            for c in fetch_weights(blk_e_ref[b], par):
                c.wait()
            w1_sc[...] = w1_st[par].astype(BF16)
            w3_sc[...] = w3_st[par].astype(BF16)
            w2_sc[...] = w2_st[par].astype(BF16)

        @pl.when(nv > 0)
        def _():
            x = _unpack_row_halves(xa_buf[slot], xb_buf[slot])
            rows = lax.broadcasted_iota(jnp.int32, x.shape, 0)
            x = jnp.where(rows < nv, x, 0.0).astype(BF16)
            hid = _silu(jnp.dot(x, w1_sc[...], preferred_element_type=F32)) * jnp.dot(
                x, w3_sc[...], preferred_element_type=F32)
            y = jnp.dot(hid.astype(BF16), w2_sc[...], preferred_element_type=F32)
            ya_ref[out_rows, :], yb_ref[out_rows, :] = _pack_row_halves(y)

        @pl.when(nv == 0)
        def _():
            ya_ref[out_rows, :] = jnp.zeros((blk, ya_ref.shape[1]), ya_ref.dtype)
            yb_ref[out_rows, :] = jnp.zeros((blk, yb_ref.shape[1]), yb_ref.dtype)

    for j in range(per_step):
        one_block(pl.program_id(0) * per_step + j, slice(j * blk, (j + 1) * blk))


def routed_experts(xa, xb, tables, w1, w3, w2, layer):
    P = xa.shape[0]
    blk = SLOT_BLOCK
    _, E, D, FF = w1.shape
    step_rows = EXPERT_BLOCKS_PER_STEP * blk
    slots = lambda: pl.BlockSpec((step_rows, PACK_W), lambda b, *_: (b, 0))
    grid_spec = pltpu.PrefetchScalarGridSpec(
        num_scalar_prefetch=len(tables),
        grid=(P // step_rows,),
        in_specs=[pl.BlockSpec(memory_space=pl.ANY)] * 5,
        out_specs=[slots(), slots()],
        scratch_shapes=[
            pltpu.VMEM((D, FF), BF16), pltpu.VMEM((D, FF), BF16), pltpu.VMEM((FF, D), BF16),
            pltpu.VMEM((EXPERT_INPUT_SLOTS, blk, PACK_W), jnp.int32),
            pltpu.VMEM((EXPERT_INPUT_SLOTS, blk, PACK_W), jnp.int32),
            pltpu.SemaphoreType.DMA((EXPERT_INPUT_SLOTS, 2)),
            pltpu.VMEM((2, D, FF), F32), pltpu.VMEM((2, D, FF), F32), pltpu.VMEM((2, FF, D), F32),
            pltpu.SemaphoreType.DMA((2, 3)),
        ],
    )
    return pl.pallas_call(
        functools.partial(_expert_kernel, layer=layer),
        grid_spec=grid_spec,
        out_shape=[jax.ShapeDtypeStruct((P, PACK_W), jnp.int32)] * 2,
        compiler_params=_cp(("arbitrary",), VMEM_LIMIT),
        name="routed_experts",
    )(*tables, xa, xb, w1, w3, w2)


def _combine_kernel(xmid_ref, ra_ref, rb_ref, mod2_ref, fg_ref, *rest):
    out_ref = rest[-1]
    D = xmid_ref.shape[1]
    x = xmid_ref[...] + mod2_ref[0][:, 2 * D:] * _unpack_row_halves(ra_ref[...], rb_ref[...])
    out_ref[...] = x * lax.rsqrt(jnp.mean(x * x, axis=-1, keepdims=True) + EPS) * fg_ref[...]


def combine(xmid, ra, rb, mod2, final_g, seq, out_rows=None, row0=0, out_buf=None):
    N, D = xmid.shape
    tm = 512
    tpb = seq // tm
    tile0 = row0 // tm
    in_specs = [
        pl.BlockSpec((tm, D), lambda i: (i, 0)),
        pl.BlockSpec((tm, PACK_W), lambda i: (i, 0)),
        pl.BlockSpec((tm, PACK_W), lambda i: (i, 0)),
        pl.BlockSpec((1, 1, 3 * D), lambda i: (i // tpb, 0, 0)),
        pl.BlockSpec((1, D), lambda i: (0, 0)),
    ]
    args = [xmid, ra, rb, mod2, final_g.reshape(1, D)]
    aliases = {}
    if out_buf is not None:
        in_specs.append(pl.BlockSpec(memory_space=pl.ANY))
        args.append(out_buf)
        aliases = {len(args) - 1: 0}
    return pl.pallas_call(
        _combine_kernel,
        grid=(N // tm,),
        in_specs=in_specs,
        out_specs=pl.BlockSpec((tm, D), lambda i: (i + tile0, 0)),
        out_shape=jax.ShapeDtypeStruct((out_rows or N, D), F32),
        input_output_aliases=aliases,
        compiler_params=_cp(("parallel",), VMEM_LIMIT),
        name="combine",
    )(*args)


TOKEN_STREAMS = 2


def _permute_w_in(w):
    ub = w[:, 3 * DA:3 * DA + DB]
    lat_lo = 3 * DA + DB
    lat_hi = lat_lo + Q_LORA + KV_LORA + QK_ROPE
    lat, gates = w[:, lat_lo:lat_hi], w[:, lat_hi:]
    pad = jnp.zeros((w.shape[0], LAT_W - (lat_hi - lat_lo)), w.dtype)
    parts = [gates, ub, lat, pad]
    for g in range(len(DIL_GROUPS)):
        sl = slice(g * GROUP_W, (g + 1) * GROUP_W)
        parts += [w[:, :DA][:, sl] * (HEAD_DIM_A ** -0.5), w[:, DA:2 * DA][:, sl], w[:, 2 * DA:3 * DA][:, sl]]
    return jnp.concatenate(parts, axis=1).astype(BF16)


def kernel(x, c, positions, ada_mix_w, ada_mix_b, norm_mix_g, w_in, pool_w, pool_scale, cq_norm_g, ckv_norm_g, w_uq, w_ukv, w_oa, w_ob, w_oc, w_out, ada_ffn_w, ada_ffn_b, norm_ffn_g, router_w, router_bias, exp_w1, exp_w3, exp_w2, sh_w1, sh_w3, sh_w2, final_g):
    B, S, D = x.shape
    depth = w_in.shape[0]
    mod_mix = adaln_rows(c, ada_mix_w, ada_mix_b)
    mod_ffn = adaln_rows(c, ada_ffn_w, ada_ffn_b)
    streams = TOKEN_STREAMS if B % TOKEN_STREAMS == 0 else 1
    Bs = B // streams
    Ns = Bs * S
    x_all = x.reshape(B * S, D)
    xs = [None] * streams
    out_all = None
    pos_s = [positions[s * Bs:(s + 1) * Bs] for s in range(streams)]
    for l in range(depth):
        last = l == depth - 1
        w_in_l = _permute_w_in(w_in[l])
        mla_w = _mla_weights(cq_norm_g[l], ckv_norm_g[l], w_uq[l], w_ukv[l])
        mix_w = (norm_ffn_g[l], pool_w[l].astype(BF16), pool_scale[l],
                 w_oa[l].astype(BF16), w_ob[l].astype(BF16), w_oc[l].astype(BF16), w_out[l].astype(BF16),
                 router_w[l].T.astype(BF16), sh_w1[l].astype(BF16), sh_w3[l].astype(BF16), sh_w2[l].astype(BF16))
        for s in range(streams):
            mod1 = mod_mix[l, s * Bs:(s + 1) * Bs].reshape(Bs, 1, 3 * D)
            mod2 = mod_ffn[l, s * Bs:(s + 1) * Bs].reshape(Bs, 1, 3 * D)
            if l == 0:
                x2, row0 = x_all, s * Ns
                gu, lat, *qkv = in_projection(x2, norm_mix_g[l], mod1, w_in_l, S, row0)
            else:
                row0 = 0
                x2, gu, lat, *qkv = in_projection(xs[s][0], norm_mix_g[l], mod1, w_in_l, S, 0, xs[s][1:])
            dil = [dilated_attention(qkv[2 * g], qkv[2 * g + 1]) for g in range(len(DIL_GROUPS))]
            q_all, k_all, vt_all = mla_prep(lat, pos_s[s], *mla_w, Bs, S)
            yc = mla_attention(q_all, k_all, vt_all, Bs, S)
            xmid, h2a, h2b, logits_t = mix_out(x2, gu, dil, yc, mod1, mod2, *mix_w, S, row0)
            dest, w_k, counts = route(logits_t, router_bias[l])
            tables = block_tables(counts, Ns)
            n_slots = tables[0].shape[0] * SLOT_BLOCK
            xa = sc_scatter_rows(h2a, dest, n_slots)
            xb = sc_scatter_rows(h2b, dest, n_slots)
            ya, yb = routed_experts(xa, xb, tables, exp_w1, exp_w3, exp_w2, l)
            ra = sc_weighted_gather(ya, dest, w_k)
            rb = sc_weighted_gather(yb, dest, w_k)
            if last:
                out_all = combine(xmid, ra, rb, mod2, final_g, S, B * S, s * Ns, out_all)
            else:
                xs[s] = (xmid, ra, rb, mod2)
    return out_all.reshape(B, S, D)
```

```python
import functools
import math

import jax
import jax.numpy as jnp
from jax import lax
from jax.experimental import pallas as pl
from jax.experimental.pallas import tpu as pltpu
from jax.experimental.pallas import tpu_sc as plsc

F32 = jnp.float32
BF16 = jnp.bfloat16
HIGHEST = lax.Precision.HIGHEST

D_MODEL = 1024
HEAD_DIM_A = 64
HEADS_PER_GROUP_A = 4
DIL_GROUPS = ((128, 1), (512, 4), (2048, 16))
GROUP_W = HEADS_PER_GROUP_A * HEAD_DIM_A
DA = GROUP_W * len(DIL_GROUPS)
POOL_WINDOWS = (2, 4, 8, 16)
POOL_GROUP_DIM = 128
DB = POOL_GROUP_DIM * len(POOL_WINDOWS)
POOL_HALO = 16
N_HEADS_C = 8
QK_NOPE = 64
QK_ROPE = 32
V_DIM = 64
Q_LORA = 384
KV_LORA = 256
DC = N_HEADS_C * V_DIM
HEAD_PAD_C = 128
ROPE_THETA = 10000.0
N_EXPERTS = 64
TOP_K = 8
N_GROUPS = 8
TOPK_GROUPS = 4
GROUP_SIZE = N_EXPERTS // N_GROUPS
ROUTED_SCALE = 2.5
EPS = 1e-6
NEG = -1e30
Q_BLOCK = 128

LAT_W = 768
GU_W = 3 * D_MODEL + DB
IN_OUT_WIDTHS = (GU_W, LAT_W) + (2 * GROUP_W, GROUP_W) * len(DIL_GROUPS)

VMEM_LIMIT = 56 * 1024 * 1024


def _cp(sem, vmem=None):
    return pltpu.CompilerParams(dimension_semantics=sem, vmem_limit_bytes=vmem)


def _silu(v):
    return v * jax.nn.sigmoid(v)


def _nt_dot(a, b):
    return lax.dot_general(a, b, (((1,), (1,)), ((), ())), preferred_element_type=F32)


PACK_W = D_MODEL // 4
_HI_MASK = -65536


def _bf16_bits(v):
    return lax.bitcast_convert_type(v.astype(BF16).astype(F32), jnp.int32)


def _pack_row_halves(v):
    halves = []
    for h in range(2):
        lo = _bf16_bits(v[:, (2 * h) * PACK_W:(2 * h + 1) * PACK_W])
        hi = _bf16_bits(v[:, (2 * h + 1) * PACK_W:(2 * h + 2) * PACK_W])
        halves.append(lax.shift_right_logical(lo, 16) | (hi & _HI_MASK))
    return halves


def _unpack_row_halves(wa, wb):
    parts = []
    for w in (wa, wb):
        parts.append(lax.bitcast_convert_type(lax.shift_left(w, 16), F32))
        parts.append(lax.bitcast_convert_type(w & _HI_MASK, F32))
    return jnp.concatenate(parts, axis=1)


def _adaln_kernel(c_ref, w_ref, b_ref, o_ref):
    s = _silu(c_ref[...])
    o_ref[0] = jnp.dot(s, w_ref[0], preferred_element_type=F32, precision=HIGHEST) + b_ref[0]


def adaln_rows(c, w, b):
    L, D, D3 = w.shape
    B = c.shape[0]
    tn = 1024
    return pl.pallas_call(
        _adaln_kernel,
        grid=(L, D3 // tn),
        in_specs=[
            pl.BlockSpec((B, D), lambda l, j: (0, 0)),
            pl.BlockSpec((1, D, tn), lambda l, j: (l, 0, j)),
            pl.BlockSpec((1, 1, tn), lambda l, j: (l, 0, j)),
        ],
        out_specs=pl.BlockSpec((1, B, tn), lambda l, j: (l, 0, j)),
        out_shape=jax.ShapeDtypeStruct((L, B, D3), F32),
        compiler_params=_cp(("parallel", "parallel")),
        name="adaln_rows",
    )(c, w, b.reshape(L, 1, D3))


LANES = 128


def _inproj_kernel(x_ref, g_ref, mod_ref, w_ref, *refs, chunk, pending):
    o_refs, scr = refs[:-1], refs[-1]
    D = x_ref.shape[1]
    x = x_ref[...]
    if pending:
        ra_ref, rb_ref, gate_ref, x_out_ref, *o_refs = o_refs
        x = x + gate_ref[0][:, 2 * D:] * _unpack_row_halves(ra_ref[...], rb_ref[...])
        x_out_ref[...] = x
    y = x * lax.rsqrt(jnp.mean(x * x, axis=-1, keepdims=True) + EPS) * g_ref[...]
    mod = mod_ref[0]
    h = (y * (1.0 + mod[:, D:2 * D]) + mod[:, :D]).astype(BF16)
    col = 0
    for o_ref in o_refs:
        width = o_ref.shape[-1]
        if o_ref.ndim == 2:
            for c0 in range(0, width, chunk):
                cw = min(chunk, width - c0)
                o_ref[:, c0:c0 + cw] = jnp.dot(
                    h, w_ref[:, col + c0:col + c0 + cw], preferred_element_type=F32).astype(o_ref.dtype)
        else:
            dil, rows = o_ref.shape[1], o_ref.shape[2]
            z = jnp.dot(h, w_ref[:, col:col + width], preferred_element_type=F32)
            if dil == 1:
                o_ref[0, 0] = z.astype(o_ref.dtype)
            else:
                for c in range(width // LANES):
                    scr[c] = z[:, c * LANES:(c + 1) * LANES]
                for r in range(dil):
                    o_ref[0, r] = jnp.concatenate(
                        [scr[c, pl.ds(r, rows, stride=dil), :] for c in range(width // LANES)],
                        axis=1).astype(o_ref.dtype)
        col += width


def in_projection(x2, g, mod, w, seq, row0=0, pending=None):
    D = x2.shape[1]
    B = mod.shape[0]
    N = B * seq
    tm = 512
    tpb = seq // tm
    tile0 = row0 // tm
    out_specs = [pl.BlockSpec((tm, wd), lambda i: (i, 0)) for wd in IN_OUT_WIDTHS[:2]]
    out_shape = [jax.ShapeDtypeStruct((N, wd), BF16) for wd in IN_OUT_WIDTHS[:2]]
    for grp, (_, dil) in enumerate(DIL_GROUPS):
        for wd in IN_OUT_WIDTHS[2 + 2 * grp:4 + 2 * grp]:
            out_specs.append(pl.BlockSpec((1, dil, tm // dil, wd), lambda i: (i // tpb, 0, i % tpb, 0)))
            out_shape.append(jax.ShapeDtypeStruct((B, dil, seq // dil, wd), BF16))
    in_specs = [
        pl.BlockSpec((tm, D), lambda i: (i + tile0, 0)),
        pl.BlockSpec((1, D), lambda i: (0, 0)),
        pl.BlockSpec((1, 1, 3 * D), lambda i: (i // tpb, 0, 0)),
        pl.BlockSpec(w.shape, lambda i: (0, 0), pipeline_mode=pl.Buffered(1)),
    ]
    args = [x2, g.reshape(1, D), mod, w]
    if pending is not None:
        in_specs += [pl.BlockSpec((tm, PACK_W), lambda i: (i, 0)), pl.BlockSpec((tm, PACK_W), lambda i: (i, 0)),
                     pl.BlockSpec((1, 1, 3 * D), lambda i: (i // tpb, 0, 0))]
        args += list(pending)
        out_specs.insert(0, pl.BlockSpec((tm, D), lambda i: (i, 0)))
        out_shape.insert(0, jax.ShapeDtypeStruct((N, D), F32))
    return pl.pallas_call(
        functools.partial(_inproj_kernel, chunk=512, pending=pending is not None),
        grid=(N // tm,),
        in_specs=in_specs,
        out_specs=out_specs,
        out_shape=out_shape,
        scratch_shapes=[pltpu.VMEM((max(IN_OUT_WIDTHS[2:]) // LANES, tm, LANES), F32)],
        compiler_params=_cp(("parallel",), VMEM_LIMIT),
        name="in_projection",
    )(*args)


def _dilated_kernel(q_ref, kc_ref, kp_ref, vc_ref, vp_ref, o_ref, lse_ref):
    i = pl.program_id(1)
    T = Q_BLOCK
    key = lax.broadcasted_iota(jnp.int32, (T, T), 0)
    qry = lax.broadcasted_iota(jnp.int32, (T, T), 1)
    valid_c = key <= qry
    near = key >= qry
    seqs, run = q_ref.shape[0], q_ref.shape[1] // T
    heads = [slice(h * HEAD_DIM_A, (h + 1) * HEAD_DIM_A) for h in range(HEADS_PER_GROUP_A)]

    def transposed(v):
        return v.astype(F32).T.astype(BF16)

    vts = {(s, j): transposed(vc_ref[s, j * T:(j + 1) * T, :]) for s in range(seqs) for j in range(run)}
    vt_before = [transposed(vp_ref[s]) for s in range(seqs)]

    def blocks(s, j):
        rows = slice(j * T, (j + 1) * T)
        if j == 0:
            return rows, kc_ref[s, rows, :], vts[s, 0], kp_ref[s], vt_before[s], near & (i > 0)
        before = slice((j - 1) * T, j * T)
        return rows, kc_ref[s, rows, :], vts[s, j], kc_ref[s, before, :], vts[s, j - 1], near

    scores, probs = {}, {}
    for s in range(seqs):
        for j in range(run):
            rows, kc, _, kp, _, valid_p = blocks(s, j)
            q = q_ref[s, rows, :]
            for h, sl in enumerate(heads):
                qh = q[:, sl]
                scores[s, j, h] = (jnp.where(valid_c, _nt_dot(kc[:, sl], qh), NEG),
                                   jnp.where(valid_p, _nt_dot(kp[:, sl], qh), NEG))
    for chain, (sc, sp) in scores.items():
        m = jnp.maximum(jnp.max(sc, axis=0, keepdims=True), jnp.max(sp, axis=0, keepdims=True))
        pc = jnp.exp(sc - m)
        pp = jnp.exp(sp - m)
        den = jnp.sum(pc, axis=0, keepdims=True) + jnp.sum(pp, axis=0, keepdims=True)
        probs[chain] = (pc.astype(BF16), pp.astype(BF16), den, m + jnp.log(den))
    spread = LSE_LANES // len(heads)
    for s in range(seqs):
        for j in range(run):
            rows, _, vtc, _, vtp, _ = blocks(s, j)
            outs = []
            for h, sl in enumerate(heads):
                pc, pp, den, _ = probs[s, j, h]
                o = (jnp.dot(vtc[sl, :], pc, preferred_element_type=F32)
                     + jnp.dot(vtp[sl, :], pp, preferred_element_type=F32))
                outs.append(o / den)
            o_ref[s, rows, :] = jnp.concatenate(outs, axis=0).T.astype(o_ref.dtype)
            lse_t = jnp.concatenate(
                [jnp.broadcast_to(probs[s, j, h][3], (spread, T)) for h in range(len(heads))], axis=0)
            lse_ref[s, rows, :] = lse_t.T


DILATED_RUN = 16


LSE_LANES = 128


def dilated_attention(qk, v):
    batch, dilation, L, _ = qk.shape
    nb = L // Q_BLOCK
    run = min(DILATED_RUN, nb)
    seqs = DILATED_RUN // run
    qk_r = qk.reshape(batch * dilation, L, 2 * GROUP_W)
    v_r = v.reshape(batch * dilation, L, GROUP_W)
    before = lambda i: jnp.maximum(i * run - 1, 0)
    o, lse = pl.pallas_call(
        _dilated_kernel,
        grid=(batch * dilation // seqs, nb // run),
        in_specs=[
            pl.BlockSpec((seqs, run * Q_BLOCK, GROUP_W), lambda s, i: (s, i, 0)),
            pl.BlockSpec((seqs, run * Q_BLOCK, GROUP_W), lambda s, i: (s, i, 1)),
            pl.BlockSpec((seqs, Q_BLOCK, GROUP_W), lambda s, i: (s, before(i), 1)),
            pl.BlockSpec((seqs, run * Q_BLOCK, GROUP_W), lambda s, i: (s, i, 0)),
            pl.BlockSpec((seqs, Q_BLOCK, GROUP_W), lambda s, i: (s, before(i), 0)),
        ],
        out_specs=[
            pl.BlockSpec((seqs, run * Q_BLOCK, GROUP_W), lambda s, i: (s, i, 0)),
            pl.BlockSpec((seqs, run * Q_BLOCK, LSE_LANES), lambda s, i: (s, i, 0)),
        ],
        out_shape=[
            jax.ShapeDtypeStruct((batch * dilation, L, GROUP_W), BF16),
            jax.ShapeDtypeStruct((batch * dilation, L, LSE_LANES), F32),
        ],
        compiler_params=_cp(("parallel", "parallel")),
        name=f"dilated_attention_d{dilation}",
    )(qk_r, qk_r, qk_r, v_r, v_r)
    return o.reshape(batch, dilation, L, GROUP_W), lse.reshape(batch, dilation, L, LSE_LANES)


def _mla_prep_kernel(lat_ref, pos_ref, gq_ref, gkv_ref, wq_ref, wk_ref, wvt_ref, freq_ref, spread_ref, one_ref,
                     q_ref, k_ref, vt_ref):
    HP = N_HEADS_C * HEAD_PAD_C
    lat = lat_ref[...].astype(F32)
    cq = lat[:, :Q_LORA]
    ckr = lat[:, Q_LORA:]
    zq = (cq * lax.rsqrt(jnp.mean(cq * cq, axis=-1, keepdims=True) + EPS) * gq_ref[...]).astype(BF16)
    lane = lax.broadcasted_iota(jnp.int32, ckr.shape, 1)
    is_kv = lane < KV_LORA
    ms = jnp.sum(jnp.where(is_kv, ckr * ckr, 0.0), axis=-1, keepdims=True) * (1.0 / KV_LORA)
    zkv = (ckr * jnp.where(is_kv, lax.rsqrt(ms + EPS) * gkv_ref[...], 1.0)).astype(BF16)
    qq = jnp.dot(zq, wq_ref[...], preferred_element_type=F32)
    kk = jnp.dot(zkv, wk_ref[:, :HP], preferred_element_type=F32)
    kk_sw = jnp.dot(zkv[:, KV_LORA:], wk_ref[KV_LORA:, HP:], preferred_element_type=F32)
    ang_t = freq_ref[...] * pos_ref[0].astype(F32)

    def to_lanes(t):
        hi = t.astype(BF16)
        lo = (t - hi.astype(F32)).astype(BF16)
        tn_dot = lambda a: lax.dot_general(a, spread_ref[...], (((0,), (0,)), ((), ())), preferred_element_type=F32)
        return tn_dot(hi) + tn_dot(lo)

    cos = to_lanes(jnp.cos(ang_t)) + one_ref[...]
    sin = to_lanes(jnp.sin(ang_t))
    slot_lane = lax.broadcasted_iota(jnp.int32, (1, HEAD_PAD_C), 1)
    half = QK_ROPE // 2
    sin_x1 = jnp.where((slot_lane >= QK_NOPE) & (slot_lane < QK_NOPE + half), -sin, 0.0)
    sin_x2 = jnp.where((slot_lane >= QK_NOPE + half) & (slot_lane < QK_NOPE + QK_ROPE), sin, 0.0)
    for h in range(N_HEADS_C):
        lo, hi = h * HEAD_PAD_C, (h + 1) * HEAD_PAD_C
        qh = qq[:, lo:hi]
        q_ref[:, lo:hi] = (qh * cos + pltpu.roll(qh, HEAD_PAD_C - half, axis=1) * sin_x1
                           + pltpu.roll(qh, half, axis=1) * sin_x2).astype(q_ref.dtype)
        k_ref[:, lo:hi] = (kk[:, lo:hi] * cos + kk_sw[:, lo:hi] * sin).astype(k_ref.dtype)
    vt_ref[0] = _nt_dot(wvt_ref[...], zkv).astype(vt_ref.dtype)


def _mla_weights(cq_g, ckv_g, w_uq, w_ukv):
    H, HPAD, half = N_HEADS_C, HEAD_PAD_C, QK_ROPE // 2
    scale = (QK_NOPE + QK_ROPE) ** -0.5 * math.log2(math.e)
    wq = w_uq.reshape(Q_LORA, H, QK_NOPE + QK_ROPE) * scale
    wq_big = jnp.pad(wq, ((0, 0), (0, 0), (0, HPAD - QK_NOPE - QK_ROPE))).reshape(Q_LORA, H * HPAD)

    rows = LAT_W - Q_LORA
    wkv = w_ukv.reshape(KV_LORA, H, QK_NOPE + V_DIM)
    eye = jnp.eye(QK_ROPE, dtype=F32)
    k_lin = jnp.zeros((rows, H, HPAD), F32)
    k_lin = k_lin.at[:KV_LORA, :, :QK_NOPE].set(wkv[..., :QK_NOPE])
    k_lin = k_lin.at[KV_LORA:KV_LORA + QK_ROPE, :, QK_NOPE:QK_NOPE + QK_ROPE].set(
        jnp.broadcast_to(eye[:, None, :], (QK_ROPE, H, QK_ROPE)))
    swap = jnp.zeros((QK_ROPE, QK_ROPE), F32).at[half:, :half].set(-jnp.eye(half)).at[:half, half:].set(jnp.eye(half))
    k_sw = jnp.zeros((rows, H, HPAD), F32)
    k_sw = k_sw.at[KV_LORA:KV_LORA + QK_ROPE, :, QK_NOPE:QK_NOPE + QK_ROPE].set(
        jnp.broadcast_to(swap[:, None, :], (QK_ROPE, H, QK_ROPE)))
    v_w = jnp.zeros((rows, H, V_DIM), F32).at[:KV_LORA].set(wkv[..., QK_NOPE:])
    wk_big = jnp.concatenate([k_lin.reshape(rows, H * HPAD), k_sw.reshape(rows, H * HPAD)], axis=1)
    wv_t = v_w.reshape(rows, H * V_DIM).T

    gkv = jnp.concatenate([ckv_g, jnp.ones((rows - KV_LORA,), F32)]).reshape(1, rows)
    return cq_g.reshape(1, Q_LORA), gkv, wq_big.astype(BF16), wk_big.astype(BF16), wv_t.astype(BF16)


def _rope_tables():
    half = QK_ROPE // 2
    freqs = (ROPE_THETA ** (-jnp.arange(0, QK_ROPE, 2, dtype=F32) / QK_ROPE)).reshape(half, 1)
    lane = jnp.arange(HEAD_PAD_C)[None, :]
    j = jnp.arange(half)[:, None]
    spread = (lane == QK_NOPE + j) | (lane == QK_NOPE + half + j)
    off_rope = ~jnp.any(spread, axis=0, keepdims=True)
    return freqs, spread.astype(BF16), off_rope.astype(F32)


def mla_prep(lat, positions, gq, gkv, wq_big, wk_big, wv_t, batch, seq):
    N = lat.shape[0]
    HP = N_HEADS_C * HEAD_PAD_C
    tm = 512
    tpb = seq // tm
    freqs, spread, off_rope = _rope_tables()
    pos_rows = positions.reshape(N // tm, 1, tm)
    const = lambda shape: pl.BlockSpec(shape, lambda i: (0, 0))
    return pl.pallas_call(
        _mla_prep_kernel,
        grid=(N // tm,),
        in_specs=[
            pl.BlockSpec((tm, LAT_W), lambda i: (i, 0)),
            pl.BlockSpec((1, 1, tm), lambda i: (i, 0, 0)),
            const(gq.shape), const(gkv.shape), const(wq_big.shape), const(wk_big.shape), const(wv_t.shape),
            const(freqs.shape), const(spread.shape), const(off_rope.shape),
        ],
        out_specs=[
            pl.BlockSpec((tm, HP), lambda i: (i, 0)),
            pl.BlockSpec((tm, HP), lambda i: (i, 0)),
            pl.BlockSpec((1, DC, tm), lambda i: (i // tpb, 0, i % tpb)),
        ],
        out_shape=[
            jax.ShapeDtypeStruct((N, HP), BF16),
            jax.ShapeDtypeStruct((N, HP), BF16),
            jax.ShapeDtypeStruct((batch, DC, seq), BF16),
        ],
        compiler_params=_cp(("parallel",), VMEM_LIMIT),
        name="mla_prep",
    )(lat, pos_rows, gq, gkv, wq_big, wk_big, wv_t, freqs, spread, off_rope)


HEADS_PER_STEP_C = 8
FLASH_Q_CHUNK = 256


def _mla_flash_kernel(qi_ref, ki_ref, q_ref, k_ref, vt_ref, o_ref, m_sc, l_sc, acc_sc):
    t = pl.program_id(2)
    qi, ki = qi_ref[t], ki_ref[t]

    @pl.when(ki == 0)
    def _():
        m_sc[...] = jnp.full(m_sc.shape, NEG, F32)
        l_sc[...] = jnp.zeros(l_sc.shape, F32)
        acc_sc[...] = jnp.zeros(acc_sc.shape, F32)

    def step(masked):
        T = q_ref.shape[1]
        if masked:
            key = lax.broadcasted_iota(jnp.int32, (T, T), 0)
            qry = lax.broadcasted_iota(jnp.int32, (T, T), 1)
            keep = key <= qry
        chains = [(h, c) for h in range(HEADS_PER_STEP_C) for c in range(T // FLASH_Q_CHUNK)]
        scores, probs, alphas = {}, {}, {}

        def keys_for(c):
            return (c + 1) * FLASH_Q_CHUNK if masked else T

        def qk(h, c):
            qs = slice(c * FLASH_Q_CHUNK, (c + 1) * FLASH_Q_CHUNK)
            q = q_ref[0, qs, h * HEAD_PAD_C:(h + 1) * HEAD_PAD_C]
            k = k_ref[0, :keys_for(c), h * HEAD_PAD_C:(h + 1) * HEAD_PAD_C]
            st = _nt_dot(k, q)
            scores[h, c] = jnp.where(keep[:keys_for(c), qs], st, NEG) if masked else st

        def softmax(h, c):
            qs = slice(c * FLASH_Q_CHUNK, (c + 1) * FLASH_Q_CHUNK)
            st = scores.pop((h, c))
            m_prev = m_sc[h, :, qs]
            m_new = jnp.maximum(m_prev, jnp.max(st, axis=0, keepdims=True))
            alpha = jnp.exp2(m_prev - m_new)
            p = jnp.exp2(st - m_new)
            l_sc[h, :, qs] = alpha * l_sc[h, :, qs] + jnp.sum(p, axis=0, keepdims=True)
            m_sc[h, :, qs] = m_new
            probs[h, c], alphas[h, c] = p.astype(BF16), alpha

        def pv(h, c):
            qs = slice(c * FLASH_Q_CHUNK, (c + 1) * FLASH_Q_CHUNK)
            vt = vt_ref[0, h * V_DIM:(h + 1) * V_DIM, :keys_for(c)]
            acc_sc[h, :, qs] = alphas.pop((h, c)) * acc_sc[h, :, qs] + jnp.dot(
                vt, probs.pop((h, c)), preferred_element_type=F32)

        for phase in (qk, softmax, pv):
            for ch in chains:
                phase(*ch)

    @pl.when(ki < qi)
    def _():
        step(False)

    @pl.when(ki == qi)
    def _():
        step(True)
        ot = jnp.concatenate([acc_sc[h] / l_sc[h] for h in range(HEADS_PER_STEP_C)], axis=0)
        o_ref[0] = ot.T.astype(o_ref.dtype)


def mla_attention(q_all, k_all, vt_all, batch, seq):
    T = 512
    nq = seq // T
    pairs = [(a, b) for a in range(nq) for b in range(a + 1)]
    qi_tab = jnp.asarray([p[0] for p in pairs], jnp.int32)
    ki_tab = jnp.asarray([p[1] for p in pairs], jnp.int32)
    hp = N_HEADS_C // HEADS_PER_STEP_C
    qw = HEADS_PER_STEP_C * HEAD_PAD_C
    vw = HEADS_PER_STEP_C * V_DIM
    q3 = q_all.reshape(batch, seq, -1)
    k3 = k_all.reshape(batch, seq, -1)
    grid_spec = pltpu.PrefetchScalarGridSpec(
        num_scalar_prefetch=2,
        grid=(batch, hp, len(pairs)),
        in_specs=[
            pl.BlockSpec((1, T, qw), lambda b, h, t, qi, ki: (b, qi[t], h)),
            pl.BlockSpec((1, T, qw), lambda b, h, t, qi, ki: (b, ki[t], h)),
            pl.BlockSpec((1, vw, T), lambda b, h, t, qi, ki: (b, h, ki[t])),
        ],
        out_specs=pl.BlockSpec((1, T, vw), lambda b, h, t, qi, ki: (b, qi[t], h)),
        scratch_shapes=[
            pltpu.VMEM((HEADS_PER_STEP_C, 1, T), F32),
            pltpu.VMEM((HEADS_PER_STEP_C, 1, T), F32),
            pltpu.VMEM((HEADS_PER_STEP_C, V_DIM, T), F32),
        ],
    )
    o = pl.pallas_call(
        _mla_flash_kernel,
        grid_spec=grid_spec,
        out_shape=jax.ShapeDtypeStruct((batch, seq, DC), BF16),
        compiler_params=_cp(("parallel", "parallel", "arbitrary")),
        name="mla_attention",
    )(qi_tab, ki_tab, q3, k3, vt_all)
    return o.reshape(batch * seq, DC)


def _mixout_kernel(x_ref, gates_ref, ub_ref, ubh_ref, o1_ref, o2_ref, o3_ref, l1_ref, l2_ref, l3_ref, yc_ref,
                   mod1_ref, mod2_ref, g2_ref, poolw_ref, pscale_ref, woa_ref, wob_ref, woc_ref, wout_ref,
                   rwt_ref, sw1_ref, sw3_ref, sw2_ref, spread_ref,
                   xmid_ref, h2a_ref, h2b_ref, logit_ref, *scratch, tiles_per_batch):
    D = x_ref.shape[1]
    tm = x_ref.shape[0]
    tile = pl.program_id(0) % tiles_per_batch
    o_scrs, l_scrs = scratch[:3], scratch[3:]

    def token_order(ref, scr):
        dil, rows, width = ref.shape[1:]
        if dil == 1:
            return ref[0, 0].astype(F32)
        for r in range(dil):
            v = ref[0, r].astype(F32)
            for c in range(width // LANES):
                scr[c, pl.ds(r, rows, stride=dil), :] = v[:, c * LANES:(c + 1) * LANES]
        return jnp.concatenate([scr[c] for c in range(width // LANES)], axis=1)

    outs = [token_order(r, s) for r, s in zip((o1_ref, o2_ref, o3_ref), o_scrs)]
    l1, l2, l3 = [token_order(r, s) for r, s in zip((l1_ref, l2_ref, l3_ref), l_scrs)]
    mx = jnp.maximum(jnp.maximum(l1, l2), l3)
    es = [jnp.exp(l1 - mx), jnp.exp(l2 - mx), jnp.exp(l3 - mx)]
    inv = 1.0 / (es[0] + es[1] + es[2])
    ya = jnp.zeros((tm, GROUP_W), F32)
    for e, o in zip(es, outs):
        w = e * inv
        w_hi = w.astype(BF16)
        w_lo = (w - w_hi.astype(F32)).astype(BF16)
        w_wide = (jnp.dot(w_hi, spread_ref[...], preferred_element_type=F32)
                  + jnp.dot(w_lo, spread_ref[...], preferred_element_type=F32))
        ya = ya + w_wide * o
    a_out = jnp.dot(ya.astype(BF16), woa_ref[...], preferred_element_type=F32)

    u = ub_ref[...].astype(F32)
    halo = jnp.where(tile > 0, ubh_ref[...].astype(F32), 0.0)
    ext = jnp.concatenate([halo, u], axis=0)
    t_seq = tile * tm + lax.broadcasted_iota(jnp.int32, (tm, 1), 0)
    pooled = []
    for gi, w in enumerate(POOL_WINDOWS):
        sl = slice(gi * POOL_GROUP_DIM, (gi + 1) * POOL_GROUP_DIM)
        acc = ext[:, sl]
        k = 1
        while k < w:
            acc = acc + pltpu.roll(acc, k, axis=0)
            k *= 2
        cnt = jnp.minimum(t_seq + 1, w).astype(F32)
        pg = acc[POOL_HALO:] / cnt - u[:, sl]
        pooled.append(jnp.dot(pg.astype(BF16), poolw_ref[gi], preferred_element_type=F32))
    yb = jnp.concatenate(pooled, axis=1) * pscale_ref[...]
    b_out = jnp.dot(yb.astype(BF16), wob_ref[...], preferred_element_type=F32)
    c_out = jnp.dot(yc_ref[...], woc_ref[...], preferred_element_type=F32)

    g = gates_ref[...].astype(F32)
    mix = (jax.nn.sigmoid(g[:, :D]) * a_out + jax.nn.sigmoid(g[:, D:2 * D]) * b_out
           + jax.nn.sigmoid(g[:, 2 * D:]) * c_out)
    tok = jnp.dot(mix.astype(BF16), wout_ref[...], preferred_element_type=F32)
    xn = x_ref[...] + mod1_ref[0][:, 2 * D:] * tok

    mod2 = mod2_ref[0]
    y = xn * lax.rsqrt(jnp.mean(xn * xn, axis=-1, keepdims=True) + EPS) * g2_ref[...]
    h2 = y * (1.0 + mod2[:, D:2 * D]) + mod2[:, :D]
    h2b = h2.astype(BF16)
    h2a_ref[...], h2b_ref[...] = _pack_row_halves(h2b)
    logit_ref[...] = _nt_dot(rwt_ref[...], h2b)
    hid = _silu(jnp.dot(h2b, sw1_ref[...], preferred_element_type=F32)) * jnp.dot(
        h2b, sw3_ref[...], preferred_element_type=F32)
    shared = jnp.dot(hid.astype(BF16), sw2_ref[...], preferred_element_type=F32)
    xmid_ref[...] = xn + mod2[:, 2 * D:] * shared


def mix_out(x2, gu, dil, yc, mod1, mod2, g2, pool_w, pool_scale, w_oa, w_ob, w_oc, w_out, rwt, sw1, sw3, sw2, seq,
            row0=0):
    D = x2.shape[1]
    N = gu.shape[0]
    tm = 512
    tpb = seq // tm
    tile0 = row0 // tm
    (o1, l1), (o2, l2), (o3, l3) = dil
    row = lambda w, c=0: pl.BlockSpec((tm, w), lambda i: (i, c))
    by_residue = lambda a: pl.BlockSpec(
        (1, a.shape[1], tm // a.shape[1], a.shape[3]), lambda i: (i // tpb, 0, i % tpb, 0))
    heads = HEADS_PER_GROUP_A
    spread = (jnp.arange(LSE_LANES)[:, None] == (jnp.arange(GROUP_W)[None, :] // HEAD_DIM_A) * (LSE_LANES // heads)
              ).astype(BF16)
    const2 = lambda a: pl.BlockSpec(a.shape, lambda i: (0,) * a.ndim, pipeline_mode=pl.Buffered(1))
    modspec = pl.BlockSpec((1, 1, 3 * D), lambda i: (i // tpb, 0, 0))
    ub_col = 3 * D // DB
    halo_spec = pl.BlockSpec(
        (POOL_HALO, DB), lambda i: (jnp.maximum(i * (tm // POOL_HALO) - 1, 0), ub_col))
    weights = [g2.reshape(1, D), pool_w, pool_scale.reshape(1, DB), w_oa, w_ob, w_oc, w_out, rwt, sw1, sw3, sw2,
               spread]
    return pl.pallas_call(
        functools.partial(_mixout_kernel, tiles_per_batch=tpb),
        grid=(N // tm,),
        in_specs=[
            pl.BlockSpec((tm, D), lambda i: (i + tile0, 0)), row(3 * D), row(DB, ub_col), halo_spec,
            by_residue(o1), by_residue(o2), by_residue(o3), by_residue(l1), by_residue(l2), by_residue(l3), row(DC),
            modspec, modspec,
        ] + [const2(a) for a in weights],
        scratch_shapes=[pltpu.VMEM((GROUP_W // LANES, tm, LANES), F32)] * 3
        + [pltpu.VMEM((LSE_LANES // LANES, tm, LANES), F32)] * 3,
        out_specs=[row(D), row(PACK_W), row(PACK_W), pl.BlockSpec((N_EXPERTS, tm), lambda i: (0, i))],
        out_shape=[
            jax.ShapeDtypeStruct((N, D), F32),
            jax.ShapeDtypeStruct((N, PACK_W), jnp.int32),
            jax.ShapeDtypeStruct((N, PACK_W), jnp.int32),
            jax.ShapeDtypeStruct((N_EXPERTS, N), F32),
        ],
        compiler_params=_cp(("parallel",), VMEM_LIMIT),
        name="mix_out",
    )(x2, gu, gu, gu, o1, o2, o3, l1, l2, l3, yc, mod1, mod2, *weights)


def _pick_rows(table, picks):
    G, GS = N_GROUPS, GROUP_SIZE
    eio = lax.broadcasted_iota(jnp.int32, (GS, table.shape[1]), 0)
    rows = []
    for k in range(TOP_K):
        idx = picks[k:k + 1]
        parts = [jnp.where(eio + g * GS == idx, table[g * GS:(g + 1) * GS], 0.0) for g in range(G)]
        rows.append(jnp.sum(functools.reduce(jnp.add, parts), axis=0, keepdims=True))
    return jnp.concatenate(rows, axis=0)


def _route_choose(lg_ref, bias_ref):
    G, GS = N_GROUPS, GROUP_SIZE
    scores = jax.nn.sigmoid(lg_ref[...])
    sel = scores + bias_ref[...]
    tn = sel.shape[1]
    eio = lax.broadcasted_iota(jnp.int32, (GS, tn), 0)
    ninf = -jnp.inf

    gs = []
    for g in range(G):
        v = sel[g * GS:(g + 1) * GS]
        m1 = jnp.max(v, axis=0, keepdims=True)
        i1 = jnp.min(jnp.where(v == m1, eio, GS), axis=0, keepdims=True)
        m2 = jnp.max(jnp.where(eio == i1, ninf, v), axis=0, keepdims=True)
        gs.append(m1 + m2)
    gsm = jnp.concatenate(gs, axis=0)
    gio = lax.broadcasted_iota(jnp.int32, (G, tn), 0)
    rank = jnp.zeros((G, tn), jnp.int32)
    for g2 in range(G):
        beats = (gs[g2] > gsm) | ((gs[g2] == gsm) & (g2 < gio))
        rank = rank + beats.astype(jnp.int32)
    gsel = rank < TOPK_GROUPS

    vs = [jnp.where(gsel[g:g + 1], sel[g * GS:(g + 1) * GS], NEG) for g in range(G)]
    eid = [eio + g * GS for g in range(G)]
    chosen = [jnp.zeros((GS, tn), jnp.bool_) for _ in range(G)]
    picks = []
    for _ in range(TOP_K):
        m = jnp.max(functools.reduce(jnp.maximum, vs), axis=0, keepdims=True)
        idx = jnp.min(functools.reduce(jnp.minimum, [jnp.where(v == m, e, N_EXPERTS) for v, e in zip(vs, eid)]),
                      axis=0, keepdims=True)
        picks.append(idx)
        for g in range(G):
            hit = eid[g] == idx
            chosen[g] = chosen[g] | hit
            vs[g] = jnp.where(hit, ninf, vs[g])
    mask = jnp.concatenate(chosen, axis=0).astype(F32)
    return scores, jnp.concatenate(picks, axis=0), mask


def _route_kernel(lg_ref, bias_ref, tri_ref, dest_ref, w_ref, cnt_ref, run_sc, start_sc, mask_sc, picks_sc,
                  *, slot_block):
    phase = pl.program_id(0)
    step = pl.program_id(1)
    tn = lg_ref.shape[1]
    cols = pl.ds(pl.multiple_of(step * tn, tn), tn)

    @pl.when(phase == 0)
    def _():
        @pl.when(step == 0)
        def _():
            run_sc[...] = jnp.zeros(run_sc.shape, F32)

        scores, picks, mask = _route_choose(lg_ref, bias_ref)
        wk = _pick_rows(scores, picks)
        w_ref[0] = wk / jnp.sum(wk, axis=0, keepdims=True) * ROUTED_SCALE
        dest_ref[0] = jnp.zeros(dest_ref.shape[1:], dest_ref.dtype)
        mask_sc[:, cols] = mask.astype(BF16)
        picks_sc[:, cols] = picks
        run_sc[...] = run_sc[...] + jnp.sum(mask, axis=1, keepdims=True)

    @pl.when(phase == 1)
    def _():
        @pl.when(step == 0)
        def _():
            counts = run_sc[...].astype(jnp.int32)
            cnt_ref[...] = jnp.broadcast_to(counts, cnt_ref.shape)
            shift = slot_block.bit_length() - 1
            padded = lax.shift_left(lax.shift_right_logical(counts + (slot_block - 1), shift), shift).astype(F32)
            r = lax.broadcasted_iota(jnp.int32, (N_EXPERTS, N_EXPERTS), 0)
            c = lax.broadcasted_iota(jnp.int32, (N_EXPERTS, N_EXPERTS), 1)
            as_row = jnp.sum(jnp.where(r == c, padded, 0.0), axis=0, keepdims=True)
            start_sc[...] = jnp.sum(jnp.where(c < r, as_row, 0.0), axis=1, keepdims=True)
            run_sc[...] = jnp.zeros(run_sc.shape, F32)

        mask_b = mask_sc[:, cols]
        mask = mask_b.astype(F32)
        before = jnp.dot(mask_b, tri_ref[...], preferred_element_type=F32) - mask
        slot = start_sc[...] + run_sc[...] + before
        dest_ref[0] = _pick_rows(slot, picks_sc[:, cols]).astype(jnp.int32)
        w_ref[0] = jnp.zeros(w_ref.shape[1:], w_ref.dtype)
        run_sc[...] = run_sc[...] + jnp.sum(mask, axis=1, keepdims=True)


SLOT_BLOCK = 512


def route(logits_t, bias):
    E, N = logits_t.shape
    tn = 1024
    tri = (jnp.arange(tn)[:, None] <= jnp.arange(tn)[None, :]).astype(BF16)
    plane = lambda: pl.BlockSpec((1, TOP_K, tn), lambda p, i: (p, 0, i))
    dest, w, cnt = pl.pallas_call(
        functools.partial(_route_kernel, slot_block=SLOT_BLOCK),
        grid=(2, N // tn),
        in_specs=[
            pl.BlockSpec((E, tn), lambda p, i: (0, i * (1 - p))),
            pl.BlockSpec((E, 1), lambda p, i: (0, 0)),
            pl.BlockSpec((tn, tn), lambda p, i: (0, 0)),
        ],
        out_specs=[plane(), plane(), pl.BlockSpec((E, 128), lambda p, i: (0, 0))],
        out_shape=[
            jax.ShapeDtypeStruct((2, TOP_K, N), jnp.int32),
            jax.ShapeDtypeStruct((2, TOP_K, N), F32),
            jax.ShapeDtypeStruct((E, 128), jnp.int32),
        ],
        scratch_shapes=[pltpu.VMEM((E, 1), F32), pltpu.VMEM((E, 1), F32),
                        pltpu.VMEM((E, N), BF16), pltpu.VMEM((TOP_K, N), jnp.int32)],
        compiler_params=_cp(("arbitrary", "arbitrary")),
        name="route",
    )(logits_t, bias.reshape(E, 1), tri)
    return dest[1], w[0], cnt[:, 0]


def block_tables(counts, n_tokens):
    E = counts.shape[0]
    blk = SLOT_BLOCK
    nblk = (n_tokens * TOP_K + E * blk) // blk
    per_expert = (counts + blk - 1) // blk
    bend = jnp.cumsum(per_expert)
    bstart = bend - per_expert
    b = jnp.arange(nblk, dtype=jnp.int32)[:, None]
    owns = (bstart[None, :] <= b) & (b < bend[None, :])
    blk_e = jnp.minimum(jnp.sum(bend[None, :] <= b, axis=1), E - 1).astype(jnp.int32)
    rows_left = counts[None, :] - (b - bstart[None, :]) * blk
    nvalid = jnp.sum(jnp.where(owns, jnp.clip(rows_left, 0, blk), 0), axis=1)
    first = jnp.concatenate([jnp.ones((1,), jnp.bool_), blk_e[1:] != blk_e[:-1]])
    run_parity = ((jnp.cumsum(first.astype(jnp.int32)) - 1) % 2).astype(jnp.int32)
    later = blk_e[None, :] > blk_e[:, None]
    next_e = jnp.min(jnp.where(later, blk_e[None, :], E), axis=1).astype(jnp.int32)
    return blk_e, nvalid.astype(jnp.int32), run_parity, next_e


def _sc_mesh():
    return plsc.VectorSubcoreMesh(core_axis_name="c", subcore_axis_name="s")


SC_WINDOW = 128


def sc_scatter_rows(x, dest, n_slots):
    N, W = x.shape
    K = dest.shape[0]

    @functools.partial(pl.kernel, out_type=jax.ShapeDtypeStruct((n_slots, W), x.dtype), mesh=_sc_mesh(),
                       scratch_types=[])
    def scatter(x_hbm, i_hbm, o_hbm):
        def body(x_vmem, i_vmem):
            for k in range(K):
                pltpu.sync_copy(x_vmem, o_hbm.at[i_vmem.at[k]])

        pltpu.emit_pipeline(
            body,
            grid=(N // SC_WINDOW,),
            in_specs=[pl.BlockSpec((SC_WINDOW, W), lambda i: (i, 0)),
                      pl.BlockSpec((K, SC_WINDOW), lambda i: (0, i))],
            out_specs=[],
            core_axis_name=("c", "s"),
            dimension_semantics=(pltpu.PARALLEL,),
        )(x_hbm, i_hbm)

    return scatter(x, dest)


SC_LANES = 16
SC_GATHER_TOKENS = 8


def sc_weighted_gather(y, dest, wts):
    W = y.shape[1]
    K, N = dest.shape
    G, L = SC_GATHER_TOKENS, SC_LANES
    batches = SC_WINDOW // G

    @functools.partial(
        pl.kernel, out_type=jax.ShapeDtypeStruct((N, W), y.dtype), mesh=_sc_mesh(),
        scratch_types=[pltpu.VMEM((2, K, G, W), y.dtype), pltpu.SemaphoreType.DMA((2,))],
        compiler_params=pltpu.CompilerParams(needs_layout_passes=False))
    def gather(y_hbm, i_hbm, w_hbm, o_hbm, rows2, sems):
        def body(i_vmem, w_vmem, o_vmem):
            def fetch(batch, slot):
                return [pltpu.make_async_copy(y_hbm.at[i_vmem.at[k, pl.ds(batch * G, G)]], rows2.at[slot, k],
                                              sems.at[slot]) for k in range(K)]

            for c in fetch(0, 0):
                c.start()

            @pl.loop(0, batches)
            def _(batch):
                slot = batch % 2

                @pl.when(batch + 1 < batches)
                def _():
                    for c in fetch(batch + 1, 1 - slot):
                        c.start()

                for c in fetch(batch, slot):
                    c.wait()
                rows = rows2.at[slot]

                @pl.loop(0, G)
                def _(t):
                    tok = jnp.full((L,), batch * G + t, jnp.int32)
                    wk = [plsc.load_gather(w_vmem, [jnp.full((L,), k, jnp.int32), tok]) for k in range(K)]

                    @plsc.parallel_loop(0, W // L, unroll=W // L)
                    def _(j):
                        lo = jnp.zeros((L,), F32)
                        hi = jnp.zeros((L,), F32)
                        for k in range(K):
                            pair = plsc.bitcast(rows[k, t, pl.ds(j * L, L)], BF16)
                            a, b = plsc.unpack(pair, format=plsc.PackFormat.INTERLEAVED)
                            lo = lo + wk[k] * a
                            hi = hi + wk[k] * b
                        o_vmem[batch * G + t, pl.ds(j * L, L)] = plsc.bitcast(
                            plsc.pack(lo, hi, format=plsc.PackFormat.INTERLEAVED), y.dtype)

        pltpu.emit_pipeline(
            body,
            grid=(N // SC_WINDOW,),
            in_specs=[pl.BlockSpec((K, SC_WINDOW), lambda i: (0, i)),
                      pl.BlockSpec((K, SC_WINDOW), lambda i: (0, i))],
            out_specs=[pl.BlockSpec((SC_WINDOW, W), lambda i: (i, 0))],
            core_axis_name=("c", "s"),
            dimension_semantics=(pltpu.PARALLEL,),
        )(i_hbm, w_hbm, o_hbm)

    return gather(y, dest, wts)


EXPERT_INPUT_SLOTS = 3


def _expert_kernel(blk_e_ref, nvalid_ref, parity_ref, next_e_ref, xa_hbm, xb_hbm, w1_hbm, w3_hbm, w2_hbm,
                   ya_ref, yb_ref, w1_sc, w3_sc, w2_sc, xa_buf, xb_buf, sems, w1_st, w3_st, w2_st, wsems,
                   *, layer):
    b = pl.program_id(0)
    nb = pl.num_programs(0)
    nv = nvalid_ref[b]
    prev_e = blk_e_ref[jnp.maximum(b - 1, 0)]
    blk = xa_buf.shape[1]
    ring = EXPERT_INPUT_SLOTS
    n_experts = w1_hbm.shape[1]

    def fetch_weights(e, par):
        return (pltpu.make_async_copy(w1_hbm.at[layer, e], w1_st.at[par], wsems.at[par, 0]),
                pltpu.make_async_copy(w3_hbm.at[layer, e], w3_st.at[par], wsems.at[par, 1]),
                pltpu.make_async_copy(w2_hbm.at[layer, e], w2_st.at[par], wsems.at[par, 2]))

    def fetch(block, slot):
        rows = pl.ds(pl.multiple_of(block * blk, blk), blk)
        return (pltpu.make_async_copy(xa_hbm.at[rows], xa_buf.at[slot], sems.at[slot, 0]),
                pltpu.make_async_copy(xb_hbm.at[rows], xb_buf.at[slot], sems.at[slot, 1]))

    @pl.when(b == 0)
    def _():
        for i in range(ring - 1):
            for c in fetch(i, i):
                c.start()

    ahead = b + (ring - 1)

    @pl.when(ahead < nb)
    def _():
        for c in fetch(ahead, ahead % ring):
            c.start()

    slot = b % ring
    for c in fetch(b, slot):
        c.wait()

    @pl.when(b == 0)
    def _():
        for c in fetch_weights(blk_e_ref[0], 0):
            c.start()

    @pl.when((b == 0) | (blk_e_ref[b] != prev_e))
    def _():
        par = parity_ref[b]
        nxt = next_e_ref[b]

        @pl.when(nxt < n_experts)
        def _():
            for c in fetch_weights(nxt, 1 - par):
                c.start()

        for c in fetch_weights(blk_e_ref[b], par):
            c.wait()
        w1_sc[...] = w1_st[par].astype(BF16)
        w3_sc[...] = w3_st[par].astype(BF16)
        w2_sc[...] = w2_st[par].astype(BF16)

    @pl.when(nv > 0)
    def _():
        x = _unpack_row_halves(xa_buf[slot], xb_buf[slot])
        rows = lax.broadcasted_iota(jnp.int32, x.shape, 0)
        x = jnp.where(rows < nv, x, 0.0).astype(BF16)
        hid = _silu(jnp.dot(x, w1_sc[...], preferred_element_type=F32)) * jnp.dot(
            x, w3_sc[...], preferred_element_type=F32)
        y = jnp.dot(hid.astype(BF16), w2_sc[...], preferred_element_type=F32)
        ya_ref[...], yb_ref[...] = _pack_row_halves(y)

    @pl.when(nv == 0)
    def _():
        ya_ref[...] = jnp.zeros(ya_ref.shape, ya_ref.dtype)
        yb_ref[...] = jnp.zeros(yb_ref.shape, yb_ref.dtype)


def routed_experts(xa, xb, tables, w1, w3, w2, layer):
    P = xa.shape[0]
    blk = SLOT_BLOCK
    _, E, D, FF = w1.shape
    slots = lambda: pl.BlockSpec((blk, PACK_W), lambda b, *_: (b, 0))
    grid_spec = pltpu.PrefetchScalarGridSpec(
        num_scalar_prefetch=len(tables),
        grid=(P // blk,),
        in_specs=[pl.BlockSpec(memory_space=pl.ANY)] * 5,
        out_specs=[slots(), slots()],
        scratch_shapes=[
            pltpu.VMEM((D, FF), BF16), pltpu.VMEM((D, FF), BF16), pltpu.VMEM((FF, D), BF16),
            pltpu.VMEM((EXPERT_INPUT_SLOTS, blk, PACK_W), jnp.int32),
            pltpu.VMEM((EXPERT_INPUT_SLOTS, blk, PACK_W), jnp.int32),
            pltpu.SemaphoreType.DMA((EXPERT_INPUT_SLOTS, 2)),
            pltpu.VMEM((2, D, FF), F32), pltpu.VMEM((2, D, FF), F32), pltpu.VMEM((2, FF, D), F32),
            pltpu.SemaphoreType.DMA((2, 3)),
        ],
    )
    return pl.pallas_call(
        functools.partial(_expert_kernel, layer=layer),
        grid_spec=grid_spec,
        out_shape=[jax.ShapeDtypeStruct((P, PACK_W), jnp.int32)] * 2,
        compiler_params=_cp(("arbitrary",), VMEM_LIMIT),
        name="routed_experts",
    )(*tables, xa, xb, w1, w3, w2)


def _combine_kernel(xmid_ref, ra_ref, rb_ref, mod2_ref, fg_ref, *rest):
    out_ref = rest[-1]
    D = xmid_ref.shape[1]
    x = xmid_ref[...] + mod2_ref[0][:, 2 * D:] * _unpack_row_halves(ra_ref[...], rb_ref[...])
    out_ref[...] = x * lax.rsqrt(jnp.mean(x * x, axis=-1, keepdims=True) + EPS) * fg_ref[...]


def combine(xmid, ra, rb, mod2, final_g, seq, out_rows=None, row0=0, out_buf=None):
    N, D = xmid.shape
    tm = 512
    tpb = seq // tm
    tile0 = row0 // tm
    in_specs = [
        pl.BlockSpec((tm, D), lambda i: (i, 0)),
        pl.BlockSpec((tm, PACK_W), lambda i: (i, 0)),
        pl.BlockSpec((tm, PACK_W), lambda i: (i, 0)),
        pl.BlockSpec((1, 1, 3 * D), lambda i: (i // tpb, 0, 0)),
        pl.BlockSpec((1, D), lambda i: (0, 0)),
    ]
    args = [xmid, ra, rb, mod2, final_g.reshape(1, D)]
    aliases = {}
    if out_buf is not None:
        in_specs.append(pl.BlockSpec(memory_space=pl.ANY))
        args.append(out_buf)
        aliases = {len(args) - 1: 0}
    return pl.pallas_call(
        _combine_kernel,
        grid=(N // tm,),
        in_specs=in_specs,
        out_specs=pl.BlockSpec((tm, D), lambda i: (i + tile0, 0)),
        out_shape=jax.ShapeDtypeStruct((out_rows or N, D), F32),
        input_output_aliases=aliases,
        compiler_params=_cp(("parallel",), VMEM_LIMIT),
        name="combine",
    )(*args)


TOKEN_STREAMS = 2


def _permute_w_in(w):
    ub = w[:, 3 * DA:3 * DA + DB]
    lat_lo = 3 * DA + DB
    lat_hi = lat_lo + Q_LORA + KV_LORA + QK_ROPE
    lat, gates = w[:, lat_lo:lat_hi], w[:, lat_hi:]
    pad = jnp.zeros((w.shape[0], LAT_W - (lat_hi - lat_lo)), w.dtype)
    parts = [gates, ub, lat, pad]
    for g in range(len(DIL_GROUPS)):
        sl = slice(g * GROUP_W, (g + 1) * GROUP_W)
        parts += [w[:, :DA][:, sl] * (HEAD_DIM_A ** -0.5), w[:, DA:2 * DA][:, sl], w[:, 2 * DA:3 * DA][:, sl]]
    return jnp.concatenate(parts, axis=1).astype(BF16)


def kernel(x, c, positions, ada_mix_w, ada_mix_b, norm_mix_g, w_in, pool_w, pool_scale, cq_norm_g, ckv_norm_g, w_uq, w_ukv, w_oa, w_ob, w_oc, w_out, ada_ffn_w, ada_ffn_b, norm_ffn_g, router_w, router_bias, exp_w1, exp_w3, exp_w2, sh_w1, sh_w3, sh_w2, final_g):
    B, S, D = x.shape
    depth = w_in.shape[0]
    mod_mix = adaln_rows(c, ada_mix_w, ada_mix_b)
    mod_ffn = adaln_rows(c, ada_ffn_w, ada_ffn_b)
    streams = TOKEN_STREAMS if B % TOKEN_STREAMS == 0 else 1
    Bs = B // streams
    Ns = Bs * S
    x_all = x.reshape(B * S, D)
    xs = [None] * streams
    out_all = None
    pos_s = [positions[s * Bs:(s + 1) * Bs] for s in range(streams)]
    for l in range(depth):
        last = l == depth - 1
        w_in_l = _permute_w_in(w_in[l])
        mla_w = _mla_weights(cq_norm_g[l], ckv_norm_g[l], w_uq[l], w_ukv[l])
        mix_w = (norm_ffn_g[l], pool_w[l].astype(BF16), pool_scale[l],
                 w_oa[l].astype(BF16), w_ob[l].astype(BF16), w_oc[l].astype(BF16), w_out[l].astype(BF16),
                 router_w[l].T.astype(BF16), sh_w1[l].astype(BF16), sh_w3[l].astype(BF16), sh_w2[l].astype(BF16))
        for s in range(streams):
            mod1 = mod_mix[l, s * Bs:(s + 1) * Bs].reshape(Bs, 1, 3 * D)
            mod2 = mod_ffn[l, s * Bs:(s + 1) * Bs].reshape(Bs, 1, 3 * D)
            if l == 0:
                x2, row0 = x_all, s * Ns
                gu, lat, *qkv = in_projection(x2, norm_mix_g[l], mod1, w_in_l, S, row0)
            else:
                row0 = 0
                x2, gu, lat, *qkv = in_projection(xs[s][0], norm_mix_g[l], mod1, w_in_l, S, 0, xs[s][1:])
            dil = [dilated_attention(qkv[2 * g], qkv[2 * g + 1]) for g in range(len(DIL_GROUPS))]
            q_all, k_all, vt_all = mla_prep(lat, pos_s[s], *mla_w, Bs, S)
            yc = mla_attention(q_all, k_all, vt_all, Bs, S)
            xmid, h2a, h2b, logits_t = mix_out(x2, gu, dil, yc, mod1, mod2, *mix_w, S, row0)
            dest, w_k, counts = route(logits_t, router_bias[l])
            tables = block_tables(counts, Ns)
            n_slots = tables[0].shape[0] * SLOT_BLOCK
            xa = sc_scatter_rows(h2a, dest, n_slots)
            xb = sc_scatter_rows(h2b, dest, n_slots)
            ya, yb = routed_experts(xa, xb, tables, exp_w1, exp_w3, exp_w2, l)
            ra = sc_weighted_gather(ya, dest, w_k)
            rb = sc_weighted_gather(yb, dest, w_k)
            if last:
                out_all = combine(xmid, ra, rb, mod2, final_g, S, B * S, s * Ns, out_all)
            else:
                xs[s] = (xmid, ra, rb, mod2)
    return out_all.reshape(B, S, D)
```

```python
import functools
import math

import jax
import jax.numpy as jnp
from jax import lax
from jax.experimental import pallas as pl
from jax.experimental.pallas import tpu as pltpu
from jax.experimental.pallas import tpu_sc as plsc

F32 = jnp.float32
BF16 = jnp.bfloat16
HIGHEST = lax.Precision.HIGHEST

D_MODEL = 1024
HEAD_DIM_A = 64
HEADS_PER_GROUP_A = 4
DIL_GROUPS = ((128, 1), (512, 4), (2048, 16))
GROUP_W = HEADS_PER_GROUP_A * HEAD_DIM_A
DA = GROUP_W * len(DIL_GROUPS)
POOL_WINDOWS = (2, 4, 8, 16)
POOL_GROUP_DIM = 128
DB = POOL_GROUP_DIM * len(POOL_WINDOWS)
POOL_HALO = 16
N_HEADS_C = 8
QK_NOPE = 64
QK_ROPE = 32
V_DIM = 64
Q_LORA = 384
KV_LORA = 256
DC = N_HEADS_C * V_DIM
HEAD_PAD_C = 128
ROPE_THETA = 10000.0
N_EXPERTS = 64
TOP_K = 8
N_GROUPS = 8
TOPK_GROUPS = 4
GROUP_SIZE = N_EXPERTS // N_GROUPS
ROUTED_SCALE = 2.5
EPS = 1e-6
NEG = -1e30
Q_BLOCK = 128

LAT_W = 768
GU_W = 3 * D_MODEL + DB
IN_OUT_WIDTHS = (GU_W, LAT_W) + (2 * GROUP_W, GROUP_W) * len(DIL_GROUPS)

VMEM_LIMIT = 56 * 1024 * 1024


def _cp(sem, vmem=None):
    return pltpu.CompilerParams(dimension_semantics=sem, vmem_limit_bytes=vmem)


def _silu(v):
    return v * jax.nn.sigmoid(v)


def _nt_dot(a, b):
    return lax.dot_general(a, b, (((1,), (1,)), ((), ())), preferred_element_type=F32)


PACK_W = D_MODEL // 4
_HI_MASK = -65536


def _bf16_bits(v):
    return lax.bitcast_convert_type(v.astype(BF16).astype(F32), jnp.int32)


def _pack_row_halves(v):
    halves = []
    for h in range(2):
        lo = _bf16_bits(v[:, (2 * h) * PACK_W:(2 * h + 1) * PACK_W])
        hi = _bf16_bits(v[:, (2 * h + 1) * PACK_W:(2 * h + 2) * PACK_W])
        halves.append(lax.shift_right_logical(lo, 16) | (hi & _HI_MASK))
    return halves


def _unpack_row_halves(wa, wb):
    parts = []
    for w in (wa, wb):
        parts.append(lax.bitcast_convert_type(lax.shift_left(w, 16), F32))
        parts.append(lax.bitcast_convert_type(w & _HI_MASK, F32))
    return jnp.concatenate(parts, axis=1)


def _adaln_kernel(c_ref, w_ref, b_ref, o_ref):
    s = _silu(c_ref[...])
    o_ref[0] = jnp.dot(s, w_ref[0], preferred_element_type=F32, precision=HIGHEST) + b_ref[0]


def adaln_rows(c, w, b):
    L, D, D3 = w.shape
    B = c.shape[0]
    tn = 1024
    return pl.pallas_call(
        _adaln_kernel,
        grid=(L, D3 // tn),
        in_specs=[
            pl.BlockSpec((B, D), lambda l, j: (0, 0)),
            pl.BlockSpec((1, D, tn), lambda l, j: (l, 0, j)),
            pl.BlockSpec((1, 1, tn), lambda l, j: (l, 0, j)),
        ],
        out_specs=pl.BlockSpec((1, B, tn), lambda l, j: (l, 0, j)),
        out_shape=jax.ShapeDtypeStruct((L, B, D3), F32),
        compiler_params=_cp(("parallel", "parallel")),
        name="adaln_rows",
    )(c, w, b.reshape(L, 1, D3))


LANES = 128


def _inproj_kernel(x_ref, g_ref, mod_ref, w_ref, *refs, chunk, pending):
    o_refs, scr = refs[:-1], refs[-1]
    D = x_ref.shape[1]
    x = x_ref[...]
    if pending:
        ra_ref, rb_ref, gate_ref, x_out_ref, *o_refs = o_refs
        x = x + gate_ref[0][:, 2 * D:] * _unpack_row_halves(ra_ref[...], rb_ref[...])
        x_out_ref[...] = x
    y = x * lax.rsqrt(jnp.mean(x * x, axis=-1, keepdims=True) + EPS) * g_ref[...]
    mod = mod_ref[0]
    h = (y * (1.0 + mod[:, D:2 * D]) + mod[:, :D]).astype(BF16)
    col = 0
    for o_ref in o_refs:
        width = o_ref.shape[-1]
        if o_ref.ndim == 2:
            for c0 in range(0, width, chunk):
                cw = min(chunk, width - c0)
                o_ref[:, c0:c0 + cw] = jnp.dot(
                    h, w_ref[:, col + c0:col + c0 + cw], preferred_element_type=F32).astype(o_ref.dtype)
        else:
            dil, rows = o_ref.shape[1], o_ref.shape[2]
            z = jnp.dot(h, w_ref[:, col:col + width], preferred_element_type=F32)
            if dil == 1:
                o_ref[0, 0] = z.astype(o_ref.dtype)
            else:
                for c in range(width // LANES):
                    scr[c] = z[:, c * LANES:(c + 1) * LANES]
                for r in range(dil):
                    o_ref[0, r] = jnp.concatenate(
                        [scr[c, pl.ds(r, rows, stride=dil), :] for c in range(width // LANES)],
                        axis=1).astype(o_ref.dtype)
        col += width


def in_projection(x2, g, mod, w, seq, row0=0, pending=None):
    D = x2.shape[1]
    B = mod.shape[0]
    N = B * seq
    tm = 512
    tpb = seq // tm
    tile0 = row0 // tm
    out_specs = [pl.BlockSpec((tm, wd), lambda i: (i, 0)) for wd in IN_OUT_WIDTHS[:2]]
    out_shape = [jax.ShapeDtypeStruct((N, wd), BF16) for wd in IN_OUT_WIDTHS[:2]]
    for grp, (_, dil) in enumerate(DIL_GROUPS):
        for wd in IN_OUT_WIDTHS[2 + 2 * grp:4 + 2 * grp]:
            out_specs.append(pl.BlockSpec((1, dil, tm // dil, wd), lambda i: (i // tpb, 0, i % tpb, 0)))
            out_shape.append(jax.ShapeDtypeStruct((B, dil, seq // dil, wd), BF16))
    in_specs = [
        pl.BlockSpec((tm, D), lambda i: (i + tile0, 0)),
        pl.BlockSpec((1, D), lambda i: (0, 0)),
        pl.BlockSpec((1, 1, 3 * D), lambda i: (i // tpb, 0, 0)),
        pl.BlockSpec(w.shape, lambda i: (0, 0), pipeline_mode=pl.Buffered(1)),
    ]
    args = [x2, g.reshape(1, D), mod, w]
    if pending is not None:
        in_specs += [pl.BlockSpec((tm, PACK_W), lambda i: (i, 0)), pl.BlockSpec((tm, PACK_W), lambda i: (i, 0)),
                     pl.BlockSpec((1, 1, 3 * D), lambda i: (i // tpb, 0, 0))]
        args += list(pending)
        out_specs.insert(0, pl.BlockSpec((tm, D), lambda i: (i, 0)))
        out_shape.insert(0, jax.ShapeDtypeStruct((N, D), F32))
    return pl.pallas_call(
        functools.partial(_inproj_kernel, chunk=512, pending=pending is not None),
        grid=(N // tm,),
        in_specs=in_specs,
        out_specs=out_specs,
        out_shape=out_shape,
        scratch_shapes=[pltpu.VMEM((max(IN_OUT_WIDTHS[2:]) // LANES, tm, LANES), F32)],
        compiler_params=_cp(("parallel",), VMEM_LIMIT),
        name="in_projection",
    )(*args)


def _dilated_kernel(q_ref, kc_ref, kp_ref, vc_ref, vp_ref, o_ref, lse_ref):
    i = pl.program_id(1)
    T = Q_BLOCK
    key = lax.broadcasted_iota(jnp.int32, (T, T), 0)
    qry = lax.broadcasted_iota(jnp.int32, (T, T), 1)
    valid_c = key <= qry
    near = key >= qry
    seqs, run = q_ref.shape[0], q_ref.shape[1] // T
    heads = [slice(h * HEAD_DIM_A, (h + 1) * HEAD_DIM_A) for h in range(HEADS_PER_GROUP_A)]

    def transposed(v):
        return v.astype(F32).T.astype(BF16)

    vts = {(s, j): transposed(vc_ref[s, j * T:(j + 1) * T, :]) for s in range(seqs) for j in range(run)}
    vt_before = [transposed(vp_ref[s]) for s in range(seqs)]

    def blocks(s, j):
        rows = slice(j * T, (j + 1) * T)
        if j == 0:
            return rows, kc_ref[s, rows, :], vts[s, 0], kp_ref[s], vt_before[s], near & (i > 0)
        before = slice((j - 1) * T, j * T)
        return rows, kc_ref[s, rows, :], vts[s, j], kc_ref[s, before, :], vts[s, j - 1], near

    scores, probs = {}, {}
    for s in range(seqs):
        for j in range(run):
            rows, kc, _, kp, _, valid_p = blocks(s, j)
            q = q_ref[s, rows, :]
            for h, sl in enumerate(heads):
                qh = q[:, sl]
                scores[s, j, h] = (jnp.where(valid_c, _nt_dot(kc[:, sl], qh), NEG),
                                   jnp.where(valid_p, _nt_dot(kp[:, sl], qh), NEG))
    for chain, (sc, sp) in scores.items():
        m = jnp.maximum(jnp.max(sc, axis=0, keepdims=True), jnp.max(sp, axis=0, keepdims=True))
        pc = jnp.exp(sc - m)
        pp = jnp.exp(sp - m)
        den = jnp.sum(pc, axis=0, keepdims=True) + jnp.sum(pp, axis=0, keepdims=True)
        probs[chain] = (pc.astype(BF16), pp.astype(BF16), den, m + jnp.log(den))
    spread = LSE_LANES // len(heads)
    for s in range(seqs):
        for j in range(run):
            rows, _, vtc, _, vtp, _ = blocks(s, j)
            outs = []
            for h, sl in enumerate(heads):
                pc, pp, den, _ = probs[s, j, h]
                o = (jnp.dot(vtc[sl, :], pc, preferred_element_type=F32)
                     + jnp.dot(vtp[sl, :], pp, preferred_element_type=F32))
                outs.append(o / den)
            o_ref[s, rows, :] = jnp.concatenate(outs, axis=0).T.astype(o_ref.dtype)
            lse_t = jnp.concatenate(
                [jnp.broadcast_to(probs[s, j, h][3], (spread, T)) for h in range(len(heads))], axis=0)
            lse_ref[s, rows, :] = lse_t.T


DILATED_RUN = 16


LSE_LANES = 128


def dilated_attention(qk, v):
    batch, dilation, L, _ = qk.shape
    nb = L // Q_BLOCK
    run = min(DILATED_RUN, nb)
    seqs = DILATED_RUN // run
    qk_r = qk.reshape(batch * dilation, L, 2 * GROUP_W)
    v_r = v.reshape(batch * dilation, L, GROUP_W)
    before = lambda i: jnp.maximum(i * run - 1, 0)
    o, lse = pl.pallas_call(
        _dilated_kernel,
        grid=(batch * dilation // seqs, nb // run),
        in_specs=[
            pl.BlockSpec((seqs, run * Q_BLOCK, GROUP_W), lambda s, i: (s, i, 0)),
            pl.BlockSpec((seqs, run * Q_BLOCK, GROUP_W), lambda s, i: (s, i, 1)),
            pl.BlockSpec((seqs, Q_BLOCK, GROUP_W), lambda s, i: (s, before(i), 1)),
            pl.BlockSpec((seqs, run * Q_BLOCK, GROUP_W), lambda s, i: (s, i, 0)),
            pl.BlockSpec((seqs, Q_BLOCK, GROUP_W), lambda s, i: (s, before(i), 0)),
        ],
        out_specs=[
            pl.BlockSpec((seqs, run * Q_BLOCK, GROUP_W), lambda s, i: (s, i, 0)),
            pl.BlockSpec((seqs, run * Q_BLOCK, LSE_LANES), lambda s, i: (s, i, 0)),
        ],
        out_shape=[
            jax.ShapeDtypeStruct((batch * dilation, L, GROUP_W), BF16),
            jax.ShapeDtypeStruct((batch * dilation, L, LSE_LANES), F32),
        ],
        compiler_params=_cp(("parallel", "parallel")),
        name=f"dilated_attention_d{dilation}",
    )(qk_r, qk_r, qk_r, v_r, v_r)
    return o.reshape(batch, dilation, L, GROUP_W), lse.reshape(batch, dilation, L, LSE_LANES)


def _mla_prep_kernel(lat_ref, pos_ref, gq_ref, gkv_ref, wq_ref, wk_ref, wvt_ref, freq_ref, spread_ref, one_ref,
                     q_ref, k_ref, vt_ref):
    HP = N_HEADS_C * HEAD_PAD_C
    lat = lat_ref[...].astype(F32)
    cq = lat[:, :Q_LORA]
    ckr = lat[:, Q_LORA:]
    zq = (cq * lax.rsqrt(jnp.mean(cq * cq, axis=-1, keepdims=True) + EPS) * gq_ref[...]).astype(BF16)
    lane = lax.broadcasted_iota(jnp.int32, ckr.shape, 1)
    is_kv = lane < KV_LORA
    ms = jnp.sum(jnp.where(is_kv, ckr * ckr, 0.0), axis=-1, keepdims=True) * (1.0 / KV_LORA)
    zkv = (ckr * jnp.where(is_kv, lax.rsqrt(ms + EPS) * gkv_ref[...], 1.0)).astype(BF16)
    qq = jnp.dot(zq, wq_ref[...], preferred_element_type=F32)
    kk = jnp.dot(zkv, wk_ref[:, :HP], preferred_element_type=F32)
    kk_sw = jnp.dot(zkv[:, KV_LORA:], wk_ref[KV_LORA:, HP:], preferred_element_type=F32)
    ang_t = freq_ref[...] * pos_ref[0].astype(F32)

    def to_lanes(t):
        hi = t.astype(BF16)
        lo = (t - hi.astype(F32)).astype(BF16)
        tn_dot = lambda a: lax.dot_general(a, spread_ref[...], (((0,), (0,)), ((), ())), preferred_element_type=F32)
        return tn_dot(hi) + tn_dot(lo)

    cos = to_lanes(jnp.cos(ang_t)) + one_ref[...]
    sin = to_lanes(jnp.sin(ang_t))
    slot_lane = lax.broadcasted_iota(jnp.int32, (1, HEAD_PAD_C), 1)
    half = QK_ROPE // 2
    sin_x1 = jnp.where((slot_lane >= QK_NOPE) & (slot_lane < QK_NOPE + half), -sin, 0.0)
    sin_x2 = jnp.where((slot_lane >= QK_NOPE + half) & (slot_lane < QK_NOPE + QK_ROPE), sin, 0.0)
    for h in range(N_HEADS_C):
        lo, hi = h * HEAD_PAD_C, (h + 1) * HEAD_PAD_C
        qh = qq[:, lo:hi]
        q_ref[:, lo:hi] = (qh * cos + pltpu.roll(qh, HEAD_PAD_C - half, axis=1) * sin_x1
                           + pltpu.roll(qh, half, axis=1) * sin_x2).astype(q_ref.dtype)
        k_ref[:, lo:hi] = (kk[:, lo:hi] * cos + kk_sw[:, lo:hi] * sin).astype(k_ref.dtype)
    vt_ref[0] = _nt_dot(wvt_ref[...], zkv).astype(vt_ref.dtype)


def _mla_weights(cq_g, ckv_g, w_uq, w_ukv):
    H, HPAD, half = N_HEADS_C, HEAD_PAD_C, QK_ROPE // 2
    scale = (QK_NOPE + QK_ROPE) ** -0.5 * math.log2(math.e)
    wq = w_uq.reshape(Q_LORA, H, QK_NOPE + QK_ROPE) * scale
    wq_big = jnp.pad(wq, ((0, 0), (0, 0), (0, HPAD - QK_NOPE - QK_ROPE))).reshape(Q_LORA, H * HPAD)

    rows = LAT_W - Q_LORA
    wkv = w_ukv.reshape(KV_LORA, H, QK_NOPE + V_DIM)
    eye = jnp.eye(QK_ROPE, dtype=F32)
    k_lin = jnp.zeros((rows, H, HPAD), F32)
    k_lin = k_lin.at[:KV_LORA, :, :QK_NOPE].set(wkv[..., :QK_NOPE])
    k_lin = k_lin.at[KV_LORA:KV_LORA + QK_ROPE, :, QK_NOPE:QK_NOPE + QK_ROPE].set(
        jnp.broadcast_to(eye[:, None, :], (QK_ROPE, H, QK_ROPE)))
    swap = jnp.zeros((QK_ROPE, QK_ROPE), F32).at[half:, :half].set(-jnp.eye(half)).at[:half, half:].set(jnp.eye(half))
    k_sw = jnp.zeros((rows, H, HPAD), F32)
    k_sw = k_sw.at[KV_LORA:KV_LORA + QK_ROPE, :, QK_NOPE:QK_NOPE + QK_ROPE].set(
        jnp.broadcast_to(swap[:, None, :], (QK_ROPE, H, QK_ROPE)))
    v_w = jnp.zeros((rows, H, V_DIM), F32).at[:KV_LORA].set(wkv[..., QK_NOPE:])
    wk_big = jnp.concatenate([k_lin.reshape(rows, H * HPAD), k_sw.reshape(rows, H * HPAD)], axis=1)
    wv_t = v_w.reshape(rows, H * V_DIM).T

    gkv = jnp.concatenate([ckv_g, jnp.ones((rows - KV_LORA,), F32)]).reshape(1, rows)
    return cq_g.reshape(1, Q_LORA), gkv, wq_big.astype(BF16), wk_big.astype(BF16), wv_t.astype(BF16)


def _rope_tables():
    half = QK_ROPE // 2
    freqs = (ROPE_THETA ** (-jnp.arange(0, QK_ROPE, 2, dtype=F32) / QK_ROPE)).reshape(half, 1)
    lane = jnp.arange(HEAD_PAD_C)[None, :]
    j = jnp.arange(half)[:, None]
    spread = (lane == QK_NOPE + j) | (lane == QK_NOPE + half + j)
    off_rope = ~jnp.any(spread, axis=0, keepdims=True)
    return freqs, spread.astype(BF16), off_rope.astype(F32)


def mla_prep(lat, positions, gq, gkv, wq_big, wk_big, wv_t, batch, seq):
    N = lat.shape[0]
    HP = N_HEADS_C * HEAD_PAD_C
    tm = 512
    tpb = seq // tm
    freqs, spread, off_rope = _rope_tables()
    pos_rows = positions.reshape(N // tm, 1, tm)
    const = lambda shape: pl.BlockSpec(shape, lambda i: (0, 0))
    return pl.pallas_call(
        _mla_prep_kernel,
        grid=(N // tm,),
        in_specs=[
            pl.BlockSpec((tm, LAT_W), lambda i: (i, 0)),
            pl.BlockSpec((1, 1, tm), lambda i: (i, 0, 0)),
            const(gq.shape), const(gkv.shape), const(wq_big.shape), const(wk_big.shape), const(wv_t.shape),
            const(freqs.shape), const(spread.shape), const(off_rope.shape),
        ],
        out_specs=[
            pl.BlockSpec((tm, HP), lambda i: (i, 0)),
            pl.BlockSpec((tm, HP), lambda i: (i, 0)),
            pl.BlockSpec((1, DC, tm), lambda i: (i // tpb, 0, i % tpb)),
        ],
        out_shape=[
            jax.ShapeDtypeStruct((N, HP), BF16),
            jax.ShapeDtypeStruct((N, HP), BF16),
            jax.ShapeDtypeStruct((batch, DC, seq), BF16),
        ],
        compiler_params=_cp(("parallel",), VMEM_LIMIT),
        name="mla_prep",
    )(lat, pos_rows, gq, gkv, wq_big, wk_big, wv_t, freqs, spread, off_rope)


HEADS_PER_STEP_C = 8
FLASH_Q_CHUNK = 256


def _mla_flash_kernel(qi_ref, ki_ref, q_ref, k_ref, vt_ref, o_ref, m_sc, l_sc, acc_sc):
    t = pl.program_id(2)
    qi, ki = qi_ref[t], ki_ref[t]

    @pl.when(ki == 0)
    def _():
        m_sc[...] = jnp.full(m_sc.shape, NEG, F32)
        l_sc[...] = jnp.zeros(l_sc.shape, F32)
        acc_sc[...] = jnp.zeros(acc_sc.shape, F32)

    def step(masked):
        T = q_ref.shape[1]
        if masked:
            key = lax.broadcasted_iota(jnp.int32, (T, T), 0)
            qry = lax.broadcasted_iota(jnp.int32, (T, T), 1)
            keep = key <= qry
        chains = [(h, c) for h in range(HEADS_PER_STEP_C) for c in range(T // FLASH_Q_CHUNK)]
        scores, probs, alphas = {}, {}, {}

        def keys_for(c):
            return (c + 1) * FLASH_Q_CHUNK if masked else T

        def qk(h, c):
            qs = slice(c * FLASH_Q_CHUNK, (c + 1) * FLASH_Q_CHUNK)
            q = q_ref[0, qs, h * HEAD_PAD_C:(h + 1) * HEAD_PAD_C]
            k = k_ref[0, :keys_for(c), h * HEAD_PAD_C:(h + 1) * HEAD_PAD_C]
            st = _nt_dot(k, q)
            scores[h, c] = jnp.where(keep[:keys_for(c), qs], st, NEG) if masked else st

        def softmax(h, c):
            qs = slice(c * FLASH_Q_CHUNK, (c + 1) * FLASH_Q_CHUNK)
            st = scores.pop((h, c))
            m_prev = m_sc[h, :, qs]
            m_new = jnp.maximum(m_prev, jnp.max(st, axis=0, keepdims=True))
            alpha = jnp.exp2(m_prev - m_new)
            p = jnp.exp2(st - m_new)
            l_sc[h, :, qs] = alpha * l_sc[h, :, qs] + jnp.sum(p, axis=0, keepdims=True)
            m_sc[h, :, qs] = m_new
            probs[h, c], alphas[h, c] = p.astype(BF16), alpha

        def pv(h, c):
            qs = slice(c * FLASH_Q_CHUNK, (c + 1) * FLASH_Q_CHUNK)
            vt = vt_ref[0, h * V_DIM:(h + 1) * V_DIM, :keys_for(c)]
            acc_sc[h, :, qs] = alphas.pop((h, c)) * acc_sc[h, :, qs] + jnp.dot(
                vt, probs.pop((h, c)), preferred_element_type=F32)

        for phase in (qk, softmax, pv):
            for ch in chains:
                phase(*ch)

    @pl.when(ki < qi)
    def _():
        step(False)

    @pl.when(ki == qi)
    def _():
        step(True)
        ot = jnp.concatenate([acc_sc[h] / l_sc[h] for h in range(HEADS_PER_STEP_C)], axis=0)
        o_ref[0] = ot.T.astype(o_ref.dtype)


def mla_attention(q_all, k_all, vt_all, batch, seq):
    T = 512
    nq = seq // T
    pairs = [(a, b) for a in range(nq) for b in range(a + 1)]
    qi_tab = jnp.asarray([p[0] for p in pairs], jnp.int32)
    ki_tab = jnp.asarray([p[1] for p in pairs], jnp.int32)
    hp = N_HEADS_C // HEADS_PER_STEP_C
    qw = HEADS_PER_STEP_C * HEAD_PAD_C
    vw = HEADS_PER_STEP_C * V_DIM
    q3 = q_all.reshape(batch, seq, -1)
    k3 = k_all.reshape(batch, seq, -1)
    grid_spec = pltpu.PrefetchScalarGridSpec(
        num_scalar_prefetch=2,
        grid=(batch, hp, len(pairs)),
        in_specs=[
            pl.BlockSpec((1, T, qw), lambda b, h, t, qi, ki: (b, qi[t], h)),
            pl.BlockSpec((1, T, qw), lambda b, h, t, qi, ki: (b, ki[t], h)),
            pl.BlockSpec((1, vw, T), lambda b, h, t, qi, ki: (b, h, ki[t])),
        ],
        out_specs=pl.BlockSpec((1, T, vw), lambda b, h, t, qi, ki: (b, qi[t], h)),
        scratch_shapes=[
            pltpu.VMEM((HEADS_PER_STEP_C, 1, T), F32),
            pltpu.VMEM((HEADS_PER_STEP_C, 1, T), F32),
            pltpu.VMEM((HEADS_PER_STEP_C, V_DIM, T), F32),
        ],
    )
    o = pl.pallas_call(
        _mla_flash_kernel,
        grid_spec=grid_spec,
        out_shape=jax.ShapeDtypeStruct((batch, seq, DC), BF16),
        compiler_params=_cp(("parallel", "parallel", "arbitrary")),
        name="mla_attention",
    )(qi_tab, ki_tab, q3, k3, vt_all)
    return o.reshape(batch * seq, DC)


def _mixout_kernel(x_ref, gates_ref, ub_ref, ubh_ref, o1_ref, o2_ref, o3_ref, l1_ref, l2_ref, l3_ref, yc_ref,
                   mod1_ref, mod2_ref, g2_ref, poolw_ref, pscale_ref, woa_ref, wob_ref, woc_ref, wout_ref,
                   rwt_ref, sw1_ref, sw3_ref, sw2_ref, spread_ref,
                   xmid_ref, h2a_ref, h2b_ref, logit_ref, *scratch, tiles_per_batch):
    D = x_ref.shape[1]
    tm = x_ref.shape[0]
    tile = pl.program_id(0) % tiles_per_batch
    o_scrs, l_scrs = scratch[:3], scratch[3:]

    def token_order(ref, scr):
        dil, rows, width = ref.shape[1:]
        if dil == 1:
            return ref[0, 0].astype(F32)
        for r in range(dil):
            v = ref[0, r].astype(F32)
            for c in range(width // LANES):
                scr[c, pl.ds(r, rows, stride=dil), :] = v[:, c * LANES:(c + 1) * LANES]
        return jnp.concatenate([scr[c] for c in range(width // LANES)], axis=1)

    outs = [token_order(r, s) for r, s in zip((o1_ref, o2_ref, o3_ref), o_scrs)]
    l1, l2, l3 = [token_order(r, s) for r, s in zip((l1_ref, l2_ref, l3_ref), l_scrs)]
    mx = jnp.maximum(jnp.maximum(l1, l2), l3)
    es = [jnp.exp(l1 - mx), jnp.exp(l2 - mx), jnp.exp(l3 - mx)]
    inv = 1.0 / (es[0] + es[1] + es[2])
    ya = jnp.zeros((tm, GROUP_W), F32)
    for e, o in zip(es, outs):
        w = e * inv
        w_hi = w.astype(BF16)
        w_lo = (w - w_hi.astype(F32)).astype(BF16)
        w_wide = (jnp.dot(w_hi, spread_ref[...], preferred_element_type=F32)
                  + jnp.dot(w_lo, spread_ref[...], preferred_element_type=F32))
        ya = ya + w_wide * o
    a_out = jnp.dot(ya.astype(BF16), woa_ref[...], preferred_element_type=F32)

    u = ub_ref[...].astype(F32)
    halo = jnp.where(tile > 0, ubh_ref[...].astype(F32), 0.0)
    ext = jnp.concatenate([halo, u], axis=0)
    t_seq = tile * tm + lax.broadcasted_iota(jnp.int32, (tm, 1), 0)
    pooled = []
    for gi, w in enumerate(POOL_WINDOWS):
        sl = slice(gi * POOL_GROUP_DIM, (gi + 1) * POOL_GROUP_DIM)
        acc = ext[:, sl]
        k = 1
        while k < w:
            acc = acc + pltpu.roll(acc, k, axis=0)
            k *= 2
        cnt = jnp.minimum(t_seq + 1, w).astype(F32)
        pg = acc[POOL_HALO:] / cnt - u[:, sl]
        pooled.append(jnp.dot(pg.astype(BF16), poolw_ref[gi], preferred_element_type=F32))
    yb = jnp.concatenate(pooled, axis=1) * pscale_ref[...]
    b_out = jnp.dot(yb.astype(BF16), wob_ref[...], preferred_element_type=F32)
    c_out = jnp.dot(yc_ref[...], woc_ref[...], preferred_element_type=F32)

    g = gates_ref[...].astype(F32)
    mix = (jax.nn.sigmoid(g[:, :D]) * a_out + jax.nn.sigmoid(g[:, D:2 * D]) * b_out
           + jax.nn.sigmoid(g[:, 2 * D:]) * c_out)
    tok = jnp.dot(mix.astype(BF16), wout_ref[...], preferred_element_type=F32)
    xn = x_ref[...] + mod1_ref[0][:, 2 * D:] * tok

    mod2 = mod2_ref[0]
    y = xn * lax.rsqrt(jnp.mean(xn * xn, axis=-1, keepdims=True) + EPS) * g2_ref[...]
    h2 = y * (1.0 + mod2[:, D:2 * D]) + mod2[:, :D]
    h2b = h2.astype(BF16)
    h2a_ref[...], h2b_ref[...] = _pack_row_halves(h2b)
    logit_ref[...] = _nt_dot(rwt_ref[...], h2b)
    hid = _silu(jnp.dot(h2b, sw1_ref[...], preferred_element_type=F32)) * jnp.dot(
        h2b, sw3_ref[...], preferred_element_type=F32)
    shared = jnp.dot(hid.astype(BF16), sw2_ref[...], preferred_element_type=F32)
    xmid_ref[...] = xn + mod2[:, 2 * D:] * shared


def mix_out(x2, gu, dil, yc, mod1, mod2, g2, pool_w, pool_scale, w_oa, w_ob, w_oc, w_out, rwt, sw1, sw3, sw2, seq,
            row0=0):
    D = x2.shape[1]
    N = gu.shape[0]
    tm = 512
    tpb = seq // tm
    tile0 = row0 // tm
    (o1, l1), (o2, l2), (o3, l3) = dil
    row = lambda w, c=0: pl.BlockSpec((tm, w), lambda i: (i, c))
    by_residue = lambda a: pl.BlockSpec(
        (1, a.shape[1], tm // a.shape[1], a.shape[3]), lambda i: (i // tpb, 0, i % tpb, 0))
    heads = HEADS_PER_GROUP_A
    spread = (jnp.arange(LSE_LANES)[:, None] == (jnp.arange(GROUP_W)[None, :] // HEAD_DIM_A) * (LSE_LANES // heads)
              ).astype(BF16)
    const2 = lambda a: pl.BlockSpec(a.shape, lambda i: (0,) * a.ndim, pipeline_mode=pl.Buffered(1))
    modspec = pl.BlockSpec((1, 1, 3 * D), lambda i: (i // tpb, 0, 0))
    ub_col = 3 * D // DB
    halo_spec = pl.BlockSpec(
        (POOL_HALO, DB), lambda i: (jnp.maximum(i * (tm // POOL_HALO) - 1, 0), ub_col))
    weights = [g2.reshape(1, D), pool_w, pool_scale.reshape(1, DB), w_oa, w_ob, w_oc, w_out, rwt, sw1, sw3, sw2,
               spread]
    return pl.pallas_call(
        functools.partial(_mixout_kernel, tiles_per_batch=tpb),
        grid=(N // tm,),
        in_specs=[
            pl.BlockSpec((tm, D), lambda i: (i + tile0, 0)), row(3 * D), row(DB, ub_col), halo_spec,
            by_residue(o1), by_residue(o2), by_residue(o3), by_residue(l1), by_residue(l2), by_residue(l3), row(DC),
            modspec, modspec,
        ] + [const2(a) for a in weights],
        scratch_shapes=[pltpu.VMEM((GROUP_W // LANES, tm, LANES), F32)] * 3
        + [pltpu.VMEM((LSE_LANES // LANES, tm, LANES), F32)] * 3,
        out_specs=[row(D), row(PACK_W), row(PACK_W), pl.BlockSpec((N_EXPERTS, tm), lambda i: (0, i))],
        out_shape=[
            jax.ShapeDtypeStruct((N, D), F32),
            jax.ShapeDtypeStruct((N, PACK_W), jnp.int32),
            jax.ShapeDtypeStruct((N, PACK_W), jnp.int32),
            jax.ShapeDtypeStruct((N_EXPERTS, N), F32),
        ],
        compiler_params=_cp(("parallel",), VMEM_LIMIT),
        name="mix_out",
    )(x2, gu, gu, gu, o1, o2, o3, l1, l2, l3, yc, mod1, mod2, *weights)


def _pick_rows(table, picks):
    G, GS = N_GROUPS, GROUP_SIZE
    eio = lax.broadcasted_iota(jnp.int32, (GS, table.shape[1]), 0)
    rows = []
    for k in range(TOP_K):
        idx = picks[k:k + 1]
        parts = [jnp.where(eio + g * GS == idx, table[g * GS:(g + 1) * GS], 0.0) for g in range(G)]
        rows.append(jnp.sum(functools.reduce(jnp.add, parts), axis=0, keepdims=True))
    return jnp.concatenate(rows, axis=0)


def _route_choose(lg_ref, bias_ref):
    G, GS = N_GROUPS, GROUP_SIZE
    scores = jax.nn.sigmoid(lg_ref[...])
    sel = scores + bias_ref[...]
    tn = sel.shape[1]
    eio = lax.broadcasted_iota(jnp.int32, (GS, tn), 0)
    ninf = -jnp.inf

    gs = []
    for g in range(G):
        v = sel[g * GS:(g + 1) * GS]
        m1 = jnp.max(v, axis=0, keepdims=True)
        i1 = jnp.min(jnp.where(v == m1, eio, GS), axis=0, keepdims=True)
        m2 = jnp.max(jnp.where(eio == i1, ninf, v), axis=0, keepdims=True)
        gs.append(m1 + m2)
    gsm = jnp.concatenate(gs, axis=0)
    gio = lax.broadcasted_iota(jnp.int32, (G, tn), 0)
    rank = jnp.zeros((G, tn), jnp.int32)
    for g2 in range(G):
        beats = (gs[g2] > gsm) | ((gs[g2] == gsm) & (g2 < gio))
        rank = rank + beats.astype(jnp.int32)
    gsel = rank < TOPK_GROUPS

    vs = [jnp.where(gsel[g:g + 1], sel[g * GS:(g + 1) * GS], NEG) for g in range(G)]
    eid = [eio + g * GS for g in range(G)]
    chosen = [jnp.zeros((GS, tn), jnp.bool_) for _ in range(G)]
    picks = []
    for _ in range(TOP_K):
        m = jnp.max(functools.reduce(jnp.maximum, vs), axis=0, keepdims=True)
        idx = jnp.min(functools.reduce(jnp.minimum, [jnp.where(v == m, e, N_EXPERTS) for v, e in zip(vs, eid)]),
                      axis=0, keepdims=True)
        picks.append(idx)
        for g in range(G):
            hit = eid[g] == idx
            chosen[g] = chosen[g] | hit
            vs[g] = jnp.where(hit, ninf, vs[g])
    mask = jnp.concatenate(chosen, axis=0).astype(F32)
    return scores, jnp.concatenate(picks, axis=0), mask


def _route_kernel(lg_ref, bias_ref, tri_ref, dest_ref, w_ref, cnt_ref, run_sc, start_sc, mask_sc, picks_sc,
                  *, slot_block):
    phase = pl.program_id(0)
    step = pl.program_id(1)
    tn = lg_ref.shape[1]
    cols = pl.ds(pl.multiple_of(step * tn, tn), tn)

    @pl.when(phase == 0)
    def _():
        @pl.when(step == 0)
        def _():
            run_sc[...] = jnp.zeros(run_sc.shape, F32)

        scores, picks, mask = _route_choose(lg_ref, bias_ref)
        wk = _pick_rows(scores, picks)
        w_ref[0] = wk / jnp.sum(wk, axis=0, keepdims=True) * ROUTED_SCALE
        dest_ref[0] = jnp.zeros(dest_ref.shape[1:], dest_ref.dtype)
        mask_sc[:, cols] = mask.astype(BF16)
        picks_sc[:, cols] = picks
        run_sc[...] = run_sc[...] + jnp.sum(mask, axis=1, keepdims=True)

    @pl.when(phase == 1)
    def _():
        @pl.when(step == 0)
        def _():
            counts = run_sc[...].astype(jnp.int32)
            cnt_ref[...] = jnp.broadcast_to(counts, cnt_ref.shape)
            shift = slot_block.bit_length() - 1
            padded = lax.shift_left(lax.shift_right_logical(counts + (slot_block - 1), shift), shift).astype(F32)
            r = lax.broadcasted_iota(jnp.int32, (N_EXPERTS, N_EXPERTS), 0)
            c = lax.broadcasted_iota(jnp.int32, (N_EXPERTS, N_EXPERTS), 1)
            as_row = jnp.sum(jnp.where(r == c, padded, 0.0), axis=0, keepdims=True)
            start_sc[...] = jnp.sum(jnp.where(c < r, as_row, 0.0), axis=1, keepdims=True)
            run_sc[...] = jnp.zeros(run_sc.shape, F32)

        mask_b = mask_sc[:, cols]
        mask = mask_b.astype(F32)
        before = jnp.dot(mask_b, tri_ref[...], preferred_element_type=F32) - mask
        slot = start_sc[...] + run_sc[...] + before
        dest_ref[0] = _pick_rows(slot, picks_sc[:, cols]).astype(jnp.int32)
        w_ref[0] = jnp.zeros(w_ref.shape[1:], w_ref.dtype)
        run_sc[...] = run_sc[...] + jnp.sum(mask, axis=1, keepdims=True)


SLOT_BLOCK = 512


def route(logits_t, bias):
    E, N = logits_t.shape
    tn = 1024
    tri = (jnp.arange(tn)[:, None] <= jnp.arange(tn)[None, :]).astype(BF16)
    plane = lambda: pl.BlockSpec((1, TOP_K, tn), lambda p, i: (p, 0, i))
    dest, w, cnt = pl.pallas_call(
        functools.partial(_route_kernel, slot_block=SLOT_BLOCK),
        grid=(2, N // tn),
        in_specs=[
            pl.BlockSpec((E, tn), lambda p, i: (0, i * (1 - p))),
            pl.BlockSpec((E, 1), lambda p, i: (0, 0)),
            pl.BlockSpec((tn, tn), lambda p, i: (0, 0)),
        ],
        out_specs=[plane(), plane(), pl.BlockSpec((E, 128), lambda p, i: (0, 0))],
        out_shape=[
            jax.ShapeDtypeStruct((2, TOP_K, N), jnp.int32),
            jax.ShapeDtypeStruct((2, TOP_K, N), F32),
            jax.ShapeDtypeStruct((E, 128), jnp.int32),
        ],
        scratch_shapes=[pltpu.VMEM((E, 1), F32), pltpu.VMEM((E, 1), F32),
                        pltpu.VMEM((E, N), BF16), pltpu.VMEM((TOP_K, N), jnp.int32)],
        compiler_params=_cp(("arbitrary", "arbitrary")),
        name="route",
    )(logits_t, bias.reshape(E, 1), tri)
    return dest[1], w[0], cnt[:, 0]


def block_tables(counts, n_tokens):
    E = counts.shape[0]
    blk = SLOT_BLOCK
    nblk = (n_tokens * TOP_K + E * blk) // blk
    per_expert = (counts + blk - 1) // blk
    bend = jnp.cumsum(per_expert)
    bstart = bend - per_expert
    b = jnp.arange(nblk, dtype=jnp.int32)[:, None]
    owns = (bstart[None, :] <= b) & (b < bend[None, :])
    blk_e = jnp.minimum(jnp.sum(bend[None, :] <= b, axis=1), E - 1).astype(jnp.int32)
    rows_left = counts[None, :] - (b - bstart[None, :]) * blk
    nvalid = jnp.sum(jnp.where(owns, jnp.clip(rows_left, 0, blk), 0), axis=1)
    first = jnp.concatenate([jnp.ones((1,), jnp.bool_), blk_e[1:] != blk_e[:-1]])
    run_parity = ((jnp.cumsum(first.astype(jnp.int32)) - 1) % 2).astype(jnp.int32)
    later = blk_e[None, :] > blk_e[:, None]
    next_e = jnp.min(jnp.where(later, blk_e[None, :], E), axis=1).astype(jnp.int32)
    return blk_e, nvalid.astype(jnp.int32), run_parity, next_e


def _sc_mesh():
    return plsc.VectorSubcoreMesh(core_axis_name="c", subcore_axis_name="s")


SC_WINDOW = 128


def sc_scatter_rows(x, dest, n_slots):
    N, W = x.shape
    K = dest.shape[0]

    @functools.partial(pl.kernel, out_type=jax.ShapeDtypeStruct((n_slots, W), x.dtype), mesh=_sc_mesh(),
                       scratch_types=[])
    def scatter(x_hbm, i_hbm, o_hbm):
        def body(x_vmem, i_vmem):
            for k in range(K):
                pltpu.sync_copy(x_vmem, o_hbm.at[i_vmem.at[k]])

        pltpu.emit_pipeline(
            body,
            grid=(N // SC_WINDOW,),
            in_specs=[pl.BlockSpec((SC_WINDOW, W), lambda i: (i, 0)),
                      pl.BlockSpec((K, SC_WINDOW), lambda i: (0, i))],
            out_specs=[],
            core_axis_name=("c", "s"),
            dimension_semantics=(pltpu.PARALLEL,),
        )(x_hbm, i_hbm)

    return scatter(x, dest)


SC_LANES = 16
SC_GATHER_TOKENS = 8


def sc_weighted_gather(y, dest, wts):
    W = y.shape[1]
    K, N = dest.shape
    G, L = SC_GATHER_TOKENS, SC_LANES
    batches = SC_WINDOW // G

    @functools.partial(
        pl.kernel, out_type=jax.ShapeDtypeStruct((N, W), y.dtype), mesh=_sc_mesh(),
        scratch_types=[pltpu.VMEM((2, K, G, W), y.dtype), pltpu.SemaphoreType.DMA((2,))],
        compiler_params=pltpu.CompilerParams(needs_layout_passes=False))
    def gather(y_hbm, i_hbm, w_hbm, o_hbm, rows2, sems):
        def body(i_vmem, w_vmem, o_vmem):
            def fetch(batch, slot):
                return [pltpu.make_async_copy(y_hbm.at[i_vmem.at[k, pl.ds(batch * G, G)]], rows2.at[slot, k],
                                              sems.at[slot]) for k in range(K)]

            for c in fetch(0, 0):
                c.start()

            @pl.loop(0, batches)
            def _(batch):
                slot = batch % 2

                @pl.when(batch + 1 < batches)
                def _():
                    for c in fetch(batch + 1, 1 - slot):
                        c.start()

                for c in fetch(batch, slot):
                    c.wait()
                rows = rows2.at[slot]

                @pl.loop(0, G)
                def _(t):
                    tok = jnp.full((L,), batch * G + t, jnp.int32)
                    wk = [plsc.load_gather(w_vmem, [jnp.full((L,), k, jnp.int32), tok]) for k in range(K)]

                    @plsc.parallel_loop(0, W // L, unroll=W // L)
                    def _(j):
                        lo = jnp.zeros((L,), F32)
                        hi = jnp.zeros((L,), F32)
                        for k in range(K):
                            pair = plsc.bitcast(rows[k, t, pl.ds(j * L, L)], BF16)
                            a, b = plsc.unpack(pair, format=plsc.PackFormat.INTERLEAVED)
                            lo = lo + wk[k] * a
                            hi = hi + wk[k] * b
                        o_vmem[batch * G + t, pl.ds(j * L, L)] = plsc.bitcast(
                            plsc.pack(lo, hi, format=plsc.PackFormat.INTERLEAVED), y.dtype)

        pltpu.emit_pipeline(
            body,
            grid=(N // SC_WINDOW,),
            in_specs=[pl.BlockSpec((K, SC_WINDOW), lambda i: (0, i)),
                      pl.BlockSpec((K, SC_WINDOW), lambda i: (0, i))],
            out_specs=[pl.BlockSpec((SC_WINDOW, W), lambda i: (i, 0))],
            core_axis_name=("c", "s"),
            dimension_semantics=(pltpu.PARALLEL,),
        )(i_hbm, w_hbm, o_hbm)

    return gather(y, dest, wts)


EXPERT_INPUT_SLOTS = 3


def _expert_kernel(blk_e_ref, nvalid_ref, parity_ref, next_e_ref, xa_hbm, xb_hbm, w1_hbm, w3_hbm, w2_hbm,
                   ya_hbm, yb_hbm, w1_sc, w3_sc, w2_sc, xa_buf, xb_buf, sems, w1_st, w3_st, w2_st, wsems,
                   ya_st, yb_st, osems, *, layer):
    b = pl.program_id(0)
    nb = pl.num_programs(0)
    nv = nvalid_ref[b]
    prev_e = blk_e_ref[jnp.maximum(b - 1, 0)]
    blk = xa_buf.shape[1]
    ring = EXPERT_INPUT_SLOTS
    n_experts = w1_hbm.shape[1]

    def fetch_weights(e, par):
        return (pltpu.make_async_copy(w1_hbm.at[layer, e], w1_st.at[par], wsems.at[par, 0]),
                pltpu.make_async_copy(w3_hbm.at[layer, e], w3_st.at[par], wsems.at[par, 1]),
                pltpu.make_async_copy(w2_hbm.at[layer, e], w2_st.at[par], wsems.at[par, 2]))

    def fetch(block, slot):
        rows = pl.ds(pl.multiple_of(block * blk, blk), blk)
        return (pltpu.make_async_copy(xa_hbm.at[rows], xa_buf.at[slot], sems.at[slot, 0]),
                pltpu.make_async_copy(xb_hbm.at[rows], xb_buf.at[slot], sems.at[slot, 1]))

    @pl.when(b == 0)
    def _():
        for i in range(ring - 1):
            for c in fetch(i, i):
                c.start()

    ahead = b + (ring - 1)

    @pl.when(ahead < nb)
    def _():
        for c in fetch(ahead, ahead % ring):
            c.start()

    slot = b % ring
    for c in fetch(b, slot):
        c.wait()

    @pl.when(b == 0)
    def _():
        for c in fetch_weights(blk_e_ref[0], 0):
            c.start()

    @pl.when((b == 0) | (blk_e_ref[b] != prev_e))
    def _():
        par = parity_ref[b]
        nxt = next_e_ref[b]

        @pl.when(nxt < n_experts)
        def _():
            for c in fetch_weights(nxt, 1 - par):
                c.start()

        for c in fetch_weights(blk_e_ref[b], par):
            c.wait()
        w1_sc[...] = w1_st[par].astype(BF16)
        w3_sc[...] = w3_st[par].astype(BF16)
        w2_sc[...] = w2_st[par].astype(BF16)

    def put(block, oslot):
        rows = pl.ds(pl.multiple_of(block * blk, blk), blk)
        return (pltpu.make_async_copy(ya_st.at[oslot], ya_hbm.at[rows], osems.at[oslot, 0]),
                pltpu.make_async_copy(yb_st.at[oslot], yb_hbm.at[rows], osems.at[oslot, 1]))

    oslot = b % 2

    @pl.when(b >= 2)
    def _():
        for c in put(b - 2, oslot):
            c.wait()

    @pl.when(nv > 0)
    def _():
        x = _unpack_row_halves(xa_buf[slot], xb_buf[slot])
        rows = lax.broadcasted_iota(jnp.int32, x.shape, 0)
        x = jnp.where(rows < nv, x, 0.0).astype(BF16)
        hid = _silu(jnp.dot(x, w1_sc[...], preferred_element_type=F32)) * jnp.dot(
            x, w3_sc[...], preferred_element_type=F32)
        y = jnp.dot(hid.astype(BF16), w2_sc[...], preferred_element_type=F32)
        ya_st[oslot], yb_st[oslot] = _pack_row_halves(y)

    @pl.when(nv == 0)
    def _():
        ya_st[oslot] = jnp.zeros(ya_st.shape[1:], ya_st.dtype)
        yb_st[oslot] = jnp.zeros(yb_st.shape[1:], yb_st.dtype)

    for c in put(b, oslot):
        c.start()

    @pl.when(b == nb - 1)
    def _():
        @pl.when(nb >= 2)
        def _():
            for c in put(b - 1, 1 - oslot):
                c.wait()

        for c in put(b, oslot):
            c.wait()


def routed_experts(xa, xb, tables, w1, w3, w2, layer):
    P = xa.shape[0]
    blk = SLOT_BLOCK
    _, E, D, FF = w1.shape
    grid_spec = pltpu.PrefetchScalarGridSpec(
        num_scalar_prefetch=len(tables),
        grid=(P // blk,),
        in_specs=[pl.BlockSpec(memory_space=pl.ANY)] * 5,
        out_specs=[pl.BlockSpec(memory_space=pl.ANY)] * 2,
        scratch_shapes=[
            pltpu.VMEM((D, FF), BF16), pltpu.VMEM((D, FF), BF16), pltpu.VMEM((FF, D), BF16),
            pltpu.VMEM((EXPERT_INPUT_SLOTS, blk, PACK_W), jnp.int32),
            pltpu.VMEM((EXPERT_INPUT_SLOTS, blk, PACK_W), jnp.int32),
            pltpu.SemaphoreType.DMA((EXPERT_INPUT_SLOTS, 2)),
            pltpu.VMEM((2, D, FF), F32), pltpu.VMEM((2, D, FF), F32), pltpu.VMEM((2, FF, D), F32),
            pltpu.SemaphoreType.DMA((2, 3)),
            pltpu.VMEM((2, blk, PACK_W), jnp.int32), pltpu.VMEM((2, blk, PACK_W), jnp.int32),
            pltpu.SemaphoreType.DMA((2, 2)),
        ],
    )
    return pl.pallas_call(
        functools.partial(_expert_kernel, layer=layer),
        grid_spec=grid_spec,
        out_shape=[jax.ShapeDtypeStruct((P, PACK_W), jnp.int32)] * 2,
        compiler_params=_cp(("arbitrary",), VMEM_LIMIT),
        name="routed_experts",
    )(*tables, xa, xb, w1, w3, w2)


def _combine_kernel(xmid_ref, ra_ref, rb_ref, mod2_ref, fg_ref, *rest):
    out_ref = rest[-1]
    D = xmid_ref.shape[1]
    x = xmid_ref[...] + mod2_ref[0][:, 2 * D:] * _unpack_row_halves(ra_ref[...], rb_ref[...])
    out_ref[...] = x * lax.rsqrt(jnp.mean(x * x, axis=-1, keepdims=True) + EPS) * fg_ref[...]


def combine(xmid, ra, rb, mod2, final_g, seq, out_rows=None, row0=0, out_buf=None):
    N, D = xmid.shape
    tm = 512
    tpb = seq // tm
    tile0 = row0 // tm
    in_specs = [
        pl.BlockSpec((tm, D), lambda i: (i, 0)),
        pl.BlockSpec((tm, PACK_W), lambda i: (i, 0)),
        pl.BlockSpec((tm, PACK_W), lambda i: (i, 0)),
        pl.BlockSpec((1, 1, 3 * D), lambda i: (i // tpb, 0, 0)),
        pl.BlockSpec((1, D), lambda i: (0, 0)),
    ]
    args = [xmid, ra, rb, mod2, final_g.reshape(1, D)]
    aliases = {}
    if out_buf is not None:
        in_specs.append(pl.BlockSpec(memory_space=pl.ANY))
        args.append(out_buf)
        aliases = {len(args) - 1: 0}
    return pl.pallas_call(
        _combine_kernel,
        grid=(N // tm,),
        in_specs=in_specs,
        out_specs=pl.BlockSpec((tm, D), lambda i: (i + tile0, 0)),
        out_shape=jax.ShapeDtypeStruct((out_rows or N, D), F32),
        input_output_aliases=aliases,
        compiler_params=_cp(("parallel",), VMEM_LIMIT),
        name="combine",
    )(*args)


TOKEN_STREAMS = 2


def _permute_w_in(w):
    ub = w[:, 3 * DA:3 * DA + DB]
    lat_lo = 3 * DA + DB
    lat_hi = lat_lo + Q_LORA + KV_LORA + QK_ROPE
    lat, gates = w[:, lat_lo:lat_hi], w[:, lat_hi:]
    pad = jnp.zeros((w.shape[0], LAT_W - (lat_hi - lat_lo)), w.dtype)
    parts = [gates, ub, lat, pad]
    for g in range(len(DIL_GROUPS)):
        sl = slice(g * GROUP_W, (g + 1) * GROUP_W)
        parts += [w[:, :DA][:, sl] * (HEAD_DIM_A ** -0.5), w[:, DA:2 * DA][:, sl], w[:, 2 * DA:3 * DA][:, sl]]
    return jnp.concatenate(parts, axis=1).astype(BF16)


def kernel(x, c, positions, ada_mix_w, ada_mix_b, norm_mix_g, w_in, pool_w, pool_scale, cq_norm_g, ckv_norm_g, w_uq, w_ukv, w_oa, w_ob, w_oc, w_out, ada_ffn_w, ada_ffn_b, norm_ffn_g, router_w, router_bias, exp_w1, exp_w3, exp_w2, sh_w1, sh_w3, sh_w2, final_g):
    B, S, D = x.shape
    depth = w_in.shape[0]
    mod_mix = adaln_rows(c, ada_mix_w, ada_mix_b)
    mod_ffn = adaln_rows(c, ada_ffn_w, ada_ffn_b)
    streams = TOKEN_STREAMS if B % TOKEN_STREAMS == 0 else 1
    Bs = B // streams
    Ns = Bs * S
    x_all = x.reshape(B * S, D)
    xs = [None] * streams
    out_all = None
    pos_s = [positions[s * Bs:(s + 1) * Bs] for s in range(streams)]
    for l in range(depth):
        last = l == depth - 1
        w_in_l = _permute_w_in(w_in[l])
        mla_w = _mla_weights(cq_norm_g[l], ckv_norm_g[l], w_uq[l], w_ukv[l])
        mix_w = (norm_ffn_g[l], pool_w[l].astype(BF16), pool_scale[l],
                 w_oa[l].astype(BF16), w_ob[l].astype(BF16), w_oc[l].astype(BF16), w_out[l].astype(BF16),
                 router_w[l].T.astype(BF16), sh_w1[l].astype(BF16), sh_w3[l].astype(BF16), sh_w2[l].astype(BF16))
        for s in range(streams):
            mod1 = mod_mix[l, s * Bs:(s + 1) * Bs].reshape(Bs, 1, 3 * D)
            mod2 = mod_ffn[l, s * Bs:(s + 1) * Bs].reshape(Bs, 1, 3 * D)
            if l == 0:
                x2, row0 = x_all, s * Ns
                gu, lat, *qkv = in_projection(x2, norm_mix_g[l], mod1, w_in_l, S, row0)
            else:
                row0 = 0
                x2, gu, lat, *qkv = in_projection(xs[s][0], norm_mix_g[l], mod1, w_in_l, S, 0, xs[s][1:])
            dil = [dilated_attention(qkv[2 * g], qkv[2 * g + 1]) for g in range(len(DIL_GROUPS))]
            q_all, k_all, vt_all = mla_prep(lat, pos_s[s], *mla_w, Bs, S)
            yc = mla_attention(q_all, k_all, vt_all, Bs, S)
            xmid, h2a, h2b, logits_t = mix_out(x2, gu, dil, yc, mod1, mod2, *mix_w, S, row0)
            dest, w_k, counts = route(logits_t, router_bias[l])
            tables = block_tables(counts, Ns)
            n_slots = tables[0].shape[0] * SLOT_BLOCK
            xa = sc_scatter_rows(h2a, dest, n_slots)
            xb = sc_scatter_rows(h2b, dest, n_slots)
            ya, yb = routed_experts(xa, xb, tables, exp_w1, exp_w3, exp_w2, l)
            ra = sc_weighted_gather(ya, dest, w_k)
            rb = sc_weighted_gather(yb, dest, w_k)
            if last:
                out_all = combine(xmid, ra, rb, mod2, final_g, S, B * S, s * Ns, out_all)
            else:
                xs[s] = (xmid, ra, rb, mod2)
    return out_all.reshape(B, S, D)
```

```python
import functools
import math

import jax
import jax.numpy as jnp
from jax import lax
from jax.experimental import pallas as pl
from jax.experimental.pallas import tpu as pltpu
from jax.experimental.pallas import tpu_sc as plsc

F32 = jnp.float32
BF16 = jnp.bfloat16
HIGHEST = lax.Precision.HIGHEST

D_MODEL = 1024
HEAD_DIM_A = 64
HEADS_PER_GROUP_A = 4
DIL_GROUPS = ((128, 1), (512, 4), (2048, 16))
GROUP_W = HEADS_PER_GROUP_A * HEAD_DIM_A
DA = GROUP_W * len(DIL_GROUPS)
POOL_WINDOWS = (2, 4, 8, 16)
POOL_GROUP_DIM = 128
DB = POOL_GROUP_DIM * len(POOL_WINDOWS)
POOL_HALO = 16
N_HEADS_C = 8
QK_NOPE = 64
QK_ROPE = 32
V_DIM = 64
Q_LORA = 384
KV_LORA = 256
DC = N_HEADS_C * V_DIM
HEAD_PAD_C = 128
ROPE_THETA = 10000.0
N_EXPERTS = 64
TOP_K = 8
N_GROUPS = 8
TOPK_GROUPS = 4
GROUP_SIZE = N_EXPERTS // N_GROUPS
ROUTED_SCALE = 2.5
EPS = 1e-6
NEG = -1e30
Q_BLOCK = 128

LAT_W = 768
GU_W = 3 * D_MODEL + DB
IN_OUT_WIDTHS = (GU_W, LAT_W) + (2 * GROUP_W, GROUP_W) * len(DIL_GROUPS)

VMEM_LIMIT = 56 * 1024 * 1024


def _cp(sem, vmem=None):
    return pltpu.CompilerParams(dimension_semantics=sem, vmem_limit_bytes=vmem)


def _silu(v):
    return v * jax.nn.sigmoid(v)


def _nt_dot(a, b):
    return lax.dot_general(a, b, (((1,), (1,)), ((), ())), preferred_element_type=F32)


PACK_W = D_MODEL // 4
_HI_MASK = -65536


def _bf16_bits(v):
    return lax.bitcast_convert_type(v.astype(BF16).astype(F32), jnp.int32)


def _pack_row_halves(v):
    halves = []
    for h in range(2):
        lo = _bf16_bits(v[:, (2 * h) * PACK_W:(2 * h + 1) * PACK_W])
        hi = _bf16_bits(v[:, (2 * h + 1) * PACK_W:(2 * h + 2) * PACK_W])
        halves.append(lax.shift_right_logical(lo, 16) | (hi & _HI_MASK))
    return halves


def _unpack_row_halves(wa, wb):
    parts = []
    for w in (wa, wb):
        parts.append(lax.bitcast_convert_type(lax.shift_left(w, 16), F32))
        parts.append(lax.bitcast_convert_type(w & _HI_MASK, F32))
    return jnp.concatenate(parts, axis=1)


def _adaln_kernel(c_ref, w_ref, b_ref, o_ref):
    s = _silu(c_ref[...])
    o_ref[0] = jnp.dot(s, w_ref[0], preferred_element_type=F32, precision=HIGHEST) + b_ref[0]


def adaln_rows(c, w, b):
    L, D, D3 = w.shape
    B = c.shape[0]
    tn = 1024
    return pl.pallas_call(
        _adaln_kernel,
        grid=(L, D3 // tn),
        in_specs=[
            pl.BlockSpec((B, D), lambda l, j: (0, 0)),
            pl.BlockSpec((1, D, tn), lambda l, j: (l, 0, j)),
            pl.BlockSpec((1, 1, tn), lambda l, j: (l, 0, j)),
        ],
        out_specs=pl.BlockSpec((1, B, tn), lambda l, j: (l, 0, j)),
        out_shape=jax.ShapeDtypeStruct((L, B, D3), F32),
        compiler_params=_cp(("parallel", "parallel")),
        name="adaln_rows",
    )(c, w, b.reshape(L, 1, D3))


LANES = 128


def _inproj_kernel(x_ref, g_ref, mod_ref, w_ref, *refs, chunk, pending):
    o_refs, scr = refs[:-1], refs[-1]
    D = x_ref.shape[1]
    x = x_ref[...]
    if pending:
        ra_ref, rb_ref, gate_ref, x_out_ref, *o_refs = o_refs
        x = x + gate_ref[0][:, 2 * D:] * _unpack_row_halves(ra_ref[...], rb_ref[...])
        x_out_ref[...] = x
    y = x * lax.rsqrt(jnp.mean(x * x, axis=-1, keepdims=True) + EPS) * g_ref[...]
    mod = mod_ref[0]
    h = (y * (1.0 + mod[:, D:2 * D]) + mod[:, :D]).astype(BF16)
    col = 0
    for o_ref in o_refs:
        width = o_ref.shape[-1]
        if o_ref.ndim == 2:
            for c0 in range(0, width, chunk):
                cw = min(chunk, width - c0)
                o_ref[:, c0:c0 + cw] = jnp.dot(
                    h, w_ref[:, col + c0:col + c0 + cw], preferred_element_type=F32).astype(o_ref.dtype)
        else:
            dil, rows = o_ref.shape[1], o_ref.shape[2]
            z = jnp.dot(h, w_ref[:, col:col + width], preferred_element_type=F32)
            if dil == 1:
                o_ref[0, 0] = z.astype(o_ref.dtype)
            else:
                for c in range(width // LANES):
                    scr[c] = z[:, c * LANES:(c + 1) * LANES]
                for r in range(dil):
                    o_ref[0, r] = jnp.concatenate(
                        [scr[c, pl.ds(r, rows, stride=dil), :] for c in range(width // LANES)],
                        axis=1).astype(o_ref.dtype)
        col += width


def in_projection(x2, g, mod, w, seq, row0=0, pending=None):
    D = x2.shape[1]
    B = mod.shape[0]
    N = B * seq
    tm = 512
    tpb = seq // tm
    tile0 = row0 // tm
    out_specs = [pl.BlockSpec((tm, wd), lambda i: (i, 0)) for wd in IN_OUT_WIDTHS[:2]]
    out_shape = [jax.ShapeDtypeStruct((N, wd), BF16) for wd in IN_OUT_WIDTHS[:2]]
    for grp, (_, dil) in enumerate(DIL_GROUPS):
        for wd in IN_OUT_WIDTHS[2 + 2 * grp:4 + 2 * grp]:
            out_specs.append(pl.BlockSpec((1, dil, tm // dil, wd), lambda i: (i // tpb, 0, i % tpb, 0)))
            out_shape.append(jax.ShapeDtypeStruct((B, dil, seq // dil, wd), BF16))
    in_specs = [
        pl.BlockSpec((tm, D), lambda i: (i + tile0, 0)),
        pl.BlockSpec((1, D), lambda i: (0, 0)),
        pl.BlockSpec((1, 1, 3 * D), lambda i: (i // tpb, 0, 0)),
        pl.BlockSpec(w.shape, lambda i: (0, 0), pipeline_mode=pl.Buffered(1)),
    ]
    args = [x2, g.reshape(1, D), mod, w]
    if pending is not None:
        in_specs += [pl.BlockSpec((tm, PACK_W), lambda i: (i, 0)), pl.BlockSpec((tm, PACK_W), lambda i: (i, 0)),
                     pl.BlockSpec((1, 1, 3 * D), lambda i: (i // tpb, 0, 0))]
        args += list(pending)
        out_specs.insert(0, pl.BlockSpec((tm, D), lambda i: (i, 0)))
        out_shape.insert(0, jax.ShapeDtypeStruct((N, D), F32))
    return pl.pallas_call(
        functools.partial(_inproj_kernel, chunk=512, pending=pending is not None),
        grid=(N // tm,),
        in_specs=in_specs,
        out_specs=out_specs,
        out_shape=out_shape,
        scratch_shapes=[pltpu.VMEM((max(IN_OUT_WIDTHS[2:]) // LANES, tm, LANES), F32)],
        compiler_params=_cp(("parallel",), VMEM_LIMIT),
        name="in_projection",
    )(*args)


def _dilated_kernel(q_ref, kc_ref, kp_ref, vc_ref, vp_ref, o_ref, lse_ref):
    i = pl.program_id(1)
    T = Q_BLOCK
    key = lax.broadcasted_iota(jnp.int32, (T, T), 0)
    qry = lax.broadcasted_iota(jnp.int32, (T, T), 1)
    valid_c = key <= qry
    near = key >= qry
    seqs, run = q_ref.shape[0], q_ref.shape[1] // T
    heads = [slice(h * HEAD_DIM_A, (h + 1) * HEAD_DIM_A) for h in range(HEADS_PER_GROUP_A)]

    def transposed(v):
        return v.astype(F32).T.astype(BF16)

    vts = {(s, j): transposed(vc_ref[s, j * T:(j + 1) * T, :]) for s in range(seqs) for j in range(run)}
    vt_before = [transposed(vp_ref[s]) for s in range(seqs)]

    def blocks(s, j):
        rows = slice(j * T, (j + 1) * T)
        if j == 0:
            return rows, kc_ref[s, rows, :], vts[s, 0], kp_ref[s], vt_before[s], near & (i > 0)
        before = slice((j - 1) * T, j * T)
        return rows, kc_ref[s, rows, :], vts[s, j], kc_ref[s, before, :], vts[s, j - 1], near

    scores, probs = {}, {}
    for s in range(seqs):
        for j in range(run):
            rows, kc, _, kp, _, valid_p = blocks(s, j)
            q = q_ref[s, rows, :]
            for h, sl in enumerate(heads):
                qh = q[:, sl]
                scores[s, j, h] = (jnp.where(valid_c, _nt_dot(kc[:, sl], qh), NEG),
                                   jnp.where(valid_p, _nt_dot(kp[:, sl], qh), NEG))
    for chain, (sc, sp) in scores.items():
        m = jnp.maximum(jnp.max(sc, axis=0, keepdims=True), jnp.max(sp, axis=0, keepdims=True))
        pc = jnp.exp(sc - m)
        pp = jnp.exp(sp - m)
        den = jnp.sum(pc, axis=0, keepdims=True) + jnp.sum(pp, axis=0, keepdims=True)
        probs[chain] = (pc.astype(BF16), pp.astype(BF16), den, m + jnp.log(den))
    spread = LSE_LANES // len(heads)
    for s in range(seqs):
        for j in range(run):
            rows, _, vtc, _, vtp, _ = blocks(s, j)
            outs = []
            for h, sl in enumerate(heads):
                pc, pp, den, _ = probs[s, j, h]
                o = (jnp.dot(vtc[sl, :], pc, preferred_element_type=F32)
                     + jnp.dot(vtp[sl, :], pp, preferred_element_type=F32))
                outs.append(o / den)
            o_ref[s, rows, :] = jnp.concatenate(outs, axis=0).T.astype(o_ref.dtype)
            lse_t = jnp.concatenate(
                [jnp.broadcast_to(probs[s, j, h][3], (spread, T)) for h in range(len(heads))], axis=0)
            lse_ref[s, rows, :] = lse_t.T


DILATED_RUN = 16


LSE_LANES = 128


def dilated_attention(qk, v):
    batch, dilation, L, _ = qk.shape
    nb = L // Q_BLOCK
    run = min(DILATED_RUN, nb)
    seqs = DILATED_RUN // run
    qk_r = qk.reshape(batch * dilation, L, 2 * GROUP_W)
    v_r = v.reshape(batch * dilation, L, GROUP_W)
    before = lambda i: jnp.maximum(i * run - 1, 0)
    o, lse = pl.pallas_call(
        _dilated_kernel,
        grid=(batch * dilation // seqs, nb // run),
        in_specs=[
            pl.BlockSpec((seqs, run * Q_BLOCK, GROUP_W), lambda s, i: (s, i, 0)),
            pl.BlockSpec((seqs, run * Q_BLOCK, GROUP_W), lambda s, i: (s, i, 1)),
            pl.BlockSpec((seqs, Q_BLOCK, GROUP_W), lambda s, i: (s, before(i), 1)),
            pl.BlockSpec((seqs, run * Q_BLOCK, GROUP_W), lambda s, i: (s, i, 0)),
            pl.BlockSpec((seqs, Q_BLOCK, GROUP_W), lambda s, i: (s, before(i), 0)),
        ],
        out_specs=[
            pl.BlockSpec((seqs, run * Q_BLOCK, GROUP_W), lambda s, i: (s, i, 0)),
            pl.BlockSpec((seqs, run * Q_BLOCK, LSE_LANES), lambda s, i: (s, i, 0)),
        ],
        out_shape=[
            jax.ShapeDtypeStruct((batch * dilation, L, GROUP_W), BF16),
            jax.ShapeDtypeStruct((batch * dilation, L, LSE_LANES), F32),
        ],
        compiler_params=_cp(("parallel", "parallel")),
        name=f"dilated_attention_d{dilation}",
    )(qk_r, qk_r, qk_r, v_r, v_r)
    return o.reshape(batch, dilation, L, GROUP_W), lse.reshape(batch, dilation, L, LSE_LANES)


def _mla_prep_kernel(lat_ref, pos_ref, gq_ref, gkv_ref, wq_ref, wk_ref, wvt_ref, freq_ref, spread_ref, one_ref,
                     q_ref, k_ref, vt_ref):
    HP = N_HEADS_C * HEAD_PAD_C
    lat = lat_ref[...].astype(F32)
    cq = lat[:, :Q_LORA]
    ckr = lat[:, Q_LORA:]
    zq = (cq * lax.rsqrt(jnp.mean(cq * cq, axis=-1, keepdims=True) + EPS) * gq_ref[...]).astype(BF16)
    lane = lax.broadcasted_iota(jnp.int32, ckr.shape, 1)
    is_kv = lane < KV_LORA
    ms = jnp.sum(jnp.where(is_kv, ckr * ckr, 0.0), axis=-1, keepdims=True) * (1.0 / KV_LORA)
    zkv = (ckr * jnp.where(is_kv, lax.rsqrt(ms + EPS) * gkv_ref[...], 1.0)).astype(BF16)
    qq = jnp.dot(zq, wq_ref[...], preferred_element_type=F32)
    kk = jnp.dot(zkv, wk_ref[:, :HP], preferred_element_type=F32)
    kk_sw = jnp.dot(zkv[:, KV_LORA:], wk_ref[KV_LORA:, HP:], preferred_element_type=F32)
    ang_t = freq_ref[...] * pos_ref[0].astype(F32)

    def to_lanes(t):
        hi = t.astype(BF16)
        lo = (t - hi.astype(F32)).astype(BF16)
        tn_dot = lambda a: lax.dot_general(a, spread_ref[...], (((0,), (0,)), ((), ())), preferred_element_type=F32)
        return tn_dot(hi) + tn_dot(lo)

    cos = to_lanes(jnp.cos(ang_t)) + one_ref[...]
    sin = to_lanes(jnp.sin(ang_t))
    slot_lane = lax.broadcasted_iota(jnp.int32, (1, HEAD_PAD_C), 1)
    half = QK_ROPE // 2
    sin_x1 = jnp.where((slot_lane >= QK_NOPE) & (slot_lane < QK_NOPE + half), -sin, 0.0)
    sin_x2 = jnp.where((slot_lane >= QK_NOPE + half) & (slot_lane < QK_NOPE + QK_ROPE), sin, 0.0)
    for h in range(N_HEADS_C):
        lo, hi = h * HEAD_PAD_C, (h + 1) * HEAD_PAD_C
        qh = qq[:, lo:hi]
        q_ref[:, lo:hi] = (qh * cos + pltpu.roll(qh, HEAD_PAD_C - half, axis=1) * sin_x1
                           + pltpu.roll(qh, half, axis=1) * sin_x2).astype(q_ref.dtype)
        k_ref[:, lo:hi] = (kk[:, lo:hi] * cos + kk_sw[:, lo:hi] * sin).astype(k_ref.dtype)
    vt_ref[0] = _nt_dot(wvt_ref[...], zkv).astype(vt_ref.dtype)


def _mla_weights(cq_g, ckv_g, w_uq, w_ukv):
    H, HPAD, half = N_HEADS_C, HEAD_PAD_C, QK_ROPE // 2
    scale = (QK_NOPE + QK_ROPE) ** -0.5 * math.log2(math.e)
    wq = w_uq.reshape(Q_LORA, H, QK_NOPE + QK_ROPE) * scale
    wq_big = jnp.pad(wq, ((0, 0), (0, 0), (0, HPAD - QK_NOPE - QK_ROPE))).reshape(Q_LORA, H * HPAD)

    rows = LAT_W - Q_LORA
    wkv = w_ukv.reshape(KV_LORA, H, QK_NOPE + V_DIM)
    eye = jnp.eye(QK_ROPE, dtype=F32)
    k_lin = jnp.zeros((rows, H, HPAD), F32)
    k_lin = k_lin.at[:KV_LORA, :, :QK_NOPE].set(wkv[..., :QK_NOPE])
    k_lin = k_lin.at[KV_LORA:KV_LORA + QK_ROPE, :, QK_NOPE:QK_NOPE + QK_ROPE].set(
        jnp.broadcast_to(eye[:, None, :], (QK_ROPE, H, QK_ROPE)))
    swap = jnp.zeros((QK_ROPE, QK_ROPE), F32).at[half:, :half].set(-jnp.eye(half)).at[:half, half:].set(jnp.eye(half))
    k_sw = jnp.zeros((rows, H, HPAD), F32)
    k_sw = k_sw.at[KV_LORA:KV_LORA + QK_ROPE, :, QK_NOPE:QK_NOPE + QK_ROPE].set(
        jnp.broadcast_to(swap[:, None, :], (QK_ROPE, H, QK_ROPE)))
    v_w = jnp.zeros((rows, H, V_DIM), F32).at[:KV_LORA].set(wkv[..., QK_NOPE:])
    wk_big = jnp.concatenate([k_lin.reshape(rows, H * HPAD), k_sw.reshape(rows, H * HPAD)], axis=1)
    wv_t = v_w.reshape(rows, H * V_DIM).T

    gkv = jnp.concatenate([ckv_g, jnp.ones((rows - KV_LORA,), F32)]).reshape(1, rows)
    return cq_g.reshape(1, Q_LORA), gkv, wq_big.astype(BF16), wk_big.astype(BF16), wv_t.astype(BF16)


def _rope_tables():
    half = QK_ROPE // 2
    freqs = (ROPE_THETA ** (-jnp.arange(0, QK_ROPE, 2, dtype=F32) / QK_ROPE)).reshape(half, 1)
    lane = jnp.arange(HEAD_PAD_C)[None, :]
    j = jnp.arange(half)[:, None]
    spread = (lane == QK_NOPE + j) | (lane == QK_NOPE + half + j)
    off_rope = ~jnp.any(spread, axis=0, keepdims=True)
    return freqs, spread.astype(BF16), off_rope.astype(F32)


def mla_prep(lat, positions, gq, gkv, wq_big, wk_big, wv_t, batch, seq):
    N = lat.shape[0]
    HP = N_HEADS_C * HEAD_PAD_C
    tm = 512
    tpb = seq // tm
    freqs, spread, off_rope = _rope_tables()
    pos_rows = positions.reshape(N // tm, 1, tm)
    const = lambda shape: pl.BlockSpec(shape, lambda i: (0, 0))
    return pl.pallas_call(
        _mla_prep_kernel,
        grid=(N // tm,),
        in_specs=[
            pl.BlockSpec((tm, LAT_W), lambda i: (i, 0)),
            pl.BlockSpec((1, 1, tm), lambda i: (i, 0, 0)),
            const(gq.shape), const(gkv.shape), const(wq_big.shape), const(wk_big.shape), const(wv_t.shape),
            const(freqs.shape), const(spread.shape), const(off_rope.shape),
        ],
        out_specs=[
            pl.BlockSpec((tm, HP), lambda i: (i, 0)),
            pl.BlockSpec((tm, HP), lambda i: (i, 0)),
            pl.BlockSpec((1, DC, tm), lambda i: (i // tpb, 0, i % tpb)),
        ],
        out_shape=[
            jax.ShapeDtypeStruct((N, HP), BF16),
            jax.ShapeDtypeStruct((N, HP), BF16),
            jax.ShapeDtypeStruct((batch, DC, seq), BF16),
        ],
        compiler_params=_cp(("parallel",), VMEM_LIMIT),
        name="mla_prep",
    )(lat, pos_rows, gq, gkv, wq_big, wk_big, wv_t, freqs, spread, off_rope)


HEADS_PER_STEP_C = 8
FLASH_Q_CHUNK = 256


def _mla_flash_kernel(qi_ref, ki_ref, q_ref, k_ref, vt_ref, o_ref, m_sc, l_sc, acc_sc):
    t = pl.program_id(2)
    qi, ki = qi_ref[t], ki_ref[t]

    @pl.when(ki == 0)
    def _():
        m_sc[...] = jnp.full(m_sc.shape, NEG, F32)
        l_sc[...] = jnp.zeros(l_sc.shape, F32)
        acc_sc[...] = jnp.zeros(acc_sc.shape, F32)

    def step(masked):
        T = q_ref.shape[1]
        if masked:
            key = lax.broadcasted_iota(jnp.int32, (T, T), 0)
            qry = lax.broadcasted_iota(jnp.int32, (T, T), 1)
            keep = key <= qry
        chains = [(h, c) for h in range(HEADS_PER_STEP_C) for c in range(T // FLASH_Q_CHUNK)]
        scores, probs, alphas = {}, {}, {}

        def keys_for(c):
            return (c + 1) * FLASH_Q_CHUNK if masked else T

        def qk(h, c):
            qs = slice(c * FLASH_Q_CHUNK, (c + 1) * FLASH_Q_CHUNK)
            q = q_ref[0, qs, h * HEAD_PAD_C:(h + 1) * HEAD_PAD_C]
            k = k_ref[0, :keys_for(c), h * HEAD_PAD_C:(h + 1) * HEAD_PAD_C]
            st = _nt_dot(k, q)
            scores[h, c] = jnp.where(keep[:keys_for(c), qs], st, NEG) if masked else st

        def softmax(h, c):
            qs = slice(c * FLASH_Q_CHUNK, (c + 1) * FLASH_Q_CHUNK)
            st = scores.pop((h, c))
            m_prev = m_sc[h, :, qs]
            m_new = jnp.maximum(m_prev, jnp.max(st, axis=0, keepdims=True))
            alpha = jnp.exp2(m_prev - m_new)
            p = jnp.exp2(st - m_new)
            l_sc[h, :, qs] = alpha * l_sc[h, :, qs] + jnp.sum(p, axis=0, keepdims=True)
            m_sc[h, :, qs] = m_new
            probs[h, c], alphas[h, c] = p.astype(BF16), alpha

        def pv(h, c):
            qs = slice(c * FLASH_Q_CHUNK, (c + 1) * FLASH_Q_CHUNK)
            vt = vt_ref[0, h * V_DIM:(h + 1) * V_DIM, :keys_for(c)]
            acc_sc[h, :, qs] = alphas.pop((h, c)) * acc_sc[h, :, qs] + jnp.dot(
                vt, probs.pop((h, c)), preferred_element_type=F32)

        for phase in (qk, softmax, pv):
            for ch in chains:
                phase(*ch)

    @pl.when(ki < qi)
    def _():
        step(False)

    @pl.when(ki == qi)
    def _():
        step(True)
        ot = jnp.concatenate([acc_sc[h] / l_sc[h] for h in range(HEADS_PER_STEP_C)], axis=0)
        o_ref[0] = ot.T.astype(o_ref.dtype)


def mla_attention(q_all, k_all, vt_all, batch, seq):
    T = 512
    nq = seq // T
    pairs = [(a, b) for a in range(nq) for b in range(a + 1)]
    qi_tab = jnp.asarray([p[0] for p in pairs], jnp.int32)
    ki_tab = jnp.asarray([p[1] for p in pairs], jnp.int32)
    hp = N_HEADS_C // HEADS_PER_STEP_C
    qw = HEADS_PER_STEP_C * HEAD_PAD_C
    vw = HEADS_PER_STEP_C * V_DIM
    q3 = q_all.reshape(batch, seq, -1)
    k3 = k_all.reshape(batch, seq, -1)
    grid_spec = pltpu.PrefetchScalarGridSpec(
        num_scalar_prefetch=2,
        grid=(batch, hp, len(pairs)),
        in_specs=[
            pl.BlockSpec((1, T, qw), lambda b, h, t, qi, ki: (b, qi[t], h)),
            pl.BlockSpec((1, T, qw), lambda b, h, t, qi, ki: (b, ki[t], h)),
            pl.BlockSpec((1, vw, T), lambda b, h, t, qi, ki: (b, h, ki[t])),
        ],
        out_specs=pl.BlockSpec((1, T, vw), lambda b, h, t, qi, ki: (b, qi[t], h)),
        scratch_shapes=[
            pltpu.VMEM((HEADS_PER_STEP_C, 1, T), F32),
            pltpu.VMEM((HEADS_PER_STEP_C, 1, T), F32),
            pltpu.VMEM((HEADS_PER_STEP_C, V_DIM, T), F32),
        ],
    )
    o = pl.pallas_call(
        _mla_flash_kernel,
        grid_spec=grid_spec,
        out_shape=jax.ShapeDtypeStruct((batch, seq, DC), BF16),
        compiler_params=_cp(("parallel", "parallel", "arbitrary")),
        name="mla_attention",
    )(qi_tab, ki_tab, q3, k3, vt_all)
    return o.reshape(batch * seq, DC)


def _mixout_kernel(x_ref, gates_ref, ub_ref, ubh_ref, o1_ref, o2_ref, o3_ref, l1_ref, l2_ref, l3_ref, yc_ref,
                   mod1_ref, mod2_ref, g2_ref, poolw_ref, pscale_ref, woa_ref, wob_ref, woc_ref, wout_ref,
                   rwt_ref, sw1_ref, sw3_ref, sw2_ref, spread_ref,
                   xmid_ref, h2a_ref, h2b_ref, logit_ref, *scratch, tiles_per_batch):
    D = x_ref.shape[1]
    tm = x_ref.shape[0]
    tile = pl.program_id(0) % tiles_per_batch
    o_scrs, l_scrs = scratch[:3], scratch[3:]

    def token_order(ref, scr):
        dil, rows, width = ref.shape[1:]
        if dil == 1:
            return ref[0, 0].astype(F32)
        for r in range(dil):
            v = ref[0, r].astype(F32)
            for c in range(width // LANES):
                scr[c, pl.ds(r, rows, stride=dil), :] = v[:, c * LANES:(c + 1) * LANES]
        return jnp.concatenate([scr[c] for c in range(width // LANES)], axis=1)

    outs = [token_order(r, s) for r, s in zip((o1_ref, o2_ref, o3_ref), o_scrs)]
    l1, l2, l3 = [token_order(r, s) for r, s in zip((l1_ref, l2_ref, l3_ref), l_scrs)]
    mx = jnp.maximum(jnp.maximum(l1, l2), l3)
    es = [jnp.exp(l1 - mx), jnp.exp(l2 - mx), jnp.exp(l3 - mx)]
    inv = 1.0 / (es[0] + es[1] + es[2])
    ya = jnp.zeros((tm, GROUP_W), F32)
    for e, o in zip(es, outs):
        w = e * inv
        w_hi = w.astype(BF16)
        w_lo = (w - w_hi.astype(F32)).astype(BF16)
        w_wide = (jnp.dot(w_hi, spread_ref[...], preferred_element_type=F32)
                  + jnp.dot(w_lo, spread_ref[...], preferred_element_type=F32))
        ya = ya + w_wide * o
    a_out = jnp.dot(ya.astype(BF16), woa_ref[...], preferred_element_type=F32)

    u = ub_ref[...].astype(F32)
    halo = jnp.where(tile > 0, ubh_ref[...].astype(F32), 0.0)
    ext = jnp.concatenate([halo, u], axis=0)
    t_seq = tile * tm + lax.broadcasted_iota(jnp.int32, (tm, 1), 0)
    pooled = []
    for gi, w in enumerate(POOL_WINDOWS):
        sl = slice(gi * POOL_GROUP_DIM, (gi + 1) * POOL_GROUP_DIM)
        acc = ext[:, sl]
        k = 1
        while k < w:
            acc = acc + pltpu.roll(acc, k, axis=0)
            k *= 2
        cnt = jnp.minimum(t_seq + 1, w).astype(F32)
        pg = acc[POOL_HALO:] / cnt - u[:, sl]
        pooled.append(jnp.dot(pg.astype(BF16), poolw_ref[gi], preferred_element_type=F32))
    yb = jnp.concatenate(pooled, axis=1) * pscale_ref[...]
    b_out = jnp.dot(yb.astype(BF16), wob_ref[...], preferred_element_type=F32)
    c_out = jnp.dot(yc_ref[...], woc_ref[...], preferred_element_type=F32)

    g = gates_ref[...].astype(F32)
    mix = (jax.nn.sigmoid(g[:, :D]) * a_out + jax.nn.sigmoid(g[:, D:2 * D]) * b_out
           + jax.nn.sigmoid(g[:, 2 * D:]) * c_out)
    tok = jnp.dot(mix.astype(BF16), wout_ref[...], preferred_element_type=F32)
    xn = x_ref[...] + mod1_ref[0][:, 2 * D:] * tok

    mod2 = mod2_ref[0]
    y = xn * lax.rsqrt(jnp.mean(xn * xn, axis=-1, keepdims=True) + EPS) * g2_ref[...]
    h2 = y * (1.0 + mod2[:, D:2 * D]) + mod2[:, :D]
    h2b = h2.astype(BF16)
    h2a_ref[...], h2b_ref[...] = _pack_row_halves(h2b)
    logit_ref[...] = _nt_dot(rwt_ref[...], h2b)
    hid = _silu(jnp.dot(h2b, sw1_ref[...], preferred_element_type=F32)) * jnp.dot(
        h2b, sw3_ref[...], preferred_element_type=F32)
    shared = jnp.dot(hid.astype(BF16), sw2_ref[...], preferred_element_type=F32)
    xmid_ref[...] = xn + mod2[:, 2 * D:] * shared


def mix_out(x2, gu, dil, yc, mod1, mod2, g2, pool_w, pool_scale, w_oa, w_ob, w_oc, w_out, rwt, sw1, sw3, sw2, seq,
            row0=0):
    D = x2.shape[1]
    N = gu.shape[0]
    tm = 512
    tpb = seq // tm
    tile0 = row0 // tm
    (o1, l1), (o2, l2), (o3, l3) = dil
    row = lambda w, c=0: pl.BlockSpec((tm, w), lambda i: (i, c))
    by_residue = lambda a: pl.BlockSpec(
        (1, a.shape[1], tm // a.shape[1], a.shape[3]), lambda i: (i // tpb, 0, i % tpb, 0))
    heads = HEADS_PER_GROUP_A
    spread = (jnp.arange(LSE_LANES)[:, None] == (jnp.arange(GROUP_W)[None, :] // HEAD_DIM_A) * (LSE_LANES // heads)
              ).astype(BF16)
    const2 = lambda a: pl.BlockSpec(a.shape, lambda i: (0,) * a.ndim, pipeline_mode=pl.Buffered(1))
    modspec = pl.BlockSpec((1, 1, 3 * D), lambda i: (i // tpb, 0, 0))
    ub_col = 3 * D // DB
    halo_spec = pl.BlockSpec(
        (POOL_HALO, DB), lambda i: (jnp.maximum(i * (tm // POOL_HALO) - 1, 0), ub_col))
    weights = [g2.reshape(1, D), pool_w, pool_scale.reshape(1, DB), w_oa, w_ob, w_oc, w_out, rwt, sw1, sw3, sw2,
               spread]
    return pl.pallas_call(
        functools.partial(_mixout_kernel, tiles_per_batch=tpb),
        grid=(N // tm,),
        in_specs=[
            pl.BlockSpec((tm, D), lambda i: (i + tile0, 0)), row(3 * D), row(DB, ub_col), halo_spec,
            by_residue(o1), by_residue(o2), by_residue(o3), by_residue(l1), by_residue(l2), by_residue(l3), row(DC),
            modspec, modspec,
        ] + [const2(a) for a in weights],
        scratch_shapes=[pltpu.VMEM((GROUP_W // LANES, tm, LANES), F32)] * 3
        + [pltpu.VMEM((LSE_LANES // LANES, tm, LANES), F32)] * 3,
        out_specs=[row(D), row(PACK_W), row(PACK_W), pl.BlockSpec((N_EXPERTS, tm), lambda i: (0, i))],
        out_shape=[
            jax.ShapeDtypeStruct((N, D), F32),
            jax.ShapeDtypeStruct((N, PACK_W), jnp.int32),
            jax.ShapeDtypeStruct((N, PACK_W), jnp.int32),
            jax.ShapeDtypeStruct((N_EXPERTS, N), F32),
        ],
        compiler_params=_cp(("parallel",), VMEM_LIMIT),
        name="mix_out",
    )(x2, gu, gu, gu, o1, o2, o3, l1, l2, l3, yc, mod1, mod2, *weights)


def _pick_rows(table, picks):
    G, GS = N_GROUPS, GROUP_SIZE
    eio = lax.broadcasted_iota(jnp.int32, (GS, table.shape[1]), 0)
    rows = []
    for k in range(TOP_K):
        idx = picks[k:k + 1]
        parts = [jnp.where(eio + g * GS == idx, table[g * GS:(g + 1) * GS], 0.0) for g in range(G)]
        rows.append(jnp.sum(functools.reduce(jnp.add, parts), axis=0, keepdims=True))
    return jnp.concatenate(rows, axis=0)


def _route_choose(lg_ref, bias_ref):
    G, GS = N_GROUPS, GROUP_SIZE
    scores = jax.nn.sigmoid(lg_ref[...])
    sel = scores + bias_ref[...]
    tn = sel.shape[1]
    eio = lax.broadcasted_iota(jnp.int32, (GS, tn), 0)
    ninf = -jnp.inf

    gs = []
    for g in range(G):
        v = sel[g * GS:(g + 1) * GS]
        m1 = jnp.max(v, axis=0, keepdims=True)
        i1 = jnp.min(jnp.where(v == m1, eio, GS), axis=0, keepdims=True)
        m2 = jnp.max(jnp.where(eio == i1, ninf, v), axis=0, keepdims=True)
        gs.append(m1 + m2)
    gsm = jnp.concatenate(gs, axis=0)
    gio = lax.broadcasted_iota(jnp.int32, (G, tn), 0)
    rank = jnp.zeros((G, tn), jnp.int32)
    for g2 in range(G):
        beats = (gs[g2] > gsm) | ((gs[g2] == gsm) & (g2 < gio))
        rank = rank + beats.astype(jnp.int32)
    gsel = rank < TOPK_GROUPS

    vs = [jnp.where(gsel[g:g + 1], sel[g * GS:(g + 1) * GS], NEG) for g in range(G)]
    eid = [eio + g * GS for g in range(G)]
    chosen = [jnp.zeros((GS, tn), jnp.bool_) for _ in range(G)]
    picks = []
    for _ in range(TOP_K):
        m = jnp.max(functools.reduce(jnp.maximum, vs), axis=0, keepdims=True)
        idx = jnp.min(functools.reduce(jnp.minimum, [jnp.where(v == m, e, N_EXPERTS) for v, e in zip(vs, eid)]),
                      axis=0, keepdims=True)
        picks.append(idx)
        for g in range(G):
            hit = eid[g] == idx
            chosen[g] = chosen[g] | hit
            vs[g] = jnp.where(hit, ninf, vs[g])
    mask = jnp.concatenate(chosen, axis=0).astype(F32)
    return scores, jnp.concatenate(picks, axis=0), mask


def _route_kernel(lg_ref, bias_ref, tri_ref, dest_ref, w_ref, cnt_ref, run_sc, start_sc, mask_sc, picks_sc,
                  *, slot_block):
    phase = pl.program_id(0)
    step = pl.program_id(1)
    tn = lg_ref.shape[1]
    cols = pl.ds(pl.multiple_of(step * tn, tn), tn)

    @pl.when(phase == 0)
    def _():
        @pl.when(step == 0)
        def _():
            run_sc[...] = jnp.zeros(run_sc.shape, F32)

        scores, picks, mask = _route_choose(lg_ref, bias_ref)
        wk = _pick_rows(scores, picks)
        w_ref[0] = wk / jnp.sum(wk, axis=0, keepdims=True) * ROUTED_SCALE
        dest_ref[0] = jnp.zeros(dest_ref.shape[1:], dest_ref.dtype)
        mask_sc[:, cols] = mask.astype(BF16)
        picks_sc[:, cols] = picks
        run_sc[...] = run_sc[...] + jnp.sum(mask, axis=1, keepdims=True)

    @pl.when(phase == 1)
    def _():
        @pl.when(step == 0)
        def _():
            counts = run_sc[...].astype(jnp.int32)
            cnt_ref[...] = jnp.broadcast_to(counts, cnt_ref.shape)
            shift = slot_block.bit_length() - 1
            padded = lax.shift_left(lax.shift_right_logical(counts + (slot_block - 1), shift), shift).astype(F32)
            r = lax.broadcasted_iota(jnp.int32, (N_EXPERTS, N_EXPERTS), 0)
            c = lax.broadcasted_iota(jnp.int32, (N_EXPERTS, N_EXPERTS), 1)
            as_row = jnp.sum(jnp.where(r == c, padded, 0.0), axis=0, keepdims=True)
            start_sc[...] = jnp.sum(jnp.where(c < r, as_row, 0.0), axis=1, keepdims=True)
            run_sc[...] = jnp.zeros(run_sc.shape, F32)

        mask_b = mask_sc[:, cols]
        mask = mask_b.astype(F32)
        before = jnp.dot(mask_b, tri_ref[...], preferred_element_type=F32) - mask
        slot = start_sc[...] + run_sc[...] + before
        dest_ref[0] = _pick_rows(slot, picks_sc[:, cols]).astype(jnp.int32)
        w_ref[0] = jnp.zeros(w_ref.shape[1:], w_ref.dtype)
        run_sc[...] = run_sc[...] + jnp.sum(mask, axis=1, keepdims=True)


SLOT_BLOCK = 512


def route(logits_t, bias):
    E, N = logits_t.shape
    tn = 1024
    tri = (jnp.arange(tn)[:, None] <= jnp.arange(tn)[None, :]).astype(BF16)
    plane = lambda: pl.BlockSpec((1, TOP_K, tn), lambda p, i: (p, 0, i))
    dest, w, cnt = pl.pallas_call(
        functools.partial(_route_kernel, slot_block=SLOT_BLOCK),
        grid=(2, N // tn),
        in_specs=[
            pl.BlockSpec((E, tn), lambda p, i: (0, i * (1 - p))),
            pl.BlockSpec((E, 1), lambda p, i: (0, 0)),
            pl.BlockSpec((tn, tn), lambda p, i: (0, 0)),
        ],
        out_specs=[plane(), plane(), pl.BlockSpec((E, 128), lambda p, i: (0, 0))],
        out_shape=[
            jax.ShapeDtypeStruct((2, TOP_K, N), jnp.int32),
            jax.ShapeDtypeStruct((2, TOP_K, N), F32),
            jax.ShapeDtypeStruct((E, 128), jnp.int32),
        ],
        scratch_shapes=[pltpu.VMEM((E, 1), F32), pltpu.VMEM((E, 1), F32),
                        pltpu.VMEM((E, N), BF16), pltpu.VMEM((TOP_K, N), jnp.int32)],
        compiler_params=_cp(("arbitrary", "arbitrary")),
        name="route",
    )(logits_t, bias.reshape(E, 1), tri)
    return dest[1], w[0], cnt[:, 0]


def block_tables(counts, n_tokens):
    E = counts.shape[0]
    blk = SLOT_BLOCK
    nblk = (n_tokens * TOP_K + E * blk) // blk
    per_expert = (counts + blk - 1) // blk
    bend = jnp.cumsum(per_expert)
    bstart = bend - per_expert
    b = jnp.arange(nblk, dtype=jnp.int32)[:, None]
    owns = (bstart[None, :] <= b) & (b < bend[None, :])
    blk_e = jnp.minimum(jnp.sum(bend[None, :] <= b, axis=1), E - 1).astype(jnp.int32)
    rows_left = counts[None, :] - (b - bstart[None, :]) * blk
    nvalid = jnp.sum(jnp.where(owns, jnp.clip(rows_left, 0, blk), 0), axis=1)
    first = jnp.concatenate([jnp.ones((1,), jnp.bool_), blk_e[1:] != blk_e[:-1]])
    run_parity = ((jnp.cumsum(first.astype(jnp.int32)) - 1) % 2).astype(jnp.int32)
    later = blk_e[None, :] > blk_e[:, None]
    next_e = jnp.min(jnp.where(later, blk_e[None, :], E), axis=1).astype(jnp.int32)
    return blk_e, nvalid.astype(jnp.int32), run_parity, next_e


def _sc_mesh():
    return plsc.VectorSubcoreMesh(core_axis_name="c", subcore_axis_name="s")


SC_WINDOW = 128


def sc_scatter_rows(x, dest, n_slots):
    N, W = x.shape
    K = dest.shape[0]

    @functools.partial(pl.kernel, out_type=jax.ShapeDtypeStruct((n_slots, W), x.dtype), mesh=_sc_mesh(),
                       scratch_types=[])
    def scatter(x_hbm, i_hbm, o_hbm):
        def body(x_vmem, i_vmem):
            for k in range(K):
                pltpu.sync_copy(x_vmem, o_hbm.at[i_vmem.at[k]])

        pltpu.emit_pipeline(
            body,
            grid=(N // SC_WINDOW,),
            in_specs=[pl.BlockSpec((SC_WINDOW, W), lambda i: (i, 0)),
                      pl.BlockSpec((K, SC_WINDOW), lambda i: (0, i))],
            out_specs=[],
            core_axis_name=("c", "s"),
            dimension_semantics=(pltpu.PARALLEL,),
        )(x_hbm, i_hbm)

    return scatter(x, dest)


SC_LANES = 16
SC_GATHER_TOKENS = 8


def sc_weighted_gather(y, dest, wts):
    W = y.shape[1]
    K, N = dest.shape
    G, L = SC_GATHER_TOKENS, SC_LANES
    batches = SC_WINDOW // G

    @functools.partial(
        pl.kernel, out_type=jax.ShapeDtypeStruct((N, W), y.dtype), mesh=_sc_mesh(),
        scratch_types=[pltpu.VMEM((2, K, G, W), y.dtype), pltpu.SemaphoreType.DMA((2,))],
        compiler_params=pltpu.CompilerParams(needs_layout_passes=False))
    def gather(y_hbm, i_hbm, w_hbm, o_hbm, rows2, sems):
        def body(i_vmem, w_vmem, o_vmem):
            def fetch(batch, slot):
                return [pltpu.make_async_copy(y_hbm.at[i_vmem.at[k, pl.ds(batch * G, G)]], rows2.at[slot, k],
                                              sems.at[slot]) for k in range(K)]

            for c in fetch(0, 0):
                c.start()

            @pl.loop(0, batches)
            def _(batch):
                slot = batch % 2

                @pl.when(batch + 1 < batches)
                def _():
                    for c in fetch(batch + 1, 1 - slot):
                        c.start()

                for c in fetch(batch, slot):
                    c.wait()
                rows = rows2.at[slot]

                @pl.loop(0, G)
                def _(t):
                    tok = jnp.full((L,), batch * G + t, jnp.int32)
                    wk = [plsc.load_gather(w_vmem, [jnp.full((L,), k, jnp.int32), tok]) for k in range(K)]

                    @plsc.parallel_loop(0, W // L, unroll=W // L)
                    def _(j):
                        lo = jnp.zeros((L,), F32)
                        hi = jnp.zeros((L,), F32)
                        for k in range(K):
                            pair = plsc.bitcast(rows[k, t, pl.ds(j * L, L)], BF16)
                            a, b = plsc.unpack(pair, format=plsc.PackFormat.INTERLEAVED)
                            lo = lo + wk[k] * a
                            hi = hi + wk[k] * b
                        o_vmem[batch * G + t, pl.ds(j * L, L)] = plsc.bitcast(
                            plsc.pack(lo, hi, format=plsc.PackFormat.INTERLEAVED), y.dtype)

        pltpu.emit_pipeline(
            body,
            grid=(N // SC_WINDOW,),
            in_specs=[pl.BlockSpec((K, SC_WINDOW), lambda i: (0, i)),
                      pl.BlockSpec((K, SC_WINDOW), lambda i: (0, i))],
            out_specs=[pl.BlockSpec((SC_WINDOW, W), lambda i: (i, 0))],
            core_axis_name=("c", "s"),
            dimension_semantics=(pltpu.PARALLEL,),
        )(i_hbm, w_hbm, o_hbm)

    return gather(y, dest, wts)


EXPERT_INPUT_SLOTS = 3


def _expert_kernel(blk_e_ref, nvalid_ref, parity_ref, next_e_ref, xa_hbm, xb_hbm, w1_hbm, w3_hbm, w2_hbm,
                   ya_hbm, yb_hbm, w1_sc, w3_sc, w2_sc, xa_buf, xb_buf, sems, w1_st, w3_st, w2_st, wsems,
                   ya_st, yb_st, osems, *, layer):
    b = pl.program_id(0)
    nb = pl.num_programs(0)
    nv = nvalid_ref[b]
    prev_e = blk_e_ref[jnp.maximum(b - 1, 0)]
    blk = xa_buf.shape[1]
    ring = EXPERT_INPUT_SLOTS
    n_experts = w1_hbm.shape[1]

    def fetch_weights(e, par):
        return (pltpu.make_async_copy(w1_hbm.at[layer, e], w1_st.at[par], wsems.at[par, 0]),
                pltpu.make_async_copy(w3_hbm.at[layer, e], w3_st.at[par], wsems.at[par, 1]),
                pltpu.make_async_copy(w2_hbm.at[layer, e], w2_st.at[par], wsems.at[par, 2]))

    def fetch(block, slot):
        rows = pl.ds(pl.multiple_of(block * blk, blk), blk)
        return (pltpu.make_async_copy(xa_hbm.at[rows], xa_buf.at[slot], sems.at[slot, 0]),
                pltpu.make_async_copy(xb_hbm.at[rows], xb_buf.at[slot], sems.at[slot, 1]))

    @pl.when(b == 0)
    def _():
        for i in range(ring - 1):
            for c in fetch(i, i):
                c.start()

    ahead = b + (ring - 1)

    @pl.when(ahead < nb)
    def _():
        for c in fetch(ahead, ahead % ring):
            c.start()

    slot = b % ring
    for c in fetch(b, slot):
        c.wait()

    @pl.when(b == 0)
    def _():
        for c in fetch_weights(blk_e_ref[0], 0):
            c.start()

    @pl.when((b == 0) | (blk_e_ref[b] != prev_e))
    def _():
        par = parity_ref[b]
        nxt = next_e_ref[b]

        @pl.when(nxt < n_experts)
        def _():
            for c in fetch_weights(nxt, 1 - par):
                c.start()

        for c in fetch_weights(blk_e_ref[b], par):
            c.wait()
        w1_sc[...] = w1_st[par].astype(BF16)
        w3_sc[...] = w3_st[par].astype(BF16)
        w2_sc[...] = w2_st[par].astype(BF16)

    def put(block, oslot):
        rows = pl.ds(pl.multiple_of(block * blk, blk), blk)
        return (pltpu.make_async_copy(ya_st.at[oslot], ya_hbm.at[rows], osems.at[oslot, 0]),
                pltpu.make_async_copy(yb_st.at[oslot], yb_hbm.at[rows], osems.at[oslot, 1]))

    oslot = b % 2

    @pl.when(b >= 2)
    def _():
        for c in put(b - 2, oslot):
            c.wait()

    @pl.when(nv > 0)
    def _():
        x = _unpack_row_halves(xa_buf[slot], xb_buf[slot])
        rows = lax.broadcasted_iota(jnp.int32, x.shape, 0)
        x = jnp.where(rows < nv, x, 0.0).astype(BF16)
        hid = _silu(jnp.dot(x, w1_sc[...], preferred_element_type=F32)) * jnp.dot(
            x, w3_sc[...], preferred_element_type=F32)
        y = jnp.dot(hid.astype(BF16), w2_sc[...], preferred_element_type=F32)
        ya_st[oslot], yb_st[oslot] = _pack_row_halves(y)

    @pl.when(nv == 0)
    def _():
        ya_st[oslot] = jnp.zeros(ya_st.shape[1:], ya_st.dtype)
        yb_st[oslot] = jnp.zeros(yb_st.shape[1:], yb_st.dtype)

    for c in put(b, oslot):
        c.start()

    @pl.when(b == nb - 1)
    def _():
        @pl.when(nb >= 2)
        def _():
            for c in put(b - 1, 1 - oslot):
                c.wait()

        for c in put(b, oslot):
            c.wait()


def routed_experts(xa, xb, tables, w1, w3, w2, layer):
    P = xa.shape[0]
    blk = SLOT_BLOCK
    _, E, D, FF = w1.shape
    grid_spec = pltpu.PrefetchScalarGridSpec(
        num_scalar_prefetch=len(tables),
        grid=(P // blk,),
        in_specs=[pl.BlockSpec(memory_space=pl.ANY)] * 5,
        out_specs=[pl.BlockSpec(memory_space=pl.ANY)] * 2,
        scratch_shapes=[
            pltpu.VMEM((D, FF), BF16), pltpu.VMEM((D, FF), BF16), pltpu.VMEM((FF, D), BF16),
            pltpu.VMEM((EXPERT_INPUT_SLOTS, blk, PACK_W), jnp.int32),
            pltpu.VMEM((EXPERT_INPUT_SLOTS, blk, PACK_W), jnp.int32),
            pltpu.SemaphoreType.DMA((EXPERT_INPUT_SLOTS, 2)),
            pltpu.VMEM((2, D, FF), F32), pltpu.VMEM((2, D, FF), F32), pltpu.VMEM((2, FF, D), F32),
            pltpu.SemaphoreType.DMA((2, 3)),
            pltpu.VMEM((2, blk, PACK_W), jnp.int32), pltpu.VMEM((2, blk, PACK_W), jnp.int32),
            pltpu.SemaphoreType.DMA((2, 2)),
        ],
    )
    return pl.pallas_call(
        functools.partial(_expert_kernel, layer=layer),
        grid_spec=grid_spec,
        out_shape=[jax.ShapeDtypeStruct((P, PACK_W), jnp.int32)] * 2,
        compiler_params=_cp(("arbitrary",), VMEM_LIMIT),
        name="routed_experts",
    )(*tables, xa, xb, w1, w3, w2)


def _combine_kernel(xmid_ref, ra_ref, rb_ref, mod2_ref, fg_ref, *rest):
    out_ref = rest[-1]
    D = xmid_ref.shape[1]
    x = xmid_ref[...] + mod2_ref[0][:, 2 * D:] * _unpack_row_halves(ra_ref[...], rb_ref[...])
    out_ref[...] = x * lax.rsqrt(jnp.mean(x * x, axis=-1, keepdims=True) + EPS) * fg_ref[...]


def combine(xmid, ra, rb, mod2, final_g, seq, out_rows=None, row0=0, out_buf=None, in_row0=0):
    N, D = ra.shape[0], xmid.shape[1]
    tm = 512
    tpb = seq // tm
    tile0 = row0 // tm
    in_tile0 = in_row0 // tm
    in_specs = [
        pl.BlockSpec((tm, D), lambda i: (i + in_tile0, 0)),
        pl.BlockSpec((tm, PACK_W), lambda i: (i, 0)),
        pl.BlockSpec((tm, PACK_W), lambda i: (i, 0)),
        pl.BlockSpec((1, 1, 3 * D), lambda i: ((i + in_tile0) // tpb, 0, 0)),
        pl.BlockSpec((1, D), lambda i: (0, 0)),
    ]
    args = [xmid, ra, rb, mod2, final_g.reshape(1, D)]
    aliases = {}
    if out_buf is not None:
        in_specs.append(pl.BlockSpec(memory_space=pl.ANY))
        args.append(out_buf)
        aliases = {len(args) - 1: 0}
    return pl.pallas_call(
        _combine_kernel,
        grid=(N // tm,),
        in_specs=in_specs,
        out_specs=pl.BlockSpec((tm, D), lambda i: (i + tile0, 0)),
        out_shape=jax.ShapeDtypeStruct((out_rows or N, D), F32),
        input_output_aliases=aliases,
        compiler_params=_cp(("parallel",), VMEM_LIMIT),
        name="combine",
    )(*args)


TOKEN_STREAMS = 2
LAST_GATHER_SPLITS = 2


def _permute_w_in(w):
    ub = w[:, 3 * DA:3 * DA + DB]
    lat_lo = 3 * DA + DB
    lat_hi = lat_lo + Q_LORA + KV_LORA + QK_ROPE
    lat, gates = w[:, lat_lo:lat_hi], w[:, lat_hi:]
    pad = jnp.zeros((w.shape[0], LAT_W - (lat_hi - lat_lo)), w.dtype)
    parts = [gates, ub, lat, pad]
    for g in range(len(DIL_GROUPS)):
        sl = slice(g * GROUP_W, (g + 1) * GROUP_W)
        parts += [w[:, :DA][:, sl] * (HEAD_DIM_A ** -0.5), w[:, DA:2 * DA][:, sl], w[:, 2 * DA:3 * DA][:, sl]]
    return jnp.concatenate(parts, axis=1).astype(BF16)


def kernel(x, c, positions, ada_mix_w, ada_mix_b, norm_mix_g, w_in, pool_w, pool_scale, cq_norm_g, ckv_norm_g, w_uq, w_ukv, w_oa, w_ob, w_oc, w_out, ada_ffn_w, ada_ffn_b, norm_ffn_g, router_w, router_bias, exp_w1, exp_w3, exp_w2, sh_w1, sh_w3, sh_w2, final_g):
    B, S, D = x.shape
    depth = w_in.shape[0]
    mod_mix = adaln_rows(c, ada_mix_w, ada_mix_b)
    mod_ffn = adaln_rows(c, ada_ffn_w, ada_ffn_b)
    streams = TOKEN_STREAMS if B % TOKEN_STREAMS == 0 else 1
    Bs = B // streams
    Ns = Bs * S
    x_all = x.reshape(B * S, D)
    xs = [None] * streams
    out_all = None
    pos_s = [positions[s * Bs:(s + 1) * Bs] for s in range(streams)]
    for l in range(depth):
        last = l == depth - 1
        w_in_l = _permute_w_in(w_in[l])
        mla_w = _mla_weights(cq_norm_g[l], ckv_norm_g[l], w_uq[l], w_ukv[l])
        mix_w = (norm_ffn_g[l], pool_w[l].astype(BF16), pool_scale[l],
                 w_oa[l].astype(BF16), w_ob[l].astype(BF16), w_oc[l].astype(BF16), w_out[l].astype(BF16),
                 router_w[l].T.astype(BF16), sh_w1[l].astype(BF16), sh_w3[l].astype(BF16), sh_w2[l].astype(BF16))
        for s in range(streams):
            mod1 = mod_mix[l, s * Bs:(s + 1) * Bs].reshape(Bs, 1, 3 * D)
            mod2 = mod_ffn[l, s * Bs:(s + 1) * Bs].reshape(Bs, 1, 3 * D)
            if l == 0:
                x2, row0 = x_all, s * Ns
                gu, lat, *qkv = in_projection(x2, norm_mix_g[l], mod1, w_in_l, S, row0)
            else:
                row0 = 0
                x2, gu, lat, *qkv = in_projection(xs[s][0], norm_mix_g[l], mod1, w_in_l, S, 0, xs[s][1:])
            dil = [dilated_attention(qkv[2 * g], qkv[2 * g + 1]) for g in range(len(DIL_GROUPS))]
            q_all, k_all, vt_all = mla_prep(lat, pos_s[s], *mla_w, Bs, S)
            yc = mla_attention(q_all, k_all, vt_all, Bs, S)
            xmid, h2a, h2b, logits_t = mix_out(x2, gu, dil, yc, mod1, mod2, *mix_w, S, row0)
            dest, w_k, counts = route(logits_t, router_bias[l])
            tables = block_tables(counts, Ns)
            n_slots = tables[0].shape[0] * SLOT_BLOCK
            xa = sc_scatter_rows(h2a, dest, n_slots)
            xb = sc_scatter_rows(h2b, dest, n_slots)
            ya, yb = routed_experts(xa, xb, tables, exp_w1, exp_w3, exp_w2, l)
            if last:
                for t0 in range(0, Ns, Ns // LAST_GATHER_SPLITS):
                    cols = slice(t0, t0 + Ns // LAST_GATHER_SPLITS)
                    ra = sc_weighted_gather(ya, dest[:, cols], w_k[:, cols])
                    rb = sc_weighted_gather(yb, dest[:, cols], w_k[:, cols])
                    out_all = combine(xmid, ra, rb, mod2, final_g, S, B * S, s * Ns + t0, out_all, t0)
            else:
                xs[s] = (xmid, sc_weighted_gather(ya, dest, w_k), sc_weighted_gather(yb, dest, w_k), mod2)
    return out_all.reshape(B, S, D)
```

```python
import functools
import math

import jax
import jax.numpy as jnp
from jax import lax
from jax.experimental import pallas as pl
from jax.experimental.pallas import tpu as pltpu
from jax.experimental.pallas import tpu_sc as plsc

F32 = jnp.float32
BF16 = jnp.bfloat16
HIGHEST = lax.Precision.HIGHEST

D_MODEL = 1024
HEAD_DIM_A = 64
HEADS_PER_GROUP_A = 4
DIL_GROUPS = ((128, 1), (512, 4), (2048, 16))
GROUP_W = HEADS_PER_GROUP_A * HEAD_DIM_A
DA = GROUP_W * len(DIL_GROUPS)
POOL_WINDOWS = (2, 4, 8, 16)
POOL_GROUP_DIM = 128
DB = POOL_GROUP_DIM * len(POOL_WINDOWS)
POOL_HALO = 16
N_HEADS_C = 8
QK_NOPE = 64
QK_ROPE = 32
V_DIM = 64
Q_LORA = 384
KV_LORA = 256
DC = N_HEADS_C * V_DIM
HEAD_PAD_C = 128
ROPE_THETA = 10000.0
N_EXPERTS = 64
TOP_K = 8
N_GROUPS = 8
TOPK_GROUPS = 4
GROUP_SIZE = N_EXPERTS // N_GROUPS
ROUTED_SCALE = 2.5
EPS = 1e-6
NEG = -1e30
Q_BLOCK = 128

LAT_W = 768
GU_W = 3 * D_MODEL + DB
IN_OUT_WIDTHS = (GU_W, LAT_W) + (2 * GROUP_W, GROUP_W) * len(DIL_GROUPS)

VMEM_LIMIT = 56 * 1024 * 1024


def _cp(sem, vmem=None):
    return pltpu.CompilerParams(dimension_semantics=sem, vmem_limit_bytes=vmem)


def _silu(v):
    return v * jax.nn.sigmoid(v)


def _nt_dot(a, b):
    return lax.dot_general(a, b, (((1,), (1,)), ((), ())), preferred_element_type=F32)


PACK_W = D_MODEL // 4
_HI_MASK = -65536


def _bf16_bits(v):
    return lax.bitcast_convert_type(v.astype(BF16).astype(F32), jnp.int32)


def _pack_row_halves(v):
    halves = []
    for h in range(2):
        lo = _bf16_bits(v[:, (2 * h) * PACK_W:(2 * h + 1) * PACK_W])
        hi = _bf16_bits(v[:, (2 * h + 1) * PACK_W:(2 * h + 2) * PACK_W])
        halves.append(lax.shift_right_logical(lo, 16) | (hi & _HI_MASK))
    return halves


def _unpack_row_halves(wa, wb):
    parts = []
    for w in (wa, wb):
        parts.append(lax.bitcast_convert_type(lax.shift_left(w, 16), F32))
        parts.append(lax.bitcast_convert_type(w & _HI_MASK, F32))
    return jnp.concatenate(parts, axis=1)


def _adaln_kernel(c_ref, w_ref, b_ref, o_ref):
    s = _silu(c_ref[...])
    o_ref[0] = jnp.dot(s, w_ref[0], preferred_element_type=F32, precision=HIGHEST) + b_ref[0]


def adaln_rows(c, w, b):
    L, D, D3 = w.shape
    B = c.shape[0]
    tn = 1024
    return pl.pallas_call(
        _adaln_kernel,
        grid=(L, D3 // tn),
        in_specs=[
            pl.BlockSpec((B, D), lambda l, j: (0, 0)),
            pl.BlockSpec((1, D, tn), lambda l, j: (l, 0, j)),
            pl.BlockSpec((1, 1, tn), lambda l, j: (l, 0, j)),
        ],
        out_specs=pl.BlockSpec((1, B, tn), lambda l, j: (l, 0, j)),
        out_shape=jax.ShapeDtypeStruct((L, B, D3), F32),
        compiler_params=_cp(("parallel", "parallel")),
        name="adaln_rows",
    )(c, w, b.reshape(L, 1, D3))


LANES = 128


def _inproj_kernel(x_ref, g_ref, mod_ref, w_ref, *refs, chunk, pending):
    o_refs, scr = refs[:-1], refs[-1]
    D = x_ref.shape[1]
    x = x_ref[...]
    if pending:
        ra_ref, rb_ref, gate_ref, x_out_ref, *o_refs = o_refs
        x = x + gate_ref[0][:, 2 * D:] * _unpack_row_halves(ra_ref[...], rb_ref[...])
        x_out_ref[...] = x
    y = x * lax.rsqrt(jnp.mean(x * x, axis=-1, keepdims=True) + EPS) * g_ref[...]
    mod = mod_ref[0]
    h = (y * (1.0 + mod[:, D:2 * D]) + mod[:, :D]).astype(BF16)
    col = 0
    for o_ref in o_refs:
        width = o_ref.shape[-1]
        if o_ref.ndim == 2:
            for c0 in range(0, width, chunk):
                cw = min(chunk, width - c0)
                o_ref[:, c0:c0 + cw] = jnp.dot(
                    h, w_ref[:, col + c0:col + c0 + cw], preferred_element_type=F32).astype(o_ref.dtype)
        else:
            dil, rows = o_ref.shape[1], o_ref.shape[2]
            z = jnp.dot(h, w_ref[:, col:col + width], preferred_element_type=F32)
            if dil == 1:
                o_ref[0, 0] = z.astype(o_ref.dtype)
            else:
                for c in range(width // LANES):
                    scr[c] = z[:, c * LANES:(c + 1) * LANES]
                for r in range(dil):
                    o_ref[0, r] = jnp.concatenate(
                        [scr[c, pl.ds(r, rows, stride=dil), :] for c in range(width // LANES)],
                        axis=1).astype(o_ref.dtype)
        col += width


def in_projection(x2, g, mod, w, seq, row0=0, pending=None):
    D = x2.shape[1]
    B = mod.shape[0]
    N = B * seq
    tm = 512
    tpb = seq // tm
    tile0 = row0 // tm
    out_specs = [pl.BlockSpec((tm, wd), lambda i: (i, 0)) for wd in IN_OUT_WIDTHS[:2]]
    out_shape = [jax.ShapeDtypeStruct((N, wd), BF16) for wd in IN_OUT_WIDTHS[:2]]
    for grp, (_, dil) in enumerate(DIL_GROUPS):
        for wd in IN_OUT_WIDTHS[2 + 2 * grp:4 + 2 * grp]:
            out_specs.append(pl.BlockSpec((1, dil, tm // dil, wd), lambda i: (i // tpb, 0, i % tpb, 0)))
            out_shape.append(jax.ShapeDtypeStruct((B, dil, seq // dil, wd), BF16))
    in_specs = [
        pl.BlockSpec((tm, D), lambda i: (i + tile0, 0)),
        pl.BlockSpec((1, D), lambda i: (0, 0)),
        pl.BlockSpec((1, 1, 3 * D), lambda i: (i // tpb, 0, 0)),
        pl.BlockSpec(w.shape, lambda i: (0, 0), pipeline_mode=pl.Buffered(1)),
    ]
    args = [x2, g.reshape(1, D), mod, w]
    if pending is not None:
        in_specs += [pl.BlockSpec((tm, PACK_W), lambda i: (i, 0)), pl.BlockSpec((tm, PACK_W), lambda i: (i, 0)),
                     pl.BlockSpec((1, 1, 3 * D), lambda i: (i // tpb, 0, 0))]
        args += list(pending)
        out_specs.insert(0, pl.BlockSpec((tm, D), lambda i: (i, 0)))
        out_shape.insert(0, jax.ShapeDtypeStruct((N, D), F32))
    return pl.pallas_call(
        functools.partial(_inproj_kernel, chunk=512, pending=pending is not None),
        grid=(N // tm,),
        in_specs=in_specs,
        out_specs=out_specs,
        out_shape=out_shape,
        scratch_shapes=[pltpu.VMEM((max(IN_OUT_WIDTHS[2:]) // LANES, tm, LANES), F32)],
        compiler_params=_cp(("parallel",), VMEM_LIMIT),
        name="in_projection",
    )(*args)


def _dilated_kernel(q_ref, kc_ref, kp_ref, vc_ref, vp_ref, o_ref, lse_ref):
    i = pl.program_id(1)
    T = Q_BLOCK
    key = lax.broadcasted_iota(jnp.int32, (T, T), 0)
    qry = lax.broadcasted_iota(jnp.int32, (T, T), 1)
    valid_c = key <= qry
    near = key >= qry
    seqs, run = q_ref.shape[0], q_ref.shape[1] // T
    heads = [slice(h * HEAD_DIM_A, (h + 1) * HEAD_DIM_A) for h in range(HEADS_PER_GROUP_A)]

    def transposed(v):
        return v.astype(F32).T.astype(BF16)

    vts = {(s, j): transposed(vc_ref[s, j * T:(j + 1) * T, :]) for s in range(seqs) for j in range(run)}
    vt_before = [transposed(vp_ref[s]) for s in range(seqs)]

    def blocks(s, j):
        rows = slice(j * T, (j + 1) * T)
        if j == 0:
            return rows, kc_ref[s, rows, :], vts[s, 0], kp_ref[s], vt_before[s], near & (i > 0)
        before = slice((j - 1) * T, j * T)
        return rows, kc_ref[s, rows, :], vts[s, j], kc_ref[s, before, :], vts[s, j - 1], near

    scores, probs = {}, {}
    for s in range(seqs):
        for j in range(run):
            rows, kc, _, kp, _, valid_p = blocks(s, j)
            q = q_ref[s, rows, :]
            for h, sl in enumerate(heads):
                qh = q[:, sl]
                scores[s, j, h] = (jnp.where(valid_c, _nt_dot(kc[:, sl], qh), NEG),
                                   jnp.where(valid_p, _nt_dot(kp[:, sl], qh), NEG))
    for chain, (sc, sp) in scores.items():
        m = jnp.maximum(jnp.max(sc, axis=0, keepdims=True), jnp.max(sp, axis=0, keepdims=True))
        pc = jnp.exp(sc - m)
        pp = jnp.exp(sp - m)
        den = jnp.sum(pc, axis=0, keepdims=True) + jnp.sum(pp, axis=0, keepdims=True)
        probs[chain] = (pc.astype(BF16), pp.astype(BF16), den, m + jnp.log(den))
    spread = LSE_LANES // len(heads)
    for s in range(seqs):
        for j in range(run):
            rows, _, vtc, _, vtp, _ = blocks(s, j)
            outs = []
            for h, sl in enumerate(heads):
                pc, pp, den, _ = probs[s, j, h]
                o = (jnp.dot(vtc[sl, :], pc, preferred_element_type=F32)
                     + jnp.dot(vtp[sl, :], pp, preferred_element_type=F32))
                outs.append(o / den)
            o_ref[s, rows, :] = jnp.concatenate(outs, axis=0).T.astype(o_ref.dtype)
            lse_t = jnp.concatenate(
                [jnp.broadcast_to(probs[s, j, h][3], (spread, T)) for h in range(len(heads))], axis=0)
            lse_ref[s, rows, :] = lse_t.T


DILATED_RUN = 16


LSE_LANES = 128


def dilated_attention(qk, v):
    batch, dilation, L, _ = qk.shape
    nb = L // Q_BLOCK
    run = min(DILATED_RUN, nb)
    seqs = DILATED_RUN // run
    qk_r = qk.reshape(batch * dilation, L, 2 * GROUP_W)
    v_r = v.reshape(batch * dilation, L, GROUP_W)
    before = lambda i: jnp.maximum(i * run - 1, 0)
    o, lse = pl.pallas_call(
        _dilated_kernel,
        grid=(batch * dilation // seqs, nb // run),
        in_specs=[
            pl.BlockSpec((seqs, run * Q_BLOCK, GROUP_W), lambda s, i: (s, i, 0)),
            pl.BlockSpec((seqs, run * Q_BLOCK, GROUP_W), lambda s, i: (s, i, 1)),
            pl.BlockSpec((seqs, Q_BLOCK, GROUP_W), lambda s, i: (s, before(i), 1)),
            pl.BlockSpec((seqs, run * Q_BLOCK, GROUP_W), lambda s, i: (s, i, 0)),
            pl.BlockSpec((seqs, Q_BLOCK, GROUP_W), lambda s, i: (s, before(i), 0)),
        ],
        out_specs=[
            pl.BlockSpec((seqs, run * Q_BLOCK, GROUP_W), lambda s, i: (s, i, 0)),
            pl.BlockSpec((seqs, run * Q_BLOCK, LSE_LANES), lambda s, i: (s, i, 0)),
        ],
        out_shape=[
            jax.ShapeDtypeStruct((batch * dilation, L, GROUP_W), BF16),
            jax.ShapeDtypeStruct((batch * dilation, L, LSE_LANES), F32),
        ],
        compiler_params=_cp(("parallel", "parallel")),
        name=f"dilated_attention_d{dilation}",
    )(qk_r, qk_r, qk_r, v_r, v_r)
    return o.reshape(batch, dilation, L, GROUP_W), lse.reshape(batch, dilation, L, LSE_LANES)


def _mla_prep_kernel(lat_ref, pos_ref, gq_ref, gkv_ref, wq_ref, wk_ref, wvt_ref, freq_ref, spread_ref, one_ref,
                     q_ref, k_ref, vt_ref):
    HP = N_HEADS_C * HEAD_PAD_C
    lat = lat_ref[...].astype(F32)
    cq = lat[:, :Q_LORA]
    ckr = lat[:, Q_LORA:]
    zq = (cq * lax.rsqrt(jnp.mean(cq * cq, axis=-1, keepdims=True) + EPS) * gq_ref[...]).astype(BF16)
    lane = lax.broadcasted_iota(jnp.int32, ckr.shape, 1)
    is_kv = lane < KV_LORA
    ms = jnp.sum(jnp.where(is_kv, ckr * ckr, 0.0), axis=-1, keepdims=True) * (1.0 / KV_LORA)
    zkv = (ckr * jnp.where(is_kv, lax.rsqrt(ms + EPS) * gkv_ref[...], 1.0)).astype(BF16)
    qq = jnp.dot(zq, wq_ref[...], preferred_element_type=F32)
    kk = jnp.dot(zkv, wk_ref[:, :HP], preferred_element_type=F32)
    kk_sw = jnp.dot(zkv[:, KV_LORA:], wk_ref[KV_LORA:, HP:], preferred_element_type=F32)
    ang_t = freq_ref[...] * pos_ref[0].astype(F32)

    def to_lanes(t):
        hi = t.astype(BF16)
        lo = (t - hi.astype(F32)).astype(BF16)
        tn_dot = lambda a: lax.dot_general(a, spread_ref[...], (((0,), (0,)), ((), ())), preferred_element_type=F32)
        return tn_dot(hi) + tn_dot(lo)

    cos = to_lanes(jnp.cos(ang_t)) + one_ref[...]
    sin = to_lanes(jnp.sin(ang_t))
    slot_lane = lax.broadcasted_iota(jnp.int32, (1, HEAD_PAD_C), 1)
    half = QK_ROPE // 2
    sin_x1 = jnp.where((slot_lane >= QK_NOPE) & (slot_lane < QK_NOPE + half), -sin, 0.0)
    sin_x2 = jnp.where((slot_lane >= QK_NOPE + half) & (slot_lane < QK_NOPE + QK_ROPE), sin, 0.0)
    for h in range(N_HEADS_C):
        lo, hi = h * HEAD_PAD_C, (h + 1) * HEAD_PAD_C
        qh = qq[:, lo:hi]
        q_ref[:, lo:hi] = (qh * cos + pltpu.roll(qh, HEAD_PAD_C - half, axis=1) * sin_x1
                           + pltpu.roll(qh, half, axis=1) * sin_x2).astype(q_ref.dtype)
        k_ref[:, lo:hi] = (kk[:, lo:hi] * cos + kk_sw[:, lo:hi] * sin).astype(k_ref.dtype)
    vt_ref[0] = _nt_dot(wvt_ref[...], zkv).astype(vt_ref.dtype)


def _mla_weights(cq_g, ckv_g, w_uq, w_ukv):
    H, HPAD, half = N_HEADS_C, HEAD_PAD_C, QK_ROPE // 2
    scale = (QK_NOPE + QK_ROPE) ** -0.5 * math.log2(math.e)
    wq = w_uq.reshape(Q_LORA, H, QK_NOPE + QK_ROPE) * scale
    wq_big = jnp.pad(wq, ((0, 0), (0, 0), (0, HPAD - QK_NOPE - QK_ROPE))).reshape(Q_LORA, H * HPAD)

    rows = LAT_W - Q_LORA
    wkv = w_ukv.reshape(KV_LORA, H, QK_NOPE + V_DIM)
    eye = jnp.eye(QK_ROPE, dtype=F32)
    k_lin = jnp.zeros((rows, H, HPAD), F32)
    k_lin = k_lin.at[:KV_LORA, :, :QK_NOPE].set(wkv[..., :QK_NOPE])
    k_lin = k_lin.at[KV_LORA:KV_LORA + QK_ROPE, :, QK_NOPE:QK_NOPE + QK_ROPE].set(
        jnp.broadcast_to(eye[:, None, :], (QK_ROPE, H, QK_ROPE)))
    swap = jnp.zeros((QK_ROPE, QK_ROPE), F32).at[half:, :half].set(-jnp.eye(half)).at[:half, half:].set(jnp.eye(half))
    k_sw = jnp.zeros((rows, H, HPAD), F32)
    k_sw = k_sw.at[KV_LORA:KV_LORA + QK_ROPE, :, QK_NOPE:QK_NOPE + QK_ROPE].set(
        jnp.broadcast_to(swap[:, None, :], (QK_ROPE, H, QK_ROPE)))
    v_w = jnp.zeros((rows, H, V_DIM), F32).at[:KV_LORA].set(wkv[..., QK_NOPE:])
    wk_big = jnp.concatenate([k_lin.reshape(rows, H * HPAD), k_sw.reshape(rows, H * HPAD)], axis=1)
    wv_t = v_w.reshape(rows, H * V_DIM).T

    gkv = jnp.concatenate([ckv_g, jnp.ones((rows - KV_LORA,), F32)]).reshape(1, rows)
    return cq_g.reshape(1, Q_LORA), gkv, wq_big.astype(BF16), wk_big.astype(BF16), wv_t.astype(BF16)


def _rope_tables():
    half = QK_ROPE // 2
    freqs = (ROPE_THETA ** (-jnp.arange(0, QK_ROPE, 2, dtype=F32) / QK_ROPE)).reshape(half, 1)
    lane = jnp.arange(HEAD_PAD_C)[None, :]
    j = jnp.arange(half)[:, None]
    spread = (lane == QK_NOPE + j) | (lane == QK_NOPE + half + j)
    off_rope = ~jnp.any(spread, axis=0, keepdims=True)
    return freqs, spread.astype(BF16), off_rope.astype(F32)


def mla_prep(lat, positions, gq, gkv, wq_big, wk_big, wv_t, batch, seq):
    N = lat.shape[0]
    HP = N_HEADS_C * HEAD_PAD_C
    tm = 512
    tpb = seq // tm
    freqs, spread, off_rope = _rope_tables()
    pos_rows = positions.reshape(N // tm, 1, tm)
    const = lambda shape: pl.BlockSpec(shape, lambda i: (0, 0))
    return pl.pallas_call(
        _mla_prep_kernel,
        grid=(N // tm,),
        in_specs=[
            pl.BlockSpec((tm, LAT_W), lambda i: (i, 0)),
            pl.BlockSpec((1, 1, tm), lambda i: (i, 0, 0)),
            const(gq.shape), const(gkv.shape), const(wq_big.shape), const(wk_big.shape), const(wv_t.shape),
            const(freqs.shape), const(spread.shape), const(off_rope.shape),
        ],
        out_specs=[
            pl.BlockSpec((tm, HP), lambda i: (i, 0)),
            pl.BlockSpec((tm, HP), lambda i: (i, 0)),
            pl.BlockSpec((1, DC, tm), lambda i: (i // tpb, 0, i % tpb)),
        ],
        out_shape=[
            jax.ShapeDtypeStruct((N, HP), BF16),
            jax.ShapeDtypeStruct((N, HP), BF16),
            jax.ShapeDtypeStruct((batch, DC, seq), BF16),
        ],
        compiler_params=_cp(("parallel",), VMEM_LIMIT),
        name="mla_prep",
    )(lat, pos_rows, gq, gkv, wq_big, wk_big, wv_t, freqs, spread, off_rope)


HEADS_PER_STEP_C = 8
FLASH_Q_CHUNK = 256


def _mla_flash_kernel(qi_ref, ki_ref, q_ref, k_ref, vt_ref, o_ref, m_sc, l_sc, acc_sc):
    t = pl.program_id(2)
    qi, ki = qi_ref[t], ki_ref[t]

    @pl.when(ki == 0)
    def _():
        m_sc[...] = jnp.full(m_sc.shape, NEG, F32)
        l_sc[...] = jnp.zeros(l_sc.shape, F32)
        acc_sc[...] = jnp.zeros(acc_sc.shape, F32)

    def step(masked):
        T = q_ref.shape[1]
        if masked:
            key = lax.broadcasted_iota(jnp.int32, (T, T), 0)
            qry = lax.broadcasted_iota(jnp.int32, (T, T), 1)
            keep = key <= qry
        chains = [(h, c) for h in range(HEADS_PER_STEP_C) for c in range(T // FLASH_Q_CHUNK)]
        scores, probs, alphas = {}, {}, {}

        def keys_for(c):
            return (c + 1) * FLASH_Q_CHUNK if masked else T

        def qk(h, c):
            qs = slice(c * FLASH_Q_CHUNK, (c + 1) * FLASH_Q_CHUNK)
            q = q_ref[0, qs, h * HEAD_PAD_C:(h + 1) * HEAD_PAD_C]
            k = k_ref[0, :keys_for(c), h * HEAD_PAD_C:(h + 1) * HEAD_PAD_C]
            st = _nt_dot(k, q)
            scores[h, c] = jnp.where(keep[:keys_for(c), qs], st, NEG) if masked else st

        def softmax(h, c):
            qs = slice(c * FLASH_Q_CHUNK, (c + 1) * FLASH_Q_CHUNK)
            st = scores.pop((h, c))
            m_prev = m_sc[h, :, qs]
            m_new = jnp.maximum(m_prev, jnp.max(st, axis=0, keepdims=True))
            alpha = jnp.exp2(m_prev - m_new)
            p = jnp.exp2(st - m_new)
            l_sc[h, :, qs] = alpha * l_sc[h, :, qs] + jnp.sum(p, axis=0, keepdims=True)
            m_sc[h, :, qs] = m_new
            probs[h, c], alphas[h, c] = p.astype(BF16), alpha

        def pv(h, c):
            qs = slice(c * FLASH_Q_CHUNK, (c + 1) * FLASH_Q_CHUNK)
            vt = vt_ref[0, h * V_DIM:(h + 1) * V_DIM, :keys_for(c)]
            acc_sc[h, :, qs] = alphas.pop((h, c)) * acc_sc[h, :, qs] + jnp.dot(
                vt, probs.pop((h, c)), preferred_element_type=F32)

        for phase in (qk, softmax, pv):
            for ch in chains:
                phase(*ch)

    @pl.when(ki < qi)
    def _():
        step(False)

    @pl.when(ki == qi)
    def _():
        step(True)
        ot = jnp.concatenate([acc_sc[h] / l_sc[h] for h in range(HEADS_PER_STEP_C)], axis=0)
        o_ref[0] = ot.T.astype(o_ref.dtype)


def mla_attention(q_all, k_all, vt_all, batch, seq):
    T = 512
    nq = seq // T
    pairs = [(a, b) for a in range(nq) for b in range(a + 1)]
    qi_tab = jnp.asarray([p[0] for p in pairs], jnp.int32)
    ki_tab = jnp.asarray([p[1] for p in pairs], jnp.int32)
    hp = N_HEADS_C // HEADS_PER_STEP_C
    qw = HEADS_PER_STEP_C * HEAD_PAD_C
    vw = HEADS_PER_STEP_C * V_DIM
    q3 = q_all.reshape(batch, seq, -1)
    k3 = k_all.reshape(batch, seq, -1)
    grid_spec = pltpu.PrefetchScalarGridSpec(
        num_scalar_prefetch=2,
        grid=(batch, hp, len(pairs)),
        in_specs=[
            pl.BlockSpec((1, T, qw), lambda b, h, t, qi, ki: (b, qi[t], h)),
            pl.BlockSpec((1, T, qw), lambda b, h, t, qi, ki: (b, ki[t], h)),
            pl.BlockSpec((1, vw, T), lambda b, h, t, qi, ki: (b, h, ki[t])),
        ],
        out_specs=pl.BlockSpec((1, T, vw), lambda b, h, t, qi, ki: (b, qi[t], h)),
        scratch_shapes=[
            pltpu.VMEM((HEADS_PER_STEP_C, 1, T), F32),
            pltpu.VMEM((HEADS_PER_STEP_C, 1, T), F32),
            pltpu.VMEM((HEADS_PER_STEP_C, V_DIM, T), F32),
        ],
    )
    o = pl.pallas_call(
        _mla_flash_kernel,
        grid_spec=grid_spec,
        out_shape=jax.ShapeDtypeStruct((batch, seq, DC), BF16),
        compiler_params=_cp(("parallel", "parallel", "arbitrary")),
        name="mla_attention",
    )(qi_tab, ki_tab, q3, k3, vt_all)
    return o.reshape(batch * seq, DC)


def _mixout_kernel(x_ref, gates_ref, ub_ref, ubh_ref, o1_ref, o2_ref, o3_ref, l1_ref, l2_ref, l3_ref, yc_ref,
                   mod1_ref, mod2_ref, g2_ref, poolw_ref, pscale_ref, woa_ref, wob_ref, woc_ref, wout_ref,
                   rwt_ref, sw1_ref, sw3_ref, sw2_ref, spread_ref,
                   xmid_ref, h2a_ref, h2b_ref, logit_ref, *scratch, tiles_per_batch):
    D = x_ref.shape[1]
    tm = x_ref.shape[0]
    tile = pl.program_id(0) % tiles_per_batch
    o_scrs, l_scrs = scratch[:3], scratch[3:]

    def token_order(ref, scr):
        dil, rows, width = ref.shape[1:]
        if dil == 1:
            return ref[0, 0].astype(F32)
        for r in range(dil):
            v = ref[0, r].astype(F32)
            for c in range(width // LANES):
                scr[c, pl.ds(r, rows, stride=dil), :] = v[:, c * LANES:(c + 1) * LANES]
        return jnp.concatenate([scr[c] for c in range(width // LANES)], axis=1)

    outs = [token_order(r, s) for r, s in zip((o1_ref, o2_ref, o3_ref), o_scrs)]
    l1, l2, l3 = [token_order(r, s) for r, s in zip((l1_ref, l2_ref, l3_ref), l_scrs)]
    mx = jnp.maximum(jnp.maximum(l1, l2), l3)
    es = [jnp.exp(l1 - mx), jnp.exp(l2 - mx), jnp.exp(l3 - mx)]
    inv = 1.0 / (es[0] + es[1] + es[2])
    ya = jnp.zeros((tm, GROUP_W), F32)
    for e, o in zip(es, outs):
        w = e * inv
        w_hi = w.astype(BF16)
        w_lo = (w - w_hi.astype(F32)).astype(BF16)
        w_wide = (jnp.dot(w_hi, spread_ref[...], preferred_element_type=F32)
                  + jnp.dot(w_lo, spread_ref[...], preferred_element_type=F32))
        ya = ya + w_wide * o
    a_out = jnp.dot(ya.astype(BF16), woa_ref[...], preferred_element_type=F32)

    u = ub_ref[...].astype(F32)
    halo = jnp.where(tile > 0, ubh_ref[...].astype(F32), 0.0)
    ext = jnp.concatenate([halo, u], axis=0)
    t_seq = tile * tm + lax.broadcasted_iota(jnp.int32, (tm, 1), 0)
    pooled = []
    for gi, w in enumerate(POOL_WINDOWS):
        sl = slice(gi * POOL_GROUP_DIM, (gi + 1) * POOL_GROUP_DIM)
        acc = ext[:, sl]
        k = 1
        while k < w:
            acc = acc + pltpu.roll(acc, k, axis=0)
            k *= 2
        cnt = jnp.minimum(t_seq + 1, w).astype(F32)
        pg = acc[POOL_HALO:] / cnt - u[:, sl]
        pooled.append(jnp.dot(pg.astype(BF16), poolw_ref[gi], preferred_element_type=F32))
    yb = jnp.concatenate(pooled, axis=1) * pscale_ref[...]
    b_out = jnp.dot(yb.astype(BF16), wob_ref[...], preferred_element_type=F32)
    c_out = jnp.dot(yc_ref[...], woc_ref[...], preferred_element_type=F32)

    g = gates_ref[...].astype(F32)
    mix = (jax.nn.sigmoid(g[:, :D]) * a_out + jax.nn.sigmoid(g[:, D:2 * D]) * b_out
           + jax.nn.sigmoid(g[:, 2 * D:]) * c_out)
    tok = jnp.dot(mix.astype(BF16), wout_ref[...], preferred_element_type=F32)
    xn = x_ref[...] + mod1_ref[0][:, 2 * D:] * tok

    mod2 = mod2_ref[0]
    y = xn * lax.rsqrt(jnp.mean(xn * xn, axis=-1, keepdims=True) + EPS) * g2_ref[...]
    h2 = y * (1.0 + mod2[:, D:2 * D]) + mod2[:, :D]
    h2b = h2.astype(BF16)
    h2a_ref[...], h2b_ref[...] = _pack_row_halves(h2b)
    logit_ref[...] = _nt_dot(rwt_ref[...], h2b)
    hid = _silu(jnp.dot(h2b, sw1_ref[...], preferred_element_type=F32)) * jnp.dot(
        h2b, sw3_ref[...], preferred_element_type=F32)
    shared = jnp.dot(hid.astype(BF16), sw2_ref[...], preferred_element_type=F32)
    xmid_ref[...] = xn + mod2[:, 2 * D:] * shared


def mix_out(x2, gu, dil, yc, mod1, mod2, g2, pool_w, pool_scale, w_oa, w_ob, w_oc, w_out, rwt, sw1, sw3, sw2, seq,
            row0=0):
    D = x2.shape[1]
    N = gu.shape[0]
    tm = 512
    tpb = seq // tm
    tile0 = row0 // tm
    (o1, l1), (o2, l2), (o3, l3) = dil
    row = lambda w, c=0: pl.BlockSpec((tm, w), lambda i: (i, c))
    by_residue = lambda a: pl.BlockSpec(
        (1, a.shape[1], tm // a.shape[1], a.shape[3]), lambda i: (i // tpb, 0, i % tpb, 0))
    heads = HEADS_PER_GROUP_A
    spread = (jnp.arange(LSE_LANES)[:, None] == (jnp.arange(GROUP_W)[None, :] // HEAD_DIM_A) * (LSE_LANES // heads)
              ).astype(BF16)
    const2 = lambda a: pl.BlockSpec(a.shape, lambda i: (0,) * a.ndim, pipeline_mode=pl.Buffered(1))
    modspec = pl.BlockSpec((1, 1, 3 * D), lambda i: (i // tpb, 0, 0))
    ub_col = 3 * D // DB
    halo_spec = pl.BlockSpec(
        (POOL_HALO, DB), lambda i: (jnp.maximum(i * (tm // POOL_HALO) - 1, 0), ub_col))
    weights = [g2.reshape(1, D), pool_w, pool_scale.reshape(1, DB), w_oa, w_ob, w_oc, w_out, rwt, sw1, sw3, sw2,
               spread]
    return pl.pallas_call(
        functools.partial(_mixout_kernel, tiles_per_batch=tpb),
        grid=(N // tm,),
        in_specs=[
            pl.BlockSpec((tm, D), lambda i: (i + tile0, 0)), row(3 * D), row(DB, ub_col), halo_spec,
            by_residue(o1), by_residue(o2), by_residue(o3), by_residue(l1), by_residue(l2), by_residue(l3), row(DC),
            modspec, modspec,
        ] + [const2(a) for a in weights],
        scratch_shapes=[pltpu.VMEM((GROUP_W // LANES, tm, LANES), F32)] * 3
        + [pltpu.VMEM((LSE_LANES // LANES, tm, LANES), F32)] * 3,
        out_specs=[row(D), row(PACK_W), row(PACK_W), pl.BlockSpec((N_EXPERTS, tm), lambda i: (0, i))],
        out_shape=[
            jax.ShapeDtypeStruct((N, D), F32),
            jax.ShapeDtypeStruct((N, PACK_W), jnp.int32),
            jax.ShapeDtypeStruct((N, PACK_W), jnp.int32),
            jax.ShapeDtypeStruct((N_EXPERTS, N), F32),
        ],
        compiler_params=_cp(("parallel",), VMEM_LIMIT),
        name="mix_out",
    )(x2, gu, gu, gu, o1, o2, o3, l1, l2, l3, yc, mod1, mod2, *weights)


def _pick_rows(table, picks):
    G, GS = N_GROUPS, GROUP_SIZE
    eio = lax.broadcasted_iota(jnp.int32, (GS, table.shape[1]), 0)
    rows = []
    for k in range(TOP_K):
        idx = picks[k:k + 1]
        parts = [jnp.where(eio + g * GS == idx, table[g * GS:(g + 1) * GS], 0.0) for g in range(G)]
        rows.append(jnp.sum(functools.reduce(jnp.add, parts), axis=0, keepdims=True))
    return jnp.concatenate(rows, axis=0)


def _route_choose(lg_ref, bias_ref):
    G, GS = N_GROUPS, GROUP_SIZE
    scores = jax.nn.sigmoid(lg_ref[...])
    sel = scores + bias_ref[...]
    tn = sel.shape[1]
    eio = lax.broadcasted_iota(jnp.int32, (GS, tn), 0)
    ninf = -jnp.inf

    gs = []
    for g in range(G):
        v = sel[g * GS:(g + 1) * GS]
        m1 = jnp.max(v, axis=0, keepdims=True)
        i1 = jnp.min(jnp.where(v == m1, eio, GS), axis=0, keepdims=True)
        m2 = jnp.max(jnp.where(eio == i1, ninf, v), axis=0, keepdims=True)
        gs.append(m1 + m2)
    gsm = jnp.concatenate(gs, axis=0)
    gio = lax.broadcasted_iota(jnp.int32, (G, tn), 0)
    rank = jnp.zeros((G, tn), jnp.int32)
    for g2 in range(G):
        beats = (gs[g2] > gsm) | ((gs[g2] == gsm) & (g2 < gio))
        rank = rank + beats.astype(jnp.int32)
    gsel = rank < TOPK_GROUPS

    vs = [jnp.where(gsel[g:g + 1], sel[g * GS:(g + 1) * GS], NEG) for g in range(G)]
    eid = [eio + g * GS for g in range(G)]
    chosen = [jnp.zeros((GS, tn), jnp.bool_) for _ in range(G)]
    picks = []
    for _ in range(TOP_K):
        m = jnp.max(functools.reduce(jnp.maximum, vs), axis=0, keepdims=True)
        idx = jnp.min(functools.reduce(jnp.minimum, [jnp.where(v == m, e, N_EXPERTS) for v, e in zip(vs, eid)]),
                      axis=0, keepdims=True)
        picks.append(idx)
        for g in range(G):
            hit = eid[g] == idx
            chosen[g] = chosen[g] | hit
            vs[g] = jnp.where(hit, ninf, vs[g])
    mask = jnp.concatenate(chosen, axis=0).astype(F32)
    return scores, jnp.concatenate(picks, axis=0), mask


def _route_kernel(lg_ref, bias_ref, tri_ref, dest_ref, w_ref, cnt_ref, run_sc, start_sc, mask_sc, picks_sc,
                  *, slot_block):
    phase = pl.program_id(0)
    step = pl.program_id(1)
    tn = lg_ref.shape[1]
    cols = pl.ds(pl.multiple_of(step * tn, tn), tn)

    @pl.when(phase == 0)
    def _():
        @pl.when(step == 0)
        def _():
            run_sc[...] = jnp.zeros(run_sc.shape, F32)

        scores, picks, mask = _route_choose(lg_ref, bias_ref)
        wk = _pick_rows(scores, picks)
        w_ref[0] = wk / jnp.sum(wk, axis=0, keepdims=True) * ROUTED_SCALE
        dest_ref[0] = jnp.zeros(dest_ref.shape[1:], dest_ref.dtype)
        mask_sc[:, cols] = mask.astype(BF16)
        picks_sc[:, cols] = picks
        run_sc[...] = run_sc[...] + jnp.sum(mask, axis=1, keepdims=True)

    @pl.when(phase == 1)
    def _():
        @pl.when(step == 0)
        def _():
            counts = run_sc[...].astype(jnp.int32)
            cnt_ref[...] = jnp.broadcast_to(counts, cnt_ref.shape)
            shift = slot_block.bit_length() - 1
            padded = lax.shift_left(lax.shift_right_logical(counts + (slot_block - 1), shift), shift).astype(F32)
            r = lax.broadcasted_iota(jnp.int32, (N_EXPERTS, N_EXPERTS), 0)
            c = lax.broadcasted_iota(jnp.int32, (N_EXPERTS, N_EXPERTS), 1)
            as_row = jnp.sum(jnp.where(r == c, padded, 0.0), axis=0, keepdims=True)
            start_sc[...] = jnp.sum(jnp.where(c < r, as_row, 0.0), axis=1, keepdims=True)
            run_sc[...] = jnp.zeros(run_sc.shape, F32)

        mask_b = mask_sc[:, cols]
        mask = mask_b.astype(F32)
        before = jnp.dot(mask_b, tri_ref[...], preferred_element_type=F32) - mask
        slot = start_sc[...] + run_sc[...] + before
        dest_ref[0] = _pick_rows(slot, picks_sc[:, cols]).astype(jnp.int32)
        w_ref[0] = jnp.zeros(w_ref.shape[1:], w_ref.dtype)
        run_sc[...] = run_sc[...] + jnp.sum(mask, axis=1, keepdims=True)


SLOT_BLOCK = 512


def route(logits_t, bias):
    E, N = logits_t.shape
    tn = 1024
    tri = (jnp.arange(tn)[:, None] <= jnp.arange(tn)[None, :]).astype(BF16)
    plane = lambda: pl.BlockSpec((1, TOP_K, tn), lambda p, i: (p, 0, i))
    dest, w, cnt = pl.pallas_call(
        functools.partial(_route_kernel, slot_block=SLOT_BLOCK),
        grid=(2, N // tn),
        in_specs=[
            pl.BlockSpec((E, tn), lambda p, i: (0, i * (1 - p))),
            pl.BlockSpec((E, 1), lambda p, i: (0, 0)),
            pl.BlockSpec((tn, tn), lambda p, i: (0, 0)),
        ],
        out_specs=[plane(), plane(), pl.BlockSpec((E, 128), lambda p, i: (0, 0))],
        out_shape=[
            jax.ShapeDtypeStruct((2, TOP_K, N), jnp.int32),
            jax.ShapeDtypeStruct((2, TOP_K, N), F32),
            jax.ShapeDtypeStruct((E, 128), jnp.int32),
        ],
        scratch_shapes=[pltpu.VMEM((E, 1), F32), pltpu.VMEM((E, 1), F32),
                        pltpu.VMEM((E, N), BF16), pltpu.VMEM((TOP_K, N), jnp.int32)],
        compiler_params=_cp(("arbitrary", "arbitrary")),
        name="route",
    )(logits_t, bias.reshape(E, 1), tri)
    return dest[1], w[0], cnt[:, 0]


def block_tables(counts, n_tokens):
    E = counts.shape[0]
    blk = SLOT_BLOCK
    nblk = (n_tokens * TOP_K + E * blk) // blk
    per_expert = (counts + blk - 1) // blk
    bend = jnp.cumsum(per_expert)
    bstart = bend - per_expert
    b = jnp.arange(nblk, dtype=jnp.int32)[:, None]
    owns = (bstart[None, :] <= b) & (b < bend[None, :])
    blk_e = jnp.minimum(jnp.sum(bend[None, :] <= b, axis=1), E - 1).astype(jnp.int32)
    rows_left = counts[None, :] - (b - bstart[None, :]) * blk
    nvalid = jnp.sum(jnp.where(owns, jnp.clip(rows_left, 0, blk), 0), axis=1)
    first = jnp.concatenate([jnp.ones((1,), jnp.bool_), blk_e[1:] != blk_e[:-1]])
    run_parity = ((jnp.cumsum(first.astype(jnp.int32)) - 1) % 2).astype(jnp.int32)
    later = blk_e[None, :] > blk_e[:, None]
    next_e = jnp.min(jnp.where(later, blk_e[None, :], E), axis=1).astype(jnp.int32)
    return blk_e, nvalid.astype(jnp.int32), run_parity, next_e


def _sc_mesh():
    return plsc.VectorSubcoreMesh(core_axis_name="c", subcore_axis_name="s")


SC_WINDOW = 128


def sc_scatter_rows(x, dest, n_slots):
    N, W = x.shape
    K = dest.shape[0]

    @functools.partial(pl.kernel, out_type=jax.ShapeDtypeStruct((n_slots, W), x.dtype), mesh=_sc_mesh(),
                       scratch_types=[])
    def scatter(x_hbm, i_hbm, o_hbm):
        def body(x_vmem, i_vmem):
            for k in range(K):
                pltpu.sync_copy(x_vmem, o_hbm.at[i_vmem.at[k]])

        pltpu.emit_pipeline(
            body,
            grid=(N // SC_WINDOW,),
            in_specs=[pl.BlockSpec((SC_WINDOW, W), lambda i: (i, 0)),
                      pl.BlockSpec((K, SC_WINDOW), lambda i: (0, i))],
            out_specs=[],
            core_axis_name=("c", "s"),
            dimension_semantics=(pltpu.PARALLEL,),
        )(x_hbm, i_hbm)

    return scatter(x, dest)


SC_LANES = 16
SC_GATHER_TOKENS = 8


def sc_weighted_gather(y, dest, wts):
    W = y.shape[1]
    K, N = dest.shape
    G, L = SC_GATHER_TOKENS, SC_LANES
    batches = SC_WINDOW // G

    @functools.partial(
        pl.kernel, out_type=jax.ShapeDtypeStruct((N, W), y.dtype), mesh=_sc_mesh(),
        scratch_types=[pltpu.VMEM((2, K, G, W), y.dtype), pltpu.SemaphoreType.DMA((2,))],
        compiler_params=pltpu.CompilerParams(needs_layout_passes=False))
    def gather(y_hbm, i_hbm, w_hbm, o_hbm, rows2, sems):
        def body(i_vmem, w_vmem, o_vmem):
            def fetch(batch, slot):
                return [pltpu.make_async_copy(y_hbm.at[i_vmem.at[k, pl.ds(batch * G, G)]], rows2.at[slot, k],
                                              sems.at[slot]) for k in range(K)]

            for c in fetch(0, 0):
                c.start()

            @pl.loop(0, batches)
            def _(batch):
                slot = batch % 2

                @pl.when(batch + 1 < batches)
                def _():
                    for c in fetch(batch + 1, 1 - slot):
                        c.start()

                for c in fetch(batch, slot):
                    c.wait()
                rows = rows2.at[slot]

                @pl.loop(0, G)
                def _(t):
                    tok = jnp.full((L,), batch * G + t, jnp.int32)
                    wk = [plsc.load_gather(w_vmem, [jnp.full((L,), k, jnp.int32), tok]) for k in range(K)]

                    @plsc.parallel_loop(0, W // L, unroll=W // L)
                    def _(j):
                        lo = jnp.zeros((L,), F32)
                        hi = jnp.zeros((L,), F32)
                        for k in range(K):
                            pair = plsc.bitcast(rows[k, t, pl.ds(j * L, L)], BF16)
                            a, b = plsc.unpack(pair, format=plsc.PackFormat.INTERLEAVED)
                            lo = lo + wk[k] * a
                            hi = hi + wk[k] * b
                        o_vmem[batch * G + t, pl.ds(j * L, L)] = plsc.bitcast(
                            plsc.pack(lo, hi, format=plsc.PackFormat.INTERLEAVED), y.dtype)

        pltpu.emit_pipeline(
            body,
            grid=(N // SC_WINDOW,),
            in_specs=[pl.BlockSpec((K, SC_WINDOW), lambda i: (0, i)),
                      pl.BlockSpec((K, SC_WINDOW), lambda i: (0, i))],
            out_specs=[pl.BlockSpec((SC_WINDOW, W), lambda i: (i, 0))],
            core_axis_name=("c", "s"),
            dimension_semantics=(pltpu.PARALLEL,),
        )(i_hbm, w_hbm, o_hbm)

    return gather(y, dest, wts)


EXPERT_INPUT_SLOTS = 3


def _expert_kernel(blk_e_ref, nvalid_ref, parity_ref, next_e_ref, xa_hbm, xb_hbm, w1_hbm, w3_hbm, w2_hbm,
                   ya_hbm, yb_hbm, w1_sc, w3_sc, w2_sc, xa_buf, xb_buf, sems, w1_st, w3_st, w2_st, wsems,
                   ya_st, yb_st, osems, *, layer):
    b = pl.program_id(0)
    nb = pl.num_programs(0)
    nv = nvalid_ref[b]
    prev_e = blk_e_ref[jnp.maximum(b - 1, 0)]
    blk = xa_buf.shape[1]
    ring = EXPERT_INPUT_SLOTS
    n_experts = w1_hbm.shape[1]

    def fetch_weights(e, par):
        return (pltpu.make_async_copy(w1_hbm.at[layer, e], w1_st.at[par], wsems.at[par, 0]),
                pltpu.make_async_copy(w3_hbm.at[layer, e], w3_st.at[par], wsems.at[par, 1]),
                pltpu.make_async_copy(w2_hbm.at[layer, e], w2_st.at[par], wsems.at[par, 2]))

    def fetch(block, slot):
        rows = pl.ds(pl.multiple_of(block * blk, blk), blk)
        return (pltpu.make_async_copy(xa_hbm.at[rows], xa_buf.at[slot], sems.at[slot, 0]),
                pltpu.make_async_copy(xb_hbm.at[rows], xb_buf.at[slot], sems.at[slot, 1]))

    @pl.when(b == 0)
    def _():
        for i in range(ring - 1):
            for c in fetch(i, i):
                c.start()

    ahead = b + (ring - 1)

    @pl.when(ahead < nb)
    def _():
        for c in fetch(ahead, ahead % ring):
            c.start()

    slot = b % ring
    for c in fetch(b, slot):
        c.wait()

    @pl.when(b == 0)
    def _():
        for c in fetch_weights(blk_e_ref[0], 0):
            c.start()

    @pl.when((b == 0) | (blk_e_ref[b] != prev_e))
    def _():
        par = parity_ref[b]
        nxt = next_e_ref[b]

        @pl.when(nxt < n_experts)
        def _():
            for c in fetch_weights(nxt, 1 - par):
                c.start()

        for c in fetch_weights(blk_e_ref[b], par):
            c.wait()
        w1_sc[...] = w1_st[par].astype(BF16)
        w3_sc[...] = w3_st[par].astype(BF16)
        w2_sc[...] = w2_st[par].astype(BF16)

    def put(block, oslot):
        rows = pl.ds(pl.multiple_of(block * blk, blk), blk)
        return (pltpu.make_async_copy(ya_st.at[oslot], ya_hbm.at[rows], osems.at[oslot, 0]),
                pltpu.make_async_copy(yb_st.at[oslot], yb_hbm.at[rows], osems.at[oslot, 1]))

    oslot = b % 2

    @pl.when(b >= 2)
    def _():
        for c in put(b - 2, oslot):
            c.wait()

    @pl.when(nv > 0)
    def _():
        x = _unpack_row_halves(xa_buf[slot], xb_buf[slot])
        rows = lax.broadcasted_iota(jnp.int32, x.shape, 0)
        x = jnp.where(rows < nv, x, 0.0).astype(BF16)
        hid = _silu(jnp.dot(x, w1_sc[...], preferred_element_type=F32)) * jnp.dot(
            x, w3_sc[...], preferred_element_type=F32)
        y = jnp.dot(hid.astype(BF16), w2_sc[...], preferred_element_type=F32)
        ya_st[oslot], yb_st[oslot] = _pack_row_halves(y)

    @pl.when(nv == 0)
    def _():
        ya_st[oslot] = jnp.zeros(ya_st.shape[1:], ya_st.dtype)
        yb_st[oslot] = jnp.zeros(yb_st.shape[1:], yb_st.dtype)

    for c in put(b, oslot):
        c.start()

    @pl.when(b == nb - 1)
    def _():
        @pl.when(nb >= 2)
        def _():
            for c in put(b - 1, 1 - oslot):
                c.wait()

        for c in put(b, oslot):
            c.wait()


def routed_experts(xa, xb, tables, w1, w3, w2, layer):
    P = xa.shape[0]
    blk = SLOT_BLOCK
    _, E, D, FF = w1.shape
    grid_spec = pltpu.PrefetchScalarGridSpec(
        num_scalar_prefetch=len(tables),
        grid=(P // blk,),
        in_specs=[pl.BlockSpec(memory_space=pl.ANY)] * 5,
        out_specs=[pl.BlockSpec(memory_space=pl.ANY)] * 2,
        scratch_shapes=[
            pltpu.VMEM((D, FF), BF16), pltpu.VMEM((D, FF), BF16), pltpu.VMEM((FF, D), BF16),
            pltpu.VMEM((EXPERT_INPUT_SLOTS, blk, PACK_W), jnp.int32),
            pltpu.VMEM((EXPERT_INPUT_SLOTS, blk, PACK_W), jnp.int32),
            pltpu.SemaphoreType.DMA((EXPERT_INPUT_SLOTS, 2)),
            pltpu.VMEM((2, D, FF), F32), pltpu.VMEM((2, D, FF), F32), pltpu.VMEM((2, FF, D), F32),
            pltpu.SemaphoreType.DMA((2, 3)),
            pltpu.VMEM((2, blk, PACK_W), jnp.int32), pltpu.VMEM((2, blk, PACK_W), jnp.int32),
            pltpu.SemaphoreType.DMA((2, 2)),
        ],
    )
    return pl.pallas_call(
        functools.partial(_expert_kernel, layer=layer),
        grid_spec=grid_spec,
        out_shape=[jax.ShapeDtypeStruct((P, PACK_W), jnp.int32)] * 2,
        compiler_params=_cp(("arbitrary",), VMEM_LIMIT),
        name="routed_experts",
    )(*tables, xa, xb, w1, w3, w2)


def _combine_kernel(xmid_ref, ra_ref, rb_ref, mod2_ref, fg_ref, *rest):
    out_ref = rest[-1]
    D = xmid_ref.shape[1]
    x = xmid_ref[...] + mod2_ref[0][:, 2 * D:] * _unpack_row_halves(ra_ref[...], rb_ref[...])
    out_ref[...] = x * lax.rsqrt(jnp.mean(x * x, axis=-1, keepdims=True) + EPS) * fg_ref[...]


def combine(xmid, ra, rb, mod2, final_g, seq, out_rows=None, row0=0, out_buf=None, in_row0=0):
    N, D = ra.shape[0], xmid.shape[1]
    tm = 512
    tpb = seq // tm
    tile0 = row0 // tm
    in_tile0 = in_row0 // tm
    in_specs = [
        pl.BlockSpec((tm, D), lambda i: (i + in_tile0, 0)),
        pl.BlockSpec((tm, PACK_W), lambda i: (i, 0)),
        pl.BlockSpec((tm, PACK_W), lambda i: (i, 0)),
        pl.BlockSpec((1, 1, 3 * D), lambda i: ((i + in_tile0) // tpb, 0, 0)),
        pl.BlockSpec((1, D), lambda i: (0, 0)),
    ]
    args = [xmid, ra, rb, mod2, final_g.reshape(1, D)]
    aliases = {}
    if out_buf is not None:
        in_specs.append(pl.BlockSpec(memory_space=pl.ANY))
        args.append(out_buf)
        aliases = {len(args) - 1: 0}
    return pl.pallas_call(
        _combine_kernel,
        grid=(N // tm,),
        in_specs=in_specs,
        out_specs=pl.BlockSpec((tm, D), lambda i: (i + tile0, 0)),
        out_shape=jax.ShapeDtypeStruct((out_rows or N, D), F32),
        input_output_aliases=aliases,
        compiler_params=_cp(("parallel",), VMEM_LIMIT),
        name="combine",
    )(*args)


TOKEN_STREAMS = 2
LAST_GATHER_SPLITS = 4


def _permute_w_in(w):
    ub = w[:, 3 * DA:3 * DA + DB]
    lat_lo = 3 * DA + DB
    lat_hi = lat_lo + Q_LORA + KV_LORA + QK_ROPE
    lat, gates = w[:, lat_lo:lat_hi], w[:, lat_hi:]
    pad = jnp.zeros((w.shape[0], LAT_W - (lat_hi - lat_lo)), w.dtype)
    parts = [gates, ub, lat, pad]
    for g in range(len(DIL_GROUPS)):
        sl = slice(g * GROUP_W, (g + 1) * GROUP_W)
        parts += [w[:, :DA][:, sl] * (HEAD_DIM_A ** -0.5), w[:, DA:2 * DA][:, sl], w[:, 2 * DA:3 * DA][:, sl]]
    return jnp.concatenate(parts, axis=1).astype(BF16)


def kernel(x, c, positions, ada_mix_w, ada_mix_b, norm_mix_g, w_in, pool_w, pool_scale, cq_norm_g, ckv_norm_g, w_uq, w_ukv, w_oa, w_ob, w_oc, w_out, ada_ffn_w, ada_ffn_b, norm_ffn_g, router_w, router_bias, exp_w1, exp_w3, exp_w2, sh_w1, sh_w3, sh_w2, final_g):
    B, S, D = x.shape
    depth = w_in.shape[0]
    mod_mix = adaln_rows(c, ada_mix_w, ada_mix_b)
    mod_ffn = adaln_rows(c, ada_ffn_w, ada_ffn_b)
    streams = TOKEN_STREAMS if B % TOKEN_STREAMS == 0 else 1
    Bs = B // streams
    Ns = Bs * S
    x_all = x.reshape(B * S, D)
    xs = [None] * streams
    out_all = None
    pos_s = [positions[s * Bs:(s + 1) * Bs] for s in range(streams)]
    for l in range(depth):
        last = l == depth - 1
        w_in_l = _permute_w_in(w_in[l])
        mla_w = _mla_weights(cq_norm_g[l], ckv_norm_g[l], w_uq[l], w_ukv[l])
        mix_w = (norm_ffn_g[l], pool_w[l].astype(BF16), pool_scale[l],
                 w_oa[l].astype(BF16), w_ob[l].astype(BF16), w_oc[l].astype(BF16), w_out[l].astype(BF16),
                 router_w[l].T.astype(BF16), sh_w1[l].astype(BF16), sh_w3[l].astype(BF16), sh_w2[l].astype(BF16))
        for s in range(streams):
            mod1 = mod_mix[l, s * Bs:(s + 1) * Bs].reshape(Bs, 1, 3 * D)
            mod2 = mod_ffn[l, s * Bs:(s + 1) * Bs].reshape(Bs, 1, 3 * D)
            if l == 0:
                x2, row0 = x_all, s * Ns
                gu, lat, *qkv = in_projection(x2, norm_mix_g[l], mod1, w_in_l, S, row0)
            else:
                row0 = 0
                x2, gu, lat, *qkv = in_projection(xs[s][0], norm_mix_g[l], mod1, w_in_l, S, 0, xs[s][1:])
            dil = [dilated_attention(qkv[2 * g], qkv[2 * g + 1]) for g in range(len(DIL_GROUPS))]
            q_all, k_all, vt_all = mla_prep(lat, pos_s[s], *mla_w, Bs, S)
            yc = mla_attention(q_all, k_all, vt_all, Bs, S)
            xmid, h2a, h2b, logits_t = mix_out(x2, gu, dil, yc, mod1, mod2, *mix_w, S, row0)
            dest, w_k, counts = route(logits_t, router_bias[l])
            tables = block_tables(counts, Ns)
            n_slots = tables[0].shape[0] * SLOT_BLOCK
            xa = sc_scatter_rows(h2a, dest, n_slots)
            xb = sc_scatter_rows(h2b, dest, n_slots)
            ya, yb = routed_experts(xa, xb, tables, exp_w1, exp_w3, exp_w2, l)
            if last:
                for t0 in range(0, Ns, Ns // LAST_GATHER_SPLITS):
                    cols = slice(t0, t0 + Ns // LAST_GATHER_SPLITS)
                    ra = sc_weighted_gather(ya, dest[:, cols], w_k[:, cols])
                    rb = sc_weighted_gather(yb, dest[:, cols], w_k[:, cols])
                    out_all = combine(xmid, ra, rb, mod2, final_g, S, B * S, s * Ns + t0, out_all, t0)
            else:
                xs[s] = (xmid, sc_weighted_gather(ya, dest, w_k), sc_weighted_gather(yb, dest, w_k), mod2)
    return out_all.reshape(B, S, D)
```

```python
import functools
import math

import jax
import jax.numpy as jnp
from jax import lax
from jax.experimental import pallas as pl
from jax.experimental.pallas import tpu as pltpu
from jax.experimental.pallas import tpu_sc as plsc

F32 = jnp.float32
BF16 = jnp.bfloat16
HIGHEST = lax.Precision.HIGHEST

D_MODEL = 1024
HEAD_DIM_A = 64
HEADS_PER_GROUP_A = 4
DIL_GROUPS = ((128, 1), (512, 4), (2048, 16))
GROUP_W = HEADS_PER_GROUP_A * HEAD_DIM_A
DA = GROUP_W * len(DIL_GROUPS)
POOL_WINDOWS = (2, 4, 8, 16)
POOL_GROUP_DIM = 128
DB = POOL_GROUP_DIM * len(POOL_WINDOWS)
POOL_HALO = 16
N_HEADS_C = 8
QK_NOPE = 64
QK_ROPE = 32
V_DIM = 64
Q_LORA = 384
KV_LORA = 256
DC = N_HEADS_C * V_DIM
HEAD_PAD_C = 128
ROPE_THETA = 10000.0
N_EXPERTS = 64
TOP_K = 8
N_GROUPS = 8
TOPK_GROUPS = 4
GROUP_SIZE = N_EXPERTS // N_GROUPS
ROUTED_SCALE = 2.5
EPS = 1e-6
NEG = -1e30
Q_BLOCK = 128

LAT_W = 768
GU_W = 3 * D_MODEL + DB
IN_OUT_WIDTHS = (GU_W, LAT_W) + (2 * GROUP_W, GROUP_W) * len(DIL_GROUPS)

VMEM_LIMIT = 56 * 1024 * 1024


def _cp(sem, vmem=None):
    return pltpu.CompilerParams(dimension_semantics=sem, vmem_limit_bytes=vmem)


def _silu(v):
    return v * jax.nn.sigmoid(v)


def _nt_dot(a, b):
    return lax.dot_general(a, b, (((1,), (1,)), ((), ())), preferred_element_type=F32)


PACK_W = D_MODEL // 4
_HI_MASK = -65536


def _bf16_bits(v):
    return lax.bitcast_convert_type(v.astype(BF16).astype(F32), jnp.int32)


def _pack_row_halves(v):
    halves = []
    for h in range(2):
        lo = _bf16_bits(v[:, (2 * h) * PACK_W:(2 * h + 1) * PACK_W])
        hi = _bf16_bits(v[:, (2 * h + 1) * PACK_W:(2 * h + 2) * PACK_W])
        halves.append(lax.shift_right_logical(lo, 16) | (hi & _HI_MASK))
    return halves


def _unpack_row_halves(wa, wb):
    parts = []
    for w in (wa, wb):
        parts.append(lax.bitcast_convert_type(lax.shift_left(w, 16), F32))
        parts.append(lax.bitcast_convert_type(w & _HI_MASK, F32))
    return jnp.concatenate(parts, axis=1)


def _adaln_kernel(c_ref, w_ref, b_ref, o_ref):
    s = _silu(c_ref[...])
    o_ref[0] = jnp.dot(s, w_ref[0], preferred_element_type=F32, precision=HIGHEST) + b_ref[0]


def adaln_rows(c, w, b):
    L, D, D3 = w.shape
    B = c.shape[0]
    tn = 1024
    return pl.pallas_call(
        _adaln_kernel,
        grid=(L, D3 // tn),
        in_specs=[
            pl.BlockSpec((B, D), lambda l, j: (0, 0)),
            pl.BlockSpec((1, D, tn), lambda l, j: (l, 0, j)),
            pl.BlockSpec((1, 1, tn), lambda l, j: (l, 0, j)),
        ],
        out_specs=pl.BlockSpec((1, B, tn), lambda l, j: (l, 0, j)),
        out_shape=jax.ShapeDtypeStruct((L, B, D3), F32),
        compiler_params=_cp(("parallel", "parallel")),
        name="adaln_rows",
    )(c, w, b.reshape(L, 1, D3))


LANES = 128


def _inproj_kernel(x_ref, g_ref, mod_ref, w_ref, *refs, chunk, pending):
    o_refs, scr = refs[:-1], refs[-1]
    D = x_ref.shape[1]
    x = x_ref[...]
    if pending:
        ra_ref, rb_ref, gate_ref, x_out_ref, *o_refs = o_refs
        x = x + gate_ref[0][:, 2 * D:] * _unpack_row_halves(ra_ref[...], rb_ref[...])
        x_out_ref[...] = x
    y = x * lax.rsqrt(jnp.mean(x * x, axis=-1, keepdims=True) + EPS) * g_ref[...]
    mod = mod_ref[0]
    h = (y * (1.0 + mod[:, D:2 * D]) + mod[:, :D]).astype(BF16)
    col = 0
    for o_ref in o_refs:
        width = o_ref.shape[-1]
        if o_ref.ndim == 2:
            for c0 in range(0, width, chunk):
                cw = min(chunk, width - c0)
                o_ref[:, c0:c0 + cw] = jnp.dot(
                    h, w_ref[:, col + c0:col + c0 + cw], preferred_element_type=F32).astype(o_ref.dtype)
        else:
            dil, rows = o_ref.shape[1], o_ref.shape[2]
            z = jnp.dot(h, w_ref[:, col:col + width], preferred_element_type=F32)
            if dil == 1:
                o_ref[0, 0] = z.astype(o_ref.dtype)
            else:
                for c in range(width // LANES):
                    scr[c] = z[:, c * LANES:(c + 1) * LANES]
                for r in range(dil):
                    o_ref[0, r] = jnp.concatenate(
                        [scr[c, pl.ds(r, rows, stride=dil), :] for c in range(width // LANES)],
                        axis=1).astype(o_ref.dtype)
        col += width


def in_projection(x2, g, mod, w, seq, row0=0, pending=None):
    D = x2.shape[1]
    B = mod.shape[0]
    N = B * seq
    tm = 512
    tpb = seq // tm
    tile0 = row0 // tm
    out_specs = [pl.BlockSpec((tm, wd), lambda i: (i, 0)) for wd in IN_OUT_WIDTHS[:2]]
    out_shape = [jax.ShapeDtypeStruct((N, wd), BF16) for wd in IN_OUT_WIDTHS[:2]]
    for grp, (_, dil) in enumerate(DIL_GROUPS):
        for wd in IN_OUT_WIDTHS[2 + 2 * grp:4 + 2 * grp]:
            out_specs.append(pl.BlockSpec((1, dil, tm // dil, wd), lambda i: (i // tpb, 0, i % tpb, 0)))
            out_shape.append(jax.ShapeDtypeStruct((B, dil, seq // dil, wd), BF16))
    in_specs = [
        pl.BlockSpec((tm, D), lambda i: (i + tile0, 0)),
        pl.BlockSpec((1, D), lambda i: (0, 0)),
        pl.BlockSpec((1, 1, 3 * D), lambda i: (i // tpb, 0, 0)),
        pl.BlockSpec(w.shape, lambda i: (0, 0), pipeline_mode=pl.Buffered(1)),
    ]
    args = [x2, g.reshape(1, D), mod, w]
    if pending is not None:
        in_specs += [pl.BlockSpec((tm, PACK_W), lambda i: (i, 0)), pl.BlockSpec((tm, PACK_W), lambda i: (i, 0)),
                     pl.BlockSpec((1, 1, 3 * D), lambda i: (i // tpb, 0, 0))]
        args += list(pending)
        out_specs.insert(0, pl.BlockSpec((tm, D), lambda i: (i, 0)))
        out_shape.insert(0, jax.ShapeDtypeStruct((N, D), F32))
    return pl.pallas_call(
        functools.partial(_inproj_kernel, chunk=512, pending=pending is not None),
        grid=(N // tm,),
        in_specs=in_specs,
        out_specs=out_specs,
        out_shape=out_shape,
        scratch_shapes=[pltpu.VMEM((max(IN_OUT_WIDTHS[2:]) // LANES, tm, LANES), F32)],
        compiler_params=_cp(("parallel",), VMEM_LIMIT),
        name="in_projection",
    )(*args)


def _dilated_kernel(q_ref, kc_ref, kp_ref, vc_ref, vp_ref, o_ref, lse_ref):
    i = pl.program_id(1)
    T = Q_BLOCK
    key = lax.broadcasted_iota(jnp.int32, (T, T), 0)
    qry = lax.broadcasted_iota(jnp.int32, (T, T), 1)
    valid_c = key <= qry
    near = key >= qry
    seqs, run = q_ref.shape[0], q_ref.shape[1] // T
    heads = [slice(h * HEAD_DIM_A, (h + 1) * HEAD_DIM_A) for h in range(HEADS_PER_GROUP_A)]

    def transposed(v):
        return v.astype(F32).T.astype(BF16)

    vts = {(s, j): transposed(vc_ref[s, j * T:(j + 1) * T, :]) for s in range(seqs) for j in range(run)}
    vt_before = [transposed(vp_ref[s]) for s in range(seqs)]

    def blocks(s, j):
        rows = slice(j * T, (j + 1) * T)
        if j == 0:
            return rows, kc_ref[s, rows, :], vts[s, 0], kp_ref[s], vt_before[s], near & (i > 0)
        before = slice((j - 1) * T, j * T)
        return rows, kc_ref[s, rows, :], vts[s, j], kc_ref[s, before, :], vts[s, j - 1], near

    scores, probs = {}, {}
    for s in range(seqs):
        for j in range(run):
            rows, kc, _, kp, _, valid_p = blocks(s, j)
            q = q_ref[s, rows, :]
            for h, sl in enumerate(heads):
                qh = q[:, sl]
                scores[s, j, h] = (jnp.where(valid_c, _nt_dot(kc[:, sl], qh), NEG),
                                   jnp.where(valid_p, _nt_dot(kp[:, sl], qh), NEG))
    for chain, (sc, sp) in scores.items():
        m = jnp.maximum(jnp.max(sc, axis=0, keepdims=True), jnp.max(sp, axis=0, keepdims=True))
        pc = jnp.exp(sc - m)
        pp = jnp.exp(sp - m)
        den = jnp.sum(pc, axis=0, keepdims=True) + jnp.sum(pp, axis=0, keepdims=True)
        probs[chain] = (pc.astype(BF16), pp.astype(BF16), den, m + jnp.log(den))
    spread = LSE_LANES // len(heads)
    for s in range(seqs):
        for j in range(run):
            rows, _, vtc, _, vtp, _ = blocks(s, j)
            outs = []
            for h, sl in enumerate(heads):
                pc, pp, den, _ = probs[s, j, h]
                o = (jnp.dot(vtc[sl, :], pc, preferred_element_type=F32)
                     + jnp.dot(vtp[sl, :], pp, preferred_element_type=F32))
                outs.append(o / den)
            o_ref[s, rows, :] = jnp.concatenate(outs, axis=0).T.astype(o_ref.dtype)
            lse_t = jnp.concatenate(
                [jnp.broadcast_to(probs[s, j, h][3], (spread, T)) for h in range(len(heads))], axis=0)
            lse_ref[s, rows, :] = lse_t.T


DILATED_RUN = 16


LSE_LANES = 128


def dilated_attention(qk, v):
    batch, dilation, L, _ = qk.shape
    nb = L // Q_BLOCK
    run = min(DILATED_RUN, nb)
    seqs = DILATED_RUN // run
    qk_r = qk.reshape(batch * dilation, L, 2 * GROUP_W)
    v_r = v.reshape(batch * dilation, L, GROUP_W)
    before = lambda i: jnp.maximum(i * run - 1, 0)
    o, lse = pl.pallas_call(
        _dilated_kernel,
        grid=(batch * dilation // seqs, nb // run),
        in_specs=[
            pl.BlockSpec((seqs, run * Q_BLOCK, GROUP_W), lambda s, i: (s, i, 0)),
            pl.BlockSpec((seqs, run * Q_BLOCK, GROUP_W), lambda s, i: (s, i, 1)),
            pl.BlockSpec((seqs, Q_BLOCK, GROUP_W), lambda s, i: (s, before(i), 1)),
            pl.BlockSpec((seqs, run * Q_BLOCK, GROUP_W), lambda s, i: (s, i, 0)),
            pl.BlockSpec((seqs, Q_BLOCK, GROUP_W), lambda s, i: (s, before(i), 0)),
        ],
        out_specs=[
            pl.BlockSpec((seqs, run * Q_BLOCK, GROUP_W), lambda s, i: (s, i, 0)),
            pl.BlockSpec((seqs, run * Q_BLOCK, LSE_LANES), lambda s, i: (s, i, 0)),
        ],
        out_shape=[
            jax.ShapeDtypeStruct((batch * dilation, L, GROUP_W), BF16),
            jax.ShapeDtypeStruct((batch * dilation, L, LSE_LANES), F32),
        ],
        compiler_params=_cp(("parallel", "parallel")),
        name=f"dilated_attention_d{dilation}",
    )(qk_r, qk_r, qk_r, v_r, v_r)
    return o.reshape(batch, dilation, L, GROUP_W), lse.reshape(batch, dilation, L, LSE_LANES)


def _mla_prep_kernel(lat_ref, pos_ref, gq_ref, gkv_ref, wq_ref, wk_ref, wvt_ref, freq_ref, spread_ref, one_ref,
                     q_ref, k_ref, vt_ref):
    HP = N_HEADS_C * HEAD_PAD_C
    lat = lat_ref[...].astype(F32)
    cq = lat[:, :Q_LORA]
    ckr = lat[:, Q_LORA:]
    zq = (cq * lax.rsqrt(jnp.mean(cq * cq, axis=-1, keepdims=True) + EPS) * gq_ref[...]).astype(BF16)
    lane = lax.broadcasted_iota(jnp.int32, ckr.shape, 1)
    is_kv = lane < KV_LORA
    ms = jnp.sum(jnp.where(is_kv, ckr * ckr, 0.0), axis=-1, keepdims=True) * (1.0 / KV_LORA)
    zkv = (ckr * jnp.where(is_kv, lax.rsqrt(ms + EPS) * gkv_ref[...], 1.0)).astype(BF16)
    qq = jnp.dot(zq, wq_ref[...], preferred_element_type=F32)
    kk = jnp.dot(zkv, wk_ref[:, :HP], preferred_element_type=F32)
    kk_sw = jnp.dot(zkv[:, KV_LORA:], wk_ref[KV_LORA:, HP:], preferred_element_type=F32)
    ang_t = freq_ref[...] * pos_ref[0].astype(F32)

    def to_lanes(t):
        hi = t.astype(BF16)
        lo = (t - hi.astype(F32)).astype(BF16)
        tn_dot = lambda a: lax.dot_general(a, spread_ref[...], (((0,), (0,)), ((), ())), preferred_element_type=F32)
        return tn_dot(hi) + tn_dot(lo)

    cos = to_lanes(jnp.cos(ang_t)) + one_ref[...]
    sin = to_lanes(jnp.sin(ang_t))
    slot_lane = lax.broadcasted_iota(jnp.int32, (1, HEAD_PAD_C), 1)
    half = QK_ROPE // 2
    sin_x1 = jnp.where((slot_lane >= QK_NOPE) & (slot_lane < QK_NOPE + half), -sin, 0.0)
    sin_x2 = jnp.where((slot_lane >= QK_NOPE + half) & (slot_lane < QK_NOPE + QK_ROPE), sin, 0.0)
    for h in range(N_HEADS_C):
        lo, hi = h * HEAD_PAD_C, (h + 1) * HEAD_PAD_C
        qh = qq[:, lo:hi]
        q_ref[:, lo:hi] = (qh * cos + pltpu.roll(qh, HEAD_PAD_C - half, axis=1) * sin_x1
                           + pltpu.roll(qh, half, axis=1) * sin_x2).astype(q_ref.dtype)
        k_ref[:, lo:hi] = (kk[:, lo:hi] * cos + kk_sw[:, lo:hi] * sin).astype(k_ref.dtype)
    vt_ref[0] = _nt_dot(wvt_ref[...], zkv).astype(vt_ref.dtype)


def _mla_weights(cq_g, ckv_g, w_uq, w_ukv):
    H, HPAD, half = N_HEADS_C, HEAD_PAD_C, QK_ROPE // 2
    scale = (QK_NOPE + QK_ROPE) ** -0.5 * math.log2(math.e)
    wq = w_uq.reshape(Q_LORA, H, QK_NOPE + QK_ROPE) * scale
    wq_big = jnp.pad(wq, ((0, 0), (0, 0), (0, HPAD - QK_NOPE - QK_ROPE))).reshape(Q_LORA, H * HPAD)

    rows = LAT_W - Q_LORA
    wkv = w_ukv.reshape(KV_LORA, H, QK_NOPE + V_DIM)
    eye = jnp.eye(QK_ROPE, dtype=F32)
    k_lin = jnp.zeros((rows, H, HPAD), F32)
    k_lin = k_lin.at[:KV_LORA, :, :QK_NOPE].set(wkv[..., :QK_NOPE])
    k_lin = k_lin.at[KV_LORA:KV_LORA + QK_ROPE, :, QK_NOPE:QK_NOPE + QK_ROPE].set(
        jnp.broadcast_to(eye[:, None, :], (QK_ROPE, H, QK_ROPE)))
    swap = jnp.zeros((QK_ROPE, QK_ROPE), F32).at[half:, :half].set(-jnp.eye(half)).at[:half, half:].set(jnp.eye(half))
    k_sw = jnp.zeros((rows, H, HPAD), F32)
    k_sw = k_sw.at[KV_LORA:KV_LORA + QK_ROPE, :, QK_NOPE:QK_NOPE + QK_ROPE].set(
        jnp.broadcast_to(swap[:, None, :], (QK_ROPE, H, QK_ROPE)))
    v_w = jnp.zeros((rows, H, V_DIM), F32).at[:KV_LORA].set(wkv[..., QK_NOPE:])
    wk_big = jnp.concatenate([k_lin.reshape(rows, H * HPAD), k_sw.reshape(rows, H * HPAD)], axis=1)
    wv_t = v_w.reshape(rows, H * V_DIM).T

    gkv = jnp.concatenate([ckv_g, jnp.ones((rows - KV_LORA,), F32)]).reshape(1, rows)
    return cq_g.reshape(1, Q_LORA), gkv, wq_big.astype(BF16), wk_big.astype(BF16), wv_t.astype(BF16)


def _rope_tables():
    half = QK_ROPE // 2
    freqs = (ROPE_THETA ** (-jnp.arange(0, QK_ROPE, 2, dtype=F32) / QK_ROPE)).reshape(half, 1)
    lane = jnp.arange(HEAD_PAD_C)[None, :]
    j = jnp.arange(half)[:, None]
    spread = (lane == QK_NOPE + j) | (lane == QK_NOPE + half + j)
    off_rope = ~jnp.any(spread, axis=0, keepdims=True)
    return freqs, spread.astype(BF16), off_rope.astype(F32)


def mla_prep(lat, positions, gq, gkv, wq_big, wk_big, wv_t, batch, seq):
    N = lat.shape[0]
    HP = N_HEADS_C * HEAD_PAD_C
    tm = 512
    tpb = seq // tm
    freqs, spread, off_rope = _rope_tables()
    pos_rows = positions.reshape(N // tm, 1, tm)
    const = lambda shape: pl.BlockSpec(shape, lambda i: (0, 0))
    return pl.pallas_call(
        _mla_prep_kernel,
        grid=(N // tm,),
        in_specs=[
            pl.BlockSpec((tm, LAT_W), lambda i: (i, 0)),
            pl.BlockSpec((1, 1, tm), lambda i: (i, 0, 0)),
            const(gq.shape), const(gkv.shape), const(wq_big.shape), const(wk_big.shape), const(wv_t.shape),
            const(freqs.shape), const(spread.shape), const(off_rope.shape),
        ],
        out_specs=[
            pl.BlockSpec((tm, HP), lambda i: (i, 0)),
            pl.BlockSpec((tm, HP), lambda i: (i, 0)),
            pl.BlockSpec((1, DC, tm), lambda i: (i // tpb, 0, i % tpb)),
        ],
        out_shape=[
            jax.ShapeDtypeStruct((N, HP), BF16),
            jax.ShapeDtypeStruct((N, HP), BF16),
            jax.ShapeDtypeStruct((batch, DC, seq), BF16),
        ],
        compiler_params=_cp(("parallel",), VMEM_LIMIT),
        name="mla_prep",
    )(lat, pos_rows, gq, gkv, wq_big, wk_big, wv_t, freqs, spread, off_rope)


HEADS_PER_STEP_C = 8
FLASH_Q_CHUNK = 256


def _mla_flash_kernel(qi_ref, ki_ref, q_ref, k_ref, vt_ref, o_ref, m_sc, l_sc, acc_sc):
    t = pl.program_id(2)
    qi, ki = qi_ref[t], ki_ref[t]

    @pl.when(ki == 0)
    def _():
        m_sc[...] = jnp.full(m_sc.shape, NEG, F32)
        l_sc[...] = jnp.zeros(l_sc.shape, F32)
        acc_sc[...] = jnp.zeros(acc_sc.shape, F32)

    def step(masked):
        T = q_ref.shape[1]
        if masked:
            key = lax.broadcasted_iota(jnp.int32, (T, T), 0)
            qry = lax.broadcasted_iota(jnp.int32, (T, T), 1)
            keep = key <= qry
        chains = [(h, c) for h in range(HEADS_PER_STEP_C) for c in range(T // FLASH_Q_CHUNK)]
        scores, probs, alphas = {}, {}, {}

        def keys_for(c):
            return (c + 1) * FLASH_Q_CHUNK if masked else T

        def qk(h, c):
            qs = slice(c * FLASH_Q_CHUNK, (c + 1) * FLASH_Q_CHUNK)
            q = q_ref[0, qs, h * HEAD_PAD_C:(h + 1) * HEAD_PAD_C]
            k = k_ref[0, :keys_for(c), h * HEAD_PAD_C:(h + 1) * HEAD_PAD_C]
            st = _nt_dot(k, q)
            scores[h, c] = jnp.where(keep[:keys_for(c), qs], st, NEG) if masked else st

        def softmax(h, c):
            qs = slice(c * FLASH_Q_CHUNK, (c + 1) * FLASH_Q_CHUNK)
            st = scores.pop((h, c))
            m_prev = m_sc[h, :, qs]
            m_new = jnp.maximum(m_prev, jnp.max(st, axis=0, keepdims=True))
            alpha = jnp.exp2(m_prev - m_new)
            p = jnp.exp2(st - m_new)
            l_sc[h, :, qs] = alpha * l_sc[h, :, qs] + jnp.sum(p, axis=0, keepdims=True)
            m_sc[h, :, qs] = m_new
            probs[h, c], alphas[h, c] = p.astype(BF16), alpha

        def pv(h, c):
            qs = slice(c * FLASH_Q_CHUNK, (c + 1) * FLASH_Q_CHUNK)
            vt = vt_ref[0, h * V_DIM:(h + 1) * V_DIM, :keys_for(c)]
            acc_sc[h, :, qs] = alphas.pop((h, c)) * acc_sc[h, :, qs] + jnp.dot(
                vt, probs.pop((h, c)), preferred_element_type=F32)

        for phase in (qk, softmax, pv):
            for ch in chains:
                phase(*ch)

    @pl.when(ki < qi)
    def _():
        step(False)

    @pl.when(ki == qi)
    def _():
        step(True)
        ot = jnp.concatenate([acc_sc[h] / l_sc[h] for h in range(HEADS_PER_STEP_C)], axis=0)
        o_ref[0] = ot.T.astype(o_ref.dtype)


def mla_attention(q_all, k_all, vt_all, batch, seq):
    T = 512
    nq = seq // T
    pairs = [(a, b) for a in range(nq) for b in range(a + 1)]
    qi_tab = jnp.asarray([p[0] for p in pairs], jnp.int32)
    ki_tab = jnp.asarray([p[1] for p in pairs], jnp.int32)
    hp = N_HEADS_C // HEADS_PER_STEP_C
    qw = HEADS_PER_STEP_C * HEAD_PAD_C
    vw = HEADS_PER_STEP_C * V_DIM
    q3 = q_all.reshape(batch, seq, -1)
    k3 = k_all.reshape(batch, seq, -1)
    grid_spec = pltpu.PrefetchScalarGridSpec(
        num_scalar_prefetch=2,
        grid=(batch, hp, len(pairs)),
        in_specs=[
            pl.BlockSpec((1, T, qw), lambda b, h, t, qi, ki: (b, qi[t], h)),
            pl.BlockSpec((1, T, qw), lambda b, h, t, qi, ki: (b, ki[t], h)),
            pl.BlockSpec((1, vw, T), lambda b, h, t, qi, ki: (b, h, ki[t])),
        ],
        out_specs=pl.BlockSpec((1, T, vw), lambda b, h, t, qi, ki: (b, qi[t], h)),
        scratch_shapes=[
            pltpu.VMEM((HEADS_PER_STEP_C, 1, T), F32),
            pltpu.VMEM((HEADS_PER_STEP_C, 1, T), F32),
            pltpu.VMEM((HEADS_PER_STEP_C, V_DIM, T), F32),
        ],
    )
    o = pl.pallas_call(
        _mla_flash_kernel,
        grid_spec=grid_spec,
        out_shape=jax.ShapeDtypeStruct((batch, seq, DC), BF16),
        compiler_params=_cp(("parallel", "parallel", "arbitrary")),
        name="mla_attention",
    )(qi_tab, ki_tab, q3, k3, vt_all)
    return o.reshape(batch * seq, DC)


MIX_CHUNK = 256


def _mixout_kernel(x_ref, gates_ref, ub_ref, ubh_ref, o1_ref, o2_ref, o3_ref, l1_ref, l2_ref, l3_ref, yc_ref,
                   mod1_ref, mod2_ref, g2_ref, poolw_ref, pscale_ref, woa_ref, wob_ref, woc_ref, wout_ref,
                   rwt_ref, sw1_ref, sw3_ref, sw2_ref, spread_ref,
                   xmid_ref, h2a_ref, h2b_ref, logit_ref, *scratch, tiles_per_batch):
    D = x_ref.shape[1]
    tm = x_ref.shape[0]
    tile = pl.program_id(0) % tiles_per_batch
    o_scrs, l_scrs = scratch[:3], scratch[3:]

    def token_order(ref, scr):
        dil, rows, width = ref.shape[1:]
        if dil == 1:
            return ref[0, 0].astype(F32)
        for r in range(dil):
            v = ref[0, r].astype(F32)
            for c in range(width // LANES):
                scr[c, pl.ds(r, rows, stride=dil), :] = v[:, c * LANES:(c + 1) * LANES]
        return jnp.concatenate([scr[c] for c in range(width // LANES)], axis=1)

    outs = [token_order(r, s) for r, s in zip((o1_ref, o2_ref, o3_ref), o_scrs)]
    l1, l2, l3 = [token_order(r, s) for r, s in zip((l1_ref, l2_ref, l3_ref), l_scrs)]
    mx = jnp.maximum(jnp.maximum(l1, l2), l3)
    es = [jnp.exp(l1 - mx), jnp.exp(l2 - mx), jnp.exp(l3 - mx)]
    inv = 1.0 / (es[0] + es[1] + es[2])
    ya = jnp.zeros((tm, GROUP_W), F32)
    for e, o in zip(es, outs):
        w = e * inv
        w_hi = w.astype(BF16)
        w_lo = (w - w_hi.astype(F32)).astype(BF16)
        w_wide = (jnp.dot(w_hi, spread_ref[...], preferred_element_type=F32)
                  + jnp.dot(w_lo, spread_ref[...], preferred_element_type=F32))
        ya = ya + w_wide * o
    ya_b = ya.astype(BF16)

    u = ub_ref[...].astype(F32)
    halo = jnp.where(tile > 0, ubh_ref[...].astype(F32), 0.0)
    ext = jnp.concatenate([halo, u], axis=0)
    t_seq = tile * tm + lax.broadcasted_iota(jnp.int32, (tm, 1), 0)
    pooled = []
    for gi, w in enumerate(POOL_WINDOWS):
        sl = slice(gi * POOL_GROUP_DIM, (gi + 1) * POOL_GROUP_DIM)
        acc = ext[:, sl]
        k = 1
        while k < w:
            acc = acc + pltpu.roll(acc, k, axis=0)
            k *= 2
        cnt = jnp.minimum(t_seq + 1, w).astype(F32)
        pg = acc[POOL_HALO:] / cnt - u[:, sl]
        pooled.append(jnp.dot(pg.astype(BF16), poolw_ref[gi], preferred_element_type=F32))
    yb = jnp.concatenate(pooled, axis=1) * pscale_ref[...]
    yb_b = yb.astype(BF16)
    yc_b = yc_ref[...]

    mix_chunks = []
    for c0 in range(0, D, MIX_CHUNK):
        cs = slice(c0, c0 + MIX_CHUNK)
        gate = lambda j: jax.nn.sigmoid(gates_ref[:, j * D + c0:j * D + c0 + MIX_CHUNK].astype(F32))
        mix_c = (gate(0) * jnp.dot(ya_b, woa_ref[:, cs], preferred_element_type=F32)
                 + gate(1) * jnp.dot(yb_b, wob_ref[:, cs], preferred_element_type=F32)
                 + gate(2) * jnp.dot(yc_b, woc_ref[:, cs], preferred_element_type=F32))
        mix_chunks.append(mix_c.astype(BF16))
    tok = jnp.dot(jnp.concatenate(mix_chunks, axis=1), wout_ref[...], preferred_element_type=F32)
    xn = x_ref[...] + mod1_ref[0][:, 2 * D:] * tok

    mod2 = mod2_ref[0]
    y = xn * lax.rsqrt(jnp.mean(xn * xn, axis=-1, keepdims=True) + EPS) * g2_ref[...]
    h2 = y * (1.0 + mod2[:, D:2 * D]) + mod2[:, :D]
    h2b = h2.astype(BF16)
    h2a_ref[...], h2b_ref[...] = _pack_row_halves(h2b)
    logit_ref[...] = _nt_dot(rwt_ref[...], h2b)
    hid = _silu(jnp.dot(h2b, sw1_ref[...], preferred_element_type=F32)) * jnp.dot(
        h2b, sw3_ref[...], preferred_element_type=F32)
    shared = jnp.dot(hid.astype(BF16), sw2_ref[...], preferred_element_type=F32)
    xmid_ref[...] = xn + mod2[:, 2 * D:] * shared


def mix_out(x2, gu, dil, yc, mod1, mod2, g2, pool_w, pool_scale, w_oa, w_ob, w_oc, w_out, rwt, sw1, sw3, sw2, seq,
            row0=0):
    D = x2.shape[1]
    N = gu.shape[0]
    tm = 512
    tpb = seq // tm
    tile0 = row0 // tm
    (o1, l1), (o2, l2), (o3, l3) = dil
    row = lambda w, c=0: pl.BlockSpec((tm, w), lambda i: (i, c))
    by_residue = lambda a: pl.BlockSpec(
        (1, a.shape[1], tm // a.shape[1], a.shape[3]), lambda i: (i // tpb, 0, i % tpb, 0))
    heads = HEADS_PER_GROUP_A
    spread = (jnp.arange(LSE_LANES)[:, None] == (jnp.arange(GROUP_W)[None, :] // HEAD_DIM_A) * (LSE_LANES // heads)
              ).astype(BF16)
    const2 = lambda a: pl.BlockSpec(a.shape, lambda i: (0,) * a.ndim, pipeline_mode=pl.Buffered(1))
    modspec = pl.BlockSpec((1, 1, 3 * D), lambda i: (i // tpb, 0, 0))
    ub_col = 3 * D // DB
    halo_spec = pl.BlockSpec(
        (POOL_HALO, DB), lambda i: (jnp.maximum(i * (tm // POOL_HALO) - 1, 0), ub_col))
    weights = [g2.reshape(1, D), pool_w, pool_scale.reshape(1, DB), w_oa, w_ob, w_oc, w_out, rwt, sw1, sw3, sw2,
               spread]
    return pl.pallas_call(
        functools.partial(_mixout_kernel, tiles_per_batch=tpb),
        grid=(N // tm,),
        in_specs=[
            pl.BlockSpec((tm, D), lambda i: (i + tile0, 0)), row(3 * D), row(DB, ub_col), halo_spec,
            by_residue(o1), by_residue(o2), by_residue(o3), by_residue(l1), by_residue(l2), by_residue(l3), row(DC),
            modspec, modspec,
        ] + [const2(a) for a in weights],
        scratch_shapes=[pltpu.VMEM((GROUP_W // LANES, tm, LANES), F32)] * 3
        + [pltpu.VMEM((LSE_LANES // LANES, tm, LANES), F32)] * 3,
        out_specs=[row(D), row(PACK_W), row(PACK_W), pl.BlockSpec((N_EXPERTS, tm), lambda i: (0, i))],
        out_shape=[
            jax.ShapeDtypeStruct((N, D), F32),
            jax.ShapeDtypeStruct((N, PACK_W), jnp.int32),
            jax.ShapeDtypeStruct((N, PACK_W), jnp.int32),
            jax.ShapeDtypeStruct((N_EXPERTS, N), F32),
        ],
        compiler_params=_cp(("parallel",), VMEM_LIMIT),
        name="mix_out",
    )(x2, gu, gu, gu, o1, o2, o3, l1, l2, l3, yc, mod1, mod2, *weights)


def _pick_rows(table, picks):
    G, GS = N_GROUPS, GROUP_SIZE
    eio = lax.broadcasted_iota(jnp.int32, (GS, table.shape[1]), 0)
    rows = []
    for k in range(TOP_K):
        idx = picks[k:k + 1]
        parts = [jnp.where(eio + g * GS == idx, table[g * GS:(g + 1) * GS], 0.0) for g in range(G)]
        rows.append(jnp.sum(functools.reduce(jnp.add, parts), axis=0, keepdims=True))
    return jnp.concatenate(rows, axis=0)


def _route_choose(lg_ref, bias_ref):
    G, GS = N_GROUPS, GROUP_SIZE
    scores = jax.nn.sigmoid(lg_ref[...])
    sel = scores + bias_ref[...]
    tn = sel.shape[1]
    eio = lax.broadcasted_iota(jnp.int32, (GS, tn), 0)
    ninf = -jnp.inf

    gs = []
    for g in range(G):
        v = sel[g * GS:(g + 1) * GS]
        m1 = jnp.max(v, axis=0, keepdims=True)
        i1 = jnp.min(jnp.where(v == m1, eio, GS), axis=0, keepdims=True)
        m2 = jnp.max(jnp.where(eio == i1, ninf, v), axis=0, keepdims=True)
        gs.append(m1 + m2)
    gsm = jnp.concatenate(gs, axis=0)
    gio = lax.broadcasted_iota(jnp.int32, (G, tn), 0)
    rank = jnp.zeros((G, tn), jnp.int32)
    for g2 in range(G):
        beats = (gs[g2] > gsm) | ((gs[g2] == gsm) & (g2 < gio))
        rank = rank + beats.astype(jnp.int32)
    gsel = rank < TOPK_GROUPS

    vs = [jnp.where(gsel[g:g + 1], sel[g * GS:(g + 1) * GS], NEG) for g in range(G)]
    eid = [eio + g * GS for g in range(G)]
    chosen = [jnp.zeros((GS, tn), jnp.bool_) for _ in range(G)]
    picks = []
    for _ in range(TOP_K):
        m = jnp.max(functools.reduce(jnp.maximum, vs), axis=0, keepdims=True)
        idx = jnp.min(functools.reduce(jnp.minimum, [jnp.where(v == m, e, N_EXPERTS) for v, e in zip(vs, eid)]),
                      axis=0, keepdims=True)
        picks.append(idx)
        for g in range(G):
            hit = eid[g] == idx
            chosen[g] = chosen[g] | hit
            vs[g] = jnp.where(hit, ninf, vs[g])
    mask = jnp.concatenate(chosen, axis=0).astype(F32)
    return scores, jnp.concatenate(picks, axis=0), mask


def _route_kernel(lg_ref, bias_ref, tri_ref, dest_ref, w_ref, cnt_ref, run_sc, start_sc, mask_sc, picks_sc,
                  *, slot_block):
    phase = pl.program_id(0)
    step = pl.program_id(1)
    tn = lg_ref.shape[1]
    cols = pl.ds(pl.multiple_of(step * tn, tn), tn)

    @pl.when(phase == 0)
    def _():
        @pl.when(step == 0)
        def _():
            run_sc[...] = jnp.zeros(run_sc.shape, F32)

        scores, picks, mask = _route_choose(lg_ref, bias_ref)
        wk = _pick_rows(scores, picks)
        w_ref[0] = wk / jnp.sum(wk, axis=0, keepdims=True) * ROUTED_SCALE
        dest_ref[0] = jnp.zeros(dest_ref.shape[1:], dest_ref.dtype)
        mask_sc[:, cols] = mask.astype(BF16)
        picks_sc[:, cols] = picks
        run_sc[...] = run_sc[...] + jnp.sum(mask, axis=1, keepdims=True)

    @pl.when(phase == 1)
    def _():
        @pl.when(step == 0)
        def _():
            counts = run_sc[...].astype(jnp.int32)
            cnt_ref[...] = jnp.broadcast_to(counts, cnt_ref.shape)
            shift = slot_block.bit_length() - 1
            padded = lax.shift_left(lax.shift_right_logical(counts + (slot_block - 1), shift), shift).astype(F32)
            r = lax.broadcasted_iota(jnp.int32, (N_EXPERTS, N_EXPERTS), 0)
            c = lax.broadcasted_iota(jnp.int32, (N_EXPERTS, N_EXPERTS), 1)
            as_row = jnp.sum(jnp.where(r == c, padded, 0.0), axis=0, keepdims=True)
            start_sc[...] = jnp.sum(jnp.where(c < r, as_row, 0.0), axis=1, keepdims=True)
            run_sc[...] = jnp.zeros(run_sc.shape, F32)

        mask_b = mask_sc[:, cols]
        mask = mask_b.astype(F32)
        before = jnp.dot(mask_b, tri_ref[...], preferred_element_type=F32) - mask
        slot = start_sc[...] + run_sc[...] + before
        dest_ref[0] = _pick_rows(slot, picks_sc[:, cols]).astype(jnp.int32)
        w_ref[0] = jnp.zeros(w_ref.shape[1:], w_ref.dtype)
        run_sc[...] = run_sc[...] + jnp.sum(mask, axis=1, keepdims=True)


SLOT_BLOCK = 512


def route(logits_t, bias):
    E, N = logits_t.shape
    tn = 1024
    tri = (jnp.arange(tn)[:, None] <= jnp.arange(tn)[None, :]).astype(BF16)
    plane = lambda: pl.BlockSpec((1, TOP_K, tn), lambda p, i: (p, 0, i))
    dest, w, cnt = pl.pallas_call(
        functools.partial(_route_kernel, slot_block=SLOT_BLOCK),
        grid=(2, N // tn),
        in_specs=[
            pl.BlockSpec((E, tn), lambda p, i: (0, i * (1 - p))),
            pl.BlockSpec((E, 1), lambda p, i: (0, 0)),
            pl.BlockSpec((tn, tn), lambda p, i: (0, 0)),
        ],
        out_specs=[plane(), plane(), pl.BlockSpec((E, 128), lambda p, i: (0, 0))],
        out_shape=[
            jax.ShapeDtypeStruct((2, TOP_K, N), jnp.int32),
            jax.ShapeDtypeStruct((2, TOP_K, N), F32),
            jax.ShapeDtypeStruct((E, 128), jnp.int32),
        ],
        scratch_shapes=[pltpu.VMEM((E, 1), F32), pltpu.VMEM((E, 1), F32),
                        pltpu.VMEM((E, N), BF16), pltpu.VMEM((TOP_K, N), jnp.int32)],
        compiler_params=_cp(("arbitrary", "arbitrary")),
        name="route",
    )(logits_t, bias.reshape(E, 1), tri)
    return dest[1], w[0], cnt[:, 0]


def block_tables(counts, n_tokens):
    E = counts.shape[0]
    blk = SLOT_BLOCK
    nblk = (n_tokens * TOP_K + E * blk) // blk
    per_expert = (counts + blk - 1) // blk
    bend = jnp.cumsum(per_expert)
    bstart = bend - per_expert
    b = jnp.arange(nblk, dtype=jnp.int32)[:, None]
    owns = (bstart[None, :] <= b) & (b < bend[None, :])
    blk_e = jnp.minimum(jnp.sum(bend[None, :] <= b, axis=1), E - 1).astype(jnp.int32)
    rows_left = counts[None, :] - (b - bstart[None, :]) * blk
    nvalid = jnp.sum(jnp.where(owns, jnp.clip(rows_left, 0, blk), 0), axis=1)
    first = jnp.concatenate([jnp.ones((1,), jnp.bool_), blk_e[1:] != blk_e[:-1]])
    run_parity = ((jnp.cumsum(first.astype(jnp.int32)) - 1) % 2).astype(jnp.int32)
    later = blk_e[None, :] > blk_e[:, None]
    next_e = jnp.min(jnp.where(later, blk_e[None, :], E), axis=1).astype(jnp.int32)
    return blk_e, nvalid.astype(jnp.int32), run_parity, next_e


def _sc_mesh():
    return plsc.VectorSubcoreMesh(core_axis_name="c", subcore_axis_name="s")


SC_WINDOW = 128


def sc_scatter_rows(x, dest, n_slots):
    N, W = x.shape
    K = dest.shape[0]

    @functools.partial(pl.kernel, out_type=jax.ShapeDtypeStruct((n_slots, W), x.dtype), mesh=_sc_mesh(),
                       scratch_types=[])
    def scatter(x_hbm, i_hbm, o_hbm):
        def body(x_vmem, i_vmem):
            for k in range(K):
                pltpu.sync_copy(x_vmem, o_hbm.at[i_vmem.at[k]])

        pltpu.emit_pipeline(
            body,
            grid=(N // SC_WINDOW,),
            in_specs=[pl.BlockSpec((SC_WINDOW, W), lambda i: (i, 0)),
                      pl.BlockSpec((K, SC_WINDOW), lambda i: (0, i))],
            out_specs=[],
            core_axis_name=("c", "s"),
            dimension_semantics=(pltpu.PARALLEL,),
        )(x_hbm, i_hbm)

    return scatter(x, dest)


SC_LANES = 16
SC_GATHER_TOKENS = 8


def sc_weighted_gather(y, dest, wts):
    W = y.shape[1]
    K, N = dest.shape
    G, L = SC_GATHER_TOKENS, SC_LANES
    batches = SC_WINDOW // G

    @functools.partial(
        pl.kernel, out_type=jax.ShapeDtypeStruct((N, W), y.dtype), mesh=_sc_mesh(),
        scratch_types=[pltpu.VMEM((2, K, G, W), y.dtype), pltpu.SemaphoreType.DMA((2,))],
        compiler_params=pltpu.CompilerParams(needs_layout_passes=False))
    def gather(y_hbm, i_hbm, w_hbm, o_hbm, rows2, sems):
        def body(i_vmem, w_vmem, o_vmem):
            def fetch(batch, slot):
                return [pltpu.make_async_copy(y_hbm.at[i_vmem.at[k, pl.ds(batch * G, G)]], rows2.at[slot, k],
                                              sems.at[slot]) for k in range(K)]

            for c in fetch(0, 0):
                c.start()

            @pl.loop(0, batches)
            def _(batch):
                slot = batch % 2

                @pl.when(batch + 1 < batches)
                def _():
                    for c in fetch(batch + 1, 1 - slot):
                        c.start()

                for c in fetch(batch, slot):
                    c.wait()
                rows = rows2.at[slot]

                @pl.loop(0, G)
                def _(t):
                    tok = jnp.full((L,), batch * G + t, jnp.int32)
                    wk = [plsc.load_gather(w_vmem, [jnp.full((L,), k, jnp.int32), tok]) for k in range(K)]

                    @plsc.parallel_loop(0, W // L, unroll=W // L)
                    def _(j):
                        lo = jnp.zeros((L,), F32)
                        hi = jnp.zeros((L,), F32)
                        for k in range(K):
                            pair = plsc.bitcast(rows[k, t, pl.ds(j * L, L)], BF16)
                            a, b = plsc.unpack(pair, format=plsc.PackFormat.INTERLEAVED)
                            lo = lo + wk[k] * a
                            hi = hi + wk[k] * b
                        o_vmem[batch * G + t, pl.ds(j * L, L)] = plsc.bitcast(
                            plsc.pack(lo, hi, format=plsc.PackFormat.INTERLEAVED), y.dtype)

        pltpu.emit_pipeline(
            body,
            grid=(N // SC_WINDOW,),
            in_specs=[pl.BlockSpec((K, SC_WINDOW), lambda i: (0, i)),
                      pl.BlockSpec((K, SC_WINDOW), lambda i: (0, i))],
            out_specs=[pl.BlockSpec((SC_WINDOW, W), lambda i: (i, 0))],
            core_axis_name=("c", "s"),
            dimension_semantics=(pltpu.PARALLEL,),
        )(i_hbm, w_hbm, o_hbm)

    return gather(y, dest, wts)


EXPERT_INPUT_SLOTS = 3


def _expert_kernel(blk_e_ref, nvalid_ref, parity_ref, next_e_ref, xa_hbm, xb_hbm, w1_hbm, w3_hbm, w2_hbm,
                   ya_hbm, yb_hbm, w1_sc, w3_sc, w2_sc, xa_buf, xb_buf, sems, w1_st, w3_st, w2_st, wsems,
                   ya_st, yb_st, osems, *, layer):
    b = pl.program_id(0)
    nb = pl.num_programs(0)
    nv = nvalid_ref[b]
    prev_e = blk_e_ref[jnp.maximum(b - 1, 0)]
    blk = xa_buf.shape[1]
    ring = EXPERT_INPUT_SLOTS
    n_experts = w1_hbm.shape[1]

    def fetch_weights(e, par):
        return (pltpu.make_async_copy(w1_hbm.at[layer, e], w1_st.at[par], wsems.at[par, 0]),
                pltpu.make_async_copy(w3_hbm.at[layer, e], w3_st.at[par], wsems.at[par, 1]),
                pltpu.make_async_copy(w2_hbm.at[layer, e], w2_st.at[par], wsems.at[par, 2]))

    def fetch(block, slot):
        rows = pl.ds(pl.multiple_of(block * blk, blk), blk)
        return (pltpu.make_async_copy(xa_hbm.at[rows], xa_buf.at[slot], sems.at[slot, 0]),
                pltpu.make_async_copy(xb_hbm.at[rows], xb_buf.at[slot], sems.at[slot, 1]))

    @pl.when(b == 0)
    def _():
        for i in range(ring - 1):
            for c in fetch(i, i):
                c.start()

    ahead = b + (ring - 1)

    @pl.when(ahead < nb)
    def _():
        for c in fetch(ahead, ahead % ring):
            c.start()

    slot = b % ring
    for c in fetch(b, slot):
        c.wait()

    @pl.when(b == 0)
    def _():
        for c in fetch_weights(blk_e_ref[0], 0):
            c.start()

    @pl.when((b == 0) | (blk_e_ref[b] != prev_e))
    def _():
        par = parity_ref[b]
        nxt = next_e_ref[b]

        @pl.when(nxt < n_experts)
        def _():
            for c in fetch_weights(nxt, 1 - par):
                c.start()

        for c in fetch_weights(blk_e_ref[b], par):
            c.wait()
        w1_sc[...] = w1_st[par].astype(BF16)
        w3_sc[...] = w3_st[par].astype(BF16)
        w2_sc[...] = w2_st[par].astype(BF16)

    def put(block, oslot):
        rows = pl.ds(pl.multiple_of(block * blk, blk), blk)
        return (pltpu.make_async_copy(ya_st.at[oslot], ya_hbm.at[rows], osems.at[oslot, 0]),
                pltpu.make_async_copy(yb_st.at[oslot], yb_hbm.at[rows], osems.at[oslot, 1]))

    oslot = b % 2

    @pl.when(b >= 2)
    def _():
        for c in put(b - 2, oslot):
            c.wait()

    @pl.when(nv > 0)
    def _():
        x = _unpack_row_halves(xa_buf[slot], xb_buf[slot])
        rows = lax.broadcasted_iota(jnp.int32, x.shape, 0)
        x = jnp.where(rows < nv, x, 0.0).astype(BF16)
        hid = _silu(jnp.dot(x, w1_sc[...], preferred_element_type=F32)) * jnp.dot(
            x, w3_sc[...], preferred_element_type=F32)
        y = jnp.dot(hid.astype(BF16), w2_sc[...], preferred_element_type=F32)
        ya_st[oslot], yb_st[oslot] = _pack_row_halves(y)

    @pl.when(nv == 0)
    def _():
        ya_st[oslot] = jnp.zeros(ya_st.shape[1:], ya_st.dtype)
        yb_st[oslot] = jnp.zeros(yb_st.shape[1:], yb_st.dtype)

    for c in put(b, oslot):
        c.start()

    @pl.when(b == nb - 1)
    def _():
        @pl.when(nb >= 2)
        def _():
            for c in put(b - 1, 1 - oslot):
                c.wait()

        for c in put(b, oslot):
            c.wait()


def routed_experts(xa, xb, tables, w1, w3, w2, layer):
    P = xa.shape[0]
    blk = SLOT_BLOCK
    _, E, D, FF = w1.shape
    grid_spec = pltpu.PrefetchScalarGridSpec(
        num_scalar_prefetch=len(tables),
        grid=(P // blk,),
        in_specs=[pl.BlockSpec(memory_space=pl.ANY)] * 5,
        out_specs=[pl.BlockSpec(memory_space=pl.ANY)] * 2,
        scratch_shapes=[
            pltpu.VMEM((D, FF), BF16), pltpu.VMEM((D, FF), BF16), pltpu.VMEM((FF, D), BF16),
            pltpu.VMEM((EXPERT_INPUT_SLOTS, blk, PACK_W), jnp.int32),
            pltpu.VMEM((EXPERT_INPUT_SLOTS, blk, PACK_W), jnp.int32),
            pltpu.SemaphoreType.DMA((EXPERT_INPUT_SLOTS, 2)),
            pltpu.VMEM((2, D, FF), F32), pltpu.VMEM((2, D, FF), F32), pltpu.VMEM((2, FF, D), F32),
            pltpu.SemaphoreType.DMA((2, 3)),
            pltpu.VMEM((2, blk, PACK_W), jnp.int32), pltpu.VMEM((2, blk, PACK_W), jnp.int32),
            pltpu.SemaphoreType.DMA((2, 2)),
        ],
    )
    return pl.pallas_call(
        functools.partial(_expert_kernel, layer=layer),
        grid_spec=grid_spec,
        out_shape=[jax.ShapeDtypeStruct((P, PACK_W), jnp.int32)] * 2,
        compiler_params=_cp(("arbitrary",), VMEM_LIMIT),
        name="routed_experts",
    )(*tables, xa, xb, w1, w3, w2)


def _combine_kernel(xmid_ref, ra_ref, rb_ref, mod2_ref, fg_ref, *rest):
    out_ref = rest[-1]
    D = xmid_ref.shape[1]
    x = xmid_ref[...] + mod2_ref[0][:, 2 * D:] * _unpack_row_halves(ra_ref[...], rb_ref[...])
    out_ref[...] = x * lax.rsqrt(jnp.mean(x * x, axis=-1, keepdims=True) + EPS) * fg_ref[...]


def combine(xmid, ra, rb, mod2, final_g, seq, out_rows=None, row0=0, out_buf=None, in_row0=0):
    N, D = ra.shape[0], xmid.shape[1]
    tm = 512
    tpb = seq // tm
    tile0 = row0 // tm
    in_tile0 = in_row0 // tm
    in_specs = [
        pl.BlockSpec((tm, D), lambda i: (i + in_tile0, 0)),
        pl.BlockSpec((tm, PACK_W), lambda i: (i, 0)),
        pl.BlockSpec((tm, PACK_W), lambda i: (i, 0)),
        pl.BlockSpec((1, 1, 3 * D), lambda i: ((i + in_tile0) // tpb, 0, 0)),
        pl.BlockSpec((1, D), lambda i: (0, 0)),
    ]
    args = [xmid, ra, rb, mod2, final_g.reshape(1, D)]
    aliases = {}
    if out_buf is not None:
        in_specs.append(pl.BlockSpec(memory_space=pl.ANY))
        args.append(out_buf)
        aliases = {len(args) - 1: 0}
    return pl.pallas_call(
        _combine_kernel,
        grid=(N // tm,),
        in_specs=in_specs,
        out_specs=pl.BlockSpec((tm, D), lambda i: (i + tile0, 0)),
        out_shape=jax.ShapeDtypeStruct((out_rows or N, D), F32),
        input_output_aliases=aliases,
        compiler_params=_cp(("parallel",), VMEM_LIMIT),
        name="combine",
    )(*args)


TOKEN_STREAMS = 2
LAST_GATHER_SPLITS = 2


def _permute_w_in(w):
    ub = w[:, 3 * DA:3 * DA + DB]
    lat_lo = 3 * DA + DB
    lat_hi = lat_lo + Q_LORA + KV_LORA + QK_ROPE
    lat, gates = w[:, lat_lo:lat_hi], w[:, lat_hi:]
    pad = jnp.zeros((w.shape[0], LAT_W - (lat_hi - lat_lo)), w.dtype)
    parts = [gates, ub, lat, pad]
    for g in range(len(DIL_GROUPS)):
        sl = slice(g * GROUP_W, (g + 1) * GROUP_W)
        parts += [w[:, :DA][:, sl] * (HEAD_DIM_A ** -0.5), w[:, DA:2 * DA][:, sl], w[:, 2 * DA:3 * DA][:, sl]]
    return jnp.concatenate(parts, axis=1).astype(BF16)


def kernel(x, c, positions, ada_mix_w, ada_mix_b, norm_mix_g, w_in, pool_w, pool_scale, cq_norm_g, ckv_norm_g, w_uq, w_ukv, w_oa, w_ob, w_oc, w_out, ada_ffn_w, ada_ffn_b, norm_ffn_g, router_w, router_bias, exp_w1, exp_w3, exp_w2, sh_w1, sh_w3, sh_w2, final_g):
    B, S, D = x.shape
    depth = w_in.shape[0]
    mod_mix = adaln_rows(c, ada_mix_w, ada_mix_b)
    mod_ffn = adaln_rows(c, ada_ffn_w, ada_ffn_b)
    streams = TOKEN_STREAMS if B % TOKEN_STREAMS == 0 else 1
    Bs = B // streams
    Ns = Bs * S
    x_all = x.reshape(B * S, D)
    xs = [None] * streams
    out_all = None
    pos_s = [positions[s * Bs:(s + 1) * Bs] for s in range(streams)]
    for l in range(depth):
        last = l == depth - 1
        w_in_l = _permute_w_in(w_in[l])
        mla_w = _mla_weights(cq_norm_g[l], ckv_norm_g[l], w_uq[l], w_ukv[l])
        mix_w = (norm_ffn_g[l], pool_w[l].astype(BF16), pool_scale[l],
                 w_oa[l].astype(BF16), w_ob[l].astype(BF16), w_oc[l].astype(BF16), w_out[l].astype(BF16),
                 router_w[l].T.astype(BF16), sh_w1[l].astype(BF16), sh_w3[l].astype(BF16), sh_w2[l].astype(BF16))
        for s in range(streams):
            mod1 = mod_mix[l, s * Bs:(s + 1) * Bs].reshape(Bs, 1, 3 * D)
            mod2 = mod_ffn[l, s * Bs:(s + 1) * Bs].reshape(Bs, 1, 3 * D)
            if l == 0:
                x2, row0 = x_all, s * Ns
                gu, lat, *qkv = in_projection(x2, norm_mix_g[l], mod1, w_in_l, S, row0)
            else:
                row0 = 0
                x2, gu, lat, *qkv = in_projection(xs[s][0], norm_mix_g[l], mod1, w_in_l, S, 0, xs[s][1:])
            dil = [dilated_attention(qkv[2 * g], qkv[2 * g + 1]) for g in range(len(DIL_GROUPS))]
            q_all, k_all, vt_all = mla_prep(lat, pos_s[s], *mla_w, Bs, S)
            yc = mla_attention(q_all, k_all, vt_all, Bs, S)
            xmid, h2a, h2b, logits_t = mix_out(x2, gu, dil, yc, mod1, mod2, *mix_w, S, row0)
            dest, w_k, counts = route(logits_t, router_bias[l])
            tables = block_tables(counts, Ns)
            n_slots = tables[0].shape[0] * SLOT_BLOCK
            xa = sc_scatter_rows(h2a, dest, n_slots)
            xb = sc_scatter_rows(h2b, dest, n_slots)
            ya, yb = routed_experts(xa, xb, tables, exp_w1, exp_w3, exp_w2, l)
            if last:
                for t0 in range(0, Ns, Ns // LAST_GATHER_SPLITS):
                    cols = slice(t0, t0 + Ns // LAST_GATHER_SPLITS)
                    ra = sc_weighted_gather(ya, dest[:, cols], w_k[:, cols])
                    rb = sc_weighted_gather(yb, dest[:, cols], w_k[:, cols])
                    out_all = combine(xmid, ra, rb, mod2, final_g, S, B * S, s * Ns + t0, out_all, t0)
            else:
                xs[s] = (xmid, sc_weighted_gather(ya, dest, w_k), sc_weighted_gather(yb, dest, w_k), mod2)
    return out_all.reshape(B, S, D)
```
